```python
import jax, jax.numpy as jnp
from jax import lax
import numpy as np

D_MODEL = 2048
BATCH = 8
SEQ = 8192
DEPTH = 1

N_META = 16
BLOCK = 128
PAD_LEN = BLOCK - N_META
MLA_HEADS = 16
Q_LORA_RANK = 1536
KV_LORA_RANK = 512
QK_NOPE_DIM = 128
QK_ROPE_DIM = 64
QK_HEAD_DIM = QK_NOPE_DIM + QK_ROPE_DIM
V_HEAD_DIM = 128
MLA_WIDTH = MLA_HEADS * V_HEAD_DIM
ROPE_THETA = 10000.0
HGRN_HEADS = 16
HGRN_EXPAND = 128
HGRN_KEY_WIDTH = HGRN_HEADS * HGRN_EXPAND
HGRN_V_DIM = D_MODEL // HGRN_HEADS
HGRN_V_WIDTH = HGRN_HEADS * HGRN_V_DIM
D_FF = 5632
CONV_WIDTH = 3
NORM_EPS = 1e-6
IN_WIDTHS = (Q_LORA_RANK, KV_LORA_RANK, QK_ROPE_DIM,
             HGRN_KEY_WIDTH, HGRN_KEY_WIDTH, HGRN_V_WIDTH,
             HGRN_V_WIDTH,
             D_MODEL, D_MODEL)
IN_COLS = sum(IN_WIDTHS)

kernel_name = "hybrid_mla_hgrn2_convffn_block"


def _rms_norm(x, g):
    xf = x.astype(jnp.float32)
    y = xf * lax.rsqrt(jnp.mean(xf * xf, axis=-1, keepdims=True) + NORM_EPS)
    return (y * g.astype(jnp.float32)).astype(x.dtype)


def _rope_tables(pos):
    inv = 1.0 / (ROPE_THETA ** (jnp.arange(0, QK_ROPE_DIM, 2, dtype=jnp.float32) / QK_ROPE_DIM))
    ang = pos.astype(jnp.float32)[..., None] * inv
    ang = jnp.concatenate([ang, ang], axis=-1)
    return jnp.cos(ang), jnp.sin(ang)


def _apply_rope(x, cos, sin):
    xf = x.astype(jnp.float32)
    x1, x2 = jnp.split(xf, 2, axis=-1)
    rot = jnp.concatenate([-x2, x1], axis=-1)
    return (xf * cos + rot * sin).astype(x.dtype)


def _mla_attention(q_nope, q_rope, k_nope, k_rope, v, valid):
    B, L, H, _ = q_nope.shape
    n_blocks = L // BLOCK
    scale = QK_HEAD_DIM ** -0.5
    key_idx = jnp.arange(L)

    def one_block(i):
        start = i * BLOCK
        qn = lax.dynamic_slice_in_dim(q_nope, start, BLOCK, axis=1)
        qr = lax.dynamic_slice_in_dim(q_rope, start, BLOCK, axis=1)
        s = (jnp.einsum('bqhd,bkhd->bhqk', qn, k_nope, preferred_element_type=jnp.float32)
             + jnp.einsum('bqhr,bkr->bhqk', qr, k_rope, preferred_element_type=jnp.float32)) * scale
        q_idx = start + jnp.arange(BLOCK)
        causal = key_idx[None, :] <= q_idx[:, None]
        visible = valid[None, :] | (key_idx[None, :] == q_idx[:, None])
        s = jnp.where(causal & visible, s, -jnp.inf)
        p = jax.nn.softmax(s, axis=-1)
        return jnp.einsum('bhqk,bkhd->bqhd', p.astype(v.dtype), v)

    out = lax.map(one_block, jnp.arange(n_blocks))
    return jnp.transpose(out, (1, 0, 2, 3, 4)).reshape(B, L, H * v.shape[-1])


def _hgrn2_chunked(q, k, v, log_f):
    B, L, H, DK = q.shape
    DV = v.shape[-1]
    n_chunks = L // BLOCK

    def to_chunks(t):
        return jnp.transpose(t.reshape(B, n_chunks, BLOCK, H, t.shape[-1]), (1, 0, 3, 2, 4))

    causal = jnp.tril(jnp.ones((BLOCK, BLOCK), dtype=bool))

    def step(S, inp):
        qc, kc, vc, gc = inp
        b = jnp.cumsum(gc, axis=2)
        diff = b[:, :, :, None, :] - b[:, :, None, :, :]
        decay = jnp.exp(jnp.where(causal[:, :, None], diff, -jnp.inf))
        scores = jnp.einsum('bhtd,bhtsd,bhsd->bhts', qc, decay, kc)
        o = (jnp.einsum('bhts,bhse->bhte', scores, vc)
             + jnp.einsum('bhtd,bhde->bhte', qc * jnp.exp(b), S))
        b_last = b[:, :, -1:, :]
        S = (jnp.exp(b_last[:, :, 0, :])[..., None] * S
             + jnp.einsum('bhsd,bhse->bhde', kc * jnp.exp(b_last - b), vc))
        return S, o

    S0 = jnp.zeros((B, H, DK, DV), jnp.float32)
    _, o = lax.scan(step, S0, (to_chunks(q), to_chunks(k), to_chunks(v), to_chunks(log_f)))
    return jnp.transpose(o, (1, 0, 3, 2, 4)).reshape(B, L, H, DV)


def _fwd_setup_inputs(seed: int = 0) -> dict:
    key = jax.random.key(seed)
    ks = jax.random.split(key, 24)
    f32 = jnp.float32

    def w(k, shape, fan_in):
        return jax.random.normal(k, shape, f32) * (fan_in ** -0.5)

    def gain(k, shape):
        return 1.0 + 0.02 * jax.random.normal(k, shape, f32)

    return {
        "x": jax.random.normal(ks[0], (BATCH, SEQ, D_MODEL), f32),
        "positions": jnp.broadcast_to(jnp.arange(SEQ, dtype=jnp.int32)[None, :], (BATCH, SEQ)),
        "meta_tokens": jax.random.normal(ks[1], (N_META, D_MODEL), f32),
        "w_in": w(ks[2], (DEPTH, D_MODEL, IN_COLS), D_MODEL),
        "w_q_up": w(ks[3], (DEPTH, Q_LORA_RANK, MLA_HEADS * QK_HEAD_DIM), Q_LORA_RANK),
        "w_kv_up": w(ks[4], (DEPTH, KV_LORA_RANK, MLA_HEADS * (QK_NOPE_DIM + V_HEAD_DIM)), KV_LORA_RANK),
        "w_branch_mla": w(ks[5], (DEPTH, MLA_WIDTH, D_MODEL), MLA_WIDTH),
        "w_branch_hgrn": w(ks[6], (DEPTH, HGRN_V_WIDTH, D_MODEL), HGRN_V_WIDTH),
        "w_out": w(ks[7], (DEPTH, D_MODEL, D_MODEL), D_MODEL),
        "w_ffn_in": w(ks[8], (DEPTH, D_MODEL, 2 * D_FF), D_MODEL),
        "w_ffn_out": w(ks[9], (DEPTH, D_FF, D_MODEL), D_FF),
        "conv_w": w(ks[10], (DEPTH, CONV_WIDTH, D_FF), CONV_WIDTH),
        "conv_b": 0.01 * jax.random.normal(ks[11], (DEPTH, D_FF), f32),
        "g_mix_norm": gain(ks[12], (DEPTH, D_MODEL)),
        "g_q_norm": gain(ks[13], (DEPTH, Q_LORA_RANK)),
        "g_kv_norm": gain(ks[14], (DEPTH, KV_LORA_RANK)),
        "g_hgrn_norm": gain(ks[15], (DEPTH, HGRN_V_DIM)),
        "g_ffn_norm": gain(ks[16], (DEPTH, D_MODEL)),
        "g_final_norm": gain(ks[17], (D_MODEL,)),
        "lb_raw": 1.0 + 0.1 * jax.random.normal(ks[18], (DEPTH + 1, HGRN_KEY_WIDTH), f32),
    }


def _fwd_reference(x, positions, meta_tokens, w_in, w_q_up, w_kv_up, w_branch_mla, w_branch_hgrn,
              w_out, w_ffn_in, w_ffn_out, conv_w, conv_b, g_mix_norm, g_q_norm, g_kv_norm,
              g_hgrn_norm, g_ffn_norm, g_final_norm, lb_raw):
    B, S, D = x.shape
    dt = x.dtype
    prefix = PAD_LEN + N_META
    L = prefix + S

    h = jnp.concatenate([jnp.zeros((B, PAD_LEN, D), dt),
                         jnp.broadcast_to(meta_tokens.astype(dt)[None], (B, N_META, D)),
                         x], axis=1)
    valid = jnp.arange(L) >= PAD_LEN
    pos = jnp.concatenate([jnp.zeros((B, PAD_LEN), jnp.int32),
                           jnp.broadcast_to(jnp.arange(N_META, dtype=jnp.int32)[None], (B, N_META)),
                           positions.astype(jnp.int32) + N_META], axis=1)
    cos, sin = _rope_tables(pos)
    split_pts = np.cumsum(IN_WIDTHS)[:-1].tolist()
    lb_all = jnp.cumsum(jax.nn.softmax(lb_raw.astype(jnp.float32), axis=0), axis=0)

    for layer in range(DEPTH):
        u = _rms_norm(h, g_mix_norm[layer])
        proj = u @ w_in[layer]
        q_lat, kv_lat, k_rope, hq, hf, hi, hg, gate_a, gate_b = jnp.split(proj, split_pts, axis=-1)

        q = (_rms_norm(q_lat, g_q_norm[layer]) @ w_q_up[layer]).reshape(B, L, MLA_HEADS, QK_HEAD_DIM)
        q_nope, q_rope = q[..., :QK_NOPE_DIM], q[..., QK_NOPE_DIM:]
        kv = (_rms_norm(kv_lat, g_kv_norm[layer]) @ w_kv_up[layer]).reshape(
            B, L, MLA_HEADS, QK_NOPE_DIM + V_HEAD_DIM)
        k_nope, v_mla = kv[..., :QK_NOPE_DIM], kv[..., QK_NOPE_DIM:]
        q_rope = _apply_rope(q_rope, cos[:, :, None, :], sin[:, :, None, :])
        k_rope = _apply_rope(k_rope, cos, sin)
        o_mla = _mla_attention(q_nope, q_rope, k_nope, k_rope, v_mla, valid)

        lb = lb_all[layer]
        f = lb + (1.0 - lb) * jax.nn.sigmoid(hf.astype(jnp.float32))
        vmask = valid[None, :, None]
        log_f = jnp.where(vmask, jnp.log(f), 0.0)
        k_in = jnp.where(vmask, 1.0 - f, 0.0)
        q_h = jax.nn.silu(hq.astype(jnp.float32))
        o_h = _hgrn2_chunked(q_h.reshape(B, L, HGRN_HEADS, HGRN_EXPAND),
                             k_in.reshape(B, L, HGRN_HEADS, HGRN_EXPAND),
                             hi.astype(jnp.float32).reshape(B, L, HGRN_HEADS, HGRN_V_DIM),
                             log_f.reshape(B, L, HGRN_HEADS, HGRN_EXPAND))
        o_h = _rms_norm(o_h, g_hgrn_norm[layer]) * jax.nn.silu(
            hg.astype(jnp.float32).reshape(B, L, HGRN_HEADS, HGRN_V_DIM))
        o_hgrn = o_h.reshape(B, L, HGRN_V_WIDTH).astype(dt)

        merged = (jax.nn.sigmoid(gate_a) * (o_mla @ w_branch_mla[layer])
                  + jax.nn.sigmoid(gate_b) * (o_hgrn @ w_branch_hgrn[layer]))
        h = h + merged @ w_out[layer]

        u = _rms_norm(h, g_ffn_norm[layer])
        gate, up = jnp.split(u @ w_ffn_in[layer], 2, axis=-1)
        gate = jnp.where(vmask, gate, 0.0)
        gp = jnp.pad(gate, ((0, 0), (CONV_WIDTH - 1, 0), (0, 0)))
        cw = conv_w[layer]
        conv = (cw[0] * gp[:, :-2] + cw[1] * gp[:, 1:-1] + cw[2] * gp[:, 2:]) + conv_b[layer]
        h = h + (jax.nn.silu(conv) * up) @ w_ffn_out[layer]

    out = _rms_norm(h, g_final_norm)
    return out[:, prefix:, :]


import jax as _jax
import jax.numpy as _jnp

TWIN_FORMAT = 'train_step'
FWD_PARAMS = ['x', 'positions', 'meta_tokens', 'w_in', 'w_q_up', 'w_kv_up', 'w_branch_mla', 'w_branch_hgrn', 'w_out', 'w_ffn_in', 'w_ffn_out', 'conv_w', 'conv_b', 'g_mix_norm', 'g_q_norm', 'g_kv_norm', 'g_hgrn_norm', 'g_ffn_norm', 'g_final_norm', 'lb_raw']
TWIN_WEIGHTS = ['meta_tokens', 'w_in', 'w_q_up', 'w_kv_up', 'w_branch_mla', 'w_branch_hgrn', 'w_out', 'w_ffn_in', 'w_ffn_out', 'conv_w', 'conv_b', 'g_mix_norm', 'g_q_norm', 'g_kv_norm', 'g_hgrn_norm', 'g_ffn_norm', 'g_final_norm', 'lb_raw']
TWIN_DIFF_INPUT = 'x'
TWIN_INPUTS = ['x', 'positions', 'meta_tokens', 'w_in', 'w_q_up', 'w_kv_up', 'w_branch_mla', 'w_branch_hgrn', 'w_out', 'w_ffn_in', 'w_ffn_out', 'conv_w', 'conv_b', 'g_mix_norm', 'g_q_norm', 'g_kv_norm', 'g_hgrn_norm', 'g_ffn_norm', 'g_final_norm', 'lb_raw', 'loss_target', 'm_meta_tokens', 'm_w_in', 'm_w_q_up', 'm_w_kv_up', 'm_w_branch_mla', 'm_w_branch_hgrn', 'm_w_out', 'm_w_ffn_in', 'm_w_ffn_out', 'm_conv_w', 'm_conv_b', 'm_g_mix_norm', 'm_g_q_norm', 'm_g_kv_norm', 'm_g_hgrn_norm', 'm_g_ffn_norm', 'm_g_final_norm', 'm_lb_raw', 'v_meta_tokens', 'v_w_in', 'v_w_q_up', 'v_w_kv_up', 'v_w_branch_mla', 'v_w_branch_hgrn', 'v_w_out', 'v_w_ffn_in', 'v_w_ffn_out', 'v_conv_w', 'v_conv_b', 'v_g_mix_norm', 'v_g_q_norm', 'v_g_kv_norm', 'v_g_hgrn_norm', 'v_g_ffn_norm', 'v_g_final_norm', 'v_lb_raw']
TWIN_OUTPUTS = ['loss', 'grad_x', 'grad_meta_tokens', 'grad_w_in', 'grad_w_q_up', 'grad_w_kv_up', 'grad_w_branch_mla', 'grad_w_branch_hgrn', 'grad_w_out', 'grad_w_ffn_in', 'grad_w_ffn_out', 'grad_conv_w', 'grad_conv_b', 'grad_g_mix_norm', 'grad_g_q_norm', 'grad_g_kv_norm', 'grad_g_hgrn_norm', 'grad_g_ffn_norm', 'grad_g_final_norm', 'grad_lb_raw', 'delta_meta_tokens', 'delta_w_in', 'delta_w_q_up', 'delta_w_kv_up', 'delta_w_branch_mla', 'delta_w_branch_hgrn', 'delta_w_out', 'delta_w_ffn_in', 'delta_w_ffn_out', 'delta_conv_w', 'delta_conv_b', 'delta_g_mix_norm', 'delta_g_q_norm', 'delta_g_kv_norm', 'delta_g_hgrn_norm', 'delta_g_ffn_norm', 'delta_g_final_norm', 'delta_lb_raw', 'new_m_meta_tokens', 'new_m_w_in', 'new_m_w_q_up', 'new_m_w_kv_up', 'new_m_w_branch_mla', 'new_m_w_branch_hgrn', 'new_m_w_out', 'new_m_w_ffn_in', 'new_m_w_ffn_out', 'new_m_conv_w', 'new_m_conv_b', 'new_m_g_mix_norm', 'new_m_g_q_norm', 'new_m_g_kv_norm', 'new_m_g_hgrn_norm', 'new_m_g_ffn_norm', 'new_m_g_final_norm', 'new_m_lb_raw', 'new_v_meta_tokens', 'new_v_w_in', 'new_v_w_q_up', 'new_v_w_kv_up', 'new_v_w_branch_mla', 'new_v_w_branch_hgrn', 'new_v_w_out', 'new_v_w_ffn_in', 'new_v_w_ffn_out', 'new_v_conv_w', 'new_v_conv_b', 'new_v_g_mix_norm', 'new_v_g_q_norm', 'new_v_g_kv_norm', 'new_v_g_hgrn_norm', 'new_v_g_ffn_norm', 'new_v_g_final_norm', 'new_v_lb_raw']
TWIN_LEAF_KINDS = {'loss': 'loss', 'grad_x': 'grad_x', 'grad_meta_tokens': 'grad_w', 'grad_w_in': 'grad_w', 'grad_w_q_up': 'grad_w', 'grad_w_kv_up': 'grad_w', 'grad_w_branch_mla': 'grad_w', 'grad_w_branch_hgrn': 'grad_w', 'grad_w_out': 'grad_w', 'grad_w_ffn_in': 'grad_w', 'grad_w_ffn_out': 'grad_w', 'grad_conv_w': 'grad_w', 'grad_conv_b': 'grad_w', 'grad_g_mix_norm': 'grad_w', 'grad_g_q_norm': 'grad_w', 'grad_g_kv_norm': 'grad_w', 'grad_g_hgrn_norm': 'grad_w', 'grad_g_ffn_norm': 'grad_w', 'grad_g_final_norm': 'grad_w', 'grad_lb_raw': 'grad_w', 'delta_meta_tokens': 'delta_w', 'delta_w_in': 'delta_w', 'delta_w_q_up': 'delta_w', 'delta_w_kv_up': 'delta_w', 'delta_w_branch_mla': 'delta_w', 'delta_w_branch_hgrn': 'delta_w', 'delta_w_out': 'delta_w', 'delta_w_ffn_in': 'delta_w', 'delta_w_ffn_out': 'delta_w', 'delta_conv_w': 'delta_w', 'delta_conv_b': 'delta_w', 'delta_g_mix_norm': 'delta_w', 'delta_g_q_norm': 'delta_w', 'delta_g_kv_norm': 'delta_w', 'delta_g_hgrn_norm': 'delta_w', 'delta_g_ffn_norm': 'delta_w', 'delta_g_final_norm': 'delta_w', 'delta_lb_raw': 'delta_w', 'new_m_meta_tokens': 'new_m', 'new_m_w_in': 'new_m', 'new_m_w_q_up': 'new_m', 'new_m_w_kv_up': 'new_m', 'new_m_w_branch_mla': 'new_m', 'new_m_w_branch_hgrn': 'new_m', 'new_m_w_out': 'new_m', 'new_m_w_ffn_in': 'new_m', 'new_m_w_ffn_out': 'new_m', 'new_m_conv_w': 'new_m', 'new_m_conv_b': 'new_m', 'new_m_g_mix_norm': 'new_m', 'new_m_g_q_norm': 'new_m', 'new_m_g_kv_norm': 'new_m', 'new_m_g_hgrn_norm': 'new_m', 'new_m_g_ffn_norm': 'new_m', 'new_m_g_final_norm': 'new_m', 'new_m_lb_raw': 'new_m', 'new_v_meta_tokens': 'new_v', 'new_v_w_in': 'new_v', 'new_v_w_q_up': 'new_v', 'new_v_w_kv_up': 'new_v', 'new_v_w_branch_mla': 'new_v', 'new_v_w_branch_hgrn': 'new_v', 'new_v_w_out': 'new_v', 'new_v_w_ffn_in': 'new_v', 'new_v_w_ffn_out': 'new_v', 'new_v_conv_w': 'new_v', 'new_v_conv_b': 'new_v', 'new_v_g_mix_norm': 'new_v', 'new_v_g_q_norm': 'new_v', 'new_v_g_kv_norm': 'new_v', 'new_v_g_hgrn_norm': 'new_v', 'new_v_g_ffn_norm': 'new_v', 'new_v_g_final_norm': 'new_v', 'new_v_lb_raw': 'new_v'}


def _forward(args):
    return _fwd_reference(*[args[k] for k in FWD_PARAMS])


def _output_shape():
    def fwd():
        inp = _fwd_setup_inputs(0)
        return _fwd_reference(*[inp[k] for k in FWD_PARAMS])
    out = _jax.eval_shape(fwd)
    return out.shape, out.dtype

N_MICROBATCH = 1
ADAM_LR = 0.001
ADAM_B1 = 0.9
ADAM_B2 = 0.999
ADAM_EPS = 1e-08
ADAM_WD = 0.01
ADAM_STEP = 10
PER_EXAMPLE_BATCH_AXIS = {'x': 0, 'positions': 0, 'loss_target': 0}
SHARED_INPUTS = []
_WEIGHT_DTYPES = {'meta_tokens': _jnp.float32, 'w_in': _jnp.float32, 'w_q_up': _jnp.float32, 'w_kv_up': _jnp.float32, 'w_branch_mla': _jnp.float32, 'w_branch_hgrn': _jnp.float32, 'w_out': _jnp.float32, 'w_ffn_in': _jnp.float32, 'w_ffn_out': _jnp.float32, 'conv_w': _jnp.float32, 'conv_b': _jnp.float32, 'g_mix_norm': _jnp.float32, 'g_q_norm': _jnp.float32, 'g_kv_norm': _jnp.float32, 'g_hgrn_norm': _jnp.float32, 'g_ffn_norm': _jnp.float32, 'g_final_norm': _jnp.float32, 'lb_raw': _jnp.float32}
MOMENT_SCALE = {'meta_tokens': 2.525388e-03, 'w_in': 2.564447e-02, 'w_q_up': 9.302085e-03, 'w_kv_up': 1.090807e-02, 'w_branch_mla': 1.217393e-02, 'w_branch_hgrn': 4.344545e-02, 'w_out': 4.481821e-02, 'w_ffn_in': 3.864389e-02, 'w_ffn_out': 6.299280e-02, 'conv_w': 3.956224e-02, 'conv_b': 3.795683e-02, 'g_mix_norm': 6.915006e-02, 'g_q_norm': 1.311435e-02, 'g_kv_norm': 3.180984e-02, 'g_hgrn_norm': 1.920104e-01, 'g_ffn_norm': 9.316482e-02, 'g_final_norm': 3.196720e+01, 'lb_raw': 4.223065e-03}


def _to_microbatches(a, axis):
    t = _jnp.moveaxis(a, axis, 0)
    t = t.reshape((N_MICROBATCH, t.shape[0] // N_MICROBATCH) + t.shape[1:])
    return _jnp.moveaxis(t, 1, axis + 1)


def setup_inputs(seed: int = 0) -> dict:
    inp = _fwd_setup_inputs(seed)
    key = _jax.random.fold_in(_jax.random.key(seed), 7919)
    shape, _ = _output_shape()
    out = dict(inp)
    out["loss_target"] = _jax.random.normal(_jax.random.fold_in(key, 0), shape, _jnp.float32)
    for i, name in enumerate(TWIN_WEIGHTS):
        w = inp[name].astype(_jnp.float32)
        if MOMENT_SCALE is None:
            s = _jnp.sqrt(_jnp.mean(_jnp.square(w)) + 1e-30)
        else:
            s = MOMENT_SCALE[name]
        km, kv = _jax.random.split(_jax.random.fold_in(key, i + 1))
        out[name] = w
        out["m_" + name] = s * _jax.random.normal(km, w.shape, _jnp.float32)
        out["v_" + name] = (s * s) * _jax.random.uniform(kv, w.shape, _jnp.float32, 0.5, 1.5)
    if N_MICROBATCH > 1:
        for name, axis in PER_EXAMPLE_BATCH_AXIS.items():
            out[name] = _to_microbatches(out[name], axis)
    return {'x': out['x'], 'positions': out['positions'], 'meta_tokens': out['meta_tokens'], 'w_in': out['w_in'], 'w_q_up': out['w_q_up'], 'w_kv_up': out['w_kv_up'], 'w_branch_mla': out['w_branch_mla'], 'w_branch_hgrn': out['w_branch_hgrn'], 'w_out': out['w_out'], 'w_ffn_in': out['w_ffn_in'], 'w_ffn_out': out['w_ffn_out'], 'conv_w': out['conv_w'], 'conv_b': out['conv_b'], 'g_mix_norm': out['g_mix_norm'], 'g_q_norm': out['g_q_norm'], 'g_kv_norm': out['g_kv_norm'], 'g_hgrn_norm': out['g_hgrn_norm'], 'g_ffn_norm': out['g_ffn_norm'], 'g_final_norm': out['g_final_norm'], 'lb_raw': out['lb_raw'], 'loss_target': out['loss_target'], 'm_meta_tokens': out['m_meta_tokens'], 'm_w_in': out['m_w_in'], 'm_w_q_up': out['m_w_q_up'], 'm_w_kv_up': out['m_w_kv_up'], 'm_w_branch_mla': out['m_w_branch_mla'], 'm_w_branch_hgrn': out['m_w_branch_hgrn'], 'm_w_out': out['m_w_out'], 'm_w_ffn_in': out['m_w_ffn_in'], 'm_w_ffn_out': out['m_w_ffn_out'], 'm_conv_w': out['m_conv_w'], 'm_conv_b': out['m_conv_b'], 'm_g_mix_norm': out['m_g_mix_norm'], 'm_g_q_norm': out['m_g_q_norm'], 'm_g_kv_norm': out['m_g_kv_norm'], 'm_g_hgrn_norm': out['m_g_hgrn_norm'], 'm_g_ffn_norm': out['m_g_ffn_norm'], 'm_g_final_norm': out['m_g_final_norm'], 'm_lb_raw': out['m_lb_raw'], 'v_meta_tokens': out['v_meta_tokens'], 'v_w_in': out['v_w_in'], 'v_w_q_up': out['v_w_q_up'], 'v_w_kv_up': out['v_w_kv_up'], 'v_w_branch_mla': out['v_w_branch_mla'], 'v_w_branch_hgrn': out['v_w_branch_hgrn'], 'v_w_out': out['v_w_out'], 'v_w_ffn_in': out['v_w_ffn_in'], 'v_w_ffn_out': out['v_w_ffn_out'], 'v_conv_w': out['v_conv_w'], 'v_conv_b': out['v_conv_b'], 'v_g_mix_norm': out['v_g_mix_norm'], 'v_g_q_norm': out['v_g_q_norm'], 'v_g_kv_norm': out['v_g_kv_norm'], 'v_g_hgrn_norm': out['v_g_hgrn_norm'], 'v_g_ffn_norm': out['v_g_ffn_norm'], 'v_g_final_norm': out['v_g_final_norm'], 'v_lb_raw': out['v_lb_raw']}


def _loss(weights, diff, rest, loss_target):
    with _jax.named_scope("forward"):
        args = {**rest, TWIN_DIFF_INPUT: diff, **{k: w.astype(_WEIGHT_DTYPES[k]) for k, w in weights.items()}}
        y = _forward(args)
    with _jax.named_scope("loss_head"):
        err = _jnp.square(y.astype(_jnp.float32) - loss_target)
        return 0.5 * _jnp.sum(_jnp.mean(err, axis=-1)) if err.ndim else 0.5 * err


def _adamw(w, g, m, v):
    m = ADAM_B1 * m + (1.0 - ADAM_B1) * g
    v = ADAM_B2 * v + (1.0 - ADAM_B2) * _jnp.square(g)
    m_hat = m / (1.0 - ADAM_B1 ** ADAM_STEP)
    v_hat = v / (1.0 - ADAM_B2 ** ADAM_STEP)
    delta = -ADAM_LR * (m_hat / (_jnp.sqrt(v_hat) + ADAM_EPS) + ADAM_WD * w)
    return delta, m, v


def reference(x, positions, meta_tokens, w_in, w_q_up, w_kv_up, w_branch_mla, w_branch_hgrn, w_out, w_ffn_in, w_ffn_out, conv_w, conv_b, g_mix_norm, g_q_norm, g_kv_norm, g_hgrn_norm, g_ffn_norm, g_final_norm, lb_raw, loss_target, m_meta_tokens, m_w_in, m_w_q_up, m_w_kv_up, m_w_branch_mla, m_w_branch_hgrn, m_w_out, m_w_ffn_in, m_w_ffn_out, m_conv_w, m_conv_b, m_g_mix_norm, m_g_q_norm, m_g_kv_norm, m_g_hgrn_norm, m_g_ffn_norm, m_g_final_norm, m_lb_raw, v_meta_tokens, v_w_in, v_w_q_up, v_w_kv_up, v_w_branch_mla, v_w_branch_hgrn, v_w_out, v_w_ffn_in, v_w_ffn_out, v_conv_w, v_conv_b, v_g_mix_norm, v_g_q_norm, v_g_kv_norm, v_g_hgrn_norm, v_g_ffn_norm, v_g_final_norm, v_lb_raw):
    given = dict(x=x, positions=positions, meta_tokens=meta_tokens, w_in=w_in, w_q_up=w_q_up, w_kv_up=w_kv_up, w_branch_mla=w_branch_mla, w_branch_hgrn=w_branch_hgrn, w_out=w_out, w_ffn_in=w_ffn_in, w_ffn_out=w_ffn_out, conv_w=conv_w, conv_b=conv_b, g_mix_norm=g_mix_norm, g_q_norm=g_q_norm, g_kv_norm=g_kv_norm, g_hgrn_norm=g_hgrn_norm, g_ffn_norm=g_ffn_norm, g_final_norm=g_final_norm, lb_raw=lb_raw, loss_target=loss_target, m_meta_tokens=m_meta_tokens, m_w_in=m_w_in, m_w_q_up=m_w_q_up, m_w_kv_up=m_w_kv_up, m_w_branch_mla=m_w_branch_mla, m_w_branch_hgrn=m_w_branch_hgrn, m_w_out=m_w_out, m_w_ffn_in=m_w_ffn_in, m_w_ffn_out=m_w_ffn_out, m_conv_w=m_conv_w, m_conv_b=m_conv_b, m_g_mix_norm=m_g_mix_norm, m_g_q_norm=m_g_q_norm, m_g_kv_norm=m_g_kv_norm, m_g_hgrn_norm=m_g_hgrn_norm, m_g_ffn_norm=m_g_ffn_norm, m_g_final_norm=m_g_final_norm, m_lb_raw=m_lb_raw, v_meta_tokens=v_meta_tokens, v_w_in=v_w_in, v_w_q_up=v_w_q_up, v_w_kv_up=v_w_kv_up, v_w_branch_mla=v_w_branch_mla, v_w_branch_hgrn=v_w_branch_hgrn, v_w_out=v_w_out, v_w_ffn_in=v_w_ffn_in, v_w_ffn_out=v_w_ffn_out, v_conv_w=v_conv_w, v_conv_b=v_conv_b, v_g_mix_norm=v_g_mix_norm, v_g_q_norm=v_g_q_norm, v_g_kv_norm=v_g_kv_norm, v_g_hgrn_norm=v_g_hgrn_norm, v_g_ffn_norm=v_g_ffn_norm, v_g_final_norm=v_g_final_norm, v_lb_raw=v_lb_raw)
    weights = {n: given[n] for n in TWIN_WEIGHTS}
    shared = {n: given[n] for n in SHARED_INPUTS}
    per_example = {n: given[n] for n in ['x', 'positions']}
    grad_fn = _jax.value_and_grad(_loss, argnums=(0, 1))

    def one_microbatch(ex, loss_target):
        ex = dict(ex)
        diff = ex.pop(TWIN_DIFF_INPUT)
        return grad_fn(weights, diff, {**shared, **ex}, loss_target)

    if N_MICROBATCH == 1:
        loss, (grad_w, grad_x) = one_microbatch(per_example, given["loss_target"])
    else:
        def body(carry, xs):
            loss_sum, grad_sum = carry
            l_k, (gw_k, gx_k) = one_microbatch(xs[0], xs[1])
            with _jax.named_scope("update"):
                return (loss_sum + l_k, _jax.tree.map(_jnp.add, grad_sum, gw_k)), gx_k

        init = (_jnp.zeros((), _jnp.float32), _jax.tree.map(_jnp.zeros_like, weights))
        (loss, grad_w), grad_x = _jax.lax.scan(body, init, (per_example, given["loss_target"]))
    with _jax.named_scope("update"):
        delta_w, new_m, new_v = {}, {}, {}
        for n in TWIN_WEIGHTS:
            delta_w[n], new_m[n], new_v[n] = _adamw(weights[n], grad_w[n], given["m_" + n], given["v_" + n])
    return (loss, grad_x, *[grad_w[n] for n in TWIN_WEIGHTS], *[delta_w[n] for n in TWIN_WEIGHTS],
            *[new_m[n] for n in TWIN_WEIGHTS], *[new_v[n] for n in TWIN_WEIGHTS])
```

```python
import math

import jax
import jax.numpy as jnp
from jax import lax
from jax.experimental import pallas as pl
from jax.experimental.pallas import tpu as pltpu

F32 = jnp.float32
BF16 = jnp.bfloat16
MESH = pl.DeviceIdType.MESH

NORM_EPS = 1e-6
N_META = 16
PREFIX = 128
PAD_LEN = PREFIX - N_META
HEAD = 128
ROPE = 64
QK_HEAD = HEAD + ROPE
QPAD = 2 * HEAD
ROPE_THETA = 10000.0
CHUNK = 128
LEVELS = 7
NEG = -1e30

ADAM_LR = 0.001
ADAM_B1 = 0.9
ADAM_B2 = 0.999
ADAM_EPS = 1e-08
ADAM_WD = 0.01
ADAM_STEP = 10

VMEM_LIMIT_BYTES = 48 * 1024 * 1024


def _params(*sem):
    return pltpu.CompilerParams(dimension_semantics=sem, vmem_limit_bytes=VMEM_LIMIT_BYTES)


def _tile(n, prefs):
    for p in prefs:
        if n % p == 0:
            return p
    return n


ROW_TILES = (640, 512, 256, 128, 64, 32, 16, 8)
COL_TILES = (1024, 512, 256, 128)
K_TILES = (512, 256, 128)


def _sigmoid(x):
    return 1.0 / (1.0 + jnp.exp(-x))


def _dot(a, b, dims):
    return lax.dot_general(a.astype(BF16), b.astype(BF16), (dims, ((), ())), preferred_element_type=F32)


NN = ((1,), (0,))
TN = ((0,), (0,))
NT = ((1,), (1,))


def _split3(x):
    hi = x.astype(BF16)
    r = x - hi.astype(F32)
    mid = r.astype(BF16)
    lo = (r - mid.astype(F32)).astype(BF16)
    return hi, mid, lo


def _dot_exact_rhs(sel, x, dims):
    hi, mid, lo = _split3(x)
    return _dot(sel, hi, dims) + _dot(sel, mid, dims) + _dot(sel, lo, dims)


def _mm(a, b, mode, out_dtype, name, res=None):
    if mode == "nn":
        (M, K), (K2, N) = a.shape, b.shape
    elif mode == "tn":
        (K, M), (K2, N) = a.shape, b.shape
    else:
        (M, K), (N, K2) = a.shape, b.shape
    assert K == K2, (name, a.shape, b.shape)
    tm = _tile(M, ROW_TILES)
    tn = _tile(N, COL_TILES)
    tk = _tile(K, ROW_TILES if mode == "tn" else K_TILES)
    nk = K // tk
    dims = {"nn": NN, "tn": TN, "nt": NT}[mode]

    def body(*refs):
        if res is None:
            a_ref, b_ref, o_ref, acc = refs
            r_ref = None
        else:
            a_ref, b_ref, r_ref, o_ref, acc = refs
        k = pl.program_id(2)

        @pl.when(k == 0)
        def _():
            acc[...] = jnp.zeros_like(acc)

        acc[...] += _dot(a_ref[...], b_ref[...], dims)

        @pl.when(k == nk - 1)
        def _():
            r = acc[...]
            if r_ref is not None:
                r = r + r_ref[...].astype(F32)
            o_ref[...] = r.astype(out_dtype)

    if mode == "nn":
        a_spec = pl.BlockSpec((tm, tk), lambda i, j, k: (i, k))
        b_spec = pl.BlockSpec((tk, tn), lambda i, j, k: (k, j))
    elif mode == "tn":
        a_spec = pl.BlockSpec((tk, tm), lambda i, j, k: (k, i))
        b_spec = pl.BlockSpec((tk, tn), lambda i, j, k: (k, j))
    else:
        a_spec = pl.BlockSpec((tm, tk), lambda i, j, k: (i, k))
        b_spec = pl.BlockSpec((tn, tk), lambda i, j, k: (j, k))
    in_specs = [a_spec, b_spec]
    operands = [a, b]
    if res is not None:
        in_specs.append(pl.BlockSpec((tm, tn), lambda i, j, k: (i, j)))
        operands.append(res)
    return pl.pallas_call(
        body, name=name,
        out_shape=jax.ShapeDtypeStruct((M, N), out_dtype),
        grid=(M // tm, N // tn, nk),
        in_specs=in_specs,
        out_specs=pl.BlockSpec((tm, tn), lambda i, j, k: (i, j)),
        scratch_shapes=[pltpu.VMEM((tm, tn), F32)],
        compiler_params=_params("parallel", "parallel", "arbitrary"),
    )(*operands)


def _rmsnorm_fwd(x, g, name, col0=0, width=None):
    L = x.shape[0]
    width = x.shape[1] if width is None else width
    assert col0 % width == 0
    cb = col0 // width
    tm = _tile(L, (128, 64, 32, 16))

    def body(x_ref, g_ref, o_ref):
        xv = x_ref[...]
        r = lax.rsqrt(jnp.mean(xv * xv, axis=-1, keepdims=True) + NORM_EPS)
        o_ref[...] = ((xv * r) * g_ref[...]).astype(BF16)

    return pl.pallas_call(
        body, name=name,
        out_shape=jax.ShapeDtypeStruct((L, width), BF16),
        grid=(L // tm,),
        in_specs=[pl.BlockSpec((tm, width), lambda i: (i, cb)), pl.BlockSpec((1, width), lambda i: (0, 0))],
        out_specs=pl.BlockSpec((tm, width), lambda i: (i, 0)),
        compiler_params=_params("parallel"),
    )(x, g)


def _rmsnorm_bwd(dy, x, g, name, out_dtype, col0=0, res=None):
    L, width = dy.shape
    assert col0 % width == 0
    cb = col0 // width
    tm = _tile(L, (128, 64, 32, 16))

    def body(*refs):
        if res is None:
            dy_ref, x_ref, g_ref, dx_ref, dg_ref = refs
            r_ref = None
        else:
            dy_ref, x_ref, g_ref, r_ref, dx_ref, dg_ref = refs

        @pl.when(pl.program_id(0) == 0)
        def _():
            dg_ref[...] = jnp.zeros_like(dg_ref)

        xv = x_ref[...]
        dyv = dy_ref[...].astype(F32)
        r = lax.rsqrt(jnp.mean(xv * xv, axis=-1, keepdims=True) + NORM_EPS)
        z = dyv * g_ref[...]
        dx = r * z - xv * ((r * r * r) * jnp.mean(z * xv, axis=-1, keepdims=True))
        if r_ref is not None:
            dx = dx + r_ref[...]
        dx_ref[...] = dx.astype(out_dtype)
        dg_ref[...] += jnp.sum(dyv * (xv * r), axis=0, keepdims=True)

    in_specs = [pl.BlockSpec((tm, width), lambda i: (i, 0)),
                pl.BlockSpec((tm, width), lambda i: (i, cb)),
                pl.BlockSpec((1, width), lambda i: (0, 0))]
    operands = [dy, x, g]
    if res is not None:
        in_specs.append(pl.BlockSpec((tm, width), lambda i: (i, 0)))
        operands.append(res)
    return pl.pallas_call(
        body, name=name,
        out_shape=(jax.ShapeDtypeStruct((L, width), out_dtype), jax.ShapeDtypeStruct((1, width), F32)),
        grid=(L // tm,),
        in_specs=in_specs,
        out_specs=(pl.BlockSpec((tm, width), lambda i: (i, 0)), pl.BlockSpec((1, width), lambda i: (0, 0))),
        compiler_params=_params("arbitrary"),
    )(*operands)


def _final_loss_bwd(h2, tgt, g):
    L, D = h2.shape
    tm = PREFIX
    inv_d = 1.0 / D

    def body(h_ref, t_ref, g_ref, dh_ref, loss_ref, dg_ref):
        i = pl.program_id(0)

        @pl.when(i == 0)
        def _():
            loss_ref[...] = jnp.zeros_like(loss_ref)
            dg_ref[...] = jnp.zeros_like(dg_ref)

        xv = h_ref[...]
        r = lax.rsqrt(jnp.mean(xv * xv, axis=-1, keepdims=True) + NORM_EPS)
        xn = xv * r
        y = xn * g_ref[...]
        real = (i >= PREFIX // tm).astype(F32)
        diff = (y - t_ref[...]) * real
        loss_ref[...] += 0.5 * inv_d * jnp.sum(diff * diff)
        dyv = diff * inv_d
        z = dyv * g_ref[...]
        dh_ref[...] = r * z - xv * ((r * r * r) * jnp.mean(z * xv, axis=-1, keepdims=True))
        dg_ref[...] += jnp.sum(dyv * xn, axis=0, keepdims=True)

    shift = PREFIX // tm
    return pl.pallas_call(
        body, name="final_loss_bwd",
        out_shape=(jax.ShapeDtypeStruct((L, D), F32), jax.ShapeDtypeStruct((1, 128), F32),
                   jax.ShapeDtypeStruct((1, D), F32)),
        grid=(L // tm,),
        in_specs=[pl.BlockSpec((tm, D), lambda i: (i, 0)),
                  pl.BlockSpec((tm, D), lambda i: (jnp.maximum(i - shift, 0), 0)),
                  pl.BlockSpec((1, D), lambda i: (0, 0))],
        out_specs=(pl.BlockSpec((tm, D), lambda i: (i, 0)), pl.BlockSpec((1, 128), lambda i: (0, 0)),
                   pl.BlockSpec((1, D), lambda i: (0, 0))),
        compiler_params=_params("arbitrary"),
    )(h2, tgt, g)


def _rot_half(x):
    lane = lax.broadcasted_iota(jnp.int32, x.shape, 1)
    return jnp.where(lane < ROPE // 2, -pltpu.roll(x, 128 - ROPE // 2, 1), pltpu.roll(x, ROPE // 2, 1))


def _rope_fwd(qp, kv, kr, cos, sin, heads):
    L = qp.shape[0]
    tm = _tile(L, (128,))

    def body(q_ref, kv_ref, kr_ref, c_ref, s_ref, qc_ref, kc_ref, v_ref):
        c, s = c_ref[...], s_ref[...]
        krv = kr_ref[...]
        kr_rot = (krv * c + _rot_half(krv) * s).astype(BF16)
        for h in range(heads):
            lo = h * QPAD
            qc_ref[:, lo:lo + HEAD] = q_ref[:, lo:lo + HEAD].astype(BF16)
            qr = q_ref[:, lo + HEAD:lo + QPAD]
            qc_ref[:, lo + HEAD:lo + QPAD] = (qr * c + _rot_half(qr) * s).astype(BF16)
            kc_ref[:, lo:lo + HEAD] = kv_ref[:, lo:lo + HEAD].astype(BF16)
            kc_ref[:, lo + HEAD:lo + QPAD] = kr_rot
            v_ref[:, h * HEAD:(h + 1) * HEAD] = kv_ref[:, lo + HEAD:lo + QPAD].astype(BF16)

    W = heads * QPAD
    row = lambda w: pl.BlockSpec((tm, w), lambda i: (i, 0))
    return pl.pallas_call(
        body, name="rope_fwd",
        out_shape=(jax.ShapeDtypeStruct((L, W), BF16), jax.ShapeDtypeStruct((L, W), BF16),
                   jax.ShapeDtypeStruct((L, heads * HEAD), BF16)),
        grid=(L // tm,),
        in_specs=[row(W), row(W), row(128), row(128), row(128)],
        out_specs=(row(W), row(W), row(heads * HEAD)),
        compiler_params=_params("parallel"),
    )(qp, kv, kr, cos, sin)


def _rope_bwd(dqc, dkc, dv, cos, sin, heads):
    L = dqc.shape[0]
    tm = _tile(L, (128,))

    def body(dq_ref, dk_ref, dv_ref, c_ref, s_ref, dqp_ref, dkv_ref, dkr_ref):
        c, s = c_ref[...], s_ref[...]
        acc = jnp.zeros((tm, 128), F32)
        for h in range(heads):
            lo = h * QPAD
            dqp_ref[:, lo:lo + HEAD] = dq_ref[:, lo:lo + HEAD].astype(BF16)
            d = dq_ref[:, lo + HEAD:lo + QPAD]
            dqp_ref[:, lo + HEAD:lo + QPAD] = (d * c - _rot_half(d) * s).astype(BF16)
            dkv_ref[:, lo:lo + HEAD] = dk_ref[:, lo:lo + HEAD].astype(BF16)
            dkv_ref[:, lo + HEAD:lo + QPAD] = dv_ref[:, h * HEAD:(h + 1) * HEAD].astype(BF16)
            acc = acc + dk_ref[:, lo + HEAD:lo + QPAD]
        dkr_ref[...] = (acc * c - _rot_half(acc) * s).astype(BF16)

    W = heads * QPAD
    row = lambda w: pl.BlockSpec((tm, w), lambda i: (i, 0))
    return pl.pallas_call(
        body, name="rope_bwd",
        out_shape=(jax.ShapeDtypeStruct((L, W), BF16), jax.ShapeDtypeStruct((L, W), BF16),
                   jax.ShapeDtypeStruct((L, 128), BF16)),
        grid=(L // tm,),
        in_specs=[row(W), row(W), row(heads * HEAD), row(128), row(128)],
        out_specs=(row(W), row(W), row(128)),
        compiler_params=_params("parallel"),
    )(dqc, dkc, dv, cos, sin)


def _attn_keep(qi, ki, ta):
    t = qi * ta + lax.broadcasted_iota(jnp.int32, (ta, ta), 0)
    s = ki * ta + lax.broadcasted_iota(jnp.int32, (ta, ta), 1)
    return (s <= t) & ((s >= PAD_LEN) | (s == t))


def _attn_fwd(qc, kc, v, heads):
    L = qc.shape[0]
    ta = _tile(L, (640, 128))
    nb = L // ta
    scale = QK_HEAD ** -0.5

    def body(q_ref, k_ref, v_ref, o_ref, lse_ref, m_sc, l_sc, acc_sc):
        qi, ki = pl.program_id(1), pl.program_id(2)

        @pl.when(ki == 0)
        def _():
            m_sc[...] = jnp.full_like(m_sc, NEG)
            l_sc[...] = jnp.zeros_like(l_sc)
            acc_sc[...] = jnp.zeros_like(acc_sc)

        @pl.when(ki <= qi)
        def _():
            s = _dot(q_ref[...], k_ref[...], NT) * scale
            s = jnp.where(_attn_keep(qi, ki, ta), s, NEG)
            m_old = m_sc[...]
            m_new = jnp.maximum(m_old, jnp.max(s, axis=-1, keepdims=True))
            p = jnp.exp(s - m_new)
            alpha = jnp.exp(m_old - m_new)
            l_sc[...] = alpha * l_sc[...] + jnp.sum(p, axis=-1, keepdims=True)
            acc_sc[...] = alpha * acc_sc[...] + _dot(p, v_ref[...], NN)
            m_sc[...] = m_new

        @pl.when(ki == qi)
        def _():
            l = l_sc[...]
            o_ref[...] = (acc_sc[...] / l).astype(BF16)
            lse_ref[...] = jnp.broadcast_to(m_sc[...] + jnp.log(l), (ta, HEAD))

    return pl.pallas_call(
        body, name="attn_fwd",
        out_shape=(jax.ShapeDtypeStruct((L, heads * HEAD), BF16), jax.ShapeDtypeStruct((L, heads * HEAD), F32)),
        grid=(heads, nb, nb),
        in_specs=[pl.BlockSpec((ta, QPAD), lambda h, i, j: (i, h)),
                  pl.BlockSpec((ta, QPAD), lambda h, i, j: (jnp.minimum(j, i), h)),
                  pl.BlockSpec((ta, HEAD), lambda h, i, j: (jnp.minimum(j, i), h))],
        out_specs=(pl.BlockSpec((ta, HEAD), lambda h, i, j: (i, h)),
                   pl.BlockSpec((ta, HEAD), lambda h, i, j: (i, h))),
        scratch_shapes=[pltpu.VMEM((ta, 1), F32), pltpu.VMEM((ta, 1), F32), pltpu.VMEM((ta, HEAD), F32)],
        compiler_params=_params("parallel", "parallel", "arbitrary"),
    )(qc, kc, v)


def _attn_bwd(qc, kc, v, o, do, lse, heads):
    L = qc.shape[0]
    ta = _tile(L, (640, 128))
    nb = L // ta
    scale = QK_HEAD ** -0.5

    def body(q_ref, k_ref, v_ref, o_ref, do_ref, lse_ref, dq_ref, dk_ref, dv_ref):
        kj, qi = pl.program_id(1), pl.program_id(2)

        @pl.when((kj == 0) & (qi == 0))
        def _():
            dq_ref[...] = jnp.zeros_like(dq_ref)

        @pl.when(qi == 0)
        def _():
            dk_ref[...] = jnp.zeros_like(dk_ref)
            dv_ref[...] = jnp.zeros_like(dv_ref)

        @pl.when(qi >= kj)
        def _():
            q, k, vv, dov = q_ref[...], k_ref[...], v_ref[...], do_ref[...]
            s = _dot(q, k, NT) * scale
            p = jnp.where(_attn_keep(qi, kj, ta), jnp.exp(s - lse_ref[:, 0:1]), 0.0)
            delta = jnp.sum(dov.astype(F32) * o_ref[...].astype(F32), axis=-1, keepdims=True)
            dp = _dot(dov, vv, NT)
            ds = p * (dp - delta) * scale
            dv_ref[...] += _dot(p, dov, TN)
            dk_ref[...] += _dot(ds, q, TN)
            rows = pl.ds(pl.multiple_of(qi * ta, ta), ta)
            dq_ref[rows, :] += _dot(ds, k, NN)

    qrow = lambda w: pl.BlockSpec((ta, w), lambda h, j, i: (jnp.maximum(i, j), h))
    krow = lambda w: pl.BlockSpec((ta, w), lambda h, j, i: (j, h))
    return pl.pallas_call(
        body, name="attn_bwd",
        out_shape=(jax.ShapeDtypeStruct((L, heads * QPAD), F32), jax.ShapeDtypeStruct((L, heads * QPAD), F32),
                   jax.ShapeDtypeStruct((L, heads * HEAD), F32)),
        grid=(heads, nb, nb),
        in_specs=[qrow(QPAD), krow(QPAD), krow(HEAD), qrow(HEAD), qrow(HEAD), qrow(HEAD)],
        out_specs=(pl.BlockSpec((L, QPAD), lambda h, j, i: (0, h)), krow(QPAD), krow(HEAD)),
        compiler_params=_params("parallel", "arbitrary", "arbitrary"),
    )(qc, kc, v, o, do, lse)


def _hgrn_constants():
    t = jnp.arange(CHUNK)
    tril = (t[None, :] <= t[:, None]).astype(BF16)
    sel = []
    for lv in range(LEVELS):
        hs = 1 << lv
        mid = (t // (2 * hs)) * (2 * hs) + hs - 1
        sel.append((t[None, :] == mid[:, None]).astype(BF16))
    return tril, jnp.concatenate(sel, axis=0)


def _hgrn_decay_grad_constants():
    r = jnp.arange(CHUNK)[:, None]
    c = jnp.arange(CHUNK)[None, :]
    mats = []
    for lv in range(LEVELS):
        same = (r >> (lv + 1)) == (c >> (lv + 1))
        second = ((r >> lv) & 1) == 1
        mats.append(same & jnp.where(second, c >= r, c < r))
    mats += [c >= r, c < r]
    return jnp.concatenate(mats, axis=1).astype(BF16)


def _hgrn_gates(hq_ref, hf_ref, hi_ref, lb_ref, row0):
    rows = row0 + lax.broadcasted_iota(jnp.int32, (CHUNK, 1), 0)
    valid = rows >= PAD_LEN
    lb = 1.0 / (1.0 + jnp.exp(lb_ref[1:2, :] - lb_ref[0:1, :]))
    hq = hq_ref[...]
    sq = _sigmoid(hq)
    sg = _sigmoid(hf_ref[...])
    f = lb + (1.0 - lb) * sg
    g = jnp.where(valid, jnp.log(f), 0.0)
    k = jnp.where(valid, 1.0 - f, 0.0)
    return dict(q=hq * sq, sq=sq, hq=hq, k=k, v=hi_ref[...], g=g, f=f, sg=sg, lb=lb, valid=valid)


def _hgrn_levels(q, k, b, sel_ref):
    t = lax.broadcasted_iota(jnp.int32, (CHUNK, 1), 0)
    tt = lax.broadcasted_iota(jnp.int32, (CHUNK, CHUNK), 0)
    ss = lax.broadcasted_iota(jnp.int32, (CHUNK, CHUNK), 1)
    bm_all = _dot_exact_rhs(sel_ref[...], b, NN)
    out = []
    for lv in range(LEVELS):
        bm = bm_all[lv * CHUNK:(lv + 1) * CHUNK, :]
        second = ((t >> lv) & 1) == 1
        eq = jnp.where(second, jnp.exp(jnp.minimum(b - bm, 0.0)), 0.0)
        ek = jnp.where(second, 0.0, jnp.exp(jnp.minimum(bm - b, 0.0)))
        same = (tt >> (lv + 1)) == (ss >> (lv + 1))
        out.append((eq, ek, (q * eq).astype(BF16), (k * ek).astype(BF16), same))
    return out


def _hgrn_intra(q, k, levels):
    tt = lax.broadcasted_iota(jnp.int32, (CHUNK, CHUNK), 0)
    ss = lax.broadcasted_iota(jnp.int32, (CHUNK, CHUNK), 1)
    p = jnp.where(tt == ss, jnp.sum(q * k, axis=-1, keepdims=True), 0.0)
    for (_, _, ql, kl, same) in levels:
        p = p + jnp.where(same, _dot(ql, kl, NT), 0.0)
    return p


def _hgrn_fwd(hp, lb_raw, g_norm, nh):
    L = hp.shape[0]
    D = nh * HEAD
    nc = L // CHUNK
    tril, sel = _hgrn_constants()

    def body(hq_ref, hf_ref, hi_ref, hg_ref, lb_ref, gn_ref, tril_ref, sel_ref,
             oh_ref, orec_ref, shist_ref, s_sc, b_sc):
        c = pl.program_id(1)

        @pl.when(c == 0)
        def _():
            s_sc[...] = jnp.zeros_like(s_sc)

        w = _hgrn_gates(hq_ref, hf_ref, hi_ref, lb_ref, c * CHUNK)
        q, k, v = w["q"], w["k"], w["v"]
        b = _dot_exact_rhs(tril_ref[...], w["g"], NN)
        b_sc[...] = b
        b_last = b_sc[CHUNK - 1:CHUNK, :]
        p = _hgrn_intra(q, k, _hgrn_levels(q, k, b, sel_ref))
        s_in = s_sc[...]
        shist_ref[0, 0] = s_in
        o = _dot(p, v, NN) + _dot(q * jnp.exp(b), s_in, NT)
        s_sc[...] = jnp.exp(b_last) * s_in + _dot(v, k * jnp.exp(b_last - b), TN)
        orec_ref[...] = o
        rn = lax.rsqrt(jnp.mean(o * o, axis=-1, keepdims=True) + NORM_EPS)
        hg = hg_ref[...]
        oh_ref[...] = (((o * rn) * gn_ref[...]) * (hg * _sigmoid(hg))).astype(BF16)

    col = lambda grp: pl.BlockSpec((CHUNK, HEAD), lambda h, c: (c, grp * nh + h))
    const = lambda shape: pl.BlockSpec(shape, lambda h, c: (0, 0))
    return pl.pallas_call(
        body, name="hgrn_fwd",
        out_shape=(jax.ShapeDtypeStruct((L, D), BF16), jax.ShapeDtypeStruct((L, D), F32),
                   jax.ShapeDtypeStruct((nc, nh, HEAD, HEAD), F32)),
        grid=(nh, nc),
        in_specs=[col(0), col(1), col(2), col(3),
                  pl.BlockSpec((2, HEAD), lambda h, c: (0, h)), const((1, HEAD)),
                  const((CHUNK, CHUNK)), const((LEVELS * CHUNK, CHUNK))],
        out_specs=(pl.BlockSpec((CHUNK, HEAD), lambda h, c: (c, h)),
                   pl.BlockSpec((CHUNK, HEAD), lambda h, c: (c, h)),
                   pl.BlockSpec((1, 1, HEAD, HEAD), lambda h, c: (c, h, 0, 0))),
        scratch_shapes=[pltpu.VMEM((HEAD, HEAD), F32), pltpu.VMEM((CHUNK, HEAD), F32)],
        compiler_params=_params("parallel", "arbitrary"),
    )(hp, hp, hp, hp, lb_raw, g_norm, tril, sel)


def _hgrn_bwd(hp, lb_raw, g_norm, do_h, o_rec, s_hist, nh):
    L = hp.shape[0]
    D = nh * HEAD
    nc = L // CHUNK
    tril, sel = _hgrn_constants()
    tdec = _hgrn_decay_grad_constants()

    def body(hq_ref, hf_ref, hi_ref, hg_ref, lb_ref, gn_ref, tril_ref, sel_ref, tdec_ref, do_ref, orec_ref, shist_ref,
             dhq_ref, dhf_ref, dhi_ref, dhg_ref, dgn_ref, dlb_ref, ds_sc, b_sc):
        ci = pl.program_id(1)
        c = nc - 1 - ci

        @pl.when(ci == 0)
        def _():
            ds_sc[...] = jnp.zeros_like(ds_sc)
            dgn_ref[...] = jnp.zeros_like(dgn_ref)
            dlb_ref[...] = jnp.zeros_like(dlb_ref)

        w = _hgrn_gates(hq_ref, hf_ref, hi_ref, lb_ref, c * CHUNK)
        q, k, v, valid = w["q"], w["k"], w["v"], w["valid"]
        b = _dot_exact_rhs(tril_ref[...], w["g"], NN)
        b_sc[...] = b
        b_last = b_sc[CHUNK - 1:CHUNK, :]
        levels = _hgrn_levels(q, k, b, sel_ref)
        p = _hgrn_intra(q, k, levels)
        s_in = shist_ref[0, 0]
        ds_out = ds_sc[...]
        eb = jnp.exp(b)
        etail = jnp.exp(b_last - b)
        decay = jnp.exp(b_last)

        o = orec_ref[...]
        hg = hg_ref[...]
        sgg = _sigmoid(hg)
        gate = hg * sgg
        rn = lax.rsqrt(jnp.mean(o * o, axis=-1, keepdims=True) + NORM_EPS)
        on = o * rn
        doh = do_ref[...]
        dy = doh * gate
        dhg_ref[...] = (doh * (on * gn_ref[...]) * (sgg * (1.0 + hg * (1.0 - sgg)))).astype(BF16)
        dgn_ref[0] += jnp.broadcast_to(jnp.sum(dy * on, axis=0, keepdims=True), (8, HEAD))
        z = dy * gn_ref[...]
        do = rn * z - o * ((rn * rn * rn) * jnp.mean(z * o, axis=-1, keepdims=True))

        tt = lax.broadcasted_iota(jnp.int32, (CHUNK, CHUNK), 0)
        ss = lax.broadcasted_iota(jnp.int32, (CHUNK, CHUNK), 1)
        dp = jnp.where(ss <= tt, _dot(do, v, NT), 0.0)
        dv = _dot(p, do, TN) + _dot(k * etail, ds_out, NT)
        dpd = jnp.sum(jnp.where(tt == ss, dp, 0.0), axis=-1, keepdims=True)
        dq_state = eb * _dot(do, s_in, NN)
        dk_state = etail * _dot(v, ds_out, NN)
        dq = dpd * k + dq_state
        dk = dpd * q + dk_state
        pair_terms = []
        for (eq, ek, ql, kl, same) in levels:
            dpl = jnp.where(same, dp, 0.0)
            dq_l = eq * _dot(dpl, kl, NN)
            dk_l = ek * _dot(dpl, ql, TN)
            dq = dq + dq_l
            dk = dk + dk_l
            pair_terms.append(q * dq_l + k * dk_l)
        pair_terms += [q * dq_state, k * dk_state]
        ds_sc[...] = decay * ds_out + _dot(do, q * eb, TN)
        through = jnp.sum((decay * s_in) * ds_out, axis=0, keepdims=True)
        dg = _dot_exact_rhs(tdec_ref[...], jnp.concatenate(pair_terms, axis=0), NN) + through

        f, sg, lb, sq, hq = w["f"], w["sg"], w["lb"], w["sq"], w["hq"]
        df = jnp.where(valid, dg / f - dk, 0.0)
        dhf_ref[...] = (df * (1.0 - lb) * sg * (1.0 - sg)).astype(BF16)
        dlb_ref[0] += jnp.broadcast_to(jnp.sum(df * (1.0 - sg), axis=0, keepdims=True), (8, HEAD))
        dhq_ref[...] = (dq * (sq * (1.0 + hq * (1.0 - sq)))).astype(BF16)
        dhi_ref[...] = dv.astype(BF16)

    col = lambda grp: pl.BlockSpec((CHUNK, HEAD), lambda h, c: (nc - 1 - c, grp * nh + h))
    const = lambda shape: pl.BlockSpec(shape, lambda h, c: (0, 0))
    tile = pl.BlockSpec((CHUNK, HEAD), lambda h, c: (nc - 1 - c, h))
    part = pl.BlockSpec((1, 8, HEAD), lambda h, c: (h, 0, 0))
    return pl.pallas_call(
        body, name="hgrn_bwd",
        out_shape=tuple([jax.ShapeDtypeStruct((L, D), BF16)] * 4 + [jax.ShapeDtypeStruct((nh, 8, HEAD), F32)] * 2),
        grid=(nh, nc),
        in_specs=[col(0), col(1), col(2), col(3),
                  pl.BlockSpec((2, HEAD), lambda h, c: (0, h)), const((1, HEAD)),
                  const((CHUNK, CHUNK)), const((LEVELS * CHUNK, CHUNK)), const((CHUNK, (LEVELS + 2) * CHUNK)),
                  tile, tile, pl.BlockSpec((1, 1, HEAD, HEAD), lambda h, c: (nc - 1 - c, h, 0, 0))],
        out_specs=(tile, tile, tile, tile, part, part),
        scratch_shapes=[pltpu.VMEM((HEAD, HEAD), F32), pltpu.VMEM((CHUNK, HEAD), F32)],
        compiler_params=_params("parallel", "arbitrary"),
    )(hp, hp, hp, hp, lb_raw, g_norm, tril, sel, tdec, do_h, o_rec, s_hist)


def _merge_fwd(a, bm, gates):
    L, D = a.shape
    tm = _tile(L, (128,))

    def body(a_ref, b_ref, g_ref, o_ref):
        o_ref[...] = (_sigmoid(g_ref[:, :D]) * a_ref[...] + _sigmoid(g_ref[:, D:]) * b_ref[...]).astype(BF16)

    row = lambda w: pl.BlockSpec((tm, w), lambda i: (i, 0))
    return pl.pallas_call(
        body, name="merge_fwd", out_shape=jax.ShapeDtypeStruct((L, D), BF16), grid=(L // tm,),
        in_specs=[row(D), row(D), row(2 * D)], out_specs=row(D), compiler_params=_params("parallel"),
    )(a, bm, gates)


def _merge_bwd(dm, a, bm, gates):
    L, D = a.shape
    tm = _tile(L, (128,))

    def body(dm_ref, a_ref, b_ref, g_ref, da_ref, db_ref, dg_ref):
        d = dm_ref[...]
        sa, sb = _sigmoid(g_ref[:, :D]), _sigmoid(g_ref[:, D:])
        da_ref[...] = (d * sa).astype(BF16)
        db_ref[...] = (d * sb).astype(BF16)
        dg_ref[:, :D] = (d * a_ref[...] * sa * (1.0 - sa)).astype(BF16)
        dg_ref[:, D:] = (d * b_ref[...] * sb * (1.0 - sb)).astype(BF16)

    row = lambda w: pl.BlockSpec((tm, w), lambda i: (i, 0))
    return pl.pallas_call(
        body, name="merge_bwd",
        out_shape=(jax.ShapeDtypeStruct((L, D), BF16), jax.ShapeDtypeStruct((L, D), BF16),
                   jax.ShapeDtypeStruct((L, 2 * D), BF16)),
        grid=(L // tm,),
        in_specs=[row(D), row(D), row(D), row(2 * D)], out_specs=(row(D), row(D), row(2 * D)),
        compiler_params=_params("parallel"),
    )(dm, a, bm, gates)


def _conv_taps(g_ref, halo_ref, i, tm):
    rows = i * tm + lax.broadcasted_iota(jnp.int32, (tm, 1), 0)
    g0 = jnp.where(rows >= PAD_LEN, g_ref[...], 0.0)
    hrow = i * tm - 8 + lax.broadcasted_iota(jnp.int32, (8, 1), 0)
    halo = jnp.where(hrow >= PAD_LEN, halo_ref[...], 0.0)
    r = lax.broadcasted_iota(jnp.int32, (tm, 1), 0)
    h7 = jnp.sum(jnp.where(lax.broadcasted_iota(jnp.int32, (8, 1), 0) == 7, halo, 0.0), axis=0, keepdims=True)
    h6 = jnp.sum(jnp.where(lax.broadcasted_iota(jnp.int32, (8, 1), 0) == 6, halo, 0.0), axis=0, keepdims=True)
    g1 = jnp.where(r == 0, h7, pltpu.roll(g0, 1, 0))
    g2 = jnp.where(r == 0, h6, jnp.where(r == 1, h7, pltpu.roll(g0, 2, 0)))
    return g0, g1, g2


def _conv_fwd(gu, cw, cb):
    L, F2 = gu.shape
    F = F2 // 2
    tm = _tile(L, ROW_TILES)
    tn = _tile(F, (512, 256, 128))
    nj = F // tn

    def body(g_ref, halo_ref, u_ref, cw_ref, cb_ref, o_ref):
        g0, g1, g2 = _conv_taps(g_ref, halo_ref, pl.program_id(0), tm)
        conv = cw_ref[0:1, :] * g2 + cw_ref[1:2, :] * g1 + cw_ref[2:3, :] * g0 + cb_ref[...]
        o_ref[...] = (conv * _sigmoid(conv) * u_ref[...]).astype(BF16)

    return pl.pallas_call(
        body, name="conv_fwd", out_shape=jax.ShapeDtypeStruct((L, F), BF16), grid=(L // tm, nj),
        in_specs=[pl.BlockSpec((tm, tn), lambda i, j: (i, j)),
                  pl.BlockSpec((8, tn), lambda i, j: (jnp.maximum(i * (tm // 8) - 1, 0), j)),
                  pl.BlockSpec((tm, tn), lambda i, j: (i, j + nj)),
                  pl.BlockSpec((3, tn), lambda i, j: (0, j)),
                  pl.BlockSpec((1, tn), lambda i, j: (0, j))],
        out_specs=pl.BlockSpec((tm, tn), lambda i, j: (i, j)),
        compiler_params=_params("parallel", "parallel"),
    )(gu, gu, gu, cw, cb)


def _conv_bwd_a(da, gu, cw, cb):
    L, F2 = gu.shape
    F = F2 // 2
    tm = _tile(L, ROW_TILES)
    tn = _tile(F, (512, 256, 128))
    nj = F // tn

    def body(da_ref, g_ref, halo_ref, u_ref, cw_ref, cb_ref, dc_ref, du_ref, dcb_ref, dcw_ref):
        i = pl.program_id(1)

        @pl.when(i == 0)
        def _():
            dcb_ref[...] = jnp.zeros_like(dcb_ref)
            dcw_ref[...] = jnp.zeros_like(dcw_ref)

        g0, g1, g2 = _conv_taps(g_ref, halo_ref, i, tm)
        conv = cw_ref[0:1, :] * g2 + cw_ref[1:2, :] * g1 + cw_ref[2:3, :] * g0 + cb_ref[...]
        sc = _sigmoid(conv)
        dav = da_ref[...]
        du_ref[...] = (dav * (conv * sc)).astype(BF16)
        dconv = dav * u_ref[...] * (sc * (1.0 + conv * (1.0 - sc)))
        dc_ref[...] = dconv
        dcb_ref[...] += jnp.sum(dconv, axis=0, keepdims=True)
        dcw_ref[0:1, :] += jnp.sum(dconv * g2, axis=0, keepdims=True)
        dcw_ref[1:2, :] += jnp.sum(dconv * g1, axis=0, keepdims=True)
        dcw_ref[2:3, :] += jnp.sum(dconv * g0, axis=0, keepdims=True)

    return pl.pallas_call(
        body, name="conv_bwd_a",
        out_shape=(jax.ShapeDtypeStruct((L, F), F32), jax.ShapeDtypeStruct((L, F), BF16),
                   jax.ShapeDtypeStruct((1, F), F32), jax.ShapeDtypeStruct((8, F), F32)),
        grid=(nj, L // tm),
        in_specs=[pl.BlockSpec((tm, tn), lambda j, i: (i, j)),
                  pl.BlockSpec((tm, tn), lambda j, i: (i, j)),
                  pl.BlockSpec((8, tn), lambda j, i: (jnp.maximum(i * (tm // 8) - 1, 0), j)),
                  pl.BlockSpec((tm, tn), lambda j, i: (i, j + nj)),
                  pl.BlockSpec((3, tn), lambda j, i: (0, j)),
                  pl.BlockSpec((1, tn), lambda j, i: (0, j))],
        out_specs=(pl.BlockSpec((tm, tn), lambda j, i: (i, j)), pl.BlockSpec((tm, tn), lambda j, i: (i, j)),
                   pl.BlockSpec((1, tn), lambda j, i: (0, j)), pl.BlockSpec((8, tn), lambda j, i: (0, j))),
        compiler_params=_params("parallel", "arbitrary"),
    )(da, gu, gu, gu, cw, cb)


def _conv_bwd_b(dconv, cw):
    L, F = dconv.shape
    tm = _tile(L, ROW_TILES)
    tn = _tile(F, (512, 256, 128))
    nblk8 = L // 8
    ni = L // tm

    def body(dc_ref, nxt_ref, cw_ref, o_ref):
        i = pl.program_id(0)
        dc = dc_ref[...]
        nxt = jnp.where(i < ni - 1, nxt_ref[...], 0.0)
        sub = lax.broadcasted_iota(jnp.int32, (8, 1), 0)
        n0 = jnp.sum(jnp.where(sub == 0, nxt, 0.0), axis=0, keepdims=True)
        n1 = jnp.sum(jnp.where(sub == 1, nxt, 0.0), axis=0, keepdims=True)
        r = lax.broadcasted_iota(jnp.int32, (tm, 1), 0)
        d1 = jnp.where(r == tm - 1, n0, pltpu.roll(dc, tm - 1, 0))
        d2 = jnp.where(r == tm - 2, n0, jnp.where(r == tm - 1, n1, pltpu.roll(dc, tm - 2, 0)))
        dg = cw_ref[2:3, :] * dc + cw_ref[1:2, :] * d1 + cw_ref[0:1, :] * d2
        rows = i * tm + r
        o_ref[...] = jnp.where(rows >= PAD_LEN, dg, 0.0).astype(BF16)

    return pl.pallas_call(
        body, name="conv_bwd_b", out_shape=jax.ShapeDtypeStruct((L, F), BF16), grid=(ni, F // tn),
        in_specs=[pl.BlockSpec((tm, tn), lambda i, j: (i, j)),
                  pl.BlockSpec((8, tn), lambda i, j: (jnp.minimum((i + 1) * (tm // 8), nblk8 - 1), j)),
                  pl.BlockSpec((3, tn), lambda i, j: (0, j))],
        out_specs=pl.BlockSpec((tm, tn), lambda i, j: (i, j)),
        compiler_params=_params("parallel", "parallel"),
    )(dconv, dconv, cw)


ANY = pl.BlockSpec(memory_space=pl.ANY)


def _coords():
    return lax.axis_index("x"), lax.axis_index("y"), lax.axis_index("c")


def _flip(v, bit):
    return 1 - v if bit else v


def _gather_shards(shards):
    n = len(shards)
    chips = [(1, 0), (0, 1), (1, 1)]

    def body(*refs):
        ins, outs = refs[:n], refs[n:2 * n]
        send_sems, recv_sems, local_sems = refs[2 * n:]
        x, y, c = _coords()
        mine = 2 * x + y
        copies = []
        for w in range(n):
            local = pltpu.make_async_copy(ins[w], outs[w].at[mine], local_sems.at[w])
            local.start()
            copies.append(local)
            for j, (dx, dy) in enumerate(chips):
                px, py = _flip(x, dx), _flip(y, dy)
                rc = pltpu.make_async_remote_copy(
                    src_ref=ins[w], dst_ref=outs[w].at[mine],
                    send_sem=send_sems.at[w * 3 + j], recv_sem=recv_sems.at[w * 3 + j],
                    device_id=(px, py, c), device_id_type=MESH)
                rc.start()
        for w in range(n):
            for j, (dx, dy) in enumerate(chips):
                px, py = _flip(x, dx), _flip(y, dy)
                rc = pltpu.make_async_remote_copy(
                    src_ref=ins[w], dst_ref=outs[w].at[2 * px + py],
                    send_sem=send_sems.at[w * 3 + j], recv_sem=recv_sems.at[w * 3 + j],
                    device_id=(px, py, c), device_id_type=MESH)
                rc.wait_recv()
                rc.wait_send()
        for cp in copies:
            cp.wait()

    return pl.pallas_call(
        body, name="gather_shards",
        out_shape=tuple(jax.ShapeDtypeStruct((4,) + s.shape, s.dtype) for s in shards),
        in_specs=[ANY] * n, out_specs=tuple([ANY] * n),
        scratch_shapes=[pltpu.SemaphoreType.DMA((3 * n,)), pltpu.SemaphoreType.DMA((3 * n,)),
                        pltpu.SemaphoreType.DMA((n,))],
        compiler_params=pltpu.CompilerParams(has_side_effects=True),
    )(*shards)


PEERS = [(dx, dy, dc) for dx in (0, 1) for dy in (0, 1) for dc in (0, 1)][1:]


def _scatter_partials(parts):
    n = len(parts)

    def body(*refs):
        ins, outs = refs[:n], refs[n:2 * n]
        send_sems, recv_sems, local_sems = refs[2 * n:]
        x, y, c = _coords()
        me = 4 * x + 2 * y + c
        copies = []
        for w in range(n):
            local = pltpu.make_async_copy(ins[w].at[2 * x + y, c], outs[w].at[me], local_sems.at[w])
            local.start()
            copies.append(local)
            for j, (dx, dy, dc) in enumerate(PEERS):
                px, py, pc = _flip(x, dx), _flip(y, dy), _flip(c, dc)
                pltpu.make_async_remote_copy(
                    src_ref=ins[w].at[2 * px + py, pc], dst_ref=outs[w].at[me],
                    send_sem=send_sems.at[w * 7 + j], recv_sem=recv_sems.at[w * 7 + j],
                    device_id=(px, py, pc), device_id_type=MESH).start()
        for w in range(n):
            for j, (dx, dy, dc) in enumerate(PEERS):
                px, py, pc = _flip(x, dx), _flip(y, dy), _flip(c, dc)
                rc = pltpu.make_async_remote_copy(
                    src_ref=ins[w].at[2 * px + py, pc], dst_ref=outs[w].at[4 * px + 2 * py + pc],
                    send_sem=send_sems.at[w * 7 + j], recv_sem=recv_sems.at[w * 7 + j],
                    device_id=(px, py, pc), device_id_type=MESH)
                rc.wait_recv()
                rc.wait_send()
        for cp in copies:
            cp.wait()

    return pl.pallas_call(
        body, name="scatter_partials",
        out_shape=tuple(jax.ShapeDtypeStruct((8,) + p.shape[2:], p.dtype) for p in parts),
        in_specs=[ANY] * n, out_specs=tuple([ANY] * n),
        scratch_shapes=[pltpu.SemaphoreType.DMA((7 * n,)), pltpu.SemaphoreType.DMA((7 * n,)),
                        pltpu.SemaphoreType.DMA((n,))],
        compiler_params=pltpu.CompilerParams(has_side_effects=True),
    )(*parts)


def _sum8(recv, name):
    _, r, c = recv.shape
    tr = _tile(r, (64, 32, 16))

    def body(in_ref, o_ref):
        acc = in_ref[0].astype(F32)
        for s in range(1, 8):
            acc = acc + in_ref[s].astype(F32)
        o_ref[...] = acc

    return pl.pallas_call(
        body, name=name, out_shape=jax.ShapeDtypeStruct((r, c), F32), grid=(r // tr,),
        in_specs=[pl.BlockSpec((8, tr, c), lambda i: (0, i, 0))],
        out_specs=pl.BlockSpec((tr, c), lambda i: (i, 0)),
        compiler_params=_params("parallel"),
    )(recv)


def _swap_halves(halves):
    n = len(halves)

    def body(*refs):
        ins, outs = refs[:n], refs[n:2 * n]
        send_sems, recv_sems, local_sems = refs[2 * n:]
        x, y, c = _coords()
        copies = []
        for w in range(n):
            local = pltpu.make_async_copy(ins[w], outs[w].at[c], local_sems.at[w])
            local.start()
            copies.append(local)
            pltpu.make_async_remote_copy(
                src_ref=ins[w], dst_ref=outs[w].at[c], send_sem=send_sems.at[w], recv_sem=recv_sems.at[w],
                device_id=(x, y, 1 - c), device_id_type=MESH).start()
        for w in range(n):
            rc = pltpu.make_async_remote_copy(
                src_ref=ins[w], dst_ref=outs[w].at[1 - c], send_sem=send_sems.at[w], recv_sem=recv_sems.at[w],
                device_id=(x, y, 1 - c), device_id_type=MESH)
            rc.wait_recv()
            rc.wait_send()
        for cp in copies:
            cp.wait()

    return pl.pallas_call(
        body, name="swap_halves",
        out_shape=tuple(jax.ShapeDtypeStruct((2,) + h.shape, h.dtype) for h in halves),
        in_specs=[ANY] * n, out_specs=tuple([ANY] * n),
        scratch_shapes=[pltpu.SemaphoreType.DMA((n,)), pltpu.SemaphoreType.DMA((n,)), pltpu.SemaphoreType.DMA((n,))],
        compiler_params=pltpu.CompilerParams(has_side_effects=True),
    )(*halves)


def _allreduce_small(packed):
    R = packed.shape[0]

    def body(in_ref, o_ref, buf, send_sems, recv_sems):
        x, y, c = _coords()
        me = 4 * x + 2 * y + c
        buf[me] = in_ref[...]
        for j, (dx, dy, dc) in enumerate(PEERS):
            px, py, pc = _flip(x, dx), _flip(y, dy), _flip(c, dc)
            pltpu.make_async_remote_copy(
                src_ref=in_ref, dst_ref=buf.at[me], send_sem=send_sems.at[j], recv_sem=recv_sems.at[j],
                device_id=(px, py, pc), device_id_type=MESH).start()
        for j, (dx, dy, dc) in enumerate(PEERS):
            px, py, pc = _flip(x, dx), _flip(y, dy), _flip(c, dc)
            rc = pltpu.make_async_remote_copy(
                src_ref=in_ref, dst_ref=buf.at[4 * px + 2 * py + pc], send_sem=send_sems.at[j],
                recv_sem=recv_sems.at[j], device_id=(px, py, pc), device_id_type=MESH)
            rc.wait_recv()
            rc.wait_send()
        acc = buf[0]
        for s in range(1, 8):
            acc = acc + buf[s]
        o_ref[...] = acc

    return pl.pallas_call(
        body, name="allreduce_small", out_shape=jax.ShapeDtypeStruct((R, 128), F32),
        in_specs=[pl.BlockSpec(memory_space=pltpu.VMEM)], out_specs=pl.BlockSpec(memory_space=pltpu.VMEM),
        scratch_shapes=[pltpu.VMEM((8, R, 128), F32), pltpu.SemaphoreType.DMA((7,)), pltpu.SemaphoreType.DMA((7,))],
        compiler_params=pltpu.CompilerParams(has_side_effects=True, vmem_limit_bytes=VMEM_LIMIT_BYTES),
    )(packed)


def _adamw_math(w, g, m, v):
    m = ADAM_B1 * m + (1.0 - ADAM_B1) * g
    v = ADAM_B2 * v + (1.0 - ADAM_B2) * (g * g)
    m_hat = m / (1.0 - ADAM_B1 ** ADAM_STEP)
    v_hat = v / (1.0 - ADAM_B2 ** ADAM_STEP)
    delta = -ADAM_LR * (m_hat / (jnp.sqrt(v_hat) + ADAM_EPS) + ADAM_WD * w)
    return delta, m, v


def _adamw_big(g, w, m, v, name):
    R, C = g.shape
    tr = _tile(R, (128, 64, 32, 16, 8))

    def body(g_ref, w_ref, m_ref, v_ref, d_ref, mo_ref, vo_ref):
        d, mn, vn = _adamw_math(w_ref[...], g_ref[...], m_ref[...], v_ref[...])
        d_ref[...] = d
        mo_ref[...] = mn
        vo_ref[...] = vn

    blk = pl.BlockSpec((tr, C), lambda i: (i, 0))
    sds = jax.ShapeDtypeStruct((R, C), F32)
    return pl.pallas_call(
        body, name=name, out_shape=(sds, sds, sds), grid=(R // tr,),
        in_specs=[blk] * 4, out_specs=(blk, blk, blk), compiler_params=_params("parallel"),
    )(g, w, m, v)


def _adamw_small(items, lb_raw, dlb):
    n = len(items)
    lb_w, lb_m, lb_v = lb_raw

    def body(*refs):
        ins = refs[:4 * n]
        dlb_ref, lw_ref, lm_ref, lv_ref = refs[4 * n:4 * n + 4]
        outs = refs[4 * n + 4:]
        for t in range(n):
            g_ref, w_ref, m_ref, v_ref = ins[4 * t:4 * t + 4]
            d, mn, vn = _adamw_math(w_ref[...], g_ref[...], m_ref[...], v_ref[...])
            outs[3 * t][...] = d
            outs[3 * t + 1][...] = mn
            outs[3 * t + 2][...] = vn
        p0 = 1.0 / (1.0 + jnp.exp(lw_ref[1:2, :] - lw_ref[0:1, :]))
        g0 = dlb_ref[...] * p0 * (1.0 - p0)
        base = 3 * n
        outs[base][0:1, :] = g0
        outs[base][1:2, :] = -g0
        d, mn, vn = _adamw_math(lw_ref[...], outs[base][...], lm_ref[...], lv_ref[...])
        outs[base + 1][...] = d
        outs[base + 2][...] = mn
        outs[base + 3][...] = vn

    operands = [a for it in items for a in it] + [dlb, lb_w, lb_m, lb_v]
    out_shape = []
    for (g, w, m, v) in items:
        out_shape += [jax.ShapeDtypeStruct(w.shape, F32)] * 3
    out_shape += [jax.ShapeDtypeStruct(lb_w.shape, F32)] * 4
    vm = pl.BlockSpec(memory_space=pltpu.VMEM)
    res = pl.pallas_call(
        body, name="adamw_small", out_shape=tuple(out_shape),
        in_specs=[vm] * len(operands), out_specs=tuple([vm] * len(out_shape)),
        compiler_params=pltpu.CompilerParams(vmem_limit_bytes=VMEM_LIMIT_BYTES),
    )(*operands)
    deltas = [res[3 * t] for t in range(n)] + [res[3 * n + 1]]
    new_m = [res[3 * t + 1] for t in range(n)] + [res[3 * n + 2]]
    new_v = [res[3 * t + 2] for t in range(n)] + [res[3 * n + 3]]
    return res[3 * n], deltas, new_m, new_v


def _cols_to_shards(g):
    R, C = g.shape
    return g.reshape(2, R // 2, 4, C // 4).transpose(2, 0, 1, 3)


def _rows_to_shards(g):
    R, C = g.shape
    return g.reshape(4, 2, R // 8, C)


def kernel(x, positions, meta_tokens, w_in, w_q_up, w_kv_up, w_branch_mla, w_branch_hgrn, w_out, w_ffn_in, w_ffn_out, conv_w, conv_b, g_mix_norm, g_q_norm, g_kv_norm, g_hgrn_norm, g_ffn_norm, g_final_norm, lb_raw, loss_target, m_meta_tokens, m_w_in, m_w_q_up, m_w_kv_up, m_w_branch_mla, m_w_branch_hgrn, m_w_out, m_w_ffn_in, m_w_ffn_out, m_conv_w, m_conv_b, m_g_mix_norm, m_g_q_norm, m_g_kv_norm, m_g_hgrn_norm, m_g_ffn_norm, m_g_final_norm, m_lb_raw, v_meta_tokens, v_w_in, v_w_q_up, v_w_kv_up, v_w_branch_mla, v_w_branch_hgrn, v_w_out, v_w_ffn_in, v_w_ffn_out, v_conv_w, v_conv_b, v_g_mix_norm, v_g_q_norm, v_g_kv_norm, v_g_hgrn_norm, v_g_ffn_norm, v_g_final_norm, v_lb_raw):
    S, D = x.shape[1], x.shape[2]
    L = S + PREFIX
    QL, KVL = g_q_norm.shape[1], g_kv_norm.shape[1]
    F = conv_b.shape[1]
    heads = (4 * w_kv_up.shape[2]) // QPAD
    nh = D // HEAD
    assert lb_raw.shape[0] == 2 and g_hgrn_norm.shape[1] == HEAD and L % CHUNK == 0
    ix, iy, _ = _coords()
    chip = 2 * ix + iy

    big = [w_in, w_q_up, w_kv_up, w_branch_mla, w_branch_hgrn, w_out, w_ffn_in, w_ffn_out]
    col_sharded = [True, True, True, False, False, False, True, False]
    gathered = _gather_shards([w[0].astype(BF16) for w in big] + [meta_tokens, conv_w[0]])

    def full(gw, by_cols):
        _, r, c = gw.shape
        return gw.transpose(1, 0, 2).reshape(r, 4 * c) if by_cols else gw.reshape(4 * r, c)

    W_in, W_q, W_kv, W_a, W_b, W_o, W_fi, W_fo = [full(gw, bc) for gw, bc in zip(gathered[:8], col_sharded)]
    meta_full = full(gathered[8], True)
    cw_full = full(gathered[9], True)
    c0 = QL + KVL
    W_lat = W_in[:, :c0]
    W_kr = jnp.pad(W_in[:, c0:c0 + ROPE], ((0, 0), (0, 128 - ROPE)))
    W_H = W_in[:, c0 + ROPE:c0 + ROPE + 4 * D]
    W_G = W_in[:, c0 + ROPE + 4 * D:]
    W_qp = jnp.pad(W_q.reshape(QL, heads, QK_HEAD), ((0, 0), (0, 0), (0, QPAD - QK_HEAD))).reshape(QL, heads * QPAD)

    h0 = jnp.concatenate([jnp.zeros((PAD_LEN, D), F32), meta_full, x[0]], axis=0)
    pos = jnp.concatenate([jnp.zeros((PAD_LEN,), jnp.int32), jnp.arange(N_META, dtype=jnp.int32),
                           positions[0].astype(jnp.int32) + N_META])
    inv = 1.0 / (ROPE_THETA ** (jnp.arange(0, ROPE, 2, dtype=F32) / ROPE))
    ang = pos.astype(F32)[:, None] * inv
    zero = jnp.zeros((L, 128 - ROPE), F32)
    cos = jnp.concatenate([jnp.cos(ang), jnp.cos(ang), zero], axis=1)
    sin = jnp.concatenate([jnp.sin(ang), jnp.sin(ang), zero], axis=1)

    u1 = _rmsnorm_fwd(h0, g_mix_norm, "norm_mix")
    lat = _mm(u1, W_lat, "nn", F32, "proj_lat")
    hp = _mm(u1, W_H, "nn", F32, "proj_hgrn")
    gates = _mm(u1, W_G, "nn", F32, "proj_gates")
    kr = _mm(u1, W_kr, "nn", F32, "proj_krope")
    qn = _rmsnorm_fwd(lat, g_q_norm, "norm_q", col0=0, width=QL)
    kvn = _rmsnorm_fwd(lat, g_kv_norm, "norm_kv", col0=QL, width=KVL)
    qp = _mm(qn, W_qp, "nn", F32, "q_up")
    kv = _mm(kvn, W_kv, "nn", F32, "kv_up")
    qc, kc, vv = _rope_fwd(qp, kv, kr, cos, sin, heads)
    o_mla, lse = _attn_fwd(qc, kc, vv, heads)
    o_hgrn, o_rec, s_hist = _hgrn_fwd(hp, lb_raw, g_hgrn_norm, nh)
    br_a = _mm(o_mla, W_a, "nn", F32, "branch_mla")
    br_b = _mm(o_hgrn, W_b, "nn", F32, "branch_hgrn")
    merged = _merge_fwd(br_a, br_b, gates)
    h1 = _mm(merged, W_o, "nn", F32, "out_proj", res=h0)
    u2 = _rmsnorm_fwd(h1, g_ffn_norm, "norm_ffn")
    gu = _mm(u2, W_fi, "nn", F32, "ffn_in")
    act = _conv_fwd(gu, cw_full, conv_b)
    h2 = _mm(act, W_fo, "nn", F32, "ffn_out", res=h1)
    dh2, loss_p, dg_final = _final_loss_bwd(h2, loss_target[0], g_final_norm.reshape(1, D))

    dact = _mm(dh2, W_fo, "nt", F32, "d_act")
    dW_fo = _mm(act, dh2, "tn", F32, "dw_ffn_out")
    dconv, dup, dcb, dcw = _conv_bwd_a(dact, gu, cw_full, conv_b)
    dgate = _conv_bwd_b(dconv, cw_full)
    dgu = jnp.concatenate([dgate, dup], axis=1)
    du2 = _mm(dgu, W_fi, "nt", F32, "d_u2")
    dW_fi = _mm(u2, dgu, "tn", F32, "dw_ffn_in")
    dh1, dg_ffn = _rmsnorm_bwd(du2, h1, g_ffn_norm, "norm_ffn_bwd", F32, res=dh2)
    dmerged = _mm(dh1, W_o, "nt", F32, "d_merged")
    dW_o = _mm(merged, dh1, "tn", F32, "dw_out")
    d_a, d_b, d_gates = _merge_bwd(dmerged, br_a, br_b, gates)
    do_mla = _mm(d_a, W_a, "nt", BF16, "d_o_mla")
    dW_a = _mm(o_mla, d_a, "tn", F32, "dw_branch_mla")
    do_hgrn = _mm(d_b, W_b, "nt", F32, "d_o_hgrn")
    dW_b = _mm(o_hgrn, d_b, "tn", F32, "dw_branch_hgrn")
    dhq, dhf, dhi, dhg, dgn_p, dlb_p = _hgrn_bwd(hp, lb_raw, g_hgrn_norm, do_hgrn, o_rec, s_hist, nh)
    dqc, dkc, dvv = _attn_bwd(qc, kc, vv, o_mla, do_mla, lse, heads)
    dqp, dkv, dkr = _rope_bwd(dqc, dkc, dvv, cos, sin, heads)
    dqn = _mm(dqp, W_qp, "nt", F32, "d_qn")
    dW_qp = _mm(qn, dqp, "tn", F32, "dw_q_up")
    dkvn = _mm(dkv, W_kv, "nt", F32, "d_kvn")
    dW_kv = _mm(kvn, dkv, "tn", F32, "dw_kv_up")
    dq_lat, dg_q = _rmsnorm_bwd(dqn, lat, g_q_norm, "norm_q_bwd", BF16, col0=0)
    dkv_lat, dg_kv = _rmsnorm_bwd(dkvn, lat, g_kv_norm, "norm_kv_bwd", BF16, col0=QL)
    dlat = jnp.concatenate([dq_lat, dkv_lat], axis=1)
    dhp = jnp.concatenate([dhq, dhf, dhi, dhg], axis=1)
    du1 = _mm(dlat, W_lat, "nt", F32, "d_u1_lat")
    du1 = _mm(dhp, W_H, "nt", F32, "d_u1_hgrn", res=du1)
    du1 = _mm(d_gates, W_G, "nt", F32, "d_u1_gates", res=du1)
    du1 = _mm(dkr, W_kr, "nt", F32, "d_u1_krope", res=du1)
    dW_lat = _mm(u1, dlat, "tn", F32, "dw_in_lat")
    dW_H = _mm(u1, dhp, "tn", F32, "dw_in_hgrn")
    dW_G = _mm(u1, d_gates, "tn", F32, "dw_in_gates")
    dW_kr = _mm(u1, dkr, "tn", F32, "dw_in_krope")
    dh0, dg_mix = _rmsnorm_bwd(du1, h0, g_mix_norm, "norm_mix_bwd", F32, res=dh1)
    grad_x = dh0[PREFIX:][None]

    dW_in = jnp.concatenate([dW_lat, dW_kr[:, :ROPE], dW_H, dW_G], axis=1)
    dW_q = dW_qp.reshape(QL, heads, QPAD)[:, :, :QK_HEAD].reshape(QL, heads * QK_HEAD)
    big_grads = [dW_in, dW_q, dW_kv, dW_a, dW_b, dW_o, dW_fi, dW_fo]
    parts = [(_cols_to_shards(g) if bc else _rows_to_shards(g)).astype(BF16) for g, bc in zip(big_grads, col_sharded)]
    received = _scatter_partials(parts)
    names = ["w_in", "w_q_up", "w_kv_up", "w_branch_mla", "w_branch_hgrn", "w_out", "w_ffn_in", "w_ffn_out"]
    halves = [_sum8(r, "sum_" + nm) for r, nm in zip(received, names)]
    swapped = _swap_halves(halves)
    big_m = [m_w_in, m_w_q_up, m_w_kv_up, m_w_branch_mla, m_w_branch_hgrn, m_w_out, m_w_ffn_in, m_w_ffn_out]
    big_v = [v_w_in, v_w_q_up, v_w_kv_up, v_w_branch_mla, v_w_branch_hgrn, v_w_out, v_w_ffn_in, v_w_ffn_out]
    big_out = {}
    for nm, sw, w, m, v in zip(names, swapped, big, big_m, big_v):
        g = sw.reshape(w.shape[1], w.shape[2])
        d, mn, vn = _adamw_big(g, w[0], m[0], v[0], "adamw_" + nm)
        big_out[nm] = (g[None], d[None], mn[None], vn[None])

    pieces = [loss_p[:, :1], dg_mix, dg_q, dg_kv, jnp.sum(dgn_p[:, 0, :], axis=0, keepdims=True), dg_ffn, dg_final,
              dlb_p[:, 0, :].reshape(1, D), dcb, dcw[0:3].reshape(1, 3 * F), dh0[PAD_LEN:PREFIX].reshape(1, N_META * D)]
    sizes = [p.shape[1] for p in pieces]
    flat = jnp.concatenate(pieces, axis=1)[0]
    rows = -(-flat.shape[0] // 1024) * 8
    packed = jnp.pad(flat, (0, rows * 128 - flat.shape[0])).reshape(rows, 128)
    total = _allreduce_small(packed).reshape(-1)
    offs = [0]
    for s in sizes:
        offs.append(offs[-1] + s)
    loss, g_mix, g_q, g_kv, g_hg, g_ffn, g_fin, dlb, g_cb, g_cw, g_meta = [
        total[offs[t]:offs[t + 1]].reshape(1, sizes[t]) for t in range(len(sizes))]
    g_cw = lax.dynamic_slice_in_dim(g_cw.reshape(3, F), chip * (F // 4), F // 4, axis=1)
    g_meta = lax.dynamic_slice_in_dim(g_meta.reshape(N_META, D), chip * (D // 4), D // 4, axis=1)
    items = [(g_meta, meta_tokens, m_meta_tokens, v_meta_tokens),
             (g_cw, conv_w[0], m_conv_w[0], v_conv_w[0]),
             (g_cb, conv_b, m_conv_b, v_conv_b),
             (g_mix, g_mix_norm, m_g_mix_norm, v_g_mix_norm),
             (g_q, g_q_norm, m_g_q_norm, v_g_q_norm),
             (g_kv, g_kv_norm, m_g_kv_norm, v_g_kv_norm),
             (g_hg, g_hgrn_norm, m_g_hgrn_norm, v_g_hgrn_norm),
             (g_ffn, g_ffn_norm, m_g_ffn_norm, v_g_ffn_norm),
             (g_fin, g_final_norm.reshape(1, D), m_g_final_norm.reshape(1, D), v_g_final_norm.reshape(1, D))]
    g_lb, s_delta, s_m, s_v = _adamw_small(items, (lb_raw, m_lb_raw, v_lb_raw), dlb)
    s_grads = [it[0] for it in items] + [g_lb]

    def shape_small(vals):
        meta, cw, cb, mix, q, kvg, hg, ffn, fin, lb = vals
        return [meta, cw[None], cb, mix, q, kvg, hg, ffn, fin.reshape(D), lb]

    s_grads, s_delta, s_m, s_v = [shape_small(v) for v in (s_grads, s_delta, s_m, s_v)]

    def ordered(kind, small):
        bigs = [big_out[nm][kind] for nm in names]
        return [small[0]] + bigs + small[1:]

    return (loss.reshape(()), grad_x, *ordered(0, s_grads), *ordered(1, s_delta), *ordered(2, s_m), *ordered(3, s_v))
```

```python
import math

import jax
import jax.numpy as jnp
from jax import lax
from jax.experimental import pallas as pl
from jax.experimental.pallas import tpu as pltpu

F32 = jnp.float32
BF16 = jnp.bfloat16
MESH = pl.DeviceIdType.MESH

NORM_EPS = 1e-6
N_META = 16
PREFIX = 128
PAD_LEN = PREFIX - N_META
HEAD = 128
ROPE = 64
QK_HEAD = HEAD + ROPE
QPAD = 2 * HEAD
SOFTMAX_SCALE = QK_HEAD ** -0.5
ROPE_THETA = 10000.0
CHUNK = 128
LEVELS = 7
NEG = -1e30

ADAM_LR = 0.001
ADAM_B1 = 0.9
ADAM_B2 = 0.999
ADAM_EPS = 1e-08
ADAM_WD = 0.01
ADAM_STEP = 10

VMEM_LIMIT_BYTES = 48 * 1024 * 1024


def _params(*sem):
    return pltpu.CompilerParams(dimension_semantics=sem, vmem_limit_bytes=VMEM_LIMIT_BYTES)


def _tile(n, prefs):
    for p in prefs:
        if n % p == 0:
            return p
    return n


ROW_TILES = (640, 512, 256, 128, 64, 32, 16, 8)
TN_ROW_TILES = (1024, 1408, 1536, 512, 256, 128)
COL_TILES = (1024, 512, 256, 128)
K_TILES = (2048, 1536, 1408, 1024, 512, 256, 128)


def _sigmoid(x):
    return 1.0 / (1.0 + jnp.exp(-x))


def _dot(a, b, dims):
    return lax.dot_general(a.astype(BF16), b.astype(BF16), (dims, ((), ())), preferred_element_type=F32)


NN = ((1,), (0,))
TN = ((0,), (0,))
NT = ((1,), (1,))


def _split3(x):
    hi = x.astype(BF16)
    r = x - hi.astype(F32)
    mid = r.astype(BF16)
    lo = (r - mid.astype(F32)).astype(BF16)
    return hi, mid, lo


def _dot_exact_rhs(sel, x, dims):
    hi, mid, lo = _split3(x)
    return _dot(sel, hi, dims) + _dot(sel, mid, dims) + _dot(sel, lo, dims)


def _mm(a, b, mode, out_dtype, name, res=None):
    if mode == "nn":
        (M, K), (K2, N) = a.shape, b.shape
    elif mode == "tn":
        (K, M), (K2, N) = a.shape, b.shape
    else:
        (M, K), (N, K2) = a.shape, b.shape
    assert K == K2, (name, a.shape, b.shape)
    tm = _tile(M, TN_ROW_TILES if mode == "tn" else ROW_TILES)
    tn = _tile(N, COL_TILES)
    tk = _tile(K, ROW_TILES if mode == "tn" else K_TILES)
    nk = K // tk
    dims = {"nn": NN, "tn": TN, "nt": NT}[mode]

    def body(*refs):
        a_ref, b_ref = refs[:2]
        r_ref = None if res is None else refs[2]
        o_ref = refs[2 if res is None else 3]

        def finish(r):
            if r_ref is not None:
                r = r + r_ref[...].astype(F32)
            o_ref[...] = r.astype(out_dtype)

        if nk == 1:
            finish(_dot(a_ref[...], b_ref[...], dims))
            return
        acc = refs[-1]
        k = pl.program_id(2)

        @pl.when(k == 0)
        def _():
            acc[...] = jnp.zeros_like(acc)

        acc[...] += _dot(a_ref[...], b_ref[...], dims)

        @pl.when(k == nk - 1)
        def _():
            finish(acc[...])

    if mode == "nn":
        a_spec = pl.BlockSpec((tm, tk), lambda i, j, k: (i, k))
        b_spec = pl.BlockSpec((tk, tn), lambda i, j, k: (k, j))
    elif mode == "tn":
        a_spec = pl.BlockSpec((tk, tm), lambda i, j, k: (k, i))
        b_spec = pl.BlockSpec((tk, tn), lambda i, j, k: (k, j))
    else:
        a_spec = pl.BlockSpec((tm, tk), lambda i, j, k: (i, k))
        b_spec = pl.BlockSpec((tn, tk), lambda i, j, k: (j, k))
    in_specs = [a_spec, b_spec]
    operands = [a, b]
    if res is not None:
        in_specs.append(pl.BlockSpec((tm, tn), lambda i, j, k: (i, j)))
        operands.append(res)
    return pl.pallas_call(
        body, name=name,
        out_shape=jax.ShapeDtypeStruct((M, N), out_dtype),
        grid=(M // tm, N // tn, nk),
        in_specs=in_specs,
        out_specs=pl.BlockSpec((tm, tn), lambda i, j, k: (i, j)),
        scratch_shapes=[] if nk == 1 else [pltpu.VMEM((tm, tn), F32)],
        compiler_params=_params("parallel", "parallel", "arbitrary"),
    )(*operands)


def _rmsnorm_fwd(x, g, name, col0=0, width=None):
    L = x.shape[0]
    width = x.shape[1] if width is None else width
    assert col0 % width == 0
    cb = col0 // width
    tm = _tile(L, (128, 64, 32, 16))

    def body(x_ref, g_ref, o_ref):
        xv = x_ref[...]
        r = lax.rsqrt(jnp.mean(xv * xv, axis=-1, keepdims=True) + NORM_EPS)
        o_ref[...] = ((xv * r) * g_ref[...]).astype(BF16)

    return pl.pallas_call(
        body, name=name,
        out_shape=jax.ShapeDtypeStruct((L, width), BF16),
        grid=(L // tm,),
        in_specs=[pl.BlockSpec((tm, width), lambda i: (i, cb)), pl.BlockSpec((1, width), lambda i: (0, 0))],
        out_specs=pl.BlockSpec((tm, width), lambda i: (i, 0)),
        compiler_params=_params("parallel"),
    )(x, g)


def _rmsnorm_bwd(dy, x, g, name, out_dtype, col0=0, res=None):
    L, width = dy.shape
    assert col0 % width == 0
    cb = col0 // width
    tm = _tile(L, (128, 64, 32, 16))

    def body(*refs):
        if res is None:
            dy_ref, x_ref, g_ref, dx_ref, dg_ref = refs
            r_ref = None
        else:
            dy_ref, x_ref, g_ref, r_ref, dx_ref, dg_ref = refs

        @pl.when(pl.program_id(0) == 0)
        def _():
            dg_ref[...] = jnp.zeros_like(dg_ref)

        xv = x_ref[...]
        dyv = dy_ref[...].astype(F32)
        r = lax.rsqrt(jnp.mean(xv * xv, axis=-1, keepdims=True) + NORM_EPS)
        z = dyv * g_ref[...]
        dx = r * z - xv * ((r * r * r) * jnp.mean(z * xv, axis=-1, keepdims=True))
        if r_ref is not None:
            dx = dx + r_ref[...]
        dx_ref[...] = dx.astype(out_dtype)
        dg_ref[...] += jnp.sum(dyv * (xv * r), axis=0, keepdims=True)

    in_specs = [pl.BlockSpec((tm, width), lambda i: (i, 0)),
                pl.BlockSpec((tm, width), lambda i: (i, cb)),
                pl.BlockSpec((1, width), lambda i: (0, 0))]
    operands = [dy, x, g]
    if res is not None:
        in_specs.append(pl.BlockSpec((tm, width), lambda i: (i, 0)))
        operands.append(res)
    return pl.pallas_call(
        body, name=name,
        out_shape=(jax.ShapeDtypeStruct((L, width), out_dtype), jax.ShapeDtypeStruct((1, width), F32)),
        grid=(L // tm,),
        in_specs=in_specs,
        out_specs=(pl.BlockSpec((tm, width), lambda i: (i, 0)), pl.BlockSpec((1, width), lambda i: (0, 0))),
        compiler_params=_params("arbitrary"),
    )(*operands)


def _final_loss_bwd(h2, tgt, g):
    L, D = h2.shape
    tm = PREFIX
    inv_d = 1.0 / D

    def body(h_ref, t_ref, g_ref, dh_ref, loss_ref, dg_ref):
        i = pl.program_id(0)

        @pl.when(i == 0)
        def _():
            loss_ref[...] = jnp.zeros_like(loss_ref)
            dg_ref[...] = jnp.zeros_like(dg_ref)

        xv = h_ref[...]
        r = lax.rsqrt(jnp.mean(xv * xv, axis=-1, keepdims=True) + NORM_EPS)
        xn = xv * r
        y = xn * g_ref[...]
        real = (i >= PREFIX // tm).astype(F32)
        diff = (y - t_ref[...]) * real
        loss_ref[...] += 0.5 * inv_d * jnp.sum(diff * diff)
        dyv = diff * inv_d
        z = dyv * g_ref[...]
        dh_ref[...] = r * z - xv * ((r * r * r) * jnp.mean(z * xv, axis=-1, keepdims=True))
        dg_ref[...] += jnp.sum(dyv * xn, axis=0, keepdims=True)

    shift = PREFIX // tm
    return pl.pallas_call(
        body, name="final_loss_bwd",
        out_shape=(jax.ShapeDtypeStruct((L, D), F32), jax.ShapeDtypeStruct((1, 128), F32),
                   jax.ShapeDtypeStruct((1, D), F32)),
        grid=(L // tm,),
        in_specs=[pl.BlockSpec((tm, D), lambda i: (i, 0)),
                  pl.BlockSpec((tm, D), lambda i: (jnp.maximum(i - shift, 0), 0)),
                  pl.BlockSpec((1, D), lambda i: (0, 0))],
        out_specs=(pl.BlockSpec((tm, D), lambda i: (i, 0)), pl.BlockSpec((1, 128), lambda i: (0, 0)),
                   pl.BlockSpec((1, D), lambda i: (0, 0))),
        compiler_params=_params("arbitrary"),
    )(h2, tgt, g)


def _rot_half(x):
    lane = lax.broadcasted_iota(jnp.int32, x.shape, 1)
    return jnp.where(lane < ROPE // 2, -pltpu.roll(x, 128 - ROPE // 2, 1), pltpu.roll(x, ROPE // 2, 1))


def _rope_fwd(qp, kv, kr, cos, sin, heads):
    L = qp.shape[0]
    tm = _tile(L, (128,))

    def body(q_ref, kv_ref, kr_ref, c_ref, s_ref, qc_ref, kc_ref, v_ref):
        c, s = c_ref[...], s_ref[...]
        krv = kr_ref[...]
        kr_rot = (krv * c + _rot_half(krv) * s).astype(BF16)
        for h in range(heads):
            lo = h * QPAD
            qc_ref[:, lo:lo + HEAD] = (q_ref[:, lo:lo + HEAD] * SOFTMAX_SCALE).astype(BF16)
            qr = q_ref[:, lo + HEAD:lo + QPAD]
            qc_ref[:, lo + HEAD:lo + QPAD] = ((qr * c + _rot_half(qr) * s) * SOFTMAX_SCALE).astype(BF16)
            kc_ref[:, lo:lo + HEAD] = kv_ref[:, lo:lo + HEAD].astype(BF16)
            kc_ref[:, lo + HEAD:lo + QPAD] = kr_rot
            v_ref[:, h * HEAD:(h + 1) * HEAD] = kv_ref[:, lo + HEAD:lo + QPAD].astype(BF16)

    W = heads * QPAD
    row = lambda w: pl.BlockSpec((tm, w), lambda i: (i, 0))
    return pl.pallas_call(
        body, name="rope_fwd",
        out_shape=(jax.ShapeDtypeStruct((L, W), BF16), jax.ShapeDtypeStruct((L, W), BF16),
                   jax.ShapeDtypeStruct((L, heads * HEAD), BF16)),
        grid=(L // tm,),
        in_specs=[row(W), row(W), row(128), row(128), row(128)],
        out_specs=(row(W), row(W), row(heads * HEAD)),
        compiler_params=_params("parallel"),
    )(qp, kv, kr, cos, sin)


def _rope_bwd(dqc, dkc, dv, cos, sin, heads):
    L = dqc.shape[0]
    tm = _tile(L, (128,))

    def body(dq_ref, dk_ref, dv_ref, c_ref, s_ref, dqp_ref, dkv_ref, dkr_ref):
        c, s = c_ref[...], s_ref[...]
        acc = jnp.zeros((tm, 128), F32)
        for h in range(heads):
            lo = h * QPAD
            dqp_ref[:, lo:lo + HEAD] = (dq_ref[:, lo:lo + HEAD] * SOFTMAX_SCALE).astype(BF16)
            d = dq_ref[:, lo + HEAD:lo + QPAD]
            dqp_ref[:, lo + HEAD:lo + QPAD] = ((d * c - _rot_half(d) * s) * SOFTMAX_SCALE).astype(BF16)
            dkv_ref[:, lo:lo + HEAD] = dk_ref[:, lo:lo + HEAD].astype(BF16)
            dkv_ref[:, lo + HEAD:lo + QPAD] = dv_ref[:, h * HEAD:(h + 1) * HEAD].astype(BF16)
            acc = acc + dk_ref[:, lo + HEAD:lo + QPAD]
        dkr_ref[...] = (acc * c - _rot_half(acc) * s).astype(BF16)

    W = heads * QPAD
    row = lambda w: pl.BlockSpec((tm, w), lambda i: (i, 0))
    return pl.pallas_call(
        body, name="rope_bwd",
        out_shape=(jax.ShapeDtypeStruct((L, W), BF16), jax.ShapeDtypeStruct((L, W), BF16),
                   jax.ShapeDtypeStruct((L, 128), BF16)),
        grid=(L // tm,),
        in_specs=[row(W), row(W), row(heads * HEAD), row(128), row(128)],
        out_specs=(row(W), row(W), row(128)),
        compiler_params=_params("parallel"),
    )(dqc, dkc, dv, cos, sin)


def _attn_keep(qi, ki, ta):
    t = qi * ta + lax.broadcasted_iota(jnp.int32, (ta, ta), 0)
    s = ki * ta + lax.broadcasted_iota(jnp.int32, (ta, ta), 1)
    return (s <= t) & ((s >= PAD_LEN) | (s == t))


def _attn_fwd(qc, kc, v, heads):
    L = qc.shape[0]
    ta = _tile(L, (640, 128))
    nb = L // ta

    def body(q_ref, k_ref, v_ref, o_ref, lse_ref, m_sc, l_sc, acc_sc):
        qi, ki = pl.program_id(1), pl.program_id(2)

        @pl.when(ki == 0)
        def _():
            m_sc[...] = jnp.full_like(m_sc, NEG)
            l_sc[...] = jnp.zeros_like(l_sc)
            acc_sc[...] = jnp.zeros_like(acc_sc)

        def step(masked):
            s = _dot(q_ref[...], k_ref[...], NT)
            if masked:
                s = jnp.where(_attn_keep(qi, ki, ta), s, NEG)
            m_old = m_sc[...]
            m_new = jnp.maximum(m_old, jnp.max(s, axis=-1, keepdims=True))
            p = jnp.exp(s - m_new)
            alpha = jnp.exp(m_old - m_new)
            l_sc[...] = alpha * l_sc[...] + jnp.sum(p, axis=-1, keepdims=True)
            acc_sc[...] = alpha * acc_sc[...] + _dot(p, v_ref[...], NN)
            m_sc[...] = m_new

        pl.when((ki == qi) | ((ki == 0) & (qi > 0)))(lambda: step(True))
        pl.when((ki > 0) & (ki < qi))(lambda: step(False))

        @pl.when(ki == qi)
        def _():
            l = l_sc[...]
            o_ref[...] = (acc_sc[...] / l).astype(BF16)
            lse_ref[...] = jnp.broadcast_to(m_sc[...] + jnp.log(l), (ta, HEAD))

    return pl.pallas_call(
        body, name="attn_fwd",
        out_shape=(jax.ShapeDtypeStruct((L, heads * HEAD), BF16), jax.ShapeDtypeStruct((L, heads * HEAD), F32)),
        grid=(heads, nb, nb),
        in_specs=[pl.BlockSpec((ta, QPAD), lambda h, i, j: (i, h)),
                  pl.BlockSpec((ta, QPAD), lambda h, i, j: (jnp.minimum(j, i), h)),
                  pl.BlockSpec((ta, HEAD), lambda h, i, j: (jnp.minimum(j, i), h))],
        out_specs=(pl.BlockSpec((ta, HEAD), lambda h, i, j: (i, h)),
                   pl.BlockSpec((ta, HEAD), lambda h, i, j: (i, h))),
        scratch_shapes=[pltpu.VMEM((ta, 1), F32), pltpu.VMEM((ta, 1), F32), pltpu.VMEM((ta, HEAD), F32)],
        compiler_params=_params("parallel", "parallel", "arbitrary"),
    )(qc, kc, v)


def _attn_bwd(qc, kc, v, o, do, lse, heads):
    L = qc.shape[0]
    ta = _tile(L, (640, 128))
    nb = L // ta

    def body(q_ref, k_ref, v_ref, o_ref, do_ref, lse_ref, dq_ref, dk_ref, dv_ref):
        kj, qi = pl.program_id(1), pl.program_id(2)

        @pl.when((kj == 0) & (qi == 0))
        def _():
            dq_ref[...] = jnp.zeros_like(dq_ref)

        @pl.when(qi == 0)
        def _():
            dk_ref[...] = jnp.zeros_like(dk_ref)
            dv_ref[...] = jnp.zeros_like(dv_ref)

        def step(masked):
            q, k, vv, dov = q_ref[...], k_ref[...], v_ref[...], do_ref[...]
            p = jnp.exp(_dot(q, k, NT) - lse_ref[:, 0:1])
            if masked:
                p = jnp.where(_attn_keep(qi, kj, ta), p, 0.0)
            delta = jnp.sum(dov.astype(F32) * o_ref[...].astype(F32), axis=-1, keepdims=True)
            ds = p * (_dot(dov, vv, NT) - delta)
            dv_ref[...] += _dot(p, dov, TN)
            dk_ref[...] += _dot(ds, q, TN)
            rows = pl.ds(pl.multiple_of(qi * ta, ta), ta)
            dq_ref[rows, :] += _dot(ds, k, NN)

        pl.when((qi == kj) | ((kj == 0) & (qi > 0)))(lambda: step(True))
        pl.when((kj > 0) & (qi > kj))(lambda: step(False))

    qrow = lambda w: pl.BlockSpec((ta, w), lambda h, j, i: (jnp.maximum(i, j), h))
    krow = lambda w: pl.BlockSpec((ta, w), lambda h, j, i: (j, h))
    return pl.pallas_call(
        body, name="attn_bwd",
        out_shape=(jax.ShapeDtypeStruct((L, heads * QPAD), F32), jax.ShapeDtypeStruct((L, heads * QPAD), F32),
                   jax.ShapeDtypeStruct((L, heads * HEAD), F32)),
        grid=(heads, nb, nb),
        in_specs=[qrow(QPAD), krow(QPAD), krow(HEAD), qrow(HEAD), qrow(HEAD), qrow(HEAD)],
        out_specs=(pl.BlockSpec((L, QPAD), lambda h, j, i: (0, h)), krow(QPAD), krow(HEAD)),
        compiler_params=_params("parallel", "arbitrary", "arbitrary"),
    )(qc, kc, v, o, do, lse)


def _hgrn_constants():
    t = jnp.arange(CHUNK)
    tril = (t[None, :] <= t[:, None]).astype(BF16)
    sel = []
    for lv in range(LEVELS):
        hs = 1 << lv
        mid = (t // (2 * hs)) * (2 * hs) + hs - 1
        sel.append((t[None, :] == mid[:, None]).astype(BF16))
    return tril, jnp.concatenate(sel, axis=0)


def _hgrn_decay_grad_constants():
    r = jnp.arange(CHUNK)[:, None]
    c = jnp.arange(CHUNK)[None, :]
    mats = []
    for lv in range(LEVELS):
        same = (r >> (lv + 1)) == (c >> (lv + 1))
        second = ((r >> lv) & 1) == 1
        mats.append(same & jnp.where(second, c >= r, c < r))
    mats += [c >= r, c < r]
    return jnp.concatenate(mats, axis=1).astype(BF16)


def _hgrn_gates(hq_ref, hf_ref, hi_ref, lb_ref, row0):
    rows = row0 + lax.broadcasted_iota(jnp.int32, (CHUNK, 1), 0)
    valid = rows >= PAD_LEN
    lb = 1.0 / (1.0 + jnp.exp(lb_ref[1:2, :] - lb_ref[0:1, :]))
    hq = hq_ref[...]
    sq = _sigmoid(hq)
    sg = _sigmoid(hf_ref[...])
    f = lb + (1.0 - lb) * sg
    g = jnp.where(valid, jnp.log(f), 0.0)
    k = jnp.where(valid, 1.0 - f, 0.0)
    return dict(q=hq * sq, sq=sq, hq=hq, k=k, v=hi_ref[...], g=g, f=f, sg=sg, lb=lb, valid=valid)


def _hgrn_levels(q, k, b, sel_ref):
    t = lax.broadcasted_iota(jnp.int32, (CHUNK, 1), 0)
    tt = lax.broadcasted_iota(jnp.int32, (CHUNK, CHUNK), 0)
    ss = lax.broadcasted_iota(jnp.int32, (CHUNK, CHUNK), 1)
    bm_all = _dot_exact_rhs(sel_ref[...], b, NN)
    out = []
    for lv in range(LEVELS):
        bm = bm_all[lv * CHUNK:(lv + 1) * CHUNK, :]
        second = ((t >> lv) & 1) == 1
        eq = jnp.where(second, jnp.exp(jnp.minimum(b - bm, 0.0)), 0.0)
        ek = jnp.where(second, 0.0, jnp.exp(jnp.minimum(bm - b, 0.0)))
        same = (tt >> (lv + 1)) == (ss >> (lv + 1))
        out.append((eq, ek, (q * eq).astype(BF16), (k * ek).astype(BF16), same))
    return out


def _hgrn_intra(q, k, levels):
    tt = lax.broadcasted_iota(jnp.int32, (CHUNK, CHUNK), 0)
    ss = lax.broadcasted_iota(jnp.int32, (CHUNK, CHUNK), 1)
    p = jnp.where(tt == ss, jnp.sum(q * k, axis=-1, keepdims=True), 0.0)
    for (_, _, ql, kl, same) in levels:
        p = p + jnp.where(same, _dot(ql, kl, NT), 0.0)
    return p


def _hgrn_fwd(hp, lb_raw, g_norm, nh):
    L = hp.shape[0]
    D = nh * HEAD
    nc = L // CHUNK
    tril, sel = _hgrn_constants()

    def body(hq_ref, hf_ref, hi_ref, hg_ref, lb_ref, gn_ref, tril_ref, sel_ref,
             oh_ref, orec_ref, shist_ref, s_sc, b_sc):
        c = pl.program_id(1)

        @pl.when(c == 0)
        def _():
            s_sc[...] = jnp.zeros_like(s_sc)

        w = _hgrn_gates(hq_ref, hf_ref, hi_ref, lb_ref, c * CHUNK)
        q, k, v = w["q"], w["k"], w["v"]
        b = _dot_exact_rhs(tril_ref[...], w["g"], NN)
        b_sc[...] = b
        b_last = b_sc[CHUNK - 1:CHUNK, :]
        p = _hgrn_intra(q, k, _hgrn_levels(q, k, b, sel_ref))
        s_in = s_sc[...]
        shist_ref[0, 0] = s_in
        o = _dot(p, v, NN) + _dot(q * jnp.exp(b), s_in, NT)
        s_sc[...] = jnp.exp(b_last) * s_in + _dot(v, k * jnp.exp(b_last - b), TN)
        orec_ref[...] = o
        rn = lax.rsqrt(jnp.mean(o * o, axis=-1, keepdims=True) + NORM_EPS)
        hg = hg_ref[...]
        oh_ref[...] = (((o * rn) * gn_ref[...]) * (hg * _sigmoid(hg))).astype(BF16)

    col = lambda grp: pl.BlockSpec((CHUNK, HEAD), lambda h, c: (c, grp * nh + h))
    const = lambda shape: pl.BlockSpec(shape, lambda h, c: (0, 0))
    return pl.pallas_call(
        body, name="hgrn_fwd",
        out_shape=(jax.ShapeDtypeStruct((L, D), BF16), jax.ShapeDtypeStruct((L, D), F32),
                   jax.ShapeDtypeStruct((nc, nh, HEAD, HEAD), F32)),
        grid=(nh, nc),
        in_specs=[col(0), col(1), col(2), col(3),
                  pl.BlockSpec((2, HEAD), lambda h, c: (0, h)), const((1, HEAD)),
                  const((CHUNK, CHUNK)), const((LEVELS * CHUNK, CHUNK))],
        out_specs=(pl.BlockSpec((CHUNK, HEAD), lambda h, c: (c, h)),
                   pl.BlockSpec((CHUNK, HEAD), lambda h, c: (c, h)),
                   pl.BlockSpec((1, 1, HEAD, HEAD), lambda h, c: (c, h, 0, 0))),
        scratch_shapes=[pltpu.VMEM((HEAD, HEAD), F32), pltpu.VMEM((CHUNK, HEAD), F32)],
        compiler_params=_params("parallel", "arbitrary"),
    )(hp, hp, hp, hp, lb_raw, g_norm, tril, sel)


def _hgrn_bwd(hp, lb_raw, g_norm, do_h, o_rec, s_hist, nh):
    L = hp.shape[0]
    D = nh * HEAD
    nc = L // CHUNK
    tril, sel = _hgrn_constants()
    tdec = _hgrn_decay_grad_constants()

    def body(hq_ref, hf_ref, hi_ref, hg_ref, lb_ref, gn_ref, tril_ref, sel_ref, tdec_ref, do_ref, orec_ref, shist_ref,
             dhq_ref, dhf_ref, dhi_ref, dhg_ref, dgn_ref, dlb_ref, ds_sc, b_sc):
        ci = pl.program_id(1)
        c = nc - 1 - ci

        @pl.when(ci == 0)
        def _():
            ds_sc[...] = jnp.zeros_like(ds_sc)
            dgn_ref[...] = jnp.zeros_like(dgn_ref)
            dlb_ref[...] = jnp.zeros_like(dlb_ref)

        w = _hgrn_gates(hq_ref, hf_ref, hi_ref, lb_ref, c * CHUNK)
        q, k, v, valid = w["q"], w["k"], w["v"], w["valid"]
        b = _dot_exact_rhs(tril_ref[...], w["g"], NN)
        b_sc[...] = b
        b_last = b_sc[CHUNK - 1:CHUNK, :]
        levels = _hgrn_levels(q, k, b, sel_ref)
        p = _hgrn_intra(q, k, levels)
        s_in = shist_ref[0, 0]
        ds_out = ds_sc[...]
        eb = jnp.exp(b)
        etail = jnp.exp(b_last - b)
        decay = jnp.exp(b_last)

        o = orec_ref[...]
        hg = hg_ref[...]
        sgg = _sigmoid(hg)
        gate = hg * sgg
        rn = lax.rsqrt(jnp.mean(o * o, axis=-1, keepdims=True) + NORM_EPS)
        on = o * rn
        doh = do_ref[...]
        dy = doh * gate
        dhg_ref[...] = (doh * (on * gn_ref[...]) * (sgg * (1.0 + hg * (1.0 - sgg)))).astype(BF16)
        dgn_ref[0] += jnp.broadcast_to(jnp.sum(dy * on, axis=0, keepdims=True), (8, HEAD))
        z = dy * gn_ref[...]
        do = rn * z - o * ((rn * rn * rn) * jnp.mean(z * o, axis=-1, keepdims=True))

        tt = lax.broadcasted_iota(jnp.int32, (CHUNK, CHUNK), 0)
        ss = lax.broadcasted_iota(jnp.int32, (CHUNK, CHUNK), 1)
        dp = jnp.where(ss <= tt, _dot(do, v, NT), 0.0)
        dv = _dot(p, do, TN) + _dot(k * etail, ds_out, NT)
        dpd = jnp.sum(jnp.where(tt == ss, dp, 0.0), axis=-1, keepdims=True)
        dq_state = eb * _dot(do, s_in, NN)
        dk_state = etail * _dot(v, ds_out, NN)
        dq = dpd * k + dq_state
        dk = dpd * q + dk_state
        pair_terms = []
        for (eq, ek, ql, kl, same) in levels:
            dpl = jnp.where(same, dp, 0.0)
            dq_l = eq * _dot(dpl, kl, NN)
            dk_l = ek * _dot(dpl, ql, TN)
            dq = dq + dq_l
            dk = dk + dk_l
            pair_terms.append(q * dq_l + k * dk_l)
        pair_terms += [q * dq_state, k * dk_state]
        ds_sc[...] = decay * ds_out + _dot(do, q * eb, TN)
        through = jnp.sum((decay * s_in) * ds_out, axis=0, keepdims=True)
        dg = _dot_exact_rhs(tdec_ref[...], jnp.concatenate(pair_terms, axis=0), NN) + through

        f, sg, lb, sq, hq = w["f"], w["sg"], w["lb"], w["sq"], w["hq"]
        df = jnp.where(valid, dg / f - dk, 0.0)
        dhf_ref[...] = (df * (1.0 - lb) * sg * (1.0 - sg)).astype(BF16)
        dlb_ref[0] += jnp.broadcast_to(jnp.sum(df * (1.0 - sg), axis=0, keepdims=True), (8, HEAD))
        dhq_ref[...] = (dq * (sq * (1.0 + hq * (1.0 - sq)))).astype(BF16)
        dhi_ref[...] = dv.astype(BF16)

    col = lambda grp: pl.BlockSpec((CHUNK, HEAD), lambda h, c: (nc - 1 - c, grp * nh + h))
    const = lambda shape: pl.BlockSpec(shape, lambda h, c: (0, 0))
    tile = pl.BlockSpec((CHUNK, HEAD), lambda h, c: (nc - 1 - c, h))
    part = pl.BlockSpec((1, 8, HEAD), lambda h, c: (h, 0, 0))
    return pl.pallas_call(
        body, name="hgrn_bwd",
        out_shape=tuple([jax.ShapeDtypeStruct((L, D), BF16)] * 4 + [jax.ShapeDtypeStruct((nh, 8, HEAD), F32)] * 2),
        grid=(nh, nc),
        in_specs=[col(0), col(1), col(2), col(3),
                  pl.BlockSpec((2, HEAD), lambda h, c: (0, h)), const((1, HEAD)),
                  const((CHUNK, CHUNK)), const((LEVELS * CHUNK, CHUNK)), const((CHUNK, (LEVELS + 2) * CHUNK)),
                  tile, tile, pl.BlockSpec((1, 1, HEAD, HEAD), lambda h, c: (nc - 1 - c, h, 0, 0))],
        out_specs=(tile, tile, tile, tile, part, part),
        scratch_shapes=[pltpu.VMEM((HEAD, HEAD), F32), pltpu.VMEM((CHUNK, HEAD), F32)],
        compiler_params=_params("parallel", "arbitrary"),
    )(hp, hp, hp, hp, lb_raw, g_norm, tril, sel, tdec, do_h, o_rec, s_hist)


def _merge_fwd(a, bm, gates):
    L, D = a.shape
    tm = _tile(L, (128,))

    def body(a_ref, b_ref, g_ref, o_ref):
        o_ref[...] = (_sigmoid(g_ref[:, :D]) * a_ref[...] + _sigmoid(g_ref[:, D:]) * b_ref[...]).astype(BF16)

    row = lambda w: pl.BlockSpec((tm, w), lambda i: (i, 0))
    return pl.pallas_call(
        body, name="merge_fwd", out_shape=jax.ShapeDtypeStruct((L, D), BF16), grid=(L // tm,),
        in_specs=[row(D), row(D), row(2 * D)], out_specs=row(D), compiler_params=_params("parallel"),
    )(a, bm, gates)


def _merge_bwd(dm, a, bm, gates):
    L, D = a.shape
    tm = _tile(L, (128,))

    def body(dm_ref, a_ref, b_ref, g_ref, da_ref, db_ref, dg_ref):
        d = dm_ref[...]
        sa, sb = _sigmoid(g_ref[:, :D]), _sigmoid(g_ref[:, D:])
        da_ref[...] = (d * sa).astype(BF16)
        db_ref[...] = (d * sb).astype(BF16)
        dg_ref[:, :D] = (d * a_ref[...] * sa * (1.0 - sa)).astype(BF16)
        dg_ref[:, D:] = (d * b_ref[...] * sb * (1.0 - sb)).astype(BF16)

    row = lambda w: pl.BlockSpec((tm, w), lambda i: (i, 0))
    return pl.pallas_call(
        body, name="merge_bwd",
        out_shape=(jax.ShapeDtypeStruct((L, D), BF16), jax.ShapeDtypeStruct((L, D), BF16),
                   jax.ShapeDtypeStruct((L, 2 * D), BF16)),
        grid=(L // tm,),
        in_specs=[row(D), row(D), row(D), row(2 * D)], out_specs=(row(D), row(D), row(2 * D)),
        compiler_params=_params("parallel"),
    )(dm, a, bm, gates)


def _conv_taps(g_ref, halo_ref, i, tm):
    rows = i * tm + lax.broadcasted_iota(jnp.int32, (tm, 1), 0)
    g0 = jnp.where(rows >= PAD_LEN, g_ref[...], 0.0)
    hrow = i * tm - 8 + lax.broadcasted_iota(jnp.int32, (8, 1), 0)
    halo = jnp.where(hrow >= PAD_LEN, halo_ref[...], 0.0)
    r = lax.broadcasted_iota(jnp.int32, (tm, 1), 0)
    h7 = jnp.sum(jnp.where(lax.broadcasted_iota(jnp.int32, (8, 1), 0) == 7, halo, 0.0), axis=0, keepdims=True)
    h6 = jnp.sum(jnp.where(lax.broadcasted_iota(jnp.int32, (8, 1), 0) == 6, halo, 0.0), axis=0, keepdims=True)
    g1 = jnp.where(r == 0, h7, pltpu.roll(g0, 1, 0))
    g2 = jnp.where(r == 0, h6, jnp.where(r == 1, h7, pltpu.roll(g0, 2, 0)))
    return g0, g1, g2


def _conv_fwd(gu, cw, cb):
    L, F2 = gu.shape
    F = F2 // 2
    tm = _tile(L, ROW_TILES)
    tn = _tile(F, (512, 256, 128))
    nj = F // tn

    def body(g_ref, halo_ref, u_ref, cw_ref, cb_ref, o_ref):
        g0, g1, g2 = _conv_taps(g_ref, halo_ref, pl.program_id(0), tm)
        conv = cw_ref[0:1, :] * g2 + cw_ref[1:2, :] * g1 + cw_ref[2:3, :] * g0 + cb_ref[...]
        o_ref[...] = (conv * _sigmoid(conv) * u_ref[...]).astype(BF16)

    return pl.pallas_call(
        body, name="conv_fwd", out_shape=jax.ShapeDtypeStruct((L, F), BF16), grid=(L // tm, nj),
        in_specs=[pl.BlockSpec((tm, tn), lambda i, j: (i, j)),
                  pl.BlockSpec((8, tn), lambda i, j: (jnp.maximum(i * (tm // 8) - 1, 0), j)),
                  pl.BlockSpec((tm, tn), lambda i, j: (i, j + nj)),
                  pl.BlockSpec((3, tn), lambda i, j: (0, j)),
                  pl.BlockSpec((1, tn), lambda i, j: (0, j))],
        out_specs=pl.BlockSpec((tm, tn), lambda i, j: (i, j)),
        compiler_params=_params("parallel", "parallel"),
    )(gu, gu, gu, cw, cb)


def _conv_bwd_a(da, gu, cw, cb):
    L, F2 = gu.shape
    F = F2 // 2
    tm = _tile(L, ROW_TILES)
    tn = _tile(F, (512, 256, 128))
    nj = F // tn

    def body(da_ref, g_ref, halo_ref, u_ref, cw_ref, cb_ref, dc_ref, du_ref, dcb_ref, dcw_ref):
        i = pl.program_id(1)

        @pl.when(i == 0)
        def _():
            dcb_ref[...] = jnp.zeros_like(dcb_ref)
            dcw_ref[...] = jnp.zeros_like(dcw_ref)

        g0, g1, g2 = _conv_taps(g_ref, halo_ref, i, tm)
        conv = cw_ref[0:1, :] * g2 + cw_ref[1:2, :] * g1 + cw_ref[2:3, :] * g0 + cb_ref[...]
        sc = _sigmoid(conv)
        dav = da_ref[...]
        du_ref[...] = (dav * (conv * sc)).astype(BF16)
        dconv = dav * u_ref[...] * (sc * (1.0 + conv * (1.0 - sc)))
        dc_ref[...] = dconv
        dcb_ref[...] += jnp.sum(dconv, axis=0, keepdims=True)
        dcw_ref[0:1, :] += jnp.sum(dconv * g2, axis=0, keepdims=True)
        dcw_ref[1:2, :] += jnp.sum(dconv * g1, axis=0, keepdims=True)
        dcw_ref[2:3, :] += jnp.sum(dconv * g0, axis=0, keepdims=True)

    return pl.pallas_call(
        body, name="conv_bwd_a",
        out_shape=(jax.ShapeDtypeStruct((L, F), F32), jax.ShapeDtypeStruct((L, F), BF16),
                   jax.ShapeDtypeStruct((1, F), F32), jax.ShapeDtypeStruct((8, F), F32)),
        grid=(nj, L // tm),
        in_specs=[pl.BlockSpec((tm, tn), lambda j, i: (i, j)),
                  pl.BlockSpec((tm, tn), lambda j, i: (i, j)),
                  pl.BlockSpec((8, tn), lambda j, i: (jnp.maximum(i * (tm // 8) - 1, 0), j)),
                  pl.BlockSpec((tm, tn), lambda j, i: (i, j + nj)),
                  pl.BlockSpec((3, tn), lambda j, i: (0, j)),
                  pl.BlockSpec((1, tn), lambda j, i: (0, j))],
        out_specs=(pl.BlockSpec((tm, tn), lambda j, i: (i, j)), pl.BlockSpec((tm, tn), lambda j, i: (i, j)),
                   pl.BlockSpec((1, tn), lambda j, i: (0, j)), pl.BlockSpec((8, tn), lambda j, i: (0, j))),
        compiler_params=_params("parallel", "arbitrary"),
    )(da, gu, gu, gu, cw, cb)


def _conv_bwd_b(dconv, cw):
    L, F = dconv.shape
    tm = _tile(L, ROW_TILES)
    tn = _tile(F, (512, 256, 128))
    nblk8 = L // 8
    ni = L // tm

    def body(dc_ref, nxt_ref, cw_ref, o_ref):
        i = pl.program_id(0)
        dc = dc_ref[...]
        nxt = jnp.where(i < ni - 1, nxt_ref[...], 0.0)
        sub = lax.broadcasted_iota(jnp.int32, (8, 1), 0)
        n0 = jnp.sum(jnp.where(sub == 0, nxt, 0.0), axis=0, keepdims=True)
        n1 = jnp.sum(jnp.where(sub == 1, nxt, 0.0), axis=0, keepdims=True)
        r = lax.broadcasted_iota(jnp.int32, (tm, 1), 0)
        d1 = jnp.where(r == tm - 1, n0, pltpu.roll(dc, tm - 1, 0))
        d2 = jnp.where(r == tm - 2, n0, jnp.where(r == tm - 1, n1, pltpu.roll(dc, tm - 2, 0)))
        dg = cw_ref[2:3, :] * dc + cw_ref[1:2, :] * d1 + cw_ref[0:1, :] * d2
        rows = i * tm + r
        o_ref[...] = jnp.where(rows >= PAD_LEN, dg, 0.0).astype(BF16)

    return pl.pallas_call(
        body, name="conv_bwd_b", out_shape=jax.ShapeDtypeStruct((L, F), BF16), grid=(ni, F // tn),
        in_specs=[pl.BlockSpec((tm, tn), lambda i, j: (i, j)),
                  pl.BlockSpec((8, tn), lambda i, j: (jnp.minimum((i + 1) * (tm // 8), nblk8 - 1), j)),
                  pl.BlockSpec((3, tn), lambda i, j: (0, j))],
        out_specs=pl.BlockSpec((tm, tn), lambda i, j: (i, j)),
        compiler_params=_params("parallel", "parallel"),
    )(dconv, dconv, cw)


ANY = pl.BlockSpec(memory_space=pl.ANY)


def _coords():
    return lax.axis_index("x"), lax.axis_index("y"), lax.axis_index("c")


def _flip(v, bit):
    return 1 - v if bit else v


def _gather_shards(shards):
    n = len(shards)
    chips = [(1, 0), (0, 1), (1, 1)]

    def body(*refs):
        ins, outs = refs[:n], refs[n:2 * n]
        send_sems, recv_sems, local_sems = refs[2 * n:]
        x, y, c = _coords()
        mine = 2 * x + y
        copies = []
        for w in range(n):
            local = pltpu.make_async_copy(ins[w], outs[w].at[mine], local_sems.at[w])
            local.start()
            copies.append(local)
            for j, (dx, dy) in enumerate(chips):
                px, py = _flip(x, dx), _flip(y, dy)
                rc = pltpu.make_async_remote_copy(
                    src_ref=ins[w], dst_ref=outs[w].at[mine],
                    send_sem=send_sems.at[w * 3 + j], recv_sem=recv_sems.at[w * 3 + j],
                    device_id=(px, py, c), device_id_type=MESH)
                rc.start()
        for w in range(n):
            for j, (dx, dy) in enumerate(chips):
                px, py = _flip(x, dx), _flip(y, dy)
                rc = pltpu.make_async_remote_copy(
                    src_ref=ins[w], dst_ref=outs[w].at[2 * px + py],
                    send_sem=send_sems.at[w * 3 + j], recv_sem=recv_sems.at[w * 3 + j],
                    device_id=(px, py, c), device_id_type=MESH)
                rc.wait_recv()
                rc.wait_send()
        for cp in copies:
            cp.wait()

    return pl.pallas_call(
        body, name="gather_shards",
        out_shape=tuple(jax.ShapeDtypeStruct((4,) + s.shape, s.dtype) for s in shards),
        in_specs=[ANY] * n, out_specs=tuple([ANY] * n),
        scratch_shapes=[pltpu.SemaphoreType.DMA((3 * n,)), pltpu.SemaphoreType.DMA((3 * n,)),
                        pltpu.SemaphoreType.DMA((n,))],
        compiler_params=pltpu.CompilerParams(has_side_effects=True),
    )(*shards)


PEERS = [(dx, dy, dc) for dx in (0, 1) for dy in (0, 1) for dc in (0, 1)][1:]


def _scatter_partials(parts):
    n = len(parts)

    def body(*refs):
        ins, outs = refs[:n], refs[n:2 * n]
        send_sems, recv_sems, local_sems = refs[2 * n:]
        x, y, c = _coords()
        me = 4 * x + 2 * y + c
        copies = []
        for w in range(n):
            local = pltpu.make_async_copy(ins[w].at[2 * x + y, c], outs[w].at[me], local_sems.at[w])
            local.start()
            copies.append(local)
            for j, (dx, dy, dc) in enumerate(PEERS):
                px, py, pc = _flip(x, dx), _flip(y, dy), _flip(c, dc)
                pltpu.make_async_remote_copy(
                    src_ref=ins[w].at[2 * px + py, pc], dst_ref=outs[w].at[me],
                    send_sem=send_sems.at[w * 7 + j], recv_sem=recv_sems.at[w * 7 + j],
                    device_id=(px, py, pc), device_id_type=MESH).start()
        for w in range(n):
            for j, (dx, dy, dc) in enumerate(PEERS):
                px, py, pc = _flip(x, dx), _flip(y, dy), _flip(c, dc)
                rc = pltpu.make_async_remote_copy(
                    src_ref=ins[w].at[2 * px + py, pc], dst_ref=outs[w].at[4 * px + 2 * py + pc],
                    send_sem=send_sems.at[w * 7 + j], recv_sem=recv_sems.at[w * 7 + j],
                    device_id=(px, py, pc), device_id_type=MESH)
                rc.wait_recv()
                rc.wait_send()
        for cp in copies:
            cp.wait()

    return pl.pallas_call(
        body, name="scatter_partials",
        out_shape=tuple(jax.ShapeDtypeStruct((8,) + p.shape[2:], p.dtype) for p in parts),
        in_specs=[ANY] * n, out_specs=tuple([ANY] * n),
        scratch_shapes=[pltpu.SemaphoreType.DMA((7 * n,)), pltpu.SemaphoreType.DMA((7 * n,)),
                        pltpu.SemaphoreType.DMA((n,))],
        compiler_params=pltpu.CompilerParams(has_side_effects=True),
    )(*parts)


def _sum8(recv, name):
    _, r, c = recv.shape
    tr = _tile(r, (64, 32, 16))

    def body(in_ref, o_ref):
        acc = in_ref[0].astype(F32)
        for s in range(1, 8):
            acc = acc + in_ref[s].astype(F32)
        o_ref[...] = acc

    return pl.pallas_call(
        body, name=name, out_shape=jax.ShapeDtypeStruct((r, c), F32), grid=(r // tr,),
        in_specs=[pl.BlockSpec((8, tr, c), lambda i: (0, i, 0))],
        out_specs=pl.BlockSpec((tr, c), lambda i: (i, 0)),
        compiler_params=_params("parallel"),
    )(recv)


def _swap_halves(halves):
    n = len(halves)

    def body(*refs):
        ins, outs = refs[:n], refs[n:2 * n]
        send_sems, recv_sems = refs[2 * n:]
        x, y, c = _coords()

        def copy(w):
            return pltpu.make_async_remote_copy(
                src_ref=ins[w], dst_ref=outs[w], send_sem=send_sems.at[w], recv_sem=recv_sems.at[w],
                device_id=(x, y, 1 - c), device_id_type=MESH)

        for w in range(n):
            copy(w).start()
        for w in range(n):
            copy(w).wait_recv()
            copy(w).wait_send()

    return pl.pallas_call(
        body, name="swap_halves",
        out_shape=tuple(jax.ShapeDtypeStruct(h.shape, h.dtype) for h in halves),
        in_specs=[ANY] * n, out_specs=tuple([ANY] * n),
        scratch_shapes=[pltpu.SemaphoreType.DMA((n,)), pltpu.SemaphoreType.DMA((n,))],
        compiler_params=pltpu.CompilerParams(has_side_effects=True),
    )(*halves)


def _allreduce_small(packed):
    R = packed.shape[0]

    def body(in_ref, o_ref, buf, send_sems, recv_sems):
        x, y, c = _coords()
        me = 4 * x + 2 * y + c
        buf[me] = in_ref[...]
        for j, (dx, dy, dc) in enumerate(PEERS):
            px, py, pc = _flip(x, dx), _flip(y, dy), _flip(c, dc)
            pltpu.make_async_remote_copy(
                src_ref=in_ref, dst_ref=buf.at[me], send_sem=send_sems.at[j], recv_sem=recv_sems.at[j],
                device_id=(px, py, pc), device_id_type=MESH).start()
        for j, (dx, dy, dc) in enumerate(PEERS):
            px, py, pc = _flip(x, dx), _flip(y, dy), _flip(c, dc)
            rc = pltpu.make_async_remote_copy(
                src_ref=in_ref, dst_ref=buf.at[4 * px + 2 * py + pc], send_sem=send_sems.at[j],
                recv_sem=recv_sems.at[j], device_id=(px, py, pc), device_id_type=MESH)
            rc.wait_recv()
            rc.wait_send()
        acc = buf[0]
        for s in range(1, 8):
            acc = acc + buf[s]
        o_ref[...] = acc

    return pl.pallas_call(
        body, name="allreduce_small", out_shape=jax.ShapeDtypeStruct((R, 128), F32),
        in_specs=[pl.BlockSpec(memory_space=pltpu.VMEM)], out_specs=pl.BlockSpec(memory_space=pltpu.VMEM),
        scratch_shapes=[pltpu.VMEM((8, R, 128), F32), pltpu.SemaphoreType.DMA((7,)), pltpu.SemaphoreType.DMA((7,))],
        compiler_params=pltpu.CompilerParams(has_side_effects=True, vmem_limit_bytes=VMEM_LIMIT_BYTES),
    )(packed)


def _adamw_math(w, g, m, v):
    m = ADAM_B1 * m + (1.0 - ADAM_B1) * g
    v = ADAM_B2 * v + (1.0 - ADAM_B2) * (g * g)
    m_hat = m / (1.0 - ADAM_B1 ** ADAM_STEP)
    v_hat = v / (1.0 - ADAM_B2 ** ADAM_STEP)
    delta = -ADAM_LR * (m_hat / (jnp.sqrt(v_hat) + ADAM_EPS) + ADAM_WD * w)
    return delta, m, v


def _adamw_big(mine, other, w, m, v, name):
    R, C = w.shape
    tr = _tile(R // 2, (128, 64, 32, 16, 8))
    nb = (R // 2) // tr

    def body(mine_ref, other_ref, w_ref, m_ref, v_ref, g_ref, d_ref, mo_ref, vo_ref):
        is_mine = (pl.program_id(0) // nb) == lax.axis_index("c")
        g = jnp.where(is_mine, mine_ref[...], other_ref[...])
        d, mn, vn = _adamw_math(w_ref[...], g, m_ref[...], v_ref[...])
        g_ref[...] = g
        d_ref[...] = d
        mo_ref[...] = mn
        vo_ref[...] = vn

    blk = pl.BlockSpec((tr, C), lambda i: (i, 0))
    half = pl.BlockSpec((tr, C), lambda i: (i % nb, 0))
    sds = jax.ShapeDtypeStruct((R, C), F32)
    return pl.pallas_call(
        body, name=name, out_shape=(sds, sds, sds, sds), grid=(2 * nb,),
        in_specs=[half, half, blk, blk, blk], out_specs=(blk, blk, blk, blk), compiler_params=_params("parallel"),
    )(mine, other, w, m, v)


def _adamw_small(items, lb_raw, dlb):
    n = len(items)
    lb_w, lb_m, lb_v = lb_raw

    def body(*refs):
        ins = refs[:4 * n]
        dlb_ref, lw_ref, lm_ref, lv_ref = refs[4 * n:4 * n + 4]
        outs = refs[4 * n + 4:]
        for t in range(n):
            g_ref, w_ref, m_ref, v_ref = ins[4 * t:4 * t + 4]
            d, mn, vn = _adamw_math(w_ref[...], g_ref[...], m_ref[...], v_ref[...])
            outs[3 * t][...] = d
            outs[3 * t + 1][...] = mn
            outs[3 * t + 2][...] = vn
        p0 = 1.0 / (1.0 + jnp.exp(lw_ref[1:2, :] - lw_ref[0:1, :]))
        g0 = dlb_ref[...] * p0 * (1.0 - p0)
        base = 3 * n
        outs[base][0:1, :] = g0
        outs[base][1:2, :] = -g0
        d, mn, vn = _adamw_math(lw_ref[...], outs[base][...], lm_ref[...], lv_ref[...])
        outs[base + 1][...] = d
        outs[base + 2][...] = mn
        outs[base + 3][...] = vn

    operands = [a for it in items for a in it] + [dlb, lb_w, lb_m, lb_v]
    out_shape = []
    for (g, w, m, v) in items:
        out_shape += [jax.ShapeDtypeStruct(w.shape, F32)] * 3
    out_shape += [jax.ShapeDtypeStruct(lb_w.shape, F32)] * 4
    vm = pl.BlockSpec(memory_space=pltpu.VMEM)
    res = pl.pallas_call(
        body, name="adamw_small", out_shape=tuple(out_shape),
        in_specs=[vm] * len(operands), out_specs=tuple([vm] * len(out_shape)),
        compiler_params=pltpu.CompilerParams(vmem_limit_bytes=VMEM_LIMIT_BYTES),
    )(*operands)
    deltas = [res[3 * t] for t in range(n)] + [res[3 * n + 1]]
    new_m = [res[3 * t + 1] for t in range(n)] + [res[3 * n + 2]]
    new_v = [res[3 * t + 2] for t in range(n)] + [res[3 * n + 3]]
    return res[3 * n], deltas, new_m, new_v


def _cols_to_shards(g):
    R, C = g.shape
    return g.reshape(2, R // 2, 4, C // 4).transpose(2, 0, 1, 3)


def _rows_to_shards(g):
    R, C = g.shape
    return g.reshape(4, 2, R // 8, C)


def kernel(x, positions, meta_tokens, w_in, w_q_up, w_kv_up, w_branch_mla, w_branch_hgrn, w_out, w_ffn_in, w_ffn_out, conv_w, conv_b, g_mix_norm, g_q_norm, g_kv_norm, g_hgrn_norm, g_ffn_norm, g_final_norm, lb_raw, loss_target, m_meta_tokens, m_w_in, m_w_q_up, m_w_kv_up, m_w_branch_mla, m_w_branch_hgrn, m_w_out, m_w_ffn_in, m_w_ffn_out, m_conv_w, m_conv_b, m_g_mix_norm, m_g_q_norm, m_g_kv_norm, m_g_hgrn_norm, m_g_ffn_norm, m_g_final_norm, m_lb_raw, v_meta_tokens, v_w_in, v_w_q_up, v_w_kv_up, v_w_branch_mla, v_w_branch_hgrn, v_w_out, v_w_ffn_in, v_w_ffn_out, v_conv_w, v_conv_b, v_g_mix_norm, v_g_q_norm, v_g_kv_norm, v_g_hgrn_norm, v_g_ffn_norm, v_g_final_norm, v_lb_raw):
    S, D = x.shape[1], x.shape[2]
    L = S + PREFIX
    QL, KVL = g_q_norm.shape[1], g_kv_norm.shape[1]
    F = conv_b.shape[1]
    heads = (4 * w_kv_up.shape[2]) // QPAD
    nh = D // HEAD
    assert lb_raw.shape[0] == 2 and g_hgrn_norm.shape[1] == HEAD and L % CHUNK == 0
    ix, iy, _ = _coords()
    chip = 2 * ix + iy

    big = [w_in, w_q_up, w_kv_up, w_branch_mla, w_branch_hgrn, w_out, w_ffn_in, w_ffn_out]
    col_sharded = [True, True, True, False, False, False, True, False]
    gathered = _gather_shards([w[0].astype(BF16) for w in big] + [meta_tokens, conv_w[0]])

    def full(gw, by_cols):
        _, r, c = gw.shape
        return gw.transpose(1, 0, 2).reshape(r, 4 * c) if by_cols else gw.reshape(4 * r, c)

    W_in, W_q, W_kv, W_a, W_b, W_o, W_fi, W_fo = [full(gw, bc) for gw, bc in zip(gathered[:8], col_sharded)]
    meta_full = full(gathered[8], True)
    cw_full = full(gathered[9], True)
    c0 = QL + KVL
    W_lat = W_in[:, :c0]
    W_kr = jnp.pad(W_in[:, c0:c0 + ROPE], ((0, 0), (0, 128 - ROPE)))
    W_H = W_in[:, c0 + ROPE:c0 + ROPE + 4 * D]
    W_G = W_in[:, c0 + ROPE + 4 * D:]
    W_qp = jnp.pad(W_q.reshape(QL, heads, QK_HEAD), ((0, 0), (0, 0), (0, QPAD - QK_HEAD))).reshape(QL, heads * QPAD)

    h0 = jnp.concatenate([jnp.zeros((PAD_LEN, D), F32), meta_full, x[0]], axis=0)
    pos = jnp.concatenate([jnp.zeros((PAD_LEN,), jnp.int32), jnp.arange(N_META, dtype=jnp.int32),
                           positions[0].astype(jnp.int32) + N_META])
    inv = 1.0 / (ROPE_THETA ** (jnp.arange(0, ROPE, 2, dtype=F32) / ROPE))
    ang = pos.astype(F32)[:, None] * inv
    zero = jnp.zeros((L, 128 - ROPE), F32)
    cos = jnp.concatenate([jnp.cos(ang), jnp.cos(ang), zero], axis=1)
    sin = jnp.concatenate([jnp.sin(ang), jnp.sin(ang), zero], axis=1)

    u1 = _rmsnorm_fwd(h0, g_mix_norm, "norm_mix")
    lat = _mm(u1, W_lat, "nn", F32, "proj_lat")
    hp = _mm(u1, W_H, "nn", F32, "proj_hgrn")
    gates = _mm(u1, W_G, "nn", F32, "proj_gates")
    kr = _mm(u1, W_kr, "nn", F32, "proj_krope")
    qn = _rmsnorm_fwd(lat, g_q_norm, "norm_q", col0=0, width=QL)
    kvn = _rmsnorm_fwd(lat, g_kv_norm, "norm_kv", col0=QL, width=KVL)
    qp = _mm(qn, W_qp, "nn", F32, "q_up")
    kv = _mm(kvn, W_kv, "nn", F32, "kv_up")
    qc, kc, vv = _rope_fwd(qp, kv, kr, cos, sin, heads)
    o_mla, lse = _attn_fwd(qc, kc, vv, heads)
    o_hgrn, o_rec, s_hist = _hgrn_fwd(hp, lb_raw, g_hgrn_norm, nh)
    br_a = _mm(o_mla, W_a, "nn", F32, "branch_mla")
    br_b = _mm(o_hgrn, W_b, "nn", F32, "branch_hgrn")
    merged = _merge_fwd(br_a, br_b, gates)
    h1 = _mm(merged, W_o, "nn", F32, "out_proj", res=h0)
    u2 = _rmsnorm_fwd(h1, g_ffn_norm, "norm_ffn")
    gu = _mm(u2, W_fi, "nn", F32, "ffn_in")
    act = _conv_fwd(gu, cw_full, conv_b)
    h2 = _mm(act, W_fo, "nn", F32, "ffn_out", res=h1)
    dh2, loss_p, dg_final = _final_loss_bwd(h2, loss_target[0], g_final_norm.reshape(1, D))

    dact = _mm(dh2, W_fo, "nt", F32, "d_act")
    dW_fo = _mm(act, dh2, "tn", F32, "dw_ffn_out")
    dconv, dup, dcb, dcw = _conv_bwd_a(dact, gu, cw_full, conv_b)
    dgate = _conv_bwd_b(dconv, cw_full)
    dgu = jnp.concatenate([dgate, dup], axis=1)
    du2 = _mm(dgu, W_fi, "nt", F32, "d_u2")
    dW_fi = _mm(u2, dgu, "tn", F32, "dw_ffn_in")
    dh1, dg_ffn = _rmsnorm_bwd(du2, h1, g_ffn_norm, "norm_ffn_bwd", F32, res=dh2)
    dmerged = _mm(dh1, W_o, "nt", F32, "d_merged")
    dW_o = _mm(merged, dh1, "tn", F32, "dw_out")
    d_a, d_b, d_gates = _merge_bwd(dmerged, br_a, br_b, gates)
    do_mla = _mm(d_a, W_a, "nt", BF16, "d_o_mla")
    dW_a = _mm(o_mla, d_a, "tn", F32, "dw_branch_mla")
    do_hgrn = _mm(d_b, W_b, "nt", F32, "d_o_hgrn")
    dW_b = _mm(o_hgrn, d_b, "tn", F32, "dw_branch_hgrn")
    dhq, dhf, dhi, dhg, dgn_p, dlb_p = _hgrn_bwd(hp, lb_raw, g_hgrn_norm, do_hgrn, o_rec, s_hist, nh)
    dqc, dkc, dvv = _attn_bwd(qc, kc, vv, o_mla, do_mla, lse, heads)
    dqp, dkv, dkr = _rope_bwd(dqc, dkc, dvv, cos, sin, heads)
    dqn = _mm(dqp, W_qp, "nt", F32, "d_qn")
    dW_qp = _mm(qn, dqp, "tn", F32, "dw_q_up")
    dkvn = _mm(dkv, W_kv, "nt", F32, "d_kvn")
    dW_kv = _mm(kvn, dkv, "tn", F32, "dw_kv_up")
    dq_lat, dg_q = _rmsnorm_bwd(dqn, lat, g_q_norm, "norm_q_bwd", BF16, col0=0)
    dkv_lat, dg_kv = _rmsnorm_bwd(dkvn, lat, g_kv_norm, "norm_kv_bwd", BF16, col0=QL)
    dlat = jnp.concatenate([dq_lat, dkv_lat], axis=1)
    dhp = jnp.concatenate([dhq, dhf, dhi, dhg], axis=1)
    du1 = _mm(dlat, W_lat, "nt", F32, "d_u1_lat")
    du1 = _mm(dhp, W_H, "nt", F32, "d_u1_hgrn", res=du1)
    du1 = _mm(d_gates, W_G, "nt", F32, "d_u1_gates", res=du1)
    du1 = _mm(dkr, W_kr, "nt", F32, "d_u1_krope", res=du1)
    dW_lat = _mm(u1, dlat, "tn", F32, "dw_in_lat")
    dW_H = _mm(u1, dhp, "tn", F32, "dw_in_hgrn")
    dW_G = _mm(u1, d_gates, "tn", F32, "dw_in_gates")
    dW_kr = _mm(u1, dkr, "tn", F32, "dw_in_krope")
    dh0, dg_mix = _rmsnorm_bwd(du1, h0, g_mix_norm, "norm_mix_bwd", F32, res=dh1)
    grad_x = dh0[PREFIX:][None]

    dW_in = jnp.concatenate([dW_lat, dW_kr[:, :ROPE], dW_H, dW_G], axis=1)
    dW_q = dW_qp.reshape(QL, heads, QPAD)[:, :, :QK_HEAD].reshape(QL, heads * QK_HEAD)
    big_grads = [dW_in, dW_q, dW_kv, dW_a, dW_b, dW_o, dW_fi, dW_fo]
    parts = [(_cols_to_shards(g) if bc else _rows_to_shards(g)).astype(BF16) for g, bc in zip(big_grads, col_sharded)]
    received = _scatter_partials(parts)
    names = ["w_in", "w_q_up", "w_kv_up", "w_branch_mla", "w_branch_hgrn", "w_out", "w_ffn_in", "w_ffn_out"]
    halves = [_sum8(r, "sum_" + nm) for r, nm in zip(received, names)]
    others = _swap_halves(halves)
    big_m = [m_w_in, m_w_q_up, m_w_kv_up, m_w_branch_mla, m_w_branch_hgrn, m_w_out, m_w_ffn_in, m_w_ffn_out]
    big_v = [v_w_in, v_w_q_up, v_w_kv_up, v_w_branch_mla, v_w_branch_hgrn, v_w_out, v_w_ffn_in, v_w_ffn_out]
    big_out = {}
    for nm, mine, other, w, m, v in zip(names, halves, others, big, big_m, big_v):
        g, d, mn, vn = _adamw_big(mine, other, w[0], m[0], v[0], "adamw_" + nm)
        big_out[nm] = (g[None], d[None], mn[None], vn[None])

    pieces = [loss_p[:, :1], dg_mix, dg_q, dg_kv, jnp.sum(dgn_p[:, 0, :], axis=0, keepdims=True), dg_ffn, dg_final,
              dlb_p[:, 0, :].reshape(1, D), dcb, dcw[0:3].reshape(1, 3 * F), dh0[PAD_LEN:PREFIX].reshape(1, N_META * D)]
    sizes = [p.shape[1] for p in pieces]
    flat = jnp.concatenate(pieces, axis=1)[0]
    rows = -(-flat.shape[0] // 1024) * 8
    packed = jnp.pad(flat, (0, rows * 128 - flat.shape[0])).reshape(rows, 128)
    total = _allreduce_small(packed).reshape(-1)
    offs = [0]
    for s in sizes:
        offs.append(offs[-1] + s)
    loss, g_mix, g_q, g_kv, g_hg, g_ffn, g_fin, dlb, g_cb, g_cw, g_meta = [
        total[offs[t]:offs[t + 1]].reshape(1, sizes[t]) for t in range(len(sizes))]
    g_cw = lax.dynamic_slice_in_dim(g_cw.reshape(3, F), chip * (F // 4), F // 4, axis=1)
    g_meta = lax.dynamic_slice_in_dim(g_meta.reshape(N_META, D), chip * (D // 4), D // 4, axis=1)
    items = [(g_meta, meta_tokens, m_meta_tokens, v_meta_tokens),
             (g_cw, conv_w[0], m_conv_w[0], v_conv_w[0]),
             (g_cb, conv_b, m_conv_b, v_conv_b),
             (g_mix, g_mix_norm, m_g_mix_norm, v_g_mix_norm),
             (g_q, g_q_norm, m_g_q_norm, v_g_q_norm),
             (g_kv, g_kv_norm, m_g_kv_norm, v_g_kv_norm),
             (g_hg, g_hgrn_norm, m_g_hgrn_norm, v_g_hgrn_norm),
             (g_ffn, g_ffn_norm, m_g_ffn_norm, v_g_ffn_norm),
             (g_fin, g_final_norm.reshape(1, D), m_g_final_norm.reshape(1, D), v_g_final_norm.reshape(1, D))]
    g_lb, s_delta, s_m, s_v = _adamw_small(items, (lb_raw, m_lb_raw, v_lb_raw), dlb)
    s_grads = [it[0] for it in items] + [g_lb]

    def shape_small(vals):
        meta, cw, cb, mix, q, kvg, hg, ffn, fin, lb = vals
        return [meta, cw[None], cb, mix, q, kvg, hg, ffn, fin.reshape(D), lb]

    s_grads, s_delta, s_m, s_v = [shape_small(v) for v in (s_grads, s_delta, s_m, s_v)]

    def ordered(kind, small):
        bigs = [big_out[nm][kind] for nm in names]
        return [small[0]] + bigs + small[1:]

    return (loss.reshape(()), grad_x, *ordered(0, s_grads), *ordered(1, s_delta), *ordered(2, s_m), *ordered(3, s_v))
```

```python
import math

import jax
import jax.numpy as jnp
from jax import lax
from jax.experimental import pallas as pl
from jax.experimental.pallas import tpu as pltpu

F32 = jnp.float32
BF16 = jnp.bfloat16
MESH = pl.DeviceIdType.MESH

NORM_EPS = 1e-6
N_META = 16
PREFIX = 128
PAD_LEN = PREFIX - N_META
HEAD = 128
ROPE = 64
QK_HEAD = HEAD + ROPE
QPAD = 2 * HEAD
SOFTMAX_SCALE = QK_HEAD ** -0.5
ROPE_THETA = 10000.0
CHUNK = 128
LEVELS = 7
NEG = -1e30

ADAM_LR = 0.001
ADAM_B1 = 0.9
ADAM_B2 = 0.999
ADAM_EPS = 1e-08
ADAM_WD = 0.01
ADAM_STEP = 10

VMEM_LIMIT_BYTES = 48 * 1024 * 1024


def _params(*sem):
    return pltpu.CompilerParams(dimension_semantics=sem, vmem_limit_bytes=VMEM_LIMIT_BYTES)


def _tile(n, prefs):
    for p in prefs:
        if n % p == 0:
            return p
    return n


ROW_TILES = (640, 512, 256, 128, 64, 32, 16, 8)
TN_ROW_TILES = (1024, 1408, 1536, 512, 256, 128)
COL_TILES = (1024, 512, 256, 128)
K_TILES = (2048, 1536, 1408, 1024, 512, 256, 128)


def _sigmoid(x):
    return 1.0 / (1.0 + jnp.exp(-x))


def _dot(a, b, dims):
    return lax.dot_general(a.astype(BF16), b.astype(BF16), (dims, ((), ())), preferred_element_type=F32)


NN = ((1,), (0,))
TN = ((0,), (0,))
NT = ((1,), (1,))


def _split3(x):
    hi = x.astype(BF16)
    r = x - hi.astype(F32)
    mid = r.astype(BF16)
    lo = (r - mid.astype(F32)).astype(BF16)
    return hi, mid, lo


def _dot_exact_rhs(sel, x, dims):
    hi, mid, lo = _split3(x)
    return _dot(sel, hi, dims) + _dot(sel, mid, dims) + _dot(sel, lo, dims)


def _mm(a, b, mode, out_dtype, name, res=None):
    if mode == "nn":
        (M, K), (K2, N) = a.shape, b.shape
    elif mode == "tn":
        (K, M), (K2, N) = a.shape, b.shape
    else:
        (M, K), (N, K2) = a.shape, b.shape
    assert K == K2, (name, a.shape, b.shape)
    tm = _tile(M, TN_ROW_TILES if mode == "tn" else ROW_TILES)
    tn = _tile(N, COL_TILES)
    tk = _tile(K, ROW_TILES if mode == "tn" else K_TILES)
    nk = K // tk
    dims = {"nn": NN, "tn": TN, "nt": NT}[mode]

    def body(*refs):
        a_ref, b_ref = refs[:2]
        r_ref = None if res is None else refs[2]
        o_ref = refs[2 if res is None else 3]

        def finish(r):
            if r_ref is not None:
                r = r + r_ref[...].astype(F32)
            o_ref[...] = r.astype(out_dtype)

        if nk == 1:
            finish(_dot(a_ref[...], b_ref[...], dims))
            return
        acc = refs[-1]
        k = pl.program_id(2)

        @pl.when(k == 0)
        def _():
            acc[...] = jnp.zeros_like(acc)

        acc[...] += _dot(a_ref[...], b_ref[...], dims)

        @pl.when(k == nk - 1)
        def _():
            finish(acc[...])

    if mode == "nn":
        a_spec = pl.BlockSpec((tm, tk), lambda i, j, k: (i, k))
        b_spec = pl.BlockSpec((tk, tn), lambda i, j, k: (k, j))
    elif mode == "tn":
        a_spec = pl.BlockSpec((tk, tm), lambda i, j, k: (k, i))
        b_spec = pl.BlockSpec((tk, tn), lambda i, j, k: (k, j))
    else:
        a_spec = pl.BlockSpec((tm, tk), lambda i, j, k: (i, k))
        b_spec = pl.BlockSpec((tn, tk), lambda i, j, k: (j, k))
    in_specs = [a_spec, b_spec]
    operands = [a, b]
    if res is not None:
        in_specs.append(pl.BlockSpec((tm, tn), lambda i, j, k: (i, j)))
        operands.append(res)
    return pl.pallas_call(
        body, name=name,
        out_shape=jax.ShapeDtypeStruct((M, N), out_dtype),
        grid=(M // tm, N // tn, nk),
        in_specs=in_specs,
        out_specs=pl.BlockSpec((tm, tn), lambda i, j, k: (i, j)),
        scratch_shapes=[] if nk == 1 else [pltpu.VMEM((tm, tn), F32)],
        compiler_params=_params("parallel", "parallel", "arbitrary"),
    )(*operands)


def _rmsnorm_fwd(x, g, name, col0=0, width=None):
    L = x.shape[0]
    width = x.shape[1] if width is None else width
    assert col0 % width == 0
    cb = col0 // width
    tm = _tile(L, (128, 64, 32, 16))

    def body(x_ref, g_ref, o_ref):
        xv = x_ref[...]
        r = lax.rsqrt(jnp.mean(xv * xv, axis=-1, keepdims=True) + NORM_EPS)
        o_ref[...] = ((xv * r) * g_ref[...]).astype(BF16)

    return pl.pallas_call(
        body, name=name,
        out_shape=jax.ShapeDtypeStruct((L, width), BF16),
        grid=(L // tm,),
        in_specs=[pl.BlockSpec((tm, width), lambda i: (i, cb)), pl.BlockSpec((1, width), lambda i: (0, 0))],
        out_specs=pl.BlockSpec((tm, width), lambda i: (i, 0)),
        compiler_params=_params("parallel"),
    )(x, g)


def _rmsnorm_bwd(dy, x, g, name, out_dtype, col0=0, res=None):
    L, width = dy.shape
    assert col0 % width == 0
    cb = col0 // width
    tm = _tile(L, (128, 64, 32, 16))

    def body(*refs):
        if res is None:
            dy_ref, x_ref, g_ref, dx_ref, dg_ref = refs
            r_ref = None
        else:
            dy_ref, x_ref, g_ref, r_ref, dx_ref, dg_ref = refs

        @pl.when(pl.program_id(0) == 0)
        def _():
            dg_ref[...] = jnp.zeros_like(dg_ref)

        xv = x_ref[...]
        dyv = dy_ref[...].astype(F32)
        r = lax.rsqrt(jnp.mean(xv * xv, axis=-1, keepdims=True) + NORM_EPS)
        z = dyv * g_ref[...]
        dx = r * z - xv * ((r * r * r) * jnp.mean(z * xv, axis=-1, keepdims=True))
        if r_ref is not None:
            dx = dx + r_ref[...]
        dx_ref[...] = dx.astype(out_dtype)
        dg_ref[...] += jnp.sum(dyv * (xv * r), axis=0, keepdims=True)

    in_specs = [pl.BlockSpec((tm, width), lambda i: (i, 0)),
                pl.BlockSpec((tm, width), lambda i: (i, cb)),
                pl.BlockSpec((1, width), lambda i: (0, 0))]
    operands = [dy, x, g]
    if res is not None:
        in_specs.append(pl.BlockSpec((tm, width), lambda i: (i, 0)))
        operands.append(res)
    return pl.pallas_call(
        body, name=name,
        out_shape=(jax.ShapeDtypeStruct((L, width), out_dtype), jax.ShapeDtypeStruct((1, width), F32)),
        grid=(L // tm,),
        in_specs=in_specs,
        out_specs=(pl.BlockSpec((tm, width), lambda i: (i, 0)), pl.BlockSpec((1, width), lambda i: (0, 0))),
        compiler_params=_params("arbitrary"),
    )(*operands)


def _final_loss_bwd(h2, tgt, g):
    L, D = h2.shape
    tm = PREFIX
    inv_d = 1.0 / D

    def body(h_ref, t_ref, g_ref, dh_ref, loss_ref, dg_ref):
        i = pl.program_id(0)

        @pl.when(i == 0)
        def _():
            loss_ref[...] = jnp.zeros_like(loss_ref)
            dg_ref[...] = jnp.zeros_like(dg_ref)

        xv = h_ref[...]
        r = lax.rsqrt(jnp.mean(xv * xv, axis=-1, keepdims=True) + NORM_EPS)
        xn = xv * r
        y = xn * g_ref[...]
        real = (i >= PREFIX // tm).astype(F32)
        diff = (y - t_ref[...]) * real
        loss_ref[...] += 0.5 * inv_d * jnp.sum(diff * diff)
        dyv = diff * inv_d
        z = dyv * g_ref[...]
        dh_ref[...] = r * z - xv * ((r * r * r) * jnp.mean(z * xv, axis=-1, keepdims=True))
        dg_ref[...] += jnp.sum(dyv * xn, axis=0, keepdims=True)

    shift = PREFIX // tm
    return pl.pallas_call(
        body, name="final_loss_bwd",
        out_shape=(jax.ShapeDtypeStruct((L, D), F32), jax.ShapeDtypeStruct((1, 128), F32),
                   jax.ShapeDtypeStruct((1, D), F32)),
        grid=(L // tm,),
        in_specs=[pl.BlockSpec((tm, D), lambda i: (i, 0)),
                  pl.BlockSpec((tm, D), lambda i: (jnp.maximum(i - shift, 0), 0)),
                  pl.BlockSpec((1, D), lambda i: (0, 0))],
        out_specs=(pl.BlockSpec((tm, D), lambda i: (i, 0)), pl.BlockSpec((1, 128), lambda i: (0, 0)),
                   pl.BlockSpec((1, D), lambda i: (0, 0))),
        compiler_params=_params("arbitrary"),
    )(h2, tgt, g)


def _rot_half(x):
    lane = lax.broadcasted_iota(jnp.int32, x.shape, 1)
    return jnp.where(lane < ROPE // 2, -pltpu.roll(x, 128 - ROPE // 2, 1), pltpu.roll(x, ROPE // 2, 1))


def _rope_fwd(qp, kv, kr, cos, sin, heads):
    L = qp.shape[0]
    tm = _tile(L, (128,))

    def body(q_ref, kv_ref, kr_ref, c_ref, s_ref, qc_ref, kc_ref, v_ref):
        c, s = c_ref[...], s_ref[...]
        krv = kr_ref[...]
        kr_rot = (krv * c + _rot_half(krv) * s).astype(BF16)
        for h in range(heads):
            lo = h * QPAD
            qc_ref[:, lo:lo + HEAD] = (q_ref[:, lo:lo + HEAD] * SOFTMAX_SCALE).astype(BF16)
            qr = q_ref[:, lo + HEAD:lo + QPAD]
            qc_ref[:, lo + HEAD:lo + QPAD] = ((qr * c + _rot_half(qr) * s) * SOFTMAX_SCALE).astype(BF16)
            kc_ref[:, lo:lo + HEAD] = kv_ref[:, lo:lo + HEAD].astype(BF16)
            kc_ref[:, lo + HEAD:lo + QPAD] = kr_rot
            v_ref[:, h * HEAD:(h + 1) * HEAD] = kv_ref[:, lo + HEAD:lo + QPAD].astype(BF16)

    W = heads * QPAD
    row = lambda w: pl.BlockSpec((tm, w), lambda i: (i, 0))
    return pl.pallas_call(
        body, name="rope_fwd",
        out_shape=(jax.ShapeDtypeStruct((L, W), BF16), jax.ShapeDtypeStruct((L, W), BF16),
                   jax.ShapeDtypeStruct((L, heads * HEAD), BF16)),
        grid=(L // tm,),
        in_specs=[row(W), row(W), row(128), row(128), row(128)],
        out_specs=(row(W), row(W), row(heads * HEAD)),
        compiler_params=_params("parallel"),
    )(qp, kv, kr, cos, sin)


def _rope_bwd(dqc, dkc, dv, cos, sin, heads):
    L = dqc.shape[0]
    tm = _tile(L, (128,))

    def body(dq_ref, dk_ref, dv_ref, c_ref, s_ref, dqp_ref, dkv_ref, dkr_ref):
        c, s = c_ref[...], s_ref[...]
        acc = jnp.zeros((tm, 128), F32)
        for h in range(heads):
            lo = h * QPAD
            dqp_ref[:, lo:lo + HEAD] = (dq_ref[:, lo:lo + HEAD] * SOFTMAX_SCALE).astype(BF16)
            d = dq_ref[:, lo + HEAD:lo + QPAD]
            dqp_ref[:, lo + HEAD:lo + QPAD] = ((d * c - _rot_half(d) * s) * SOFTMAX_SCALE).astype(BF16)
            dkv_ref[:, lo:lo + HEAD] = dk_ref[:, lo:lo + HEAD].astype(BF16)
            dkv_ref[:, lo + HEAD:lo + QPAD] = dv_ref[:, h * HEAD:(h + 1) * HEAD].astype(BF16)
            acc = acc + dk_ref[:, lo + HEAD:lo + QPAD]
        dkr_ref[...] = (acc * c - _rot_half(acc) * s).astype(BF16)

    W = heads * QPAD
    row = lambda w: pl.BlockSpec((tm, w), lambda i: (i, 0))
    return pl.pallas_call(
        body, name="rope_bwd",
        out_shape=(jax.ShapeDtypeStruct((L, W), BF16), jax.ShapeDtypeStruct((L, W), BF16),
                   jax.ShapeDtypeStruct((L, 128), BF16)),
        grid=(L // tm,),
        in_specs=[row(W), row(W), row(heads * HEAD), row(128), row(128)],
        out_specs=(row(W), row(W), row(128)),
        compiler_params=_params("parallel"),
    )(dqc, dkc, dv, cos, sin)


def _attn_keep(qi, ki, ta):
    t = qi * ta + lax.broadcasted_iota(jnp.int32, (ta, ta), 0)
    s = ki * ta + lax.broadcasted_iota(jnp.int32, (ta, ta), 1)
    return (s <= t) & ((s >= PAD_LEN) | (s == t))


def _attn_fwd(qc, kc, v, heads):
    L = qc.shape[0]
    ta = _tile(L, (640, 128))
    nb = L // ta

    def body(q_ref, k_ref, v_ref, o_ref, lse_ref, m_sc, l_sc, acc_sc):
        qi, ki = pl.program_id(1), pl.program_id(2)

        @pl.when(ki == 0)
        def _():
            m_sc[...] = jnp.full_like(m_sc, NEG)
            l_sc[...] = jnp.zeros_like(l_sc)
            acc_sc[...] = jnp.zeros_like(acc_sc)

        def step(masked):
            s = _dot(q_ref[...], k_ref[...], NT)
            if masked:
                s = jnp.where(_attn_keep(qi, ki, ta), s, NEG)
            m_old = m_sc[...]
            m_new = jnp.maximum(m_old, jnp.max(s, axis=-1, keepdims=True))
            p = jnp.exp(s - m_new)
            alpha = jnp.exp(m_old - m_new)
            l_sc[...] = alpha * l_sc[...] + jnp.sum(p, axis=-1, keepdims=True)
            acc_sc[...] = alpha * acc_sc[...] + _dot(p, v_ref[...], NN)
            m_sc[...] = m_new

        pl.when((ki == qi) | ((ki == 0) & (qi > 0)))(lambda: step(True))
        pl.when((ki > 0) & (ki < qi))(lambda: step(False))

        @pl.when(ki == qi)
        def _():
            l = l_sc[...]
            o_ref[...] = (acc_sc[...] / l).astype(BF16)
            lse_ref[...] = jnp.broadcast_to(m_sc[...] + jnp.log(l), (ta, HEAD))

    return pl.pallas_call(
        body, name="attn_fwd",
        out_shape=(jax.ShapeDtypeStruct((L, heads * HEAD), BF16), jax.ShapeDtypeStruct((L, heads * HEAD), F32)),
        grid=(heads, nb, nb),
        in_specs=[pl.BlockSpec((ta, QPAD), lambda h, i, j: (i, h)),
                  pl.BlockSpec((ta, QPAD), lambda h, i, j: (jnp.minimum(j, i), h)),
                  pl.BlockSpec((ta, HEAD), lambda h, i, j: (jnp.minimum(j, i), h))],
        out_specs=(pl.BlockSpec((ta, HEAD), lambda h, i, j: (i, h)),
                   pl.BlockSpec((ta, HEAD), lambda h, i, j: (i, h))),
        scratch_shapes=[pltpu.VMEM((ta, 1), F32), pltpu.VMEM((ta, 1), F32), pltpu.VMEM((ta, HEAD), F32)],
        compiler_params=_params("parallel", "parallel", "arbitrary"),
    )(qc, kc, v)


def _attn_bwd(qc, kc, v, o, do, lse, heads):
    L = qc.shape[0]
    ta = _tile(L, (640, 128))
    nb = L // ta

    def body(q_ref, k_ref, v_ref, o_ref, do_ref, lse_ref, dq_ref, dk_ref, dv_ref):
        kj, qi = pl.program_id(1), pl.program_id(2)

        @pl.when((kj == 0) & (qi == 0))
        def _():
            dq_ref[...] = jnp.zeros_like(dq_ref)

        @pl.when(qi == 0)
        def _():
            dk_ref[...] = jnp.zeros_like(dk_ref)
            dv_ref[...] = jnp.zeros_like(dv_ref)

        def step(masked):
            q, k, vv, dov = q_ref[...], k_ref[...], v_ref[...], do_ref[...]
            p = jnp.exp(_dot(q, k, NT) - lse_ref[:, 0:1])
            if masked:
                p = jnp.where(_attn_keep(qi, kj, ta), p, 0.0)
            delta = jnp.sum(dov.astype(F32) * o_ref[...].astype(F32), axis=-1, keepdims=True)
            ds = p * (_dot(dov, vv, NT) - delta)
            dv_ref[...] += _dot(p, dov, TN)
            dk_ref[...] += _dot(ds, q, TN)
            rows = pl.ds(pl.multiple_of(qi * ta, ta), ta)
            dq_ref[rows, :] += _dot(ds, k, NN)

        pl.when((qi == kj) | ((kj == 0) & (qi > 0)))(lambda: step(True))
        pl.when((kj > 0) & (qi > kj))(lambda: step(False))

    qrow = lambda w: pl.BlockSpec((ta, w), lambda h, j, i: (jnp.maximum(i, j), h))
    krow = lambda w: pl.BlockSpec((ta, w), lambda h, j, i: (j, h))
    return pl.pallas_call(
        body, name="attn_bwd",
        out_shape=(jax.ShapeDtypeStruct((L, heads * QPAD), F32), jax.ShapeDtypeStruct((L, heads * QPAD), F32),
                   jax.ShapeDtypeStruct((L, heads * HEAD), F32)),
        grid=(heads, nb, nb),
        in_specs=[qrow(QPAD), krow(QPAD), krow(HEAD), qrow(HEAD), qrow(HEAD), qrow(HEAD)],
        out_specs=(pl.BlockSpec((L, QPAD), lambda h, j, i: (0, h)), krow(QPAD), krow(HEAD)),
        compiler_params=_params("parallel", "arbitrary", "arbitrary"),
    )(qc, kc, v, o, do, lse)


def _hgrn_constants():
    t = jnp.arange(CHUNK)
    tril = (t[None, :] <= t[:, None]).astype(BF16)
    sel = []
    for lv in range(LEVELS):
        hs = 1 << lv
        mid = (t // (2 * hs)) * (2 * hs) + hs - 1
        sel.append((t[None, :] == mid[:, None]).astype(BF16))
    return tril, jnp.concatenate(sel, axis=0)


def _hgrn_decay_grad_constants():
    r = jnp.arange(CHUNK)[:, None]
    c = jnp.arange(CHUNK)[None, :]
    mats = []
    for lv in range(LEVELS):
        same = (r >> (lv + 1)) == (c >> (lv + 1))
        second = ((r >> lv) & 1) == 1
        mats.append(same & jnp.where(second, c >= r, c < r))
    mats += [c >= r, c < r]
    return jnp.concatenate(mats, axis=1).astype(BF16)


def _hgrn_gates(hq_ref, hf_ref, hi_ref, lb_ref, row0):
    rows = row0 + lax.broadcasted_iota(jnp.int32, (CHUNK, 1), 0)
    valid = rows >= PAD_LEN
    lb = 1.0 / (1.0 + jnp.exp(lb_ref[1:2, :] - lb_ref[0:1, :]))
    hq = hq_ref[...]
    sq = _sigmoid(hq)
    sg = _sigmoid(hf_ref[...])
    f = lb + (1.0 - lb) * sg
    g = jnp.where(valid, jnp.log(f), 0.0)
    k = jnp.where(valid, 1.0 - f, 0.0)
    return dict(q=hq * sq, sq=sq, hq=hq, k=k, v=hi_ref[...], g=g, f=f, sg=sg, lb=lb, valid=valid)


def _hgrn_levels(q, k, b, sel_ref):
    t = lax.broadcasted_iota(jnp.int32, (CHUNK, 1), 0)
    tt = lax.broadcasted_iota(jnp.int32, (CHUNK, CHUNK), 0)
    ss = lax.broadcasted_iota(jnp.int32, (CHUNK, CHUNK), 1)
    bm_all = _dot_exact_rhs(sel_ref[...], b, NN)
    out = []
    for lv in range(LEVELS):
        bm = bm_all[lv * CHUNK:(lv + 1) * CHUNK, :]
        second = ((t >> lv) & 1) == 1
        eq = jnp.where(second, jnp.exp(jnp.minimum(b - bm, 0.0)), 0.0)
        ek = jnp.where(second, 0.0, jnp.exp(jnp.minimum(bm - b, 0.0)))
        same = (tt >> (lv + 1)) == (ss >> (lv + 1))
        out.append((eq, ek, (q * eq).astype(BF16), (k * ek).astype(BF16), same))
    return out


def _hgrn_intra(q, k, levels):
    tt = lax.broadcasted_iota(jnp.int32, (CHUNK, CHUNK), 0)
    ss = lax.broadcasted_iota(jnp.int32, (CHUNK, CHUNK), 1)
    p = jnp.where(tt == ss, jnp.sum(q * k, axis=-1, keepdims=True), 0.0)
    for (_, _, ql, kl, same) in levels:
        p = p + jnp.where(same, _dot(ql, kl, NT), 0.0)
    return p


def _hgrn_fwd(hp, lb_raw, g_norm, nh):
    L = hp.shape[0]
    D = nh * HEAD
    nc = L // CHUNK
    tril, sel = _hgrn_constants()

    def body(hq_ref, hf_ref, hi_ref, hg_ref, lb_ref, gn_ref, tril_ref, sel_ref,
             oh_ref, orec_ref, shist_ref, s_sc, b_sc):
        c = pl.program_id(1)

        @pl.when(c == 0)
        def _():
            s_sc[...] = jnp.zeros_like(s_sc)

        w = _hgrn_gates(hq_ref, hf_ref, hi_ref, lb_ref, c * CHUNK)
        q, k, v = w["q"], w["k"], w["v"]
        b = _dot_exact_rhs(tril_ref[...], w["g"], NN)
        b_sc[...] = b
        b_last = b_sc[CHUNK - 1:CHUNK, :]
        p = _hgrn_intra(q, k, _hgrn_levels(q, k, b, sel_ref))
        s_in = s_sc[...]
        shist_ref[0, 0] = s_in
        o = _dot(p, v, NN) + _dot(q * jnp.exp(b), s_in, NT)
        s_sc[...] = jnp.exp(b_last) * s_in + _dot(v, k * jnp.exp(b_last - b), TN)
        orec_ref[...] = o
        rn = lax.rsqrt(jnp.mean(o * o, axis=-1, keepdims=True) + NORM_EPS)
        hg = hg_ref[...]
        oh_ref[...] = (((o * rn) * gn_ref[...]) * (hg * _sigmoid(hg))).astype(BF16)

    col = lambda grp: pl.BlockSpec((CHUNK, HEAD), lambda h, c: (c, grp * nh + h))
    const = lambda shape: pl.BlockSpec(shape, lambda h, c: (0, 0))
    return pl.pallas_call(
        body, name="hgrn_fwd",
        out_shape=(jax.ShapeDtypeStruct((L, D), BF16), jax.ShapeDtypeStruct((L, D), F32),
                   jax.ShapeDtypeStruct((nc, nh, HEAD, HEAD), F32)),
        grid=(nh, nc),
        in_specs=[col(0), col(1), col(2), col(3),
                  pl.BlockSpec((2, HEAD), lambda h, c: (0, h)), const((1, HEAD)),
                  const((CHUNK, CHUNK)), const((LEVELS * CHUNK, CHUNK))],
        out_specs=(pl.BlockSpec((CHUNK, HEAD), lambda h, c: (c, h)),
                   pl.BlockSpec((CHUNK, HEAD), lambda h, c: (c, h)),
                   pl.BlockSpec((1, 1, HEAD, HEAD), lambda h, c: (c, h, 0, 0))),
        scratch_shapes=[pltpu.VMEM((HEAD, HEAD), F32), pltpu.VMEM((CHUNK, HEAD), F32)],
        compiler_params=_params("parallel", "arbitrary"),
    )(hp, hp, hp, hp, lb_raw, g_norm, tril, sel)


def _hgrn_bwd(hp, lb_raw, g_norm, do_h, o_rec, s_hist, nh):
    L = hp.shape[0]
    D = nh * HEAD
    nc = L // CHUNK
    tril, sel = _hgrn_constants()
    tdec = _hgrn_decay_grad_constants()

    def body(hq_ref, hf_ref, hi_ref, hg_ref, lb_ref, gn_ref, tril_ref, sel_ref, tdec_ref, do_ref, orec_ref, shist_ref,
             dhq_ref, dhf_ref, dhi_ref, dhg_ref, dgn_ref, dlb_ref, ds_sc, b_sc):
        ci = pl.program_id(1)
        c = nc - 1 - ci

        @pl.when(ci == 0)
        def _():
            ds_sc[...] = jnp.zeros_like(ds_sc)
            dgn_ref[...] = jnp.zeros_like(dgn_ref)
            dlb_ref[...] = jnp.zeros_like(dlb_ref)

        w = _hgrn_gates(hq_ref, hf_ref, hi_ref, lb_ref, c * CHUNK)
        q, k, v, valid = w["q"], w["k"], w["v"], w["valid"]
        b = _dot_exact_rhs(tril_ref[...], w["g"], NN)
        b_sc[...] = b
        b_last = b_sc[CHUNK - 1:CHUNK, :]
        levels = _hgrn_levels(q, k, b, sel_ref)
        p = _hgrn_intra(q, k, levels)
        s_in = shist_ref[0, 0]
        ds_out = ds_sc[...]
        eb = jnp.exp(b)
        etail = jnp.exp(b_last - b)
        decay = jnp.exp(b_last)

        o = orec_ref[...]
        hg = hg_ref[...]
        sgg = _sigmoid(hg)
        gate = hg * sgg
        rn = lax.rsqrt(jnp.mean(o * o, axis=-1, keepdims=True) + NORM_EPS)
        on = o * rn
        doh = do_ref[...]
        dy = doh * gate
        dhg_ref[...] = (doh * (on * gn_ref[...]) * (sgg * (1.0 + hg * (1.0 - sgg)))).astype(BF16)
        dgn_ref[0] += jnp.broadcast_to(jnp.sum(dy * on, axis=0, keepdims=True), (8, HEAD))
        z = dy * gn_ref[...]
        do = rn * z - o * ((rn * rn * rn) * jnp.mean(z * o, axis=-1, keepdims=True))

        tt = lax.broadcasted_iota(jnp.int32, (CHUNK, CHUNK), 0)
        ss = lax.broadcasted_iota(jnp.int32, (CHUNK, CHUNK), 1)
        dp = jnp.where(ss <= tt, _dot(do, v, NT), 0.0)
        dv = _dot(p, do, TN) + _dot(k * etail, ds_out, NT)
        dpd = jnp.sum(jnp.where(tt == ss, dp, 0.0), axis=-1, keepdims=True)
        dq_state = eb * _dot(do, s_in, NN)
        dk_state = etail * _dot(v, ds_out, NN)
        dq = dpd * k + dq_state
        dk = dpd * q + dk_state
        pair_terms = []
        for (eq, ek, ql, kl, same) in levels:
            dpl = jnp.where(same, dp, 0.0)
            dq_l = eq * _dot(dpl, kl, NN)
            dk_l = ek * _dot(dpl, ql, TN)
            dq = dq + dq_l
            dk = dk + dk_l
            pair_terms.append(q * dq_l + k * dk_l)
        pair_terms += [q * dq_state, k * dk_state]
        ds_sc[...] = decay * ds_out + _dot(do, q * eb, TN)
        through = jnp.sum((decay * s_in) * ds_out, axis=0, keepdims=True)
        dg = _dot_exact_rhs(tdec_ref[...], jnp.concatenate(pair_terms, axis=0), NN) + through

        f, sg, lb, sq, hq = w["f"], w["sg"], w["lb"], w["sq"], w["hq"]
        df = jnp.where(valid, dg / f - dk, 0.0)
        dhf_ref[...] = (df * (1.0 - lb) * sg * (1.0 - sg)).astype(BF16)
        dlb_ref[0] += jnp.broadcast_to(jnp.sum(df * (1.0 - sg), axis=0, keepdims=True), (8, HEAD))
        dhq_ref[...] = (dq * (sq * (1.0 + hq * (1.0 - sq)))).astype(BF16)
        dhi_ref[...] = dv.astype(BF16)

    col = lambda grp: pl.BlockSpec((CHUNK, HEAD), lambda h, c: (nc - 1 - c, grp * nh + h))
    const = lambda shape: pl.BlockSpec(shape, lambda h, c: (0, 0))
    tile = pl.BlockSpec((CHUNK, HEAD), lambda h, c: (nc - 1 - c, h))
    part = pl.BlockSpec((1, 8, HEAD), lambda h, c: (h, 0, 0))
    return pl.pallas_call(
        body, name="hgrn_bwd",
        out_shape=tuple([jax.ShapeDtypeStruct((L, D), BF16)] * 4 + [jax.ShapeDtypeStruct((nh, 8, HEAD), F32)] * 2),
        grid=(nh, nc),
        in_specs=[col(0), col(1), col(2), col(3),
                  pl.BlockSpec((2, HEAD), lambda h, c: (0, h)), const((1, HEAD)),
                  const((CHUNK, CHUNK)), const((LEVELS * CHUNK, CHUNK)), const((CHUNK, (LEVELS + 2) * CHUNK)),
                  tile, tile, pl.BlockSpec((1, 1, HEAD, HEAD), lambda h, c: (nc - 1 - c, h, 0, 0))],
        out_specs=(tile, tile, tile, tile, part, part),
        scratch_shapes=[pltpu.VMEM((HEAD, HEAD), F32), pltpu.VMEM((CHUNK, HEAD), F32)],
        compiler_params=_params("parallel", "arbitrary"),
    )(hp, hp, hp, hp, lb_raw, g_norm, tril, sel, tdec, do_h, o_rec, s_hist)


def _merge_fwd(a, bm, gates):
    L, D = a.shape
    tm = _tile(L, (128,))

    def body(a_ref, b_ref, g_ref, o_ref):
        o_ref[...] = (_sigmoid(g_ref[:, :D]) * a_ref[...] + _sigmoid(g_ref[:, D:]) * b_ref[...]).astype(BF16)

    row = lambda w: pl.BlockSpec((tm, w), lambda i: (i, 0))
    return pl.pallas_call(
        body, name="merge_fwd", out_shape=jax.ShapeDtypeStruct((L, D), BF16), grid=(L // tm,),
        in_specs=[row(D), row(D), row(2 * D)], out_specs=row(D), compiler_params=_params("parallel"),
    )(a, bm, gates)


def _merge_bwd(dm, a, bm, gates):
    L, D = a.shape
    tm = _tile(L, (128,))

    def body(dm_ref, a_ref, b_ref, g_ref, da_ref, db_ref, dg_ref):
        d = dm_ref[...]
        sa, sb = _sigmoid(g_ref[:, :D]), _sigmoid(g_ref[:, D:])
        da_ref[...] = (d * sa).astype(BF16)
        db_ref[...] = (d * sb).astype(BF16)
        dg_ref[:, :D] = (d * a_ref[...] * sa * (1.0 - sa)).astype(BF16)
        dg_ref[:, D:] = (d * b_ref[...] * sb * (1.0 - sb)).astype(BF16)

    row = lambda w: pl.BlockSpec((tm, w), lambda i: (i, 0))
    return pl.pallas_call(
        body, name="merge_bwd",
        out_shape=(jax.ShapeDtypeStruct((L, D), BF16), jax.ShapeDtypeStruct((L, D), BF16),
                   jax.ShapeDtypeStruct((L, 2 * D), BF16)),
        grid=(L // tm,),
        in_specs=[row(D), row(D), row(D), row(2 * D)], out_specs=(row(D), row(D), row(2 * D)),
        compiler_params=_params("parallel"),
    )(dm, a, bm, gates)


def _conv_taps(g_ref, halo_ref, i, tm):
    rows = i * tm + lax.broadcasted_iota(jnp.int32, (tm, 1), 0)
    g0 = jnp.where(rows >= PAD_LEN, g_ref[...], 0.0)
    hrow = i * tm - 8 + lax.broadcasted_iota(jnp.int32, (8, 1), 0)
    halo = jnp.where(hrow >= PAD_LEN, halo_ref[...], 0.0)
    r = lax.broadcasted_iota(jnp.int32, (tm, 1), 0)
    h7 = jnp.sum(jnp.where(lax.broadcasted_iota(jnp.int32, (8, 1), 0) == 7, halo, 0.0), axis=0, keepdims=True)
    h6 = jnp.sum(jnp.where(lax.broadcasted_iota(jnp.int32, (8, 1), 0) == 6, halo, 0.0), axis=0, keepdims=True)
    g1 = jnp.where(r == 0, h7, pltpu.roll(g0, 1, 0))
    g2 = jnp.where(r == 0, h6, jnp.where(r == 1, h7, pltpu.roll(g0, 2, 0)))
    return g0, g1, g2


def _conv_fwd(gu, cw, cb):
    L, F2 = gu.shape
    F = F2 // 2
    tm = _tile(L, ROW_TILES)
    tn = _tile(F, (512, 256, 128))
    nj = F // tn

    def body(g_ref, halo_ref, u_ref, cw_ref, cb_ref, o_ref):
        g0, g1, g2 = _conv_taps(g_ref, halo_ref, pl.program_id(0), tm)
        conv = cw_ref[0:1, :] * g2 + cw_ref[1:2, :] * g1 + cw_ref[2:3, :] * g0 + cb_ref[...]
        o_ref[...] = (conv * _sigmoid(conv) * u_ref[...]).astype(BF16)

    return pl.pallas_call(
        body, name="conv_fwd", out_shape=jax.ShapeDtypeStruct((L, F), BF16), grid=(L // tm, nj),
        in_specs=[pl.BlockSpec((tm, tn), lambda i, j: (i, j)),
                  pl.BlockSpec((8, tn), lambda i, j: (jnp.maximum(i * (tm // 8) - 1, 0), j)),
                  pl.BlockSpec((tm, tn), lambda i, j: (i, j + nj)),
                  pl.BlockSpec((3, tn), lambda i, j: (0, j)),
                  pl.BlockSpec((1, tn), lambda i, j: (0, j))],
        out_specs=pl.BlockSpec((tm, tn), lambda i, j: (i, j)),
        compiler_params=_params("parallel", "parallel"),
    )(gu, gu, gu, cw, cb)


def _conv_bwd_a(da, gu, cw, cb):
    L, F2 = gu.shape
    F = F2 // 2
    tm = _tile(L, ROW_TILES)
    tn = _tile(F, (512, 256, 128))
    nj = F // tn

    def body(da_ref, g_ref, halo_ref, u_ref, cw_ref, cb_ref, dc_ref, du_ref, dcb_ref, dcw_ref):
        i = pl.program_id(1)

        @pl.when(i == 0)
        def _():
            dcb_ref[...] = jnp.zeros_like(dcb_ref)
            dcw_ref[...] = jnp.zeros_like(dcw_ref)

        g0, g1, g2 = _conv_taps(g_ref, halo_ref, i, tm)
        conv = cw_ref[0:1, :] * g2 + cw_ref[1:2, :] * g1 + cw_ref[2:3, :] * g0 + cb_ref[...]
        sc = _sigmoid(conv)
        dav = da_ref[...]
        du_ref[...] = (dav * (conv * sc)).astype(BF16)
        dconv = dav * u_ref[...] * (sc * (1.0 + conv * (1.0 - sc)))
        dc_ref[...] = dconv
        dcb_ref[...] += jnp.sum(dconv, axis=0, keepdims=True)
        dcw_ref[0:1, :] += jnp.sum(dconv * g2, axis=0, keepdims=True)
        dcw_ref[1:2, :] += jnp.sum(dconv * g1, axis=0, keepdims=True)
        dcw_ref[2:3, :] += jnp.sum(dconv * g0, axis=0, keepdims=True)

    return pl.pallas_call(
        body, name="conv_bwd_a",
        out_shape=(jax.ShapeDtypeStruct((L, F), F32), jax.ShapeDtypeStruct((L, F), BF16),
                   jax.ShapeDtypeStruct((1, F), F32), jax.ShapeDtypeStruct((8, F), F32)),
        grid=(nj, L // tm),
        in_specs=[pl.BlockSpec((tm, tn), lambda j, i: (i, j)),
                  pl.BlockSpec((tm, tn), lambda j, i: (i, j)),
                  pl.BlockSpec((8, tn), lambda j, i: (jnp.maximum(i * (tm // 8) - 1, 0), j)),
                  pl.BlockSpec((tm, tn), lambda j, i: (i, j + nj)),
                  pl.BlockSpec((3, tn), lambda j, i: (0, j)),
                  pl.BlockSpec((1, tn), lambda j, i: (0, j))],
        out_specs=(pl.BlockSpec((tm, tn), lambda j, i: (i, j)), pl.BlockSpec((tm, tn), lambda j, i: (i, j)),
                   pl.BlockSpec((1, tn), lambda j, i: (0, j)), pl.BlockSpec((8, tn), lambda j, i: (0, j))),
        compiler_params=_params("parallel", "arbitrary"),
    )(da, gu, gu, gu, cw, cb)


def _conv_bwd_b(dconv, cw):
    L, F = dconv.shape
    tm = _tile(L, ROW_TILES)
    tn = _tile(F, (512, 256, 128))
    nblk8 = L // 8
    ni = L // tm

    def body(dc_ref, nxt_ref, cw_ref, o_ref):
        i = pl.program_id(0)
        dc = dc_ref[...]
        nxt = jnp.where(i < ni - 1, nxt_ref[...], 0.0)
        sub = lax.broadcasted_iota(jnp.int32, (8, 1), 0)
        n0 = jnp.sum(jnp.where(sub == 0, nxt, 0.0), axis=0, keepdims=True)
        n1 = jnp.sum(jnp.where(sub == 1, nxt, 0.0), axis=0, keepdims=True)
        r = lax.broadcasted_iota(jnp.int32, (tm, 1), 0)
        d1 = jnp.where(r == tm - 1, n0, pltpu.roll(dc, tm - 1, 0))
        d2 = jnp.where(r == tm - 2, n0, jnp.where(r == tm - 1, n1, pltpu.roll(dc, tm - 2, 0)))
        dg = cw_ref[2:3, :] * dc + cw_ref[1:2, :] * d1 + cw_ref[0:1, :] * d2
        rows = i * tm + r
        o_ref[...] = jnp.where(rows >= PAD_LEN, dg, 0.0).astype(BF16)

    return pl.pallas_call(
        body, name="conv_bwd_b", out_shape=jax.ShapeDtypeStruct((L, F), BF16), grid=(ni, F // tn),
        in_specs=[pl.BlockSpec((tm, tn), lambda i, j: (i, j)),
                  pl.BlockSpec((8, tn), lambda i, j: (jnp.minimum((i + 1) * (tm // 8), nblk8 - 1), j)),
                  pl.BlockSpec((3, tn), lambda i, j: (0, j))],
        out_specs=pl.BlockSpec((tm, tn), lambda i, j: (i, j)),
        compiler_params=_params("parallel", "parallel"),
    )(dconv, dconv, cw)


ANY = pl.BlockSpec(memory_space=pl.ANY)


def _coords():
    return lax.axis_index("x"), lax.axis_index("y"), lax.axis_index("c")


def _flip(v, bit):
    return 1 - v if bit else v


CHIPS = [(1, 0), (0, 1), (1, 1)]
PEERS = [(dx, dy, dc) for dx in (0, 1) for dy in (0, 1) for dc in (0, 1)][1:]


def _gather_shards(big, small):
    nbig, n = len(big), len(big) + len(small)
    arrays = list(big) + list(small)

    def body(*refs):
        ins, outs = refs[:n], refs[n:2 * n]
        ici_send, ici_recv, d2d_send, d2d_recv, local_sems = refs[2 * n:]
        x, y, c = _coords()
        mine = 2 * x + y

        def half(w, h):
            r2 = arrays[w].shape[0] // 2
            return pl.ds(h * r2, r2)

        def ici(w, j, landing):
            px, py = _flip(x, CHIPS[j][0]), _flip(y, CHIPS[j][1])
            slot = 2 * px + py if landing else mine
            if w < nbig:
                src, dst = ins[w].at[half(w, c)], outs[w].at[slot, half(w, c)]
            else:
                src, dst = ins[w], outs[w].at[slot]
            return pltpu.make_async_remote_copy(
                src_ref=src, dst_ref=dst, send_sem=ici_send.at[w * 3 + j], recv_sem=ici_recv.at[w * 3 + j],
                device_id=(px, py, c), device_id_type=MESH)

        def d2d(w, j, landing):
            px, py = _flip(x, CHIPS[j][0]), _flip(y, CHIPS[j][1])
            mine_rows = outs[w].at[2 * px + py, half(w, c)]
            dst = outs[w].at[2 * px + py, half(w, 1 - c)] if landing else mine_rows
            return pltpu.make_async_remote_copy(
                src_ref=mine_rows, dst_ref=dst, send_sem=d2d_send.at[w * 3 + j], recv_sem=d2d_recv.at[w * 3 + j],
                device_id=(x, y, 1 - c), device_id_type=MESH)

        local = [pltpu.make_async_copy(ins[w], outs[w].at[mine], local_sems.at[w]) for w in range(n)]
        for cp in local:
            cp.start()
        for w in range(n):
            for j in range(3):
                ici(w, j, False).start()
        for w in range(n):
            for j in range(3):
                ici(w, j, True).wait_recv()
                if w < nbig:
                    d2d(w, j, False).start()
        for w in range(nbig):
            for j in range(3):
                d2d(w, j, True).wait_recv()
        for w in range(n):
            for j in range(3):
                ici(w, j, False).wait_send()
                if w < nbig:
                    d2d(w, j, False).wait_send()
        for cp in local:
            cp.wait()

    return pl.pallas_call(
        body, name="gather_shards",
        out_shape=tuple(jax.ShapeDtypeStruct((4,) + s.shape, s.dtype) for s in arrays),
        in_specs=[ANY] * n, out_specs=tuple([ANY] * n),
        scratch_shapes=[pltpu.SemaphoreType.DMA((3 * n,)), pltpu.SemaphoreType.DMA((3 * n,)),
                        pltpu.SemaphoreType.DMA((3 * nbig,)), pltpu.SemaphoreType.DMA((3 * nbig,)),
                        pltpu.SemaphoreType.DMA((n,))],
        compiler_params=pltpu.CompilerParams(has_side_effects=True),
    )(*arrays)


def _to_sibling(arrays, name):
    n = len(arrays)

    def body(*refs):
        ins, outs = refs[:n], refs[n:2 * n]
        send_sems, recv_sems = refs[2 * n:]
        x, y, c = _coords()

        def copy(w):
            return pltpu.make_async_remote_copy(
                src_ref=ins[w], dst_ref=outs[w], send_sem=send_sems.at[w], recv_sem=recv_sems.at[w],
                device_id=(x, y, 1 - c), device_id_type=MESH)

        for w in range(n):
            copy(w).start()
        for w in range(n):
            copy(w).wait_recv()
            copy(w).wait_send()

    return pl.pallas_call(
        body, name=name,
        out_shape=tuple(jax.ShapeDtypeStruct(a.shape, a.dtype) for a in arrays),
        in_specs=[ANY] * n, out_specs=tuple([ANY] * n),
        scratch_shapes=[pltpu.SemaphoreType.DMA((n,)), pltpu.SemaphoreType.DMA((n,))],
        compiler_params=pltpu.CompilerParams(has_side_effects=True),
    )(*arrays)


def _pair_sum(a, b, name):
    _, r, c = a.shape
    tr = _tile(r, (64, 32, 16))

    def body(a_ref, b_ref, o_ref):
        o_ref[...] = (a_ref[...].astype(F32) + b_ref[...].astype(F32)).astype(BF16)

    blk = pl.BlockSpec((4, tr, c), lambda i: (0, i, 0))
    return pl.pallas_call(
        body, name=name, out_shape=jax.ShapeDtypeStruct(a.shape, BF16), grid=(r // tr,),
        in_specs=[blk, blk], out_specs=blk, compiler_params=_params("parallel"),
    )(a, b)


def _scatter_chip_parts(parts):
    n = len(parts)

    def body(*refs):
        ins, outs = refs[:n], refs[n:2 * n]
        send_sems, recv_sems, local_sems = refs[2 * n:]
        x, y, c = _coords()
        mine = 2 * x + y

        def ici(w, j, landing):
            px, py = _flip(x, CHIPS[j][0]), _flip(y, CHIPS[j][1])
            return pltpu.make_async_remote_copy(
                src_ref=ins[w].at[2 * px + py], dst_ref=outs[w].at[2 * px + py if landing else mine],
                send_sem=send_sems.at[w * 3 + j], recv_sem=recv_sems.at[w * 3 + j],
                device_id=(px, py, c), device_id_type=MESH)

        local = [pltpu.make_async_copy(ins[w].at[mine], outs[w].at[mine], local_sems.at[w]) for w in range(n)]
        for cp in local:
            cp.start()
        for w in range(n):
            for j in range(3):
                ici(w, j, False).start()
        for w in range(n):
            for j in range(3):
                ici(w, j, True).wait_recv()
                ici(w, j, False).wait_send()
        for cp in local:
            cp.wait()

    return pl.pallas_call(
        body, name="scatter_chip_parts",
        out_shape=tuple(jax.ShapeDtypeStruct(p.shape, p.dtype) for p in parts),
        in_specs=[ANY] * n, out_specs=tuple([ANY] * n),
        scratch_shapes=[pltpu.SemaphoreType.DMA((3 * n,)), pltpu.SemaphoreType.DMA((3 * n,)),
                        pltpu.SemaphoreType.DMA((n,))],
        compiler_params=pltpu.CompilerParams(has_side_effects=True),
    )(*parts)


def _sum4(recv, name):
    _, r, c = recv.shape
    tr = _tile(r, (64, 32, 16))

    def body(in_ref, o_ref):
        o_ref[...] = ((in_ref[0].astype(F32) + in_ref[1].astype(F32)) + in_ref[2].astype(F32)) + in_ref[3].astype(F32)

    return pl.pallas_call(
        body, name=name, out_shape=jax.ShapeDtypeStruct((r, c), F32), grid=(r // tr,),
        in_specs=[pl.BlockSpec((4, tr, c), lambda i: (0, i, 0))],
        out_specs=pl.BlockSpec((tr, c), lambda i: (i, 0)),
        compiler_params=_params("parallel"),
    )(recv)


def _allreduce_small(packed):
    R = packed.shape[0]

    def body(in_ref, o_ref, buf, send_sems, recv_sems):
        x, y, c = _coords()
        me = 4 * x + 2 * y + c
        buf[me] = in_ref[...]
        for j, (dx, dy, dc) in enumerate(PEERS):
            px, py, pc = _flip(x, dx), _flip(y, dy), _flip(c, dc)
            pltpu.make_async_remote_copy(
                src_ref=in_ref, dst_ref=buf.at[me], send_sem=send_sems.at[j], recv_sem=recv_sems.at[j],
                device_id=(px, py, pc), device_id_type=MESH).start()
        for j, (dx, dy, dc) in enumerate(PEERS):
            px, py, pc = _flip(x, dx), _flip(y, dy), _flip(c, dc)
            rc = pltpu.make_async_remote_copy(
                src_ref=in_ref, dst_ref=buf.at[4 * px + 2 * py + pc], send_sem=send_sems.at[j],
                recv_sem=recv_sems.at[j], device_id=(px, py, pc), device_id_type=MESH)
            rc.wait_recv()
            rc.wait_send()
        acc = buf[0]
        for s in range(1, 8):
            acc = acc + buf[s]
        o_ref[...] = acc

    return pl.pallas_call(
        body, name="allreduce_small", out_shape=jax.ShapeDtypeStruct((R, 128), F32),
        in_specs=[pl.BlockSpec(memory_space=pltpu.VMEM)], out_specs=pl.BlockSpec(memory_space=pltpu.VMEM),
        scratch_shapes=[pltpu.VMEM((8, R, 128), F32), pltpu.SemaphoreType.DMA((7,)), pltpu.SemaphoreType.DMA((7,))],
        compiler_params=pltpu.CompilerParams(has_side_effects=True, vmem_limit_bytes=VMEM_LIMIT_BYTES),
    )(packed)


def _adamw_math(w, g, m, v):
    m = ADAM_B1 * m + (1.0 - ADAM_B1) * g
    v = ADAM_B2 * v + (1.0 - ADAM_B2) * (g * g)
    m_hat = m / (1.0 - ADAM_B1 ** ADAM_STEP)
    v_hat = v / (1.0 - ADAM_B2 ** ADAM_STEP)
    delta = -ADAM_LR * (m_hat / (jnp.sqrt(v_hat) + ADAM_EPS) + ADAM_WD * w)
    return delta, m, v


def _adamw_big(mine, other, w, m, v, name):
    R, C = w.shape
    tr = _tile(R // 2, (128, 64, 32, 16, 8))
    nb = (R // 2) // tr

    def body(mine_ref, other_ref, w_ref, m_ref, v_ref, g_ref, d_ref, mo_ref, vo_ref):
        is_mine = (pl.program_id(0) // nb) == lax.axis_index("c")
        g = jnp.where(is_mine, mine_ref[...], other_ref[...])
        d, mn, vn = _adamw_math(w_ref[...], g, m_ref[...], v_ref[...])
        g_ref[...] = g
        d_ref[...] = d
        mo_ref[...] = mn
        vo_ref[...] = vn

    blk = pl.BlockSpec((tr, C), lambda i: (i, 0))
    half = pl.BlockSpec((tr, C), lambda i: (i % nb, 0))
    sds = jax.ShapeDtypeStruct((R, C), F32)
    return pl.pallas_call(
        body, name=name, out_shape=(sds, sds, sds, sds), grid=(2 * nb,),
        in_specs=[half, half, blk, blk, blk], out_specs=(blk, blk, blk, blk), compiler_params=_params("parallel"),
    )(mine, other, w, m, v)


def _adamw_small(items, lb_raw, dlb):
    n = len(items)
    lb_w, lb_m, lb_v = lb_raw

    def body(*refs):
        ins = refs[:4 * n]
        dlb_ref, lw_ref, lm_ref, lv_ref = refs[4 * n:4 * n + 4]
        outs = refs[4 * n + 4:]
        for t in range(n):
            g_ref, w_ref, m_ref, v_ref = ins[4 * t:4 * t + 4]
            d, mn, vn = _adamw_math(w_ref[...], g_ref[...], m_ref[...], v_ref[...])
            outs[3 * t][...] = d
            outs[3 * t + 1][...] = mn
            outs[3 * t + 2][...] = vn
        p0 = 1.0 / (1.0 + jnp.exp(lw_ref[1:2, :] - lw_ref[0:1, :]))
        g0 = dlb_ref[...] * p0 * (1.0 - p0)
        base = 3 * n
        outs[base][0:1, :] = g0
        outs[base][1:2, :] = -g0
        d, mn, vn = _adamw_math(lw_ref[...], outs[base][...], lm_ref[...], lv_ref[...])
        outs[base + 1][...] = d
        outs[base + 2][...] = mn
        outs[base + 3][...] = vn

    operands = [a for it in items for a in it] + [dlb, lb_w, lb_m, lb_v]
    out_shape = []
    for (g, w, m, v) in items:
        out_shape += [jax.ShapeDtypeStruct(w.shape, F32)] * 3
    out_shape += [jax.ShapeDtypeStruct(lb_w.shape, F32)] * 4
    vm = pl.BlockSpec(memory_space=pltpu.VMEM)
    res = pl.pallas_call(
        body, name="adamw_small", out_shape=tuple(out_shape),
        in_specs=[vm] * len(operands), out_specs=tuple([vm] * len(out_shape)),
        compiler_params=pltpu.CompilerParams(vmem_limit_bytes=VMEM_LIMIT_BYTES),
    )(*operands)
    deltas = [res[3 * t] for t in range(n)] + [res[3 * n + 1]]
    new_m = [res[3 * t + 1] for t in range(n)] + [res[3 * n + 2]]
    new_v = [res[3 * t + 2] for t in range(n)] + [res[3 * n + 3]]
    return res[3 * n], deltas, new_m, new_v


def _shard_row_half(g, by_cols, h):
    R, C = g.shape
    if by_cols:
        part = lax.dynamic_index_in_dim(g.reshape(2, R // 2, 4, C // 4), h, axis=0, keepdims=False)
        return part.transpose(1, 0, 2).astype(BF16)
    return lax.dynamic_index_in_dim(g.reshape(4, 2, R // 8, C), h, axis=1, keepdims=False).astype(BF16)


def kernel(x, positions, meta_tokens, w_in, w_q_up, w_kv_up, w_branch_mla, w_branch_hgrn, w_out, w_ffn_in, w_ffn_out, conv_w, conv_b, g_mix_norm, g_q_norm, g_kv_norm, g_hgrn_norm, g_ffn_norm, g_final_norm, lb_raw, loss_target, m_meta_tokens, m_w_in, m_w_q_up, m_w_kv_up, m_w_branch_mla, m_w_branch_hgrn, m_w_out, m_w_ffn_in, m_w_ffn_out, m_conv_w, m_conv_b, m_g_mix_norm, m_g_q_norm, m_g_kv_norm, m_g_hgrn_norm, m_g_ffn_norm, m_g_final_norm, m_lb_raw, v_meta_tokens, v_w_in, v_w_q_up, v_w_kv_up, v_w_branch_mla, v_w_branch_hgrn, v_w_out, v_w_ffn_in, v_w_ffn_out, v_conv_w, v_conv_b, v_g_mix_norm, v_g_q_norm, v_g_kv_norm, v_g_hgrn_norm, v_g_ffn_norm, v_g_final_norm, v_lb_raw):
    S, D = x.shape[1], x.shape[2]
    L = S + PREFIX
    QL, KVL = g_q_norm.shape[1], g_kv_norm.shape[1]
    F = conv_b.shape[1]
    heads = (4 * w_kv_up.shape[2]) // QPAD
    nh = D // HEAD
    assert lb_raw.shape[0] == 2 and g_hgrn_norm.shape[1] == HEAD and L % CHUNK == 0
    ix, iy, ic = _coords()
    chip = 2 * ix + iy

    big = [w_in, w_q_up, w_kv_up, w_branch_mla, w_branch_hgrn, w_out, w_ffn_in, w_ffn_out]
    col_sharded = [True, True, True, False, False, False, True, False]
    gathered = _gather_shards([w[0].astype(BF16) for w in big], [meta_tokens, conv_w[0]])

    def full(gw, by_cols):
        _, r, c = gw.shape
        return gw.transpose(1, 0, 2).reshape(r, 4 * c) if by_cols else gw.reshape(4 * r, c)

    W_in, W_q, W_kv, W_a, W_b, W_o, W_fi, W_fo = [full(gw, bc) for gw, bc in zip(gathered[:8], col_sharded)]
    meta_full = full(gathered[8], True)
    cw_full = full(gathered[9], True)
    c0 = QL + KVL
    W_lat = W_in[:, :c0]
    W_kr = jnp.pad(W_in[:, c0:c0 + ROPE], ((0, 0), (0, 128 - ROPE)))
    W_H = W_in[:, c0 + ROPE:c0 + ROPE + 4 * D]
    W_G = W_in[:, c0 + ROPE + 4 * D:]
    W_qp = jnp.pad(W_q.reshape(QL, heads, QK_HEAD), ((0, 0), (0, 0), (0, QPAD - QK_HEAD))).reshape(QL, heads * QPAD)

    h0 = jnp.concatenate([jnp.zeros((PAD_LEN, D), F32), meta_full, x[0]], axis=0)
    pos = jnp.concatenate([jnp.zeros((PAD_LEN,), jnp.int32), jnp.arange(N_META, dtype=jnp.int32),
                           positions[0].astype(jnp.int32) + N_META])
    inv = 1.0 / (ROPE_THETA ** (jnp.arange(0, ROPE, 2, dtype=F32) / ROPE))
    ang = pos.astype(F32)[:, None] * inv
    zero = jnp.zeros((L, 128 - ROPE), F32)
    cos = jnp.concatenate([jnp.cos(ang), jnp.cos(ang), zero], axis=1)
    sin = jnp.concatenate([jnp.sin(ang), jnp.sin(ang), zero], axis=1)

    u1 = _rmsnorm_fwd(h0, g_mix_norm, "norm_mix")
    lat = _mm(u1, W_lat, "nn", F32, "proj_lat")
    hp = _mm(u1, W_H, "nn", F32, "proj_hgrn")
    gates = _mm(u1, W_G, "nn", F32, "proj_gates")
    kr = _mm(u1, W_kr, "nn", F32, "proj_krope")
    qn = _rmsnorm_fwd(lat, g_q_norm, "norm_q", col0=0, width=QL)
    kvn = _rmsnorm_fwd(lat, g_kv_norm, "norm_kv", col0=QL, width=KVL)
    qp = _mm(qn, W_qp, "nn", F32, "q_up")
    kv = _mm(kvn, W_kv, "nn", F32, "kv_up")
    qc, kc, vv = _rope_fwd(qp, kv, kr, cos, sin, heads)
    o_mla, lse = _attn_fwd(qc, kc, vv, heads)
    o_hgrn, o_rec, s_hist = _hgrn_fwd(hp, lb_raw, g_hgrn_norm, nh)
    br_a = _mm(o_mla, W_a, "nn", F32, "branch_mla")
    br_b = _mm(o_hgrn, W_b, "nn", F32, "branch_hgrn")
    merged = _merge_fwd(br_a, br_b, gates)
    h1 = _mm(merged, W_o, "nn", F32, "out_proj", res=h0)
    u2 = _rmsnorm_fwd(h1, g_ffn_norm, "norm_ffn")
    gu = _mm(u2, W_fi, "nn", F32, "ffn_in")
    act = _conv_fwd(gu, cw_full, conv_b)
    h2 = _mm(act, W_fo, "nn", F32, "ffn_out", res=h1)
    dh2, loss_p, dg_final = _final_loss_bwd(h2, loss_target[0], g_final_norm.reshape(1, D))

    dact = _mm(dh2, W_fo, "nt", F32, "d_act")
    dW_fo = _mm(act, dh2, "tn", F32, "dw_ffn_out")
    dconv, dup, dcb, dcw = _conv_bwd_a(dact, gu, cw_full, conv_b)
    dgate = _conv_bwd_b(dconv, cw_full)
    dgu = jnp.concatenate([dgate, dup], axis=1)
    du2 = _mm(dgu, W_fi, "nt", F32, "d_u2")
    dW_fi = _mm(u2, dgu, "tn", F32, "dw_ffn_in")
    dh1, dg_ffn = _rmsnorm_bwd(du2, h1, g_ffn_norm, "norm_ffn_bwd", F32, res=dh2)
    dmerged = _mm(dh1, W_o, "nt", F32, "d_merged")
    dW_o = _mm(merged, dh1, "tn", F32, "dw_out")
    d_a, d_b, d_gates = _merge_bwd(dmerged, br_a, br_b, gates)
    do_mla = _mm(d_a, W_a, "nt", BF16, "d_o_mla")
    dW_a = _mm(o_mla, d_a, "tn", F32, "dw_branch_mla")
    do_hgrn = _mm(d_b, W_b, "nt", F32, "d_o_hgrn")
    dW_b = _mm(o_hgrn, d_b, "tn", F32, "dw_branch_hgrn")
    dhq, dhf, dhi, dhg, dgn_p, dlb_p = _hgrn_bwd(hp, lb_raw, g_hgrn_norm, do_hgrn, o_rec, s_hist, nh)
    dqc, dkc, dvv = _attn_bwd(qc, kc, vv, o_mla, do_mla, lse, heads)
    dqp, dkv, dkr = _rope_bwd(dqc, dkc, dvv, cos, sin, heads)
    dqn = _mm(dqp, W_qp, "nt", F32, "d_qn")
    dW_qp = _mm(qn, dqp, "tn", F32, "dw_q_up")
    dkvn = _mm(dkv, W_kv, "nt", F32, "d_kvn")
    dW_kv = _mm(kvn, dkv, "tn", F32, "dw_kv_up")
    dq_lat, dg_q = _rmsnorm_bwd(dqn, lat, g_q_norm, "norm_q_bwd", BF16, col0=0)
    dkv_lat, dg_kv = _rmsnorm_bwd(dkvn, lat, g_kv_norm, "norm_kv_bwd", BF16, col0=QL)
    dlat = jnp.concatenate([dq_lat, dkv_lat], axis=1)
    dhp = jnp.concatenate([dhq, dhf, dhi, dhg], axis=1)
    du1 = _mm(dlat, W_lat, "nt", F32, "d_u1_lat")
    du1 = _mm(dhp, W_H, "nt", F32, "d_u1_hgrn", res=du1)
    du1 = _mm(d_gates, W_G, "nt", F32, "d_u1_gates", res=du1)
    du1 = _mm(dkr, W_kr, "nt", F32, "d_u1_krope", res=du1)
    dW_lat = _mm(u1, dlat, "tn", F32, "dw_in_lat")
    dW_H = _mm(u1, dhp, "tn", F32, "dw_in_hgrn")
    dW_G = _mm(u1, d_gates, "tn", F32, "dw_in_gates")
    dW_kr = _mm(u1, dkr, "tn", F32, "dw_in_krope")
    dh0, dg_mix = _rmsnorm_bwd(du1, h0, g_mix_norm, "norm_mix_bwd", F32, res=dh1)
    grad_x = dh0[PREFIX:][None]

    dW_in = jnp.concatenate([dW_lat, dW_kr[:, :ROPE], dW_H, dW_G], axis=1)
    dW_q = dW_qp.reshape(QL, heads, QPAD)[:, :, :QK_HEAD].reshape(QL, heads * QK_HEAD)
    big_grads = [dW_in, dW_q, dW_kv, dW_a, dW_b, dW_o, dW_fi, dW_fo]
    names = ["w_in", "w_q_up", "w_kv_up", "w_branch_mla", "w_branch_hgrn", "w_out", "w_ffn_in", "w_ffn_out"]
    keep = [_shard_row_half(g, bc, ic) for g, bc in zip(big_grads, col_sharded)]
    give = [_shard_row_half(g, bc, 1 - ic) for g, bc in zip(big_grads, col_sharded)]
    taken = _to_sibling(give, "pair_exchange")
    chip_parts = [_pair_sum(a, b, "pair_sum_" + nm) for a, b, nm in zip(keep, taken, names)]
    received = _scatter_chip_parts(chip_parts)
    halves = [_sum4(r, "sum_" + nm) for r, nm in zip(received, names)]
    others = _to_sibling(halves, "swap_halves")
    big_m = [m_w_in, m_w_q_up, m_w_kv_up, m_w_branch_mla, m_w_branch_hgrn, m_w_out, m_w_ffn_in, m_w_ffn_out]
    big_v = [v_w_in, v_w_q_up, v_w_kv_up, v_w_branch_mla, v_w_branch_hgrn, v_w_out, v_w_ffn_in, v_w_ffn_out]
    big_out = {}
    for nm, mine, other, w, m, v in zip(names, halves, others, big, big_m, big_v):
        g, d, mn, vn = _adamw_big(mine, other, w[0], m[0], v[0], "adamw_" + nm)
        big_out[nm] = (g[None], d[None], mn[None], vn[None])

    pieces = [loss_p[:, :1], dg_mix, dg_q, dg_kv, jnp.sum(dgn_p[:, 0, :], axis=0, keepdims=True), dg_ffn, dg_final,
              dlb_p[:, 0, :].reshape(1, D), dcb, dcw[0:3].reshape(1, 3 * F), dh0[PAD_LEN:PREFIX].reshape(1, N_META * D)]
    sizes = [p.shape[1] for p in pieces]
    flat = jnp.concatenate(pieces, axis=1)[0]
    rows = -(-flat.shape[0] // 1024) * 8
    packed = jnp.pad(flat, (0, rows * 128 - flat.shape[0])).reshape(rows, 128)
    total = _allreduce_small(packed).reshape(-1)
    offs = [0]
    for s in sizes:
        offs.append(offs[-1] + s)
    loss, g_mix, g_q, g_kv, g_hg, g_ffn, g_fin, dlb, g_cb, g_cw, g_meta = [
        total[offs[t]:offs[t + 1]].reshape(1, sizes[t]) for t in range(len(sizes))]
    g_cw = lax.dynamic_slice_in_dim(g_cw.reshape(3, F), chip * (F // 4), F // 4, axis=1)
    g_meta = lax.dynamic_slice_in_dim(g_meta.reshape(N_META, D), chip * (D // 4), D // 4, axis=1)
    items = [(g_meta, meta_tokens, m_meta_tokens, v_meta_tokens),
             (g_cw, conv_w[0], m_conv_w[0], v_conv_w[0]),
             (g_cb, conv_b, m_conv_b, v_conv_b),
             (g_mix, g_mix_norm, m_g_mix_norm, v_g_mix_norm),
             (g_q, g_q_norm, m_g_q_norm, v_g_q_norm),
             (g_kv, g_kv_norm, m_g_kv_norm, v_g_kv_norm),
             (g_hg, g_hgrn_norm, m_g_hgrn_norm, v_g_hgrn_norm),
             (g_ffn, g_ffn_norm, m_g_ffn_norm, v_g_ffn_norm),
             (g_fin, g_final_norm.reshape(1, D), m_g_final_norm.reshape(1, D), v_g_final_norm.reshape(1, D))]
    g_lb, s_delta, s_m, s_v = _adamw_small(items, (lb_raw, m_lb_raw, v_lb_raw), dlb)
    s_grads = [it[0] for it in items] + [g_lb]

    def shape_small(vals):
        meta, cw, cb, mix, q, kvg, hg, ffn, fin, lb = vals
        return [meta, cw[None], cb, mix, q, kvg, hg, ffn, fin.reshape(D), lb]

    s_grads, s_delta, s_m, s_v = [shape_small(v) for v in (s_grads, s_delta, s_m, s_v)]

    def ordered(kind, small):
        bigs = [big_out[nm][kind] for nm in names]
        return [small[0]] + bigs + small[1:]

    return (loss.reshape(()), grad_x, *ordered(0, s_grads), *ordered(1, s_delta), *ordered(2, s_m), *ordered(3, s_v))
```

```python
import math

import jax
import jax.numpy as jnp
from jax import lax
from jax.experimental import pallas as pl
from jax.experimental.pallas import tpu as pltpu

F32 = jnp.float32
BF16 = jnp.bfloat16
MESH = pl.DeviceIdType.MESH

NORM_EPS = 1e-6
N_META = 16
PREFIX = 128
PAD_LEN = PREFIX - N_META
HEAD = 128
ROPE = 64
QK_HEAD = HEAD + ROPE
QPAD = 2 * HEAD
SOFTMAX_SCALE = QK_HEAD ** -0.5
ROPE_THETA = 10000.0
CHUNK = 128
LEVELS = 7
HGRN_FWD_HEADS = 4
HGRN_BWD_HEADS = 2
NEG = -1e30

ADAM_LR = 0.001
ADAM_B1 = 0.9
ADAM_B2 = 0.999
ADAM_EPS = 1e-08
ADAM_WD = 0.01
ADAM_STEP = 10

VMEM_LIMIT_BYTES = 48 * 1024 * 1024


def _params(*sem):
    return pltpu.CompilerParams(dimension_semantics=sem, vmem_limit_bytes=VMEM_LIMIT_BYTES)


def _tile(n, prefs):
    for p in prefs:
        if n % p == 0:
            return p
    return n


ROW_TILES = (640, 512, 256, 128, 64, 32, 16, 8)
TN_ROW_TILES = (1024, 1408, 1536, 512, 256, 128)
COL_TILES = (1024, 512, 256, 128)
K_TILES = (2048, 1536, 1408, 1024, 512, 256, 128)


def _sigmoid(x):
    return 1.0 / (1.0 + jnp.exp(-x))


def _dot(a, b, dims):
    return lax.dot_general(a.astype(BF16), b.astype(BF16), (dims, ((), ())), preferred_element_type=F32)


NN = ((1,), (0,))
TN = ((0,), (0,))
NT = ((1,), (1,))


def _split3(x):
    hi = x.astype(BF16)
    r = x - hi.astype(F32)
    mid = r.astype(BF16)
    lo = (r - mid.astype(F32)).astype(BF16)
    return hi, mid, lo


def _dot_exact_rhs(sel, x, dims):
    hi, mid, lo = _split3(x)
    return _dot(sel, hi, dims) + _dot(sel, mid, dims) + _dot(sel, lo, dims)


def _mm(a, b, mode, out_dtype, name, res=None):
    if mode == "nn":
        (M, K), (K2, N) = a.shape, b.shape
    elif mode == "tn":
        (K, M), (K2, N) = a.shape, b.shape
    else:
        (M, K), (N, K2) = a.shape, b.shape
    assert K == K2, (name, a.shape, b.shape)
    tm = _tile(M, TN_ROW_TILES if mode == "tn" else ROW_TILES)
    tn = _tile(N, COL_TILES)
    tk = _tile(K, ROW_TILES if mode == "tn" else K_TILES)
    nk = K // tk
    dims = {"nn": NN, "tn": TN, "nt": NT}[mode]

    def body(*refs):
        a_ref, b_ref = refs[:2]
        r_ref = None if res is None else refs[2]
        o_ref = refs[2 if res is None else 3]

        def finish(r):
            if r_ref is not None:
                r = r + r_ref[...].astype(F32)
            o_ref[...] = r.astype(out_dtype)

        if nk == 1:
            finish(_dot(a_ref[...], b_ref[...], dims))
            return
        acc = refs[-1]
        k = pl.program_id(2)

        @pl.when(k == 0)
        def _():
            acc[...] = jnp.zeros_like(acc)

        acc[...] += _dot(a_ref[...], b_ref[...], dims)

        @pl.when(k == nk - 1)
        def _():
            finish(acc[...])

    if mode == "nn":
        a_spec = pl.BlockSpec((tm, tk), lambda i, j, k: (i, k))
        b_spec = pl.BlockSpec((tk, tn), lambda i, j, k: (k, j))
    elif mode == "tn":
        a_spec = pl.BlockSpec((tk, tm), lambda i, j, k: (k, i))
        b_spec = pl.BlockSpec((tk, tn), lambda i, j, k: (k, j))
    else:
        a_spec = pl.BlockSpec((tm, tk), lambda i, j, k: (i, k))
        b_spec = pl.BlockSpec((tn, tk), lambda i, j, k: (j, k))
    in_specs = [a_spec, b_spec]
    operands = [a, b]
    if res is not None:
        in_specs.append(pl.BlockSpec((tm, tn), lambda i, j, k: (i, j)))
        operands.append(res)
    return pl.pallas_call(
        body, name=name,
        out_shape=jax.ShapeDtypeStruct((M, N), out_dtype),
        grid=(M // tm, N // tn, nk),
        in_specs=in_specs,
        out_specs=pl.BlockSpec((tm, tn), lambda i, j, k: (i, j)),
        scratch_shapes=[] if nk == 1 else [pltpu.VMEM((tm, tn), F32)],
        compiler_params=_params("parallel", "parallel", "arbitrary"),
    )(*operands)


def _rmsnorm_fwd(x, g, name, col0=0, width=None):
    L = x.shape[0]
    width = x.shape[1] if width is None else width
    assert col0 % width == 0
    cb = col0 // width
    tm = _tile(L, (128, 64, 32, 16))

    def body(x_ref, g_ref, o_ref):
        xv = x_ref[...]
        r = lax.rsqrt(jnp.mean(xv * xv, axis=-1, keepdims=True) + NORM_EPS)
        o_ref[...] = ((xv * r) * g_ref[...]).astype(BF16)

    return pl.pallas_call(
        body, name=name,
        out_shape=jax.ShapeDtypeStruct((L, width), BF16),
        grid=(L // tm,),
        in_specs=[pl.BlockSpec((tm, width), lambda i: (i, cb)), pl.BlockSpec((1, width), lambda i: (0, 0))],
        out_specs=pl.BlockSpec((tm, width), lambda i: (i, 0)),
        compiler_params=_params("parallel"),
    )(x, g)


def _rmsnorm_bwd(dy, x, g, name, out_dtype, col0=0, res=None):
    L, width = dy.shape
    assert col0 % width == 0
    cb = col0 // width
    tm = _tile(L, (128, 64, 32, 16))

    def body(*refs):
        if res is None:
            dy_ref, x_ref, g_ref, dx_ref, dg_ref = refs
            r_ref = None
        else:
            dy_ref, x_ref, g_ref, r_ref, dx_ref, dg_ref = refs

        @pl.when(pl.program_id(0) == 0)
        def _():
            dg_ref[...] = jnp.zeros_like(dg_ref)

        xv = x_ref[...]
        dyv = dy_ref[...].astype(F32)
        r = lax.rsqrt(jnp.mean(xv * xv, axis=-1, keepdims=True) + NORM_EPS)
        z = dyv * g_ref[...]
        dx = r * z - xv * ((r * r * r) * jnp.mean(z * xv, axis=-1, keepdims=True))
        if r_ref is not None:
            dx = dx + r_ref[...]
        dx_ref[...] = dx.astype(out_dtype)
        dg_ref[...] += jnp.sum(dyv * (xv * r), axis=0, keepdims=True)

    in_specs = [pl.BlockSpec((tm, width), lambda i: (i, 0)),
                pl.BlockSpec((tm, width), lambda i: (i, cb)),
                pl.BlockSpec((1, width), lambda i: (0, 0))]
    operands = [dy, x, g]
    if res is not None:
        in_specs.append(pl.BlockSpec((tm, width), lambda i: (i, 0)))
        operands.append(res)
    return pl.pallas_call(
        body, name=name,
        out_shape=(jax.ShapeDtypeStruct((L, width), out_dtype), jax.ShapeDtypeStruct((1, width), F32)),
        grid=(L // tm,),
        in_specs=in_specs,
        out_specs=(pl.BlockSpec((tm, width), lambda i: (i, 0)), pl.BlockSpec((1, width), lambda i: (0, 0))),
        compiler_params=_params("arbitrary"),
    )(*operands)


def _final_loss_bwd(h2, tgt, g):
    L, D = h2.shape
    tm = PREFIX
    inv_d = 1.0 / D

    def body(h_ref, t_ref, g_ref, dh_ref, loss_ref, dg_ref):
        i = pl.program_id(0)

        @pl.when(i == 0)
        def _():
            loss_ref[...] = jnp.zeros_like(loss_ref)
            dg_ref[...] = jnp.zeros_like(dg_ref)

        xv = h_ref[...]
        r = lax.rsqrt(jnp.mean(xv * xv, axis=-1, keepdims=True) + NORM_EPS)
        xn = xv * r
        y = xn * g_ref[...]
        real = (i >= PREFIX // tm).astype(F32)
        diff = (y - t_ref[...]) * real
        loss_ref[...] += 0.5 * inv_d * jnp.sum(diff * diff)
        dyv = diff * inv_d
        z = dyv * g_ref[...]
        dh_ref[...] = r * z - xv * ((r * r * r) * jnp.mean(z * xv, axis=-1, keepdims=True))
        dg_ref[...] += jnp.sum(dyv * xn, axis=0, keepdims=True)

    shift = PREFIX // tm
    return pl.pallas_call(
        body, name="final_loss_bwd",
        out_shape=(jax.ShapeDtypeStruct((L, D), F32), jax.ShapeDtypeStruct((1, 128), F32),
                   jax.ShapeDtypeStruct((1, D), F32)),
        grid=(L // tm,),
        in_specs=[pl.BlockSpec((tm, D), lambda i: (i, 0)),
                  pl.BlockSpec((tm, D), lambda i: (jnp.maximum(i - shift, 0), 0)),
                  pl.BlockSpec((1, D), lambda i: (0, 0))],
        out_specs=(pl.BlockSpec((tm, D), lambda i: (i, 0)), pl.BlockSpec((1, 128), lambda i: (0, 0)),
                   pl.BlockSpec((1, D), lambda i: (0, 0))),
        compiler_params=_params("arbitrary"),
    )(h2, tgt, g)


def _rot_half(x):
    lane = lax.broadcasted_iota(jnp.int32, x.shape, 1)
    return jnp.where(lane < ROPE // 2, -pltpu.roll(x, 128 - ROPE // 2, 1), pltpu.roll(x, ROPE // 2, 1))


def _rope_fwd(qp, kv, kr, cos, sin, heads):
    L = qp.shape[0]
    tm = _tile(L, (128,))

    def body(q_ref, kv_ref, kr_ref, c_ref, s_ref, qc_ref, kc_ref, v_ref):
        c, s = c_ref[...], s_ref[...]
        krv = kr_ref[...]
        kr_rot = (krv * c + _rot_half(krv) * s).astype(BF16)
        for h in range(heads):
            lo = h * QPAD
            qc_ref[:, lo:lo + HEAD] = (q_ref[:, lo:lo + HEAD] * SOFTMAX_SCALE).astype(BF16)
            qr = q_ref[:, lo + HEAD:lo + QPAD]
            qc_ref[:, lo + HEAD:lo + QPAD] = ((qr * c + _rot_half(qr) * s) * SOFTMAX_SCALE).astype(BF16)
            kc_ref[:, lo:lo + HEAD] = kv_ref[:, lo:lo + HEAD].astype(BF16)
            kc_ref[:, lo + HEAD:lo + QPAD] = kr_rot
            v_ref[:, h * HEAD:(h + 1) * HEAD] = kv_ref[:, lo + HEAD:lo + QPAD].astype(BF16)

    W = heads * QPAD
    row = lambda w: pl.BlockSpec((tm, w), lambda i: (i, 0))
    return pl.pallas_call(
        body, name="rope_fwd",
        out_shape=(jax.ShapeDtypeStruct((L, W), BF16), jax.ShapeDtypeStruct((L, W), BF16),
                   jax.ShapeDtypeStruct((L, heads * HEAD), BF16)),
        grid=(L // tm,),
        in_specs=[row(W), row(W), row(128), row(128), row(128)],
        out_specs=(row(W), row(W), row(heads * HEAD)),
        compiler_params=_params("parallel"),
    )(qp, kv, kr, cos, sin)


def _rope_bwd(dqc, dkc, dv, cos, sin, heads):
    L = dqc.shape[0]
    tm = _tile(L, (128,))

    def body(dq_ref, dk_ref, dv_ref, c_ref, s_ref, dqp_ref, dkv_ref, dkr_ref):
        c, s = c_ref[...], s_ref[...]
        acc = jnp.zeros((tm, 128), F32)
        for h in range(heads):
            lo = h * QPAD
            dqp_ref[:, lo:lo + HEAD] = (dq_ref[:, lo:lo + HEAD] * SOFTMAX_SCALE).astype(BF16)
            d = dq_ref[:, lo + HEAD:lo + QPAD]
            dqp_ref[:, lo + HEAD:lo + QPAD] = ((d * c - _rot_half(d) * s) * SOFTMAX_SCALE).astype(BF16)
            dkv_ref[:, lo:lo + HEAD] = dk_ref[:, lo:lo + HEAD].astype(BF16)
            dkv_ref[:, lo + HEAD:lo + QPAD] = dv_ref[:, h * HEAD:(h + 1) * HEAD].astype(BF16)
            acc = acc + dk_ref[:, lo + HEAD:lo + QPAD]
        dkr_ref[...] = (acc * c - _rot_half(acc) * s).astype(BF16)

    W = heads * QPAD
    row = lambda w: pl.BlockSpec((tm, w), lambda i: (i, 0))
    return pl.pallas_call(
        body, name="rope_bwd",
        out_shape=(jax.ShapeDtypeStruct((L, W), BF16), jax.ShapeDtypeStruct((L, W), BF16),
                   jax.ShapeDtypeStruct((L, 128), BF16)),
        grid=(L // tm,),
        in_specs=[row(W), row(W), row(heads * HEAD), row(128), row(128)],
        out_specs=(row(W), row(W), row(128)),
        compiler_params=_params("parallel"),
    )(dqc, dkc, dv, cos, sin)


def _attn_keep(qi, ki, ta):
    t = qi * ta + lax.broadcasted_iota(jnp.int32, (ta, ta), 0)
    s = ki * ta + lax.broadcasted_iota(jnp.int32, (ta, ta), 1)
    return (s <= t) & ((s >= PAD_LEN) | (s == t))


def _attn_fwd(qc, kc, v, heads):
    L = qc.shape[0]
    ta = _tile(L, (640, 128))
    nb = L // ta

    def body(q_ref, k_ref, v_ref, o_ref, lse_ref, m_sc, l_sc, acc_sc):
        qi, ki = pl.program_id(1), pl.program_id(2)

        @pl.when(ki == 0)
        def _():
            m_sc[...] = jnp.full_like(m_sc, NEG)
            l_sc[...] = jnp.zeros_like(l_sc)
            acc_sc[...] = jnp.zeros_like(acc_sc)

        def step(masked):
            s = _dot(q_ref[...], k_ref[...], NT)
            if masked:
                s = jnp.where(_attn_keep(qi, ki, ta), s, NEG)
            m_old = m_sc[...]
            m_new = jnp.maximum(m_old, jnp.max(s, axis=-1, keepdims=True))
            p = jnp.exp(s - jnp.tile(m_new, (1, ta // HEAD)))
            alpha = jnp.exp(m_old - m_new)
            l_sc[...] = alpha * l_sc[...] + jnp.sum(p, axis=-1, keepdims=True)
            acc_sc[...] = alpha * acc_sc[...] + _dot(p, v_ref[...], NN)
            m_sc[...] = m_new

        pl.when((ki == qi) | ((ki == 0) & (qi > 0)))(lambda: step(True))
        pl.when((ki > 0) & (ki < qi))(lambda: step(False))

        @pl.when(ki == qi)
        def _():
            l = l_sc[...]
            o_ref[...] = (acc_sc[...] / l).astype(BF16)
            lse_ref[...] = m_sc[...] + jnp.log(l)

    return pl.pallas_call(
        body, name="attn_fwd",
        out_shape=(jax.ShapeDtypeStruct((L, heads * HEAD), BF16), jax.ShapeDtypeStruct((L, heads * HEAD), F32)),
        grid=(heads, nb, nb),
        in_specs=[pl.BlockSpec((ta, QPAD), lambda h, i, j: (i, h)),
                  pl.BlockSpec((ta, QPAD), lambda h, i, j: (jnp.minimum(j, i), h)),
                  pl.BlockSpec((ta, HEAD), lambda h, i, j: (jnp.minimum(j, i), h))],
        out_specs=(pl.BlockSpec((ta, HEAD), lambda h, i, j: (i, h)),
                   pl.BlockSpec((ta, HEAD), lambda h, i, j: (i, h))),
        scratch_shapes=[pltpu.VMEM((ta, HEAD), F32), pltpu.VMEM((ta, HEAD), F32), pltpu.VMEM((ta, HEAD), F32)],
        compiler_params=_params("parallel", "parallel", "arbitrary"),
    )(qc, kc, v)


def _attn_delta(o, do, heads):
    L = o.shape[0]
    tm = _tile(L, (128,))

    def body(o_ref, do_ref, d_ref):
        for h in range(heads):
            cols = slice(h * HEAD, (h + 1) * HEAD)
            d = jnp.sum(do_ref[:, cols].astype(F32) * o_ref[:, cols].astype(F32), axis=-1, keepdims=True)
            d_ref[:, cols] = jnp.broadcast_to(d, (tm, HEAD))

    row = pl.BlockSpec((tm, heads * HEAD), lambda i: (i, 0))
    return pl.pallas_call(
        body, name="attn_delta", out_shape=jax.ShapeDtypeStruct((L, heads * HEAD), F32), grid=(L // tm,),
        in_specs=[row, row], out_specs=row, compiler_params=_params("parallel"),
    )(o, do)


def _attn_bwd(qc, kc, v, do, lse, delta, heads):
    L = qc.shape[0]
    ta = _tile(L, (640, 128))
    nb = L // ta

    def body(q_ref, k_ref, v_ref, do_ref, lse_ref, delta_ref, dq_ref, dk_ref, dv_ref):
        kj, qi = pl.program_id(1), pl.program_id(2)

        @pl.when((kj == 0) & (qi == 0))
        def _():
            dq_ref[...] = jnp.zeros_like(dq_ref)

        @pl.when(qi == 0)
        def _():
            dk_ref[...] = jnp.zeros_like(dk_ref)
            dv_ref[...] = jnp.zeros_like(dv_ref)

        def step(masked):
            q, k, vv, dov = q_ref[...], k_ref[...], v_ref[...], do_ref[...]
            wide = (1, ta // HEAD)
            p = jnp.exp(_dot(q, k, NT) - jnp.tile(lse_ref[...], wide))
            if masked:
                p = jnp.where(_attn_keep(qi, kj, ta), p, 0.0)
            ds = p * (_dot(dov, vv, NT) - jnp.tile(delta_ref[...], wide))
            dv_ref[...] += _dot(p, dov, TN)
            dk_ref[...] += _dot(ds, q, TN)
            rows = pl.ds(pl.multiple_of(qi * ta, ta), ta)
            dq_ref[rows, :] += _dot(ds, k, NN)

        pl.when((qi == kj) | ((kj == 0) & (qi > 0)))(lambda: step(True))
        pl.when((kj > 0) & (qi > kj))(lambda: step(False))

    qrow = lambda w: pl.BlockSpec((ta, w), lambda h, j, i: (jnp.maximum(i, j), h))
    krow = lambda w: pl.BlockSpec((ta, w), lambda h, j, i: (j, h))
    return pl.pallas_call(
        body, name="attn_bwd",
        out_shape=(jax.ShapeDtypeStruct((L, heads * QPAD), F32), jax.ShapeDtypeStruct((L, heads * QPAD), F32),
                   jax.ShapeDtypeStruct((L, heads * HEAD), F32)),
        grid=(heads, nb, nb),
        in_specs=[qrow(QPAD), krow(QPAD), krow(HEAD), qrow(HEAD), qrow(HEAD), qrow(HEAD)],
        out_specs=(pl.BlockSpec((L, QPAD), lambda h, j, i: (0, h)), krow(QPAD), krow(HEAD)),
        compiler_params=_params("parallel", "arbitrary", "arbitrary"),
    )(qc, kc, v, do, lse, delta)


def _hgrn_constants():
    t = jnp.arange(CHUNK)
    tril = (t[None, :] <= t[:, None]).astype(BF16)
    sel = []
    for lv in range(LEVELS):
        hs = 1 << lv
        mid = (t // (2 * hs)) * (2 * hs) + hs - 1
        sel.append((t[None, :] == mid[:, None]).astype(BF16))
    return tril, jnp.concatenate(sel, axis=0)


def _hgrn_decay_grad_constants():
    r = jnp.arange(CHUNK)[:, None]
    c = jnp.arange(CHUNK)[None, :]
    mats = []
    for lv in range(LEVELS):
        same = (r >> (lv + 1)) == (c >> (lv + 1))
        second = ((r >> lv) & 1) == 1
        mats.append(same & jnp.where(second, c >= r, c < r))
    mats += [c >= r, c < r]
    return jnp.concatenate(mats, axis=1).astype(BF16)


def _hgrn_gates(hq_ref, hf_ref, hi_ref, lb_ref, cols, row0):
    rows = row0 + lax.broadcasted_iota(jnp.int32, (CHUNK, 1), 0)
    valid = rows >= PAD_LEN
    lb = 1.0 / (1.0 + jnp.exp(lb_ref[1:2, cols] - lb_ref[0:1, cols]))
    hq = hq_ref[:, cols]
    sq = _sigmoid(hq)
    sg = _sigmoid(hf_ref[:, cols])
    f = lb + (1.0 - lb) * sg
    g = jnp.where(valid, jnp.log(f), 0.0)
    k = jnp.where(valid, 1.0 - f, 0.0)
    return dict(q=hq * sq, sq=sq, hq=hq, k=k, v=hi_ref[:, cols], g=g, f=f, sg=sg, lb=lb, valid=valid)


def _hgrn_levels(q, k, b, sel_ref):
    t = lax.broadcasted_iota(jnp.int32, (CHUNK, 1), 0)
    tt = lax.broadcasted_iota(jnp.int32, (CHUNK, CHUNK), 0)
    ss = lax.broadcasted_iota(jnp.int32, (CHUNK, CHUNK), 1)
    bm_all = _dot_exact_rhs(sel_ref[...], b, NN)
    out = []
    for lv in range(LEVELS):
        bm = bm_all[lv * CHUNK:(lv + 1) * CHUNK, :]
        second = ((t >> lv) & 1) == 1
        eq = jnp.where(second, jnp.exp(jnp.minimum(b - bm, 0.0)), 0.0)
        ek = jnp.where(second, 0.0, jnp.exp(jnp.minimum(bm - b, 0.0)))
        same = (tt >> (lv + 1)) == (ss >> (lv + 1))
        out.append((eq, ek, (q * eq).astype(BF16), (k * ek).astype(BF16), same))
    return out


def _hgrn_intra(q, k, levels):
    tt = lax.broadcasted_iota(jnp.int32, (CHUNK, CHUNK), 0)
    ss = lax.broadcasted_iota(jnp.int32, (CHUNK, CHUNK), 1)
    p = jnp.where(tt == ss, jnp.sum(q * k, axis=-1, keepdims=True), 0.0)
    for (_, _, ql, kl, same) in levels:
        p = p + jnp.where(same, _dot(ql, kl, NT), 0.0)
    return p


def _hgrn_fwd(hp, lb_raw, g_norm, nh):
    L = hp.shape[0]
    D = nh * HEAD
    nc = L // CHUNK
    per = _tile(nh, (HGRN_FWD_HEADS, 2, 1))
    ng = nh // per
    tril, sel = _hgrn_constants()

    def body(hq_ref, hf_ref, hi_ref, hg_ref, lb_ref, gn_ref, tril_ref, sel_ref,
             oh_ref, orec_ref, shist_ref, s_sc, b_sc):
        c = pl.program_id(1)

        @pl.when(c == 0)
        def _():
            s_sc[...] = jnp.zeros_like(s_sc)

        for hh in range(per):
            cols = slice(hh * HEAD, (hh + 1) * HEAD)
            w = _hgrn_gates(hq_ref, hf_ref, hi_ref, lb_ref, cols, c * CHUNK)
            q, k, v = w["q"], w["k"], w["v"]
            b = _dot_exact_rhs(tril_ref[...], w["g"], NN)
            b_sc[hh] = b
            b_last = b_sc[hh, CHUNK - 1:CHUNK, :]
            p = _hgrn_intra(q, k, _hgrn_levels(q, k, b, sel_ref))
            s_in = s_sc[hh]
            shist_ref[0, hh] = s_in
            o = _dot(p, v, NN) + _dot(q * jnp.exp(b), s_in, NT)
            s_sc[hh] = jnp.exp(b_last) * s_in + _dot(v, k * jnp.exp(b_last - b), TN)
            orec_ref[:, cols] = o
            rn = lax.rsqrt(jnp.mean(o * o, axis=-1, keepdims=True) + NORM_EPS)
            hg = hg_ref[:, cols]
            oh_ref[:, cols] = (((o * rn) * gn_ref[...]) * (hg * _sigmoid(hg))).astype(BF16)

    col = lambda grp: pl.BlockSpec((CHUNK, per * HEAD), lambda h, c: (c, grp * ng + h))
    const = lambda shape: pl.BlockSpec(shape, lambda h, c: (0, 0))
    return pl.pallas_call(
        body, name="hgrn_fwd",
        out_shape=(jax.ShapeDtypeStruct((L, D), BF16), jax.ShapeDtypeStruct((L, D), F32),
                   jax.ShapeDtypeStruct((nc, nh, HEAD, HEAD), F32)),
        grid=(ng, nc),
        in_specs=[col(0), col(1), col(2), col(3),
                  pl.BlockSpec((2, per * HEAD), lambda h, c: (0, h)), const((1, HEAD)),
                  const((CHUNK, CHUNK)), const((LEVELS * CHUNK, CHUNK))],
        out_specs=(pl.BlockSpec((CHUNK, per * HEAD), lambda h, c: (c, h)),
                   pl.BlockSpec((CHUNK, per * HEAD), lambda h, c: (c, h)),
                   pl.BlockSpec((1, per, HEAD, HEAD), lambda h, c: (c, h, 0, 0))),
        scratch_shapes=[pltpu.VMEM((per, HEAD, HEAD), F32), pltpu.VMEM((per, CHUNK, HEAD), F32)],
        compiler_params=_params("parallel", "arbitrary"),
    )(hp, hp, hp, hp, lb_raw, g_norm, tril, sel)


def _hgrn_bwd(hp, lb_raw, g_norm, do_h, o_rec, s_hist, nh):
    L = hp.shape[0]
    D = nh * HEAD
    nc = L // CHUNK
    per = _tile(nh, (HGRN_BWD_HEADS, 1))
    ng = nh // per
    tril, sel = _hgrn_constants()
    tdec = _hgrn_decay_grad_constants()

    def body(hq_ref, hf_ref, hi_ref, hg_ref, lb_ref, gn_ref, tril_ref, sel_ref, tdec_ref, do_ref, orec_ref, shist_ref,
             dhq_ref, dhf_ref, dhi_ref, dhg_ref, dgn_ref, dlb_ref, ds_sc, b_sc):
        ci = pl.program_id(1)
        c = nc - 1 - ci

        @pl.when(ci == 0)
        def _():
            ds_sc[...] = jnp.zeros_like(ds_sc)
            dgn_ref[...] = jnp.zeros_like(dgn_ref)
            dlb_ref[...] = jnp.zeros_like(dlb_ref)

        for hh in range(per):
            cols = slice(hh * HEAD, (hh + 1) * HEAD)
            w = _hgrn_gates(hq_ref, hf_ref, hi_ref, lb_ref, cols, c * CHUNK)
            q, k, v, valid = w["q"], w["k"], w["v"], w["valid"]
            b = _dot_exact_rhs(tril_ref[...], w["g"], NN)
            b_sc[hh] = b
            b_last = b_sc[hh, CHUNK - 1:CHUNK, :]
            levels = _hgrn_levels(q, k, b, sel_ref)
            p = _hgrn_intra(q, k, levels)
            s_in = shist_ref[0, hh]
            ds_out = ds_sc[hh]
            eb = jnp.exp(b)
            etail = jnp.exp(b_last - b)
            decay = jnp.exp(b_last)

            o = orec_ref[:, cols]
            hg = hg_ref[:, cols]
            sgg = _sigmoid(hg)
            gate = hg * sgg
            rn = lax.rsqrt(jnp.mean(o * o, axis=-1, keepdims=True) + NORM_EPS)
            on = o * rn
            doh = do_ref[:, cols]
            dy = doh * gate
            dhg_ref[:, cols] = (doh * (on * gn_ref[...]) * (sgg * (1.0 + hg * (1.0 - sgg)))).astype(BF16)
            dgn_ref[hh] += jnp.broadcast_to(jnp.sum(dy * on, axis=0, keepdims=True), (8, HEAD))
            z = dy * gn_ref[...]
            do = rn * z - o * ((rn * rn * rn) * jnp.mean(z * o, axis=-1, keepdims=True))

            tt = lax.broadcasted_iota(jnp.int32, (CHUNK, CHUNK), 0)
            ss = lax.broadcasted_iota(jnp.int32, (CHUNK, CHUNK), 1)
            dp = jnp.where(ss <= tt, _dot(do, v, NT), 0.0)
            dv = _dot(p, do, TN) + _dot(k * etail, ds_out, NT)
            dpd = jnp.sum(jnp.where(tt == ss, dp, 0.0), axis=-1, keepdims=True)
            dq_state = eb * _dot(do, s_in, NN)
            dk_state = etail * _dot(v, ds_out, NN)
            dq = dpd * k + dq_state
            dk = dpd * q + dk_state
            pair_terms = []
            for (eq, ek, ql, kl, same) in levels:
                dpl = jnp.where(same, dp, 0.0)
                dq_l = eq * _dot(dpl, kl, NN)
                dk_l = ek * _dot(dpl, ql, TN)
                dq = dq + dq_l
                dk = dk + dk_l
                pair_terms.append(q * dq_l + k * dk_l)
            pair_terms += [q * dq_state, k * dk_state]
            ds_sc[hh] = decay * ds_out + _dot(do, q * eb, TN)
            through = jnp.sum((decay * s_in) * ds_out, axis=0, keepdims=True)
            dg = _dot_exact_rhs(tdec_ref[...], jnp.concatenate(pair_terms, axis=0), NN) + through

            f, sg, lb, sq, hq = w["f"], w["sg"], w["lb"], w["sq"], w["hq"]
            df = jnp.where(valid, dg / f - dk, 0.0)
            dhf_ref[:, cols] = (df * (1.0 - lb) * sg * (1.0 - sg)).astype(BF16)
            dlb_ref[hh] += jnp.broadcast_to(jnp.sum(df * (1.0 - sg), axis=0, keepdims=True), (8, HEAD))
            dhq_ref[:, cols] = (dq * (sq * (1.0 + hq * (1.0 - sq)))).astype(BF16)
            dhi_ref[:, cols] = dv.astype(BF16)

    col = lambda grp: pl.BlockSpec((CHUNK, per * HEAD), lambda h, c: (nc - 1 - c, grp * ng + h))
    const = lambda shape: pl.BlockSpec(shape, lambda h, c: (0, 0))
    tile = pl.BlockSpec((CHUNK, per * HEAD), lambda h, c: (nc - 1 - c, h))
    part = pl.BlockSpec((per, 8, HEAD), lambda h, c: (h, 0, 0))
    return pl.pallas_call(
        body, name="hgrn_bwd",
        out_shape=tuple([jax.ShapeDtypeStruct((L, D), BF16)] * 4 + [jax.ShapeDtypeStruct((nh, 8, HEAD), F32)] * 2),
        grid=(ng, nc),
        in_specs=[col(0), col(1), col(2), col(3),
                  pl.BlockSpec((2, per * HEAD), lambda h, c: (0, h)), const((1, HEAD)),
                  const((CHUNK, CHUNK)), const((LEVELS * CHUNK, CHUNK)), const((CHUNK, (LEVELS + 2) * CHUNK)),
                  tile, tile, pl.BlockSpec((1, per, HEAD, HEAD), lambda h, c: (nc - 1 - c, h, 0, 0))],
        out_specs=(tile, tile, tile, tile, part, part),
        scratch_shapes=[pltpu.VMEM((per, HEAD, HEAD), F32), pltpu.VMEM((per, CHUNK, HEAD), F32)],
        compiler_params=_params("parallel", "arbitrary"),
    )(hp, hp, hp, hp, lb_raw, g_norm, tril, sel, tdec, do_h, o_rec, s_hist)


def _merge_fwd(a, bm, gates):
    L, D = a.shape
    tm = _tile(L, (128,))

    def body(a_ref, b_ref, g_ref, o_ref):
        o_ref[...] = (_sigmoid(g_ref[:, :D]) * a_ref[...] + _sigmoid(g_ref[:, D:]) * b_ref[...]).astype(BF16)

    row = lambda w: pl.BlockSpec((tm, w), lambda i: (i, 0))
    return pl.pallas_call(
        body, name="merge_fwd", out_shape=jax.ShapeDtypeStruct((L, D), BF16), grid=(L // tm,),
        in_specs=[row(D), row(D), row(2 * D)], out_specs=row(D), compiler_params=_params("parallel"),
    )(a, bm, gates)


def _merge_bwd(dm, a, bm, gates):
    L, D = a.shape
    tm = _tile(L, (128,))

    def body(dm_ref, a_ref, b_ref, g_ref, da_ref, db_ref, dg_ref):
        d = dm_ref[...]
        sa, sb = _sigmoid(g_ref[:, :D]), _sigmoid(g_ref[:, D:])
        da_ref[...] = (d * sa).astype(BF16)
        db_ref[...] = (d * sb).astype(BF16)
        dg_ref[:, :D] = (d * a_ref[...] * sa * (1.0 - sa)).astype(BF16)
        dg_ref[:, D:] = (d * b_ref[...] * sb * (1.0 - sb)).astype(BF16)

    row = lambda w: pl.BlockSpec((tm, w), lambda i: (i, 0))
    return pl.pallas_call(
        body, name="merge_bwd",
        out_shape=(jax.ShapeDtypeStruct((L, D), BF16), jax.ShapeDtypeStruct((L, D), BF16),
                   jax.ShapeDtypeStruct((L, 2 * D), BF16)),
        grid=(L // tm,),
        in_specs=[row(D), row(D), row(D), row(2 * D)], out_specs=(row(D), row(D), row(2 * D)),
        compiler_params=_params("parallel"),
    )(dm, a, bm, gates)


def _conv_taps(g_ref, halo_ref, i, tm):
    rows = i * tm + lax.broadcasted_iota(jnp.int32, (tm, 1), 0)
    g0 = jnp.where(rows >= PAD_LEN, g_ref[...], 0.0)
    hrow = i * tm - 8 + lax.broadcasted_iota(jnp.int32, (8, 1), 0)
    halo = jnp.where(hrow >= PAD_LEN, halo_ref[...], 0.0)
    r = lax.broadcasted_iota(jnp.int32, (tm, 1), 0)
    h7 = jnp.sum(jnp.where(lax.broadcasted_iota(jnp.int32, (8, 1), 0) == 7, halo, 0.0), axis=0, keepdims=True)
    h6 = jnp.sum(jnp.where(lax.broadcasted_iota(jnp.int32, (8, 1), 0) == 6, halo, 0.0), axis=0, keepdims=True)
    g1 = jnp.where(r == 0, h7, pltpu.roll(g0, 1, 0))
    g2 = jnp.where(r == 0, h6, jnp.where(r == 1, h7, pltpu.roll(g0, 2, 0)))
    return g0, g1, g2


def _conv_fwd(gu, cw, cb):
    L, F2 = gu.shape
    F = F2 // 2
    tm = _tile(L, ROW_TILES)
    tn = _tile(F, (512, 256, 128))
    nj = F // tn

    def body(g_ref, halo_ref, u_ref, cw_ref, cb_ref, o_ref):
        g0, g1, g2 = _conv_taps(g_ref, halo_ref, pl.program_id(0), tm)
        conv = cw_ref[0:1, :] * g2 + cw_ref[1:2, :] * g1 + cw_ref[2:3, :] * g0 + cb_ref[...]
        o_ref[...] = (conv * _sigmoid(conv) * u_ref[...]).astype(BF16)

    return pl.pallas_call(
        body, name="conv_fwd", out_shape=jax.ShapeDtypeStruct((L, F), BF16), grid=(L // tm, nj),
        in_specs=[pl.BlockSpec((tm, tn), lambda i, j: (i, j)),
                  pl.BlockSpec((8, tn), lambda i, j: (jnp.maximum(i * (tm // 8) - 1, 0), j)),
                  pl.BlockSpec((tm, tn), lambda i, j: (i, j + nj)),
                  pl.BlockSpec((3, tn), lambda i, j: (0, j)),
                  pl.BlockSpec((1, tn), lambda i, j: (0, j))],
        out_specs=pl.BlockSpec((tm, tn), lambda i, j: (i, j)),
        compiler_params=_params("parallel", "parallel"),
    )(gu, gu, gu, cw, cb)


def _conv_bwd_a(da, gu, cw, cb):
    L, F2 = gu.shape
    F = F2 // 2
    tm = _tile(L, ROW_TILES)
    tn = _tile(F, (512, 256, 128))
    nj = F // tn

    def body(da_ref, g_ref, halo_ref, u_ref, cw_ref, cb_ref, dc_ref, du_ref, dcb_ref, dcw_ref):
        i = pl.program_id(1)

        @pl.when(i == 0)
        def _():
            dcb_ref[...] = jnp.zeros_like(dcb_ref)
            dcw_ref[...] = jnp.zeros_like(dcw_ref)

        g0, g1, g2 = _conv_taps(g_ref, halo_ref, i, tm)
        conv = cw_ref[0:1, :] * g2 + cw_ref[1:2, :] * g1 + cw_ref[2:3, :] * g0 + cb_ref[...]
        sc = _sigmoid(conv)
        dav = da_ref[...]
        du_ref[...] = (dav * (conv * sc)).astype(BF16)
        dconv = dav * u_ref[...] * (sc * (1.0 + conv * (1.0 - sc)))
        dc_ref[...] = dconv
        dcb_ref[...] += jnp.sum(dconv, axis=0, keepdims=True)
        dcw_ref[0:1, :] += jnp.sum(dconv * g2, axis=0, keepdims=True)
        dcw_ref[1:2, :] += jnp.sum(dconv * g1, axis=0, keepdims=True)
        dcw_ref[2:3, :] += jnp.sum(dconv * g0, axis=0, keepdims=True)

    return pl.pallas_call(
        body, name="conv_bwd_a",
        out_shape=(jax.ShapeDtypeStruct((L, F), F32), jax.ShapeDtypeStruct((L, F), BF16),
                   jax.ShapeDtypeStruct((1, F), F32), jax.ShapeDtypeStruct((8, F), F32)),
        grid=(nj, L // tm),
        in_specs=[pl.BlockSpec((tm, tn), lambda j, i: (i, j)),
                  pl.BlockSpec((tm, tn), lambda j, i: (i, j)),
                  pl.BlockSpec((8, tn), lambda j, i: (jnp.maximum(i * (tm // 8) - 1, 0), j)),
                  pl.BlockSpec((tm, tn), lambda j, i: (i, j + nj)),
                  pl.BlockSpec((3, tn), lambda j, i: (0, j)),
                  pl.BlockSpec((1, tn), lambda j, i: (0, j))],
        out_specs=(pl.BlockSpec((tm, tn), lambda j, i: (i, j)), pl.BlockSpec((tm, tn), lambda j, i: (i, j)),
                   pl.BlockSpec((1, tn), lambda j, i: (0, j)), pl.BlockSpec((8, tn), lambda j, i: (0, j))),
        compiler_params=_params("parallel", "arbitrary"),
    )(da, gu, gu, gu, cw, cb)


def _conv_bwd_b(dconv, cw):
    L, F = dconv.shape
    tm = _tile(L, ROW_TILES)
    tn = _tile(F, (512, 256, 128))
    nblk8 = L // 8
    ni = L // tm

    def body(dc_ref, nxt_ref, cw_ref, o_ref):
        i = pl.program_id(0)
        dc = dc_ref[...]
        nxt = jnp.where(i < ni - 1, nxt_ref[...], 0.0)
        sub = lax.broadcasted_iota(jnp.int32, (8, 1), 0)
        n0 = jnp.sum(jnp.where(sub == 0, nxt, 0.0), axis=0, keepdims=True)
        n1 = jnp.sum(jnp.where(sub == 1, nxt, 0.0), axis=0, keepdims=True)
        r = lax.broadcasted_iota(jnp.int32, (tm, 1), 0)
        d1 = jnp.where(r == tm - 1, n0, pltpu.roll(dc, tm - 1, 0))
        d2 = jnp.where(r == tm - 2, n0, jnp.where(r == tm - 1, n1, pltpu.roll(dc, tm - 2, 0)))
        dg = cw_ref[2:3, :] * dc + cw_ref[1:2, :] * d1 + cw_ref[0:1, :] * d2
        rows = i * tm + r
        o_ref[...] = jnp.where(rows >= PAD_LEN, dg, 0.0).astype(BF16)

    return pl.pallas_call(
        body, name="conv_bwd_b", out_shape=jax.ShapeDtypeStruct((L, F), BF16), grid=(ni, F // tn),
        in_specs=[pl.BlockSpec((tm, tn), lambda i, j: (i, j)),
                  pl.BlockSpec((8, tn), lambda i, j: (jnp.minimum((i + 1) * (tm // 8), nblk8 - 1), j)),
                  pl.BlockSpec((3, tn), lambda i, j: (0, j))],
        out_specs=pl.BlockSpec((tm, tn), lambda i, j: (i, j)),
        compiler_params=_params("parallel", "parallel"),
    )(dconv, dconv, cw)


ANY = pl.BlockSpec(memory_space=pl.ANY)


def _coords():
    return lax.axis_index("x"), lax.axis_index("y"), lax.axis_index("c")


def _flip(v, bit):
    return 1 - v if bit else v


CHIPS = [(1, 0), (0, 1), (1, 1)]
PEERS = [(dx, dy, dc) for dx in (0, 1) for dy in (0, 1) for dc in (0, 1)][1:]


def _gather_shards(big, small):
    nbig, n = len(big), len(big) + len(small)
    arrays = list(big) + list(small)

    def body(*refs):
        ins, outs = refs[:n], refs[n:2 * n]
        ici_send, ici_recv, d2d_send, d2d_recv, local_sems = refs[2 * n:]
        x, y, c = _coords()
        mine = 2 * x + y

        def half(w, h):
            r2 = arrays[w].shape[0] // 2
            return pl.ds(h * r2, r2)

        def ici(w, j, landing):
            px, py = _flip(x, CHIPS[j][0]), _flip(y, CHIPS[j][1])
            slot = 2 * px + py if landing else mine
            if w < nbig:
                src, dst = ins[w].at[half(w, c)], outs[w].at[slot, half(w, c)]
            else:
                src, dst = ins[w], outs[w].at[slot]
            return pltpu.make_async_remote_copy(
                src_ref=src, dst_ref=dst, send_sem=ici_send.at[w * 3 + j], recv_sem=ici_recv.at[w * 3 + j],
                device_id=(px, py, c), device_id_type=MESH)

        def d2d(w, j, landing):
            px, py = _flip(x, CHIPS[j][0]), _flip(y, CHIPS[j][1])
            mine_rows = outs[w].at[2 * px + py, half(w, c)]
            dst = outs[w].at[2 * px + py, half(w, 1 - c)] if landing else mine_rows
            return pltpu.make_async_remote_copy(
                src_ref=mine_rows, dst_ref=dst, send_sem=d2d_send.at[w * 3 + j], recv_sem=d2d_recv.at[w * 3 + j],
                device_id=(x, y, 1 - c), device_id_type=MESH)

        local = [pltpu.make_async_copy(ins[w], outs[w].at[mine], local_sems.at[w]) for w in range(n)]
        for cp in local:
            cp.start()
        for w in range(n):
            for j in range(3):
                ici(w, j, False).start()
        for w in range(n):
            for j in range(3):
                ici(w, j, True).wait_recv()
                if w < nbig:
                    d2d(w, j, False).start()
        for w in range(nbig):
            for j in range(3):
                d2d(w, j, True).wait_recv()
        for w in range(n):
            for j in range(3):
                ici(w, j, False).wait_send()
                if w < nbig:
                    d2d(w, j, False).wait_send()
        for cp in local:
            cp.wait()

    return pl.pallas_call(
        body, name="gather_shards",
        out_shape=tuple(jax.ShapeDtypeStruct((4,) + s.shape, s.dtype) for s in arrays),
        in_specs=[ANY] * n, out_specs=tuple([ANY] * n),
        scratch_shapes=[pltpu.SemaphoreType.DMA((3 * n,)), pltpu.SemaphoreType.DMA((3 * n,)),
                        pltpu.SemaphoreType.DMA((3 * nbig,)), pltpu.SemaphoreType.DMA((3 * nbig,)),
                        pltpu.SemaphoreType.DMA((n,))],
        compiler_params=pltpu.CompilerParams(has_side_effects=True),
    )(*arrays)


def _to_sibling(arrays, name):
    n = len(arrays)

    def body(*refs):
        ins, outs = refs[:n], refs[n:2 * n]
        send_sems, recv_sems = refs[2 * n:]
        x, y, c = _coords()

        def copy(w):
            return pltpu.make_async_remote_copy(
                src_ref=ins[w], dst_ref=outs[w], send_sem=send_sems.at[w], recv_sem=recv_sems.at[w],
                device_id=(x, y, 1 - c), device_id_type=MESH)

        for w in range(n):
            copy(w).start()
        for w in range(n):
            copy(w).wait_recv()
            copy(w).wait_send()

    return pl.pallas_call(
        body, name=name,
        out_shape=tuple(jax.ShapeDtypeStruct(a.shape, a.dtype) for a in arrays),
        in_specs=[ANY] * n, out_specs=tuple([ANY] * n),
        scratch_shapes=[pltpu.SemaphoreType.DMA((n,)), pltpu.SemaphoreType.DMA((n,))],
        compiler_params=pltpu.CompilerParams(has_side_effects=True),
    )(*arrays)


def _pair_sum(a, b, name):
    _, r, c = a.shape
    tr = _tile(r, (64, 32, 16))

    def body(a_ref, b_ref, o_ref):
        o_ref[...] = (a_ref[...].astype(F32) + b_ref[...].astype(F32)).astype(BF16)

    blk = pl.BlockSpec((4, tr, c), lambda i: (0, i, 0))
    return pl.pallas_call(
        body, name=name, out_shape=jax.ShapeDtypeStruct(a.shape, BF16), grid=(r // tr,),
        in_specs=[blk, blk], out_specs=blk, compiler_params=_params("parallel"),
    )(a, b)


def _scatter_chip_parts(parts):
    n = len(parts)

    def body(*refs):
        ins, outs = refs[:n], refs[n:2 * n]
        send_sems, recv_sems, local_sems = refs[2 * n:]
        x, y, c = _coords()
        mine = 2 * x + y

        def ici(w, j, landing):
            px, py = _flip(x, CHIPS[j][0]), _flip(y, CHIPS[j][1])
            return pltpu.make_async_remote_copy(
                src_ref=ins[w].at[2 * px + py], dst_ref=outs[w].at[2 * px + py if landing else mine],
                send_sem=send_sems.at[w * 3 + j], recv_sem=recv_sems.at[w * 3 + j],
                device_id=(px, py, c), device_id_type=MESH)

        local = [pltpu.make_async_copy(ins[w].at[mine], outs[w].at[mine], local_sems.at[w]) for w in range(n)]
        for cp in local:
            cp.start()
        for w in range(n):
            for j in range(3):
                ici(w, j, False).start()
        for w in range(n):
            for j in range(3):
                ici(w, j, True).wait_recv()
                ici(w, j, False).wait_send()
        for cp in local:
            cp.wait()

    return pl.pallas_call(
        body, name="scatter_chip_parts",
        out_shape=tuple(jax.ShapeDtypeStruct(p.shape, p.dtype) for p in parts),
        in_specs=[ANY] * n, out_specs=tuple([ANY] * n),
        scratch_shapes=[pltpu.SemaphoreType.DMA((3 * n,)), pltpu.SemaphoreType.DMA((3 * n,)),
                        pltpu.SemaphoreType.DMA((n,))],
        compiler_params=pltpu.CompilerParams(has_side_effects=True),
    )(*parts)


def _sum4(recv, name):
    _, r, c = recv.shape
    tr = _tile(r, (64, 32, 16))

    def body(in_ref, o_ref):
        o_ref[...] = ((in_ref[0].astype(F32) + in_ref[1].astype(F32)) + in_ref[2].astype(F32)) + in_ref[3].astype(F32)

    return pl.pallas_call(
        body, name=name, out_shape=jax.ShapeDtypeStruct((r, c), F32), grid=(r // tr,),
        in_specs=[pl.BlockSpec((4, tr, c), lambda i: (0, i, 0))],
        out_specs=pl.BlockSpec((tr, c), lambda i: (i, 0)),
        compiler_params=_params("parallel"),
    )(recv)


def _allreduce_small(packed):
    R = packed.shape[0]

    def body(in_ref, o_ref, buf, send_sems, recv_sems):
        x, y, c = _coords()
        me = 4 * x + 2 * y + c
        buf[me] = in_ref[...]
        for j, (dx, dy, dc) in enumerate(PEERS):
            px, py, pc = _flip(x, dx), _flip(y, dy), _flip(c, dc)
            pltpu.make_async_remote_copy(
                src_ref=in_ref, dst_ref=buf.at[me], send_sem=send_sems.at[j], recv_sem=recv_sems.at[j],
                device_id=(px, py, pc), device_id_type=MESH).start()
        for j, (dx, dy, dc) in enumerate(PEERS):
            px, py, pc = _flip(x, dx), _flip(y, dy), _flip(c, dc)
            rc = pltpu.make_async_remote_copy(
                src_ref=in_ref, dst_ref=buf.at[4 * px + 2 * py + pc], send_sem=send_sems.at[j],
                recv_sem=recv_sems.at[j], device_id=(px, py, pc), device_id_type=MESH)
            rc.wait_recv()
            rc.wait_send()
        acc = buf[0]
        for s in range(1, 8):
            acc = acc + buf[s]
        o_ref[...] = acc

    return pl.pallas_call(
        body, name="allreduce_small", out_shape=jax.ShapeDtypeStruct((R, 128), F32),
        in_specs=[pl.BlockSpec(memory_space=pltpu.VMEM)], out_specs=pl.BlockSpec(memory_space=pltpu.VMEM),
        scratch_shapes=[pltpu.VMEM((8, R, 128), F32), pltpu.SemaphoreType.DMA((7,)), pltpu.SemaphoreType.DMA((7,))],
        compiler_params=pltpu.CompilerParams(has_side_effects=True, vmem_limit_bytes=VMEM_LIMIT_BYTES),
    )(packed)


def _adamw_math(w, g, m, v):
    m = ADAM_B1 * m + (1.0 - ADAM_B1) * g
    v = ADAM_B2 * v + (1.0 - ADAM_B2) * (g * g)
    m_hat = m / (1.0 - ADAM_B1 ** ADAM_STEP)
    v_hat = v / (1.0 - ADAM_B2 ** ADAM_STEP)
    delta = -ADAM_LR * (m_hat / (jnp.sqrt(v_hat) + ADAM_EPS) + ADAM_WD * w)
    return delta, m, v


def _adamw_big(mine, other, w, m, v, name):
    R, C = w.shape
    tr = _tile(R // 2, (128, 64, 32, 16, 8))
    nb = (R // 2) // tr

    def body(mine_ref, other_ref, w_ref, m_ref, v_ref, g_ref, d_ref, mo_ref, vo_ref):
        is_mine = (pl.program_id(0) // nb) == lax.axis_index("c")
        g = jnp.where(is_mine, mine_ref[...], other_ref[...])
        d, mn, vn = _adamw_math(w_ref[...], g, m_ref[...], v_ref[...])
        g_ref[...] = g
        d_ref[...] = d
        mo_ref[...] = mn
        vo_ref[...] = vn

    blk = pl.BlockSpec((tr, C), lambda i: (i, 0))
    half = pl.BlockSpec((tr, C), lambda i: (i % nb, 0))
    sds = jax.ShapeDtypeStruct((R, C), F32)
    return pl.pallas_call(
        body, name=name, out_shape=(sds, sds, sds, sds), grid=(2 * nb,),
        in_specs=[half, half, blk, blk, blk], out_specs=(blk, blk, blk, blk), compiler_params=_params("parallel"),
    )(mine, other, w, m, v)


def _adamw_small(items, lb_raw, dlb):
    n = len(items)
    lb_w, lb_m, lb_v = lb_raw

    def body(*refs):
        ins = refs[:4 * n]
        dlb_ref, lw_ref, lm_ref, lv_ref = refs[4 * n:4 * n + 4]
        outs = refs[4 * n + 4:]
        for t in range(n):
            g_ref, w_ref, m_ref, v_ref = ins[4 * t:4 * t + 4]
            d, mn, vn = _adamw_math(w_ref[...], g_ref[...], m_ref[...], v_ref[...])
            outs[3 * t][...] = d
            outs[3 * t + 1][...] = mn
            outs[3 * t + 2][...] = vn
        p0 = 1.0 / (1.0 + jnp.exp(lw_ref[1:2, :] - lw_ref[0:1, :]))
        g0 = dlb_ref[...] * p0 * (1.0 - p0)
        base = 3 * n
        outs[base][0:1, :] = g0
        outs[base][1:2, :] = -g0
        d, mn, vn = _adamw_math(lw_ref[...], outs[base][...], lm_ref[...], lv_ref[...])
        outs[base + 1][...] = d
        outs[base + 2][...] = mn
        outs[base + 3][...] = vn

    operands = [a for it in items for a in it] + [dlb, lb_w, lb_m, lb_v]
    out_shape = []
    for (g, w, m, v) in items:
        out_shape += [jax.ShapeDtypeStruct(w.shape, F32)] * 3
    out_shape += [jax.ShapeDtypeStruct(lb_w.shape, F32)] * 4
    vm = pl.BlockSpec(memory_space=pltpu.VMEM)
    res = pl.pallas_call(
        body, name="adamw_small", out_shape=tuple(out_shape),
        in_specs=[vm] * len(operands), out_specs=tuple([vm] * len(out_shape)),
        compiler_params=pltpu.CompilerParams(vmem_limit_bytes=VMEM_LIMIT_BYTES),
    )(*operands)
    deltas = [res[3 * t] for t in range(n)] + [res[3 * n + 1]]
    new_m = [res[3 * t + 1] for t in range(n)] + [res[3 * n + 2]]
    new_v = [res[3 * t + 2] for t in range(n)] + [res[3 * n + 3]]
    return res[3 * n], deltas, new_m, new_v


def _shard_row_half(g, by_cols, h):
    R, C = g.shape
    if by_cols:
        part = lax.dynamic_index_in_dim(g.reshape(2, R // 2, 4, C // 4), h, axis=0, keepdims=False)
        return part.transpose(1, 0, 2).astype(BF16)
    return lax.dynamic_index_in_dim(g.reshape(4, 2, R // 8, C), h, axis=1, keepdims=False).astype(BF16)


def kernel(x, positions, meta_tokens, w_in, w_q_up, w_kv_up, w_branch_mla, w_branch_hgrn, w_out, w_ffn_in, w_ffn_out, conv_w, conv_b, g_mix_norm, g_q_norm, g_kv_norm, g_hgrn_norm, g_ffn_norm, g_final_norm, lb_raw, loss_target, m_meta_tokens, m_w_in, m_w_q_up, m_w_kv_up, m_w_branch_mla, m_w_branch_hgrn, m_w_out, m_w_ffn_in, m_w_ffn_out, m_conv_w, m_conv_b, m_g_mix_norm, m_g_q_norm, m_g_kv_norm, m_g_hgrn_norm, m_g_ffn_norm, m_g_final_norm, m_lb_raw, v_meta_tokens, v_w_in, v_w_q_up, v_w_kv_up, v_w_branch_mla, v_w_branch_hgrn, v_w_out, v_w_ffn_in, v_w_ffn_out, v_conv_w, v_conv_b, v_g_mix_norm, v_g_q_norm, v_g_kv_norm, v_g_hgrn_norm, v_g_ffn_norm, v_g_final_norm, v_lb_raw):
    S, D = x.shape[1], x.shape[2]
    L = S + PREFIX
    QL, KVL = g_q_norm.shape[1], g_kv_norm.shape[1]
    F = conv_b.shape[1]
    heads = (4 * w_kv_up.shape[2]) // QPAD
    nh = D // HEAD
    assert lb_raw.shape[0] == 2 and g_hgrn_norm.shape[1] == HEAD and L % CHUNK == 0
    ix, iy, ic = _coords()
    chip = 2 * ix + iy

    big = [w_in, w_q_up, w_kv_up, w_branch_mla, w_branch_hgrn, w_out, w_ffn_in, w_ffn_out]
    col_sharded = [True, True, True, False, False, False, True, False]
    gathered = _gather_shards([w[0].astype(BF16) for w in big], [meta_tokens, conv_w[0]])

    def full(gw, by_cols):
        _, r, c = gw.shape
        return gw.transpose(1, 0, 2).reshape(r, 4 * c) if by_cols else gw.reshape(4 * r, c)

    W_in, W_q, W_kv, W_a, W_b, W_o, W_fi, W_fo = [full(gw, bc) for gw, bc in zip(gathered[:8], col_sharded)]
    meta_full = full(gathered[8], True)
    cw_full = full(gathered[9], True)
    c0 = QL + KVL
    W_lat = W_in[:, :c0]
    W_kr = jnp.pad(W_in[:, c0:c0 + ROPE], ((0, 0), (0, 128 - ROPE)))
    W_H = W_in[:, c0 + ROPE:c0 + ROPE + 4 * D]
    W_G = W_in[:, c0 + ROPE + 4 * D:]
    W_qp = jnp.pad(W_q.reshape(QL, heads, QK_HEAD), ((0, 0), (0, 0), (0, QPAD - QK_HEAD))).reshape(QL, heads * QPAD)

    h0 = jnp.concatenate([jnp.zeros((PAD_LEN, D), F32), meta_full, x[0]], axis=0)
    pos = jnp.concatenate([jnp.zeros((PAD_LEN,), jnp.int32), jnp.arange(N_META, dtype=jnp.int32),
                           positions[0].astype(jnp.int32) + N_META])
    inv = 1.0 / (ROPE_THETA ** (jnp.arange(0, ROPE, 2, dtype=F32) / ROPE))
    ang = pos.astype(F32)[:, None] * inv
    zero = jnp.zeros((L, 128 - ROPE), F32)
    cos = jnp.concatenate([jnp.cos(ang), jnp.cos(ang), zero], axis=1)
    sin = jnp.concatenate([jnp.sin(ang), jnp.sin(ang), zero], axis=1)

    u1 = _rmsnorm_fwd(h0, g_mix_norm, "norm_mix")
    lat = _mm(u1, W_lat, "nn", F32, "proj_lat")
    hp = _mm(u1, W_H, "nn", F32, "proj_hgrn")
    gates = _mm(u1, W_G, "nn", F32, "proj_gates")
    kr = _mm(u1, W_kr, "nn", F32, "proj_krope")
    qn = _rmsnorm_fwd(lat, g_q_norm, "norm_q", col0=0, width=QL)
    kvn = _rmsnorm_fwd(lat, g_kv_norm, "norm_kv", col0=QL, width=KVL)
    qp = _mm(qn, W_qp, "nn", F32, "q_up")
    kv = _mm(kvn, W_kv, "nn", F32, "kv_up")
    qc, kc, vv = _rope_fwd(qp, kv, kr, cos, sin, heads)
    o_mla, lse = _attn_fwd(qc, kc, vv, heads)
    o_hgrn, o_rec, s_hist = _hgrn_fwd(hp, lb_raw, g_hgrn_norm, nh)
    br_a = _mm(o_mla, W_a, "nn", F32, "branch_mla")
    br_b = _mm(o_hgrn, W_b, "nn", F32, "branch_hgrn")
    merged = _merge_fwd(br_a, br_b, gates)
    h1 = _mm(merged, W_o, "nn", F32, "out_proj", res=h0)
    u2 = _rmsnorm_fwd(h1, g_ffn_norm, "norm_ffn")
    gu = _mm(u2, W_fi, "nn", F32, "ffn_in")
    act = _conv_fwd(gu, cw_full, conv_b)
    h2 = _mm(act, W_fo, "nn", F32, "ffn_out", res=h1)
    dh2, loss_p, dg_final = _final_loss_bwd(h2, loss_target[0], g_final_norm.reshape(1, D))

    dact = _mm(dh2, W_fo, "nt", F32, "d_act")
    dW_fo = _mm(act, dh2, "tn", F32, "dw_ffn_out")
    dconv, dup, dcb, dcw = _conv_bwd_a(dact, gu, cw_full, conv_b)
    dgate = _conv_bwd_b(dconv, cw_full)
    dgu = jnp.concatenate([dgate, dup], axis=1)
    du2 = _mm(dgu, W_fi, "nt", F32, "d_u2")
    dW_fi = _mm(u2, dgu, "tn", F32, "dw_ffn_in")
    dh1, dg_ffn = _rmsnorm_bwd(du2, h1, g_ffn_norm, "norm_ffn_bwd", F32, res=dh2)
    dmerged = _mm(dh1, W_o, "nt", F32, "d_merged")
    dW_o = _mm(merged, dh1, "tn", F32, "dw_out")
    d_a, d_b, d_gates = _merge_bwd(dmerged, br_a, br_b, gates)
    do_mla = _mm(d_a, W_a, "nt", BF16, "d_o_mla")
    dW_a = _mm(o_mla, d_a, "tn", F32, "dw_branch_mla")
    do_hgrn = _mm(d_b, W_b, "nt", F32, "d_o_hgrn")
    dW_b = _mm(o_hgrn, d_b, "tn", F32, "dw_branch_hgrn")
    dhq, dhf, dhi, dhg, dgn_p, dlb_p = _hgrn_bwd(hp, lb_raw, g_hgrn_norm, do_hgrn, o_rec, s_hist, nh)
    dqc, dkc, dvv = _attn_bwd(qc, kc, vv, do_mla, lse, _attn_delta(o_mla, do_mla, heads), heads)
    dqp, dkv, dkr = _rope_bwd(dqc, dkc, dvv, cos, sin, heads)
    dqn = _mm(dqp, W_qp, "nt", F32, "d_qn")
    dW_qp = _mm(qn, dqp, "tn", F32, "dw_q_up")
    dkvn = _mm(dkv, W_kv, "nt", F32, "d_kvn")
    dW_kv = _mm(kvn, dkv, "tn", F32, "dw_kv_up")
    dq_lat, dg_q = _rmsnorm_bwd(dqn, lat, g_q_norm, "norm_q_bwd", BF16, col0=0)
    dkv_lat, dg_kv = _rmsnorm_bwd(dkvn, lat, g_kv_norm, "norm_kv_bwd", BF16, col0=QL)
    dlat = jnp.concatenate([dq_lat, dkv_lat], axis=1)
    dhp = jnp.concatenate([dhq, dhf, dhi, dhg], axis=1)
    du1 = _mm(dlat, W_lat, "nt", F32, "d_u1_lat")
    du1 = _mm(dhp, W_H, "nt", F32, "d_u1_hgrn", res=du1)
    du1 = _mm(d_gates, W_G, "nt", F32, "d_u1_gates", res=du1)
    du1 = _mm(dkr, W_kr, "nt", F32, "d_u1_krope", res=du1)
    dW_lat = _mm(u1, dlat, "tn", F32, "dw_in_lat")
    dW_H = _mm(u1, dhp, "tn", F32, "dw_in_hgrn")
    dW_G = _mm(u1, d_gates, "tn", F32, "dw_in_gates")
    dW_kr = _mm(u1, dkr, "tn", F32, "dw_in_krope")
    dh0, dg_mix = _rmsnorm_bwd(du1, h0, g_mix_norm, "norm_mix_bwd", F32, res=dh1)
    grad_x = dh0[PREFIX:][None]

    dW_in = jnp.concatenate([dW_lat, dW_kr[:, :ROPE], dW_H, dW_G], axis=1)
    dW_q = dW_qp.reshape(QL, heads, QPAD)[:, :, :QK_HEAD].reshape(QL, heads * QK_HEAD)
    big_grads = [dW_in, dW_q, dW_kv, dW_a, dW_b, dW_o, dW_fi, dW_fo]
    names = ["w_in", "w_q_up", "w_kv_up", "w_branch_mla", "w_branch_hgrn", "w_out", "w_ffn_in", "w_ffn_out"]
    keep = [_shard_row_half(g, bc, ic) for g, bc in zip(big_grads, col_sharded)]
    give = [_shard_row_half(g, bc, 1 - ic) for g, bc in zip(big_grads, col_sharded)]
    taken = _to_sibling(give, "pair_exchange")
    chip_parts = [_pair_sum(a, b, "pair_sum_" + nm) for a, b, nm in zip(keep, taken, names)]
    received = _scatter_chip_parts(chip_parts)
    halves = [_sum4(r, "sum_" + nm) for r, nm in zip(received, names)]
    others = _to_sibling(halves, "swap_halves")
    big_m = [m_w_in, m_w_q_up, m_w_kv_up, m_w_branch_mla, m_w_branch_hgrn, m_w_out, m_w_ffn_in, m_w_ffn_out]
    big_v = [v_w_in, v_w_q_up, v_w_kv_up, v_w_branch_mla, v_w_branch_hgrn, v_w_out, v_w_ffn_in, v_w_ffn_out]
    big_out = {}
    for nm, mine, other, w, m, v in zip(names, halves, others, big, big_m, big_v):
        g, d, mn, vn = _adamw_big(mine, other, w[0], m[0], v[0], "adamw_" + nm)
        big_out[nm] = (g[None], d[None], mn[None], vn[None])

    pieces = [loss_p[:, :1], dg_mix, dg_q, dg_kv, jnp.sum(dgn_p[:, 0, :], axis=0, keepdims=True), dg_ffn, dg_final,
              dlb_p[:, 0, :].reshape(1, D), dcb, dcw[0:3].reshape(1, 3 * F), dh0[PAD_LEN:PREFIX].reshape(1, N_META * D)]
    sizes = [p.shape[1] for p in pieces]
    flat = jnp.concatenate(pieces, axis=1)[0]
    rows = -(-flat.shape[0] // 1024) * 8
    packed = jnp.pad(flat, (0, rows * 128 - flat.shape[0])).reshape(rows, 128)
    total = _allreduce_small(packed).reshape(-1)
    offs = [0]
    for s in sizes:
        offs.append(offs[-1] + s)
    loss, g_mix, g_q, g_kv, g_hg, g_ffn, g_fin, dlb, g_cb, g_cw, g_meta = [
        total[offs[t]:offs[t + 1]].reshape(1, sizes[t]) for t in range(len(sizes))]
    g_cw = lax.dynamic_slice_in_dim(g_cw.reshape(3, F), chip * (F // 4), F // 4, axis=1)
    g_meta = lax.dynamic_slice_in_dim(g_meta.reshape(N_META, D), chip * (D // 4), D // 4, axis=1)
    items = [(g_meta, meta_tokens, m_meta_tokens, v_meta_tokens),
             (g_cw, conv_w[0], m_conv_w[0], v_conv_w[0]),
             (g_cb, conv_b, m_conv_b, v_conv_b),
             (g_mix, g_mix_norm, m_g_mix_norm, v_g_mix_norm),
             (g_q, g_q_norm, m_g_q_norm, v_g_q_norm),
             (g_kv, g_kv_norm, m_g_kv_norm, v_g_kv_norm),
             (g_hg, g_hgrn_norm, m_g_hgrn_norm, v_g_hgrn_norm),
             (g_ffn, g_ffn_norm, m_g_ffn_norm, v_g_ffn_norm),
             (g_fin, g_final_norm.reshape(1, D), m_g_final_norm.reshape(1, D), v_g_final_norm.reshape(1, D))]
    g_lb, s_delta, s_m, s_v = _adamw_small(items, (lb_raw, m_lb_raw, v_lb_raw), dlb)
    s_grads = [it[0] for it in items] + [g_lb]

    def shape_small(vals):
        meta, cw, cb, mix, q, kvg, hg, ffn, fin, lb = vals
        return [meta, cw[None], cb, mix, q, kvg, hg, ffn, fin.reshape(D), lb]

    s_grads, s_delta, s_m, s_v = [shape_small(v) for v in (s_grads, s_delta, s_m, s_v)]

    def ordered(kind, small):
        bigs = [big_out[nm][kind] for nm in names]
        return [small[0]] + bigs + small[1:]

    return (loss.reshape(()), grad_x, *ordered(0, s_grads), *ordered(1, s_delta), *ordered(2, s_m), *ordered(3, s_v))
```

```python
import math

import jax
import jax.numpy as jnp
from jax import lax
from jax.experimental import pallas as pl
from jax.experimental.pallas import tpu as pltpu

F32 = jnp.float32
BF16 = jnp.bfloat16
MESH = pl.DeviceIdType.MESH

NORM_EPS = 1e-6
N_META = 16
PREFIX = 128
PAD_LEN = PREFIX - N_META
HEAD = 128
ROPE = 64
QK_HEAD = HEAD + ROPE
QPAD = 2 * HEAD
SOFTMAX_SCALE = QK_HEAD ** -0.5
ROPE_THETA = 10000.0
CHUNK = 128
LEVELS = 7
HGRN_FWD_HEADS = 4
HGRN_BWD_HEADS = 4
NEG = -1e30

ADAM_LR = 0.001
ADAM_B1 = 0.9
ADAM_B2 = 0.999
ADAM_EPS = 1e-08
ADAM_WD = 0.01
ADAM_STEP = 10

VMEM_LIMIT_BYTES = 48 * 1024 * 1024


def _params(*sem):
    return pltpu.CompilerParams(dimension_semantics=sem, vmem_limit_bytes=VMEM_LIMIT_BYTES)


def _tile(n, prefs):
    for p in prefs:
        if n % p == 0:
            return p
    return n


ROW_TILES = (640, 512, 256, 128, 64, 32, 16, 8)
TN_ROW_TILES = (1024, 1408, 1536, 512, 256, 128)
COL_TILES = (1024, 512, 256, 128)
K_TILES = (2048, 1536, 1408, 1024, 512, 256, 128)


def _sigmoid(x):
    return 1.0 / (1.0 + jnp.exp(-x))


def _dot(a, b, dims):
    return lax.dot_general(a.astype(BF16), b.astype(BF16), (dims, ((), ())), preferred_element_type=F32)


NN = ((1,), (0,))
TN = ((0,), (0,))
NT = ((1,), (1,))


def _split3(x):
    hi = x.astype(BF16)
    r = x - hi.astype(F32)
    mid = r.astype(BF16)
    lo = (r - mid.astype(F32)).astype(BF16)
    return hi, mid, lo


def _dot_exact_rhs(sel, x, dims):
    hi, mid, lo = _split3(x)
    return _dot(sel, hi, dims) + _dot(sel, mid, dims) + _dot(sel, lo, dims)


def _mm(a, b, mode, out_dtype, name, res=None):
    if mode == "nn":
        (M, K), (K2, N) = a.shape, b.shape
    elif mode == "tn":
        (K, M), (K2, N) = a.shape, b.shape
    else:
        (M, K), (N, K2) = a.shape, b.shape
    assert K == K2, (name, a.shape, b.shape)
    tm = _tile(M, TN_ROW_TILES if mode == "tn" else ROW_TILES)
    tn = _tile(N, COL_TILES)
    tk = _tile(K, ROW_TILES if mode == "tn" else K_TILES)
    nk = K // tk
    dims = {"nn": NN, "tn": TN, "nt": NT}[mode]

    def body(*refs):
        a_ref, b_ref = refs[:2]
        r_ref = None if res is None else refs[2]
        o_ref = refs[2 if res is None else 3]

        def finish(r):
            if r_ref is not None:
                r = r + r_ref[...].astype(F32)
            o_ref[...] = r.astype(out_dtype)

        if nk == 1:
            finish(_dot(a_ref[...], b_ref[...], dims))
            return
        acc = refs[-1]
        k = pl.program_id(2)

        @pl.when(k == 0)
        def _():
            acc[...] = jnp.zeros_like(acc)

        acc[...] += _dot(a_ref[...], b_ref[...], dims)

        @pl.when(k == nk - 1)
        def _():
            finish(acc[...])

    if mode == "nn":
        a_spec = pl.BlockSpec((tm, tk), lambda i, j, k: (i, k))
        b_spec = pl.BlockSpec((tk, tn), lambda i, j, k: (k, j))
    elif mode == "tn":
        a_spec = pl.BlockSpec((tk, tm), lambda i, j, k: (k, i))
        b_spec = pl.BlockSpec((tk, tn), lambda i, j, k: (k, j))
    else:
        a_spec = pl.BlockSpec((tm, tk), lambda i, j, k: (i, k))
        b_spec = pl.BlockSpec((tn, tk), lambda i, j, k: (j, k))
    in_specs = [a_spec, b_spec]
    operands = [a, b]
    if res is not None:
        in_specs.append(pl.BlockSpec((tm, tn), lambda i, j, k: (i, j)))
        operands.append(res)
    return pl.pallas_call(
        body, name=name,
        out_shape=jax.ShapeDtypeStruct((M, N), out_dtype),
        grid=(M // tm, N // tn, nk),
        in_specs=in_specs,
        out_specs=pl.BlockSpec((tm, tn), lambda i, j, k: (i, j)),
        scratch_shapes=[] if nk == 1 else [pltpu.VMEM((tm, tn), F32)],
        compiler_params=_params("parallel", "parallel", "arbitrary"),
    )(*operands)


def _rmsnorm_fwd(x, g, name, col0=0, width=None):
    L = x.shape[0]
    width = x.shape[1] if width is None else width
    assert col0 % width == 0
    cb = col0 // width
    tm = _tile(L, (128, 64, 32, 16))

    def body(x_ref, g_ref, o_ref):
        xv = x_ref[...]
        r = lax.rsqrt(jnp.mean(xv * xv, axis=-1, keepdims=True) + NORM_EPS)
        o_ref[...] = ((xv * r) * g_ref[...]).astype(BF16)

    return pl.pallas_call(
        body, name=name,
        out_shape=jax.ShapeDtypeStruct((L, width), BF16),
        grid=(L // tm,),
        in_specs=[pl.BlockSpec((tm, width), lambda i: (i, cb)), pl.BlockSpec((1, width), lambda i: (0, 0))],
        out_specs=pl.BlockSpec((tm, width), lambda i: (i, 0)),
        compiler_params=_params("parallel"),
    )(x, g)


def _rmsnorm_bwd(dy, x, g, name, out_dtype, col0=0, res=None):
    L, width = dy.shape
    assert col0 % width == 0
    cb = col0 // width
    tm = _tile(L, (128, 64, 32, 16))

    def body(*refs):
        if res is None:
            dy_ref, x_ref, g_ref, dx_ref, dg_ref = refs
            r_ref = None
        else:
            dy_ref, x_ref, g_ref, r_ref, dx_ref, dg_ref = refs

        @pl.when(pl.program_id(0) == 0)
        def _():
            dg_ref[...] = jnp.zeros_like(dg_ref)

        xv = x_ref[...]
        dyv = dy_ref[...].astype(F32)
        r = lax.rsqrt(jnp.mean(xv * xv, axis=-1, keepdims=True) + NORM_EPS)
        z = dyv * g_ref[...]
        dx = r * z - xv * ((r * r * r) * jnp.mean(z * xv, axis=-1, keepdims=True))
        if r_ref is not None:
            dx = dx + r_ref[...]
        dx_ref[...] = dx.astype(out_dtype)
        dg_ref[...] += jnp.sum(dyv * (xv * r), axis=0, keepdims=True)

    in_specs = [pl.BlockSpec((tm, width), lambda i: (i, 0)),
                pl.BlockSpec((tm, width), lambda i: (i, cb)),
                pl.BlockSpec((1, width), lambda i: (0, 0))]
    operands = [dy, x, g]
    if res is not None:
        in_specs.append(pl.BlockSpec((tm, width), lambda i: (i, 0)))
        operands.append(res)
    return pl.pallas_call(
        body, name=name,
        out_shape=(jax.ShapeDtypeStruct((L, width), out_dtype), jax.ShapeDtypeStruct((1, width), F32)),
        grid=(L // tm,),
        in_specs=in_specs,
        out_specs=(pl.BlockSpec((tm, width), lambda i: (i, 0)), pl.BlockSpec((1, width), lambda i: (0, 0))),
        compiler_params=_params("arbitrary"),
    )(*operands)


def _final_loss_bwd(h2, tgt, g):
    L, D = h2.shape
    tm = PREFIX
    inv_d = 1.0 / D

    def body(h_ref, t_ref, g_ref, dh_ref, loss_ref, dg_ref):
        i = pl.program_id(0)

        @pl.when(i == 0)
        def _():
            loss_ref[...] = jnp.zeros_like(loss_ref)
            dg_ref[...] = jnp.zeros_like(dg_ref)

        xv = h_ref[...]
        r = lax.rsqrt(jnp.mean(xv * xv, axis=-1, keepdims=True) + NORM_EPS)
        xn = xv * r
        y = xn * g_ref[...]
        real = (i >= PREFIX // tm).astype(F32)
        diff = (y - t_ref[...]) * real
        loss_ref[...] += 0.5 * inv_d * jnp.sum(diff * diff)
        dyv = diff * inv_d
        z = dyv * g_ref[...]
        dh_ref[...] = r * z - xv * ((r * r * r) * jnp.mean(z * xv, axis=-1, keepdims=True))
        dg_ref[...] += jnp.sum(dyv * xn, axis=0, keepdims=True)

    shift = PREFIX // tm
    return pl.pallas_call(
        body, name="final_loss_bwd",
        out_shape=(jax.ShapeDtypeStruct((L, D), F32), jax.ShapeDtypeStruct((1, 128), F32),
                   jax.ShapeDtypeStruct((1, D), F32)),
        grid=(L // tm,),
        in_specs=[pl.BlockSpec((tm, D), lambda i: (i, 0)),
                  pl.BlockSpec((tm, D), lambda i: (jnp.maximum(i - shift, 0), 0)),
                  pl.BlockSpec((1, D), lambda i: (0, 0))],
        out_specs=(pl.BlockSpec((tm, D), lambda i: (i, 0)), pl.BlockSpec((1, 128), lambda i: (0, 0)),
                   pl.BlockSpec((1, D), lambda i: (0, 0))),
        compiler_params=_params("arbitrary"),
    )(h2, tgt, g)


def _rot_half(x):
    lane = lax.broadcasted_iota(jnp.int32, x.shape, 1)
    return jnp.where(lane < ROPE // 2, -pltpu.roll(x, 128 - ROPE // 2, 1), pltpu.roll(x, ROPE // 2, 1))


def _rope_fwd(qp, kv, kr, cos, sin, heads):
    L = qp.shape[0]
    tm = _tile(L, (128,))

    def body(q_ref, kv_ref, kr_ref, c_ref, s_ref, qc_ref, kc_ref, v_ref):
        c, s = c_ref[...], s_ref[...]
        krv = kr_ref[...]
        kr_rot = (krv * c + _rot_half(krv) * s).astype(BF16)
        for h in range(heads):
            lo = h * QPAD
            qc_ref[:, lo:lo + HEAD] = (q_ref[:, lo:lo + HEAD] * SOFTMAX_SCALE).astype(BF16)
            qr = q_ref[:, lo + HEAD:lo + QPAD]
            qc_ref[:, lo + HEAD:lo + QPAD] = ((qr * c + _rot_half(qr) * s) * SOFTMAX_SCALE).astype(BF16)
            kc_ref[:, lo:lo + HEAD] = kv_ref[:, lo:lo + HEAD].astype(BF16)
            kc_ref[:, lo + HEAD:lo + QPAD] = kr_rot
            v_ref[:, h * HEAD:(h + 1) * HEAD] = kv_ref[:, lo + HEAD:lo + QPAD].astype(BF16)

    W = heads * QPAD
    row = lambda w: pl.BlockSpec((tm, w), lambda i: (i, 0))
    return pl.pallas_call(
        body, name="rope_fwd",
        out_shape=(jax.ShapeDtypeStruct((L, W), BF16), jax.ShapeDtypeStruct((L, W), BF16),
                   jax.ShapeDtypeStruct((L, heads * HEAD), BF16)),
        grid=(L // tm,),
        in_specs=[row(W), row(W), row(128), row(128), row(128)],
        out_specs=(row(W), row(W), row(heads * HEAD)),
        compiler_params=_params("parallel"),
    )(qp, kv, kr, cos, sin)


def _rope_bwd(dqc, dkc, dv, cos, sin, heads):
    L = dqc.shape[0]
    tm = _tile(L, (128,))

    def body(dq_ref, dk_ref, dv_ref, c_ref, s_ref, dqp_ref, dkv_ref, dkr_ref):
        c, s = c_ref[...], s_ref[...]
        acc = jnp.zeros((tm, 128), F32)
        for h in range(heads):
            lo = h * QPAD
            dqp_ref[:, lo:lo + HEAD] = (dq_ref[:, lo:lo + HEAD] * SOFTMAX_SCALE).astype(BF16)
            d = dq_ref[:, lo + HEAD:lo + QPAD]
            dqp_ref[:, lo + HEAD:lo + QPAD] = ((d * c - _rot_half(d) * s) * SOFTMAX_SCALE).astype(BF16)
            dkv_ref[:, lo:lo + HEAD] = dk_ref[:, lo:lo + HEAD].astype(BF16)
            dkv_ref[:, lo + HEAD:lo + QPAD] = dv_ref[:, h * HEAD:(h + 1) * HEAD].astype(BF16)
            acc = acc + dk_ref[:, lo + HEAD:lo + QPAD]
        dkr_ref[...] = (acc * c - _rot_half(acc) * s).astype(BF16)

    W = heads * QPAD
    row = lambda w: pl.BlockSpec((tm, w), lambda i: (i, 0))
    return pl.pallas_call(
        body, name="rope_bwd",
        out_shape=(jax.ShapeDtypeStruct((L, W), BF16), jax.ShapeDtypeStruct((L, W), BF16),
                   jax.ShapeDtypeStruct((L, 128), BF16)),
        grid=(L // tm,),
        in_specs=[row(W), row(W), row(heads * HEAD), row(128), row(128)],
        out_specs=(row(W), row(W), row(128)),
        compiler_params=_params("parallel"),
    )(dqc, dkc, dv, cos, sin)


def _attn_keep(qi, ki, ta):
    t = qi * ta + lax.broadcasted_iota(jnp.int32, (ta, ta), 0)
    s = ki * ta + lax.broadcasted_iota(jnp.int32, (ta, ta), 1)
    return (s <= t) & ((s >= PAD_LEN) | (s == t))


def _attn_fwd(qc, kc, v, heads):
    L = qc.shape[0]
    ta = _tile(L, (640, 128))
    nb = L // ta
    pairs = [(i, j) for i in range(nb) for j in range(i + 1)]
    q_of = jnp.asarray([p[0] for p in pairs], jnp.int32)
    k_of = jnp.asarray([p[1] for p in pairs], jnp.int32)

    def body(q_of_ref, k_of_ref, q_ref, k_ref, v_ref, o_ref, lse_ref, m_sc, l_sc, acc_sc):
        t = pl.program_id(1)
        qi, ki = q_of_ref[t], k_of_ref[t]

        @pl.when(ki == 0)
        def _():
            m_sc[...] = jnp.full_like(m_sc, NEG)
            l_sc[...] = jnp.zeros_like(l_sc)
            acc_sc[...] = jnp.zeros_like(acc_sc)

        def step(masked):
            s = _dot(q_ref[...], k_ref[...], NT)
            if masked:
                s = jnp.where(_attn_keep(qi, ki, ta), s, NEG)
            m_old = m_sc[...]
            m_new = jnp.maximum(m_old, jnp.max(s, axis=-1, keepdims=True))
            p = jnp.exp(s - jnp.tile(m_new, (1, ta // HEAD)))
            alpha = jnp.exp(m_old - m_new)
            l_sc[...] = alpha * l_sc[...] + jnp.sum(p, axis=-1, keepdims=True)
            acc_sc[...] = alpha * acc_sc[...] + _dot(p, v_ref[...], NN)
            m_sc[...] = m_new

        pl.when((ki == qi) | ((ki == 0) & (qi > 0)))(lambda: step(True))
        pl.when((ki > 0) & (ki < qi))(lambda: step(False))

        @pl.when(ki == qi)
        def _():
            l = l_sc[...]
            o_ref[...] = (acc_sc[...] / l).astype(BF16)
            lse_ref[...] = m_sc[...] + jnp.log(l)

    qrow = lambda w: pl.BlockSpec((ta, w), lambda h, t, q_of, k_of: (q_of[t], h))
    krow = lambda w: pl.BlockSpec((ta, w), lambda h, t, q_of, k_of: (k_of[t], h))
    return pl.pallas_call(
        body, name="attn_fwd",
        out_shape=(jax.ShapeDtypeStruct((L, heads * HEAD), BF16), jax.ShapeDtypeStruct((L, heads * HEAD), F32)),
        grid_spec=pltpu.PrefetchScalarGridSpec(
            num_scalar_prefetch=2, grid=(heads, len(pairs)),
            in_specs=[qrow(QPAD), krow(QPAD), krow(HEAD)],
            out_specs=(qrow(HEAD), qrow(HEAD)),
            scratch_shapes=[pltpu.VMEM((ta, HEAD), F32), pltpu.VMEM((ta, HEAD), F32), pltpu.VMEM((ta, HEAD), F32)]),
        compiler_params=_params("parallel", "arbitrary"),
    )(q_of, k_of, qc, kc, v)


def _attn_delta(o, do, heads):
    L = o.shape[0]
    tm = _tile(L, (128,))

    def body(o_ref, do_ref, d_ref):
        for h in range(heads):
            cols = slice(h * HEAD, (h + 1) * HEAD)
            d = jnp.sum(do_ref[:, cols].astype(F32) * o_ref[:, cols].astype(F32), axis=-1, keepdims=True)
            d_ref[:, cols] = jnp.broadcast_to(d, (tm, HEAD))

    row = pl.BlockSpec((tm, heads * HEAD), lambda i: (i, 0))
    return pl.pallas_call(
        body, name="attn_delta", out_shape=jax.ShapeDtypeStruct((L, heads * HEAD), F32), grid=(L // tm,),
        in_specs=[row, row], out_specs=row, compiler_params=_params("parallel"),
    )(o, do)


def _attn_bwd(qc, kc, v, do, lse, delta, heads):
    L = qc.shape[0]
    ta = _tile(L, (640, 128))
    nb = L // ta
    pairs = [(j, i) for j in range(nb) for i in range(j, nb)]
    k_of = jnp.asarray([p[0] for p in pairs], jnp.int32)
    q_of = jnp.asarray([p[1] for p in pairs], jnp.int32)

    def body(k_of_ref, q_of_ref, q_ref, k_ref, v_ref, do_ref, lse_ref, delta_ref, dq_ref, dk_ref, dv_ref):
        t = pl.program_id(1)
        kj, qi = k_of_ref[t], q_of_ref[t]

        @pl.when(t == 0)
        def _():
            dq_ref[...] = jnp.zeros_like(dq_ref)

        @pl.when(qi == kj)
        def _():
            dk_ref[...] = jnp.zeros_like(dk_ref)
            dv_ref[...] = jnp.zeros_like(dv_ref)

        def step(masked):
            q, k, vv, dov = q_ref[...], k_ref[...], v_ref[...], do_ref[...]
            wide = (1, ta // HEAD)
            p = jnp.exp(_dot(q, k, NT) - jnp.tile(lse_ref[...], wide))
            if masked:
                p = jnp.where(_attn_keep(qi, kj, ta), p, 0.0)
            ds = p * (_dot(dov, vv, NT) - jnp.tile(delta_ref[...], wide))
            dv_ref[...] += _dot(p, dov, TN)
            dk_ref[...] += _dot(ds, q, TN)
            rows = pl.ds(pl.multiple_of(qi * ta, ta), ta)
            dq_ref[rows, :] += _dot(ds, k, NN)

        pl.when((qi == kj) | ((kj == 0) & (qi > 0)))(lambda: step(True))
        pl.when((kj > 0) & (qi > kj))(lambda: step(False))

    qrow = lambda w: pl.BlockSpec((ta, w), lambda h, t, k_of, q_of: (q_of[t], h))
    krow = lambda w: pl.BlockSpec((ta, w), lambda h, t, k_of, q_of: (k_of[t], h))
    return pl.pallas_call(
        body, name="attn_bwd",
        out_shape=(jax.ShapeDtypeStruct((L, heads * QPAD), F32), jax.ShapeDtypeStruct((L, heads * QPAD), F32),
                   jax.ShapeDtypeStruct((L, heads * HEAD), F32)),
        grid_spec=pltpu.PrefetchScalarGridSpec(
            num_scalar_prefetch=2, grid=(heads, len(pairs)),
            in_specs=[qrow(QPAD), krow(QPAD), krow(HEAD), qrow(HEAD), qrow(HEAD), qrow(HEAD)],
            out_specs=(pl.BlockSpec((L, QPAD), lambda h, t, k_of, q_of: (0, h)), krow(QPAD), krow(HEAD))),
        compiler_params=_params("parallel", "arbitrary"),
    )(k_of, q_of, qc, kc, v, do, lse, delta)


def _hgrn_constants():
    t = jnp.arange(CHUNK)
    tril = (t[None, :] <= t[:, None]).astype(BF16)
    sel = []
    for lv in range(LEVELS):
        hs = 1 << lv
        mid = (t // (2 * hs)) * (2 * hs) + hs - 1
        sel.append((t[None, :] == mid[:, None]).astype(BF16))
    return tril, jnp.concatenate(sel, axis=0)


def _hgrn_decay_grad_constants():
    r = jnp.arange(CHUNK)[:, None]
    c = jnp.arange(CHUNK)[None, :]
    mats = []
    for lv in range(LEVELS):
        same = (r >> (lv + 1)) == (c >> (lv + 1))
        second = ((r >> lv) & 1) == 1
        mats.append(same & jnp.where(second, c >= r, c < r))
    mats += [c >= r, c < r]
    return jnp.concatenate(mats, axis=1).astype(BF16)


def _hgrn_gates(hq_ref, hf_ref, hi_ref, lb_ref, cols, row0):
    rows = row0 + lax.broadcasted_iota(jnp.int32, (CHUNK, 1), 0)
    valid = rows >= PAD_LEN
    lb = 1.0 / (1.0 + jnp.exp(lb_ref[1:2, cols] - lb_ref[0:1, cols]))
    hq = hq_ref[:, cols]
    sq = _sigmoid(hq)
    sg = _sigmoid(hf_ref[:, cols])
    f = lb + (1.0 - lb) * sg
    g = jnp.where(valid, jnp.log(f), 0.0)
    k = jnp.where(valid, 1.0 - f, 0.0)
    return dict(q=hq * sq, sq=sq, hq=hq, k=k, v=hi_ref[:, cols], g=g, f=f, sg=sg, lb=lb, valid=valid)


def _hgrn_prefix(tril_ref, sel_ref, gs):
    n = len(gs)
    b_all = _dot_exact_rhs(tril_ref[...], jnp.concatenate(gs, axis=1) if n > 1 else gs[0], NN)
    bm_all = _dot_exact_rhs(sel_ref[...], b_all, NN)
    cut = lambda a, i: a[:, i * HEAD:(i + 1) * HEAD]
    return [cut(b_all, i) for i in range(n)], [cut(bm_all, i) for i in range(n)]


def _hgrn_levels(q, k, b, bm_all):
    t = lax.broadcasted_iota(jnp.int32, (CHUNK, 1), 0)
    tt = lax.broadcasted_iota(jnp.int32, (CHUNK, CHUNK), 0)
    ss = lax.broadcasted_iota(jnp.int32, (CHUNK, CHUNK), 1)
    out = []
    for lv in range(LEVELS):
        bm = bm_all[lv * CHUNK:(lv + 1) * CHUNK, :]
        second = ((t >> lv) & 1) == 1
        eq = jnp.where(second, jnp.exp(jnp.minimum(b - bm, 0.0)), 0.0)
        ek = jnp.where(second, 0.0, jnp.exp(jnp.minimum(bm - b, 0.0)))
        same = (tt >> (lv + 1)) == (ss >> (lv + 1))
        out.append((eq, ek, (q * eq).astype(BF16), (k * ek).astype(BF16), same))
    return out


def _hgrn_intra(qs, ks, levels):
    tt = lax.broadcasted_iota(jnp.int32, (CHUNK, CHUNK), 0)
    ss = lax.broadcasted_iota(jnp.int32, (CHUNK, CHUNK), 1)
    ps = [jnp.where(tt == ss, jnp.sum(q * k, axis=-1, keepdims=True), 0.0) for q, k in zip(qs, ks)]
    for lv in range(LEVELS):
        for i, lvl in enumerate(levels):
            _, _, ql, kl, same = lvl[lv]
            ps[i] = ps[i] + jnp.where(same, _dot(ql, kl, NT), 0.0)
    return ps


def _hgrn_fwd(hp, lb_raw, g_norm, nh):
    L = hp.shape[0]
    D = nh * HEAD
    nc = L // CHUNK
    per = _tile(nh, (HGRN_FWD_HEADS, 2, 1))
    ng = nh // per
    tril, sel = _hgrn_constants()

    def body(hq_ref, hf_ref, hi_ref, hg_ref, lb_ref, gn_ref, tril_ref, sel_ref,
             oh_ref, orec_ref, shist_ref, s_sc, b_sc):
        c = pl.program_id(1)

        @pl.when(c == 0)
        def _():
            s_sc[...] = jnp.zeros_like(s_sc)

        heads_here = range(per)
        lanes = [slice(hh * HEAD, (hh + 1) * HEAD) for hh in heads_here]
        ws = [_hgrn_gates(hq_ref, hf_ref, hi_ref, lb_ref, lanes[hh], c * CHUNK) for hh in heads_here]
        qs, ks, vs = [w["q"] for w in ws], [w["k"] for w in ws], [w["v"] for w in ws]
        bs, bms = _hgrn_prefix(tril_ref, sel_ref, [w["g"] for w in ws])
        for hh in heads_here:
            b_sc[hh] = bs[hh]
        b_lasts = [b_sc[hh, CHUNK - 1:CHUNK, :] for hh in heads_here]
        ps = _hgrn_intra(qs, ks, [_hgrn_levels(qs[hh], ks[hh], bs[hh], bms[hh]) for hh in heads_here])
        s_ins = [s_sc[hh] for hh in heads_here]
        os_ = [_dot(ps[hh], vs[hh], NN) + _dot(qs[hh] * jnp.exp(bs[hh]), s_ins[hh], NT) for hh in heads_here]
        for hh in heads_here:
            shist_ref[0, hh] = s_ins[hh]
            s_sc[hh] = (jnp.exp(b_lasts[hh]) * s_ins[hh]
                        + _dot(vs[hh], ks[hh] * jnp.exp(b_lasts[hh] - bs[hh]), TN))
        for hh in heads_here:
            o = os_[hh]
            orec_ref[:, lanes[hh]] = o
            rn = lax.rsqrt(jnp.mean(o * o, axis=-1, keepdims=True) + NORM_EPS)
            hg = hg_ref[:, lanes[hh]]
            oh_ref[:, lanes[hh]] = (((o * rn) * gn_ref[...]) * (hg * _sigmoid(hg))).astype(BF16)

    col = lambda grp: pl.BlockSpec((CHUNK, per * HEAD), lambda h, c: (c, grp * ng + h))
    const = lambda shape: pl.BlockSpec(shape, lambda h, c: (0, 0))
    return pl.pallas_call(
        body, name="hgrn_fwd",
        out_shape=(jax.ShapeDtypeStruct((L, D), BF16), jax.ShapeDtypeStruct((L, D), F32),
                   jax.ShapeDtypeStruct((nc, nh, HEAD, HEAD), F32)),
        grid=(ng, nc),
        in_specs=[col(0), col(1), col(2), col(3),
                  pl.BlockSpec((2, per * HEAD), lambda h, c: (0, h)), const((1, HEAD)),
                  const((CHUNK, CHUNK)), const((LEVELS * CHUNK, CHUNK))],
        out_specs=(pl.BlockSpec((CHUNK, per * HEAD), lambda h, c: (c, h)),
                   pl.BlockSpec((CHUNK, per * HEAD), lambda h, c: (c, h)),
                   pl.BlockSpec((1, per, HEAD, HEAD), lambda h, c: (c, h, 0, 0))),
        scratch_shapes=[pltpu.VMEM((per, HEAD, HEAD), F32), pltpu.VMEM((per, CHUNK, HEAD), F32)],
        compiler_params=_params("parallel", "arbitrary"),
    )(hp, hp, hp, hp, lb_raw, g_norm, tril, sel)


def _hgrn_bwd(hp, lb_raw, g_norm, do_h, o_rec, s_hist, nh):
    L = hp.shape[0]
    D = nh * HEAD
    nc = L // CHUNK
    per = _tile(nh, (HGRN_BWD_HEADS, 1))
    ng = nh // per
    tril, sel = _hgrn_constants()
    tdec = _hgrn_decay_grad_constants()

    def body(hq_ref, hf_ref, hi_ref, hg_ref, lb_ref, gn_ref, tril_ref, sel_ref, tdec_ref, do_ref, orec_ref, shist_ref,
             dhq_ref, dhf_ref, dhi_ref, dhg_ref, dgn_ref, dlb_ref, ds_sc, b_sc):
        ci = pl.program_id(1)
        c = nc - 1 - ci

        @pl.when(ci == 0)
        def _():
            ds_sc[...] = jnp.zeros_like(ds_sc)
            dgn_ref[...] = jnp.zeros_like(dgn_ref)
            dlb_ref[...] = jnp.zeros_like(dlb_ref)

        heads_here = range(per)
        lanes = [slice(hh * HEAD, (hh + 1) * HEAD) for hh in heads_here]
        ws = [_hgrn_gates(hq_ref, hf_ref, hi_ref, lb_ref, lanes[hh], c * CHUNK) for hh in heads_here]
        qs, ks, vs = [w["q"] for w in ws], [w["k"] for w in ws], [w["v"] for w in ws]
        bs, bms = _hgrn_prefix(tril_ref, sel_ref, [w["g"] for w in ws])
        for hh in heads_here:
            b_sc[hh] = bs[hh]
        b_lasts = [b_sc[hh, CHUNK - 1:CHUNK, :] for hh in heads_here]
        levels = [_hgrn_levels(qs[hh], ks[hh], bs[hh], bms[hh]) for hh in heads_here]
        ps = _hgrn_intra(qs, ks, levels)
        s_ins = [shist_ref[0, hh] for hh in heads_here]
        ds_outs = [ds_sc[hh] for hh in heads_here]
        ebs = [jnp.exp(b) for b in bs]
        etails = [jnp.exp(b_lasts[hh] - bs[hh]) for hh in heads_here]
        decays = [jnp.exp(bl) for bl in b_lasts]

        dos = []
        for hh in heads_here:
            o = orec_ref[:, lanes[hh]]
            hg = hg_ref[:, lanes[hh]]
            sgg = _sigmoid(hg)
            rn = lax.rsqrt(jnp.mean(o * o, axis=-1, keepdims=True) + NORM_EPS)
            on = o * rn
            doh = do_ref[:, lanes[hh]]
            dy = doh * (hg * sgg)
            dhg_ref[:, lanes[hh]] = (doh * (on * gn_ref[...]) * (sgg * (1.0 + hg * (1.0 - sgg)))).astype(BF16)
            dgn_ref[hh] += jnp.broadcast_to(jnp.sum(dy * on, axis=0, keepdims=True), (8, HEAD))
            z = dy * gn_ref[...]
            dos.append(rn * z - o * ((rn * rn * rn) * jnp.mean(z * o, axis=-1, keepdims=True)))

        tt = lax.broadcasted_iota(jnp.int32, (CHUNK, CHUNK), 0)
        ss = lax.broadcasted_iota(jnp.int32, (CHUNK, CHUNK), 1)
        dps = [jnp.where(ss <= tt, _dot(dos[hh], vs[hh], NT), 0.0) for hh in heads_here]
        dvs = [_dot(ps[hh], dos[hh], TN) + _dot(ks[hh] * etails[hh], ds_outs[hh], NT) for hh in heads_here]
        dq_states = [ebs[hh] * _dot(dos[hh], s_ins[hh], NN) for hh in heads_here]
        dk_states = [etails[hh] * _dot(vs[hh], ds_outs[hh], NN) for hh in heads_here]
        dpds = [jnp.sum(jnp.where(tt == ss, dp, 0.0), axis=-1, keepdims=True) for dp in dps]
        dqs = [dpds[hh] * ks[hh] + dq_states[hh] for hh in heads_here]
        dks = [dpds[hh] * qs[hh] + dk_states[hh] for hh in heads_here]
        pair_terms = [[] for _ in heads_here]
        for lv in range(LEVELS):
            for hh in heads_here:
                eq, ek, ql, kl, same = levels[hh][lv]
                dpl = jnp.where(same, dps[hh], 0.0)
                dq_l = eq * _dot(dpl, kl, NN)
                dk_l = ek * _dot(dpl, ql, TN)
                dqs[hh] = dqs[hh] + dq_l
                dks[hh] = dks[hh] + dk_l
                pair_terms[hh].append(qs[hh] * dq_l + ks[hh] * dk_l)
        for hh in heads_here:
            pair_terms[hh] += [qs[hh] * dq_states[hh], ks[hh] * dk_states[hh]]
            ds_sc[hh] = decays[hh] * ds_outs[hh] + _dot(dos[hh], qs[hh] * ebs[hh], TN)
        stacked = [jnp.concatenate(terms, axis=0) for terms in pair_terms]
        dg_all = _dot_exact_rhs(tdec_ref[...], jnp.concatenate(stacked, axis=1) if per > 1 else stacked[0], NN)

        for hh in heads_here:
            w = ws[hh]
            f, sg, lb, sq, hq = w["f"], w["sg"], w["lb"], w["sq"], w["hq"]
            through = jnp.sum((decays[hh] * s_ins[hh]) * ds_outs[hh], axis=0, keepdims=True)
            dg = dg_all[:, lanes[hh]] + through
            df = jnp.where(w["valid"], dg / f - dks[hh], 0.0)
            dhf_ref[:, lanes[hh]] = (df * (1.0 - lb) * sg * (1.0 - sg)).astype(BF16)
            dlb_ref[hh] += jnp.broadcast_to(jnp.sum(df * (1.0 - sg), axis=0, keepdims=True), (8, HEAD))
            dhq_ref[:, lanes[hh]] = (dqs[hh] * (sq * (1.0 + hq * (1.0 - sq)))).astype(BF16)
            dhi_ref[:, lanes[hh]] = dvs[hh].astype(BF16)

    col = lambda grp: pl.BlockSpec((CHUNK, per * HEAD), lambda h, c: (nc - 1 - c, grp * ng + h))
    const = lambda shape: pl.BlockSpec(shape, lambda h, c: (0, 0))
    tile = pl.BlockSpec((CHUNK, per * HEAD), lambda h, c: (nc - 1 - c, h))
    part = pl.BlockSpec((per, 8, HEAD), lambda h, c: (h, 0, 0))
    return pl.pallas_call(
        body, name="hgrn_bwd",
        out_shape=tuple([jax.ShapeDtypeStruct((L, D), BF16)] * 4 + [jax.ShapeDtypeStruct((nh, 8, HEAD), F32)] * 2),
        grid=(ng, nc),
        in_specs=[col(0), col(1), col(2), col(3),
                  pl.BlockSpec((2, per * HEAD), lambda h, c: (0, h)), const((1, HEAD)),
                  const((CHUNK, CHUNK)), const((LEVELS * CHUNK, CHUNK)), const((CHUNK, (LEVELS + 2) * CHUNK)),
                  tile, tile, pl.BlockSpec((1, per, HEAD, HEAD), lambda h, c: (nc - 1 - c, h, 0, 0))],
        out_specs=(tile, tile, tile, tile, part, part),
        scratch_shapes=[pltpu.VMEM((per, HEAD, HEAD), F32), pltpu.VMEM((per, CHUNK, HEAD), F32)],
        compiler_params=_params("parallel", "arbitrary"),
    )(hp, hp, hp, hp, lb_raw, g_norm, tril, sel, tdec, do_h, o_rec, s_hist)


def _merge_fwd(a, bm, gates):
    L, D = a.shape
    tm = _tile(L, (128,))

    def body(a_ref, b_ref, g_ref, o_ref):
        o_ref[...] = (_sigmoid(g_ref[:, :D]) * a_ref[...] + _sigmoid(g_ref[:, D:]) * b_ref[...]).astype(BF16)

    row = lambda w: pl.BlockSpec((tm, w), lambda i: (i, 0))
    return pl.pallas_call(
        body, name="merge_fwd", out_shape=jax.ShapeDtypeStruct((L, D), BF16), grid=(L // tm,),
        in_specs=[row(D), row(D), row(2 * D)], out_specs=row(D), compiler_params=_params("parallel"),
    )(a, bm, gates)


def _merge_bwd(dm, a, bm, gates):
    L, D = a.shape
    tm = _tile(L, (128,))

    def body(dm_ref, a_ref, b_ref, g_ref, da_ref, db_ref, dg_ref):
        d = dm_ref[...]
        sa, sb = _sigmoid(g_ref[:, :D]), _sigmoid(g_ref[:, D:])
        da_ref[...] = (d * sa).astype(BF16)
        db_ref[...] = (d * sb).astype(BF16)
        dg_ref[:, :D] = (d * a_ref[...] * sa * (1.0 - sa)).astype(BF16)
        dg_ref[:, D:] = (d * b_ref[...] * sb * (1.0 - sb)).astype(BF16)

    row = lambda w: pl.BlockSpec((tm, w), lambda i: (i, 0))
    return pl.pallas_call(
        body, name="merge_bwd",
        out_shape=(jax.ShapeDtypeStruct((L, D), BF16), jax.ShapeDtypeStruct((L, D), BF16),
                   jax.ShapeDtypeStruct((L, 2 * D), BF16)),
        grid=(L // tm,),
        in_specs=[row(D), row(D), row(D), row(2 * D)], out_specs=(row(D), row(D), row(2 * D)),
        compiler_params=_params("parallel"),
    )(dm, a, bm, gates)


def _conv_taps(g_ref, halo_ref, i, tm):
    rows = i * tm + lax.broadcasted_iota(jnp.int32, (tm, 1), 0)
    g0 = jnp.where(rows >= PAD_LEN, g_ref[...], 0.0)
    hrow = i * tm - 8 + lax.broadcasted_iota(jnp.int32, (8, 1), 0)
    halo = jnp.where(hrow >= PAD_LEN, halo_ref[...], 0.0)
    r = lax.broadcasted_iota(jnp.int32, (tm, 1), 0)
    h7 = jnp.sum(jnp.where(lax.broadcasted_iota(jnp.int32, (8, 1), 0) == 7, halo, 0.0), axis=0, keepdims=True)
    h6 = jnp.sum(jnp.where(lax.broadcasted_iota(jnp.int32, (8, 1), 0) == 6, halo, 0.0), axis=0, keepdims=True)
    g1 = jnp.where(r == 0, h7, pltpu.roll(g0, 1, 0))
    g2 = jnp.where(r == 0, h6, jnp.where(r == 1, h7, pltpu.roll(g0, 2, 0)))
    return g0, g1, g2


def _conv_fwd(gu, cw, cb):
    L, F2 = gu.shape
    F = F2 // 2
    tm = _tile(L, ROW_TILES)
    tn = _tile(F, (512, 256, 128))
    nj = F // tn

    def body(g_ref, halo_ref, u_ref, cw_ref, cb_ref, o_ref):
        g0, g1, g2 = _conv_taps(g_ref, halo_ref, pl.program_id(0), tm)
        conv = cw_ref[0:1, :] * g2 + cw_ref[1:2, :] * g1 + cw_ref[2:3, :] * g0 + cb_ref[...]
        o_ref[...] = (conv * _sigmoid(conv) * u_ref[...]).astype(BF16)

    return pl.pallas_call(
        body, name="conv_fwd", out_shape=jax.ShapeDtypeStruct((L, F), BF16), grid=(L // tm, nj),
        in_specs=[pl.BlockSpec((tm, tn), lambda i, j: (i, j)),
                  pl.BlockSpec((8, tn), lambda i, j: (jnp.maximum(i * (tm // 8) - 1, 0), j)),
                  pl.BlockSpec((tm, tn), lambda i, j: (i, j + nj)),
                  pl.BlockSpec((3, tn), lambda i, j: (0, j)),
                  pl.BlockSpec((1, tn), lambda i, j: (0, j))],
        out_specs=pl.BlockSpec((tm, tn), lambda i, j: (i, j)),
        compiler_params=_params("parallel", "parallel"),
    )(gu, gu, gu, cw, cb)


def _conv_bwd_a(da, gu, cw, cb):
    L, F2 = gu.shape
    F = F2 // 2
    tm = _tile(L, ROW_TILES)
    tn = _tile(F, (512, 256, 128))
    nj = F // tn

    def body(da_ref, g_ref, halo_ref, u_ref, cw_ref, cb_ref, dc_ref, du_ref, dcb_ref, dcw_ref):
        i = pl.program_id(1)

        @pl.when(i == 0)
        def _():
            dcb_ref[...] = jnp.zeros_like(dcb_ref)
            dcw_ref[...] = jnp.zeros_like(dcw_ref)

        g0, g1, g2 = _conv_taps(g_ref, halo_ref, i, tm)
        conv = cw_ref[0:1, :] * g2 + cw_ref[1:2, :] * g1 + cw_ref[2:3, :] * g0 + cb_ref[...]
        sc = _sigmoid(conv)
        dav = da_ref[...]
        du_ref[...] = (dav * (conv * sc)).astype(BF16)
        dconv = dav * u_ref[...] * (sc * (1.0 + conv * (1.0 - sc)))
        dc_ref[...] = dconv
        dcb_ref[...] += jnp.sum(dconv, axis=0, keepdims=True)
        dcw_ref[0:1, :] += jnp.sum(dconv * g2, axis=0, keepdims=True)
        dcw_ref[1:2, :] += jnp.sum(dconv * g1, axis=0, keepdims=True)
        dcw_ref[2:3, :] += jnp.sum(dconv * g0, axis=0, keepdims=True)

    return pl.pallas_call(
        body, name="conv_bwd_a",
        out_shape=(jax.ShapeDtypeStruct((L, F), F32), jax.ShapeDtypeStruct((L, F), BF16),
                   jax.ShapeDtypeStruct((1, F), F32), jax.ShapeDtypeStruct((8, F), F32)),
        grid=(nj, L // tm),
        in_specs=[pl.BlockSpec((tm, tn), lambda j, i: (i, j)),
                  pl.BlockSpec((tm, tn), lambda j, i: (i, j)),
                  pl.BlockSpec((8, tn), lambda j, i: (jnp.maximum(i * (tm // 8) - 1, 0), j)),
                  pl.BlockSpec((tm, tn), lambda j, i: (i, j + nj)),
                  pl.BlockSpec((3, tn), lambda j, i: (0, j)),
                  pl.BlockSpec((1, tn), lambda j, i: (0, j))],
        out_specs=(pl.BlockSpec((tm, tn), lambda j, i: (i, j)), pl.BlockSpec((tm, tn), lambda j, i: (i, j)),
                   pl.BlockSpec((1, tn), lambda j, i: (0, j)), pl.BlockSpec((8, tn), lambda j, i: (0, j))),
        compiler_params=_params("parallel", "arbitrary"),
    )(da, gu, gu, gu, cw, cb)


def _conv_bwd_b(dconv, cw):
    L, F = dconv.shape
    tm = _tile(L, ROW_TILES)
    tn = _tile(F, (512, 256, 128))
    nblk8 = L // 8
    ni = L // tm

    def body(dc_ref, nxt_ref, cw_ref, o_ref):
        i = pl.program_id(0)
        dc = dc_ref[...]
        nxt = jnp.where(i < ni - 1, nxt_ref[...], 0.0)
        sub = lax.broadcasted_iota(jnp.int32, (8, 1), 0)
        n0 = jnp.sum(jnp.where(sub == 0, nxt, 0.0), axis=0, keepdims=True)
        n1 = jnp.sum(jnp.where(sub == 1, nxt, 0.0), axis=0, keepdims=True)
        r = lax.broadcasted_iota(jnp.int32, (tm, 1), 0)
        d1 = jnp.where(r == tm - 1, n0, pltpu.roll(dc, tm - 1, 0))
        d2 = jnp.where(r == tm - 2, n0, jnp.where(r == tm - 1, n1, pltpu.roll(dc, tm - 2, 0)))
        dg = cw_ref[2:3, :] * dc + cw_ref[1:2, :] * d1 + cw_ref[0:1, :] * d2
        rows = i * tm + r
        o_ref[...] = jnp.where(rows >= PAD_LEN, dg, 0.0).astype(BF16)

    return pl.pallas_call(
        body, name="conv_bwd_b", out_shape=jax.ShapeDtypeStruct((L, F), BF16), grid=(ni, F // tn),
        in_specs=[pl.BlockSpec((tm, tn), lambda i, j: (i, j)),
                  pl.BlockSpec((8, tn), lambda i, j: (jnp.minimum((i + 1) * (tm // 8), nblk8 - 1), j)),
                  pl.BlockSpec((3, tn), lambda i, j: (0, j))],
        out_specs=pl.BlockSpec((tm, tn), lambda i, j: (i, j)),
        compiler_params=_params("parallel", "parallel"),
    )(dconv, dconv, cw)


ANY = pl.BlockSpec(memory_space=pl.ANY)


def _coords():
    return lax.axis_index("x"), lax.axis_index("y"), lax.axis_index("c")


def _flip(v, bit):
    return 1 - v if bit else v


CHIPS = [(1, 0), (0, 1), (1, 1)]
PEERS = [(dx, dy, dc) for dx in (0, 1) for dy in (0, 1) for dc in (0, 1)][1:]


def _gather_shards(big, small):
    nbig, n = len(big), len(big) + len(small)
    arrays = list(big) + list(small)

    def body(*refs):
        ins, outs = refs[:n], refs[n:2 * n]
        ici_send, ici_recv, d2d_send, d2d_recv, local_sems = refs[2 * n:]
        x, y, c = _coords()
        mine = 2 * x + y

        def half(w, h):
            r2 = arrays[w].shape[0] // 2
            return pl.ds(h * r2, r2)

        def ici(w, j, landing):
            px, py = _flip(x, CHIPS[j][0]), _flip(y, CHIPS[j][1])
            slot = 2 * px + py if landing else mine
            if w < nbig:
                src, dst = ins[w].at[half(w, c)], outs[w].at[slot, half(w, c)]
            else:
                src, dst = ins[w], outs[w].at[slot]
            return pltpu.make_async_remote_copy(
                src_ref=src, dst_ref=dst, send_sem=ici_send.at[w * 3 + j], recv_sem=ici_recv.at[w * 3 + j],
                device_id=(px, py, c), device_id_type=MESH)

        def d2d(w, j, landing):
            px, py = _flip(x, CHIPS[j][0]), _flip(y, CHIPS[j][1])
            mine_rows = outs[w].at[2 * px + py, half(w, c)]
            dst = outs[w].at[2 * px + py, half(w, 1 - c)] if landing else mine_rows
            return pltpu.make_async_remote_copy(
                src_ref=mine_rows, dst_ref=dst, send_sem=d2d_send.at[w * 3 + j], recv_sem=d2d_recv.at[w * 3 + j],
                device_id=(x, y, 1 - c), device_id_type=MESH)

        local = [pltpu.make_async_copy(ins[w], outs[w].at[mine], local_sems.at[w]) for w in range(n)]
        for cp in local:
            cp.start()
        for w in range(n):
            for j in range(3):
                ici(w, j, False).start()
        for w in range(n):
            for j in range(3):
                ici(w, j, True).wait_recv()
                if w < nbig:
                    d2d(w, j, False).start()
        for w in range(nbig):
            for j in range(3):
                d2d(w, j, True).wait_recv()
        for w in range(n):
            for j in range(3):
                ici(w, j, False).wait_send()
                if w < nbig:
                    d2d(w, j, False).wait_send()
        for cp in local:
            cp.wait()

    return pl.pallas_call(
        body, name="gather_shards",
        out_shape=tuple(jax.ShapeDtypeStruct((4,) + s.shape, s.dtype) for s in arrays),
        in_specs=[ANY] * n, out_specs=tuple([ANY] * n),
        scratch_shapes=[pltpu.SemaphoreType.DMA((3 * n,)), pltpu.SemaphoreType.DMA((3 * n,)),
                        pltpu.SemaphoreType.DMA((3 * nbig,)), pltpu.SemaphoreType.DMA((3 * nbig,)),
                        pltpu.SemaphoreType.DMA((n,))],
        compiler_params=pltpu.CompilerParams(has_side_effects=True),
    )(*arrays)


def _to_sibling(arrays, name):
    n = len(arrays)

    def body(*refs):
        ins, outs = refs[:n], refs[n:2 * n]
        send_sems, recv_sems = refs[2 * n:]
        x, y, c = _coords()

        def copy(w):
            return pltpu.make_async_remote_copy(
                src_ref=ins[w], dst_ref=outs[w], send_sem=send_sems.at[w], recv_sem=recv_sems.at[w],
                device_id=(x, y, 1 - c), device_id_type=MESH)

        for w in range(n):
            copy(w).start()
        for w in range(n):
            copy(w).wait_recv()
            copy(w).wait_send()

    return pl.pallas_call(
        body, name=name,
        out_shape=tuple(jax.ShapeDtypeStruct(a.shape, a.dtype) for a in arrays),
        in_specs=[ANY] * n, out_specs=tuple([ANY] * n),
        scratch_shapes=[pltpu.SemaphoreType.DMA((n,)), pltpu.SemaphoreType.DMA((n,))],
        compiler_params=pltpu.CompilerParams(has_side_effects=True),
    )(*arrays)


def _pair_sum(a, b, name):
    _, r, c = a.shape
    tr = _tile(r, (64, 32, 16))

    def body(a_ref, b_ref, o_ref):
        o_ref[...] = (a_ref[...].astype(F32) + b_ref[...].astype(F32)).astype(BF16)

    blk = pl.BlockSpec((4, tr, c), lambda i: (0, i, 0))
    return pl.pallas_call(
        body, name=name, out_shape=jax.ShapeDtypeStruct(a.shape, BF16), grid=(r // tr,),
        in_specs=[blk, blk], out_specs=blk, compiler_params=_params("parallel"),
    )(a, b)


def _scatter_chip_parts(parts):
    n = len(parts)

    def body(*refs):
        ins, outs = refs[:n], refs[n:2 * n]
        send_sems, recv_sems, local_sems = refs[2 * n:]
        x, y, c = _coords()
        mine = 2 * x + y

        def ici(w, j, landing):
            px, py = _flip(x, CHIPS[j][0]), _flip(y, CHIPS[j][1])
            return pltpu.make_async_remote_copy(
                src_ref=ins[w].at[2 * px + py], dst_ref=outs[w].at[2 * px + py if landing else mine],
                send_sem=send_sems.at[w * 3 + j], recv_sem=recv_sems.at[w * 3 + j],
                device_id=(px, py, c), device_id_type=MESH)

        local = [pltpu.make_async_copy(ins[w].at[mine], outs[w].at[mine], local_sems.at[w]) for w in range(n)]
        for cp in local:
            cp.start()
        for w in range(n):
            for j in range(3):
                ici(w, j, False).start()
        for w in range(n):
            for j in range(3):
                ici(w, j, True).wait_recv()
                ici(w, j, False).wait_send()
        for cp in local:
            cp.wait()

    return pl.pallas_call(
        body, name="scatter_chip_parts",
        out_shape=tuple(jax.ShapeDtypeStruct(p.shape, p.dtype) for p in parts),
        in_specs=[ANY] * n, out_specs=tuple([ANY] * n),
        scratch_shapes=[pltpu.SemaphoreType.DMA((3 * n,)), pltpu.SemaphoreType.DMA((3 * n,)),
                        pltpu.SemaphoreType.DMA((n,))],
        compiler_params=pltpu.CompilerParams(has_side_effects=True),
    )(*parts)


def _sum4(recv, name):
    _, r, c = recv.shape
    tr = _tile(r, (64, 32, 16))

    def body(in_ref, o_ref):
        o_ref[...] = ((in_ref[0].astype(F32) + in_ref[1].astype(F32)) + in_ref[2].astype(F32)) + in_ref[3].astype(F32)

    return pl.pallas_call(
        body, name=name, out_shape=jax.ShapeDtypeStruct((r, c), F32), grid=(r // tr,),
        in_specs=[pl.BlockSpec((4, tr, c), lambda i: (0, i, 0))],
        out_specs=pl.BlockSpec((tr, c), lambda i: (i, 0)),
        compiler_params=_params("parallel"),
    )(recv)


def _allreduce_small(packed):
    R = packed.shape[0]

    def body(in_ref, o_ref, buf, send_sems, recv_sems):
        x, y, c = _coords()
        me = 4 * x + 2 * y + c
        buf[me] = in_ref[...]
        for j, (dx, dy, dc) in enumerate(PEERS):
            px, py, pc = _flip(x, dx), _flip(y, dy), _flip(c, dc)
            pltpu.make_async_remote_copy(
                src_ref=in_ref, dst_ref=buf.at[me], send_sem=send_sems.at[j], recv_sem=recv_sems.at[j],
                device_id=(px, py, pc), device_id_type=MESH).start()
        for j, (dx, dy, dc) in enumerate(PEERS):
            px, py, pc = _flip(x, dx), _flip(y, dy), _flip(c, dc)
            rc = pltpu.make_async_remote_copy(
                src_ref=in_ref, dst_ref=buf.at[4 * px + 2 * py + pc], send_sem=send_sems.at[j],
                recv_sem=recv_sems.at[j], device_id=(px, py, pc), device_id_type=MESH)
            rc.wait_recv()
            rc.wait_send()
        acc = buf[0]
        for s in range(1, 8):
            acc = acc + buf[s]
        o_ref[...] = acc

    return pl.pallas_call(
        body, name="allreduce_small", out_shape=jax.ShapeDtypeStruct((R, 128), F32),
        in_specs=[pl.BlockSpec(memory_space=pltpu.VMEM)], out_specs=pl.BlockSpec(memory_space=pltpu.VMEM),
        scratch_shapes=[pltpu.VMEM((8, R, 128), F32), pltpu.SemaphoreType.DMA((7,)), pltpu.SemaphoreType.DMA((7,))],
        compiler_params=pltpu.CompilerParams(has_side_effects=True, vmem_limit_bytes=VMEM_LIMIT_BYTES),
    )(packed)


def _adamw_math(w, g, m, v):
    m = ADAM_B1 * m + (1.0 - ADAM_B1) * g
    v = ADAM_B2 * v + (1.0 - ADAM_B2) * (g * g)
    m_hat = m / (1.0 - ADAM_B1 ** ADAM_STEP)
    v_hat = v / (1.0 - ADAM_B2 ** ADAM_STEP)
    delta = -ADAM_LR * (m_hat / (jnp.sqrt(v_hat) + ADAM_EPS) + ADAM_WD * w)
    return delta, m, v


def _adamw_big(mine, other, w, m, v, name):
    R, C = w.shape
    tr = _tile(R // 2, (128, 64, 32, 16, 8))
    nb = (R // 2) // tr

    def body(mine_ref, other_ref, w_ref, m_ref, v_ref, g_ref, d_ref, mo_ref, vo_ref):
        is_mine = (pl.program_id(0) // nb) == lax.axis_index("c")
        g = jnp.where(is_mine, mine_ref[...], other_ref[...])
        d, mn, vn = _adamw_math(w_ref[...], g, m_ref[...], v_ref[...])
        g_ref[...] = g
        d_ref[...] = d
        mo_ref[...] = mn
        vo_ref[...] = vn

    blk = pl.BlockSpec((tr, C), lambda i: (i, 0))
    half = pl.BlockSpec((tr, C), lambda i: (i % nb, 0))
    sds = jax.ShapeDtypeStruct((R, C), F32)
    return pl.pallas_call(
        body, name=name, out_shape=(sds, sds, sds, sds), grid=(2 * nb,),
        in_specs=[half, half, blk, blk, blk], out_specs=(blk, blk, blk, blk), compiler_params=_params("parallel"),
    )(mine, other, w, m, v)


def _adamw_small(items, lb_raw, dlb):
    n = len(items)
    lb_w, lb_m, lb_v = lb_raw

    def body(*refs):
        ins = refs[:4 * n]
        dlb_ref, lw_ref, lm_ref, lv_ref = refs[4 * n:4 * n + 4]
        outs = refs[4 * n + 4:]
        for t in range(n):
            g_ref, w_ref, m_ref, v_ref = ins[4 * t:4 * t + 4]
            d, mn, vn = _adamw_math(w_ref[...], g_ref[...], m_ref[...], v_ref[...])
            outs[3 * t][...] = d
            outs[3 * t + 1][...] = mn
            outs[3 * t + 2][...] = vn
        p0 = 1.0 / (1.0 + jnp.exp(lw_ref[1:2, :] - lw_ref[0:1, :]))
        g0 = dlb_ref[...] * p0 * (1.0 - p0)
        base = 3 * n
        outs[base][0:1, :] = g0
        outs[base][1:2, :] = -g0
        d, mn, vn = _adamw_math(lw_ref[...], outs[base][...], lm_ref[...], lv_ref[...])
        outs[base + 1][...] = d
        outs[base + 2][...] = mn
        outs[base + 3][...] = vn

    operands = [a for it in items for a in it] + [dlb, lb_w, lb_m, lb_v]
    out_shape = []
    for (g, w, m, v) in items:
        out_shape += [jax.ShapeDtypeStruct(w.shape, F32)] * 3
    out_shape += [jax.ShapeDtypeStruct(lb_w.shape, F32)] * 4
    vm = pl.BlockSpec(memory_space=pltpu.VMEM)
    res = pl.pallas_call(
        body, name="adamw_small", out_shape=tuple(out_shape),
        in_specs=[vm] * len(operands), out_specs=tuple([vm] * len(out_shape)),
        compiler_params=pltpu.CompilerParams(vmem_limit_bytes=VMEM_LIMIT_BYTES),
    )(*operands)
    deltas = [res[3 * t] for t in range(n)] + [res[3 * n + 1]]
    new_m = [res[3 * t + 1] for t in range(n)] + [res[3 * n + 2]]
    new_v = [res[3 * t + 2] for t in range(n)] + [res[3 * n + 3]]
    return res[3 * n], deltas, new_m, new_v


def _shard_row_half(g, by_cols, h):
    R, C = g.shape
    if by_cols:
        part = lax.dynamic_index_in_dim(g.reshape(2, R // 2, 4, C // 4), h, axis=0, keepdims=False)
        return part.transpose(1, 0, 2).astype(BF16)
    return lax.dynamic_index_in_dim(g.reshape(4, 2, R // 8, C), h, axis=1, keepdims=False).astype(BF16)


def kernel(x, positions, meta_tokens, w_in, w_q_up, w_kv_up, w_branch_mla, w_branch_hgrn, w_out, w_ffn_in, w_ffn_out, conv_w, conv_b, g_mix_norm, g_q_norm, g_kv_norm, g_hgrn_norm, g_ffn_norm, g_final_norm, lb_raw, loss_target, m_meta_tokens, m_w_in, m_w_q_up, m_w_kv_up, m_w_branch_mla, m_w_branch_hgrn, m_w_out, m_w_ffn_in, m_w_ffn_out, m_conv_w, m_conv_b, m_g_mix_norm, m_g_q_norm, m_g_kv_norm, m_g_hgrn_norm, m_g_ffn_norm, m_g_final_norm, m_lb_raw, v_meta_tokens, v_w_in, v_w_q_up, v_w_kv_up, v_w_branch_mla, v_w_branch_hgrn, v_w_out, v_w_ffn_in, v_w_ffn_out, v_conv_w, v_conv_b, v_g_mix_norm, v_g_q_norm, v_g_kv_norm, v_g_hgrn_norm, v_g_ffn_norm, v_g_final_norm, v_lb_raw):
    S, D = x.shape[1], x.shape[2]
    L = S + PREFIX
    QL, KVL = g_q_norm.shape[1], g_kv_norm.shape[1]
    F = conv_b.shape[1]
    heads = (4 * w_kv_up.shape[2]) // QPAD
    nh = D // HEAD
    assert lb_raw.shape[0] == 2 and g_hgrn_norm.shape[1] == HEAD and L % CHUNK == 0
    ix, iy, ic = _coords()
    chip = 2 * ix + iy

    big = [w_in, w_q_up, w_kv_up, w_branch_mla, w_branch_hgrn, w_out, w_ffn_in, w_ffn_out]
    col_sharded = [True, True, True, False, False, False, True, False]
    gathered = _gather_shards([w[0].astype(BF16) for w in big], [meta_tokens, conv_w[0]])

    def full(gw, by_cols):
        _, r, c = gw.shape
        return gw.transpose(1, 0, 2).reshape(r, 4 * c) if by_cols else gw.reshape(4 * r, c)

    W_in, W_q, W_kv, W_a, W_b, W_o, W_fi, W_fo = [full(gw, bc) for gw, bc in zip(gathered[:8], col_sharded)]
    meta_full = full(gathered[8], True)
    cw_full = full(gathered[9], True)
    c0 = QL + KVL
    W_lat = W_in[:, :c0]
    W_kr = jnp.pad(W_in[:, c0:c0 + ROPE], ((0, 0), (0, 128 - ROPE)))
    W_H = W_in[:, c0 + ROPE:c0 + ROPE + 4 * D]
    W_G = W_in[:, c0 + ROPE + 4 * D:]
    W_qp = jnp.pad(W_q.reshape(QL, heads, QK_HEAD), ((0, 0), (0, 0), (0, QPAD - QK_HEAD))).reshape(QL, heads * QPAD)

    h0 = jnp.concatenate([jnp.zeros((PAD_LEN, D), F32), meta_full, x[0]], axis=0)
    pos = jnp.concatenate([jnp.zeros((PAD_LEN,), jnp.int32), jnp.arange(N_META, dtype=jnp.int32),
                           positions[0].astype(jnp.int32) + N_META])
    inv = 1.0 / (ROPE_THETA ** (jnp.arange(0, ROPE, 2, dtype=F32) / ROPE))
    ang = pos.astype(F32)[:, None] * inv
    zero = jnp.zeros((L, 128 - ROPE), F32)
    cos = jnp.concatenate([jnp.cos(ang), jnp.cos(ang), zero], axis=1)
    sin = jnp.concatenate([jnp.sin(ang), jnp.sin(ang), zero], axis=1)

    u1 = _rmsnorm_fwd(h0, g_mix_norm, "norm_mix")
    lat = _mm(u1, W_lat, "nn", F32, "proj_lat")
    hp = _mm(u1, W_H, "nn", F32, "proj_hgrn")
    gates = _mm(u1, W_G, "nn", F32, "proj_gates")
    kr = _mm(u1, W_kr, "nn", F32, "proj_krope")
    qn = _rmsnorm_fwd(lat, g_q_norm, "norm_q", col0=0, width=QL)
    kvn = _rmsnorm_fwd(lat, g_kv_norm, "norm_kv", col0=QL, width=KVL)
    qp = _mm(qn, W_qp, "nn", F32, "q_up")
    kv = _mm(kvn, W_kv, "nn", F32, "kv_up")
    qc, kc, vv = _rope_fwd(qp, kv, kr, cos, sin, heads)
    o_mla, lse = _attn_fwd(qc, kc, vv, heads)
    o_hgrn, o_rec, s_hist = _hgrn_fwd(hp, lb_raw, g_hgrn_norm, nh)
    br_a = _mm(o_mla, W_a, "nn", F32, "branch_mla")
    br_b = _mm(o_hgrn, W_b, "nn", F32, "branch_hgrn")
    merged = _merge_fwd(br_a, br_b, gates)
    h1 = _mm(merged, W_o, "nn", F32, "out_proj", res=h0)
    u2 = _rmsnorm_fwd(h1, g_ffn_norm, "norm_ffn")
    gu = _mm(u2, W_fi, "nn", F32, "ffn_in")
    act = _conv_fwd(gu, cw_full, conv_b)
    h2 = _mm(act, W_fo, "nn", F32, "ffn_out", res=h1)
    dh2, loss_p, dg_final = _final_loss_bwd(h2, loss_target[0], g_final_norm.reshape(1, D))

    dact = _mm(dh2, W_fo, "nt", F32, "d_act")
    dW_fo = _mm(act, dh2, "tn", F32, "dw_ffn_out")
    dconv, dup, dcb, dcw = _conv_bwd_a(dact, gu, cw_full, conv_b)
    dgate = _conv_bwd_b(dconv, cw_full)
    dgu = jnp.concatenate([dgate, dup], axis=1)
    du2 = _mm(dgu, W_fi, "nt", F32, "d_u2")
    dW_fi = _mm(u2, dgu, "tn", F32, "dw_ffn_in")
    dh1, dg_ffn = _rmsnorm_bwd(du2, h1, g_ffn_norm, "norm_ffn_bwd", F32, res=dh2)
    dmerged = _mm(dh1, W_o, "nt", F32, "d_merged")
    dW_o = _mm(merged, dh1, "tn", F32, "dw_out")
    d_a, d_b, d_gates = _merge_bwd(dmerged, br_a, br_b, gates)
    do_mla = _mm(d_a, W_a, "nt", BF16, "d_o_mla")
    dW_a = _mm(o_mla, d_a, "tn", F32, "dw_branch_mla")
    do_hgrn = _mm(d_b, W_b, "nt", F32, "d_o_hgrn")
    dW_b = _mm(o_hgrn, d_b, "tn", F32, "dw_branch_hgrn")
    dhq, dhf, dhi, dhg, dgn_p, dlb_p = _hgrn_bwd(hp, lb_raw, g_hgrn_norm, do_hgrn, o_rec, s_hist, nh)
    dqc, dkc, dvv = _attn_bwd(qc, kc, vv, do_mla, lse, _attn_delta(o_mla, do_mla, heads), heads)
    dqp, dkv, dkr = _rope_bwd(dqc, dkc, dvv, cos, sin, heads)
    dqn = _mm(dqp, W_qp, "nt", F32, "d_qn")
    dW_qp = _mm(qn, dqp, "tn", F32, "dw_q_up")
    dkvn = _mm(dkv, W_kv, "nt", F32, "d_kvn")
    dW_kv = _mm(kvn, dkv, "tn", F32, "dw_kv_up")
    dq_lat, dg_q = _rmsnorm_bwd(dqn, lat, g_q_norm, "norm_q_bwd", BF16, col0=0)
    dkv_lat, dg_kv = _rmsnorm_bwd(dkvn, lat, g_kv_norm, "norm_kv_bwd", BF16, col0=QL)
    dlat = jnp.concatenate([dq_lat, dkv_lat], axis=1)
    dhp = jnp.concatenate([dhq, dhf, dhi, dhg], axis=1)
    du1 = _mm(dlat, W_lat, "nt", F32, "d_u1_lat")
    du1 = _mm(dhp, W_H, "nt", F32, "d_u1_hgrn", res=du1)
    du1 = _mm(d_gates, W_G, "nt", F32, "d_u1_gates", res=du1)
    du1 = _mm(dkr, W_kr, "nt", F32, "d_u1_krope", res=du1)
    dW_lat = _mm(u1, dlat, "tn", F32, "dw_in_lat")
    dW_H = _mm(u1, dhp, "tn", F32, "dw_in_hgrn")
    dW_G = _mm(u1, d_gates, "tn", F32, "dw_in_gates")
    dW_kr = _mm(u1, dkr, "tn", F32, "dw_in_krope")
    dh0, dg_mix = _rmsnorm_bwd(du1, h0, g_mix_norm, "norm_mix_bwd", F32, res=dh1)
    grad_x = dh0[PREFIX:][None]

    dW_in = jnp.concatenate([dW_lat, dW_kr[:, :ROPE], dW_H, dW_G], axis=1)
    dW_q = dW_qp.reshape(QL, heads, QPAD)[:, :, :QK_HEAD].reshape(QL, heads * QK_HEAD)
    big_grads = [dW_in, dW_q, dW_kv, dW_a, dW_b, dW_o, dW_fi, dW_fo]
    names = ["w_in", "w_q_up", "w_kv_up", "w_branch_mla", "w_branch_hgrn", "w_out", "w_ffn_in", "w_ffn_out"]
    keep = [_shard_row_half(g, bc, ic) for g, bc in zip(big_grads, col_sharded)]
    give = [_shard_row_half(g, bc, 1 - ic) for g, bc in zip(big_grads, col_sharded)]
    taken = _to_sibling(give, "pair_exchange")
    chip_parts = [_pair_sum(a, b, "pair_sum_" + nm) for a, b, nm in zip(keep, taken, names)]
    received = _scatter_chip_parts(chip_parts)
    halves = [_sum4(r, "sum_" + nm) for r, nm in zip(received, names)]
    others = _to_sibling(halves, "swap_halves")
    big_m = [m_w_in, m_w_q_up, m_w_kv_up, m_w_branch_mla, m_w_branch_hgrn, m_w_out, m_w_ffn_in, m_w_ffn_out]
    big_v = [v_w_in, v_w_q_up, v_w_kv_up, v_w_branch_mla, v_w_branch_hgrn, v_w_out, v_w_ffn_in, v_w_ffn_out]
    big_out = {}
    for nm, mine, other, w, m, v in zip(names, halves, others, big, big_m, big_v):
        g, d, mn, vn = _adamw_big(mine, other, w[0], m[0], v[0], "adamw_" + nm)
        big_out[nm] = (g[None], d[None], mn[None], vn[None])

    pieces = [loss_p[:, :1], dg_mix, dg_q, dg_kv, jnp.sum(dgn_p[:, 0, :], axis=0, keepdims=True), dg_ffn, dg_final,
              dlb_p[:, 0, :].reshape(1, D), dcb, dcw[0:3].reshape(1, 3 * F), dh0[PAD_LEN:PREFIX].reshape(1, N_META * D)]
    sizes = [p.shape[1] for p in pieces]
    flat = jnp.concatenate(pieces, axis=1)[0]
    rows = -(-flat.shape[0] // 1024) * 8
    packed = jnp.pad(flat, (0, rows * 128 - flat.shape[0])).reshape(rows, 128)
    total = _allreduce_small(packed).reshape(-1)
    offs = [0]
    for s in sizes:
        offs.append(offs[-1] + s)
    loss, g_mix, g_q, g_kv, g_hg, g_ffn, g_fin, dlb, g_cb, g_cw, g_meta = [
        total[offs[t]:offs[t + 1]].reshape(1, sizes[t]) for t in range(len(sizes))]
    g_cw = lax.dynamic_slice_in_dim(g_cw.reshape(3, F), chip * (F // 4), F // 4, axis=1)
    g_meta = lax.dynamic_slice_in_dim(g_meta.reshape(N_META, D), chip * (D // 4), D // 4, axis=1)
    items = [(g_meta, meta_tokens, m_meta_tokens, v_meta_tokens),
             (g_cw, conv_w[0], m_conv_w[0], v_conv_w[0]),
             (g_cb, conv_b, m_conv_b, v_conv_b),
             (g_mix, g_mix_norm, m_g_mix_norm, v_g_mix_norm),
             (g_q, g_q_norm, m_g_q_norm, v_g_q_norm),
             (g_kv, g_kv_norm, m_g_kv_norm, v_g_kv_norm),
             (g_hg, g_hgrn_norm, m_g_hgrn_norm, v_g_hgrn_norm),
             (g_ffn, g_ffn_norm, m_g_ffn_norm, v_g_ffn_norm),
             (g_fin, g_final_norm.reshape(1, D), m_g_final_norm.reshape(1, D), v_g_final_norm.reshape(1, D))]
    g_lb, s_delta, s_m, s_v = _adamw_small(items, (lb_raw, m_lb_raw, v_lb_raw), dlb)
    s_grads = [it[0] for it in items] + [g_lb]

    def shape_small(vals):
        meta, cw, cb, mix, q, kvg, hg, ffn, fin, lb = vals
        return [meta, cw[None], cb, mix, q, kvg, hg, ffn, fin.reshape(D), lb]

    s_grads, s_delta, s_m, s_v = [shape_small(v) for v in (s_grads, s_delta, s_m, s_v)]

    def ordered(kind, small):
        bigs = [big_out[nm][kind] for nm in names]
        return [small[0]] + bigs + small[1:]

    return (loss.reshape(()), grad_x, *ordered(0, s_grads), *ordered(1, s_delta), *ordered(2, s_m), *ordered(3, s_v))
```

```python
import math

import jax
import jax.numpy as jnp
from jax import lax
from jax.experimental import pallas as pl
from jax.experimental.pallas import tpu as pltpu

F32 = jnp.float32
BF16 = jnp.bfloat16
MESH = pl.DeviceIdType.MESH

NORM_EPS = 1e-6
N_META = 16
PREFIX = 128
PAD_LEN = PREFIX - N_META
HEAD = 128
ROPE = 64
QK_HEAD = HEAD + ROPE
QPAD = 2 * HEAD
SOFTMAX_SCALE = QK_HEAD ** -0.5
ROPE_THETA = 10000.0
CHUNK = 128
LEVELS = 7
HGRN_FWD_HEADS = 4
HGRN_BWD_HEADS = 4
NEG = -1e30

ADAM_LR = 0.001
ADAM_B1 = 0.9
ADAM_B2 = 0.999
ADAM_EPS = 1e-08
ADAM_WD = 0.01
ADAM_STEP = 10

VMEM_LIMIT_BYTES = 48 * 1024 * 1024


def _params(*sem):
    return pltpu.CompilerParams(dimension_semantics=sem, vmem_limit_bytes=VMEM_LIMIT_BYTES)


def _tile(n, prefs):
    for p in prefs:
        if n % p == 0:
            return p
    return n


ROW_TILES = (640, 512, 256, 128, 64, 32, 16, 8)
TN_ROW_TILES = (1024, 1408, 1536, 512, 256, 128)
TN_K_TILES = (1664, 640, 512, 256, 128)
COL_TILES = (1024, 512, 256, 128)
K_TILES = (2048, 1536, 1408, 1024, 512, 256, 128)


def _sigmoid(x):
    return 1.0 / (1.0 + jnp.exp(-x))


def _dot(a, b, dims):
    return lax.dot_general(a.astype(BF16), b.astype(BF16), (dims, ((), ())), preferred_element_type=F32)


NN = ((1,), (0,))
TN = ((0,), (0,))
NT = ((1,), (1,))


def _split3(x):
    hi = x.astype(BF16)
    r = x - hi.astype(F32)
    mid = r.astype(BF16)
    lo = (r - mid.astype(F32)).astype(BF16)
    return hi, mid, lo


def _dot_exact_rhs(sel, x, dims):
    hi, mid, lo = _split3(x)
    return _dot(sel, hi, dims) + _dot(sel, mid, dims) + _dot(sel, lo, dims)


def _mm(a, b, mode, out_dtype, name, res=None):
    if mode == "nn":
        (M, K), (K2, N) = a.shape, b.shape
    elif mode == "tn":
        (K, M), (K2, N) = a.shape, b.shape
    else:
        (M, K), (N, K2) = a.shape, b.shape
    assert K == K2, (name, a.shape, b.shape)
    tm = _tile(M, TN_ROW_TILES if mode == "tn" else ROW_TILES)
    tn = _tile(N, COL_TILES)
    tk = _tile(K, TN_K_TILES if mode == "tn" else K_TILES)
    nk = K // tk
    dims = {"nn": NN, "tn": TN, "nt": NT}[mode]

    def body(*refs):
        a_ref, b_ref = refs[:2]
        r_ref = None if res is None else refs[2]
        o_ref = refs[2 if res is None else 3]

        def finish(r):
            if r_ref is not None:
                r = r + r_ref[...].astype(F32)
            o_ref[...] = r.astype(out_dtype)

        if nk == 1:
            finish(_dot(a_ref[...], b_ref[...], dims))
            return
        acc = refs[-1]
        k = pl.program_id(2)

        @pl.when(k == 0)
        def _():
            acc[...] = jnp.zeros_like(acc)

        acc[...] += _dot(a_ref[...], b_ref[...], dims)

        @pl.when(k == nk - 1)
        def _():
            finish(acc[...])

    if mode == "nn":
        a_spec = pl.BlockSpec((tm, tk), lambda i, j, k: (i, k))
        b_spec = pl.BlockSpec((tk, tn), lambda i, j, k: (k, j))
    elif mode == "tn":
        a_spec = pl.BlockSpec((tk, tm), lambda i, j, k: (k, i))
        b_spec = pl.BlockSpec((tk, tn), lambda i, j, k: (k, j))
    else:
        a_spec = pl.BlockSpec((tm, tk), lambda i, j, k: (i, k))
        b_spec = pl.BlockSpec((tn, tk), lambda i, j, k: (j, k))
    in_specs = [a_spec, b_spec]
    operands = [a, b]
    if res is not None:
        in_specs.append(pl.BlockSpec((tm, tn), lambda i, j, k: (i, j)))
        operands.append(res)
    return pl.pallas_call(
        body, name=name,
        out_shape=jax.ShapeDtypeStruct((M, N), out_dtype),
        grid=(M // tm, N // tn, nk),
        in_specs=in_specs,
        out_specs=pl.BlockSpec((tm, tn), lambda i, j, k: (i, j)),
        scratch_shapes=[] if nk == 1 else [pltpu.VMEM((tm, tn), F32)],
        compiler_params=_params("parallel", "parallel", "arbitrary"),
    )(*operands)


def _rmsnorm_fwd(x, g, name, col0=0, width=None):
    L = x.shape[0]
    width = x.shape[1] if width is None else width
    assert col0 % width == 0
    cb = col0 // width
    tm = _tile(L, (128, 64, 32, 16))

    def body(x_ref, g_ref, o_ref):
        xv = x_ref[...]
        r = lax.rsqrt(jnp.mean(xv * xv, axis=-1, keepdims=True) + NORM_EPS)
        o_ref[...] = ((xv * r) * g_ref[...]).astype(BF16)

    return pl.pallas_call(
        body, name=name,
        out_shape=jax.ShapeDtypeStruct((L, width), BF16),
        grid=(L // tm,),
        in_specs=[pl.BlockSpec((tm, width), lambda i: (i, cb)), pl.BlockSpec((1, width), lambda i: (0, 0))],
        out_specs=pl.BlockSpec((tm, width), lambda i: (i, 0)),
        compiler_params=_params("parallel"),
    )(x, g)


def _rmsnorm_bwd(dy, x, g, name, out_dtype, col0=0, res=None):
    L, width = dy.shape
    assert col0 % width == 0
    cb = col0 // width
    tm = _tile(L, (128, 64, 32, 16))

    def body(*refs):
        if res is None:
            dy_ref, x_ref, g_ref, dx_ref, dg_ref = refs
            r_ref = None
        else:
            dy_ref, x_ref, g_ref, r_ref, dx_ref, dg_ref = refs

        @pl.when(pl.program_id(0) == 0)
        def _():
            dg_ref[...] = jnp.zeros_like(dg_ref)

        xv = x_ref[...]
        dyv = dy_ref[...].astype(F32)
        r = lax.rsqrt(jnp.mean(xv * xv, axis=-1, keepdims=True) + NORM_EPS)
        z = dyv * g_ref[...]
        dx = r * z - xv * ((r * r * r) * jnp.mean(z * xv, axis=-1, keepdims=True))
        if r_ref is not None:
            dx = dx + r_ref[...]
        dx_ref[...] = dx.astype(out_dtype)
        dg_ref[...] += jnp.sum(dyv * (xv * r), axis=0, keepdims=True)

    in_specs = [pl.BlockSpec((tm, width), lambda i: (i, 0)),
                pl.BlockSpec((tm, width), lambda i: (i, cb)),
                pl.BlockSpec((1, width), lambda i: (0, 0))]
    operands = [dy, x, g]
    if res is not None:
        in_specs.append(pl.BlockSpec((tm, width), lambda i: (i, 0)))
        operands.append(res)
    return pl.pallas_call(
        body, name=name,
        out_shape=(jax.ShapeDtypeStruct((L, width), out_dtype), jax.ShapeDtypeStruct((1, width), F32)),
        grid=(L // tm,),
        in_specs=in_specs,
        out_specs=(pl.BlockSpec((tm, width), lambda i: (i, 0)), pl.BlockSpec((1, width), lambda i: (0, 0))),
        compiler_params=_params("arbitrary"),
    )(*operands)


def _final_loss_bwd(h2, tgt, g):
    L, D = h2.shape
    tm = PREFIX
    inv_d = 1.0 / D

    def body(h_ref, t_ref, g_ref, dh_ref, loss_ref, dg_ref):
        i = pl.program_id(0)

        @pl.when(i == 0)
        def _():
            loss_ref[...] = jnp.zeros_like(loss_ref)
            dg_ref[...] = jnp.zeros_like(dg_ref)

        xv = h_ref[...]
        r = lax.rsqrt(jnp.mean(xv * xv, axis=-1, keepdims=True) + NORM_EPS)
        xn = xv * r
        y = xn * g_ref[...]
        real = (i >= PREFIX // tm).astype(F32)
        diff = (y - t_ref[...]) * real
        loss_ref[...] += 0.5 * inv_d * jnp.sum(diff * diff)
        dyv = diff * inv_d
        z = dyv * g_ref[...]
        dh_ref[...] = r * z - xv * ((r * r * r) * jnp.mean(z * xv, axis=-1, keepdims=True))
        dg_ref[...] += jnp.sum(dyv * xn, axis=0, keepdims=True)

    shift = PREFIX // tm
    return pl.pallas_call(
        body, name="final_loss_bwd",
        out_shape=(jax.ShapeDtypeStruct((L, D), F32), jax.ShapeDtypeStruct((1, 128), F32),
                   jax.ShapeDtypeStruct((1, D), F32)),
        grid=(L // tm,),
        in_specs=[pl.BlockSpec((tm, D), lambda i: (i, 0)),
                  pl.BlockSpec((tm, D), lambda i: (jnp.maximum(i - shift, 0), 0)),
                  pl.BlockSpec((1, D), lambda i: (0, 0))],
        out_specs=(pl.BlockSpec((tm, D), lambda i: (i, 0)), pl.BlockSpec((1, 128), lambda i: (0, 0)),
                   pl.BlockSpec((1, D), lambda i: (0, 0))),
        compiler_params=_params("arbitrary"),
    )(h2, tgt, g)


def _rot_half(x):
    lane = lax.broadcasted_iota(jnp.int32, x.shape, 1)
    return jnp.where(lane < ROPE // 2, -pltpu.roll(x, 128 - ROPE // 2, 1), pltpu.roll(x, ROPE // 2, 1))


def _rope_fwd(qp, kv, kr, cos, sin, heads):
    L = qp.shape[0]
    tm = _tile(L, (128,))

    def body(q_ref, kv_ref, kr_ref, c_ref, s_ref, qc_ref, kc_ref, v_ref):
        c, s = c_ref[...], s_ref[...]
        krv = kr_ref[...]
        kr_rot = (krv * c + _rot_half(krv) * s).astype(BF16)
        for h in range(heads):
            lo = h * QPAD
            qc_ref[:, lo:lo + HEAD] = (q_ref[:, lo:lo + HEAD] * SOFTMAX_SCALE).astype(BF16)
            qr = q_ref[:, lo + HEAD:lo + QPAD]
            qc_ref[:, lo + HEAD:lo + QPAD] = ((qr * c + _rot_half(qr) * s) * SOFTMAX_SCALE).astype(BF16)
            kc_ref[:, lo:lo + HEAD] = kv_ref[:, lo:lo + HEAD].astype(BF16)
            kc_ref[:, lo + HEAD:lo + QPAD] = kr_rot
            v_ref[:, h * HEAD:(h + 1) * HEAD] = kv_ref[:, lo + HEAD:lo + QPAD].astype(BF16)

    W = heads * QPAD
    row = lambda w: pl.BlockSpec((tm, w), lambda i: (i, 0))
    return pl.pallas_call(
        body, name="rope_fwd",
        out_shape=(jax.ShapeDtypeStruct((L, W), BF16), jax.ShapeDtypeStruct((L, W), BF16),
                   jax.ShapeDtypeStruct((L, heads * HEAD), BF16)),
        grid=(L // tm,),
        in_specs=[row(W), row(W), row(128), row(128), row(128)],
        out_specs=(row(W), row(W), row(heads * HEAD)),
        compiler_params=_params("parallel"),
    )(qp, kv, kr, cos, sin)


def _rope_bwd(dqc, dkc, dv, cos, sin, heads):
    L = dqc.shape[0]
    tm = _tile(L, (128,))

    def body(dq_ref, dk_ref, dv_ref, c_ref, s_ref, dqp_ref, dkv_ref, dkr_ref):
        c, s = c_ref[...], s_ref[...]
        acc = jnp.zeros((tm, 128), F32)
        for h in range(heads):
            lo = h * QPAD
            dqp_ref[:, lo:lo + HEAD] = (dq_ref[:, lo:lo + HEAD] * SOFTMAX_SCALE).astype(BF16)
            d = dq_ref[:, lo + HEAD:lo + QPAD]
            dqp_ref[:, lo + HEAD:lo + QPAD] = ((d * c - _rot_half(d) * s) * SOFTMAX_SCALE).astype(BF16)
            dkv_ref[:, lo:lo + HEAD] = dk_ref[:, lo:lo + HEAD].astype(BF16)
            dkv_ref[:, lo + HEAD:lo + QPAD] = dv_ref[:, h * HEAD:(h + 1) * HEAD].astype(BF16)
            acc = acc + dk_ref[:, lo + HEAD:lo + QPAD]
        dkr_ref[...] = (acc * c - _rot_half(acc) * s).astype(BF16)

    W = heads * QPAD
    row = lambda w: pl.BlockSpec((tm, w), lambda i: (i, 0))
    return pl.pallas_call(
        body, name="rope_bwd",
        out_shape=(jax.ShapeDtypeStruct((L, W), BF16), jax.ShapeDtypeStruct((L, W), BF16),
                   jax.ShapeDtypeStruct((L, 128), BF16)),
        grid=(L // tm,),
        in_specs=[row(W), row(W), row(heads * HEAD), row(128), row(128)],
        out_specs=(row(W), row(W), row(128)),
        compiler_params=_params("parallel"),
    )(dqc, dkc, dv, cos, sin)


def _attn_keep(qi, ki, ta):
    t = qi * ta + lax.broadcasted_iota(jnp.int32, (ta, ta), 0)
    s = ki * ta + lax.broadcasted_iota(jnp.int32, (ta, ta), 1)
    return (s <= t) & ((s >= PAD_LEN) | (s == t))


def _split_refs(refs, sizes):
    out, at = [], 0
    for n in sizes:
        out.append(refs[at:at + n])
        at += n
    return out


def _attn_fwd(qc, kc, v, heads, rider):
    L = qc.shape[0]
    ta = _tile(L, (640, 128))
    nb = L // ta
    pairs = [(i, j) for i in range(nb) for j in range(i + 1)]
    q_of = jnp.asarray([p[0] for p in pairs], jnp.int32)
    k_of = jnp.asarray([p[1] for p in pairs], jnp.int32)
    n_rin, n_rout = len(rider.operands), len(rider.out_shapes)

    def body(*refs):
        (q_of_ref, k_of_ref, q_ref, k_ref, v_ref), r_ins, (o_ref, lse_ref), r_outs, (m_sc, l_sc, acc_sc), r_sems = \
            _split_refs(refs, (5, n_rin, 2, n_rout, 3, len(rider.sem_shapes)))
        h, t = pl.program_id(0), pl.program_id(1)
        qi, ki = q_of_ref[t], k_of_ref[t]
        pl.when((h == 0) & (t == 0))(lambda: rider.start(r_ins, r_outs, r_sems))

        @pl.when(ki == 0)
        def _():
            m_sc[...] = jnp.full_like(m_sc, NEG)
            l_sc[...] = jnp.zeros_like(l_sc)
            acc_sc[...] = jnp.zeros_like(acc_sc)

        def step(masked):
            s = _dot(q_ref[...], k_ref[...], NT)
            if masked:
                s = jnp.where(_attn_keep(qi, ki, ta), s, NEG)
            m_old = m_sc[...]
            m_new = jnp.maximum(m_old, jnp.max(s, axis=-1, keepdims=True))
            p = jnp.exp(s - jnp.tile(m_new, (1, ta // HEAD)))
            alpha = jnp.exp(m_old - m_new)
            l_sc[...] = alpha * l_sc[...] + jnp.sum(p, axis=-1, keepdims=True)
            acc_sc[...] = alpha * acc_sc[...] + _dot(p, v_ref[...], NN)
            m_sc[...] = m_new

        pl.when((ki == qi) | ((ki == 0) & (qi > 0)))(lambda: step(True))
        pl.when((ki > 0) & (ki < qi))(lambda: step(False))

        @pl.when(ki == qi)
        def _():
            l = l_sc[...]
            o_ref[...] = (acc_sc[...] / l).astype(BF16)
            lse_ref[...] = m_sc[...] + jnp.log(l)

        pl.when((h == heads - 1) & (t == len(pairs) - 1))(lambda: rider.finish(r_ins, r_outs, r_sems))

    qrow = lambda w: pl.BlockSpec((ta, w), lambda h, t, q_of, k_of: (q_of[t], h))
    krow = lambda w: pl.BlockSpec((ta, w), lambda h, t, q_of, k_of: (k_of[t], h))
    res = pl.pallas_call(
        body, name="attn_fwd",
        out_shape=(jax.ShapeDtypeStruct((L, heads * HEAD), BF16), jax.ShapeDtypeStruct((L, heads * HEAD), F32),
                   *rider.out_shapes),
        grid_spec=pltpu.PrefetchScalarGridSpec(
            num_scalar_prefetch=2, grid=(heads, len(pairs)),
            in_specs=[qrow(QPAD), krow(QPAD), krow(HEAD)] + [ANY] * n_rin,
            out_specs=(qrow(HEAD), qrow(HEAD), *([ANY] * n_rout)),
            scratch_shapes=[pltpu.VMEM((ta, HEAD), F32), pltpu.VMEM((ta, HEAD), F32), pltpu.VMEM((ta, HEAD), F32)]
            + rider.sem_shapes),
        compiler_params=pltpu.CompilerParams(dimension_semantics=("arbitrary", "arbitrary"),
                                             vmem_limit_bytes=VMEM_LIMIT_BYTES, has_side_effects=True),
    )(q_of, k_of, qc, kc, v, *rider.operands)
    return res[0], res[1], res[2:]


def _attn_delta(o, do, heads):
    L = o.shape[0]
    tm = _tile(L, (128,))

    def body(o_ref, do_ref, d_ref):
        for h in range(heads):
            cols = slice(h * HEAD, (h + 1) * HEAD)
            d = jnp.sum(do_ref[:, cols].astype(F32) * o_ref[:, cols].astype(F32), axis=-1, keepdims=True)
            d_ref[:, cols] = jnp.broadcast_to(d, (tm, HEAD))

    row = pl.BlockSpec((tm, heads * HEAD), lambda i: (i, 0))
    return pl.pallas_call(
        body, name="attn_delta", out_shape=jax.ShapeDtypeStruct((L, heads * HEAD), F32), grid=(L // tm,),
        in_specs=[row, row], out_specs=row, compiler_params=_params("parallel"),
    )(o, do)


def _attn_bwd(qc, kc, v, do, lse, delta, heads, rider):
    L = qc.shape[0]
    ta = _tile(L, (640, 128))
    nb = L // ta
    pairs = [(j, i) for j in range(nb) for i in range(j, nb)]
    k_of = jnp.asarray([p[0] for p in pairs], jnp.int32)
    q_of = jnp.asarray([p[1] for p in pairs], jnp.int32)
    n_rin, n_rout = len(rider.operands), len(rider.out_shapes)

    def body(*refs):
        ((k_of_ref, q_of_ref, q_ref, k_ref, v_ref, do_ref, lse_ref, delta_ref), r_ins, (dq_ref, dk_ref, dv_ref),
         r_outs, r_sems) = _split_refs(refs, (8, n_rin, 3, n_rout, len(rider.sem_shapes)))
        h, t = pl.program_id(0), pl.program_id(1)
        kj, qi = k_of_ref[t], q_of_ref[t]
        pl.when((h == 0) & (t == 0))(lambda: rider.start(r_ins, r_outs, r_sems))

        @pl.when(t == 0)
        def _():
            dq_ref[...] = jnp.zeros_like(dq_ref)

        @pl.when(qi == kj)
        def _():
            dk_ref[...] = jnp.zeros_like(dk_ref)
            dv_ref[...] = jnp.zeros_like(dv_ref)

        def step(masked):
            q, k, vv, dov = q_ref[...], k_ref[...], v_ref[...], do_ref[...]
            wide = (1, ta // HEAD)
            p = jnp.exp(_dot(q, k, NT) - jnp.tile(lse_ref[...], wide))
            if masked:
                p = jnp.where(_attn_keep(qi, kj, ta), p, 0.0)
            ds = p * (_dot(dov, vv, NT) - jnp.tile(delta_ref[...], wide))
            dv_ref[...] += _dot(p, dov, TN)
            dk_ref[...] += _dot(ds, q, TN)
            rows = pl.ds(pl.multiple_of(qi * ta, ta), ta)
            dq_ref[rows, :] += _dot(ds, k, NN)

        pl.when((qi == kj) | ((kj == 0) & (qi > 0)))(lambda: step(True))
        pl.when((kj > 0) & (qi > kj))(lambda: step(False))
        pl.when((h == heads - 1) & (t == len(pairs) - 1))(lambda: rider.finish(r_ins, r_outs, r_sems))

    qrow = lambda w: pl.BlockSpec((ta, w), lambda h, t, k_of, q_of: (q_of[t], h))
    krow = lambda w: pl.BlockSpec((ta, w), lambda h, t, k_of, q_of: (k_of[t], h))
    res = pl.pallas_call(
        body, name="attn_bwd",
        out_shape=(jax.ShapeDtypeStruct((L, heads * QPAD), F32), jax.ShapeDtypeStruct((L, heads * QPAD), F32),
                   jax.ShapeDtypeStruct((L, heads * HEAD), F32), *rider.out_shapes),
        grid_spec=pltpu.PrefetchScalarGridSpec(
            num_scalar_prefetch=2, grid=(heads, len(pairs)),
            in_specs=[qrow(QPAD), krow(QPAD), krow(HEAD), qrow(HEAD), qrow(HEAD), qrow(HEAD)] + [ANY] * n_rin,
            out_specs=(pl.BlockSpec((L, QPAD), lambda h, t, k_of, q_of: (0, h)), krow(QPAD), krow(HEAD),
                       *([ANY] * n_rout)),
            scratch_shapes=rider.sem_shapes),
        compiler_params=pltpu.CompilerParams(dimension_semantics=("arbitrary", "arbitrary"),
                                             vmem_limit_bytes=VMEM_LIMIT_BYTES, has_side_effects=True),
    )(k_of, q_of, qc, kc, v, do, lse, delta, *rider.operands)
    return res[0], res[1], res[2], res[3:]


def _hgrn_constants():
    t = jnp.arange(CHUNK)
    tril = (t[None, :] <= t[:, None]).astype(BF16)
    sel = []
    for lv in range(LEVELS):
        hs = 1 << lv
        mid = (t // (2 * hs)) * (2 * hs) + hs - 1
        sel.append((t[None, :] == mid[:, None]).astype(BF16))
    return tril, jnp.concatenate(sel, axis=0)


def _hgrn_decay_grad_constants():
    r = jnp.arange(CHUNK)[:, None]
    c = jnp.arange(CHUNK)[None, :]
    mats = []
    for lv in range(LEVELS):
        same = (r >> (lv + 1)) == (c >> (lv + 1))
        second = ((r >> lv) & 1) == 1
        mats.append(same & jnp.where(second, c >= r, c < r))
    mats += [c >= r, c < r]
    return jnp.concatenate(mats, axis=1).astype(BF16)


def _hgrn_gates(hq_ref, hf_ref, hi_ref, lb_ref, cols, row0):
    rows = row0 + lax.broadcasted_iota(jnp.int32, (CHUNK, 1), 0)
    valid = rows >= PAD_LEN
    lb = 1.0 / (1.0 + jnp.exp(lb_ref[1:2, cols] - lb_ref[0:1, cols]))
    hq = hq_ref[:, cols]
    sq = _sigmoid(hq)
    sg = _sigmoid(hf_ref[:, cols])
    f = lb + (1.0 - lb) * sg
    g = jnp.where(valid, jnp.log(f), 0.0)
    k = jnp.where(valid, 1.0 - f, 0.0)
    return dict(q=hq * sq, sq=sq, hq=hq, k=k, v=hi_ref[:, cols], g=g, f=f, sg=sg, lb=lb, valid=valid)


def _hgrn_prefix(tril_ref, sel_ref, gs):
    n = len(gs)
    b_all = _dot_exact_rhs(tril_ref[...], jnp.concatenate(gs, axis=1) if n > 1 else gs[0], NN)
    bm_all = _dot_exact_rhs(sel_ref[...], b_all, NN)
    cut = lambda a, i: a[:, i * HEAD:(i + 1) * HEAD]
    return [cut(b_all, i) for i in range(n)], [cut(bm_all, i) for i in range(n)]


def _hgrn_levels(q, k, b, bm_all):
    t = lax.broadcasted_iota(jnp.int32, (CHUNK, 1), 0)
    tt = lax.broadcasted_iota(jnp.int32, (CHUNK, CHUNK), 0)
    ss = lax.broadcasted_iota(jnp.int32, (CHUNK, CHUNK), 1)
    out = []
    for lv in range(LEVELS):
        bm = bm_all[lv * CHUNK:(lv + 1) * CHUNK, :]
        second = ((t >> lv) & 1) == 1
        eq = jnp.where(second, jnp.exp(jnp.minimum(b - bm, 0.0)), 0.0)
        ek = jnp.where(second, 0.0, jnp.exp(jnp.minimum(bm - b, 0.0)))
        same = (tt >> (lv + 1)) == (ss >> (lv + 1))
        out.append((eq, ek, (q * eq).astype(BF16), (k * ek).astype(BF16), same))
    return out


def _hgrn_intra(qs, ks, levels):
    tt = lax.broadcasted_iota(jnp.int32, (CHUNK, CHUNK), 0)
    ss = lax.broadcasted_iota(jnp.int32, (CHUNK, CHUNK), 1)
    ps = [jnp.where(tt == ss, jnp.sum(q * k, axis=-1, keepdims=True), 0.0) for q, k in zip(qs, ks)]
    for lv in range(LEVELS):
        for i, lvl in enumerate(levels):
            _, _, ql, kl, same = lvl[lv]
            ps[i] = ps[i] + jnp.where(same, _dot(ql, kl, NT), 0.0)
    return ps


def _hgrn_fwd(hp, lb_raw, g_norm, nh):
    L = hp.shape[0]
    D = nh * HEAD
    nc = L // CHUNK
    per = _tile(nh, (HGRN_FWD_HEADS, 2, 1))
    ng = nh // per
    tril, sel = _hgrn_constants()

    def body(hq_ref, hf_ref, hi_ref, hg_ref, lb_ref, gn_ref, tril_ref, sel_ref,
             oh_ref, orec_ref, shist_ref, s_sc, b_sc):
        c = pl.program_id(1)

        @pl.when(c == 0)
        def _():
            s_sc[...] = jnp.zeros_like(s_sc)

        heads_here = range(per)
        lanes = [slice(hh * HEAD, (hh + 1) * HEAD) for hh in heads_here]
        ws = [_hgrn_gates(hq_ref, hf_ref, hi_ref, lb_ref, lanes[hh], c * CHUNK) for hh in heads_here]
        qs, ks, vs = [w["q"] for w in ws], [w["k"] for w in ws], [w["v"] for w in ws]
        bs, bms = _hgrn_prefix(tril_ref, sel_ref, [w["g"] for w in ws])
        for hh in heads_here:
            b_sc[hh] = bs[hh]
        b_lasts = [b_sc[hh, CHUNK - 1:CHUNK, :] for hh in heads_here]
        ps = _hgrn_intra(qs, ks, [_hgrn_levels(qs[hh], ks[hh], bs[hh], bms[hh]) for hh in heads_here])
        s_ins = [s_sc[hh] for hh in heads_here]
        os_ = [_dot(ps[hh], vs[hh], NN) + _dot(qs[hh] * jnp.exp(bs[hh]), s_ins[hh], NT) for hh in heads_here]
        for hh in heads_here:
            shist_ref[0, hh] = s_ins[hh]
            s_sc[hh] = (jnp.exp(b_lasts[hh]) * s_ins[hh]
                        + _dot(vs[hh], ks[hh] * jnp.exp(b_lasts[hh] - bs[hh]), TN))
        for hh in heads_here:
            o = os_[hh]
            orec_ref[:, lanes[hh]] = o
            rn = lax.rsqrt(jnp.mean(o * o, axis=-1, keepdims=True) + NORM_EPS)
            hg = hg_ref[:, lanes[hh]]
            oh_ref[:, lanes[hh]] = (((o * rn) * gn_ref[...]) * (hg * _sigmoid(hg))).astype(BF16)

    col = lambda grp: pl.BlockSpec((CHUNK, per * HEAD), lambda h, c: (c, grp * ng + h))
    const = lambda shape: pl.BlockSpec(shape, lambda h, c: (0, 0))
    return pl.pallas_call(
        body, name="hgrn_fwd",
        out_shape=(jax.ShapeDtypeStruct((L, D), BF16), jax.ShapeDtypeStruct((L, D), F32),
                   jax.ShapeDtypeStruct((nc, nh, HEAD, HEAD), F32)),
        grid=(ng, nc),
        in_specs=[col(0), col(1), col(2), col(3),
                  pl.BlockSpec((2, per * HEAD), lambda h, c: (0, h)), const((1, HEAD)),
                  const((CHUNK, CHUNK)), const((LEVELS * CHUNK, CHUNK))],
        out_specs=(pl.BlockSpec((CHUNK, per * HEAD), lambda h, c: (c, h)),
                   pl.BlockSpec((CHUNK, per * HEAD), lambda h, c: (c, h)),
                   pl.BlockSpec((1, per, HEAD, HEAD), lambda h, c: (c, h, 0, 0))),
        scratch_shapes=[pltpu.VMEM((per, HEAD, HEAD), F32), pltpu.VMEM((per, CHUNK, HEAD), F32)],
        compiler_params=_params("parallel", "arbitrary"),
    )(hp, hp, hp, hp, lb_raw, g_norm, tril, sel)


def _hgrn_bwd(hp, lb_raw, g_norm, do_h, o_rec, s_hist, nh):
    L = hp.shape[0]
    D = nh * HEAD
    nc = L // CHUNK
    per = _tile(nh, (HGRN_BWD_HEADS, 1))
    ng = nh // per
    tril, sel = _hgrn_constants()
    tdec = _hgrn_decay_grad_constants()

    def body(hq_ref, hf_ref, hi_ref, hg_ref, lb_ref, gn_ref, tril_ref, sel_ref, tdec_ref, do_ref, orec_ref, shist_ref,
             dhq_ref, dhf_ref, dhi_ref, dhg_ref, dgn_ref, dlb_ref, ds_sc, b_sc):
        ci = pl.program_id(1)
        c = nc - 1 - ci

        @pl.when(ci == 0)
        def _():
            ds_sc[...] = jnp.zeros_like(ds_sc)
            dgn_ref[...] = jnp.zeros_like(dgn_ref)
            dlb_ref[...] = jnp.zeros_like(dlb_ref)

        heads_here = range(per)
        lanes = [slice(hh * HEAD, (hh + 1) * HEAD) for hh in heads_here]
        ws = [_hgrn_gates(hq_ref, hf_ref, hi_ref, lb_ref, lanes[hh], c * CHUNK) for hh in heads_here]
        qs, ks, vs = [w["q"] for w in ws], [w["k"] for w in ws], [w["v"] for w in ws]
        bs, bms = _hgrn_prefix(tril_ref, sel_ref, [w["g"] for w in ws])
        for hh in heads_here:
            b_sc[hh] = bs[hh]
        b_lasts = [b_sc[hh, CHUNK - 1:CHUNK, :] for hh in heads_here]
        levels = [_hgrn_levels(qs[hh], ks[hh], bs[hh], bms[hh]) for hh in heads_here]
        ps = _hgrn_intra(qs, ks, levels)
        s_ins = [shist_ref[0, hh] for hh in heads_here]
        ds_outs = [ds_sc[hh] for hh in heads_here]
        ebs = [jnp.exp(b) for b in bs]
        etails = [jnp.exp(b_lasts[hh] - bs[hh]) for hh in heads_here]
        decays = [jnp.exp(bl) for bl in b_lasts]

        dos = []
        for hh in heads_here:
            o = orec_ref[:, lanes[hh]]
            hg = hg_ref[:, lanes[hh]]
            sgg = _sigmoid(hg)
            rn = lax.rsqrt(jnp.mean(o * o, axis=-1, keepdims=True) + NORM_EPS)
            on = o * rn
            doh = do_ref[:, lanes[hh]]
            dy = doh * (hg * sgg)
            dhg_ref[:, lanes[hh]] = (doh * (on * gn_ref[...]) * (sgg * (1.0 + hg * (1.0 - sgg)))).astype(BF16)
            dgn_ref[hh] += jnp.broadcast_to(jnp.sum(dy * on, axis=0, keepdims=True), (8, HEAD))
            z = dy * gn_ref[...]
            dos.append(rn * z - o * ((rn * rn * rn) * jnp.mean(z * o, axis=-1, keepdims=True)))

        tt = lax.broadcasted_iota(jnp.int32, (CHUNK, CHUNK), 0)
        ss = lax.broadcasted_iota(jnp.int32, (CHUNK, CHUNK), 1)
        dps = [jnp.where(ss <= tt, _dot(dos[hh], vs[hh], NT), 0.0) for hh in heads_here]
        dvs = [_dot(ps[hh], dos[hh], TN) + _dot(ks[hh] * etails[hh], ds_outs[hh], NT) for hh in heads_here]
        dq_states = [ebs[hh] * _dot(dos[hh], s_ins[hh], NN) for hh in heads_here]
        dk_states = [etails[hh] * _dot(vs[hh], ds_outs[hh], NN) for hh in heads_here]
        dpds = [jnp.sum(jnp.where(tt == ss, dp, 0.0), axis=-1, keepdims=True) for dp in dps]
        dqs = [dpds[hh] * ks[hh] + dq_states[hh] for hh in heads_here]
        dks = [dpds[hh] * qs[hh] + dk_states[hh] for hh in heads_here]
        pair_terms = [[] for _ in heads_here]
        for lv in range(LEVELS):
            for hh in heads_here:
                eq, ek, ql, kl, same = levels[hh][lv]
                dpl = jnp.where(same, dps[hh], 0.0)
                dq_l = eq * _dot(dpl, kl, NN)
                dk_l = ek * _dot(dpl, ql, TN)
                dqs[hh] = dqs[hh] + dq_l
                dks[hh] = dks[hh] + dk_l
                pair_terms[hh].append(qs[hh] * dq_l + ks[hh] * dk_l)
        for hh in heads_here:
            pair_terms[hh] += [qs[hh] * dq_states[hh], ks[hh] * dk_states[hh]]
            ds_sc[hh] = decays[hh] * ds_outs[hh] + _dot(dos[hh], qs[hh] * ebs[hh], TN)
        stacked = [jnp.concatenate(terms, axis=0) for terms in pair_terms]
        dg_all = _dot_exact_rhs(tdec_ref[...], jnp.concatenate(stacked, axis=1) if per > 1 else stacked[0], NN)

        for hh in heads_here:
            w = ws[hh]
            f, sg, lb, sq, hq = w["f"], w["sg"], w["lb"], w["sq"], w["hq"]
            through = jnp.sum((decays[hh] * s_ins[hh]) * ds_outs[hh], axis=0, keepdims=True)
            dg = dg_all[:, lanes[hh]] + through
            df = jnp.where(w["valid"], dg / f - dks[hh], 0.0)
            dhf_ref[:, lanes[hh]] = (df * (1.0 - lb) * sg * (1.0 - sg)).astype(BF16)
            dlb_ref[hh] += jnp.broadcast_to(jnp.sum(df * (1.0 - sg), axis=0, keepdims=True), (8, HEAD))
            dhq_ref[:, lanes[hh]] = (dqs[hh] * (sq * (1.0 + hq * (1.0 - sq)))).astype(BF16)
            dhi_ref[:, lanes[hh]] = dvs[hh].astype(BF16)

    col = lambda grp: pl.BlockSpec((CHUNK, per * HEAD), lambda h, c: (nc - 1 - c, grp * ng + h))
    const = lambda shape: pl.BlockSpec(shape, lambda h, c: (0, 0))
    tile = pl.BlockSpec((CHUNK, per * HEAD), lambda h, c: (nc - 1 - c, h))
    part = pl.BlockSpec((per, 8, HEAD), lambda h, c: (h, 0, 0))
    return pl.pallas_call(
        body, name="hgrn_bwd",
        out_shape=tuple([jax.ShapeDtypeStruct((L, D), BF16)] * 4 + [jax.ShapeDtypeStruct((nh, 8, HEAD), F32)] * 2),
        grid=(ng, nc),
        in_specs=[col(0), col(1), col(2), col(3),
                  pl.BlockSpec((2, per * HEAD), lambda h, c: (0, h)), const((1, HEAD)),
                  const((CHUNK, CHUNK)), const((LEVELS * CHUNK, CHUNK)), const((CHUNK, (LEVELS + 2) * CHUNK)),
                  tile, tile, pl.BlockSpec((1, per, HEAD, HEAD), lambda h, c: (nc - 1 - c, h, 0, 0))],
        out_specs=(tile, tile, tile, tile, part, part),
        scratch_shapes=[pltpu.VMEM((per, HEAD, HEAD), F32), pltpu.VMEM((per, CHUNK, HEAD), F32)],
        compiler_params=_params("parallel", "arbitrary"),
    )(hp, hp, hp, hp, lb_raw, g_norm, tril, sel, tdec, do_h, o_rec, s_hist)


def _merge_fwd(a, bm, gates):
    L, D = a.shape
    tm = _tile(L, (128,))

    def body(a_ref, b_ref, g_ref, o_ref):
        o_ref[...] = (_sigmoid(g_ref[:, :D]) * a_ref[...] + _sigmoid(g_ref[:, D:]) * b_ref[...]).astype(BF16)

    row = lambda w: pl.BlockSpec((tm, w), lambda i: (i, 0))
    return pl.pallas_call(
        body, name="merge_fwd", out_shape=jax.ShapeDtypeStruct((L, D), BF16), grid=(L // tm,),
        in_specs=[row(D), row(D), row(2 * D)], out_specs=row(D), compiler_params=_params("parallel"),
    )(a, bm, gates)


def _merge_bwd(dm, a, bm, gates):
    L, D = a.shape
    tm = _tile(L, (128,))

    def body(dm_ref, a_ref, b_ref, g_ref, da_ref, db_ref, dg_ref):
        d = dm_ref[...]
        sa, sb = _sigmoid(g_ref[:, :D]), _sigmoid(g_ref[:, D:])
        da_ref[...] = (d * sa).astype(BF16)
        db_ref[...] = (d * sb).astype(BF16)
        dg_ref[:, :D] = (d * a_ref[...] * sa * (1.0 - sa)).astype(BF16)
        dg_ref[:, D:] = (d * b_ref[...] * sb * (1.0 - sb)).astype(BF16)

    row = lambda w: pl.BlockSpec((tm, w), lambda i: (i, 0))
    return pl.pallas_call(
        body, name="merge_bwd",
        out_shape=(jax.ShapeDtypeStruct((L, D), BF16), jax.ShapeDtypeStruct((L, D), BF16),
                   jax.ShapeDtypeStruct((L, 2 * D), BF16)),
        grid=(L // tm,),
        in_specs=[row(D), row(D), row(D), row(2 * D)], out_specs=(row(D), row(D), row(2 * D)),
        compiler_params=_params("parallel"),
    )(dm, a, bm, gates)


def _conv_taps(g_ref, halo_ref, i, tm):
    rows = i * tm + lax.broadcasted_iota(jnp.int32, (tm, 1), 0)
    g0 = jnp.where(rows >= PAD_LEN, g_ref[...], 0.0)
    hrow = i * tm - 8 + lax.broadcasted_iota(jnp.int32, (8, 1), 0)
    halo = jnp.where(hrow >= PAD_LEN, halo_ref[...], 0.0)
    r = lax.broadcasted_iota(jnp.int32, (tm, 1), 0)
    h7 = jnp.sum(jnp.where(lax.broadcasted_iota(jnp.int32, (8, 1), 0) == 7, halo, 0.0), axis=0, keepdims=True)
    h6 = jnp.sum(jnp.where(lax.broadcasted_iota(jnp.int32, (8, 1), 0) == 6, halo, 0.0), axis=0, keepdims=True)
    g1 = jnp.where(r == 0, h7, pltpu.roll(g0, 1, 0))
    g2 = jnp.where(r == 0, h6, jnp.where(r == 1, h7, pltpu.roll(g0, 2, 0)))
    return g0, g1, g2


def _conv_fwd(gu, cw, cb):
    L, F2 = gu.shape
    F = F2 // 2
    tm = _tile(L, ROW_TILES)
    tn = _tile(F, (512, 256, 128))
    nj = F // tn

    def body(g_ref, halo_ref, u_ref, cw_ref, cb_ref, o_ref):
        g0, g1, g2 = _conv_taps(g_ref, halo_ref, pl.program_id(0), tm)
        conv = cw_ref[0:1, :] * g2 + cw_ref[1:2, :] * g1 + cw_ref[2:3, :] * g0 + cb_ref[...]
        o_ref[...] = (conv * _sigmoid(conv) * u_ref[...]).astype(BF16)

    return pl.pallas_call(
        body, name="conv_fwd", out_shape=jax.ShapeDtypeStruct((L, F), BF16), grid=(L // tm, nj),
        in_specs=[pl.BlockSpec((tm, tn), lambda i, j: (i, j)),
                  pl.BlockSpec((8, tn), lambda i, j: (jnp.maximum(i * (tm // 8) - 1, 0), j)),
                  pl.BlockSpec((tm, tn), lambda i, j: (i, j + nj)),
                  pl.BlockSpec((3, tn), lambda i, j: (0, j)),
                  pl.BlockSpec((1, tn), lambda i, j: (0, j))],
        out_specs=pl.BlockSpec((tm, tn), lambda i, j: (i, j)),
        compiler_params=_params("parallel", "parallel"),
    )(gu, gu, gu, cw, cb)


def _conv_bwd_a(da, gu, cw, cb):
    L, F2 = gu.shape
    F = F2 // 2
    tm = _tile(L, ROW_TILES)
    tn = _tile(F, (512, 256, 128))
    nj = F // tn

    def body(da_ref, g_ref, halo_ref, u_ref, cw_ref, cb_ref, dc_ref, du_ref, dcb_ref, dcw_ref):
        i = pl.program_id(1)

        @pl.when(i == 0)
        def _():
            dcb_ref[...] = jnp.zeros_like(dcb_ref)
            dcw_ref[...] = jnp.zeros_like(dcw_ref)

        g0, g1, g2 = _conv_taps(g_ref, halo_ref, i, tm)
        conv = cw_ref[0:1, :] * g2 + cw_ref[1:2, :] * g1 + cw_ref[2:3, :] * g0 + cb_ref[...]
        sc = _sigmoid(conv)
        dav = da_ref[...]
        du_ref[...] = (dav * (conv * sc)).astype(BF16)
        dconv = dav * u_ref[...] * (sc * (1.0 + conv * (1.0 - sc)))
        dc_ref[...] = dconv
        dcb_ref[...] += jnp.sum(dconv, axis=0, keepdims=True)
        dcw_ref[0:1, :] += jnp.sum(dconv * g2, axis=0, keepdims=True)
        dcw_ref[1:2, :] += jnp.sum(dconv * g1, axis=0, keepdims=True)
        dcw_ref[2:3, :] += jnp.sum(dconv * g0, axis=0, keepdims=True)

    return pl.pallas_call(
        body, name="conv_bwd_a",
        out_shape=(jax.ShapeDtypeStruct((L, F), F32), jax.ShapeDtypeStruct((L, F), BF16),
                   jax.ShapeDtypeStruct((1, F), F32), jax.ShapeDtypeStruct((8, F), F32)),
        grid=(nj, L // tm),
        in_specs=[pl.BlockSpec((tm, tn), lambda j, i: (i, j)),
                  pl.BlockSpec((tm, tn), lambda j, i: (i, j)),
                  pl.BlockSpec((8, tn), lambda j, i: (jnp.maximum(i * (tm // 8) - 1, 0), j)),
                  pl.BlockSpec((tm, tn), lambda j, i: (i, j + nj)),
                  pl.BlockSpec((3, tn), lambda j, i: (0, j)),
                  pl.BlockSpec((1, tn), lambda j, i: (0, j))],
        out_specs=(pl.BlockSpec((tm, tn), lambda j, i: (i, j)), pl.BlockSpec((tm, tn), lambda j, i: (i, j)),
                   pl.BlockSpec((1, tn), lambda j, i: (0, j)), pl.BlockSpec((8, tn), lambda j, i: (0, j))),
        compiler_params=_params("parallel", "arbitrary"),
    )(da, gu, gu, gu, cw, cb)


def _conv_bwd_b(dconv, cw):
    L, F = dconv.shape
    tm = _tile(L, ROW_TILES)
    tn = _tile(F, (512, 256, 128))
    nblk8 = L // 8
    ni = L // tm

    def body(dc_ref, nxt_ref, cw_ref, o_ref):
        i = pl.program_id(0)
        dc = dc_ref[...]
        nxt = jnp.where(i < ni - 1, nxt_ref[...], 0.0)
        sub = lax.broadcasted_iota(jnp.int32, (8, 1), 0)
        n0 = jnp.sum(jnp.where(sub == 0, nxt, 0.0), axis=0, keepdims=True)
        n1 = jnp.sum(jnp.where(sub == 1, nxt, 0.0), axis=0, keepdims=True)
        r = lax.broadcasted_iota(jnp.int32, (tm, 1), 0)
        d1 = jnp.where(r == tm - 1, n0, pltpu.roll(dc, tm - 1, 0))
        d2 = jnp.where(r == tm - 2, n0, jnp.where(r == tm - 1, n1, pltpu.roll(dc, tm - 2, 0)))
        dg = cw_ref[2:3, :] * dc + cw_ref[1:2, :] * d1 + cw_ref[0:1, :] * d2
        rows = i * tm + r
        o_ref[...] = jnp.where(rows >= PAD_LEN, dg, 0.0).astype(BF16)

    return pl.pallas_call(
        body, name="conv_bwd_b", out_shape=jax.ShapeDtypeStruct((L, F), BF16), grid=(ni, F // tn),
        in_specs=[pl.BlockSpec((tm, tn), lambda i, j: (i, j)),
                  pl.BlockSpec((8, tn), lambda i, j: (jnp.minimum((i + 1) * (tm // 8), nblk8 - 1), j)),
                  pl.BlockSpec((3, tn), lambda i, j: (0, j))],
        out_specs=pl.BlockSpec((tm, tn), lambda i, j: (i, j)),
        compiler_params=_params("parallel", "parallel"),
    )(dconv, dconv, cw)


ANY = pl.BlockSpec(memory_space=pl.ANY)


def _coords():
    return lax.axis_index("x"), lax.axis_index("y"), lax.axis_index("c")


def _flip(v, bit):
    return 1 - v if bit else v


CHIPS = [(1, 0), (0, 1), (1, 1)]
PEERS = [(dx, dy, dc) for dx in (0, 1) for dy in (0, 1) for dc in (0, 1)][1:]


class _Rider:
    def __init__(self, operands, out_shapes, sem_shapes, start, finish):
        self.operands, self.out_shapes, self.sem_shapes = list(operands), list(out_shapes), list(sem_shapes)
        self.start, self.finish = start, finish


def _run_rider(rider, name):
    n_in, n_out = len(rider.operands), len(rider.out_shapes)

    def body(*refs):
        ins, outs, sems = refs[:n_in], refs[n_in:n_in + n_out], refs[n_in + n_out:]
        rider.start(ins, outs, sems)
        rider.finish(ins, outs, sems)

    return pl.pallas_call(
        body, name=name, out_shape=tuple(rider.out_shapes),
        in_specs=[ANY] * n_in, out_specs=tuple([ANY] * n_out), scratch_shapes=rider.sem_shapes,
        compiler_params=pltpu.CompilerParams(has_side_effects=True),
    )(*rider.operands)


def _gather_rider(big, small):
    nbig, n = len(big), len(big) + len(small)
    arrays = list(big) + list(small)

    def plan(ins, outs, sems):
        ici_send, ici_recv, d2d_send, d2d_recv, local_sems = sems
        x, y, c = _coords()
        mine = 2 * x + y

        def half(w, h):
            r2 = arrays[w].shape[0] // 2
            return pl.ds(h * r2, r2)

        def ici(w, j, landing):
            px, py = _flip(x, CHIPS[j][0]), _flip(y, CHIPS[j][1])
            slot = 2 * px + py if landing else mine
            if w < nbig:
                src, dst = ins[w].at[half(w, c)], outs[w].at[slot, half(w, c)]
            else:
                src, dst = ins[w], outs[w].at[slot]
            return pltpu.make_async_remote_copy(
                src_ref=src, dst_ref=dst, send_sem=ici_send.at[w * 3 + j], recv_sem=ici_recv.at[w * 3 + j],
                device_id=(px, py, c), device_id_type=MESH)

        def d2d(w, j, landing):
            px, py = _flip(x, CHIPS[j][0]), _flip(y, CHIPS[j][1])
            mine_rows = outs[w].at[2 * px + py, half(w, c)]
            dst = outs[w].at[2 * px + py, half(w, 1 - c)] if landing else mine_rows
            return pltpu.make_async_remote_copy(
                src_ref=mine_rows, dst_ref=dst, send_sem=d2d_send.at[w * 3 + j], recv_sem=d2d_recv.at[w * 3 + j],
                device_id=(x, y, 1 - c), device_id_type=MESH)

        local = [pltpu.make_async_copy(ins[w], outs[w].at[mine], local_sems.at[w]) for w in range(n)]
        return ici, d2d, local

    def start(ins, outs, sems):
        ici, _, local = plan(ins, outs, sems)
        for cp in local:
            cp.start()
        for w in range(n):
            for j in range(3):
                ici(w, j, False).start()

    def finish(ins, outs, sems):
        ici, d2d, local = plan(ins, outs, sems)
        for w in range(n):
            for j in range(3):
                ici(w, j, True).wait_recv()
                if w < nbig:
                    d2d(w, j, False).start()
        for w in range(nbig):
            for j in range(3):
                d2d(w, j, True).wait_recv()
        for w in range(n):
            for j in range(3):
                ici(w, j, False).wait_send()
                if w < nbig:
                    d2d(w, j, False).wait_send()
        for cp in local:
            cp.wait()

    return _Rider(
        arrays, [jax.ShapeDtypeStruct((4,) + s.shape, s.dtype) for s in arrays],
        [pltpu.SemaphoreType.DMA((3 * n,)), pltpu.SemaphoreType.DMA((3 * n,)),
         pltpu.SemaphoreType.DMA((max(3 * nbig, 1),)), pltpu.SemaphoreType.DMA((max(3 * nbig, 1),)),
         pltpu.SemaphoreType.DMA((n,))],
        start, finish)


def _to_sibling(arrays, name):
    n = len(arrays)

    def body(*refs):
        ins, outs = refs[:n], refs[n:2 * n]
        send_sems, recv_sems = refs[2 * n:]
        x, y, c = _coords()

        def copy(w):
            return pltpu.make_async_remote_copy(
                src_ref=ins[w], dst_ref=outs[w], send_sem=send_sems.at[w], recv_sem=recv_sems.at[w],
                device_id=(x, y, 1 - c), device_id_type=MESH)

        for w in range(n):
            copy(w).start()
        for w in range(n):
            copy(w).wait_recv()
            copy(w).wait_send()

    return pl.pallas_call(
        body, name=name,
        out_shape=tuple(jax.ShapeDtypeStruct(a.shape, a.dtype) for a in arrays),
        in_specs=[ANY] * n, out_specs=tuple([ANY] * n),
        scratch_shapes=[pltpu.SemaphoreType.DMA((n,)), pltpu.SemaphoreType.DMA((n,))],
        compiler_params=pltpu.CompilerParams(has_side_effects=True),
    )(*arrays)


def _pair_sum(a, b, name):
    _, r, c = a.shape
    tr = _tile(r, (64, 32, 16))

    def body(a_ref, b_ref, o_ref):
        o_ref[...] = (a_ref[...].astype(F32) + b_ref[...].astype(F32)).astype(BF16)

    blk = pl.BlockSpec((4, tr, c), lambda i: (0, i, 0))
    return pl.pallas_call(
        body, name=name, out_shape=jax.ShapeDtypeStruct(a.shape, BF16), grid=(r // tr,),
        in_specs=[blk, blk], out_specs=blk, compiler_params=_params("parallel"),
    )(a, b)


def _scatter_rider(parts):
    n = len(parts)

    def plan(ins, outs, sems):
        send_sems, recv_sems, local_sems = sems
        x, y, c = _coords()
        mine = 2 * x + y

        def ici(w, j, landing):
            px, py = _flip(x, CHIPS[j][0]), _flip(y, CHIPS[j][1])
            return pltpu.make_async_remote_copy(
                src_ref=ins[w].at[2 * px + py], dst_ref=outs[w].at[2 * px + py if landing else mine],
                send_sem=send_sems.at[w * 3 + j], recv_sem=recv_sems.at[w * 3 + j],
                device_id=(px, py, c), device_id_type=MESH)

        local = [pltpu.make_async_copy(ins[w].at[mine], outs[w].at[mine], local_sems.at[w]) for w in range(n)]
        return ici, local

    def start(ins, outs, sems):
        ici, local = plan(ins, outs, sems)
        for cp in local:
            cp.start()
        for w in range(n):
            for j in range(3):
                ici(w, j, False).start()

    def finish(ins, outs, sems):
        ici, local = plan(ins, outs, sems)
        for w in range(n):
            for j in range(3):
                ici(w, j, True).wait_recv()
                ici(w, j, False).wait_send()
        for cp in local:
            cp.wait()

    return _Rider(
        parts, [jax.ShapeDtypeStruct(p.shape, p.dtype) for p in parts],
        [pltpu.SemaphoreType.DMA((3 * n,)), pltpu.SemaphoreType.DMA((3 * n,)), pltpu.SemaphoreType.DMA((n,))],
        start, finish)


def _sum4(recv, name):
    _, r, c = recv.shape
    tr = _tile(r, (64, 32, 16))

    def body(in_ref, o_ref):
        o_ref[...] = ((in_ref[0].astype(F32) + in_ref[1].astype(F32)) + in_ref[2].astype(F32)) + in_ref[3].astype(F32)

    return pl.pallas_call(
        body, name=name, out_shape=jax.ShapeDtypeStruct((r, c), F32), grid=(r // tr,),
        in_specs=[pl.BlockSpec((4, tr, c), lambda i: (0, i, 0))],
        out_specs=pl.BlockSpec((tr, c), lambda i: (i, 0)),
        compiler_params=_params("parallel"),
    )(recv)


def _allreduce_small(packed):
    R = packed.shape[0]

    def body(in_ref, o_ref, buf, send_sems, recv_sems):
        x, y, c = _coords()
        me = 4 * x + 2 * y + c
        buf[me] = in_ref[...]
        for j, (dx, dy, dc) in enumerate(PEERS):
            px, py, pc = _flip(x, dx), _flip(y, dy), _flip(c, dc)
            pltpu.make_async_remote_copy(
                src_ref=in_ref, dst_ref=buf.at[me], send_sem=send_sems.at[j], recv_sem=recv_sems.at[j],
                device_id=(px, py, pc), device_id_type=MESH).start()
        for j, (dx, dy, dc) in enumerate(PEERS):
            px, py, pc = _flip(x, dx), _flip(y, dy), _flip(c, dc)
            rc = pltpu.make_async_remote_copy(
                src_ref=in_ref, dst_ref=buf.at[4 * px + 2 * py + pc], send_sem=send_sems.at[j],
                recv_sem=recv_sems.at[j], device_id=(px, py, pc), device_id_type=MESH)
            rc.wait_recv()
            rc.wait_send()
        acc = buf[0]
        for s in range(1, 8):
            acc = acc + buf[s]
        o_ref[...] = acc

    return pl.pallas_call(
        body, name="allreduce_small", out_shape=jax.ShapeDtypeStruct((R, 128), F32),
        in_specs=[pl.BlockSpec(memory_space=pltpu.VMEM)], out_specs=pl.BlockSpec(memory_space=pltpu.VMEM),
        scratch_shapes=[pltpu.VMEM((8, R, 128), F32), pltpu.SemaphoreType.DMA((7,)), pltpu.SemaphoreType.DMA((7,))],
        compiler_params=pltpu.CompilerParams(has_side_effects=True, vmem_limit_bytes=VMEM_LIMIT_BYTES),
    )(packed)


def _adamw_math(w, g, m, v):
    m = ADAM_B1 * m + (1.0 - ADAM_B1) * g
    v = ADAM_B2 * v + (1.0 - ADAM_B2) * (g * g)
    m_hat = m / (1.0 - ADAM_B1 ** ADAM_STEP)
    v_hat = v / (1.0 - ADAM_B2 ** ADAM_STEP)
    delta = -ADAM_LR * (m_hat / (jnp.sqrt(v_hat) + ADAM_EPS) + ADAM_WD * w)
    return delta, m, v


def _adamw_big(mine, other, w, m, v, name):
    R, C = w.shape
    tr = _tile(R // 2, (128, 64, 32, 16, 8))
    nb = (R // 2) // tr

    def body(mine_ref, other_ref, w_ref, m_ref, v_ref, g_ref, d_ref, mo_ref, vo_ref):
        is_mine = (pl.program_id(0) // nb) == lax.axis_index("c")
        g = jnp.where(is_mine, mine_ref[...], other_ref[...])
        d, mn, vn = _adamw_math(w_ref[...], g, m_ref[...], v_ref[...])
        g_ref[...] = g
        d_ref[...] = d
        mo_ref[...] = mn
        vo_ref[...] = vn

    blk = pl.BlockSpec((tr, C), lambda i: (i, 0))
    half = pl.BlockSpec((tr, C), lambda i: (i % nb, 0))
    sds = jax.ShapeDtypeStruct((R, C), F32)
    return pl.pallas_call(
        body, name=name, out_shape=(sds, sds, sds, sds), grid=(2 * nb,),
        in_specs=[half, half, blk, blk, blk], out_specs=(blk, blk, blk, blk), compiler_params=_params("parallel"),
    )(mine, other, w, m, v)


def _adamw_small(items, lb_raw, dlb):
    n = len(items)
    lb_w, lb_m, lb_v = lb_raw

    def body(*refs):
        ins = refs[:4 * n]
        dlb_ref, lw_ref, lm_ref, lv_ref = refs[4 * n:4 * n + 4]
        outs = refs[4 * n + 4:]
        for t in range(n):
            g_ref, w_ref, m_ref, v_ref = ins[4 * t:4 * t + 4]
            d, mn, vn = _adamw_math(w_ref[...], g_ref[...], m_ref[...], v_ref[...])
            outs[3 * t][...] = d
            outs[3 * t + 1][...] = mn
            outs[3 * t + 2][...] = vn
        p0 = 1.0 / (1.0 + jnp.exp(lw_ref[1:2, :] - lw_ref[0:1, :]))
        g0 = dlb_ref[...] * p0 * (1.0 - p0)
        base = 3 * n
        outs[base][0:1, :] = g0
        outs[base][1:2, :] = -g0
        d, mn, vn = _adamw_math(lw_ref[...], outs[base][...], lm_ref[...], lv_ref[...])
        outs[base + 1][...] = d
        outs[base + 2][...] = mn
        outs[base + 3][...] = vn

    operands = [a for it in items for a in it] + [dlb, lb_w, lb_m, lb_v]
    out_shape = []
    for (g, w, m, v) in items:
        out_shape += [jax.ShapeDtypeStruct(w.shape, F32)] * 3
    out_shape += [jax.ShapeDtypeStruct(lb_w.shape, F32)] * 4
    vm = pl.BlockSpec(memory_space=pltpu.VMEM)
    res = pl.pallas_call(
        body, name="adamw_small", out_shape=tuple(out_shape),
        in_specs=[vm] * len(operands), out_specs=tuple([vm] * len(out_shape)),
        compiler_params=pltpu.CompilerParams(vmem_limit_bytes=VMEM_LIMIT_BYTES),
    )(*operands)
    deltas = [res[3 * t] for t in range(n)] + [res[3 * n + 1]]
    new_m = [res[3 * t + 1] for t in range(n)] + [res[3 * n + 2]]
    new_v = [res[3 * t + 2] for t in range(n)] + [res[3 * n + 3]]
    return res[3 * n], deltas, new_m, new_v


def _shard_row_half(g, by_cols, h):
    R, C = g.shape
    if by_cols:
        part = lax.dynamic_index_in_dim(g.reshape(2, R // 2, 4, C // 4), h, axis=0, keepdims=False)
        return part.transpose(1, 0, 2).astype(BF16)
    return lax.dynamic_index_in_dim(g.reshape(4, 2, R // 8, C), h, axis=1, keepdims=False).astype(BF16)


def kernel(x, positions, meta_tokens, w_in, w_q_up, w_kv_up, w_branch_mla, w_branch_hgrn, w_out, w_ffn_in, w_ffn_out, conv_w, conv_b, g_mix_norm, g_q_norm, g_kv_norm, g_hgrn_norm, g_ffn_norm, g_final_norm, lb_raw, loss_target, m_meta_tokens, m_w_in, m_w_q_up, m_w_kv_up, m_w_branch_mla, m_w_branch_hgrn, m_w_out, m_w_ffn_in, m_w_ffn_out, m_conv_w, m_conv_b, m_g_mix_norm, m_g_q_norm, m_g_kv_norm, m_g_hgrn_norm, m_g_ffn_norm, m_g_final_norm, m_lb_raw, v_meta_tokens, v_w_in, v_w_q_up, v_w_kv_up, v_w_branch_mla, v_w_branch_hgrn, v_w_out, v_w_ffn_in, v_w_ffn_out, v_conv_w, v_conv_b, v_g_mix_norm, v_g_q_norm, v_g_kv_norm, v_g_hgrn_norm, v_g_ffn_norm, v_g_final_norm, v_lb_raw):
    S, D = x.shape[1], x.shape[2]
    L = S + PREFIX
    QL, KVL = g_q_norm.shape[1], g_kv_norm.shape[1]
    F = conv_b.shape[1]
    heads = (4 * w_kv_up.shape[2]) // QPAD
    nh = D // HEAD
    assert lb_raw.shape[0] == 2 and g_hgrn_norm.shape[1] == HEAD and L % CHUNK == 0
    ix, iy, ic = _coords()
    chip = 2 * ix + iy

    big = [w_in, w_q_up, w_kv_up, w_branch_mla, w_branch_hgrn, w_out, w_ffn_in, w_ffn_out]
    col_sharded = [True, True, True, False, False, False, True, False]
    shards = [w[0].astype(BF16) for w in big]
    early = _run_rider(_gather_rider(shards[:3], [meta_tokens, conv_w[0]]), "gather_early")
    late_rider = _gather_rider(shards[3:], [])

    def full(gw, by_cols):
        _, r, c = gw.shape
        return gw.transpose(1, 0, 2).reshape(r, 4 * c) if by_cols else gw.reshape(4 * r, c)

    W_in, W_q, W_kv = [full(gw, True) for gw in early[:3]]
    meta_full = full(early[3], True)
    cw_full = full(early[4], True)
    c0 = QL + KVL
    W_lat = W_in[:, :c0]
    W_kr = jnp.pad(W_in[:, c0:c0 + ROPE], ((0, 0), (0, 128 - ROPE)))
    W_H = W_in[:, c0 + ROPE:c0 + ROPE + 4 * D]
    W_G = W_in[:, c0 + ROPE + 4 * D:]
    W_qp = jnp.pad(W_q.reshape(QL, heads, QK_HEAD), ((0, 0), (0, 0), (0, QPAD - QK_HEAD))).reshape(QL, heads * QPAD)

    h0 = jnp.concatenate([jnp.zeros((PAD_LEN, D), F32), meta_full, x[0]], axis=0)
    pos = jnp.concatenate([jnp.zeros((PAD_LEN,), jnp.int32), jnp.arange(N_META, dtype=jnp.int32),
                           positions[0].astype(jnp.int32) + N_META])
    inv = 1.0 / (ROPE_THETA ** (jnp.arange(0, ROPE, 2, dtype=F32) / ROPE))
    ang = pos.astype(F32)[:, None] * inv
    zero = jnp.zeros((L, 128 - ROPE), F32)
    cos = jnp.concatenate([jnp.cos(ang), jnp.cos(ang), zero], axis=1)
    sin = jnp.concatenate([jnp.sin(ang), jnp.sin(ang), zero], axis=1)

    u1 = _rmsnorm_fwd(h0, g_mix_norm, "norm_mix")
    lat = _mm(u1, W_lat, "nn", F32, "proj_lat")
    hp = _mm(u1, W_H, "nn", F32, "proj_hgrn")
    gates = _mm(u1, W_G, "nn", F32, "proj_gates")
    kr = _mm(u1, W_kr, "nn", F32, "proj_krope")
    qn = _rmsnorm_fwd(lat, g_q_norm, "norm_q", col0=0, width=QL)
    kvn = _rmsnorm_fwd(lat, g_kv_norm, "norm_kv", col0=QL, width=KVL)
    qp = _mm(qn, W_qp, "nn", F32, "q_up")
    kv = _mm(kvn, W_kv, "nn", F32, "kv_up")
    qc, kc, vv = _rope_fwd(qp, kv, kr, cos, sin, heads)
    o_mla, lse, late = _attn_fwd(qc, kc, vv, heads, late_rider)
    W_a, W_b, W_o, W_fi, W_fo = [full(gw, bc) for gw, bc in zip(late, col_sharded[3:])]
    o_hgrn, o_rec, s_hist = _hgrn_fwd(hp, lb_raw, g_hgrn_norm, nh)
    br_a = _mm(o_mla, W_a, "nn", F32, "branch_mla")
    br_b = _mm(o_hgrn, W_b, "nn", F32, "branch_hgrn")
    merged = _merge_fwd(br_a, br_b, gates)
    h1 = _mm(merged, W_o, "nn", F32, "out_proj", res=h0)
    u2 = _rmsnorm_fwd(h1, g_ffn_norm, "norm_ffn")
    gu = _mm(u2, W_fi, "nn", F32, "ffn_in")
    act = _conv_fwd(gu, cw_full, conv_b)
    h2 = _mm(act, W_fo, "nn", F32, "ffn_out", res=h1)
    dh2, loss_p, dg_final = _final_loss_bwd(h2, loss_target[0], g_final_norm.reshape(1, D))

    dact = _mm(dh2, W_fo, "nt", F32, "d_act")
    dW_fo = _mm(act, dh2, "tn", F32, "dw_ffn_out")
    dconv, dup, dcb, dcw = _conv_bwd_a(dact, gu, cw_full, conv_b)
    dgate = _conv_bwd_b(dconv, cw_full)
    dgu = jnp.concatenate([dgate, dup], axis=1)
    du2 = _mm(dgu, W_fi, "nt", F32, "d_u2")
    dW_fi = _mm(u2, dgu, "tn", F32, "dw_ffn_in")
    dh1, dg_ffn = _rmsnorm_bwd(du2, h1, g_ffn_norm, "norm_ffn_bwd", F32, res=dh2)
    dmerged = _mm(dh1, W_o, "nt", F32, "d_merged")
    dW_o = _mm(merged, dh1, "tn", F32, "dw_out")
    d_a, d_b, d_gates = _merge_bwd(dmerged, br_a, br_b, gates)
    do_mla = _mm(d_a, W_a, "nt", BF16, "d_o_mla")
    dW_a = _mm(o_mla, d_a, "tn", F32, "dw_branch_mla")
    do_hgrn = _mm(d_b, W_b, "nt", F32, "d_o_hgrn")
    dW_b = _mm(o_hgrn, d_b, "tn", F32, "dw_branch_hgrn")
    names = ["w_in", "w_q_up", "w_kv_up", "w_branch_mla", "w_branch_hgrn", "w_out", "w_ffn_in", "w_ffn_out"]

    def chip_partials(grads, by_cols, nms, tag):
        keep = [_shard_row_half(g, bc, ic) for g, bc in zip(grads, by_cols)]
        give = [_shard_row_half(g, bc, 1 - ic) for g, bc in zip(grads, by_cols)]
        taken = _to_sibling(give, "pair_exchange_" + tag)
        return [_pair_sum(a, b, "pair_sum_" + nm) for a, b, nm in zip(keep, taken, nms)]

    late_parts = chip_partials([dW_a, dW_b, dW_o, dW_fi, dW_fo], col_sharded[3:], names[3:], "late")
    dhq, dhf, dhi, dhg, dgn_p, dlb_p = _hgrn_bwd(hp, lb_raw, g_hgrn_norm, do_hgrn, o_rec, s_hist, nh)
    dqc, dkc, dvv, late_recv = _attn_bwd(qc, kc, vv, do_mla, lse, _attn_delta(o_mla, do_mla, heads), heads,
                                         _scatter_rider(late_parts))
    dqp, dkv, dkr = _rope_bwd(dqc, dkc, dvv, cos, sin, heads)
    dqn = _mm(dqp, W_qp, "nt", F32, "d_qn")
    dW_qp = _mm(qn, dqp, "tn", F32, "dw_q_up")
    dkvn = _mm(dkv, W_kv, "nt", F32, "d_kvn")
    dW_kv = _mm(kvn, dkv, "tn", F32, "dw_kv_up")
    dq_lat, dg_q = _rmsnorm_bwd(dqn, lat, g_q_norm, "norm_q_bwd", BF16, col0=0)
    dkv_lat, dg_kv = _rmsnorm_bwd(dkvn, lat, g_kv_norm, "norm_kv_bwd", BF16, col0=QL)
    dlat = jnp.concatenate([dq_lat, dkv_lat], axis=1)
    dhp = jnp.concatenate([dhq, dhf, dhi, dhg], axis=1)
    du1 = _mm(dlat, W_lat, "nt", F32, "d_u1_lat")
    du1 = _mm(dhp, W_H, "nt", F32, "d_u1_hgrn", res=du1)
    du1 = _mm(d_gates, W_G, "nt", F32, "d_u1_gates", res=du1)
    du1 = _mm(dkr, W_kr, "nt", F32, "d_u1_krope", res=du1)
    dW_lat = _mm(u1, dlat, "tn", F32, "dw_in_lat")
    dW_H = _mm(u1, dhp, "tn", F32, "dw_in_hgrn")
    dW_G = _mm(u1, d_gates, "tn", F32, "dw_in_gates")
    dW_kr = _mm(u1, dkr, "tn", F32, "dw_in_krope")
    dh0, dg_mix = _rmsnorm_bwd(du1, h0, g_mix_norm, "norm_mix_bwd", F32, res=dh1)
    grad_x = dh0[PREFIX:][None]

    dW_in = jnp.concatenate([dW_lat, dW_kr[:, :ROPE], dW_H, dW_G], axis=1)
    dW_q = dW_qp.reshape(QL, heads, QPAD)[:, :, :QK_HEAD].reshape(QL, heads * QK_HEAD)
    early_parts = chip_partials([dW_in, dW_q, dW_kv], col_sharded[:3], names[:3], "early")
    received = list(_run_rider(_scatter_rider(early_parts), "scatter_early")) + list(late_recv)
    halves = [_sum4(r, "sum_" + nm) for r, nm in zip(received, names)]
    others = _to_sibling(halves, "swap_halves")
    big_m = [m_w_in, m_w_q_up, m_w_kv_up, m_w_branch_mla, m_w_branch_hgrn, m_w_out, m_w_ffn_in, m_w_ffn_out]
    big_v = [v_w_in, v_w_q_up, v_w_kv_up, v_w_branch_mla, v_w_branch_hgrn, v_w_out, v_w_ffn_in, v_w_ffn_out]
    big_out = {}
    for nm, mine, other, w, m, v in zip(names, halves, others, big, big_m, big_v):
        g, d, mn, vn = _adamw_big(mine, other, w[0], m[0], v[0], "adamw_" + nm)
        big_out[nm] = (g[None], d[None], mn[None], vn[None])

    pieces = [loss_p[:, :1], dg_mix, dg_q, dg_kv, jnp.sum(dgn_p[:, 0, :], axis=0, keepdims=True), dg_ffn, dg_final,
              dlb_p[:, 0, :].reshape(1, D), dcb, dcw[0:3].reshape(1, 3 * F), dh0[PAD_LEN:PREFIX].reshape(1, N_META * D)]
    sizes = [p.shape[1] for p in pieces]
    flat = jnp.concatenate(pieces, axis=1)[0]
    rows = -(-flat.shape[0] // 1024) * 8
    packed = jnp.pad(flat, (0, rows * 128 - flat.shape[0])).reshape(rows, 128)
    total = _allreduce_small(packed).reshape(-1)
    offs = [0]
    for s in sizes:
        offs.append(offs[-1] + s)
    loss, g_mix, g_q, g_kv, g_hg, g_ffn, g_fin, dlb, g_cb, g_cw, g_meta = [
        total[offs[t]:offs[t + 1]].reshape(1, sizes[t]) for t in range(len(sizes))]
    g_cw = lax.dynamic_slice_in_dim(g_cw.reshape(3, F), chip * (F // 4), F // 4, axis=1)
    g_meta = lax.dynamic_slice_in_dim(g_meta.reshape(N_META, D), chip * (D // 4), D // 4, axis=1)
    items = [(g_meta, meta_tokens, m_meta_tokens, v_meta_tokens),
             (g_cw, conv_w[0], m_conv_w[0], v_conv_w[0]),
             (g_cb, conv_b, m_conv_b, v_conv_b),
             (g_mix, g_mix_norm, m_g_mix_norm, v_g_mix_norm),
             (g_q, g_q_norm, m_g_q_norm, v_g_q_norm),
             (g_kv, g_kv_norm, m_g_kv_norm, v_g_kv_norm),
             (g_hg, g_hgrn_norm, m_g_hgrn_norm, v_g_hgrn_norm),
             (g_ffn, g_ffn_norm, m_g_ffn_norm, v_g_ffn_norm),
             (g_fin, g_final_norm.reshape(1, D), m_g_final_norm.reshape(1, D), v_g_final_norm.reshape(1, D))]
    g_lb, s_delta, s_m, s_v = _adamw_small(items, (lb_raw, m_lb_raw, v_lb_raw), dlb)
    s_grads = [it[0] for it in items] + [g_lb]

    def shape_small(vals):
        meta, cw, cb, mix, q, kvg, hg, ffn, fin, lb = vals
        return [meta, cw[None], cb, mix, q, kvg, hg, ffn, fin.reshape(D), lb]

    s_grads, s_delta, s_m, s_v = [shape_small(v) for v in (s_grads, s_delta, s_m, s_v)]

    def ordered(kind, small):
        bigs = [big_out[nm][kind] for nm in names]
        return [small[0]] + bigs + small[1:]

    return (loss.reshape(()), grad_x, *ordered(0, s_grads), *ordered(1, s_delta), *ordered(2, s_m), *ordered(3, s_v))
```

```python
import math

import jax
import jax.numpy as jnp
from jax import lax
from jax.experimental import pallas as pl
from jax.experimental.pallas import tpu as pltpu

F32 = jnp.float32
BF16 = jnp.bfloat16
MESH = pl.DeviceIdType.MESH

NORM_EPS = 1e-6
N_META = 16
PREFIX = 128
PAD_LEN = PREFIX - N_META
HEAD = 128
ROPE = 64
QK_HEAD = HEAD + ROPE
QPAD = 2 * HEAD
SOFTMAX_SCALE = QK_HEAD ** -0.5
ROPE_THETA = 10000.0
CHUNK = 128
LEVELS = 7
HGRN_FWD_HEADS = 4
HGRN_BWD_HEADS = 4
ATTN_FWD_HEADS = 2
ATTN_BWD_HEADS = 2
NEG = -1e30

ADAM_LR = 0.001
ADAM_B1 = 0.9
ADAM_B2 = 0.999
ADAM_EPS = 1e-08
ADAM_WD = 0.01
ADAM_STEP = 10

VMEM_LIMIT_BYTES = 48 * 1024 * 1024


def _params(*sem):
    return pltpu.CompilerParams(dimension_semantics=sem, vmem_limit_bytes=VMEM_LIMIT_BYTES)


def _tile(n, prefs):
    for p in prefs:
        if n % p == 0:
            return p
    return n


ROW_TILES = (640, 512, 256, 128, 64, 32, 16, 8)
TN_ROW_TILES = (1024, 1408, 1536, 512, 256, 128)
TN_K_TILES = (1664, 640, 512, 256, 128)
COL_TILES = (1024, 512, 256, 128)
K_TILES = (2048, 1536, 1408, 1024, 512, 256, 128)


def _sigmoid(x):
    return 1.0 / (1.0 + jnp.exp(-x))


def _dot(a, b, dims):
    return lax.dot_general(a.astype(BF16), b.astype(BF16), (dims, ((), ())), preferred_element_type=F32)


NN = ((1,), (0,))
TN = ((0,), (0,))
NT = ((1,), (1,))


def _split3(x):
    hi = x.astype(BF16)
    r = x - hi.astype(F32)
    mid = r.astype(BF16)
    lo = (r - mid.astype(F32)).astype(BF16)
    return hi, mid, lo


def _dot_exact_rhs(sel, x, dims):
    hi, mid, lo = _split3(x)
    return _dot(sel, hi, dims) + _dot(sel, mid, dims) + _dot(sel, lo, dims)


def _mm(a, b, mode, out_dtype, name, res=None):
    if mode == "nn":
        (M, K), (K2, N) = a.shape, b.shape
    elif mode == "tn":
        (K, M), (K2, N) = a.shape, b.shape
    else:
        (M, K), (N, K2) = a.shape, b.shape
    assert K == K2, (name, a.shape, b.shape)
    tm = _tile(M, TN_ROW_TILES if mode == "tn" else ROW_TILES)
    tn = _tile(N, COL_TILES)
    tk = _tile(K, TN_K_TILES if mode == "tn" else K_TILES)
    nk = K // tk
    dims = {"nn": NN, "tn": TN, "nt": NT}[mode]

    def body(*refs):
        a_ref, b_ref = refs[:2]
        r_ref = None if res is None else refs[2]
        o_ref = refs[2 if res is None else 3]

        def finish(r):
            if r_ref is not None:
                r = r + r_ref[...].astype(F32)
            o_ref[...] = r.astype(out_dtype)

        if nk == 1:
            finish(_dot(a_ref[...], b_ref[...], dims))
            return
        acc = refs[-1]
        k = pl.program_id(2)

        @pl.when(k == 0)
        def _():
            acc[...] = jnp.zeros_like(acc)

        acc[...] += _dot(a_ref[...], b_ref[...], dims)

        @pl.when(k == nk - 1)
        def _():
            finish(acc[...])

    if mode == "nn":
        a_spec = pl.BlockSpec((tm, tk), lambda i, j, k: (i, k))
        b_spec = pl.BlockSpec((tk, tn), lambda i, j, k: (k, j))
    elif mode == "tn":
        a_spec = pl.BlockSpec((tk, tm), lambda i, j, k: (k, i))
        b_spec = pl.BlockSpec((tk, tn), lambda i, j, k: (k, j))
    else:
        a_spec = pl.BlockSpec((tm, tk), lambda i, j, k: (i, k))
        b_spec = pl.BlockSpec((tn, tk), lambda i, j, k: (j, k))
    in_specs = [a_spec, b_spec]
    operands = [a, b]
    if res is not None:
        in_specs.append(pl.BlockSpec((tm, tn), lambda i, j, k: (i, j)))
        operands.append(res)
    return pl.pallas_call(
        body, name=name,
        out_shape=jax.ShapeDtypeStruct((M, N), out_dtype),
        grid=(M // tm, N // tn, nk),
        in_specs=in_specs,
        out_specs=pl.BlockSpec((tm, tn), lambda i, j, k: (i, j)),
        scratch_shapes=[] if nk == 1 else [pltpu.VMEM((tm, tn), F32)],
        compiler_params=_params("parallel", "parallel", "arbitrary"),
    )(*operands)


def _rmsnorm_fwd(x, g, name, col0=0, width=None):
    L = x.shape[0]
    width = x.shape[1] if width is None else width
    assert col0 % width == 0
    cb = col0 // width
    tm = _tile(L, (128, 64, 32, 16))

    def body(x_ref, g_ref, o_ref):
        xv = x_ref[...]
        r = lax.rsqrt(jnp.mean(xv * xv, axis=-1, keepdims=True) + NORM_EPS)
        o_ref[...] = ((xv * r) * g_ref[...]).astype(BF16)

    return pl.pallas_call(
        body, name=name,
        out_shape=jax.ShapeDtypeStruct((L, width), BF16),
        grid=(L // tm,),
        in_specs=[pl.BlockSpec((tm, width), lambda i: (i, cb)), pl.BlockSpec((1, width), lambda i: (0, 0))],
        out_specs=pl.BlockSpec((tm, width), lambda i: (i, 0)),
        compiler_params=_params("parallel"),
    )(x, g)


def _rmsnorm_bwd(dy, x, g, name, out_dtype, col0=0, res=None):
    L, width = dy.shape
    assert col0 % width == 0
    cb = col0 // width
    tm = _tile(L, (128, 64, 32, 16))

    def body(*refs):
        if res is None:
            dy_ref, x_ref, g_ref, dx_ref, dg_ref = refs
            r_ref = None
        else:
            dy_ref, x_ref, g_ref, r_ref, dx_ref, dg_ref = refs

        @pl.when(pl.program_id(0) == 0)
        def _():
            dg_ref[...] = jnp.zeros_like(dg_ref)

        xv = x_ref[...]
        dyv = dy_ref[...].astype(F32)
        r = lax.rsqrt(jnp.mean(xv * xv, axis=-1, keepdims=True) + NORM_EPS)
        z = dyv * g_ref[...]
        dx = r * z - xv * ((r * r * r) * jnp.mean(z * xv, axis=-1, keepdims=True))
        if r_ref is not None:
            dx = dx + r_ref[...]
        dx_ref[...] = dx.astype(out_dtype)
        dg_ref[...] += jnp.sum(dyv * (xv * r), axis=0, keepdims=True)

    in_specs = [pl.BlockSpec((tm, width), lambda i: (i, 0)),
                pl.BlockSpec((tm, width), lambda i: (i, cb)),
                pl.BlockSpec((1, width), lambda i: (0, 0))]
    operands = [dy, x, g]
    if res is not None:
        in_specs.append(pl.BlockSpec((tm, width), lambda i: (i, 0)))
        operands.append(res)
    return pl.pallas_call(
        body, name=name,
        out_shape=(jax.ShapeDtypeStruct((L, width), out_dtype), jax.ShapeDtypeStruct((1, width), F32)),
        grid=(L // tm,),
        in_specs=in_specs,
        out_specs=(pl.BlockSpec((tm, width), lambda i: (i, 0)), pl.BlockSpec((1, width), lambda i: (0, 0))),
        compiler_params=_params("arbitrary"),
    )(*operands)


def _final_loss_bwd(h2, tgt, g):
    L, D = h2.shape
    tm = PREFIX
    inv_d = 1.0 / D

    def body(h_ref, t_ref, g_ref, dh_ref, loss_ref, dg_ref):
        i = pl.program_id(0)

        @pl.when(i == 0)
        def _():
            loss_ref[...] = jnp.zeros_like(loss_ref)
            dg_ref[...] = jnp.zeros_like(dg_ref)

        xv = h_ref[...]
        r = lax.rsqrt(jnp.mean(xv * xv, axis=-1, keepdims=True) + NORM_EPS)
        xn = xv * r
        y = xn * g_ref[...]
        real = (i >= PREFIX // tm).astype(F32)
        diff = (y - t_ref[...]) * real
        loss_ref[...] += 0.5 * inv_d * jnp.sum(diff * diff)
        dyv = diff * inv_d
        z = dyv * g_ref[...]
        dh_ref[...] = r * z - xv * ((r * r * r) * jnp.mean(z * xv, axis=-1, keepdims=True))
        dg_ref[...] += jnp.sum(dyv * xn, axis=0, keepdims=True)

    shift = PREFIX // tm
    return pl.pallas_call(
        body, name="final_loss_bwd",
        out_shape=(jax.ShapeDtypeStruct((L, D), F32), jax.ShapeDtypeStruct((1, 128), F32),
                   jax.ShapeDtypeStruct((1, D), F32)),
        grid=(L // tm,),
        in_specs=[pl.BlockSpec((tm, D), lambda i: (i, 0)),
                  pl.BlockSpec((tm, D), lambda i: (jnp.maximum(i - shift, 0), 0)),
                  pl.BlockSpec((1, D), lambda i: (0, 0))],
        out_specs=(pl.BlockSpec((tm, D), lambda i: (i, 0)), pl.BlockSpec((1, 128), lambda i: (0, 0)),
                   pl.BlockSpec((1, D), lambda i: (0, 0))),
        compiler_params=_params("arbitrary"),
    )(h2, tgt, g)


def _rot_half(x):
    lane = lax.broadcasted_iota(jnp.int32, x.shape, 1)
    return jnp.where(lane < ROPE // 2, -pltpu.roll(x, 128 - ROPE // 2, 1), pltpu.roll(x, ROPE // 2, 1))


def _rope_fwd(qp, kv, kr, cos, sin, heads):
    L = qp.shape[0]
    tm = _tile(L, (128,))

    def body(q_ref, kv_ref, kr_ref, c_ref, s_ref, qc_ref, kc_ref, v_ref):
        c, s = c_ref[...], s_ref[...]
        krv = kr_ref[...]
        kr_rot = (krv * c + _rot_half(krv) * s).astype(BF16)
        for h in range(heads):
            lo = h * QPAD
            qc_ref[:, lo:lo + HEAD] = (q_ref[:, lo:lo + HEAD] * SOFTMAX_SCALE).astype(BF16)
            qr = q_ref[:, lo + HEAD:lo + QPAD]
            qc_ref[:, lo + HEAD:lo + QPAD] = ((qr * c + _rot_half(qr) * s) * SOFTMAX_SCALE).astype(BF16)
            kc_ref[:, lo:lo + HEAD] = kv_ref[:, lo:lo + HEAD].astype(BF16)
            kc_ref[:, lo + HEAD:lo + QPAD] = kr_rot
            v_ref[:, h * HEAD:(h + 1) * HEAD] = kv_ref[:, lo + HEAD:lo + QPAD].astype(BF16)

    W = heads * QPAD
    row = lambda w: pl.BlockSpec((tm, w), lambda i: (i, 0))
    return pl.pallas_call(
        body, name="rope_fwd",
        out_shape=(jax.ShapeDtypeStruct((L, W), BF16), jax.ShapeDtypeStruct((L, W), BF16),
                   jax.ShapeDtypeStruct((L, heads * HEAD), BF16)),
        grid=(L // tm,),
        in_specs=[row(W), row(W), row(128), row(128), row(128)],
        out_specs=(row(W), row(W), row(heads * HEAD)),
        compiler_params=_params("parallel"),
    )(qp, kv, kr, cos, sin)


def _rope_bwd(dqc, dkc, dv, cos, sin, heads):
    L = dqc.shape[0]
    tm = _tile(L, (128,))

    def body(dq_ref, dk_ref, dv_ref, c_ref, s_ref, dqp_ref, dkv_ref, dkr_ref):
        c, s = c_ref[...], s_ref[...]
        acc = jnp.zeros((tm, 128), F32)
        for h in range(heads):
            lo = h * QPAD
            dqp_ref[:, lo:lo + HEAD] = (dq_ref[:, lo:lo + HEAD] * SOFTMAX_SCALE).astype(BF16)
            d = dq_ref[:, lo + HEAD:lo + QPAD]
            dqp_ref[:, lo + HEAD:lo + QPAD] = ((d * c - _rot_half(d) * s) * SOFTMAX_SCALE).astype(BF16)
            dkv_ref[:, lo:lo + HEAD] = dk_ref[:, lo:lo + HEAD].astype(BF16)
            dkv_ref[:, lo + HEAD:lo + QPAD] = dv_ref[:, h * HEAD:(h + 1) * HEAD].astype(BF16)
            acc = acc + dk_ref[:, lo + HEAD:lo + QPAD]
        dkr_ref[...] = (acc * c - _rot_half(acc) * s).astype(BF16)

    W = heads * QPAD
    row = lambda w: pl.BlockSpec((tm, w), lambda i: (i, 0))
    return pl.pallas_call(
        body, name="rope_bwd",
        out_shape=(jax.ShapeDtypeStruct((L, W), BF16), jax.ShapeDtypeStruct((L, W), BF16),
                   jax.ShapeDtypeStruct((L, 128), BF16)),
        grid=(L // tm,),
        in_specs=[row(W), row(W), row(heads * HEAD), row(128), row(128)],
        out_specs=(row(W), row(W), row(128)),
        compiler_params=_params("parallel"),
    )(dqc, dkc, dv, cos, sin)


def _attn_keep(qi, ki, ta):
    t = qi * ta + lax.broadcasted_iota(jnp.int32, (ta, ta), 0)
    s = ki * ta + lax.broadcasted_iota(jnp.int32, (ta, ta), 1)
    return (s <= t) & ((s >= PAD_LEN) | (s == t))


def _split_refs(refs, sizes):
    out, at = [], 0
    for n in sizes:
        out.append(refs[at:at + n])
        at += n
    return out


def _attn_fwd(qc, kc, v, heads, rider):
    L = qc.shape[0]
    ta = _tile(L, (640, 128))
    nb = L // ta
    pairs = [(i, j) for i in range(nb) for j in range(i + 1)]
    q_of = jnp.asarray([p[0] for p in pairs], jnp.int32)
    k_of = jnp.asarray([p[1] for p in pairs], jnp.int32)
    n_rin, n_rout = len(rider.operands), len(rider.out_shapes)
    per = _tile(heads, (ATTN_FWD_HEADS, 1))
    ng = heads // per

    def body(*refs):
        (q_of_ref, k_of_ref, q_ref, k_ref, v_ref), r_ins, (o_ref, lse_ref), r_outs, (m_sc, l_sc, acc_sc), r_sems = \
            _split_refs(refs, (5, n_rin, 2, n_rout, 3, len(rider.sem_shapes)))
        h, t = pl.program_id(0), pl.program_id(1)
        qi, ki = q_of_ref[t], k_of_ref[t]
        pl.when((h == 0) & (t == 0))(lambda: rider.start(r_ins, r_outs, r_sems))

        @pl.when(ki == 0)
        def _():
            m_sc[...] = jnp.full_like(m_sc, NEG)
            l_sc[...] = jnp.zeros_like(l_sc)
            acc_sc[...] = jnp.zeros_like(acc_sc)

        def step(masked):
            wide = lambda hh: slice(hh * QPAD, (hh + 1) * QPAD)
            lanes = lambda hh: slice(hh * HEAD, (hh + 1) * HEAD)
            ss = [_dot(q_ref[:, wide(hh)], k_ref[:, wide(hh)], NT) for hh in range(per)]
            if masked:
                keep = _attn_keep(qi, ki, ta)
                ss = [jnp.where(keep, s, NEG) for s in ss]
            for hh in range(per):
                m_old = m_sc[hh]
                m_new = jnp.maximum(m_old, jnp.max(ss[hh], axis=-1, keepdims=True))
                p = jnp.exp(ss[hh] - jnp.tile(m_new, (1, ta // HEAD)))
                alpha = jnp.exp(m_old - m_new)
                l_sc[hh] = alpha * l_sc[hh] + jnp.sum(p, axis=-1, keepdims=True)
                acc_sc[hh] = alpha * acc_sc[hh] + _dot(p, v_ref[:, lanes(hh)], NN)
                m_sc[hh] = m_new

        pl.when((ki == qi) | ((ki == 0) & (qi > 0)))(lambda: step(True))
        pl.when((ki > 0) & (ki < qi))(lambda: step(False))

        @pl.when(ki == qi)
        def _():
            for hh in range(per):
                l = l_sc[hh]
                o_ref[:, hh * HEAD:(hh + 1) * HEAD] = (acc_sc[hh] / l).astype(BF16)
                lse_ref[:, hh * HEAD:(hh + 1) * HEAD] = m_sc[hh] + jnp.log(l)

        pl.when((h == ng - 1) & (t == len(pairs) - 1))(lambda: rider.finish(r_ins, r_outs, r_sems))

    qrow = lambda w: pl.BlockSpec((ta, per * w), lambda h, t, q_of, k_of: (q_of[t], h))
    krow = lambda w: pl.BlockSpec((ta, per * w), lambda h, t, q_of, k_of: (k_of[t], h))
    stat = pltpu.VMEM((per, ta, HEAD), F32)
    res = pl.pallas_call(
        body, name="attn_fwd",
        out_shape=(jax.ShapeDtypeStruct((L, heads * HEAD), BF16), jax.ShapeDtypeStruct((L, heads * HEAD), F32),
                   *rider.out_shapes),
        grid_spec=pltpu.PrefetchScalarGridSpec(
            num_scalar_prefetch=2, grid=(ng, len(pairs)),
            in_specs=[qrow(QPAD), krow(QPAD), krow(HEAD)] + [ANY] * n_rin,
            out_specs=(qrow(HEAD), qrow(HEAD), *([ANY] * n_rout)),
            scratch_shapes=[stat, stat, stat] + rider.sem_shapes),
        compiler_params=pltpu.CompilerParams(dimension_semantics=("arbitrary", "arbitrary"),
                                             vmem_limit_bytes=VMEM_LIMIT_BYTES, has_side_effects=True),
    )(q_of, k_of, qc, kc, v, *rider.operands)
    return res[0], res[1], res[2:]


def _attn_delta(o, do, heads):
    L = o.shape[0]
    tm = _tile(L, (128,))

    def body(o_ref, do_ref, d_ref):
        for h in range(heads):
            cols = slice(h * HEAD, (h + 1) * HEAD)
            d = jnp.sum(do_ref[:, cols].astype(F32) * o_ref[:, cols].astype(F32), axis=-1, keepdims=True)
            d_ref[:, cols] = jnp.broadcast_to(d, (tm, HEAD))

    row = pl.BlockSpec((tm, heads * HEAD), lambda i: (i, 0))
    return pl.pallas_call(
        body, name="attn_delta", out_shape=jax.ShapeDtypeStruct((L, heads * HEAD), F32), grid=(L // tm,),
        in_specs=[row, row], out_specs=row, compiler_params=_params("parallel"),
    )(o, do)


def _attn_bwd(qc, kc, v, do, lse, delta, heads, rider):
    L = qc.shape[0]
    ta = _tile(L, (640, 128))
    nb = L // ta
    pairs = [(j, i) for j in range(nb) for i in range(j, nb)]
    k_of = jnp.asarray([p[0] for p in pairs], jnp.int32)
    q_of = jnp.asarray([p[1] for p in pairs], jnp.int32)
    n_rin, n_rout = len(rider.operands), len(rider.out_shapes)
    per = _tile(heads, (ATTN_BWD_HEADS, 1))
    ng = heads // per

    def body(*refs):
        ((k_of_ref, q_of_ref, q_ref, k_ref, v_ref, do_ref, lse_ref, delta_ref), r_ins, (dq_hbm, dk_ref, dv_ref),
         r_outs, (dq_sc, dq_sem), r_sems) = _split_refs(refs, (8, n_rin, 3, n_rout, 2, len(rider.sem_shapes)))
        h, t = pl.program_id(0), pl.program_id(1)
        kj, qi = k_of_ref[t], q_of_ref[t]
        pl.when((h == 0) & (t == 0))(lambda: rider.start(r_ins, r_outs, r_sems))

        @pl.when(t == 0)
        def _():
            dq_sc[...] = jnp.zeros_like(dq_sc)

        @pl.when(qi == kj)
        def _():
            dk_ref[...] = jnp.zeros_like(dk_ref)
            dv_ref[...] = jnp.zeros_like(dv_ref)

        def step(masked):
            wide = lambda hh: slice(hh * QPAD, (hh + 1) * QPAD)
            lanes = lambda hh: slice(hh * HEAD, (hh + 1) * HEAD)
            rep = (1, ta // HEAD)
            rows = pl.ds(pl.multiple_of(qi * ta, ta), ta)
            ss = [_dot(q_ref[:, wide(hh)], k_ref[:, wide(hh)], NT) for hh in range(per)]
            dps = [_dot(do_ref[:, lanes(hh)], v_ref[:, lanes(hh)], NT) for hh in range(per)]
            ps = [jnp.exp(ss[hh] - jnp.tile(lse_ref[:, lanes(hh)], rep)) for hh in range(per)]
            if masked:
                keep = _attn_keep(qi, kj, ta)
                ps = [jnp.where(keep, p, 0.0) for p in ps]
            dss = [ps[hh] * (dps[hh] - jnp.tile(delta_ref[:, lanes(hh)], rep)) for hh in range(per)]
            for hh in range(per):
                dv_ref[:, lanes(hh)] += _dot(ps[hh], do_ref[:, lanes(hh)], TN)
                dk_ref[:, wide(hh)] += _dot(dss[hh], q_ref[:, wide(hh)], TN)
                dq_sc[rows, wide(hh)] += _dot(dss[hh], k_ref[:, wide(hh)], NN)

        pl.when((qi == kj) | ((kj == 0) & (qi > 0)))(lambda: step(True))
        pl.when((kj > 0) & (qi > kj))(lambda: step(False))

        @pl.when(t == len(pairs) - 1)
        def _():
            cols = pl.ds(pl.multiple_of(h * (per * QPAD), per * QPAD), per * QPAD)
            out = pltpu.make_async_copy(dq_sc, dq_hbm.at[:, cols], dq_sem)
            out.start()
            out.wait()

        pl.when((h == ng - 1) & (t == len(pairs) - 1))(lambda: rider.finish(r_ins, r_outs, r_sems))

    qrow = lambda w: pl.BlockSpec((ta, per * w), lambda h, t, k_of, q_of: (q_of[t], h))
    krow = lambda w: pl.BlockSpec((ta, per * w), lambda h, t, k_of, q_of: (k_of[t], h))
    res = pl.pallas_call(
        body, name="attn_bwd",
        out_shape=(jax.ShapeDtypeStruct((L, heads * QPAD), F32), jax.ShapeDtypeStruct((L, heads * QPAD), F32),
                   jax.ShapeDtypeStruct((L, heads * HEAD), F32), *rider.out_shapes),
        grid_spec=pltpu.PrefetchScalarGridSpec(
            num_scalar_prefetch=2, grid=(ng, len(pairs)),
            in_specs=[qrow(QPAD), krow(QPAD), krow(HEAD), qrow(HEAD), qrow(HEAD), qrow(HEAD)] + [ANY] * n_rin,
            out_specs=(ANY, krow(QPAD), krow(HEAD), *([ANY] * n_rout)),
            scratch_shapes=[pltpu.VMEM((L, per * QPAD), F32), pltpu.SemaphoreType.DMA] + rider.sem_shapes),
        compiler_params=pltpu.CompilerParams(dimension_semantics=("arbitrary", "arbitrary"),
                                             vmem_limit_bytes=VMEM_LIMIT_BYTES, has_side_effects=True),
    )(k_of, q_of, qc, kc, v, do, lse, delta, *rider.operands)
    return res[0], res[1], res[2], res[3:]


def _hgrn_constants():
    t = jnp.arange(CHUNK)
    tril = (t[None, :] <= t[:, None]).astype(BF16)
    sel = []
    for lv in range(LEVELS):
        hs = 1 << lv
        mid = (t // (2 * hs)) * (2 * hs) + hs - 1
        sel.append((t[None, :] == mid[:, None]).astype(BF16))
    return tril, jnp.concatenate(sel, axis=0)


def _hgrn_decay_grad_constants():
    r = jnp.arange(CHUNK)[:, None]
    c = jnp.arange(CHUNK)[None, :]
    mats = []
    for lv in range(LEVELS):
        same = (r >> (lv + 1)) == (c >> (lv + 1))
        second = ((r >> lv) & 1) == 1
        mats.append(same & jnp.where(second, c >= r, c < r))
    mats += [c >= r, c < r]
    return jnp.concatenate(mats, axis=1).astype(BF16)


def _hgrn_gates(hq_ref, hf_ref, hi_ref, lb_ref, cols, row0):
    rows = row0 + lax.broadcasted_iota(jnp.int32, (CHUNK, 1), 0)
    valid = rows >= PAD_LEN
    lb = 1.0 / (1.0 + jnp.exp(lb_ref[1:2, cols] - lb_ref[0:1, cols]))
    hq = hq_ref[:, cols]
    sq = _sigmoid(hq)
    sg = _sigmoid(hf_ref[:, cols])
    f = lb + (1.0 - lb) * sg
    g = jnp.where(valid, jnp.log(f), 0.0)
    k = jnp.where(valid, 1.0 - f, 0.0)
    return dict(q=hq * sq, sq=sq, hq=hq, k=k, v=hi_ref[:, cols], g=g, f=f, sg=sg, lb=lb, valid=valid)


def _hgrn_prefix(tril_ref, sel_ref, gs):
    n = len(gs)
    b_all = _dot_exact_rhs(tril_ref[...], jnp.concatenate(gs, axis=1) if n > 1 else gs[0], NN)
    bm_all = _dot_exact_rhs(sel_ref[...], b_all, NN)
    cut = lambda a, i: a[:, i * HEAD:(i + 1) * HEAD]
    return [cut(b_all, i) for i in range(n)], [cut(bm_all, i) for i in range(n)]


def _hgrn_levels(q, k, b, bm_all):
    t = lax.broadcasted_iota(jnp.int32, (CHUNK, 1), 0)
    tt = lax.broadcasted_iota(jnp.int32, (CHUNK, CHUNK), 0)
    ss = lax.broadcasted_iota(jnp.int32, (CHUNK, CHUNK), 1)
    out = []
    for lv in range(LEVELS):
        bm = bm_all[lv * CHUNK:(lv + 1) * CHUNK, :]
        second = ((t >> lv) & 1) == 1
        eq = jnp.where(second, jnp.exp(jnp.minimum(b - bm, 0.0)), 0.0)
        ek = jnp.where(second, 0.0, jnp.exp(jnp.minimum(bm - b, 0.0)))
        same = (tt >> (lv + 1)) == (ss >> (lv + 1))
        out.append((eq, ek, (q * eq).astype(BF16), (k * ek).astype(BF16), same))
    return out


def _hgrn_intra(qs, ks, levels):
    tt = lax.broadcasted_iota(jnp.int32, (CHUNK, CHUNK), 0)
    ss = lax.broadcasted_iota(jnp.int32, (CHUNK, CHUNK), 1)
    ps = [jnp.where(tt == ss, jnp.sum(q * k, axis=-1, keepdims=True), 0.0) for q, k in zip(qs, ks)]
    for lv in range(LEVELS):
        for i, lvl in enumerate(levels):
            _, _, ql, kl, same = lvl[lv]
            ps[i] = ps[i] + jnp.where(same, _dot(ql, kl, NT), 0.0)
    return ps


def _hgrn_fwd(hp, lb_raw, g_norm, nh):
    L = hp.shape[0]
    D = nh * HEAD
    nc = L // CHUNK
    per = _tile(nh, (HGRN_FWD_HEADS, 2, 1))
    ng = nh // per
    tril, sel = _hgrn_constants()

    def body(hq_ref, hf_ref, hi_ref, hg_ref, lb_ref, gn_ref, tril_ref, sel_ref,
             oh_ref, orec_ref, shist_ref, s_sc, b_sc):
        c = pl.program_id(1)

        @pl.when(c == 0)
        def _():
            s_sc[...] = jnp.zeros_like(s_sc)

        heads_here = range(per)
        lanes = [slice(hh * HEAD, (hh + 1) * HEAD) for hh in heads_here]
        ws = [_hgrn_gates(hq_ref, hf_ref, hi_ref, lb_ref, lanes[hh], c * CHUNK) for hh in heads_here]
        qs, ks, vs = [w["q"] for w in ws], [w["k"] for w in ws], [w["v"] for w in ws]
        bs, bms = _hgrn_prefix(tril_ref, sel_ref, [w["g"] for w in ws])
        for hh in heads_here:
            b_sc[hh] = bs[hh]
        b_lasts = [b_sc[hh, CHUNK - 1:CHUNK, :] for hh in heads_here]
        ps = _hgrn_intra(qs, ks, [_hgrn_levels(qs[hh], ks[hh], bs[hh], bms[hh]) for hh in heads_here])
        s_ins = [s_sc[hh] for hh in heads_here]
        os_ = [_dot(ps[hh], vs[hh], NN) + _dot(qs[hh] * jnp.exp(bs[hh]), s_ins[hh], NT) for hh in heads_here]
        for hh in heads_here:
            shist_ref[0, hh] = s_ins[hh]
            s_sc[hh] = (jnp.exp(b_lasts[hh]) * s_ins[hh]
                        + _dot(vs[hh], ks[hh] * jnp.exp(b_lasts[hh] - bs[hh]), TN))
        for hh in heads_here:
            o = os_[hh]
            orec_ref[:, lanes[hh]] = o
            rn = lax.rsqrt(jnp.mean(o * o, axis=-1, keepdims=True) + NORM_EPS)
            hg = hg_ref[:, lanes[hh]]
            oh_ref[:, lanes[hh]] = (((o * rn) * gn_ref[...]) * (hg * _sigmoid(hg))).astype(BF16)

    col = lambda grp: pl.BlockSpec((CHUNK, per * HEAD), lambda h, c: (c, grp * ng + h))
    const = lambda shape: pl.BlockSpec(shape, lambda h, c: (0, 0))
    return pl.pallas_call(
        body, name="hgrn_fwd",
        out_shape=(jax.ShapeDtypeStruct((L, D), BF16), jax.ShapeDtypeStruct((L, D), F32),
                   jax.ShapeDtypeStruct((nc, nh, HEAD, HEAD), F32)),
        grid=(ng, nc),
        in_specs=[col(0), col(1), col(2), col(3),
                  pl.BlockSpec((2, per * HEAD), lambda h, c: (0, h)), const((1, HEAD)),
                  const((CHUNK, CHUNK)), const((LEVELS * CHUNK, CHUNK))],
        out_specs=(pl.BlockSpec((CHUNK, per * HEAD), lambda h, c: (c, h)),
                   pl.BlockSpec((CHUNK, per * HEAD), lambda h, c: (c, h)),
                   pl.BlockSpec((1, per, HEAD, HEAD), lambda h, c: (c, h, 0, 0))),
        scratch_shapes=[pltpu.VMEM((per, HEAD, HEAD), F32), pltpu.VMEM((per, CHUNK, HEAD), F32)],
        compiler_params=_params("parallel", "arbitrary"),
    )(hp, hp, hp, hp, lb_raw, g_norm, tril, sel)


def _hgrn_bwd(hp, lb_raw, g_norm, do_h, o_rec, s_hist, nh):
    L = hp.shape[0]
    D = nh * HEAD
    nc = L // CHUNK
    per = _tile(nh, (HGRN_BWD_HEADS, 1))
    ng = nh // per
    tril, sel = _hgrn_constants()
    tdec = _hgrn_decay_grad_constants()

    def body(hq_ref, hf_ref, hi_ref, hg_ref, lb_ref, gn_ref, tril_ref, sel_ref, tdec_ref, do_ref, orec_ref, shist_ref,
             dhq_ref, dhf_ref, dhi_ref, dhg_ref, dgn_ref, dlb_ref, ds_sc, b_sc):
        ci = pl.program_id(1)
        c = nc - 1 - ci

        @pl.when(ci == 0)
        def _():
            ds_sc[...] = jnp.zeros_like(ds_sc)
            dgn_ref[...] = jnp.zeros_like(dgn_ref)
            dlb_ref[...] = jnp.zeros_like(dlb_ref)

        heads_here = range(per)
        lanes = [slice(hh * HEAD, (hh + 1) * HEAD) for hh in heads_here]
        ws = [_hgrn_gates(hq_ref, hf_ref, hi_ref, lb_ref, lanes[hh], c * CHUNK) for hh in heads_here]
        qs, ks, vs = [w["q"] for w in ws], [w["k"] for w in ws], [w["v"] for w in ws]
        bs, bms = _hgrn_prefix(tril_ref, sel_ref, [w["g"] for w in ws])
        for hh in heads_here:
            b_sc[hh] = bs[hh]
        b_lasts = [b_sc[hh, CHUNK - 1:CHUNK, :] for hh in heads_here]
        levels = [_hgrn_levels(qs[hh], ks[hh], bs[hh], bms[hh]) for hh in heads_here]
        ps = _hgrn_intra(qs, ks, levels)
        s_ins = [shist_ref[0, hh] for hh in heads_here]
        ds_outs = [ds_sc[hh] for hh in heads_here]
        ebs = [jnp.exp(b) for b in bs]
        etails = [jnp.exp(b_lasts[hh] - bs[hh]) for hh in heads_here]
        decays = [jnp.exp(bl) for bl in b_lasts]

        dos = []
        for hh in heads_here:
            o = orec_ref[:, lanes[hh]]
            hg = hg_ref[:, lanes[hh]]
            sgg = _sigmoid(hg)
            rn = lax.rsqrt(jnp.mean(o * o, axis=-1, keepdims=True) + NORM_EPS)
            on = o * rn
            doh = do_ref[:, lanes[hh]]
            dy = doh * (hg * sgg)
            dhg_ref[:, lanes[hh]] = (doh * (on * gn_ref[...]) * (sgg * (1.0 + hg * (1.0 - sgg)))).astype(BF16)
            dgn_ref[hh] += jnp.broadcast_to(jnp.sum(dy * on, axis=0, keepdims=True), (8, HEAD))
            z = dy * gn_ref[...]
            dos.append(rn * z - o * ((rn * rn * rn) * jnp.mean(z * o, axis=-1, keepdims=True)))

        tt = lax.broadcasted_iota(jnp.int32, (CHUNK, CHUNK), 0)
        ss = lax.broadcasted_iota(jnp.int32, (CHUNK, CHUNK), 1)
        dps = [jnp.where(ss <= tt, _dot(dos[hh], vs[hh], NT), 0.0) for hh in heads_here]
        dvs = [_dot(ps[hh], dos[hh], TN) + _dot(ks[hh] * etails[hh], ds_outs[hh], NT) for hh in heads_here]
        dq_states = [ebs[hh] * _dot(dos[hh], s_ins[hh], NN) for hh in heads_here]
        dk_states = [etails[hh] * _dot(vs[hh], ds_outs[hh], NN) for hh in heads_here]
        dpds = [jnp.sum(jnp.where(tt == ss, dp, 0.0), axis=-1, keepdims=True) for dp in dps]
        dqs = [dpds[hh] * ks[hh] + dq_states[hh] for hh in heads_here]
        dks = [dpds[hh] * qs[hh] + dk_states[hh] for hh in heads_here]
        pair_terms = [[] for _ in heads_here]
        for lv in range(LEVELS):
            for hh in heads_here:
                eq, ek, ql, kl, same = levels[hh][lv]
                dpl = jnp.where(same, dps[hh], 0.0)
                dq_l = eq * _dot(dpl, kl, NN)
                dk_l = ek * _dot(dpl, ql, TN)
                dqs[hh] = dqs[hh] + dq_l
                dks[hh] = dks[hh] + dk_l
                pair_terms[hh].append(qs[hh] * dq_l + ks[hh] * dk_l)
        for hh in heads_here:
            pair_terms[hh] += [qs[hh] * dq_states[hh], ks[hh] * dk_states[hh]]
            ds_sc[hh] = decays[hh] * ds_outs[hh] + _dot(dos[hh], qs[hh] * ebs[hh], TN)
        stacked = [jnp.concatenate(terms, axis=0) for terms in pair_terms]
        dg_all = _dot_exact_rhs(tdec_ref[...], jnp.concatenate(stacked, axis=1) if per > 1 else stacked[0], NN)

        for hh in heads_here:
            w = ws[hh]
            f, sg, lb, sq, hq = w["f"], w["sg"], w["lb"], w["sq"], w["hq"]
            through = jnp.sum((decays[hh] * s_ins[hh]) * ds_outs[hh], axis=0, keepdims=True)
            dg = dg_all[:, lanes[hh]] + through
            df = jnp.where(w["valid"], dg / f - dks[hh], 0.0)
            dhf_ref[:, lanes[hh]] = (df * (1.0 - lb) * sg * (1.0 - sg)).astype(BF16)
            dlb_ref[hh] += jnp.broadcast_to(jnp.sum(df * (1.0 - sg), axis=0, keepdims=True), (8, HEAD))
            dhq_ref[:, lanes[hh]] = (dqs[hh] * (sq * (1.0 + hq * (1.0 - sq)))).astype(BF16)
            dhi_ref[:, lanes[hh]] = dvs[hh].astype(BF16)

    col = lambda grp: pl.BlockSpec((CHUNK, per * HEAD), lambda h, c: (nc - 1 - c, grp * ng + h))
    const = lambda shape: pl.BlockSpec(shape, lambda h, c: (0, 0))
    tile = pl.BlockSpec((CHUNK, per * HEAD), lambda h, c: (nc - 1 - c, h))
    part = pl.BlockSpec((per, 8, HEAD), lambda h, c: (h, 0, 0))
    return pl.pallas_call(
        body, name="hgrn_bwd",
        out_shape=tuple([jax.ShapeDtypeStruct((L, D), BF16)] * 4 + [jax.ShapeDtypeStruct((nh, 8, HEAD), F32)] * 2),
        grid=(ng, nc),
        in_specs=[col(0), col(1), col(2), col(3),
                  pl.BlockSpec((2, per * HEAD), lambda h, c: (0, h)), const((1, HEAD)),
                  const((CHUNK, CHUNK)), const((LEVELS * CHUNK, CHUNK)), const((CHUNK, (LEVELS + 2) * CHUNK)),
                  tile, tile, pl.BlockSpec((1, per, HEAD, HEAD), lambda h, c: (nc - 1 - c, h, 0, 0))],
        out_specs=(tile, tile, tile, tile, part, part),
        scratch_shapes=[pltpu.VMEM((per, HEAD, HEAD), F32), pltpu.VMEM((per, CHUNK, HEAD), F32)],
        compiler_params=_params("parallel", "arbitrary"),
    )(hp, hp, hp, hp, lb_raw, g_norm, tril, sel, tdec, do_h, o_rec, s_hist)


def _merge_fwd(a, bm, gates):
    L, D = a.shape
    tm = _tile(L, (128,))

    def body(a_ref, b_ref, g_ref, o_ref):
        o_ref[...] = (_sigmoid(g_ref[:, :D]) * a_ref[...] + _sigmoid(g_ref[:, D:]) * b_ref[...]).astype(BF16)

    row = lambda w: pl.BlockSpec((tm, w), lambda i: (i, 0))
    return pl.pallas_call(
        body, name="merge_fwd", out_shape=jax.ShapeDtypeStruct((L, D), BF16), grid=(L // tm,),
        in_specs=[row(D), row(D), row(2 * D)], out_specs=row(D), compiler_params=_params("parallel"),
    )(a, bm, gates)


def _merge_bwd(dm, a, bm, gates):
    L, D = a.shape
    tm = _tile(L, (128,))

    def body(dm_ref, a_ref, b_ref, g_ref, da_ref, db_ref, dg_ref):
        d = dm_ref[...]
        sa, sb = _sigmoid(g_ref[:, :D]), _sigmoid(g_ref[:, D:])
        da_ref[...] = (d * sa).astype(BF16)
        db_ref[...] = (d * sb).astype(BF16)
        dg_ref[:, :D] = (d * a_ref[...] * sa * (1.0 - sa)).astype(BF16)
        dg_ref[:, D:] = (d * b_ref[...] * sb * (1.0 - sb)).astype(BF16)

    row = lambda w: pl.BlockSpec((tm, w), lambda i: (i, 0))
    return pl.pallas_call(
        body, name="merge_bwd",
        out_shape=(jax.ShapeDtypeStruct((L, D), BF16), jax.ShapeDtypeStruct((L, D), BF16),
                   jax.ShapeDtypeStruct((L, 2 * D), BF16)),
        grid=(L // tm,),
        in_specs=[row(D), row(D), row(D), row(2 * D)], out_specs=(row(D), row(D), row(2 * D)),
        compiler_params=_params("parallel"),
    )(dm, a, bm, gates)


def _conv_taps(g_ref, halo_ref, i, tm):
    rows = i * tm + lax.broadcasted_iota(jnp.int32, (tm, 1), 0)
    g0 = jnp.where(rows >= PAD_LEN, g_ref[...], 0.0)
    hrow = i * tm - 8 + lax.broadcasted_iota(jnp.int32, (8, 1), 0)
    halo = jnp.where(hrow >= PAD_LEN, halo_ref[...], 0.0)
    r = lax.broadcasted_iota(jnp.int32, (tm, 1), 0)
    h7 = jnp.sum(jnp.where(lax.broadcasted_iota(jnp.int32, (8, 1), 0) == 7, halo, 0.0), axis=0, keepdims=True)
    h6 = jnp.sum(jnp.where(lax.broadcasted_iota(jnp.int32, (8, 1), 0) == 6, halo, 0.0), axis=0, keepdims=True)
    g1 = jnp.where(r == 0, h7, pltpu.roll(g0, 1, 0))
    g2 = jnp.where(r == 0, h6, jnp.where(r == 1, h7, pltpu.roll(g0, 2, 0)))
    return g0, g1, g2


def _conv_fwd(gu, cw, cb):
    L, F2 = gu.shape
    F = F2 // 2
    tm = _tile(L, ROW_TILES)
    tn = _tile(F, (512, 256, 128))
    nj = F // tn

    def body(g_ref, halo_ref, u_ref, cw_ref, cb_ref, o_ref):
        g0, g1, g2 = _conv_taps(g_ref, halo_ref, pl.program_id(0), tm)
        conv = cw_ref[0:1, :] * g2 + cw_ref[1:2, :] * g1 + cw_ref[2:3, :] * g0 + cb_ref[...]
        o_ref[...] = (conv * _sigmoid(conv) * u_ref[...]).astype(BF16)

    return pl.pallas_call(
        body, name="conv_fwd", out_shape=jax.ShapeDtypeStruct((L, F), BF16), grid=(L // tm, nj),
        in_specs=[pl.BlockSpec((tm, tn), lambda i, j: (i, j)),
                  pl.BlockSpec((8, tn), lambda i, j: (jnp.maximum(i * (tm // 8) - 1, 0), j)),
                  pl.BlockSpec((tm, tn), lambda i, j: (i, j + nj)),
                  pl.BlockSpec((3, tn), lambda i, j: (0, j)),
                  pl.BlockSpec((1, tn), lambda i, j: (0, j))],
        out_specs=pl.BlockSpec((tm, tn), lambda i, j: (i, j)),
        compiler_params=_params("parallel", "parallel"),
    )(gu, gu, gu, cw, cb)


def _conv_bwd_a(da, gu, cw, cb):
    L, F2 = gu.shape
    F = F2 // 2
    tm = _tile(L, ROW_TILES)
    tn = _tile(F, (512, 256, 128))
    nj = F // tn

    def body(da_ref, g_ref, halo_ref, u_ref, cw_ref, cb_ref, dc_ref, du_ref, dcb_ref, dcw_ref):
        i = pl.program_id(1)

        @pl.when(i == 0)
        def _():
            dcb_ref[...] = jnp.zeros_like(dcb_ref)
            dcw_ref[...] = jnp.zeros_like(dcw_ref)

        g0, g1, g2 = _conv_taps(g_ref, halo_ref, i, tm)
        conv = cw_ref[0:1, :] * g2 + cw_ref[1:2, :] * g1 + cw_ref[2:3, :] * g0 + cb_ref[...]
        sc = _sigmoid(conv)
        dav = da_ref[...]
        du_ref[...] = (dav * (conv * sc)).astype(BF16)
        dconv = dav * u_ref[...] * (sc * (1.0 + conv * (1.0 - sc)))
        dc_ref[...] = dconv
        dcb_ref[...] += jnp.sum(dconv, axis=0, keepdims=True)
        dcw_ref[0:1, :] += jnp.sum(dconv * g2, axis=0, keepdims=True)
        dcw_ref[1:2, :] += jnp.sum(dconv * g1, axis=0, keepdims=True)
        dcw_ref[2:3, :] += jnp.sum(dconv * g0, axis=0, keepdims=True)

    return pl.pallas_call(
        body, name="conv_bwd_a",
        out_shape=(jax.ShapeDtypeStruct((L, F), F32), jax.ShapeDtypeStruct((L, F), BF16),
                   jax.ShapeDtypeStruct((1, F), F32), jax.ShapeDtypeStruct((8, F), F32)),
        grid=(nj, L // tm),
        in_specs=[pl.BlockSpec((tm, tn), lambda j, i: (i, j)),
                  pl.BlockSpec((tm, tn), lambda j, i: (i, j)),
                  pl.BlockSpec((8, tn), lambda j, i: (jnp.maximum(i * (tm // 8) - 1, 0), j)),
                  pl.BlockSpec((tm, tn), lambda j, i: (i, j + nj)),
                  pl.BlockSpec((3, tn), lambda j, i: (0, j)),
                  pl.BlockSpec((1, tn), lambda j, i: (0, j))],
        out_specs=(pl.BlockSpec((tm, tn), lambda j, i: (i, j)), pl.BlockSpec((tm, tn), lambda j, i: (i, j)),
                   pl.BlockSpec((1, tn), lambda j, i: (0, j)), pl.BlockSpec((8, tn), lambda j, i: (0, j))),
        compiler_params=_params("parallel", "arbitrary"),
    )(da, gu, gu, gu, cw, cb)


def _conv_bwd_b(dconv, cw):
    L, F = dconv.shape
    tm = _tile(L, ROW_TILES)
    tn = _tile(F, (512, 256, 128))
    nblk8 = L // 8
    ni = L // tm

    def body(dc_ref, nxt_ref, cw_ref, o_ref):
        i = pl.program_id(0)
        dc = dc_ref[...]
        nxt = jnp.where(i < ni - 1, nxt_ref[...], 0.0)
        sub = lax.broadcasted_iota(jnp.int32, (8, 1), 0)
        n0 = jnp.sum(jnp.where(sub == 0, nxt, 0.0), axis=0, keepdims=True)
        n1 = jnp.sum(jnp.where(sub == 1, nxt, 0.0), axis=0, keepdims=True)
        r = lax.broadcasted_iota(jnp.int32, (tm, 1), 0)
        d1 = jnp.where(r == tm - 1, n0, pltpu.roll(dc, tm - 1, 0))
        d2 = jnp.where(r == tm - 2, n0, jnp.where(r == tm - 1, n1, pltpu.roll(dc, tm - 2, 0)))
        dg = cw_ref[2:3, :] * dc + cw_ref[1:2, :] * d1 + cw_ref[0:1, :] * d2
        rows = i * tm + r
        o_ref[...] = jnp.where(rows >= PAD_LEN, dg, 0.0).astype(BF16)

    return pl.pallas_call(
        body, name="conv_bwd_b", out_shape=jax.ShapeDtypeStruct((L, F), BF16), grid=(ni, F // tn),
        in_specs=[pl.BlockSpec((tm, tn), lambda i, j: (i, j)),
                  pl.BlockSpec((8, tn), lambda i, j: (jnp.minimum((i + 1) * (tm // 8), nblk8 - 1), j)),
                  pl.BlockSpec((3, tn), lambda i, j: (0, j))],
        out_specs=pl.BlockSpec((tm, tn), lambda i, j: (i, j)),
        compiler_params=_params("parallel", "parallel"),
    )(dconv, dconv, cw)


ANY = pl.BlockSpec(memory_space=pl.ANY)


def _coords():
    return lax.axis_index("x"), lax.axis_index("y"), lax.axis_index("c")


def _flip(v, bit):
    return 1 - v if bit else v


CHIPS = [(1, 0), (0, 1), (1, 1)]
PEERS = [(dx, dy, dc) for dx in (0, 1) for dy in (0, 1) for dc in (0, 1)][1:]


class _Rider:
    def __init__(self, operands, out_shapes, sem_shapes, start, finish):
        self.operands, self.out_shapes, self.sem_shapes = list(operands), list(out_shapes), list(sem_shapes)
        self.start, self.finish = start, finish


def _run_rider(rider, name):
    n_in, n_out = len(rider.operands), len(rider.out_shapes)

    def body(*refs):
        ins, outs, sems = refs[:n_in], refs[n_in:n_in + n_out], refs[n_in + n_out:]
        rider.start(ins, outs, sems)
        rider.finish(ins, outs, sems)

    return pl.pallas_call(
        body, name=name, out_shape=tuple(rider.out_shapes),
        in_specs=[ANY] * n_in, out_specs=tuple([ANY] * n_out), scratch_shapes=rider.sem_shapes,
        compiler_params=pltpu.CompilerParams(has_side_effects=True),
    )(*rider.operands)


def _gather_rider(big, small):
    nbig, n = len(big), len(big) + len(small)
    arrays = list(big) + list(small)

    def plan(ins, outs, sems):
        ici_send, ici_recv, d2d_send, d2d_recv, local_sems = sems
        x, y, c = _coords()
        mine = 2 * x + y

        def half(w, h):
            r2 = arrays[w].shape[0] // 2
            return pl.ds(h * r2, r2)

        def ici(w, j, landing):
            px, py = _flip(x, CHIPS[j][0]), _flip(y, CHIPS[j][1])
            slot = 2 * px + py if landing else mine
            if w < nbig:
                src, dst = ins[w].at[half(w, c)], outs[w].at[slot, half(w, c)]
            else:
                src, dst = ins[w], outs[w].at[slot]
            return pltpu.make_async_remote_copy(
                src_ref=src, dst_ref=dst, send_sem=ici_send.at[w * 3 + j], recv_sem=ici_recv.at[w * 3 + j],
                device_id=(px, py, c), device_id_type=MESH)

        def d2d(w, j, landing):
            px, py = _flip(x, CHIPS[j][0]), _flip(y, CHIPS[j][1])
            mine_rows = outs[w].at[2 * px + py, half(w, c)]
            dst = outs[w].at[2 * px + py, half(w, 1 - c)] if landing else mine_rows
            return pltpu.make_async_remote_copy(
                src_ref=mine_rows, dst_ref=dst, send_sem=d2d_send.at[w * 3 + j], recv_sem=d2d_recv.at[w * 3 + j],
                device_id=(x, y, 1 - c), device_id_type=MESH)

        local = [pltpu.make_async_copy(ins[w], outs[w].at[mine], local_sems.at[w]) for w in range(n)]
        return ici, d2d, local

    def start(ins, outs, sems):
        ici, _, local = plan(ins, outs, sems)
        for cp in local:
            cp.start()
        for w in range(n):
            for j in range(3):
                ici(w, j, False).start()

    def finish(ins, outs, sems):
        ici, d2d, local = plan(ins, outs, sems)
        for w in range(n):
            for j in range(3):
                ici(w, j, True).wait_recv()
                if w < nbig:
                    d2d(w, j, False).start()
        for w in range(nbig):
            for j in range(3):
                d2d(w, j, True).wait_recv()
        for w in range(n):
            for j in range(3):
                ici(w, j, False).wait_send()
                if w < nbig:
                    d2d(w, j, False).wait_send()
        for cp in local:
            cp.wait()

    return _Rider(
        arrays, [jax.ShapeDtypeStruct((4,) + s.shape, s.dtype) for s in arrays],
        [pltpu.SemaphoreType.DMA((3 * n,)), pltpu.SemaphoreType.DMA((3 * n,)),
         pltpu.SemaphoreType.DMA((max(3 * nbig, 1),)), pltpu.SemaphoreType.DMA((max(3 * nbig, 1),)),
         pltpu.SemaphoreType.DMA((n,))],
        start, finish)


def _to_sibling(arrays, name):
    n = len(arrays)

    def body(*refs):
        ins, outs = refs[:n], refs[n:2 * n]
        send_sems, recv_sems = refs[2 * n:]
        x, y, c = _coords()

        def copy(w):
            return pltpu.make_async_remote_copy(
                src_ref=ins[w], dst_ref=outs[w], send_sem=send_sems.at[w], recv_sem=recv_sems.at[w],
                device_id=(x, y, 1 - c), device_id_type=MESH)

        for w in range(n):
            copy(w).start()
        for w in range(n):
            copy(w).wait_recv()
            copy(w).wait_send()

    return pl.pallas_call(
        body, name=name,
        out_shape=tuple(jax.ShapeDtypeStruct(a.shape, a.dtype) for a in arrays),
        in_specs=[ANY] * n, out_specs=tuple([ANY] * n),
        scratch_shapes=[pltpu.SemaphoreType.DMA((n,)), pltpu.SemaphoreType.DMA((n,))],
        compiler_params=pltpu.CompilerParams(has_side_effects=True),
    )(*arrays)


def _pair_sum(a, b, name):
    _, r, c = a.shape
    tr = _tile(r, (64, 32, 16))

    def body(a_ref, b_ref, o_ref):
        o_ref[...] = (a_ref[...].astype(F32) + b_ref[...].astype(F32)).astype(BF16)

    blk = pl.BlockSpec((4, tr, c), lambda i: (0, i, 0))
    return pl.pallas_call(
        body, name=name, out_shape=jax.ShapeDtypeStruct(a.shape, BF16), grid=(r // tr,),
        in_specs=[blk, blk], out_specs=blk, compiler_params=_params("parallel"),
    )(a, b)


def _scatter_rider(parts):
    n = len(parts)

    def plan(ins, outs, sems):
        send_sems, recv_sems, local_sems = sems
        x, y, c = _coords()
        mine = 2 * x + y

        def ici(w, j, landing):
            px, py = _flip(x, CHIPS[j][0]), _flip(y, CHIPS[j][1])
            return pltpu.make_async_remote_copy(
                src_ref=ins[w].at[2 * px + py], dst_ref=outs[w].at[2 * px + py if landing else mine],
                send_sem=send_sems.at[w * 3 + j], recv_sem=recv_sems.at[w * 3 + j],
                device_id=(px, py, c), device_id_type=MESH)

        local = [pltpu.make_async_copy(ins[w].at[mine], outs[w].at[mine], local_sems.at[w]) for w in range(n)]
        return ici, local

    def start(ins, outs, sems):
        ici, local = plan(ins, outs, sems)
        for cp in local:
            cp.start()
        for w in range(n):
            for j in range(3):
                ici(w, j, False).start()

    def finish(ins, outs, sems):
        ici, local = plan(ins, outs, sems)
        for w in range(n):
            for j in range(3):
                ici(w, j, True).wait_recv()
                ici(w, j, False).wait_send()
        for cp in local:
            cp.wait()

    return _Rider(
        parts, [jax.ShapeDtypeStruct(p.shape, p.dtype) for p in parts],
        [pltpu.SemaphoreType.DMA((3 * n,)), pltpu.SemaphoreType.DMA((3 * n,)), pltpu.SemaphoreType.DMA((n,))],
        start, finish)


def _sum4(recv, name):
    _, r, c = recv.shape
    tr = _tile(r, (64, 32, 16))

    def body(in_ref, o_ref):
        o_ref[...] = ((in_ref[0].astype(F32) + in_ref[1].astype(F32)) + in_ref[2].astype(F32)) + in_ref[3].astype(F32)

    return pl.pallas_call(
        body, name=name, out_shape=jax.ShapeDtypeStruct((r, c), F32), grid=(r // tr,),
        in_specs=[pl.BlockSpec((4, tr, c), lambda i: (0, i, 0))],
        out_specs=pl.BlockSpec((tr, c), lambda i: (i, 0)),
        compiler_params=_params("parallel"),
    )(recv)


def _allreduce_small(packed):
    R = packed.shape[0]

    def body(in_ref, o_ref, buf, send_sems, recv_sems):
        x, y, c = _coords()
        me = 4 * x + 2 * y + c
        buf[me] = in_ref[...]
        for j, (dx, dy, dc) in enumerate(PEERS):
            px, py, pc = _flip(x, dx), _flip(y, dy), _flip(c, dc)
            pltpu.make_async_remote_copy(
                src_ref=in_ref, dst_ref=buf.at[me], send_sem=send_sems.at[j], recv_sem=recv_sems.at[j],
                device_id=(px, py, pc), device_id_type=MESH).start()
        for j, (dx, dy, dc) in enumerate(PEERS):
            px, py, pc = _flip(x, dx), _flip(y, dy), _flip(c, dc)
            rc = pltpu.make_async_remote_copy(
                src_ref=in_ref, dst_ref=buf.at[4 * px + 2 * py + pc], send_sem=send_sems.at[j],
                recv_sem=recv_sems.at[j], device_id=(px, py, pc), device_id_type=MESH)
            rc.wait_recv()
            rc.wait_send()
        acc = buf[0]
        for s in range(1, 8):
            acc = acc + buf[s]
        o_ref[...] = acc

    return pl.pallas_call(
        body, name="allreduce_small", out_shape=jax.ShapeDtypeStruct((R, 128), F32),
        in_specs=[pl.BlockSpec(memory_space=pltpu.VMEM)], out_specs=pl.BlockSpec(memory_space=pltpu.VMEM),
        scratch_shapes=[pltpu.VMEM((8, R, 128), F32), pltpu.SemaphoreType.DMA((7,)), pltpu.SemaphoreType.DMA((7,))],
        compiler_params=pltpu.CompilerParams(has_side_effects=True, vmem_limit_bytes=VMEM_LIMIT_BYTES),
    )(packed)


def _adamw_math(w, g, m, v):
    m = ADAM_B1 * m + (1.0 - ADAM_B1) * g
    v = ADAM_B2 * v + (1.0 - ADAM_B2) * (g * g)
    m_hat = m / (1.0 - ADAM_B1 ** ADAM_STEP)
    v_hat = v / (1.0 - ADAM_B2 ** ADAM_STEP)
    delta = -ADAM_LR * (m_hat / (jnp.sqrt(v_hat) + ADAM_EPS) + ADAM_WD * w)
    return delta, m, v


def _adamw_big(mine, other, w, m, v, name):
    R, C = w.shape
    tr = _tile(R // 2, (128, 64, 32, 16, 8))
    nb = (R // 2) // tr

    def body(mine_ref, other_ref, w_ref, m_ref, v_ref, g_ref, d_ref, mo_ref, vo_ref):
        is_mine = (pl.program_id(0) // nb) == lax.axis_index("c")
        g = jnp.where(is_mine, mine_ref[...], other_ref[...])
        d, mn, vn = _adamw_math(w_ref[...], g, m_ref[...], v_ref[...])
        g_ref[...] = g
        d_ref[...] = d
        mo_ref[...] = mn
        vo_ref[...] = vn

    blk = pl.BlockSpec((tr, C), lambda i: (i, 0))
    half = pl.BlockSpec((tr, C), lambda i: (i % nb, 0))
    sds = jax.ShapeDtypeStruct((R, C), F32)
    return pl.pallas_call(
        body, name=name, out_shape=(sds, sds, sds, sds), grid=(2 * nb,),
        in_specs=[half, half, blk, blk, blk], out_specs=(blk, blk, blk, blk), compiler_params=_params("parallel"),
    )(mine, other, w, m, v)


def _adamw_small(items, lb_raw, dlb):
    n = len(items)
    lb_w, lb_m, lb_v = lb_raw

    def body(*refs):
        ins = refs[:4 * n]
        dlb_ref, lw_ref, lm_ref, lv_ref = refs[4 * n:4 * n + 4]
        outs = refs[4 * n + 4:]
        for t in range(n):
            g_ref, w_ref, m_ref, v_ref = ins[4 * t:4 * t + 4]
            d, mn, vn = _adamw_math(w_ref[...], g_ref[...], m_ref[...], v_ref[...])
            outs[3 * t][...] = d
            outs[3 * t + 1][...] = mn
            outs[3 * t + 2][...] = vn
        p0 = 1.0 / (1.0 + jnp.exp(lw_ref[1:2, :] - lw_ref[0:1, :]))
        g0 = dlb_ref[...] * p0 * (1.0 - p0)
        base = 3 * n
        outs[base][0:1, :] = g0
        outs[base][1:2, :] = -g0
        d, mn, vn = _adamw_math(lw_ref[...], outs[base][...], lm_ref[...], lv_ref[...])
        outs[base + 1][...] = d
        outs[base + 2][...] = mn
        outs[base + 3][...] = vn

    operands = [a for it in items for a in it] + [dlb, lb_w, lb_m, lb_v]
    out_shape = []
    for (g, w, m, v) in items:
        out_shape += [jax.ShapeDtypeStruct(w.shape, F32)] * 3
    out_shape += [jax.ShapeDtypeStruct(lb_w.shape, F32)] * 4
    vm = pl.BlockSpec(memory_space=pltpu.VMEM)
    res = pl.pallas_call(
        body, name="adamw_small", out_shape=tuple(out_shape),
        in_specs=[vm] * len(operands), out_specs=tuple([vm] * len(out_shape)),
        compiler_params=pltpu.CompilerParams(vmem_limit_bytes=VMEM_LIMIT_BYTES),
    )(*operands)
    deltas = [res[3 * t] for t in range(n)] + [res[3 * n + 1]]
    new_m = [res[3 * t + 1] for t in range(n)] + [res[3 * n + 2]]
    new_v = [res[3 * t + 2] for t in range(n)] + [res[3 * n + 3]]
    return res[3 * n], deltas, new_m, new_v


def _shard_row_half(g, by_cols, h):
    R, C = g.shape
    if by_cols:
        part = lax.dynamic_index_in_dim(g.reshape(2, R // 2, 4, C // 4), h, axis=0, keepdims=False)
        return part.transpose(1, 0, 2).astype(BF16)
    return lax.dynamic_index_in_dim(g.reshape(4, 2, R // 8, C), h, axis=1, keepdims=False).astype(BF16)


def kernel(x, positions, meta_tokens, w_in, w_q_up, w_kv_up, w_branch_mla, w_branch_hgrn, w_out, w_ffn_in, w_ffn_out, conv_w, conv_b, g_mix_norm, g_q_norm, g_kv_norm, g_hgrn_norm, g_ffn_norm, g_final_norm, lb_raw, loss_target, m_meta_tokens, m_w_in, m_w_q_up, m_w_kv_up, m_w_branch_mla, m_w_branch_hgrn, m_w_out, m_w_ffn_in, m_w_ffn_out, m_conv_w, m_conv_b, m_g_mix_norm, m_g_q_norm, m_g_kv_norm, m_g_hgrn_norm, m_g_ffn_norm, m_g_final_norm, m_lb_raw, v_meta_tokens, v_w_in, v_w_q_up, v_w_kv_up, v_w_branch_mla, v_w_branch_hgrn, v_w_out, v_w_ffn_in, v_w_ffn_out, v_conv_w, v_conv_b, v_g_mix_norm, v_g_q_norm, v_g_kv_norm, v_g_hgrn_norm, v_g_ffn_norm, v_g_final_norm, v_lb_raw):
    S, D = x.shape[1], x.shape[2]
    L = S + PREFIX
    QL, KVL = g_q_norm.shape[1], g_kv_norm.shape[1]
    F = conv_b.shape[1]
    heads = (4 * w_kv_up.shape[2]) // QPAD
    nh = D // HEAD
    assert lb_raw.shape[0] == 2 and g_hgrn_norm.shape[1] == HEAD and L % CHUNK == 0
    ix, iy, ic = _coords()
    chip = 2 * ix + iy

    big = [w_in, w_q_up, w_kv_up, w_branch_mla, w_branch_hgrn, w_out, w_ffn_in, w_ffn_out]
    col_sharded = [True, True, True, False, False, False, True, False]
    shards = [w[0].astype(BF16) for w in big]
    early = _run_rider(_gather_rider(shards[:3], [meta_tokens, conv_w[0]]), "gather_early")
    late_rider = _gather_rider(shards[3:], [])

    def full(gw, by_cols):
        _, r, c = gw.shape
        return gw.transpose(1, 0, 2).reshape(r, 4 * c) if by_cols else gw.reshape(4 * r, c)

    W_in, W_q, W_kv = [full(gw, True) for gw in early[:3]]
    meta_full = full(early[3], True)
    cw_full = full(early[4], True)
    c0 = QL + KVL
    W_lat = W_in[:, :c0]
    W_kr = jnp.pad(W_in[:, c0:c0 + ROPE], ((0, 0), (0, 128 - ROPE)))
    W_H = W_in[:, c0 + ROPE:c0 + ROPE + 4 * D]
    W_G = W_in[:, c0 + ROPE + 4 * D:]
    W_qp = jnp.pad(W_q.reshape(QL, heads, QK_HEAD), ((0, 0), (0, 0), (0, QPAD - QK_HEAD))).reshape(QL, heads * QPAD)

    h0 = jnp.concatenate([jnp.zeros((PAD_LEN, D), F32), meta_full, x[0]], axis=0)
    pos = jnp.concatenate([jnp.zeros((PAD_LEN,), jnp.int32), jnp.arange(N_META, dtype=jnp.int32),
                           positions[0].astype(jnp.int32) + N_META])
    inv = 1.0 / (ROPE_THETA ** (jnp.arange(0, ROPE, 2, dtype=F32) / ROPE))
    ang = pos.astype(F32)[:, None] * inv
    zero = jnp.zeros((L, 128 - ROPE), F32)
    cos = jnp.concatenate([jnp.cos(ang), jnp.cos(ang), zero], axis=1)
    sin = jnp.concatenate([jnp.sin(ang), jnp.sin(ang), zero], axis=1)

    u1 = _rmsnorm_fwd(h0, g_mix_norm, "norm_mix")
    lat = _mm(u1, W_lat, "nn", F32, "proj_lat")
    hp = _mm(u1, W_H, "nn", F32, "proj_hgrn")
    gates = _mm(u1, W_G, "nn", F32, "proj_gates")
    kr = _mm(u1, W_kr, "nn", F32, "proj_krope")
    qn = _rmsnorm_fwd(lat, g_q_norm, "norm_q", col0=0, width=QL)
    kvn = _rmsnorm_fwd(lat, g_kv_norm, "norm_kv", col0=QL, width=KVL)
    qp = _mm(qn, W_qp, "nn", F32, "q_up")
    kv = _mm(kvn, W_kv, "nn", F32, "kv_up")
    qc, kc, vv = _rope_fwd(qp, kv, kr, cos, sin, heads)
    o_mla, lse, late = _attn_fwd(qc, kc, vv, heads, late_rider)
    W_a, W_b, W_o, W_fi, W_fo = [full(gw, bc) for gw, bc in zip(late, col_sharded[3:])]
    o_hgrn, o_rec, s_hist = _hgrn_fwd(hp, lb_raw, g_hgrn_norm, nh)
    br_a = _mm(o_mla, W_a, "nn", F32, "branch_mla")
    br_b = _mm(o_hgrn, W_b, "nn", F32, "branch_hgrn")
    merged = _merge_fwd(br_a, br_b, gates)
    h1 = _mm(merged, W_o, "nn", F32, "out_proj", res=h0)
    u2 = _rmsnorm_fwd(h1, g_ffn_norm, "norm_ffn")
    gu = _mm(u2, W_fi, "nn", F32, "ffn_in")
    act = _conv_fwd(gu, cw_full, conv_b)
    h2 = _mm(act, W_fo, "nn", F32, "ffn_out", res=h1)
    dh2, loss_p, dg_final = _final_loss_bwd(h2, loss_target[0], g_final_norm.reshape(1, D))

    dact = _mm(dh2, W_fo, "nt", F32, "d_act")
    dW_fo = _mm(act, dh2, "tn", F32, "dw_ffn_out")
    dconv, dup, dcb, dcw = _conv_bwd_a(dact, gu, cw_full, conv_b)
    dgate = _conv_bwd_b(dconv, cw_full)
    dgu = jnp.concatenate([dgate, dup], axis=1)
    du2 = _mm(dgu, W_fi, "nt", F32, "d_u2")
    dW_fi = _mm(u2, dgu, "tn", F32, "dw_ffn_in")
    dh1, dg_ffn = _rmsnorm_bwd(du2, h1, g_ffn_norm, "norm_ffn_bwd", F32, res=dh2)
    dmerged = _mm(dh1, W_o, "nt", F32, "d_merged")
    dW_o = _mm(merged, dh1, "tn", F32, "dw_out")
    d_a, d_b, d_gates = _merge_bwd(dmerged, br_a, br_b, gates)
    do_mla = _mm(d_a, W_a, "nt", BF16, "d_o_mla")
    dW_a = _mm(o_mla, d_a, "tn", F32, "dw_branch_mla")
    do_hgrn = _mm(d_b, W_b, "nt", F32, "d_o_hgrn")
    dW_b = _mm(o_hgrn, d_b, "tn", F32, "dw_branch_hgrn")
    names = ["w_in", "w_q_up", "w_kv_up", "w_branch_mla", "w_branch_hgrn", "w_out", "w_ffn_in", "w_ffn_out"]

    def chip_partials(grads, by_cols, nms, tag):
        keep = [_shard_row_half(g, bc, ic) for g, bc in zip(grads, by_cols)]
        give = [_shard_row_half(g, bc, 1 - ic) for g, bc in zip(grads, by_cols)]
        taken = _to_sibling(give, "pair_exchange_" + tag)
        return [_pair_sum(a, b, "pair_sum_" + nm) for a, b, nm in zip(keep, taken, nms)]

    late_parts = chip_partials([dW_a, dW_b, dW_o, dW_fi, dW_fo], col_sharded[3:], names[3:], "late")
    dhq, dhf, dhi, dhg, dgn_p, dlb_p = _hgrn_bwd(hp, lb_raw, g_hgrn_norm, do_hgrn, o_rec, s_hist, nh)
    dqc, dkc, dvv, late_recv = _attn_bwd(qc, kc, vv, do_mla, lse, _attn_delta(o_mla, do_mla, heads), heads,
                                         _scatter_rider(late_parts))
    dqp, dkv, dkr = _rope_bwd(dqc, dkc, dvv, cos, sin, heads)
    dqn = _mm(dqp, W_qp, "nt", F32, "d_qn")
    dW_qp = _mm(qn, dqp, "tn", F32, "dw_q_up")
    dkvn = _mm(dkv, W_kv, "nt", F32, "d_kvn")
    dW_kv = _mm(kvn, dkv, "tn", F32, "dw_kv_up")
    dq_lat, dg_q = _rmsnorm_bwd(dqn, lat, g_q_norm, "norm_q_bwd", BF16, col0=0)
    dkv_lat, dg_kv = _rmsnorm_bwd(dkvn, lat, g_kv_norm, "norm_kv_bwd", BF16, col0=QL)
    dlat = jnp.concatenate([dq_lat, dkv_lat], axis=1)
    dhp = jnp.concatenate([dhq, dhf, dhi, dhg], axis=1)
    du1 = _mm(dlat, W_lat, "nt", F32, "d_u1_lat")
    du1 = _mm(dhp, W_H, "nt", F32, "d_u1_hgrn", res=du1)
    du1 = _mm(d_gates, W_G, "nt", F32, "d_u1_gates", res=du1)
    du1 = _mm(dkr, W_kr, "nt", F32, "d_u1_krope", res=du1)
    dW_lat = _mm(u1, dlat, "tn", F32, "dw_in_lat")
    dW_H = _mm(u1, dhp, "tn", F32, "dw_in_hgrn")
    dW_G = _mm(u1, d_gates, "tn", F32, "dw_in_gates")
    dW_kr = _mm(u1, dkr, "tn", F32, "dw_in_krope")
    dh0, dg_mix = _rmsnorm_bwd(du1, h0, g_mix_norm, "norm_mix_bwd", F32, res=dh1)
    grad_x = dh0[PREFIX:][None]

    dW_in = jnp.concatenate([dW_lat, dW_kr[:, :ROPE], dW_H, dW_G], axis=1)
    dW_q = dW_qp.reshape(QL, heads, QPAD)[:, :, :QK_HEAD].reshape(QL, heads * QK_HEAD)
    early_parts = chip_partials([dW_in, dW_q, dW_kv], col_sharded[:3], names[:3], "early")
    received = list(_run_rider(_scatter_rider(early_parts), "scatter_early")) + list(late_recv)
    halves = [_sum4(r, "sum_" + nm) for r, nm in zip(received, names)]
    others = _to_sibling(halves, "swap_halves")
    big_m = [m_w_in, m_w_q_up, m_w_kv_up, m_w_branch_mla, m_w_branch_hgrn, m_w_out, m_w_ffn_in, m_w_ffn_out]
    big_v = [v_w_in, v_w_q_up, v_w_kv_up, v_w_branch_mla, v_w_branch_hgrn, v_w_out, v_w_ffn_in, v_w_ffn_out]
    big_out = {}
    for nm, mine, other, w, m, v in zip(names, halves, others, big, big_m, big_v):
        g, d, mn, vn = _adamw_big(mine, other, w[0], m[0], v[0], "adamw_" + nm)
        big_out[nm] = (g[None], d[None], mn[None], vn[None])

    pieces = [loss_p[:, :1], dg_mix, dg_q, dg_kv, jnp.sum(dgn_p[:, 0, :], axis=0, keepdims=True), dg_ffn, dg_final,
              dlb_p[:, 0, :].reshape(1, D), dcb, dcw[0:3].reshape(1, 3 * F), dh0[PAD_LEN:PREFIX].reshape(1, N_META * D)]
    sizes = [p.shape[1] for p in pieces]
    flat = jnp.concatenate(pieces, axis=1)[0]
    rows = -(-flat.shape[0] // 1024) * 8
    packed = jnp.pad(flat, (0, rows * 128 - flat.shape[0])).reshape(rows, 128)
    total = _allreduce_small(packed).reshape(-1)
    offs = [0]
    for s in sizes:
        offs.append(offs[-1] + s)
    loss, g_mix, g_q, g_kv, g_hg, g_ffn, g_fin, dlb, g_cb, g_cw, g_meta = [
        total[offs[t]:offs[t + 1]].reshape(1, sizes[t]) for t in range(len(sizes))]
    g_cw = lax.dynamic_slice_in_dim(g_cw.reshape(3, F), chip * (F // 4), F // 4, axis=1)
    g_meta = lax.dynamic_slice_in_dim(g_meta.reshape(N_META, D), chip * (D // 4), D // 4, axis=1)
    items = [(g_meta, meta_tokens, m_meta_tokens, v_meta_tokens),
             (g_cw, conv_w[0], m_conv_w[0], v_conv_w[0]),
             (g_cb, conv_b, m_conv_b, v_conv_b),
             (g_mix, g_mix_norm, m_g_mix_norm, v_g_mix_norm),
             (g_q, g_q_norm, m_g_q_norm, v_g_q_norm),
             (g_kv, g_kv_norm, m_g_kv_norm, v_g_kv_norm),
             (g_hg, g_hgrn_norm, m_g_hgrn_norm, v_g_hgrn_norm),
             (g_ffn, g_ffn_norm, m_g_ffn_norm, v_g_ffn_norm),
             (g_fin, g_final_norm.reshape(1, D), m_g_final_norm.reshape(1, D), v_g_final_norm.reshape(1, D))]
    g_lb, s_delta, s_m, s_v = _adamw_small(items, (lb_raw, m_lb_raw, v_lb_raw), dlb)
    s_grads = [it[0] for it in items] + [g_lb]

    def shape_small(vals):
        meta, cw, cb, mix, q, kvg, hg, ffn, fin, lb = vals
        return [meta, cw[None], cb, mix, q, kvg, hg, ffn, fin.reshape(D), lb]

    s_grads, s_delta, s_m, s_v = [shape_small(v) for v in (s_grads, s_delta, s_m, s_v)]

    def ordered(kind, small):
        bigs = [big_out[nm][kind] for nm in names]
        return [small[0]] + bigs + small[1:]

    return (loss.reshape(()), grad_x, *ordered(0, s_grads), *ordered(1, s_delta), *ordered(2, s_m), *ordered(3, s_v))
```

```python
import math

import jax
import jax.numpy as jnp
from jax import lax
from jax.experimental import pallas as pl
from jax.experimental.pallas import tpu as pltpu

F32 = jnp.float32
BF16 = jnp.bfloat16
MESH = pl.DeviceIdType.MESH

NORM_EPS = 1e-6
N_META = 16
PREFIX = 128
PAD_LEN = PREFIX - N_META
HEAD = 128
ROPE = 64
QK_HEAD = HEAD + ROPE
QPAD = 2 * HEAD
SOFTMAX_SCALE = QK_HEAD ** -0.5
ROPE_THETA = 10000.0
CHUNK = 128
HALO = 16
LEVELS = 7
HGRN_FWD_HEADS = 4
HGRN_BWD_HEADS = 4
ATTN_FWD_HEADS = 2
ATTN_BWD_HEADS = 2
NEG = -1e30

ADAM_LR = 0.001
ADAM_B1 = 0.9
ADAM_B2 = 0.999
ADAM_EPS = 1e-08
ADAM_WD = 0.01
ADAM_STEP = 10

VMEM_LIMIT_BYTES = 48 * 1024 * 1024


def _params(*sem):
    return pltpu.CompilerParams(dimension_semantics=sem, vmem_limit_bytes=VMEM_LIMIT_BYTES)


def _tile(n, prefs):
    for p in prefs:
        if n % p == 0:
            return p
    return n


ROW_TILES = (640, 512, 256, 128, 64, 32, 16, 8)
TN_ROW_TILES = (1024, 1408, 1536, 512, 256, 128)
TN_K_TILES = (1664, 640, 512, 256, 128)
COL_TILES = (1024, 512, 256, 128)
K_TILES = (2048, 1536, 1408, 1024, 512, 256, 128)


def _sigmoid(x):
    return 1.0 / (1.0 + jnp.exp(-x))


def _dot(a, b, dims):
    return lax.dot_general(a.astype(BF16), b.astype(BF16), (dims, ((), ())), preferred_element_type=F32)


NN = ((1,), (0,))
TN = ((0,), (0,))
NT = ((1,), (1,))


def _split3(x):
    hi = x.astype(BF16)
    r = x - hi.astype(F32)
    mid = r.astype(BF16)
    lo = (r - mid.astype(F32)).astype(BF16)
    return hi, mid, lo


def _dot_exact_rhs(sel, x, dims):
    hi, mid, lo = _split3(x)
    return _dot(sel, hi, dims) + _dot(sel, mid, dims) + _dot(sel, lo, dims)


def _mm(a, b, mode, out_dtype, name, res=None, rider=None):
    if mode == "nn":
        (M, K), (K2, N) = a.shape, b.shape
    elif mode == "tn":
        (K, M), (K2, N) = a.shape, b.shape
    else:
        (M, K), (N, K2) = a.shape, b.shape
    assert K == K2, (name, a.shape, b.shape)
    tm = _tile(M, TN_ROW_TILES if mode == "tn" else ROW_TILES)
    tn = _tile(N, COL_TILES)
    tk = _tile(K, TN_K_TILES if mode == "tn" else K_TILES)
    nk = K // tk
    dims = {"nn": NN, "tn": TN, "nt": NT}[mode]

    n_rin = 0 if rider is None else len(rider.operands)
    n_rout = 0 if rider is None else len(rider.out_shapes)
    n_rsem = 0 if rider is None else len(rider.sem_shapes)
    grid = (M // tm, N // tn, nk)

    def body(*refs):
        main_in, r_ins, (o_ref,), r_outs, accs, r_sems = _split_refs(
            refs, (2 if res is None else 3, n_rin, 1, n_rout, 0 if nk == 1 else 1, n_rsem))
        a_ref, b_ref = main_in[:2]
        r_ref = None if res is None else main_in[2]
        i, j, k = pl.program_id(0), pl.program_id(1), pl.program_id(2)
        if rider is not None:
            pl.when((i == 0) & (j == 0) & (k == 0))(lambda: rider.start(r_ins, r_outs, r_sems))

        def finish(r):
            if r_ref is not None:
                r = r + r_ref[...].astype(F32)
            o_ref[...] = r.astype(out_dtype)

        if nk == 1:
            finish(_dot(a_ref[...], b_ref[...], dims))
        else:
            acc = accs[0]

            @pl.when(k == 0)
            def _():
                acc[...] = jnp.zeros_like(acc)

            acc[...] += _dot(a_ref[...], b_ref[...], dims)

            @pl.when(k == nk - 1)
            def _():
                finish(acc[...])

        if rider is not None:
            last = (i == grid[0] - 1) & (j == grid[1] - 1) & (k == nk - 1)
            pl.when(last)(lambda: rider.finish(r_ins, r_outs, r_sems))

    if mode == "nn":
        a_spec = pl.BlockSpec((tm, tk), lambda i, j, k: (i, k))
        b_spec = pl.BlockSpec((tk, tn), lambda i, j, k: (k, j))
    elif mode == "tn":
        a_spec = pl.BlockSpec((tk, tm), lambda i, j, k: (k, i))
        b_spec = pl.BlockSpec((tk, tn), lambda i, j, k: (k, j))
    else:
        a_spec = pl.BlockSpec((tm, tk), lambda i, j, k: (i, k))
        b_spec = pl.BlockSpec((tn, tk), lambda i, j, k: (j, k))
    in_specs = [a_spec, b_spec]
    operands = [a, b]
    if res is not None:
        in_specs.append(pl.BlockSpec((tm, tn), lambda i, j, k: (i, j)))
        operands.append(res)
    out_shape = jax.ShapeDtypeStruct((M, N), out_dtype)
    out_spec = pl.BlockSpec((tm, tn), lambda i, j, k: (i, j))
    scratch = [] if nk == 1 else [pltpu.VMEM((tm, tn), F32)]
    if rider is None:
        return pl.pallas_call(
            body, name=name, out_shape=out_shape, grid=grid, in_specs=in_specs, out_specs=out_spec,
            scratch_shapes=scratch, compiler_params=_params("parallel", "parallel", "arbitrary"),
        )(*operands)
    res_all = pl.pallas_call(
        body, name=name, out_shape=(out_shape, *rider.out_shapes), grid=grid,
        in_specs=in_specs + [ANY] * n_rin, out_specs=(out_spec, *([ANY] * n_rout)),
        scratch_shapes=scratch + rider.sem_shapes,
        compiler_params=pltpu.CompilerParams(dimension_semantics=("arbitrary", "arbitrary", "arbitrary"),
                                             vmem_limit_bytes=VMEM_LIMIT_BYTES, has_side_effects=True),
    )(*operands, *rider.operands)
    return res_all[0], res_all[1:]


def _rmsnorm_fwd(x, g, name, col0=0, width=None):
    L = x.shape[0]
    width = x.shape[1] if width is None else width
    assert col0 % width == 0
    cb = col0 // width
    tm = _tile(L, (128, 64, 32, 16))

    def body(x_ref, g_ref, o_ref):
        xv = x_ref[...]
        r = lax.rsqrt(jnp.mean(xv * xv, axis=-1, keepdims=True) + NORM_EPS)
        o_ref[...] = ((xv * r) * g_ref[...]).astype(BF16)

    return pl.pallas_call(
        body, name=name,
        out_shape=jax.ShapeDtypeStruct((L, width), BF16),
        grid=(L // tm,),
        in_specs=[pl.BlockSpec((tm, width), lambda i: (i, cb)), pl.BlockSpec((1, width), lambda i: (0, 0))],
        out_specs=pl.BlockSpec((tm, width), lambda i: (i, 0)),
        compiler_params=_params("parallel"),
    )(x, g)


def _rmsnorm_bwd(dy, x, g, name, out_dtype, col0=0, res=None):
    L, width = dy.shape
    assert col0 % width == 0
    cb = col0 // width
    tm = _tile(L, (128, 64, 32, 16))

    def body(*refs):
        if res is None:
            dy_ref, x_ref, g_ref, dx_ref, dg_ref = refs
            r_ref = None
        else:
            dy_ref, x_ref, g_ref, r_ref, dx_ref, dg_ref = refs

        @pl.when(pl.program_id(0) == 0)
        def _():
            dg_ref[...] = jnp.zeros_like(dg_ref)

        xv = x_ref[...]
        dyv = dy_ref[...].astype(F32)
        r = lax.rsqrt(jnp.mean(xv * xv, axis=-1, keepdims=True) + NORM_EPS)
        z = dyv * g_ref[...]
        dx = r * z - xv * ((r * r * r) * jnp.mean(z * xv, axis=-1, keepdims=True))
        if r_ref is not None:
            dx = dx + r_ref[...]
        dx_ref[...] = dx.astype(out_dtype)
        dg_ref[...] += jnp.sum(dyv * (xv * r), axis=0, keepdims=True)

    in_specs = [pl.BlockSpec((tm, width), lambda i: (i, 0)),
                pl.BlockSpec((tm, width), lambda i: (i, cb)),
                pl.BlockSpec((1, width), lambda i: (0, 0))]
    operands = [dy, x, g]
    if res is not None:
        in_specs.append(pl.BlockSpec((tm, width), lambda i: (i, 0)))
        operands.append(res)
    return pl.pallas_call(
        body, name=name,
        out_shape=(jax.ShapeDtypeStruct((L, width), out_dtype), jax.ShapeDtypeStruct((1, width), F32)),
        grid=(L // tm,),
        in_specs=in_specs,
        out_specs=(pl.BlockSpec((tm, width), lambda i: (i, 0)), pl.BlockSpec((1, width), lambda i: (0, 0))),
        compiler_params=_params("arbitrary"),
    )(*operands)


def _final_loss_bwd(h2, tgt, g):
    L, D = h2.shape
    tm = PREFIX
    inv_d = 1.0 / D

    def body(h_ref, t_ref, g_ref, dh_ref, loss_ref, dg_ref):
        i = pl.program_id(0)

        @pl.when(i == 0)
        def _():
            loss_ref[...] = jnp.zeros_like(loss_ref)
            dg_ref[...] = jnp.zeros_like(dg_ref)

        xv = h_ref[...]
        r = lax.rsqrt(jnp.mean(xv * xv, axis=-1, keepdims=True) + NORM_EPS)
        xn = xv * r
        y = xn * g_ref[...]
        real = (i >= PREFIX // tm).astype(F32)
        diff = (y - t_ref[...]) * real
        loss_ref[...] += 0.5 * inv_d * jnp.sum(diff * diff)
        dyv = diff * inv_d
        z = dyv * g_ref[...]
        dh_ref[...] = r * z - xv * ((r * r * r) * jnp.mean(z * xv, axis=-1, keepdims=True))
        dg_ref[...] += jnp.sum(dyv * xn, axis=0, keepdims=True)

    shift = PREFIX // tm
    return pl.pallas_call(
        body, name="final_loss_bwd",
        out_shape=(jax.ShapeDtypeStruct((L, D), F32), jax.ShapeDtypeStruct((1, 128), F32),
                   jax.ShapeDtypeStruct((1, D), F32)),
        grid=(L // tm,),
        in_specs=[pl.BlockSpec((tm, D), lambda i: (i, 0)),
                  pl.BlockSpec((tm, D), lambda i: (jnp.maximum(i - shift, 0), 0)),
                  pl.BlockSpec((1, D), lambda i: (0, 0))],
        out_specs=(pl.BlockSpec((tm, D), lambda i: (i, 0)), pl.BlockSpec((1, 128), lambda i: (0, 0)),
                   pl.BlockSpec((1, D), lambda i: (0, 0))),
        compiler_params=_params("arbitrary"),
    )(h2, tgt, g)


def _rot_half(x):
    lane = lax.broadcasted_iota(jnp.int32, x.shape, 1)
    return jnp.where(lane < ROPE // 2, -pltpu.roll(x, 128 - ROPE // 2, 1), pltpu.roll(x, ROPE // 2, 1))


def _rope_fwd(qp, kv, kr, cos, sin, heads):
    L = qp.shape[0]
    tm = _tile(L, (128,))

    def body(q_ref, kv_ref, kr_ref, c_ref, s_ref, qc_ref, kc_ref, v_ref):
        c, s = c_ref[...], s_ref[...]
        krv = kr_ref[...]
        kr_rot = (krv * c + _rot_half(krv) * s).astype(BF16)
        for h in range(heads):
            lo = h * QPAD
            qc_ref[:, lo:lo + HEAD] = (q_ref[:, lo:lo + HEAD].astype(F32) * SOFTMAX_SCALE).astype(BF16)
            qr = q_ref[:, lo + HEAD:lo + QPAD].astype(F32)
            qc_ref[:, lo + HEAD:lo + QPAD] = ((qr * c + _rot_half(qr) * s) * SOFTMAX_SCALE).astype(BF16)
            kc_ref[:, lo:lo + HEAD] = kv_ref[:, lo:lo + HEAD].astype(BF16)
            kc_ref[:, lo + HEAD:lo + QPAD] = kr_rot
            v_ref[:, h * HEAD:(h + 1) * HEAD] = kv_ref[:, lo + HEAD:lo + QPAD].astype(BF16)

    W = heads * QPAD
    row = lambda w: pl.BlockSpec((tm, w), lambda i: (i, 0))
    return pl.pallas_call(
        body, name="rope_fwd",
        out_shape=(jax.ShapeDtypeStruct((L, W), BF16), jax.ShapeDtypeStruct((L, W), BF16),
                   jax.ShapeDtypeStruct((L, heads * HEAD), BF16)),
        grid=(L // tm,),
        in_specs=[row(W), row(W), row(128), row(128), row(128)],
        out_specs=(row(W), row(W), row(heads * HEAD)),
        compiler_params=_params("parallel"),
    )(qp, kv, kr, cos, sin)


def _rope_bwd(dqc, dkc, dv, cos, sin, heads):
    L = dqc.shape[0]
    tm = _tile(L, (128,))

    def body(dq_ref, dk_ref, dv_ref, c_ref, s_ref, dqp_ref, dkv_ref, dkr_ref):
        c, s = c_ref[...], s_ref[...]
        acc = jnp.zeros((tm, 128), F32)
        for h in range(heads):
            lo = h * QPAD
            dqp_ref[:, lo:lo + HEAD] = (dq_ref[:, lo:lo + HEAD] * SOFTMAX_SCALE).astype(BF16)
            d = dq_ref[:, lo + HEAD:lo + QPAD]
            dqp_ref[:, lo + HEAD:lo + QPAD] = ((d * c - _rot_half(d) * s) * SOFTMAX_SCALE).astype(BF16)
            dkv_ref[:, lo:lo + HEAD] = dk_ref[:, lo:lo + HEAD].astype(BF16)
            dkv_ref[:, lo + HEAD:lo + QPAD] = dv_ref[:, h * HEAD:(h + 1) * HEAD].astype(BF16)
            acc = acc + dk_ref[:, lo + HEAD:lo + QPAD]
        dkr_ref[...] = (acc * c - _rot_half(acc) * s).astype(BF16)

    W = heads * QPAD
    row = lambda w: pl.BlockSpec((tm, w), lambda i: (i, 0))
    return pl.pallas_call(
        body, name="rope_bwd",
        out_shape=(jax.ShapeDtypeStruct((L, W), BF16), jax.ShapeDtypeStruct((L, W), BF16),
                   jax.ShapeDtypeStruct((L, 128), BF16)),
        grid=(L // tm,),
        in_specs=[row(W), row(W), row(heads * HEAD), row(128), row(128)],
        out_specs=(row(W), row(W), row(128)),
        compiler_params=_params("parallel"),
    )(dqc, dkc, dv, cos, sin)


def _attn_keep(qi, ki, ta):
    t = qi * ta + lax.broadcasted_iota(jnp.int32, (ta, ta), 0)
    s = ki * ta + lax.broadcasted_iota(jnp.int32, (ta, ta), 1)
    return (s <= t) & ((s >= PAD_LEN) | (s == t))


def _split_refs(refs, sizes):
    out, at = [], 0
    for n in sizes:
        out.append(refs[at:at + n])
        at += n
    return out


def _attn_fwd(qc, kc, v, heads, rider):
    L = qc.shape[0]
    ta = _tile(L, (640, 128))
    nb = L // ta
    pairs = [(i, j) for i in range(nb) for j in range(i + 1)]
    q_of = jnp.asarray([p[0] for p in pairs], jnp.int32)
    k_of = jnp.asarray([p[1] for p in pairs], jnp.int32)
    n_rin, n_rout = len(rider.operands), len(rider.out_shapes)
    per = _tile(heads, (ATTN_FWD_HEADS, 1))
    ng = heads // per

    def body(*refs):
        (q_of_ref, k_of_ref, q_ref, k_ref, v_ref), r_ins, (o_ref, lse_ref), r_outs, (m_sc, l_sc, acc_sc), r_sems = \
            _split_refs(refs, (5, n_rin, 2, n_rout, 3, len(rider.sem_shapes)))
        h, t = pl.program_id(0), pl.program_id(1)
        qi, ki = q_of_ref[t], k_of_ref[t]
        pl.when((h == 0) & (t == 0))(lambda: rider.start(r_ins, r_outs, r_sems))

        @pl.when(ki == 0)
        def _():
            m_sc[...] = jnp.full_like(m_sc, NEG)
            l_sc[...] = jnp.zeros_like(l_sc)
            acc_sc[...] = jnp.zeros_like(acc_sc)

        def step(masked):
            wide = lambda hh: slice(hh * QPAD, (hh + 1) * QPAD)
            lanes = lambda hh: slice(hh * HEAD, (hh + 1) * HEAD)
            ss = [_dot(q_ref[:, wide(hh)], k_ref[:, wide(hh)], NT) for hh in range(per)]
            if masked:
                keep = _attn_keep(qi, ki, ta)
                ss = [jnp.where(keep, s, NEG) for s in ss]
            for hh in range(per):
                m_old = m_sc[hh]
                m_new = jnp.maximum(m_old, jnp.max(ss[hh], axis=-1, keepdims=True))
                p = jnp.exp(ss[hh] - jnp.tile(m_new, (1, ta // HEAD)))
                alpha = jnp.exp(m_old - m_new)
                l_sc[hh] = alpha * l_sc[hh] + jnp.sum(p, axis=-1, keepdims=True)
                acc_sc[hh] = alpha * acc_sc[hh] + _dot(p, v_ref[:, lanes(hh)], NN)
                m_sc[hh] = m_new

        pl.when((ki == qi) | ((ki == 0) & (qi > 0)))(lambda: step(True))
        pl.when((ki > 0) & (ki < qi))(lambda: step(False))

        @pl.when(ki == qi)
        def _():
            for hh in range(per):
                l = l_sc[hh]
                o_ref[:, hh * HEAD:(hh + 1) * HEAD] = (acc_sc[hh] / l).astype(BF16)
                lse_ref[:, hh * HEAD:(hh + 1) * HEAD] = m_sc[hh] + jnp.log(l)

        pl.when((h == ng - 1) & (t == len(pairs) - 1))(lambda: rider.finish(r_ins, r_outs, r_sems))

    qrow = lambda w: pl.BlockSpec((ta, per * w), lambda h, t, q_of, k_of: (q_of[t], h))
    krow = lambda w: pl.BlockSpec((ta, per * w), lambda h, t, q_of, k_of: (k_of[t], h))
    stat = pltpu.VMEM((per, ta, HEAD), F32)
    res = pl.pallas_call(
        body, name="attn_fwd",
        out_shape=(jax.ShapeDtypeStruct((L, heads * HEAD), BF16), jax.ShapeDtypeStruct((L, heads * HEAD), F32),
                   *rider.out_shapes),
        grid_spec=pltpu.PrefetchScalarGridSpec(
            num_scalar_prefetch=2, grid=(ng, len(pairs)),
            in_specs=[qrow(QPAD), krow(QPAD), krow(HEAD)] + [ANY] * n_rin,
            out_specs=(qrow(HEAD), qrow(HEAD), *([ANY] * n_rout)),
            scratch_shapes=[stat, stat, stat] + rider.sem_shapes),
        compiler_params=pltpu.CompilerParams(dimension_semantics=("arbitrary", "arbitrary"),
                                             vmem_limit_bytes=VMEM_LIMIT_BYTES, has_side_effects=True),
    )(q_of, k_of, qc, kc, v, *rider.operands)
    return res[0], res[1], res[2:]


def _attn_delta(o, do, heads):
    L = o.shape[0]
    tm = _tile(L, (128,))

    def body(o_ref, do_ref, d_ref):
        for h in range(heads):
            cols = slice(h * HEAD, (h + 1) * HEAD)
            d = jnp.sum(do_ref[:, cols].astype(F32) * o_ref[:, cols].astype(F32), axis=-1, keepdims=True)
            d_ref[:, cols] = jnp.broadcast_to(d, (tm, HEAD))

    row = pl.BlockSpec((tm, heads * HEAD), lambda i: (i, 0))
    return pl.pallas_call(
        body, name="attn_delta", out_shape=jax.ShapeDtypeStruct((L, heads * HEAD), F32), grid=(L // tm,),
        in_specs=[row, row], out_specs=row, compiler_params=_params("parallel"),
    )(o, do)


def _attn_bwd(qc, kc, v, do, lse, delta, heads, rider):
    L = qc.shape[0]
    ta = _tile(L, (640, 128))
    nb = L // ta
    pairs = [(j, i) for j in range(nb) for i in range(j, nb)]
    k_of = jnp.asarray([p[0] for p in pairs], jnp.int32)
    q_of = jnp.asarray([p[1] for p in pairs], jnp.int32)
    n_rin, n_rout = len(rider.operands), len(rider.out_shapes)
    per = _tile(heads, (ATTN_BWD_HEADS, 1))
    ng = heads // per

    def body(*refs):
        ((k_of_ref, q_of_ref, q_ref, k_ref, v_ref, do_ref, lse_ref, delta_ref), r_ins, (dq_hbm, dk_ref, dv_ref),
         r_outs, (dq_sc, dq_sem), r_sems) = _split_refs(refs, (8, n_rin, 3, n_rout, 2, len(rider.sem_shapes)))
        h, t = pl.program_id(0), pl.program_id(1)
        kj, qi = k_of_ref[t], q_of_ref[t]
        pl.when((h == 0) & (t == 0))(lambda: rider.start(r_ins, r_outs, r_sems))

        @pl.when(t == 0)
        def _():
            dq_sc[...] = jnp.zeros_like(dq_sc)

        @pl.when(qi == kj)
        def _():
            dk_ref[...] = jnp.zeros_like(dk_ref)
            dv_ref[...] = jnp.zeros_like(dv_ref)

        def step(masked):
            wide = lambda hh: slice(hh * QPAD, (hh + 1) * QPAD)
            lanes = lambda hh: slice(hh * HEAD, (hh + 1) * HEAD)
            rep = (1, ta // HEAD)
            rows = pl.ds(pl.multiple_of(qi * ta, ta), ta)
            ss = [_dot(q_ref[:, wide(hh)], k_ref[:, wide(hh)], NT) for hh in range(per)]
            dps = [_dot(do_ref[:, lanes(hh)], v_ref[:, lanes(hh)], NT) for hh in range(per)]
            ps = [jnp.exp(ss[hh] - jnp.tile(lse_ref[:, lanes(hh)], rep)) for hh in range(per)]
            if masked:
                keep = _attn_keep(qi, kj, ta)
                ps = [jnp.where(keep, p, 0.0) for p in ps]
            dss = [ps[hh] * (dps[hh] - jnp.tile(delta_ref[:, lanes(hh)], rep)) for hh in range(per)]
            for hh in range(per):
                dv_ref[:, lanes(hh)] += _dot(ps[hh], do_ref[:, lanes(hh)], TN)
                dk_ref[:, wide(hh)] += _dot(dss[hh], q_ref[:, wide(hh)], TN)
                dq_sc[rows, wide(hh)] += _dot(dss[hh], k_ref[:, wide(hh)], NN)

        pl.when((qi == kj) | ((kj == 0) & (qi > 0)))(lambda: step(True))
        pl.when((kj > 0) & (qi > kj))(lambda: step(False))

        @pl.when(t == len(pairs) - 1)
        def _():
            cols = pl.ds(pl.multiple_of(h * (per * QPAD), per * QPAD), per * QPAD)
            out = pltpu.make_async_copy(dq_sc, dq_hbm.at[:, cols], dq_sem)
            out.start()
            out.wait()

        pl.when((h == ng - 1) & (t == len(pairs) - 1))(lambda: rider.finish(r_ins, r_outs, r_sems))

    qrow = lambda w: pl.BlockSpec((ta, per * w), lambda h, t, k_of, q_of: (q_of[t], h))
    krow = lambda w: pl.BlockSpec((ta, per * w), lambda h, t, k_of, q_of: (k_of[t], h))
    res = pl.pallas_call(
        body, name="attn_bwd",
        out_shape=(jax.ShapeDtypeStruct((L, heads * QPAD), F32), jax.ShapeDtypeStruct((L, heads * QPAD), F32),
                   jax.ShapeDtypeStruct((L, heads * HEAD), F32), *rider.out_shapes),
        grid_spec=pltpu.PrefetchScalarGridSpec(
            num_scalar_prefetch=2, grid=(ng, len(pairs)),
            in_specs=[qrow(QPAD), krow(QPAD), krow(HEAD), qrow(HEAD), qrow(HEAD), qrow(HEAD)] + [ANY] * n_rin,
            out_specs=(ANY, krow(QPAD), krow(HEAD), *([ANY] * n_rout)),
            scratch_shapes=[pltpu.VMEM((L, per * QPAD), F32), pltpu.SemaphoreType.DMA] + rider.sem_shapes),
        compiler_params=pltpu.CompilerParams(dimension_semantics=("arbitrary", "arbitrary"),
                                             vmem_limit_bytes=VMEM_LIMIT_BYTES, has_side_effects=True),
    )(k_of, q_of, qc, kc, v, do, lse, delta, *rider.operands)
    return res[0], res[1], res[2], res[3:]


def _hgrn_constants():
    t = jnp.arange(CHUNK)
    tril = (t[None, :] <= t[:, None]).astype(BF16)
    sel = []
    for lv in range(LEVELS):
        hs = 1 << lv
        mid = (t // (2 * hs)) * (2 * hs) + hs - 1
        sel.append((t[None, :] == mid[:, None]).astype(BF16))
    return tril, jnp.concatenate(sel, axis=0)


def _hgrn_decay_grad_constants():
    r = jnp.arange(CHUNK)[:, None]
    c = jnp.arange(CHUNK)[None, :]
    mats = []
    for lv in range(LEVELS):
        same = (r >> (lv + 1)) == (c >> (lv + 1))
        second = ((r >> lv) & 1) == 1
        mats.append(same & jnp.where(second, c >= r, c < r))
    mats += [c >= r, c < r]
    return jnp.concatenate(mats, axis=1).astype(BF16)


def _hgrn_gates(hq_ref, hf_ref, hi_ref, lb_ref, cols, row0):
    rows = row0 + lax.broadcasted_iota(jnp.int32, (CHUNK, 1), 0)
    valid = rows >= PAD_LEN
    lb = 1.0 / (1.0 + jnp.exp(lb_ref[1:2, cols] - lb_ref[0:1, cols]))
    hq = hq_ref[:, cols]
    sq = _sigmoid(hq)
    sg = _sigmoid(hf_ref[:, cols])
    f = lb + (1.0 - lb) * sg
    g = jnp.where(valid, jnp.log(f), 0.0)
    k = jnp.where(valid, 1.0 - f, 0.0)
    return dict(q=hq * sq, sq=sq, hq=hq, k=k, v=hi_ref[:, cols], g=g, f=f, sg=sg, lb=lb, valid=valid)


def _hgrn_prefix(tril_ref, sel_ref, gs):
    n = len(gs)
    b_all = _dot_exact_rhs(tril_ref[...], jnp.concatenate(gs, axis=1) if n > 1 else gs[0], NN)
    bm_all = _dot_exact_rhs(sel_ref[...], b_all, NN)
    cut = lambda a, i: a[:, i * HEAD:(i + 1) * HEAD]
    return [cut(b_all, i) for i in range(n)], [cut(bm_all, i) for i in range(n)]


def _hgrn_levels(q, k, b, bm_all):
    t = lax.broadcasted_iota(jnp.int32, (CHUNK, 1), 0)
    tt = lax.broadcasted_iota(jnp.int32, (CHUNK, CHUNK), 0)
    ss = lax.broadcasted_iota(jnp.int32, (CHUNK, CHUNK), 1)
    out = []
    for lv in range(LEVELS):
        bm = bm_all[lv * CHUNK:(lv + 1) * CHUNK, :]
        second = ((t >> lv) & 1) == 1
        eq = jnp.where(second, jnp.exp(jnp.minimum(b - bm, 0.0)), 0.0)
        ek = jnp.where(second, 0.0, jnp.exp(jnp.minimum(bm - b, 0.0)))
        same = (tt >> (lv + 1)) == (ss >> (lv + 1))
        out.append((eq, ek, (q * eq).astype(BF16), (k * ek).astype(BF16), same))
    return out


def _hgrn_intra(qs, ks, levels):
    tt = lax.broadcasted_iota(jnp.int32, (CHUNK, CHUNK), 0)
    ss = lax.broadcasted_iota(jnp.int32, (CHUNK, CHUNK), 1)
    ps = [jnp.where(tt == ss, jnp.sum(q * k, axis=-1, keepdims=True), 0.0) for q, k in zip(qs, ks)]
    for lv in range(LEVELS):
        for i, lvl in enumerate(levels):
            _, _, ql, kl, same = lvl[lv]
            ps[i] = ps[i] + jnp.where(same, _dot(ql, kl, NT), 0.0)
    return ps


def _hgrn_fwd(hp, lb_raw, g_norm, nh):
    L = hp.shape[0]
    D = nh * HEAD
    nc = L // CHUNK
    per = _tile(nh, (HGRN_FWD_HEADS, 2, 1))
    ng = nh // per
    tril, sel = _hgrn_constants()

    def body(hq_ref, hf_ref, hi_ref, hg_ref, lb_ref, gn_ref, tril_ref, sel_ref,
             oh_ref, orec_ref, shist_ref, s_sc, b_sc):
        c = pl.program_id(1)

        @pl.when(c == 0)
        def _():
            s_sc[...] = jnp.zeros_like(s_sc)

        heads_here = range(per)
        lanes = [slice(hh * HEAD, (hh + 1) * HEAD) for hh in heads_here]
        ws = [_hgrn_gates(hq_ref, hf_ref, hi_ref, lb_ref, lanes[hh], c * CHUNK) for hh in heads_here]
        qs, ks, vs = [w["q"] for w in ws], [w["k"] for w in ws], [w["v"] for w in ws]
        bs, bms = _hgrn_prefix(tril_ref, sel_ref, [w["g"] for w in ws])
        for hh in heads_here:
            b_sc[hh] = bs[hh]
        b_lasts = [b_sc[hh, CHUNK - 1:CHUNK, :] for hh in heads_here]
        ps = _hgrn_intra(qs, ks, [_hgrn_levels(qs[hh], ks[hh], bs[hh], bms[hh]) for hh in heads_here])
        s_ins = [s_sc[hh] for hh in heads_here]
        os_ = [_dot(ps[hh], vs[hh], NN) + _dot(qs[hh] * jnp.exp(bs[hh]), s_ins[hh], NT) for hh in heads_here]
        for hh in heads_here:
            shist_ref[0, hh] = s_ins[hh]
            s_sc[hh] = (jnp.exp(b_lasts[hh]) * s_ins[hh]
                        + _dot(vs[hh], ks[hh] * jnp.exp(b_lasts[hh] - bs[hh]), TN))
        for hh in heads_here:
            o = os_[hh]
            orec_ref[:, lanes[hh]] = o
            rn = lax.rsqrt(jnp.mean(o * o, axis=-1, keepdims=True) + NORM_EPS)
            hg = hg_ref[:, lanes[hh]]
            oh_ref[:, lanes[hh]] = (((o * rn) * gn_ref[...]) * (hg * _sigmoid(hg))).astype(BF16)

    col = lambda grp: pl.BlockSpec((CHUNK, per * HEAD), lambda h, c: (c, grp * ng + h))
    const = lambda shape: pl.BlockSpec(shape, lambda h, c: (0, 0))
    return pl.pallas_call(
        body, name="hgrn_fwd",
        out_shape=(jax.ShapeDtypeStruct((L, D), BF16), jax.ShapeDtypeStruct((L, D), F32),
                   jax.ShapeDtypeStruct((nc, nh, HEAD, HEAD), F32)),
        grid=(ng, nc),
        in_specs=[col(0), col(1), col(2), col(3),
                  pl.BlockSpec((2, per * HEAD), lambda h, c: (0, h)), const((1, HEAD)),
                  const((CHUNK, CHUNK)), const((LEVELS * CHUNK, CHUNK))],
        out_specs=(pl.BlockSpec((CHUNK, per * HEAD), lambda h, c: (c, h)),
                   pl.BlockSpec((CHUNK, per * HEAD), lambda h, c: (c, h)),
                   pl.BlockSpec((1, per, HEAD, HEAD), lambda h, c: (c, h, 0, 0))),
        scratch_shapes=[pltpu.VMEM((per, HEAD, HEAD), F32), pltpu.VMEM((per, CHUNK, HEAD), F32)],
        compiler_params=_params("parallel", "arbitrary"),
    )(hp, hp, hp, hp, lb_raw, g_norm, tril, sel)


def _hgrn_bwd(hp, lb_raw, g_norm, do_h, o_rec, s_hist, nh):
    L = hp.shape[0]
    D = nh * HEAD
    nc = L // CHUNK
    per = _tile(nh, (HGRN_BWD_HEADS, 1))
    ng = nh // per
    tril, sel = _hgrn_constants()
    tdec = _hgrn_decay_grad_constants()

    def body(hq_ref, hf_ref, hi_ref, hg_ref, lb_ref, gn_ref, tril_ref, sel_ref, tdec_ref, do_ref, orec_ref, shist_ref,
             dhq_ref, dhf_ref, dhi_ref, dhg_ref, dgn_ref, dlb_ref, ds_sc, b_sc):
        ci = pl.program_id(1)
        c = nc - 1 - ci

        @pl.when(ci == 0)
        def _():
            ds_sc[...] = jnp.zeros_like(ds_sc)
            dgn_ref[...] = jnp.zeros_like(dgn_ref)
            dlb_ref[...] = jnp.zeros_like(dlb_ref)

        heads_here = range(per)
        lanes = [slice(hh * HEAD, (hh + 1) * HEAD) for hh in heads_here]
        ws = [_hgrn_gates(hq_ref, hf_ref, hi_ref, lb_ref, lanes[hh], c * CHUNK) for hh in heads_here]
        qs, ks, vs = [w["q"] for w in ws], [w["k"] for w in ws], [w["v"] for w in ws]
        bs, bms = _hgrn_prefix(tril_ref, sel_ref, [w["g"] for w in ws])
        for hh in heads_here:
            b_sc[hh] = bs[hh]
        b_lasts = [b_sc[hh, CHUNK - 1:CHUNK, :] for hh in heads_here]
        levels = [_hgrn_levels(qs[hh], ks[hh], bs[hh], bms[hh]) for hh in heads_here]
        ps = _hgrn_intra(qs, ks, levels)
        s_ins = [shist_ref[0, hh] for hh in heads_here]
        ds_outs = [ds_sc[hh] for hh in heads_here]
        ebs = [jnp.exp(b) for b in bs]
        etails = [jnp.exp(b_lasts[hh] - bs[hh]) for hh in heads_here]
        decays = [jnp.exp(bl) for bl in b_lasts]

        dos = []
        for hh in heads_here:
            o = orec_ref[:, lanes[hh]]
            hg = hg_ref[:, lanes[hh]]
            sgg = _sigmoid(hg)
            rn = lax.rsqrt(jnp.mean(o * o, axis=-1, keepdims=True) + NORM_EPS)
            on = o * rn
            doh = do_ref[:, lanes[hh]]
            dy = doh * (hg * sgg)
            dhg_ref[:, lanes[hh]] = (doh * (on * gn_ref[...]) * (sgg * (1.0 + hg * (1.0 - sgg)))).astype(BF16)
            dgn_ref[hh] += jnp.broadcast_to(jnp.sum(dy * on, axis=0, keepdims=True), (8, HEAD))
            z = dy * gn_ref[...]
            dos.append(rn * z - o * ((rn * rn * rn) * jnp.mean(z * o, axis=-1, keepdims=True)))

        tt = lax.broadcasted_iota(jnp.int32, (CHUNK, CHUNK), 0)
        ss = lax.broadcasted_iota(jnp.int32, (CHUNK, CHUNK), 1)
        dps = [jnp.where(ss <= tt, _dot(dos[hh], vs[hh], NT), 0.0) for hh in heads_here]
        dvs = [_dot(ps[hh], dos[hh], TN) + _dot(ks[hh] * etails[hh], ds_outs[hh], NT) for hh in heads_here]
        dq_states = [ebs[hh] * _dot(dos[hh], s_ins[hh], NN) for hh in heads_here]
        dk_states = [etails[hh] * _dot(vs[hh], ds_outs[hh], NN) for hh in heads_here]
        dpds = [jnp.sum(jnp.where(tt == ss, dp, 0.0), axis=-1, keepdims=True) for dp in dps]
        dqs = [dpds[hh] * ks[hh] + dq_states[hh] for hh in heads_here]
        dks = [dpds[hh] * qs[hh] + dk_states[hh] for hh in heads_here]
        pair_terms = [[] for _ in heads_here]
        for lv in range(LEVELS):
            for hh in heads_here:
                eq, ek, ql, kl, same = levels[hh][lv]
                dpl = jnp.where(same, dps[hh], 0.0)
                dq_l = eq * _dot(dpl, kl, NN)
                dk_l = ek * _dot(dpl, ql, TN)
                dqs[hh] = dqs[hh] + dq_l
                dks[hh] = dks[hh] + dk_l
                pair_terms[hh].append(qs[hh] * dq_l + ks[hh] * dk_l)
        for hh in heads_here:
            pair_terms[hh] += [qs[hh] * dq_states[hh], ks[hh] * dk_states[hh]]
            ds_sc[hh] = decays[hh] * ds_outs[hh] + _dot(dos[hh], qs[hh] * ebs[hh], TN)
        stacked = [jnp.concatenate(terms, axis=0) for terms in pair_terms]
        dg_all = _dot_exact_rhs(tdec_ref[...], jnp.concatenate(stacked, axis=1) if per > 1 else stacked[0], NN)

        for hh in heads_here:
            w = ws[hh]
            f, sg, lb, sq, hq = w["f"], w["sg"], w["lb"], w["sq"], w["hq"]
            through = jnp.sum((decays[hh] * s_ins[hh]) * ds_outs[hh], axis=0, keepdims=True)
            dg = dg_all[:, lanes[hh]] + through
            df = jnp.where(w["valid"], dg / f - dks[hh], 0.0)
            dhf_ref[:, lanes[hh]] = (df * (1.0 - lb) * sg * (1.0 - sg)).astype(BF16)
            dlb_ref[hh] += jnp.broadcast_to(jnp.sum(df * (1.0 - sg), axis=0, keepdims=True), (8, HEAD))
            dhq_ref[:, lanes[hh]] = (dqs[hh] * (sq * (1.0 + hq * (1.0 - sq)))).astype(BF16)
            dhi_ref[:, lanes[hh]] = dvs[hh].astype(BF16)

    col = lambda grp: pl.BlockSpec((CHUNK, per * HEAD), lambda h, c: (nc - 1 - c, grp * ng + h))
    const = lambda shape: pl.BlockSpec(shape, lambda h, c: (0, 0))
    tile = pl.BlockSpec((CHUNK, per * HEAD), lambda h, c: (nc - 1 - c, h))
    part = pl.BlockSpec((per, 8, HEAD), lambda h, c: (h, 0, 0))
    return pl.pallas_call(
        body, name="hgrn_bwd",
        out_shape=tuple([jax.ShapeDtypeStruct((L, D), BF16)] * 4 + [jax.ShapeDtypeStruct((nh, 8, HEAD), F32)] * 2),
        grid=(ng, nc),
        in_specs=[col(0), col(1), col(2), col(3),
                  pl.BlockSpec((2, per * HEAD), lambda h, c: (0, h)), const((1, HEAD)),
                  const((CHUNK, CHUNK)), const((LEVELS * CHUNK, CHUNK)), const((CHUNK, (LEVELS + 2) * CHUNK)),
                  tile, tile, pl.BlockSpec((1, per, HEAD, HEAD), lambda h, c: (nc - 1 - c, h, 0, 0))],
        out_specs=(tile, tile, tile, tile, part, part),
        scratch_shapes=[pltpu.VMEM((per, HEAD, HEAD), F32), pltpu.VMEM((per, CHUNK, HEAD), F32)],
        compiler_params=_params("parallel", "arbitrary"),
    )(hp, hp, hp, hp, lb_raw, g_norm, tril, sel, tdec, do_h, o_rec, s_hist)


def _merge_fwd(a, bm, gates):
    L, D = a.shape
    tm = _tile(L, (128,))

    def body(a_ref, b_ref, g_ref, o_ref):
        o_ref[...] = (_sigmoid(g_ref[:, :D]) * a_ref[...] + _sigmoid(g_ref[:, D:]) * b_ref[...]).astype(BF16)

    row = lambda w: pl.BlockSpec((tm, w), lambda i: (i, 0))
    return pl.pallas_call(
        body, name="merge_fwd", out_shape=jax.ShapeDtypeStruct((L, D), BF16), grid=(L // tm,),
        in_specs=[row(D), row(D), row(2 * D)], out_specs=row(D), compiler_params=_params("parallel"),
    )(a, bm, gates)


def _merge_bwd(dm, a, bm, gates):
    L, D = a.shape
    tm = _tile(L, (128,))

    def body(dm_ref, a_ref, b_ref, g_ref, da_ref, db_ref, dg_ref):
        d = dm_ref[...]
        sa, sb = _sigmoid(g_ref[:, :D]), _sigmoid(g_ref[:, D:])
        da_ref[...] = (d * sa).astype(BF16)
        db_ref[...] = (d * sb).astype(BF16)
        dg_ref[:, :D] = (d * a_ref[...] * sa * (1.0 - sa)).astype(BF16)
        dg_ref[:, D:] = (d * b_ref[...] * sb * (1.0 - sb)).astype(BF16)

    row = lambda w: pl.BlockSpec((tm, w), lambda i: (i, 0))
    return pl.pallas_call(
        body, name="merge_bwd",
        out_shape=(jax.ShapeDtypeStruct((L, D), BF16), jax.ShapeDtypeStruct((L, D), BF16),
                   jax.ShapeDtypeStruct((L, 2 * D), BF16)),
        grid=(L // tm,),
        in_specs=[row(D), row(D), row(D), row(2 * D)], out_specs=(row(D), row(D), row(2 * D)),
        compiler_params=_params("parallel"),
    )(dm, a, bm, gates)


def _conv_taps(g_ref, halo_ref, i, tm):
    rows = i * tm + lax.broadcasted_iota(jnp.int32, (tm, 1), 0)
    g0 = jnp.where(rows >= PAD_LEN, g_ref[...].astype(F32), 0.0)
    sub = lax.broadcasted_iota(jnp.int32, (HALO, 1), 0)
    hrow = i * tm - HALO + sub
    halo = jnp.where(hrow >= PAD_LEN, halo_ref[...].astype(F32), 0.0)
    r = lax.broadcasted_iota(jnp.int32, (tm, 1), 0)
    h7 = jnp.sum(jnp.where(sub == HALO - 1, halo, 0.0), axis=0, keepdims=True)
    h6 = jnp.sum(jnp.where(sub == HALO - 2, halo, 0.0), axis=0, keepdims=True)
    g1 = jnp.where(r == 0, h7, pltpu.roll(g0, 1, 0))
    g2 = jnp.where(r == 0, h6, jnp.where(r == 1, h7, pltpu.roll(g0, 2, 0)))
    return g0, g1, g2


def _conv_fwd(gu, cw, cb):
    L, F2 = gu.shape
    F = F2 // 2
    tm = _tile(L, ROW_TILES)
    tn = _tile(F, (512, 256, 128))
    nj = F // tn

    def body(g_ref, halo_ref, u_ref, cw_ref, cb_ref, o_ref):
        g0, g1, g2 = _conv_taps(g_ref, halo_ref, pl.program_id(0), tm)
        conv = cw_ref[0:1, :] * g2 + cw_ref[1:2, :] * g1 + cw_ref[2:3, :] * g0 + cb_ref[...]
        o_ref[...] = (conv * _sigmoid(conv) * u_ref[...].astype(F32)).astype(BF16)

    return pl.pallas_call(
        body, name="conv_fwd", out_shape=jax.ShapeDtypeStruct((L, F), BF16), grid=(L // tm, nj),
        in_specs=[pl.BlockSpec((tm, tn), lambda i, j: (i, j)),
                  pl.BlockSpec((HALO, tn), lambda i, j: (jnp.maximum(i * (tm // HALO) - 1, 0), j)),
                  pl.BlockSpec((tm, tn), lambda i, j: (i, j + nj)),
                  pl.BlockSpec((3, tn), lambda i, j: (0, j)),
                  pl.BlockSpec((1, tn), lambda i, j: (0, j))],
        out_specs=pl.BlockSpec((tm, tn), lambda i, j: (i, j)),
        compiler_params=_params("parallel", "parallel"),
    )(gu, gu, gu, cw, cb)


def _conv_bwd_a(da, gu, cw, cb):
    L, F2 = gu.shape
    F = F2 // 2
    tm = _tile(L, ROW_TILES)
    tn = _tile(F, (512, 256, 128))
    nj = F // tn

    def body(da_ref, g_ref, halo_ref, u_ref, cw_ref, cb_ref, dc_ref, du_ref, dcb_ref, dcw_ref):
        i = pl.program_id(1)

        @pl.when(i == 0)
        def _():
            dcb_ref[...] = jnp.zeros_like(dcb_ref)
            dcw_ref[...] = jnp.zeros_like(dcw_ref)

        g0, g1, g2 = _conv_taps(g_ref, halo_ref, i, tm)
        conv = cw_ref[0:1, :] * g2 + cw_ref[1:2, :] * g1 + cw_ref[2:3, :] * g0 + cb_ref[...]
        sc = _sigmoid(conv)
        dav = da_ref[...]
        du_ref[...] = (dav * (conv * sc)).astype(BF16)
        dconv = dav * u_ref[...].astype(F32) * (sc * (1.0 + conv * (1.0 - sc)))
        dc_ref[...] = dconv
        dcb_ref[...] += jnp.sum(dconv, axis=0, keepdims=True)
        dcw_ref[0:1, :] += jnp.sum(dconv * g2, axis=0, keepdims=True)
        dcw_ref[1:2, :] += jnp.sum(dconv * g1, axis=0, keepdims=True)
        dcw_ref[2:3, :] += jnp.sum(dconv * g0, axis=0, keepdims=True)

    return pl.pallas_call(
        body, name="conv_bwd_a",
        out_shape=(jax.ShapeDtypeStruct((L, F), F32), jax.ShapeDtypeStruct((L, F), BF16),
                   jax.ShapeDtypeStruct((1, F), F32), jax.ShapeDtypeStruct((8, F), F32)),
        grid=(nj, L // tm),
        in_specs=[pl.BlockSpec((tm, tn), lambda j, i: (i, j)),
                  pl.BlockSpec((tm, tn), lambda j, i: (i, j)),
                  pl.BlockSpec((HALO, tn), lambda j, i: (jnp.maximum(i * (tm // HALO) - 1, 0), j)),
                  pl.BlockSpec((tm, tn), lambda j, i: (i, j + nj)),
                  pl.BlockSpec((3, tn), lambda j, i: (0, j)),
                  pl.BlockSpec((1, tn), lambda j, i: (0, j))],
        out_specs=(pl.BlockSpec((tm, tn), lambda j, i: (i, j)), pl.BlockSpec((tm, tn), lambda j, i: (i, j)),
                   pl.BlockSpec((1, tn), lambda j, i: (0, j)), pl.BlockSpec((8, tn), lambda j, i: (0, j))),
        compiler_params=_params("parallel", "arbitrary"),
    )(da, gu, gu, gu, cw, cb)


def _conv_bwd_b(dconv, cw):
    L, F = dconv.shape
    tm = _tile(L, ROW_TILES)
    tn = _tile(F, (512, 256, 128))
    nblk8 = L // 8
    ni = L // tm

    def body(dc_ref, nxt_ref, cw_ref, o_ref):
        i = pl.program_id(0)
        dc = dc_ref[...]
        nxt = jnp.where(i < ni - 1, nxt_ref[...], 0.0)
        sub = lax.broadcasted_iota(jnp.int32, (8, 1), 0)
        n0 = jnp.sum(jnp.where(sub == 0, nxt, 0.0), axis=0, keepdims=True)
        n1 = jnp.sum(jnp.where(sub == 1, nxt, 0.0), axis=0, keepdims=True)
        r = lax.broadcasted_iota(jnp.int32, (tm, 1), 0)
        d1 = jnp.where(r == tm - 1, n0, pltpu.roll(dc, tm - 1, 0))
        d2 = jnp.where(r == tm - 2, n0, jnp.where(r == tm - 1, n1, pltpu.roll(dc, tm - 2, 0)))
        dg = cw_ref[2:3, :] * dc + cw_ref[1:2, :] * d1 + cw_ref[0:1, :] * d2
        rows = i * tm + r
        o_ref[...] = jnp.where(rows >= PAD_LEN, dg, 0.0).astype(BF16)

    return pl.pallas_call(
        body, name="conv_bwd_b", out_shape=jax.ShapeDtypeStruct((L, F), BF16), grid=(ni, F // tn),
        in_specs=[pl.BlockSpec((tm, tn), lambda i, j: (i, j)),
                  pl.BlockSpec((8, tn), lambda i, j: (jnp.minimum((i + 1) * (tm // 8), nblk8 - 1), j)),
                  pl.BlockSpec((3, tn), lambda i, j: (0, j))],
        out_specs=pl.BlockSpec((tm, tn), lambda i, j: (i, j)),
        compiler_params=_params("parallel", "parallel"),
    )(dconv, dconv, cw)


ANY = pl.BlockSpec(memory_space=pl.ANY)


def _coords():
    return lax.axis_index("x"), lax.axis_index("y"), lax.axis_index("c")


def _flip(v, bit):
    return 1 - v if bit else v


CHIPS = [(1, 0), (0, 1), (1, 1)]
PEERS = [(dx, dy, dc) for dx in (0, 1) for dy in (0, 1) for dc in (0, 1)][1:]


class _Rider:
    def __init__(self, operands, out_shapes, sem_shapes, start, finish):
        self.operands, self.out_shapes, self.sem_shapes = list(operands), list(out_shapes), list(sem_shapes)
        self.start, self.finish = start, finish


def _run_rider(rider, name):
    n_in, n_out = len(rider.operands), len(rider.out_shapes)

    def body(*refs):
        ins, outs, sems = refs[:n_in], refs[n_in:n_in + n_out], refs[n_in + n_out:]
        rider.start(ins, outs, sems)
        rider.finish(ins, outs, sems)

    return pl.pallas_call(
        body, name=name, out_shape=tuple(rider.out_shapes),
        in_specs=[ANY] * n_in, out_specs=tuple([ANY] * n_out), scratch_shapes=rider.sem_shapes,
        compiler_params=pltpu.CompilerParams(has_side_effects=True),
    )(*rider.operands)


def _gather_rider(big, small):
    nbig, n = len(big), len(big) + len(small)
    arrays = list(big) + list(small)

    def plan(ins, outs, sems):
        ici_send, ici_recv, d2d_send, d2d_recv, local_sems = sems
        x, y, c = _coords()
        mine = 2 * x + y

        def half(w, h):
            r2 = arrays[w].shape[0] // 2
            return pl.ds(h * r2, r2)

        def ici(w, j, landing):
            px, py = _flip(x, CHIPS[j][0]), _flip(y, CHIPS[j][1])
            slot = 2 * px + py if landing else mine
            if w < nbig:
                src, dst = ins[w].at[half(w, c)], outs[w].at[slot, half(w, c)]
            else:
                src, dst = ins[w], outs[w].at[slot]
            return pltpu.make_async_remote_copy(
                src_ref=src, dst_ref=dst, send_sem=ici_send.at[w * 3 + j], recv_sem=ici_recv.at[w * 3 + j],
                device_id=(px, py, c), device_id_type=MESH)

        def d2d(w, j, landing):
            px, py = _flip(x, CHIPS[j][0]), _flip(y, CHIPS[j][1])
            mine_rows = outs[w].at[2 * px + py, half(w, c)]
            dst = outs[w].at[2 * px + py, half(w, 1 - c)] if landing else mine_rows
            return pltpu.make_async_remote_copy(
                src_ref=mine_rows, dst_ref=dst, send_sem=d2d_send.at[w * 3 + j], recv_sem=d2d_recv.at[w * 3 + j],
                device_id=(x, y, 1 - c), device_id_type=MESH)

        local = [pltpu.make_async_copy(ins[w], outs[w].at[mine], local_sems.at[w]) for w in range(n)]
        return ici, d2d, local

    def start(ins, outs, sems):
        ici, _, local = plan(ins, outs, sems)
        for cp in local:
            cp.start()
        for w in range(n):
            for j in range(3):
                ici(w, j, False).start()

    def finish(ins, outs, sems):
        ici, d2d, local = plan(ins, outs, sems)
        for w in range(n):
            for j in range(3):
                ici(w, j, True).wait_recv()
                if w < nbig:
                    d2d(w, j, False).start()
        for w in range(nbig):
            for j in range(3):
                d2d(w, j, True).wait_recv()
        for w in range(n):
            for j in range(3):
                ici(w, j, False).wait_send()
                if w < nbig:
                    d2d(w, j, False).wait_send()
        for cp in local:
            cp.wait()

    return _Rider(
        arrays, [jax.ShapeDtypeStruct((4,) + s.shape, s.dtype) for s in arrays],
        [pltpu.SemaphoreType.DMA((3 * n,)), pltpu.SemaphoreType.DMA((3 * n,)),
         pltpu.SemaphoreType.DMA((max(3 * nbig, 1),)), pltpu.SemaphoreType.DMA((max(3 * nbig, 1),)),
         pltpu.SemaphoreType.DMA((n,))],
        start, finish)


def _to_sibling(arrays, name):
    n = len(arrays)

    def body(*refs):
        ins, outs = refs[:n], refs[n:2 * n]
        send_sems, recv_sems = refs[2 * n:]
        x, y, c = _coords()

        def copy(w):
            return pltpu.make_async_remote_copy(
                src_ref=ins[w], dst_ref=outs[w], send_sem=send_sems.at[w], recv_sem=recv_sems.at[w],
                device_id=(x, y, 1 - c), device_id_type=MESH)

        for w in range(n):
            copy(w).start()
        for w in range(n):
            copy(w).wait_recv()
            copy(w).wait_send()

    return pl.pallas_call(
        body, name=name,
        out_shape=tuple(jax.ShapeDtypeStruct(a.shape, a.dtype) for a in arrays),
        in_specs=[ANY] * n, out_specs=tuple([ANY] * n),
        scratch_shapes=[pltpu.SemaphoreType.DMA((n,)), pltpu.SemaphoreType.DMA((n,))],
        compiler_params=pltpu.CompilerParams(has_side_effects=True),
    )(*arrays)


def _pair_sum(a, b, name):
    _, r, c = a.shape
    tr = _tile(r, (64, 32, 16))

    def body(a_ref, b_ref, o_ref):
        o_ref[...] = (a_ref[...].astype(F32) + b_ref[...].astype(F32)).astype(BF16)

    blk = pl.BlockSpec((4, tr, c), lambda i: (0, i, 0))
    return pl.pallas_call(
        body, name=name, out_shape=jax.ShapeDtypeStruct(a.shape, BF16), grid=(r // tr,),
        in_specs=[blk, blk], out_specs=blk, compiler_params=_params("parallel"),
    )(a, b)


def _scatter_rider(parts):
    n = len(parts)

    def plan(ins, outs, sems):
        send_sems, recv_sems, local_sems = sems
        x, y, c = _coords()
        mine = 2 * x + y

        def ici(w, j, landing):
            px, py = _flip(x, CHIPS[j][0]), _flip(y, CHIPS[j][1])
            return pltpu.make_async_remote_copy(
                src_ref=ins[w].at[2 * px + py], dst_ref=outs[w].at[2 * px + py if landing else mine],
                send_sem=send_sems.at[w * 3 + j], recv_sem=recv_sems.at[w * 3 + j],
                device_id=(px, py, c), device_id_type=MESH)

        local = [pltpu.make_async_copy(ins[w].at[mine], outs[w].at[mine], local_sems.at[w]) for w in range(n)]
        return ici, local

    def start(ins, outs, sems):
        ici, local = plan(ins, outs, sems)
        for cp in local:
            cp.start()
        for w in range(n):
            for j in range(3):
                ici(w, j, False).start()

    def finish(ins, outs, sems):
        ici, local = plan(ins, outs, sems)
        for w in range(n):
            for j in range(3):
                ici(w, j, True).wait_recv()
                ici(w, j, False).wait_send()
        for cp in local:
            cp.wait()

    return _Rider(
        parts, [jax.ShapeDtypeStruct(p.shape, p.dtype) for p in parts],
        [pltpu.SemaphoreType.DMA((3 * n,)), pltpu.SemaphoreType.DMA((3 * n,)), pltpu.SemaphoreType.DMA((n,))],
        start, finish)


def _sum4(recv, name):
    _, r, c = recv.shape
    tr = _tile(r, (64, 32, 16))

    def body(in_ref, o_ref):
        o_ref[...] = ((in_ref[0].astype(F32) + in_ref[1].astype(F32)) + in_ref[2].astype(F32)) + in_ref[3].astype(F32)

    return pl.pallas_call(
        body, name=name, out_shape=jax.ShapeDtypeStruct((r, c), F32), grid=(r // tr,),
        in_specs=[pl.BlockSpec((4, tr, c), lambda i: (0, i, 0))],
        out_specs=pl.BlockSpec((tr, c), lambda i: (i, 0)),
        compiler_params=_params("parallel"),
    )(recv)


def _allreduce_small(packed):
    R = packed.shape[0]

    def body(in_ref, o_ref, buf, send_sems, recv_sems):
        x, y, c = _coords()
        me = 4 * x + 2 * y + c
        buf[me] = in_ref[...]
        for j, (dx, dy, dc) in enumerate(PEERS):
            px, py, pc = _flip(x, dx), _flip(y, dy), _flip(c, dc)
            pltpu.make_async_remote_copy(
                src_ref=in_ref, dst_ref=buf.at[me], send_sem=send_sems.at[j], recv_sem=recv_sems.at[j],
                device_id=(px, py, pc), device_id_type=MESH).start()
        for j, (dx, dy, dc) in enumerate(PEERS):
            px, py, pc = _flip(x, dx), _flip(y, dy), _flip(c, dc)
            rc = pltpu.make_async_remote_copy(
                src_ref=in_ref, dst_ref=buf.at[4 * px + 2 * py + pc], send_sem=send_sems.at[j],
                recv_sem=recv_sems.at[j], device_id=(px, py, pc), device_id_type=MESH)
            rc.wait_recv()
            rc.wait_send()
        acc = buf[0]
        for s in range(1, 8):
            acc = acc + buf[s]
        o_ref[...] = acc

    return pl.pallas_call(
        body, name="allreduce_small", out_shape=jax.ShapeDtypeStruct((R, 128), F32),
        in_specs=[pl.BlockSpec(memory_space=pltpu.VMEM)], out_specs=pl.BlockSpec(memory_space=pltpu.VMEM),
        scratch_shapes=[pltpu.VMEM((8, R, 128), F32), pltpu.SemaphoreType.DMA((7,)), pltpu.SemaphoreType.DMA((7,))],
        compiler_params=pltpu.CompilerParams(has_side_effects=True, vmem_limit_bytes=VMEM_LIMIT_BYTES),
    )(packed)


def _adamw_math(w, g, m, v):
    m = ADAM_B1 * m + (1.0 - ADAM_B1) * g
    v = ADAM_B2 * v + (1.0 - ADAM_B2) * (g * g)
    m_hat = m / (1.0 - ADAM_B1 ** ADAM_STEP)
    v_hat = v / (1.0 - ADAM_B2 ** ADAM_STEP)
    delta = -ADAM_LR * (m_hat / (jnp.sqrt(v_hat) + ADAM_EPS) + ADAM_WD * w)
    return delta, m, v


def _adamw_big(mine, other, w, m, v, name):
    R, C = w.shape
    tr = _tile(R // 2, (128, 64, 32, 16, 8))
    nb = (R // 2) // tr

    def body(mine_ref, other_ref, w_ref, m_ref, v_ref, g_ref, d_ref, mo_ref, vo_ref):
        is_mine = (pl.program_id(0) // nb) == lax.axis_index("c")
        g = jnp.where(is_mine, mine_ref[...], other_ref[...])
        d, mn, vn = _adamw_math(w_ref[...], g, m_ref[...], v_ref[...])
        g_ref[...] = g
        d_ref[...] = d
        mo_ref[...] = mn
        vo_ref[...] = vn

    blk = pl.BlockSpec((tr, C), lambda i: (i, 0))
    half = pl.BlockSpec((tr, C), lambda i: (i % nb, 0))
    sds = jax.ShapeDtypeStruct((R, C), F32)
    return pl.pallas_call(
        body, name=name, out_shape=(sds, sds, sds, sds), grid=(2 * nb,),
        in_specs=[half, half, blk, blk, blk], out_specs=(blk, blk, blk, blk), compiler_params=_params("parallel"),
    )(mine, other, w, m, v)


def _adamw_small(items, lb_raw, dlb):
    n = len(items)
    lb_w, lb_m, lb_v = lb_raw

    def body(*refs):
        ins = refs[:4 * n]
        dlb_ref, lw_ref, lm_ref, lv_ref = refs[4 * n:4 * n + 4]
        outs = refs[4 * n + 4:]
        for t in range(n):
            g_ref, w_ref, m_ref, v_ref = ins[4 * t:4 * t + 4]
            d, mn, vn = _adamw_math(w_ref[...], g_ref[...], m_ref[...], v_ref[...])
            outs[3 * t][...] = d
            outs[3 * t + 1][...] = mn
            outs[3 * t + 2][...] = vn
        p0 = 1.0 / (1.0 + jnp.exp(lw_ref[1:2, :] - lw_ref[0:1, :]))
        g0 = dlb_ref[...] * p0 * (1.0 - p0)
        base = 3 * n
        outs[base][0:1, :] = g0
        outs[base][1:2, :] = -g0
        d, mn, vn = _adamw_math(lw_ref[...], outs[base][...], lm_ref[...], lv_ref[...])
        outs[base + 1][...] = d
        outs[base + 2][...] = mn
        outs[base + 3][...] = vn

    operands = [a for it in items for a in it] + [dlb, lb_w, lb_m, lb_v]
    out_shape = []
    for (g, w, m, v) in items:
        out_shape += [jax.ShapeDtypeStruct(w.shape, F32)] * 3
    out_shape += [jax.ShapeDtypeStruct(lb_w.shape, F32)] * 4
    vm = pl.BlockSpec(memory_space=pltpu.VMEM)
    res = pl.pallas_call(
        body, name="adamw_small", out_shape=tuple(out_shape),
        in_specs=[vm] * len(operands), out_specs=tuple([vm] * len(out_shape)),
        compiler_params=pltpu.CompilerParams(vmem_limit_bytes=VMEM_LIMIT_BYTES),
    )(*operands)
    deltas = [res[3 * t] for t in range(n)] + [res[3 * n + 1]]
    new_m = [res[3 * t + 1] for t in range(n)] + [res[3 * n + 2]]
    new_v = [res[3 * t + 2] for t in range(n)] + [res[3 * n + 3]]
    return res[3 * n], deltas, new_m, new_v


def _shard_row_half(g, by_cols, h):
    R, C = g.shape
    if by_cols:
        part = lax.dynamic_index_in_dim(g.reshape(2, R // 2, 4, C // 4), h, axis=0, keepdims=False)
        return part.transpose(1, 0, 2).astype(BF16)
    return lax.dynamic_index_in_dim(g.reshape(4, 2, R // 8, C), h, axis=1, keepdims=False).astype(BF16)


def kernel(x, positions, meta_tokens, w_in, w_q_up, w_kv_up, w_branch_mla, w_branch_hgrn, w_out, w_ffn_in, w_ffn_out, conv_w, conv_b, g_mix_norm, g_q_norm, g_kv_norm, g_hgrn_norm, g_ffn_norm, g_final_norm, lb_raw, loss_target, m_meta_tokens, m_w_in, m_w_q_up, m_w_kv_up, m_w_branch_mla, m_w_branch_hgrn, m_w_out, m_w_ffn_in, m_w_ffn_out, m_conv_w, m_conv_b, m_g_mix_norm, m_g_q_norm, m_g_kv_norm, m_g_hgrn_norm, m_g_ffn_norm, m_g_final_norm, m_lb_raw, v_meta_tokens, v_w_in, v_w_q_up, v_w_kv_up, v_w_branch_mla, v_w_branch_hgrn, v_w_out, v_w_ffn_in, v_w_ffn_out, v_conv_w, v_conv_b, v_g_mix_norm, v_g_q_norm, v_g_kv_norm, v_g_hgrn_norm, v_g_ffn_norm, v_g_final_norm, v_lb_raw):
    S, D = x.shape[1], x.shape[2]
    L = S + PREFIX
    QL, KVL = g_q_norm.shape[1], g_kv_norm.shape[1]
    F = conv_b.shape[1]
    heads = (4 * w_kv_up.shape[2]) // QPAD
    nh = D // HEAD
    assert lb_raw.shape[0] == 2 and g_hgrn_norm.shape[1] == HEAD and L % CHUNK == 0
    ix, iy, ic = _coords()
    chip = 2 * ix + iy

    big = [w_in, w_q_up, w_kv_up, w_branch_mla, w_branch_hgrn, w_out, w_ffn_in, w_ffn_out]
    col_sharded = [True, True, True, False, False, False, True, False]
    shards = [w[0].astype(BF16) for w in big]
    early = _run_rider(_gather_rider(shards[:3], [meta_tokens, conv_w[0]]), "gather_early")
    late_rider = _gather_rider(shards[3:], [])

    def full(gw, by_cols):
        _, r, c = gw.shape
        return gw.transpose(1, 0, 2).reshape(r, 4 * c) if by_cols else gw.reshape(4 * r, c)

    W_in, W_q, W_kv = [full(gw, True) for gw in early[:3]]
    meta_full = full(early[3], True)
    cw_full = full(early[4], True)
    c0 = QL + KVL
    W_lat = W_in[:, :c0]
    W_kr = jnp.pad(W_in[:, c0:c0 + ROPE], ((0, 0), (0, 128 - ROPE)))
    W_H = W_in[:, c0 + ROPE:c0 + ROPE + 4 * D]
    W_G = W_in[:, c0 + ROPE + 4 * D:]
    W_qp = jnp.pad(W_q.reshape(QL, heads, QK_HEAD), ((0, 0), (0, 0), (0, QPAD - QK_HEAD))).reshape(QL, heads * QPAD)

    h0 = jnp.concatenate([jnp.zeros((PAD_LEN, D), F32), meta_full, x[0]], axis=0)
    pos = jnp.concatenate([jnp.zeros((PAD_LEN,), jnp.int32), jnp.arange(N_META, dtype=jnp.int32),
                           positions[0].astype(jnp.int32) + N_META])
    inv = 1.0 / (ROPE_THETA ** (jnp.arange(0, ROPE, 2, dtype=F32) / ROPE))
    ang = pos.astype(F32)[:, None] * inv
    zero = jnp.zeros((L, 128 - ROPE), F32)
    cos = jnp.concatenate([jnp.cos(ang), jnp.cos(ang), zero], axis=1)
    sin = jnp.concatenate([jnp.sin(ang), jnp.sin(ang), zero], axis=1)

    u1 = _rmsnorm_fwd(h0, g_mix_norm, "norm_mix")
    lat = _mm(u1, W_lat, "nn", F32, "proj_lat")
    hp = _mm(u1, W_H, "nn", F32, "proj_hgrn")
    gates = _mm(u1, W_G, "nn", F32, "proj_gates")
    kr = _mm(u1, W_kr, "nn", F32, "proj_krope")
    qn = _rmsnorm_fwd(lat, g_q_norm, "norm_q", col0=0, width=QL)
    kvn = _rmsnorm_fwd(lat, g_kv_norm, "norm_kv", col0=QL, width=KVL)
    qp = _mm(qn, W_qp, "nn", BF16, "q_up")
    kv = _mm(kvn, W_kv, "nn", BF16, "kv_up")
    qc, kc, vv = _rope_fwd(qp, kv, kr, cos, sin, heads)
    o_mla, lse, late = _attn_fwd(qc, kc, vv, heads, late_rider)
    W_a, W_b, W_o, W_fi, W_fo = [full(gw, bc) for gw, bc in zip(late, col_sharded[3:])]
    o_hgrn, o_rec, s_hist = _hgrn_fwd(hp, lb_raw, g_hgrn_norm, nh)
    br_a = _mm(o_mla, W_a, "nn", F32, "branch_mla")
    br_b = _mm(o_hgrn, W_b, "nn", F32, "branch_hgrn")
    merged = _merge_fwd(br_a, br_b, gates)
    h1 = _mm(merged, W_o, "nn", F32, "out_proj", res=h0)
    u2 = _rmsnorm_fwd(h1, g_ffn_norm, "norm_ffn")
    gu = _mm(u2, W_fi, "nn", BF16, "ffn_in")
    act = _conv_fwd(gu, cw_full, conv_b)
    h2 = _mm(act, W_fo, "nn", F32, "ffn_out", res=h1)
    dh2, loss_p, dg_final = _final_loss_bwd(h2, loss_target[0], g_final_norm.reshape(1, D))

    dact = _mm(dh2, W_fo, "nt", F32, "d_act")
    dW_fo = _mm(act, dh2, "tn", F32, "dw_ffn_out")
    dconv, dup, dcb, dcw = _conv_bwd_a(dact, gu, cw_full, conv_b)
    dgate = _conv_bwd_b(dconv, cw_full)
    dgu = jnp.concatenate([dgate, dup], axis=1)
    du2 = _mm(dgu, W_fi, "nt", F32, "d_u2")
    dW_fi = _mm(u2, dgu, "tn", F32, "dw_ffn_in")
    dh1, dg_ffn = _rmsnorm_bwd(du2, h1, g_ffn_norm, "norm_ffn_bwd", F32, res=dh2)
    dmerged = _mm(dh1, W_o, "nt", F32, "d_merged")
    dW_o = _mm(merged, dh1, "tn", F32, "dw_out")
    d_a, d_b, d_gates = _merge_bwd(dmerged, br_a, br_b, gates)
    do_mla = _mm(d_a, W_a, "nt", BF16, "d_o_mla")
    dW_a = _mm(o_mla, d_a, "tn", F32, "dw_branch_mla")
    do_hgrn = _mm(d_b, W_b, "nt", F32, "d_o_hgrn")
    dW_b = _mm(o_hgrn, d_b, "tn", F32, "dw_branch_hgrn")
    names = ["w_in", "w_q_up", "w_kv_up", "w_branch_mla", "w_branch_hgrn", "w_out", "w_ffn_in", "w_ffn_out"]

    def chip_partials(grads, by_cols, nms, tag):
        keep = [_shard_row_half(g, bc, ic) for g, bc in zip(grads, by_cols)]
        give = [_shard_row_half(g, bc, 1 - ic) for g, bc in zip(grads, by_cols)]
        taken = _to_sibling(give, "pair_exchange_" + tag)
        return [_pair_sum(a, b, "pair_sum_" + nm) for a, b, nm in zip(keep, taken, nms)]

    late_parts = chip_partials([dW_a, dW_b, dW_o, dW_fi, dW_fo], col_sharded[3:], names[3:], "late")
    dhq, dhf, dhi, dhg, dgn_p, dlb_p = _hgrn_bwd(hp, lb_raw, g_hgrn_norm, do_hgrn, o_rec, s_hist, nh)
    dqc, dkc, dvv, late_recv = _attn_bwd(qc, kc, vv, do_mla, lse, _attn_delta(o_mla, do_mla, heads), heads,
                                         _scatter_rider(late_parts))
    dqp, dkv, dkr = _rope_bwd(dqc, dkc, dvv, cos, sin, heads)
    dqn = _mm(dqp, W_qp, "nt", F32, "d_qn")
    dW_qp = _mm(qn, dqp, "tn", F32, "dw_q_up")
    dkvn = _mm(dkv, W_kv, "nt", F32, "d_kvn")
    dW_kv = _mm(kvn, dkv, "tn", F32, "dw_kv_up")
    dq_lat, dg_q = _rmsnorm_bwd(dqn, lat, g_q_norm, "norm_q_bwd", BF16, col0=0)
    dkv_lat, dg_kv = _rmsnorm_bwd(dkvn, lat, g_kv_norm, "norm_kv_bwd", BF16, col0=QL)
    dlat = jnp.concatenate([dq_lat, dkv_lat], axis=1)
    dhp = jnp.concatenate([dhq, dhf, dhi, dhg], axis=1)
    dW_lat = _mm(u1, dlat, "tn", F32, "dw_in_lat")
    dW_H = _mm(u1, dhp, "tn", F32, "dw_in_hgrn")
    dW_G = _mm(u1, d_gates, "tn", F32, "dw_in_gates")
    dW_kr = _mm(u1, dkr, "tn", F32, "dw_in_krope")
    dW_in = jnp.concatenate([dW_lat, dW_kr[:, :ROPE], dW_H, dW_G], axis=1)
    dW_q = dW_qp.reshape(QL, heads, QPAD)[:, :, :QK_HEAD].reshape(QL, heads * QK_HEAD)
    early_parts = chip_partials([dW_in, dW_q, dW_kv], col_sharded[:3], names[:3], "early")
    du1 = _mm(dlat, W_lat, "nt", F32, "d_u1_lat")
    du1, early_recv = _mm(dhp, W_H, "nt", F32, "d_u1_hgrn", res=du1, rider=_scatter_rider(early_parts))
    du1 = _mm(d_gates, W_G, "nt", F32, "d_u1_gates", res=du1)
    du1 = _mm(dkr, W_kr, "nt", F32, "d_u1_krope", res=du1)
    dh0, dg_mix = _rmsnorm_bwd(du1, h0, g_mix_norm, "norm_mix_bwd", F32, res=dh1)
    grad_x = dh0[PREFIX:][None]

    received = list(early_recv) + list(late_recv)
    halves = [_sum4(r, "sum_" + nm) for r, nm in zip(received, names)]
    others = _to_sibling(halves, "swap_halves")
    big_m = [m_w_in, m_w_q_up, m_w_kv_up, m_w_branch_mla, m_w_branch_hgrn, m_w_out, m_w_ffn_in, m_w_ffn_out]
    big_v = [v_w_in, v_w_q_up, v_w_kv_up, v_w_branch_mla, v_w_branch_hgrn, v_w_out, v_w_ffn_in, v_w_ffn_out]
    big_out = {}
    for nm, mine, other, w, m, v in zip(names, halves, others, big, big_m, big_v):
        g, d, mn, vn = _adamw_big(mine, other, w[0], m[0], v[0], "adamw_" + nm)
        big_out[nm] = (g[None], d[None], mn[None], vn[None])

    pieces = [loss_p[:, :1], dg_mix, dg_q, dg_kv, jnp.sum(dgn_p[:, 0, :], axis=0, keepdims=True), dg_ffn, dg_final,
              dlb_p[:, 0, :].reshape(1, D), dcb, dcw[0:3].reshape(1, 3 * F), dh0[PAD_LEN:PREFIX].reshape(1, N_META * D)]
    sizes = [p.shape[1] for p in pieces]
    flat = jnp.concatenate(pieces, axis=1)[0]
    rows = -(-flat.shape[0] // 1024) * 8
    packed = jnp.pad(flat, (0, rows * 128 - flat.shape[0])).reshape(rows, 128)
    total = _allreduce_small(packed).reshape(-1)
    offs = [0]
    for s in sizes:
        offs.append(offs[-1] + s)
    loss, g_mix, g_q, g_kv, g_hg, g_ffn, g_fin, dlb, g_cb, g_cw, g_meta = [
        total[offs[t]:offs[t + 1]].reshape(1, sizes[t]) for t in range(len(sizes))]
    g_cw = lax.dynamic_slice_in_dim(g_cw.reshape(3, F), chip * (F // 4), F // 4, axis=1)
    g_meta = lax.dynamic_slice_in_dim(g_meta.reshape(N_META, D), chip * (D // 4), D // 4, axis=1)
    items = [(g_meta, meta_tokens, m_meta_tokens, v_meta_tokens),
             (g_cw, conv_w[0], m_conv_w[0], v_conv_w[0]),
             (g_cb, conv_b, m_conv_b, v_conv_b),
             (g_mix, g_mix_norm, m_g_mix_norm, v_g_mix_norm),
             (g_q, g_q_norm, m_g_q_norm, v_g_q_norm),
             (g_kv, g_kv_norm, m_g_kv_norm, v_g_kv_norm),
             (g_hg, g_hgrn_norm, m_g_hgrn_norm, v_g_hgrn_norm),
             (g_ffn, g_ffn_norm, m_g_ffn_norm, v_g_ffn_norm),
             (g_fin, g_final_norm.reshape(1, D), m_g_final_norm.reshape(1, D), v_g_final_norm.reshape(1, D))]
    g_lb, s_delta, s_m, s_v = _adamw_small(items, (lb_raw, m_lb_raw, v_lb_raw), dlb)
    s_grads = [it[0] for it in items] + [g_lb]

    def shape_small(vals):
        meta, cw, cb, mix, q, kvg, hg, ffn, fin, lb = vals
        return [meta, cw[None], cb, mix, q, kvg, hg, ffn, fin.reshape(D), lb]

    s_grads, s_delta, s_m, s_v = [shape_small(v) for v in (s_grads, s_delta, s_m, s_v)]

    def ordered(kind, small):
        bigs = [big_out[nm][kind] for nm in names]
        return [small[0]] + bigs + small[1:]

    return (loss.reshape(()), grad_x, *ordered(0, s_grads), *ordered(1, s_delta), *ordered(2, s_m), *ordered(3, s_v))
```

```python
import math

import jax
import jax.numpy as jnp
from jax import lax
from jax.experimental import pallas as pl
from jax.experimental.pallas import tpu as pltpu

F32 = jnp.float32
BF16 = jnp.bfloat16
MESH = pl.DeviceIdType.MESH

NORM_EPS = 1e-6
N_META = 16
PREFIX = 128
PAD_LEN = PREFIX - N_META
HEAD = 128
ROPE = 64
QK_HEAD = HEAD + ROPE
QPAD = 2 * HEAD
SOFTMAX_SCALE = QK_HEAD ** -0.5
ROPE_THETA = 10000.0
CHUNK = 128
HALO = 16
LEVELS = 7
HGRN_FWD_HEADS = 4
HGRN_BWD_HEADS = 4
ATTN_FWD_HEADS = 2
ATTN_BWD_HEADS = 2
NEG = -1e30

ADAM_LR = 0.001
ADAM_B1 = 0.9
ADAM_B2 = 0.999
ADAM_EPS = 1e-08
ADAM_WD = 0.01
ADAM_STEP = 10

VMEM_LIMIT_BYTES = 48 * 1024 * 1024


def _params(*sem):
    return pltpu.CompilerParams(dimension_semantics=sem, vmem_limit_bytes=VMEM_LIMIT_BYTES)


def _tile(n, prefs):
    for p in prefs:
        if n % p == 0:
            return p
    return n


ROW_TILES = (640, 512, 256, 128, 64, 32, 16, 8)
TN_ROW_TILES = (1024, 1408, 1536, 512, 256, 128)
TN_K_TILES = (1664, 640, 512, 256, 128)
COL_TILES = (1024, 512, 256, 128)
K_TILES = (2048, 1536, 1408, 1024, 512, 256, 128)


def _sigmoid(x):
    return 1.0 / (1.0 + jnp.exp(-x))


def _dot(a, b, dims):
    return lax.dot_general(a.astype(BF16), b.astype(BF16), (dims, ((), ())), preferred_element_type=F32)


NN = ((1,), (0,))
TN = ((0,), (0,))
NT = ((1,), (1,))


def _split3(x):
    hi = x.astype(BF16)
    r = x - hi.astype(F32)
    mid = r.astype(BF16)
    lo = (r - mid.astype(F32)).astype(BF16)
    return hi, mid, lo


def _dot_exact_rhs(sel, x, dims):
    hi, mid, lo = _split3(x)
    return _dot(sel, hi, dims) + _dot(sel, mid, dims) + _dot(sel, lo, dims)


def _mm(a, b, mode, out_dtype, name, res=None, rider=None):
    if mode == "nn":
        (M, K), (K2, N) = a.shape, b.shape
    elif mode == "tn":
        (K, M), (K2, N) = a.shape, b.shape
    else:
        (M, K), (N, K2) = a.shape, b.shape
    assert K == K2, (name, a.shape, b.shape)
    tm = _tile(M, TN_ROW_TILES if mode == "tn" else ROW_TILES)
    tn = _tile(N, COL_TILES)
    tk = _tile(K, TN_K_TILES if mode == "tn" else K_TILES)
    nk = K // tk
    dims = {"nn": NN, "tn": TN, "nt": NT}[mode]

    n_rin = 0 if rider is None else len(rider.operands)
    n_rout = 0 if rider is None else len(rider.out_shapes)
    n_rsem = 0 if rider is None else len(rider.sem_shapes)
    grid = (M // tm, N // tn, nk)

    def body(*refs):
        main_in, r_ins, (o_ref,), r_outs, accs, r_sems = _split_refs(
            refs, (2 if res is None else 3, n_rin, 1, n_rout, 0 if nk == 1 else 1, n_rsem))
        a_ref, b_ref = main_in[:2]
        r_ref = None if res is None else main_in[2]
        i, j, k = pl.program_id(0), pl.program_id(1), pl.program_id(2)
        if rider is not None:
            pl.when((i == 0) & (j == 0) & (k == 0))(lambda: rider.start(r_ins, r_outs, r_sems))

        def finish(r):
            if r_ref is not None:
                r = r + r_ref[...].astype(F32)
            o_ref[...] = r.astype(out_dtype)

        if nk == 1:
            finish(_dot(a_ref[...], b_ref[...], dims))
        else:
            acc = accs[0]

            @pl.when(k == 0)
            def _():
                acc[...] = jnp.zeros_like(acc)

            acc[...] += _dot(a_ref[...], b_ref[...], dims)

            @pl.when(k == nk - 1)
            def _():
                finish(acc[...])

        if rider is not None:
            last = (i == grid[0] - 1) & (j == grid[1] - 1) & (k == nk - 1)
            pl.when(last)(lambda: rider.finish(r_ins, r_outs, r_sems))

    if mode == "nn":
        a_spec = pl.BlockSpec((tm, tk), lambda i, j, k: (i, k))
        b_spec = pl.BlockSpec((tk, tn), lambda i, j, k: (k, j))
    elif mode == "tn":
        a_spec = pl.BlockSpec((tk, tm), lambda i, j, k: (k, i))
        b_spec = pl.BlockSpec((tk, tn), lambda i, j, k: (k, j))
    else:
        a_spec = pl.BlockSpec((tm, tk), lambda i, j, k: (i, k))
        b_spec = pl.BlockSpec((tn, tk), lambda i, j, k: (j, k))
    in_specs = [a_spec, b_spec]
    operands = [a, b]
    if res is not None:
        in_specs.append(pl.BlockSpec((tm, tn), lambda i, j, k: (i, j)))
        operands.append(res)
    out_shape = jax.ShapeDtypeStruct((M, N), out_dtype)
    out_spec = pl.BlockSpec((tm, tn), lambda i, j, k: (i, j))
    scratch = [] if nk == 1 else [pltpu.VMEM((tm, tn), F32)]
    if rider is None:
        return pl.pallas_call(
            body, name=name, out_shape=out_shape, grid=grid, in_specs=in_specs, out_specs=out_spec,
            scratch_shapes=scratch, compiler_params=_params("parallel", "parallel", "arbitrary"),
        )(*operands)
    res_all = pl.pallas_call(
        body, name=name, out_shape=(out_shape, *rider.out_shapes), grid=grid,
        in_specs=in_specs + [ANY] * n_rin, out_specs=(out_spec, *([ANY] * n_rout)),
        scratch_shapes=scratch + rider.sem_shapes,
        compiler_params=pltpu.CompilerParams(dimension_semantics=("arbitrary", "arbitrary", "arbitrary"),
                                             vmem_limit_bytes=VMEM_LIMIT_BYTES, has_side_effects=True),
    )(*operands, *rider.operands)
    return res_all[0], res_all[1:]


def _rmsnorm_fwd(x, g, name, col0=0, width=None):
    L = x.shape[0]
    width = x.shape[1] if width is None else width
    assert col0 % width == 0
    cb = col0 // width
    tm = _tile(L, (128, 64, 32, 16))

    def body(x_ref, g_ref, o_ref):
        xv = x_ref[...]
        r = lax.rsqrt(jnp.mean(xv * xv, axis=-1, keepdims=True) + NORM_EPS)
        o_ref[...] = ((xv * r) * g_ref[...]).astype(BF16)

    return pl.pallas_call(
        body, name=name,
        out_shape=jax.ShapeDtypeStruct((L, width), BF16),
        grid=(L // tm,),
        in_specs=[pl.BlockSpec((tm, width), lambda i: (i, cb)), pl.BlockSpec((1, width), lambda i: (0, 0))],
        out_specs=pl.BlockSpec((tm, width), lambda i: (i, 0)),
        compiler_params=_params("parallel"),
    )(x, g)


def _embed_norm(x, meta, g):
    S, D = x.shape
    L = S + PREFIX
    tm = PREFIX

    def body(x_ref, meta_ref, g_ref, h_ref, u_ref):
        i = pl.program_id(0)

        @pl.when(i == 0)
        def _():
            h_ref[...] = jnp.zeros_like(h_ref)
            h_ref[PAD_LEN:PREFIX, :] = meta_ref[...]

        @pl.when(i > 0)
        def _():
            h_ref[...] = x_ref[...]

        xv = h_ref[...]
        r = lax.rsqrt(jnp.mean(xv * xv, axis=-1, keepdims=True) + NORM_EPS)
        u_ref[...] = ((xv * r) * g_ref[...]).astype(BF16)

    return pl.pallas_call(
        body, name="embed_norm_mix",
        out_shape=(jax.ShapeDtypeStruct((L, D), F32), jax.ShapeDtypeStruct((L, D), BF16)),
        grid=(L // tm,),
        in_specs=[pl.BlockSpec((tm, D), lambda i: (jnp.maximum(i - 1, 0), 0)),
                  pl.BlockSpec((N_META, D), lambda i: (0, 0)), pl.BlockSpec((1, D), lambda i: (0, 0))],
        out_specs=(pl.BlockSpec((tm, D), lambda i: (i, 0)), pl.BlockSpec((tm, D), lambda i: (i, 0))),
        compiler_params=_params("parallel"),
    )(x, meta, g)


def _rmsnorm_bwd(dy, x, g, name, out_dtype, col0=0, res=None, split_prefix=False):
    L, width = dy.shape
    assert col0 % width == 0
    cb = col0 // width
    tm = PREFIX if split_prefix else _tile(L, (128, 64, 32, 16))

    def body(*refs):
        head_ref = None
        if split_prefix:
            refs, head_ref = refs[:-1], refs[-1]
        if res is None:
            dy_ref, x_ref, g_ref, dx_ref, dg_ref = refs
            r_ref = None
        else:
            dy_ref, x_ref, g_ref, r_ref, dx_ref, dg_ref = refs

        @pl.when(pl.program_id(0) == 0)
        def _():
            dg_ref[...] = jnp.zeros_like(dg_ref)

        xv = x_ref[...]
        dyv = dy_ref[...].astype(F32)
        r = lax.rsqrt(jnp.mean(xv * xv, axis=-1, keepdims=True) + NORM_EPS)
        z = dyv * g_ref[...]
        dx = r * z - xv * ((r * r * r) * jnp.mean(z * xv, axis=-1, keepdims=True))
        if r_ref is not None:
            dx = dx + r_ref[...]
        dg_ref[...] += jnp.sum(dyv * (xv * r), axis=0, keepdims=True)
        if head_ref is None:
            dx_ref[...] = dx.astype(out_dtype)
        else:
            @pl.when(pl.program_id(0) == 0)
            def _():
                head_ref[...] = dx.astype(out_dtype)

            @pl.when(pl.program_id(0) > 0)
            def _():
                dx_ref[...] = dx.astype(out_dtype)

    in_specs = [pl.BlockSpec((tm, width), lambda i: (i, 0)),
                pl.BlockSpec((tm, width), lambda i: (i, cb)),
                pl.BlockSpec((1, width), lambda i: (0, 0))]
    operands = [dy, x, g]
    if res is not None:
        in_specs.append(pl.BlockSpec((tm, width), lambda i: (i, 0)))
        operands.append(res)
    dg_shape, dg_spec = jax.ShapeDtypeStruct((1, width), F32), pl.BlockSpec((1, width), lambda i: (0, 0))
    if not split_prefix:
        return pl.pallas_call(
            body, name=name, out_shape=(jax.ShapeDtypeStruct((L, width), out_dtype), dg_shape), grid=(L // tm,),
            in_specs=in_specs, out_specs=(pl.BlockSpec((tm, width), lambda i: (i, 0)), dg_spec),
            compiler_params=_params("arbitrary"),
        )(*operands)
    dx_rest, dg, dx_head = pl.pallas_call(
        body, name=name,
        out_shape=(jax.ShapeDtypeStruct((L - PREFIX, width), out_dtype), dg_shape,
                   jax.ShapeDtypeStruct((PREFIX, width), out_dtype)),
        grid=(L // tm,), in_specs=in_specs,
        out_specs=(pl.BlockSpec((tm, width), lambda i: (jnp.maximum(i - 1, 0), 0)), dg_spec,
                   pl.BlockSpec((PREFIX, width), lambda i: (0, 0))),
        compiler_params=_params("arbitrary"),
    )(*operands)
    return (dx_rest, dx_head), dg


def _final_loss_bwd(h2, tgt, g):
    L, D = h2.shape
    tm = PREFIX
    inv_d = 1.0 / D

    def body(h_ref, t_ref, g_ref, dh_ref, loss_ref, dg_ref):
        i = pl.program_id(0)

        @pl.when(i == 0)
        def _():
            loss_ref[...] = jnp.zeros_like(loss_ref)
            dg_ref[...] = jnp.zeros_like(dg_ref)

        xv = h_ref[...]
        r = lax.rsqrt(jnp.mean(xv * xv, axis=-1, keepdims=True) + NORM_EPS)
        xn = xv * r
        y = xn * g_ref[...]
        real = (i >= PREFIX // tm).astype(F32)
        diff = (y - t_ref[...]) * real
        loss_ref[...] += 0.5 * inv_d * jnp.sum(diff * diff)
        dyv = diff * inv_d
        z = dyv * g_ref[...]
        dh_ref[...] = r * z - xv * ((r * r * r) * jnp.mean(z * xv, axis=-1, keepdims=True))
        dg_ref[...] += jnp.sum(dyv * xn, axis=0, keepdims=True)

    shift = PREFIX // tm
    return pl.pallas_call(
        body, name="final_loss_bwd",
        out_shape=(jax.ShapeDtypeStruct((L, D), F32), jax.ShapeDtypeStruct((1, 128), F32),
                   jax.ShapeDtypeStruct((1, D), F32)),
        grid=(L // tm,),
        in_specs=[pl.BlockSpec((tm, D), lambda i: (i, 0)),
                  pl.BlockSpec((tm, D), lambda i: (jnp.maximum(i - shift, 0), 0)),
                  pl.BlockSpec((1, D), lambda i: (0, 0))],
        out_specs=(pl.BlockSpec((tm, D), lambda i: (i, 0)), pl.BlockSpec((1, 128), lambda i: (0, 0)),
                   pl.BlockSpec((1, D), lambda i: (0, 0))),
        compiler_params=_params("arbitrary"),
    )(h2, tgt, g)


def _rot_half(x):
    lane = lax.broadcasted_iota(jnp.int32, x.shape, 1)
    return jnp.where(lane < ROPE // 2, -pltpu.roll(x, 128 - ROPE // 2, 1), pltpu.roll(x, ROPE // 2, 1))


def _rope_fwd(qp, kv, kr, cos, sin, heads):
    L = qp.shape[0]
    tm = _tile(L, (128,))

    def body(q_ref, kv_ref, kr_ref, c_ref, s_ref, qc_ref, kc_ref, v_ref):
        c, s = c_ref[...], s_ref[...]
        krv = kr_ref[...]
        kr_rot = (krv * c + _rot_half(krv) * s).astype(BF16)
        for h in range(heads):
            lo = h * QPAD
            qc_ref[:, lo:lo + HEAD] = (q_ref[:, lo:lo + HEAD].astype(F32) * SOFTMAX_SCALE).astype(BF16)
            qr = q_ref[:, lo + HEAD:lo + QPAD].astype(F32)
            qc_ref[:, lo + HEAD:lo + QPAD] = ((qr * c + _rot_half(qr) * s) * SOFTMAX_SCALE).astype(BF16)
            kc_ref[:, lo:lo + HEAD] = kv_ref[:, lo:lo + HEAD].astype(BF16)
            kc_ref[:, lo + HEAD:lo + QPAD] = kr_rot
            v_ref[:, h * HEAD:(h + 1) * HEAD] = kv_ref[:, lo + HEAD:lo + QPAD].astype(BF16)

    W = heads * QPAD
    row = lambda w: pl.BlockSpec((tm, w), lambda i: (i, 0))
    return pl.pallas_call(
        body, name="rope_fwd",
        out_shape=(jax.ShapeDtypeStruct((L, W), BF16), jax.ShapeDtypeStruct((L, W), BF16),
                   jax.ShapeDtypeStruct((L, heads * HEAD), BF16)),
        grid=(L // tm,),
        in_specs=[row(W), row(W), row(128), row(128), row(128)],
        out_specs=(row(W), row(W), row(heads * HEAD)),
        compiler_params=_params("parallel"),
    )(qp, kv, kr, cos, sin)


def _rope_bwd(dqc, dkc, dv, cos, sin, heads):
    L = dqc.shape[0]
    tm = _tile(L, (128,))

    def body(dq_ref, dk_ref, dv_ref, c_ref, s_ref, dqp_ref, dkv_ref, dkr_ref):
        c, s = c_ref[...], s_ref[...]
        acc = jnp.zeros((tm, 128), F32)
        for h in range(heads):
            lo = h * QPAD
            dqp_ref[:, lo:lo + HEAD] = (dq_ref[:, lo:lo + HEAD] * SOFTMAX_SCALE).astype(BF16)
            d = dq_ref[:, lo + HEAD:lo + QPAD]
            dqp_ref[:, lo + HEAD:lo + QPAD] = ((d * c - _rot_half(d) * s) * SOFTMAX_SCALE).astype(BF16)
            dkv_ref[:, lo:lo + HEAD] = dk_ref[:, lo:lo + HEAD].astype(BF16)
            dkv_ref[:, lo + HEAD:lo + QPAD] = dv_ref[:, h * HEAD:(h + 1) * HEAD].astype(BF16)
            acc = acc + dk_ref[:, lo + HEAD:lo + QPAD]
        dkr_ref[...] = (acc * c - _rot_half(acc) * s).astype(BF16)

    W = heads * QPAD
    row = lambda w: pl.BlockSpec((tm, w), lambda i: (i, 0))
    return pl.pallas_call(
        body, name="rope_bwd",
        out_shape=(jax.ShapeDtypeStruct((L, W), BF16), jax.ShapeDtypeStruct((L, W), BF16),
                   jax.ShapeDtypeStruct((L, 128), BF16)),
        grid=(L // tm,),
        in_specs=[row(W), row(W), row(heads * HEAD), row(128), row(128)],
        out_specs=(row(W), row(W), row(128)),
        compiler_params=_params("parallel"),
    )(dqc, dkc, dv, cos, sin)


def _attn_keep(qi, ki, ta):
    t = qi * ta + lax.broadcasted_iota(jnp.int32, (ta, ta), 0)
    s = ki * ta + lax.broadcasted_iota(jnp.int32, (ta, ta), 1)
    return (s <= t) & ((s >= PAD_LEN) | (s == t))


def _split_refs(refs, sizes):
    out, at = [], 0
    for n in sizes:
        out.append(refs[at:at + n])
        at += n
    return out


def _attn_fwd(qc, kc, v, heads, rider):
    L = qc.shape[0]
    ta = _tile(L, (640, 128))
    nb = L // ta
    pairs = [(i, j) for i in range(nb) for j in range(i + 1)]
    q_of = jnp.asarray([p[0] for p in pairs], jnp.int32)
    k_of = jnp.asarray([p[1] for p in pairs], jnp.int32)
    n_rin, n_rout = len(rider.operands), len(rider.out_shapes)
    per = _tile(heads, (ATTN_FWD_HEADS, 1))
    ng = heads // per

    def body(*refs):
        (q_of_ref, k_of_ref, q_ref, k_ref, v_ref), r_ins, (o_ref, lse_ref), r_outs, (m_sc, l_sc, acc_sc), r_sems = \
            _split_refs(refs, (5, n_rin, 2, n_rout, 3, len(rider.sem_shapes)))
        h, t = pl.program_id(0), pl.program_id(1)
        qi, ki = q_of_ref[t], k_of_ref[t]
        pl.when((h == 0) & (t == 0))(lambda: rider.start(r_ins, r_outs, r_sems))

        @pl.when(ki == 0)
        def _():
            m_sc[...] = jnp.full_like(m_sc, NEG)
            l_sc[...] = jnp.zeros_like(l_sc)
            acc_sc[...] = jnp.zeros_like(acc_sc)

        def step(masked):
            wide = lambda hh: slice(hh * QPAD, (hh + 1) * QPAD)
            lanes = lambda hh: slice(hh * HEAD, (hh + 1) * HEAD)
            ss = [_dot(q_ref[:, wide(hh)], k_ref[:, wide(hh)], NT) for hh in range(per)]
            if masked:
                keep = _attn_keep(qi, ki, ta)
                ss = [jnp.where(keep, s, NEG) for s in ss]
            for hh in range(per):
                m_old = m_sc[hh]
                m_new = jnp.maximum(m_old, jnp.max(ss[hh], axis=-1, keepdims=True))
                p = jnp.exp(ss[hh] - jnp.tile(m_new, (1, ta // HEAD)))
                alpha = jnp.exp(m_old - m_new)
                l_sc[hh] = alpha * l_sc[hh] + jnp.sum(p, axis=-1, keepdims=True)
                acc_sc[hh] = alpha * acc_sc[hh] + _dot(p, v_ref[:, lanes(hh)], NN)
                m_sc[hh] = m_new

        pl.when((ki == qi) | ((ki == 0) & (qi > 0)))(lambda: step(True))
        pl.when((ki > 0) & (ki < qi))(lambda: step(False))

        @pl.when(ki == qi)
        def _():
            for hh in range(per):
                l = l_sc[hh]
                o_ref[:, hh * HEAD:(hh + 1) * HEAD] = (acc_sc[hh] / l).astype(BF16)
                lse_ref[:, hh * HEAD:(hh + 1) * HEAD] = m_sc[hh] + jnp.log(l)

        pl.when((h == ng - 1) & (t == len(pairs) - 1))(lambda: rider.finish(r_ins, r_outs, r_sems))

    qrow = lambda w: pl.BlockSpec((ta, per * w), lambda h, t, q_of, k_of: (q_of[t], h))
    krow = lambda w: pl.BlockSpec((ta, per * w), lambda h, t, q_of, k_of: (k_of[t], h))
    stat = pltpu.VMEM((per, ta, HEAD), F32)
    res = pl.pallas_call(
        body, name="attn_fwd",
        out_shape=(jax.ShapeDtypeStruct((L, heads * HEAD), BF16), jax.ShapeDtypeStruct((L, heads * HEAD), F32),
                   *rider.out_shapes),
        grid_spec=pltpu.PrefetchScalarGridSpec(
            num_scalar_prefetch=2, grid=(ng, len(pairs)),
            in_specs=[qrow(QPAD), krow(QPAD), krow(HEAD)] + [ANY] * n_rin,
            out_specs=(qrow(HEAD), qrow(HEAD), *([ANY] * n_rout)),
            scratch_shapes=[stat, stat, stat] + rider.sem_shapes),
        compiler_params=pltpu.CompilerParams(dimension_semantics=("arbitrary", "arbitrary"),
                                             vmem_limit_bytes=VMEM_LIMIT_BYTES, has_side_effects=True),
    )(q_of, k_of, qc, kc, v, *rider.operands)
    return res[0], res[1], res[2:]


def _attn_delta(o, do, heads):
    L = o.shape[0]
    tm = _tile(L, (128,))

    def body(o_ref, do_ref, d_ref):
        for h in range(heads):
            cols = slice(h * HEAD, (h + 1) * HEAD)
            d = jnp.sum(do_ref[:, cols].astype(F32) * o_ref[:, cols].astype(F32), axis=-1, keepdims=True)
            d_ref[:, cols] = jnp.broadcast_to(d, (tm, HEAD))

    row = pl.BlockSpec((tm, heads * HEAD), lambda i: (i, 0))
    return pl.pallas_call(
        body, name="attn_delta", out_shape=jax.ShapeDtypeStruct((L, heads * HEAD), F32), grid=(L // tm,),
        in_specs=[row, row], out_specs=row, compiler_params=_params("parallel"),
    )(o, do)


def _attn_bwd(qc, kc, v, do, lse, delta, heads, rider):
    L = qc.shape[0]
    ta = _tile(L, (640, 128))
    nb = L // ta
    pairs = [(j, i) for j in range(nb) for i in range(j, nb)]
    k_of = jnp.asarray([p[0] for p in pairs], jnp.int32)
    q_of = jnp.asarray([p[1] for p in pairs], jnp.int32)
    n_rin, n_rout = len(rider.operands), len(rider.out_shapes)
    per = _tile(heads, (ATTN_BWD_HEADS, 1))
    ng = heads // per

    def body(*refs):
        ((k_of_ref, q_of_ref, q_ref, k_ref, v_ref, do_ref, lse_ref, delta_ref), r_ins, (dq_hbm, dk_ref, dv_ref),
         r_outs, (dq_sc, dq_sem), r_sems) = _split_refs(refs, (8, n_rin, 3, n_rout, 2, len(rider.sem_shapes)))
        h, t = pl.program_id(0), pl.program_id(1)
        kj, qi = k_of_ref[t], q_of_ref[t]
        pl.when((h == 0) & (t == 0))(lambda: rider.start(r_ins, r_outs, r_sems))

        @pl.when(t == 0)
        def _():
            dq_sc[...] = jnp.zeros_like(dq_sc)

        @pl.when(qi == kj)
        def _():
            dk_ref[...] = jnp.zeros_like(dk_ref)
            dv_ref[...] = jnp.zeros_like(dv_ref)

        def step(masked):
            wide = lambda hh: slice(hh * QPAD, (hh + 1) * QPAD)
            lanes = lambda hh: slice(hh * HEAD, (hh + 1) * HEAD)
            rep = (1, ta // HEAD)
            rows = pl.ds(pl.multiple_of(qi * ta, ta), ta)
            ss = [_dot(q_ref[:, wide(hh)], k_ref[:, wide(hh)], NT) for hh in range(per)]
            dps = [_dot(do_ref[:, lanes(hh)], v_ref[:, lanes(hh)], NT) for hh in range(per)]
            ps = [jnp.exp(ss[hh] - jnp.tile(lse_ref[:, lanes(hh)], rep)) for hh in range(per)]
            if masked:
                keep = _attn_keep(qi, kj, ta)
                ps = [jnp.where(keep, p, 0.0) for p in ps]
            dss = [ps[hh] * (dps[hh] - jnp.tile(delta_ref[:, lanes(hh)], rep)) for hh in range(per)]
            for hh in range(per):
                dv_ref[:, lanes(hh)] += _dot(ps[hh], do_ref[:, lanes(hh)], TN)
                dk_ref[:, wide(hh)] += _dot(dss[hh], q_ref[:, wide(hh)], TN)
                dq_sc[rows, wide(hh)] += _dot(dss[hh], k_ref[:, wide(hh)], NN)

        pl.when((qi == kj) | ((kj == 0) & (qi > 0)))(lambda: step(True))
        pl.when((kj > 0) & (qi > kj))(lambda: step(False))

        @pl.when(t == len(pairs) - 1)
        def _():
            cols = pl.ds(pl.multiple_of(h * (per * QPAD), per * QPAD), per * QPAD)
            out = pltpu.make_async_copy(dq_sc, dq_hbm.at[:, cols], dq_sem)
            out.start()
            out.wait()

        pl.when((h == ng - 1) & (t == len(pairs) - 1))(lambda: rider.finish(r_ins, r_outs, r_sems))

    qrow = lambda w: pl.BlockSpec((ta, per * w), lambda h, t, k_of, q_of: (q_of[t], h))
    krow = lambda w: pl.BlockSpec((ta, per * w), lambda h, t, k_of, q_of: (k_of[t], h))
    res = pl.pallas_call(
        body, name="attn_bwd",
        out_shape=(jax.ShapeDtypeStruct((L, heads * QPAD), F32), jax.ShapeDtypeStruct((L, heads * QPAD), F32),
                   jax.ShapeDtypeStruct((L, heads * HEAD), F32), *rider.out_shapes),
        grid_spec=pltpu.PrefetchScalarGridSpec(
            num_scalar_prefetch=2, grid=(ng, len(pairs)),
            in_specs=[qrow(QPAD), krow(QPAD), krow(HEAD), qrow(HEAD), qrow(HEAD), qrow(HEAD)] + [ANY] * n_rin,
            out_specs=(ANY, krow(QPAD), krow(HEAD), *([ANY] * n_rout)),
            scratch_shapes=[pltpu.VMEM((L, per * QPAD), F32), pltpu.SemaphoreType.DMA] + rider.sem_shapes),
        compiler_params=pltpu.CompilerParams(dimension_semantics=("arbitrary", "arbitrary"),
                                             vmem_limit_bytes=VMEM_LIMIT_BYTES, has_side_effects=True),
    )(k_of, q_of, qc, kc, v, do, lse, delta, *rider.operands)
    return res[0], res[1], res[2], res[3:]


def _hgrn_constants():
    t = jnp.arange(CHUNK)
    tril = (t[None, :] <= t[:, None]).astype(BF16)
    sel, blk = [], []
    for lv in range(LEVELS):
        hs = 1 << lv
        mid = (t // (2 * hs)) * (2 * hs) + hs - 1
        sel.append((t[None, :] == mid[:, None]).astype(BF16))
        blk.append(((t[:, None] // (2 * hs)) == (t[None, :] // (2 * hs))).astype(F32))
    return tril, jnp.concatenate(sel, axis=0), jnp.concatenate(blk, axis=0)


def _hgrn_decay_grad_constants():
    r = jnp.arange(CHUNK)[:, None]
    c = jnp.arange(CHUNK)[None, :]
    mats = []
    for lv in range(LEVELS):
        same = (r >> (lv + 1)) == (c >> (lv + 1))
        second = ((r >> lv) & 1) == 1
        mats.append(same & jnp.where(second, c >= r, c < r))
    mats += [c >= r, c < r]
    return jnp.concatenate(mats, axis=1).astype(BF16)


def _hgrn_gates(hq_ref, hf_ref, hi_ref, lb_ref, cols, row0):
    rows = row0 + lax.broadcasted_iota(jnp.int32, (CHUNK, 1), 0)
    valid = rows >= PAD_LEN
    lb = 1.0 / (1.0 + jnp.exp(lb_ref[1:2, cols] - lb_ref[0:1, cols]))
    hq = hq_ref[:, cols]
    sq = _sigmoid(hq)
    sg = _sigmoid(hf_ref[:, cols])
    f = lb + (1.0 - lb) * sg
    g = jnp.where(valid, jnp.log(f), 0.0)
    k = jnp.where(valid, 1.0 - f, 0.0)
    return dict(q=hq * sq, sq=sq, hq=hq, k=k, v=hi_ref[:, cols], g=g, f=f, sg=sg, lb=lb, valid=valid)


def _hgrn_prefix(tril_ref, sel_ref, gs):
    n = len(gs)
    b_all = _dot_exact_rhs(tril_ref[...], jnp.concatenate(gs, axis=1) if n > 1 else gs[0], NN)
    bm_all = _dot_exact_rhs(sel_ref[...], b_all, NN)
    cut = lambda a, i: a[:, i * HEAD:(i + 1) * HEAD]
    return [cut(b_all, i) for i in range(n)], [cut(bm_all, i) for i in range(n)]


def _hgrn_levels(q, k, b, bm_all, blk_ref):
    t = lax.broadcasted_iota(jnp.int32, (CHUNK, 1), 0)
    out = []
    for lv in range(LEVELS):
        bm = bm_all[lv * CHUNK:(lv + 1) * CHUNK, :]
        second = ((t >> lv) & 1) == 1
        eq = jnp.where(second, jnp.exp(jnp.minimum(b - bm, 0.0)), 0.0)
        ek = jnp.where(second, 0.0, jnp.exp(jnp.minimum(bm - b, 0.0)))
        same = blk_ref[lv * CHUNK:(lv + 1) * CHUNK, :]
        out.append((eq, ek, (q * eq).astype(BF16), (k * ek).astype(BF16), same))
    return out


def _hgrn_intra(qs, ks, levels):
    tt = lax.broadcasted_iota(jnp.int32, (CHUNK, CHUNK), 0)
    ss = lax.broadcasted_iota(jnp.int32, (CHUNK, CHUNK), 1)
    ps = [jnp.where(tt == ss, jnp.sum(q * k, axis=-1, keepdims=True), 0.0) for q, k in zip(qs, ks)]
    for lv in range(LEVELS):
        for i, lvl in enumerate(levels):
            _, _, ql, kl, same = lvl[lv]
            ps[i] = ps[i] + _dot(ql, kl, NT) * same
    return ps


def _hgrn_fwd(hp, lb_raw, g_norm, nh):
    L = hp.shape[0]
    D = nh * HEAD
    nc = L // CHUNK
    per = _tile(nh, (HGRN_FWD_HEADS, 2, 1))
    ng = nh // per
    tril, sel, blk = _hgrn_constants()

    def body(hq_ref, hf_ref, hi_ref, hg_ref, lb_ref, gn_ref, tril_ref, sel_ref, blk_ref,
             oh_ref, orec_ref, shist_ref, s_sc, b_sc):
        c = pl.program_id(1)

        @pl.when(c == 0)
        def _():
            s_sc[...] = jnp.zeros_like(s_sc)

        heads_here = range(per)
        lanes = [slice(hh * HEAD, (hh + 1) * HEAD) for hh in heads_here]
        ws = [_hgrn_gates(hq_ref, hf_ref, hi_ref, lb_ref, lanes[hh], c * CHUNK) for hh in heads_here]
        qs, ks, vs = [w["q"] for w in ws], [w["k"] for w in ws], [w["v"] for w in ws]
        bs, bms = _hgrn_prefix(tril_ref, sel_ref, [w["g"] for w in ws])
        for hh in heads_here:
            b_sc[hh] = bs[hh]
        b_lasts = [b_sc[hh, CHUNK - 1:CHUNK, :] for hh in heads_here]
        ps = _hgrn_intra(qs, ks, [_hgrn_levels(qs[hh], ks[hh], bs[hh], bms[hh], blk_ref) for hh in heads_here])
        s_ins = [s_sc[hh] for hh in heads_here]
        os_ = [_dot(ps[hh], vs[hh], NN) + _dot(qs[hh] * jnp.exp(bs[hh]), s_ins[hh], NT) for hh in heads_here]
        for hh in heads_here:
            shist_ref[0, hh] = s_ins[hh]
            s_sc[hh] = (jnp.exp(b_lasts[hh]) * s_ins[hh]
                        + _dot(vs[hh], ks[hh] * jnp.exp(b_lasts[hh] - bs[hh]), TN))
        for hh in heads_here:
            o = os_[hh]
            orec_ref[:, lanes[hh]] = o
            rn = lax.rsqrt(jnp.mean(o * o, axis=-1, keepdims=True) + NORM_EPS)
            hg = hg_ref[:, lanes[hh]]
            oh_ref[:, lanes[hh]] = (((o * rn) * gn_ref[...]) * (hg * _sigmoid(hg))).astype(BF16)

    col = lambda grp: pl.BlockSpec((CHUNK, per * HEAD), lambda h, c: (c, grp * ng + h))
    const = lambda shape: pl.BlockSpec(shape, lambda h, c: (0, 0))
    return pl.pallas_call(
        body, name="hgrn_fwd",
        out_shape=(jax.ShapeDtypeStruct((L, D), BF16), jax.ShapeDtypeStruct((L, D), F32),
                   jax.ShapeDtypeStruct((nc, nh, HEAD, HEAD), F32)),
        grid=(ng, nc),
        in_specs=[col(0), col(1), col(2), col(3),
                  pl.BlockSpec((2, per * HEAD), lambda h, c: (0, h)), const((1, HEAD)),
                  const((CHUNK, CHUNK)), const((LEVELS * CHUNK, CHUNK)), const((LEVELS * CHUNK, CHUNK))],
        out_specs=(pl.BlockSpec((CHUNK, per * HEAD), lambda h, c: (c, h)),
                   pl.BlockSpec((CHUNK, per * HEAD), lambda h, c: (c, h)),
                   pl.BlockSpec((1, per, HEAD, HEAD), lambda h, c: (c, h, 0, 0))),
        scratch_shapes=[pltpu.VMEM((per, HEAD, HEAD), F32), pltpu.VMEM((per, CHUNK, HEAD), F32)],
        compiler_params=_params("parallel", "arbitrary"),
    )(hp, hp, hp, hp, lb_raw, g_norm, tril, sel, blk)


def _hgrn_bwd(hp, lb_raw, g_norm, do_h, o_rec, s_hist, nh):
    L = hp.shape[0]
    D = nh * HEAD
    nc = L // CHUNK
    per = _tile(nh, (HGRN_BWD_HEADS, 1))
    ng = nh // per
    tril, sel, blk = _hgrn_constants()
    tdec = _hgrn_decay_grad_constants()

    def body(hq_ref, hf_ref, hi_ref, hg_ref, lb_ref, gn_ref, tril_ref, sel_ref, blk_ref, tdec_ref,
             do_ref, orec_ref, shist_ref, dhq_ref, dhf_ref, dhi_ref, dhg_ref, dgn_ref, dlb_ref, ds_sc, b_sc):
        ci = pl.program_id(1)
        c = nc - 1 - ci

        @pl.when(ci == 0)
        def _():
            ds_sc[...] = jnp.zeros_like(ds_sc)
            dgn_ref[...] = jnp.zeros_like(dgn_ref)
            dlb_ref[...] = jnp.zeros_like(dlb_ref)

        heads_here = range(per)
        lanes = [slice(hh * HEAD, (hh + 1) * HEAD) for hh in heads_here]
        ws = [_hgrn_gates(hq_ref, hf_ref, hi_ref, lb_ref, lanes[hh], c * CHUNK) for hh in heads_here]
        qs, ks, vs = [w["q"] for w in ws], [w["k"] for w in ws], [w["v"] for w in ws]
        bs, bms = _hgrn_prefix(tril_ref, sel_ref, [w["g"] for w in ws])
        for hh in heads_here:
            b_sc[hh] = bs[hh]
        b_lasts = [b_sc[hh, CHUNK - 1:CHUNK, :] for hh in heads_here]
        levels = [_hgrn_levels(qs[hh], ks[hh], bs[hh], bms[hh], blk_ref) for hh in heads_here]
        ps = _hgrn_intra(qs, ks, levels)
        s_ins = [shist_ref[0, hh] for hh in heads_here]
        ds_outs = [ds_sc[hh] for hh in heads_here]
        ebs = [jnp.exp(b) for b in bs]
        etails = [jnp.exp(b_lasts[hh] - bs[hh]) for hh in heads_here]
        decays = [jnp.exp(bl) for bl in b_lasts]

        dos = []
        for hh in heads_here:
            o = orec_ref[:, lanes[hh]]
            hg = hg_ref[:, lanes[hh]]
            sgg = _sigmoid(hg)
            rn = lax.rsqrt(jnp.mean(o * o, axis=-1, keepdims=True) + NORM_EPS)
            on = o * rn
            doh = do_ref[:, lanes[hh]]
            dy = doh * (hg * sgg)
            dhg_ref[:, lanes[hh]] = (doh * (on * gn_ref[...]) * (sgg * (1.0 + hg * (1.0 - sgg)))).astype(BF16)
            dgn_ref[hh] += jnp.broadcast_to(jnp.sum(dy * on, axis=0, keepdims=True), (8, HEAD))
            z = dy * gn_ref[...]
            dos.append(rn * z - o * ((rn * rn * rn) * jnp.mean(z * o, axis=-1, keepdims=True)))

        tt = lax.broadcasted_iota(jnp.int32, (CHUNK, CHUNK), 0)
        ss = lax.broadcasted_iota(jnp.int32, (CHUNK, CHUNK), 1)
        dps = [jnp.where(ss <= tt, _dot(dos[hh], vs[hh], NT), 0.0) for hh in heads_here]
        dvs = [_dot(ps[hh], dos[hh], TN) + _dot(ks[hh] * etails[hh], ds_outs[hh], NT) for hh in heads_here]
        dq_states = [ebs[hh] * _dot(dos[hh], s_ins[hh], NN) for hh in heads_here]
        dk_states = [etails[hh] * _dot(vs[hh], ds_outs[hh], NN) for hh in heads_here]
        dpds = [jnp.sum(jnp.where(tt == ss, dp, 0.0), axis=-1, keepdims=True) for dp in dps]
        dqs = [dpds[hh] * ks[hh] + dq_states[hh] for hh in heads_here]
        dks = [dpds[hh] * qs[hh] + dk_states[hh] for hh in heads_here]
        pair_terms = [[] for _ in heads_here]
        for lv in range(LEVELS):
            for hh in heads_here:
                eq, ek, ql, kl, same = levels[hh][lv]
                dpl = dps[hh] * same
                dq_l = eq * _dot(dpl, kl, NN)
                dk_l = ek * _dot(dpl, ql, TN)
                dqs[hh] = dqs[hh] + dq_l
                dks[hh] = dks[hh] + dk_l
                pair_terms[hh].append(qs[hh] * dq_l + ks[hh] * dk_l)
        for hh in heads_here:
            pair_terms[hh] += [qs[hh] * dq_states[hh], ks[hh] * dk_states[hh]]
            ds_sc[hh] = decays[hh] * ds_outs[hh] + _dot(dos[hh], qs[hh] * ebs[hh], TN)
        stacked = [jnp.concatenate(terms, axis=0) for terms in pair_terms]
        dg_all = _dot_exact_rhs(tdec_ref[...], jnp.concatenate(stacked, axis=1) if per > 1 else stacked[0], NN)

        for hh in heads_here:
            w = ws[hh]
            f, sg, lb, sq, hq = w["f"], w["sg"], w["lb"], w["sq"], w["hq"]
            through = jnp.sum((decays[hh] * s_ins[hh]) * ds_outs[hh], axis=0, keepdims=True)
            dg = dg_all[:, lanes[hh]] + through
            df = jnp.where(w["valid"], dg / f - dks[hh], 0.0)
            dhf_ref[:, lanes[hh]] = (df * (1.0 - lb) * sg * (1.0 - sg)).astype(BF16)
            dlb_ref[hh] += jnp.broadcast_to(jnp.sum(df * (1.0 - sg), axis=0, keepdims=True), (8, HEAD))
            dhq_ref[:, lanes[hh]] = (dqs[hh] * (sq * (1.0 + hq * (1.0 - sq)))).astype(BF16)
            dhi_ref[:, lanes[hh]] = dvs[hh].astype(BF16)

    col = lambda grp: pl.BlockSpec((CHUNK, per * HEAD), lambda h, c: (nc - 1 - c, grp * ng + h))
    const = lambda shape: pl.BlockSpec(shape, lambda h, c: (0, 0))
    tile = pl.BlockSpec((CHUNK, per * HEAD), lambda h, c: (nc - 1 - c, h))
    part = pl.BlockSpec((per, 8, HEAD), lambda h, c: (h, 0, 0))
    return pl.pallas_call(
        body, name="hgrn_bwd",
        out_shape=tuple([jax.ShapeDtypeStruct((L, D), BF16)] * 4 + [jax.ShapeDtypeStruct((nh, 8, HEAD), F32)] * 2),
        grid=(ng, nc),
        in_specs=[col(0), col(1), col(2), col(3),
                  pl.BlockSpec((2, per * HEAD), lambda h, c: (0, h)), const((1, HEAD)),
                  const((CHUNK, CHUNK)), const((LEVELS * CHUNK, CHUNK)), const((LEVELS * CHUNK, CHUNK)),
                  const((CHUNK, (LEVELS + 2) * CHUNK)),
                  tile, tile, pl.BlockSpec((1, per, HEAD, HEAD), lambda h, c: (nc - 1 - c, h, 0, 0))],
        out_specs=(tile, tile, tile, tile, part, part),
        scratch_shapes=[pltpu.VMEM((per, HEAD, HEAD), F32), pltpu.VMEM((per, CHUNK, HEAD), F32)],
        compiler_params=_params("parallel", "arbitrary"),
    )(hp, hp, hp, hp, lb_raw, g_norm, tril, sel, blk, tdec, do_h, o_rec, s_hist)


def _merge_fwd(a, bm, gates):
    L, D = a.shape
    tm = _tile(L, (128,))

    def body(a_ref, b_ref, g_ref, o_ref):
        o_ref[...] = (_sigmoid(g_ref[:, :D]) * a_ref[...] + _sigmoid(g_ref[:, D:]) * b_ref[...]).astype(BF16)

    row = lambda w: pl.BlockSpec((tm, w), lambda i: (i, 0))
    return pl.pallas_call(
        body, name="merge_fwd", out_shape=jax.ShapeDtypeStruct((L, D), BF16), grid=(L // tm,),
        in_specs=[row(D), row(D), row(2 * D)], out_specs=row(D), compiler_params=_params("parallel"),
    )(a, bm, gates)


def _merge_bwd(dm, a, bm, gates):
    L, D = a.shape
    tm = _tile(L, (128,))

    def body(dm_ref, a_ref, b_ref, g_ref, da_ref, db_ref, dg_ref):
        d = dm_ref[...]
        sa, sb = _sigmoid(g_ref[:, :D]), _sigmoid(g_ref[:, D:])
        da_ref[...] = (d * sa).astype(BF16)
        db_ref[...] = (d * sb).astype(BF16)
        dg_ref[:, :D] = (d * a_ref[...] * sa * (1.0 - sa)).astype(BF16)
        dg_ref[:, D:] = (d * b_ref[...] * sb * (1.0 - sb)).astype(BF16)

    row = lambda w: pl.BlockSpec((tm, w), lambda i: (i, 0))
    return pl.pallas_call(
        body, name="merge_bwd",
        out_shape=(jax.ShapeDtypeStruct((L, D), BF16), jax.ShapeDtypeStruct((L, D), BF16),
                   jax.ShapeDtypeStruct((L, 2 * D), BF16)),
        grid=(L // tm,),
        in_specs=[row(D), row(D), row(D), row(2 * D)], out_specs=(row(D), row(D), row(2 * D)),
        compiler_params=_params("parallel"),
    )(dm, a, bm, gates)


def _conv_taps(g_ref, halo_ref, i, tm):
    rows = i * tm + lax.broadcasted_iota(jnp.int32, (tm, 1), 0)
    g0 = jnp.where(rows >= PAD_LEN, g_ref[...].astype(F32), 0.0)
    sub = lax.broadcasted_iota(jnp.int32, (HALO, 1), 0)
    hrow = i * tm - HALO + sub
    halo = jnp.where(hrow >= PAD_LEN, halo_ref[...].astype(F32), 0.0)
    r = lax.broadcasted_iota(jnp.int32, (tm, 1), 0)
    h7 = jnp.sum(jnp.where(sub == HALO - 1, halo, 0.0), axis=0, keepdims=True)
    h6 = jnp.sum(jnp.where(sub == HALO - 2, halo, 0.0), axis=0, keepdims=True)
    g1 = jnp.where(r == 0, h7, pltpu.roll(g0, 1, 0))
    g2 = jnp.where(r == 0, h6, jnp.where(r == 1, h7, pltpu.roll(g0, 2, 0)))
    return g0, g1, g2


def _conv_fwd(gu, cw, cb):
    L, F2 = gu.shape
    F = F2 // 2
    tm = _tile(L, ROW_TILES)
    tn = _tile(F, (512, 256, 128))
    nj = F // tn

    def body(g_ref, halo_ref, u_ref, cw_ref, cb_ref, o_ref):
        g0, g1, g2 = _conv_taps(g_ref, halo_ref, pl.program_id(0), tm)
        conv = cw_ref[0:1, :] * g2 + cw_ref[1:2, :] * g1 + cw_ref[2:3, :] * g0 + cb_ref[...]
        o_ref[...] = (conv * _sigmoid(conv) * u_ref[...].astype(F32)).astype(BF16)

    return pl.pallas_call(
        body, name="conv_fwd", out_shape=jax.ShapeDtypeStruct((L, F), BF16), grid=(L // tm, nj),
        in_specs=[pl.BlockSpec((tm, tn), lambda i, j: (i, j)),
                  pl.BlockSpec((HALO, tn), lambda i, j: (jnp.maximum(i * (tm // HALO) - 1, 0), j)),
                  pl.BlockSpec((tm, tn), lambda i, j: (i, j + nj)),
                  pl.BlockSpec((3, tn), lambda i, j: (0, j)),
                  pl.BlockSpec((1, tn), lambda i, j: (0, j))],
        out_specs=pl.BlockSpec((tm, tn), lambda i, j: (i, j)),
        compiler_params=_params("parallel", "parallel"),
    )(gu, gu, gu, cw, cb)


def _conv_bwd_a(da, gu, cw, cb):
    L, F2 = gu.shape
    F = F2 // 2
    tm = _tile(L, ROW_TILES)
    tn = _tile(F, (512, 256, 128))
    nj = F // tn

    def body(da_ref, g_ref, halo_ref, u_ref, cw_ref, cb_ref, dc_ref, du_ref, dcb_ref, dcw_ref):
        i = pl.program_id(1)

        @pl.when(i == 0)
        def _():
            dcb_ref[...] = jnp.zeros_like(dcb_ref)
            dcw_ref[...] = jnp.zeros_like(dcw_ref)

        g0, g1, g2 = _conv_taps(g_ref, halo_ref, i, tm)
        conv = cw_ref[0:1, :] * g2 + cw_ref[1:2, :] * g1 + cw_ref[2:3, :] * g0 + cb_ref[...]
        sc = _sigmoid(conv)
        dav = da_ref[...]
        du_ref[...] = (dav * (conv * sc)).astype(BF16)
        dconv = dav * u_ref[...].astype(F32) * (sc * (1.0 + conv * (1.0 - sc)))
        dc_ref[...] = dconv
        dcb_ref[...] += jnp.sum(dconv, axis=0, keepdims=True)
        dcw_ref[0:1, :] += jnp.sum(dconv * g2, axis=0, keepdims=True)
        dcw_ref[1:2, :] += jnp.sum(dconv * g1, axis=0, keepdims=True)
        dcw_ref[2:3, :] += jnp.sum(dconv * g0, axis=0, keepdims=True)

    return pl.pallas_call(
        body, name="conv_bwd_a",
        out_shape=(jax.ShapeDtypeStruct((L, F), F32), jax.ShapeDtypeStruct((L, 2 * F), BF16),
                   jax.ShapeDtypeStruct((1, F), F32), jax.ShapeDtypeStruct((8, F), F32)),
        grid=(nj, L // tm),
        in_specs=[pl.BlockSpec((tm, tn), lambda j, i: (i, j)),
                  pl.BlockSpec((tm, tn), lambda j, i: (i, j)),
                  pl.BlockSpec((HALO, tn), lambda j, i: (jnp.maximum(i * (tm // HALO) - 1, 0), j)),
                  pl.BlockSpec((tm, tn), lambda j, i: (i, j + nj)),
                  pl.BlockSpec((3, tn), lambda j, i: (0, j)),
                  pl.BlockSpec((1, tn), lambda j, i: (0, j))],
        out_specs=(pl.BlockSpec((tm, tn), lambda j, i: (i, j)), pl.BlockSpec((tm, tn), lambda j, i: (i, j + nj)),
                   pl.BlockSpec((1, tn), lambda j, i: (0, j)), pl.BlockSpec((8, tn), lambda j, i: (0, j))),
        compiler_params=_params("parallel", "arbitrary"),
    )(da, gu, gu, gu, cw, cb)


def _conv_bwd_b(dconv, cw, dgu):
    L, F = dconv.shape
    tm = _tile(L, ROW_TILES)
    tn = _tile(F, (512, 256, 128))
    nblk8 = L // 8
    ni = L // tm

    def body(dc_ref, nxt_ref, cw_ref, dgu_ref, o_ref):
        i = pl.program_id(0)
        dc = dc_ref[...]
        nxt = jnp.where(i < ni - 1, nxt_ref[...], 0.0)
        sub = lax.broadcasted_iota(jnp.int32, (8, 1), 0)
        n0 = jnp.sum(jnp.where(sub == 0, nxt, 0.0), axis=0, keepdims=True)
        n1 = jnp.sum(jnp.where(sub == 1, nxt, 0.0), axis=0, keepdims=True)
        r = lax.broadcasted_iota(jnp.int32, (tm, 1), 0)
        d1 = jnp.where(r == tm - 1, n0, pltpu.roll(dc, tm - 1, 0))
        d2 = jnp.where(r == tm - 2, n0, jnp.where(r == tm - 1, n1, pltpu.roll(dc, tm - 2, 0)))
        dg = cw_ref[2:3, :] * dc + cw_ref[1:2, :] * d1 + cw_ref[0:1, :] * d2
        rows = i * tm + r
        o_ref[...] = jnp.where(rows >= PAD_LEN, dg, 0.0).astype(BF16)

    return pl.pallas_call(
        body, name="conv_bwd_b", out_shape=jax.ShapeDtypeStruct(dgu.shape, BF16), grid=(ni, F // tn),
        in_specs=[pl.BlockSpec((tm, tn), lambda i, j: (i, j)),
                  pl.BlockSpec((8, tn), lambda i, j: (jnp.minimum((i + 1) * (tm // 8), nblk8 - 1), j)),
                  pl.BlockSpec((3, tn), lambda i, j: (0, j)), ANY],
        out_specs=pl.BlockSpec((tm, tn), lambda i, j: (i, j)),
        input_output_aliases={3: 0},
        compiler_params=_params("parallel", "parallel"),
    )(dconv, dconv, cw, dgu)


ANY = pl.BlockSpec(memory_space=pl.ANY)


def _coords():
    return lax.axis_index("x"), lax.axis_index("y"), lax.axis_index("c")


def _flip(v, bit):
    return 1 - v if bit else v


CHIPS = [(1, 0), (0, 1), (1, 1)]
PEERS = [(dx, dy, dc) for dx in (0, 1) for dy in (0, 1) for dc in (0, 1)][1:]


class _Rider:
    def __init__(self, operands, out_shapes, sem_shapes, start, finish):
        self.operands, self.out_shapes, self.sem_shapes = list(operands), list(out_shapes), list(sem_shapes)
        self.start, self.finish = start, finish


def _run_rider(rider, name):
    n_in, n_out = len(rider.operands), len(rider.out_shapes)

    def body(*refs):
        ins, outs, sems = refs[:n_in], refs[n_in:n_in + n_out], refs[n_in + n_out:]
        rider.start(ins, outs, sems)
        rider.finish(ins, outs, sems)

    return pl.pallas_call(
        body, name=name, out_shape=tuple(rider.out_shapes),
        in_specs=[ANY] * n_in, out_specs=tuple([ANY] * n_out), scratch_shapes=rider.sem_shapes,
        compiler_params=pltpu.CompilerParams(has_side_effects=True),
    )(*rider.operands)


def _gather_rider(big, small):
    nbig, n = len(big), len(big) + len(small)
    arrays = list(big) + list(small)

    def plan(ins, outs, sems):
        ici_send, ici_recv, d2d_send, d2d_recv, local_sems = sems
        x, y, c = _coords()
        mine = 2 * x + y

        def half(w, h):
            r2 = arrays[w].shape[0] // 2
            return pl.ds(h * r2, r2)

        def ici(w, j, landing):
            px, py = _flip(x, CHIPS[j][0]), _flip(y, CHIPS[j][1])
            slot = 2 * px + py if landing else mine
            if w < nbig:
                src, dst = ins[w].at[half(w, c)], outs[w].at[slot, half(w, c)]
            else:
                src, dst = ins[w], outs[w].at[slot]
            return pltpu.make_async_remote_copy(
                src_ref=src, dst_ref=dst, send_sem=ici_send.at[w * 3 + j], recv_sem=ici_recv.at[w * 3 + j],
                device_id=(px, py, c), device_id_type=MESH)

        def d2d(w, j, landing):
            px, py = _flip(x, CHIPS[j][0]), _flip(y, CHIPS[j][1])
            mine_rows = outs[w].at[2 * px + py, half(w, c)]
            dst = outs[w].at[2 * px + py, half(w, 1 - c)] if landing else mine_rows
            return pltpu.make_async_remote_copy(
                src_ref=mine_rows, dst_ref=dst, send_sem=d2d_send.at[w * 3 + j], recv_sem=d2d_recv.at[w * 3 + j],
                device_id=(x, y, 1 - c), device_id_type=MESH)

        local = [pltpu.make_async_copy(ins[w], outs[w].at[mine], local_sems.at[w]) for w in range(n)]
        return ici, d2d, local

    def start(ins, outs, sems):
        ici, _, local = plan(ins, outs, sems)
        for cp in local:
            cp.start()
        for w in range(n):
            for j in range(3):
                ici(w, j, False).start()

    def finish(ins, outs, sems):
        ici, d2d, local = plan(ins, outs, sems)
        for w in range(n):
            for j in range(3):
                ici(w, j, True).wait_recv()
                if w < nbig:
                    d2d(w, j, False).start()
        for w in range(nbig):
            for j in range(3):
                d2d(w, j, True).wait_recv()
        for w in range(n):
            for j in range(3):
                ici(w, j, False).wait_send()
                if w < nbig:
                    d2d(w, j, False).wait_send()
        for cp in local:
            cp.wait()

    return _Rider(
        arrays, [jax.ShapeDtypeStruct((4,) + s.shape, s.dtype) for s in arrays],
        [pltpu.SemaphoreType.DMA((3 * n,)), pltpu.SemaphoreType.DMA((3 * n,)),
         pltpu.SemaphoreType.DMA((max(3 * nbig, 1),)), pltpu.SemaphoreType.DMA((max(3 * nbig, 1),)),
         pltpu.SemaphoreType.DMA((n,))],
        start, finish)


def _to_sibling(arrays, name):
    n = len(arrays)

    def body(*refs):
        ins, outs = refs[:n], refs[n:2 * n]
        send_sems, recv_sems = refs[2 * n:]
        x, y, c = _coords()

        def copy(w):
            return pltpu.make_async_remote_copy(
                src_ref=ins[w], dst_ref=outs[w], send_sem=send_sems.at[w], recv_sem=recv_sems.at[w],
                device_id=(x, y, 1 - c), device_id_type=MESH)

        for w in range(n):
            copy(w).start()
        for w in range(n):
            copy(w).wait_recv()
            copy(w).wait_send()

    return pl.pallas_call(
        body, name=name,
        out_shape=tuple(jax.ShapeDtypeStruct(a.shape, a.dtype) for a in arrays),
        in_specs=[ANY] * n, out_specs=tuple([ANY] * n),
        scratch_shapes=[pltpu.SemaphoreType.DMA((n,)), pltpu.SemaphoreType.DMA((n,))],
        compiler_params=pltpu.CompilerParams(has_side_effects=True),
    )(*arrays)


def _pair_sum(a, b, name):
    _, r, c = a.shape
    tr = _tile(r, (64, 32, 16))

    def body(a_ref, b_ref, o_ref):
        o_ref[...] = (a_ref[...].astype(F32) + b_ref[...].astype(F32)).astype(BF16)

    blk = pl.BlockSpec((4, tr, c), lambda i: (0, i, 0))
    return pl.pallas_call(
        body, name=name, out_shape=jax.ShapeDtypeStruct(a.shape, BF16), grid=(r // tr,),
        in_specs=[blk, blk], out_specs=blk, compiler_params=_params("parallel"),
    )(a, b)


def _scatter_rider(parts):
    n = len(parts)

    def plan(ins, outs, sems):
        send_sems, recv_sems, local_sems = sems
        x, y, c = _coords()
        mine = 2 * x + y

        def ici(w, j, landing):
            px, py = _flip(x, CHIPS[j][0]), _flip(y, CHIPS[j][1])
            return pltpu.make_async_remote_copy(
                src_ref=ins[w].at[2 * px + py], dst_ref=outs[w].at[2 * px + py if landing else mine],
                send_sem=send_sems.at[w * 3 + j], recv_sem=recv_sems.at[w * 3 + j],
                device_id=(px, py, c), device_id_type=MESH)

        local = [pltpu.make_async_copy(ins[w].at[mine], outs[w].at[mine], local_sems.at[w]) for w in range(n)]
        return ici, local

    def start(ins, outs, sems):
        ici, local = plan(ins, outs, sems)
        for cp in local:
            cp.start()
        for w in range(n):
            for j in range(3):
                ici(w, j, False).start()

    def finish(ins, outs, sems):
        ici, local = plan(ins, outs, sems)
        for w in range(n):
            for j in range(3):
                ici(w, j, True).wait_recv()
                ici(w, j, False).wait_send()
        for cp in local:
            cp.wait()

    return _Rider(
        parts, [jax.ShapeDtypeStruct(p.shape, p.dtype) for p in parts],
        [pltpu.SemaphoreType.DMA((3 * n,)), pltpu.SemaphoreType.DMA((3 * n,)), pltpu.SemaphoreType.DMA((n,))],
        start, finish)


def _sum4(recv, name):
    _, r, c = recv.shape
    tr = _tile(r, (64, 32, 16))

    def body(in_ref, o_ref):
        o_ref[...] = ((in_ref[0].astype(F32) + in_ref[1].astype(F32)) + in_ref[2].astype(F32)) + in_ref[3].astype(F32)

    return pl.pallas_call(
        body, name=name, out_shape=jax.ShapeDtypeStruct((r, c), F32), grid=(r // tr,),
        in_specs=[pl.BlockSpec((4, tr, c), lambda i: (0, i, 0))],
        out_specs=pl.BlockSpec((tr, c), lambda i: (i, 0)),
        compiler_params=_params("parallel"),
    )(recv)


def _allreduce_small(packed):
    R = packed.shape[0]

    def body(in_ref, o_ref, buf, send_sems, recv_sems):
        x, y, c = _coords()
        me = 4 * x + 2 * y + c
        buf[me] = in_ref[...]
        for j, (dx, dy, dc) in enumerate(PEERS):
            px, py, pc = _flip(x, dx), _flip(y, dy), _flip(c, dc)
            pltpu.make_async_remote_copy(
                src_ref=in_ref, dst_ref=buf.at[me], send_sem=send_sems.at[j], recv_sem=recv_sems.at[j],
                device_id=(px, py, pc), device_id_type=MESH).start()
        for j, (dx, dy, dc) in enumerate(PEERS):
            px, py, pc = _flip(x, dx), _flip(y, dy), _flip(c, dc)
            rc = pltpu.make_async_remote_copy(
                src_ref=in_ref, dst_ref=buf.at[4 * px + 2 * py + pc], send_sem=send_sems.at[j],
                recv_sem=recv_sems.at[j], device_id=(px, py, pc), device_id_type=MESH)
            rc.wait_recv()
            rc.wait_send()
        acc = buf[0]
        for s in range(1, 8):
            acc = acc + buf[s]
        o_ref[...] = acc

    return pl.pallas_call(
        body, name="allreduce_small", out_shape=jax.ShapeDtypeStruct((R, 128), F32),
        in_specs=[pl.BlockSpec(memory_space=pltpu.VMEM)], out_specs=pl.BlockSpec(memory_space=pltpu.VMEM),
        scratch_shapes=[pltpu.VMEM((8, R, 128), F32), pltpu.SemaphoreType.DMA((7,)), pltpu.SemaphoreType.DMA((7,))],
        compiler_params=pltpu.CompilerParams(has_side_effects=True, vmem_limit_bytes=VMEM_LIMIT_BYTES),
    )(packed)


def _adamw_math(w, g, m, v):
    m = ADAM_B1 * m + (1.0 - ADAM_B1) * g
    v = ADAM_B2 * v + (1.0 - ADAM_B2) * (g * g)
    m_hat = m / (1.0 - ADAM_B1 ** ADAM_STEP)
    v_hat = v / (1.0 - ADAM_B2 ** ADAM_STEP)
    delta = -ADAM_LR * (m_hat / (jnp.sqrt(v_hat) + ADAM_EPS) + ADAM_WD * w)
    return delta, m, v


def _adamw_big(mine, other, w, m, v, name):
    R, C = w.shape
    tr = _tile(R // 2, (128, 64, 32, 16, 8))
    nb = (R // 2) // tr

    def body(mine_ref, other_ref, w_ref, m_ref, v_ref, g_ref, d_ref, mo_ref, vo_ref):
        is_mine = (pl.program_id(0) // nb) == lax.axis_index("c")
        g = jnp.where(is_mine, mine_ref[...], other_ref[...])
        d, mn, vn = _adamw_math(w_ref[...], g, m_ref[...], v_ref[...])
        g_ref[...] = g
        d_ref[...] = d
        mo_ref[...] = mn
        vo_ref[...] = vn

    blk = pl.BlockSpec((tr, C), lambda i: (i, 0))
    half = pl.BlockSpec((tr, C), lambda i: (i % nb, 0))
    sds = jax.ShapeDtypeStruct((R, C), F32)
    return pl.pallas_call(
        body, name=name, out_shape=(sds, sds, sds, sds), grid=(2 * nb,),
        in_specs=[half, half, blk, blk, blk], out_specs=(blk, blk, blk, blk), compiler_params=_params("parallel"),
    )(mine, other, w, m, v)


def _adamw_small(items, lb_raw, dlb):
    n = len(items)
    lb_w, lb_m, lb_v = lb_raw

    def body(*refs):
        ins = refs[:4 * n]
        dlb_ref, lw_ref, lm_ref, lv_ref = refs[4 * n:4 * n + 4]
        outs = refs[4 * n + 4:]
        for t in range(n):
            g_ref, w_ref, m_ref, v_ref = ins[4 * t:4 * t + 4]
            d, mn, vn = _adamw_math(w_ref[...], g_ref[...], m_ref[...], v_ref[...])
            outs[3 * t][...] = d
            outs[3 * t + 1][...] = mn
            outs[3 * t + 2][...] = vn
        p0 = 1.0 / (1.0 + jnp.exp(lw_ref[1:2, :] - lw_ref[0:1, :]))
        g0 = dlb_ref[...] * p0 * (1.0 - p0)
        base = 3 * n
        outs[base][0:1, :] = g0
        outs[base][1:2, :] = -g0
        d, mn, vn = _adamw_math(lw_ref[...], outs[base][...], lm_ref[...], lv_ref[...])
        outs[base + 1][...] = d
        outs[base + 2][...] = mn
        outs[base + 3][...] = vn

    operands = [a for it in items for a in it] + [dlb, lb_w, lb_m, lb_v]
    out_shape = []
    for (g, w, m, v) in items:
        out_shape += [jax.ShapeDtypeStruct(w.shape, F32)] * 3
    out_shape += [jax.ShapeDtypeStruct(lb_w.shape, F32)] * 4
    vm = pl.BlockSpec(memory_space=pltpu.VMEM)
    res = pl.pallas_call(
        body, name="adamw_small", out_shape=tuple(out_shape),
        in_specs=[vm] * len(operands), out_specs=tuple([vm] * len(out_shape)),
        compiler_params=pltpu.CompilerParams(vmem_limit_bytes=VMEM_LIMIT_BYTES),
    )(*operands)
    deltas = [res[3 * t] for t in range(n)] + [res[3 * n + 1]]
    new_m = [res[3 * t + 1] for t in range(n)] + [res[3 * n + 2]]
    new_v = [res[3 * t + 2] for t in range(n)] + [res[3 * n + 3]]
    return res[3 * n], deltas, new_m, new_v


def _shard_row_half(g, by_cols, h):
    R, C = g.shape
    if by_cols:
        part = lax.dynamic_index_in_dim(g.reshape(2, R // 2, 4, C // 4), h, axis=0, keepdims=False)
        return part.transpose(1, 0, 2).astype(BF16)
    return lax.dynamic_index_in_dim(g.reshape(4, 2, R // 8, C), h, axis=1, keepdims=False).astype(BF16)


def kernel(x, positions, meta_tokens, w_in, w_q_up, w_kv_up, w_branch_mla, w_branch_hgrn, w_out, w_ffn_in, w_ffn_out, conv_w, conv_b, g_mix_norm, g_q_norm, g_kv_norm, g_hgrn_norm, g_ffn_norm, g_final_norm, lb_raw, loss_target, m_meta_tokens, m_w_in, m_w_q_up, m_w_kv_up, m_w_branch_mla, m_w_branch_hgrn, m_w_out, m_w_ffn_in, m_w_ffn_out, m_conv_w, m_conv_b, m_g_mix_norm, m_g_q_norm, m_g_kv_norm, m_g_hgrn_norm, m_g_ffn_norm, m_g_final_norm, m_lb_raw, v_meta_tokens, v_w_in, v_w_q_up, v_w_kv_up, v_w_branch_mla, v_w_branch_hgrn, v_w_out, v_w_ffn_in, v_w_ffn_out, v_conv_w, v_conv_b, v_g_mix_norm, v_g_q_norm, v_g_kv_norm, v_g_hgrn_norm, v_g_ffn_norm, v_g_final_norm, v_lb_raw):
    S, D = x.shape[1], x.shape[2]
    L = S + PREFIX
    QL, KVL = g_q_norm.shape[1], g_kv_norm.shape[1]
    F = conv_b.shape[1]
    heads = (4 * w_kv_up.shape[2]) // QPAD
    nh = D // HEAD
    assert lb_raw.shape[0] == 2 and g_hgrn_norm.shape[1] == HEAD and L % CHUNK == 0
    ix, iy, ic = _coords()
    chip = 2 * ix + iy

    big = [w_in, w_q_up, w_kv_up, w_branch_mla, w_branch_hgrn, w_out, w_ffn_in, w_ffn_out]
    col_sharded = [True, True, True, False, False, False, True, False]
    shards = [w[0].astype(BF16) for w in big]
    early = _run_rider(_gather_rider(shards[:3], [meta_tokens, conv_w[0]]), "gather_early")
    late_rider = _gather_rider(shards[3:], [])

    def full(gw, by_cols):
        _, r, c = gw.shape
        return gw.transpose(1, 0, 2).reshape(r, 4 * c) if by_cols else gw.reshape(4 * r, c)

    W_in, W_q, W_kv = [full(gw, True) for gw in early[:3]]
    meta_full = full(early[3], True)
    cw_full = full(early[4], True)
    c0 = QL + KVL
    W_lat = W_in[:, :c0]
    W_kr = jnp.pad(W_in[:, c0:c0 + ROPE], ((0, 0), (0, 128 - ROPE)))
    W_H = W_in[:, c0 + ROPE:c0 + ROPE + 4 * D]
    W_G = W_in[:, c0 + ROPE + 4 * D:]
    W_qp = jnp.pad(W_q.reshape(QL, heads, QK_HEAD), ((0, 0), (0, 0), (0, QPAD - QK_HEAD))).reshape(QL, heads * QPAD)

    pos = jnp.concatenate([jnp.zeros((PAD_LEN,), jnp.int32), jnp.arange(N_META, dtype=jnp.int32),
                           positions[0].astype(jnp.int32) + N_META])
    inv = 1.0 / (ROPE_THETA ** (jnp.arange(0, ROPE, 2, dtype=F32) / ROPE))
    ang = pos.astype(F32)[:, None] * inv
    zero = jnp.zeros((L, 128 - ROPE), F32)
    cos = jnp.concatenate([jnp.cos(ang), jnp.cos(ang), zero], axis=1)
    sin = jnp.concatenate([jnp.sin(ang), jnp.sin(ang), zero], axis=1)

    h0, u1 = _embed_norm(x[0], meta_full, g_mix_norm)
    lat = _mm(u1, W_lat, "nn", F32, "proj_lat")
    hp = _mm(u1, W_H, "nn", F32, "proj_hgrn")
    gates = _mm(u1, W_G, "nn", F32, "proj_gates")
    kr = _mm(u1, W_kr, "nn", F32, "proj_krope")
    qn = _rmsnorm_fwd(lat, g_q_norm, "norm_q", col0=0, width=QL)
    kvn = _rmsnorm_fwd(lat, g_kv_norm, "norm_kv", col0=QL, width=KVL)
    qp = _mm(qn, W_qp, "nn", BF16, "q_up")
    kv = _mm(kvn, W_kv, "nn", BF16, "kv_up")
    qc, kc, vv = _rope_fwd(qp, kv, kr, cos, sin, heads)
    o_mla, lse, late = _attn_fwd(qc, kc, vv, heads, late_rider)
    W_a, W_b, W_o, W_fi, W_fo = [full(gw, bc) for gw, bc in zip(late, col_sharded[3:])]
    o_hgrn, o_rec, s_hist = _hgrn_fwd(hp, lb_raw, g_hgrn_norm, nh)
    br_a = _mm(o_mla, W_a, "nn", F32, "branch_mla")
    br_b = _mm(o_hgrn, W_b, "nn", F32, "branch_hgrn")
    merged = _merge_fwd(br_a, br_b, gates)
    h1 = _mm(merged, W_o, "nn", F32, "out_proj", res=h0)
    u2 = _rmsnorm_fwd(h1, g_ffn_norm, "norm_ffn")
    gu = _mm(u2, W_fi, "nn", BF16, "ffn_in")
    act = _conv_fwd(gu, cw_full, conv_b)
    h2 = _mm(act, W_fo, "nn", F32, "ffn_out", res=h1)
    dh2, loss_p, dg_final = _final_loss_bwd(h2, loss_target[0], g_final_norm.reshape(1, D))

    dact = _mm(dh2, W_fo, "nt", F32, "d_act")
    dW_fo = _mm(act, dh2, "tn", F32, "dw_ffn_out")
    dconv, dgu_right, dcb, dcw = _conv_bwd_a(dact, gu, cw_full, conv_b)
    dgu = _conv_bwd_b(dconv, cw_full, dgu_right)
    du2 = _mm(dgu, W_fi, "nt", F32, "d_u2")
    dW_fi = _mm(u2, dgu, "tn", F32, "dw_ffn_in")
    dh1, dg_ffn = _rmsnorm_bwd(du2, h1, g_ffn_norm, "norm_ffn_bwd", F32, res=dh2)
    dmerged = _mm(dh1, W_o, "nt", F32, "d_merged")
    dW_o = _mm(merged, dh1, "tn", F32, "dw_out")
    d_a, d_b, d_gates = _merge_bwd(dmerged, br_a, br_b, gates)
    do_mla = _mm(d_a, W_a, "nt", BF16, "d_o_mla")
    dW_a = _mm(o_mla, d_a, "tn", F32, "dw_branch_mla")
    do_hgrn = _mm(d_b, W_b, "nt", F32, "d_o_hgrn")
    dW_b = _mm(o_hgrn, d_b, "tn", F32, "dw_branch_hgrn")
    names = ["w_in", "w_q_up", "w_kv_up", "w_branch_mla", "w_branch_hgrn", "w_out", "w_ffn_in", "w_ffn_out"]

    def chip_partials(grads, by_cols, nms, tag):
        keep = [_shard_row_half(g, bc, ic) for g, bc in zip(grads, by_cols)]
        give = [_shard_row_half(g, bc, 1 - ic) for g, bc in zip(grads, by_cols)]
        taken = _to_sibling(give, "pair_exchange_" + tag)
        return [_pair_sum(a, b, "pair_sum_" + nm) for a, b, nm in zip(keep, taken, nms)]

    late_parts = chip_partials([dW_a, dW_b, dW_o, dW_fi, dW_fo], col_sharded[3:], names[3:], "late")
    dhq, dhf, dhi, dhg, dgn_p, dlb_p = _hgrn_bwd(hp, lb_raw, g_hgrn_norm, do_hgrn, o_rec, s_hist, nh)
    dqc, dkc, dvv, late_recv = _attn_bwd(qc, kc, vv, do_mla, lse, _attn_delta(o_mla, do_mla, heads), heads,
                                         _scatter_rider(late_parts))
    dqp, dkv, dkr = _rope_bwd(dqc, dkc, dvv, cos, sin, heads)
    dqn = _mm(dqp, W_qp, "nt", F32, "d_qn")
    dW_qp = _mm(qn, dqp, "tn", F32, "dw_q_up")
    dkvn = _mm(dkv, W_kv, "nt", F32, "d_kvn")
    dW_kv = _mm(kvn, dkv, "tn", F32, "dw_kv_up")
    dq_lat, dg_q = _rmsnorm_bwd(dqn, lat, g_q_norm, "norm_q_bwd", BF16, col0=0)
    dkv_lat, dg_kv = _rmsnorm_bwd(dkvn, lat, g_kv_norm, "norm_kv_bwd", BF16, col0=QL)
    dlat = jnp.concatenate([dq_lat, dkv_lat], axis=1)
    dhp = jnp.concatenate([dhq, dhf, dhi, dhg], axis=1)
    dW_lat = _mm(u1, dlat, "tn", F32, "dw_in_lat")
    dW_H = _mm(u1, dhp, "tn", F32, "dw_in_hgrn")
    dW_G = _mm(u1, d_gates, "tn", F32, "dw_in_gates")
    dW_kr = _mm(u1, dkr, "tn", F32, "dw_in_krope")
    dW_in = jnp.concatenate([dW_lat, dW_kr[:, :ROPE], dW_H, dW_G], axis=1)
    dW_q = dW_qp.reshape(QL, heads, QPAD)[:, :, :QK_HEAD].reshape(QL, heads * QK_HEAD)
    early_parts = chip_partials([dW_in, dW_q, dW_kv], col_sharded[:3], names[:3], "early")
    du1 = _mm(dlat, W_lat, "nt", F32, "d_u1_lat")
    du1, early_recv = _mm(dhp, W_H, "nt", F32, "d_u1_hgrn", res=du1, rider=_scatter_rider(early_parts))
    du1 = _mm(d_gates, W_G, "nt", F32, "d_u1_gates", res=du1)
    du1 = _mm(dkr, W_kr, "nt", F32, "d_u1_krope", res=du1)
    (dx_tokens, dh0_prefix), dg_mix = _rmsnorm_bwd(du1, h0, g_mix_norm, "norm_mix_bwd", F32, res=dh1,
                                                   split_prefix=True)
    grad_x = dx_tokens[None]

    received = list(early_recv) + list(late_recv)
    halves = [_sum4(r, "sum_" + nm) for r, nm in zip(received, names)]
    others = _to_sibling(halves, "swap_halves")
    big_m = [m_w_in, m_w_q_up, m_w_kv_up, m_w_branch_mla, m_w_branch_hgrn, m_w_out, m_w_ffn_in, m_w_ffn_out]
    big_v = [v_w_in, v_w_q_up, v_w_kv_up, v_w_branch_mla, v_w_branch_hgrn, v_w_out, v_w_ffn_in, v_w_ffn_out]
    big_out = {}
    for nm, mine, other, w, m, v in zip(names, halves, others, big, big_m, big_v):
        g, d, mn, vn = _adamw_big(mine, other, w[0], m[0], v[0], "adamw_" + nm)
        big_out[nm] = (g[None], d[None], mn[None], vn[None])

    pieces = [loss_p[:, :1], dg_mix, dg_q, dg_kv, jnp.sum(dgn_p[:, 0, :], axis=0, keepdims=True), dg_ffn, dg_final,
              dlb_p[:, 0, :].reshape(1, D), dcb, dcw[0:3].reshape(1, 3 * F), dh0_prefix[PAD_LEN:PREFIX].reshape(1, N_META * D)]
    sizes = [p.shape[1] for p in pieces]
    flat = jnp.concatenate(pieces, axis=1)[0]
    rows = -(-flat.shape[0] // 1024) * 8
    packed = jnp.pad(flat, (0, rows * 128 - flat.shape[0])).reshape(rows, 128)
    total = _allreduce_small(packed).reshape(-1)
    offs = [0]
    for s in sizes:
        offs.append(offs[-1] + s)
    loss, g_mix, g_q, g_kv, g_hg, g_ffn, g_fin, dlb, g_cb, g_cw, g_meta = [
        total[offs[t]:offs[t + 1]].reshape(1, sizes[t]) for t in range(len(sizes))]
    g_cw = lax.dynamic_slice_in_dim(g_cw.reshape(3, F), chip * (F // 4), F // 4, axis=1)
    g_meta = lax.dynamic_slice_in_dim(g_meta.reshape(N_META, D), chip * (D // 4), D // 4, axis=1)
    items = [(g_meta, meta_tokens, m_meta_tokens, v_meta_tokens),
             (g_cw, conv_w[0], m_conv_w[0], v_conv_w[0]),
             (g_cb, conv_b, m_conv_b, v_conv_b),
             (g_mix, g_mix_norm, m_g_mix_norm, v_g_mix_norm),
             (g_q, g_q_norm, m_g_q_norm, v_g_q_norm),
             (g_kv, g_kv_norm, m_g_kv_norm, v_g_kv_norm),
             (g_hg, g_hgrn_norm, m_g_hgrn_norm, v_g_hgrn_norm),
             (g_ffn, g_ffn_norm, m_g_ffn_norm, v_g_ffn_norm),
             (g_fin, g_final_norm.reshape(1, D), m_g_final_norm.reshape(1, D), v_g_final_norm.reshape(1, D))]
    g_lb, s_delta, s_m, s_v = _adamw_small(items, (lb_raw, m_lb_raw, v_lb_raw), dlb)
    s_grads = [it[0] for it in items] + [g_lb]

    def shape_small(vals):
        meta, cw, cb, mix, q, kvg, hg, ffn, fin, lb = vals
        return [meta, cw[None], cb, mix, q, kvg, hg, ffn, fin.reshape(D), lb]

    s_grads, s_delta, s_m, s_v = [shape_small(v) for v in (s_grads, s_delta, s_m, s_v)]

    def ordered(kind, small):
        bigs = [big_out[nm][kind] for nm in names]
        return [small[0]] + bigs + small[1:]

    return (loss.reshape(()), grad_x, *ordered(0, s_grads), *ordered(1, s_delta), *ordered(2, s_m), *ordered(3, s_v))
```

```python
import math

import jax
import jax.numpy as jnp
from jax import lax
from jax.experimental import pallas as pl
from jax.experimental.pallas import tpu as pltpu

F32 = jnp.float32
BF16 = jnp.bfloat16
MESH = pl.DeviceIdType.MESH

NORM_EPS = 1e-6
N_META = 16
PREFIX = 128
PAD_LEN = PREFIX - N_META
HEAD = 128
ROPE = 64
QK_HEAD = HEAD + ROPE
QPAD = 2 * HEAD
SOFTMAX_SCALE = QK_HEAD ** -0.5
ROPE_THETA = 10000.0
CHUNK = 128
HALO = 16
LEVELS = 7
HGRN_FWD_HEADS = 4
HGRN_BWD_HEADS = 4
ATTN_FWD_HEADS = 2
ATTN_BWD_HEADS = 2
NEG = -1e30

ADAM_LR = 0.001
ADAM_B1 = 0.9
ADAM_B2 = 0.999
ADAM_EPS = 1e-08
ADAM_WD = 0.01
ADAM_STEP = 10

VMEM_LIMIT_BYTES = 48 * 1024 * 1024


def _params(*sem):
    return pltpu.CompilerParams(dimension_semantics=sem, vmem_limit_bytes=VMEM_LIMIT_BYTES)


def _tile(n, prefs):
    for p in prefs:
        if n % p == 0:
            return p
    return n


ROW_TILES = (640, 512, 256, 128, 64, 32, 16, 8)
TN_ROW_TILES = (1024, 1408, 1536, 512, 256, 128)
TN_K_TILES = (1664, 640, 512, 256, 128)
TALL_ROW_TILE = 1664
COL_TILES = (1024, 512, 256, 128)
K_TILES = (2048, 1536, 1408, 1024, 512, 256, 128)


def _sigmoid(x):
    return 1.0 / (1.0 + jnp.exp(-x))


def _dot(a, b, dims):
    return lax.dot_general(a.astype(BF16), b.astype(BF16), (dims, ((), ())), preferred_element_type=F32)


NN = ((1,), (0,))
TN = ((0,), (0,))
NT = ((1,), (1,))


def _split3(x):
    hi = x.astype(BF16)
    r = x - hi.astype(F32)
    mid = r.astype(BF16)
    lo = (r - mid.astype(F32)).astype(BF16)
    return hi, mid, lo


def _dot_exact_rhs(sel, x, dims):
    hi, mid, lo = _split3(x)
    return _dot(sel, hi, dims) + _dot(sel, mid, dims) + _dot(sel, lo, dims)


def _mm(a, b, mode, out_dtype, name, res=None, rider=None):
    if mode == "nn":
        (M, K), (K2, N) = a.shape, b.shape
    elif mode == "tn":
        (K, M), (K2, N) = a.shape, b.shape
    else:
        (M, K), (N, K2) = a.shape, b.shape
    assert K == K2, (name, a.shape, b.shape)
    tm = _tile(M, TN_ROW_TILES if mode == "tn" else ROW_TILES)
    tn = _tile(N, COL_TILES)
    tk = _tile(K, TN_K_TILES if mode == "tn" else K_TILES)
    nk = K // tk
    if mode != "tn" and nk > 1 and res is None and a.dtype == BF16 and M % TALL_ROW_TILE == 0:
        tm = TALL_ROW_TILE
    dims = {"nn": NN, "tn": TN, "nt": NT}[mode]

    n_rin = 0 if rider is None else len(rider.operands)
    n_rout = 0 if rider is None else len(rider.out_shapes)
    n_rsem = 0 if rider is None else len(rider.sem_shapes)
    grid = (M // tm, N // tn, nk)

    def body(*refs):
        main_in, r_ins, (o_ref,), r_outs, accs, r_sems = _split_refs(
            refs, (2 if res is None else 3, n_rin, 1, n_rout, 0 if nk == 1 else 1, n_rsem))
        a_ref, b_ref = main_in[:2]
        r_ref = None if res is None else main_in[2]
        i, j, k = pl.program_id(0), pl.program_id(1), pl.program_id(2)
        if rider is not None:
            pl.when((i == 0) & (j == 0) & (k == 0))(lambda: rider.start(r_ins, r_outs, r_sems))

        def finish(r):
            if r_ref is not None:
                r = r + r_ref[...].astype(F32)
            o_ref[...] = r.astype(out_dtype)

        if nk == 1:
            finish(_dot(a_ref[...], b_ref[...], dims))
        else:
            acc = accs[0]

            @pl.when(k == 0)
            def _():
                acc[...] = jnp.zeros_like(acc)

            acc[...] += _dot(a_ref[...], b_ref[...], dims)

            @pl.when(k == nk - 1)
            def _():
                finish(acc[...])

        if rider is not None:
            last = (i == grid[0] - 1) & (j == grid[1] - 1) & (k == nk - 1)
            pl.when(last)(lambda: rider.finish(r_ins, r_outs, r_sems))

    if mode == "nn":
        a_spec = pl.BlockSpec((tm, tk), lambda i, j, k: (i, k))
        b_spec = pl.BlockSpec((tk, tn), lambda i, j, k: (k, j))
    elif mode == "tn":
        a_spec = pl.BlockSpec((tk, tm), lambda i, j, k: (k, i))
        b_spec = pl.BlockSpec((tk, tn), lambda i, j, k: (k, j))
    else:
        a_spec = pl.BlockSpec((tm, tk), lambda i, j, k: (i, k))
        b_spec = pl.BlockSpec((tn, tk), lambda i, j, k: (j, k))
    in_specs = [a_spec, b_spec]
    operands = [a, b]
    if res is not None:
        in_specs.append(pl.BlockSpec((tm, tn), lambda i, j, k: (i, j)))
        operands.append(res)
    out_shape = jax.ShapeDtypeStruct((M, N), out_dtype)
    out_spec = pl.BlockSpec((tm, tn), lambda i, j, k: (i, j))
    scratch = [] if nk == 1 else [pltpu.VMEM((tm, tn), F32)]
    if rider is None:
        return pl.pallas_call(
            body, name=name, out_shape=out_shape, grid=grid, in_specs=in_specs, out_specs=out_spec,
            scratch_shapes=scratch, compiler_params=_params("parallel", "parallel", "arbitrary"),
        )(*operands)
    res_all = pl.pallas_call(
        body, name=name, out_shape=(out_shape, *rider.out_shapes), grid=grid,
        in_specs=in_specs + [ANY] * n_rin, out_specs=(out_spec, *([ANY] * n_rout)),
        scratch_shapes=scratch + rider.sem_shapes,
        compiler_params=pltpu.CompilerParams(dimension_semantics=("arbitrary", "arbitrary", "arbitrary"),
                                             vmem_limit_bytes=VMEM_LIMIT_BYTES, has_side_effects=True),
    )(*operands, *rider.operands)
    return res_all[0], res_all[1:]


def _rmsnorm_fwd(x, g, name, col0=0, width=None):
    L = x.shape[0]
    width = x.shape[1] if width is None else width
    assert col0 % width == 0
    cb = col0 // width
    tm = _tile(L, (128, 64, 32, 16))

    def body(x_ref, g_ref, o_ref):
        xv = x_ref[...]
        r = lax.rsqrt(jnp.mean(xv * xv, axis=-1, keepdims=True) + NORM_EPS)
        o_ref[...] = ((xv * r) * g_ref[...]).astype(BF16)

    return pl.pallas_call(
        body, name=name,
        out_shape=jax.ShapeDtypeStruct((L, width), BF16),
        grid=(L // tm,),
        in_specs=[pl.BlockSpec((tm, width), lambda i: (i, cb)), pl.BlockSpec((1, width), lambda i: (0, 0))],
        out_specs=pl.BlockSpec((tm, width), lambda i: (i, 0)),
        compiler_params=_params("parallel"),
    )(x, g)


def _embed_norm(x, meta, g):
    S, D = x.shape
    L = S + PREFIX
    tm = PREFIX

    def body(x_ref, meta_ref, g_ref, h_ref, u_ref):
        i = pl.program_id(0)

        @pl.when(i == 0)
        def _():
            h_ref[...] = jnp.zeros_like(h_ref)
            h_ref[PAD_LEN:PREFIX, :] = meta_ref[...]

        @pl.when(i > 0)
        def _():
            h_ref[...] = x_ref[...]

        xv = h_ref[...]
        r = lax.rsqrt(jnp.mean(xv * xv, axis=-1, keepdims=True) + NORM_EPS)
        u_ref[...] = ((xv * r) * g_ref[...]).astype(BF16)

    return pl.pallas_call(
        body, name="embed_norm_mix",
        out_shape=(jax.ShapeDtypeStruct((L, D), F32), jax.ShapeDtypeStruct((L, D), BF16)),
        grid=(L // tm,),
        in_specs=[pl.BlockSpec((tm, D), lambda i: (jnp.maximum(i - 1, 0), 0)),
                  pl.BlockSpec((N_META, D), lambda i: (0, 0)), pl.BlockSpec((1, D), lambda i: (0, 0))],
        out_specs=(pl.BlockSpec((tm, D), lambda i: (i, 0)), pl.BlockSpec((tm, D), lambda i: (i, 0))),
        compiler_params=_params("parallel"),
    )(x, meta, g)


def _rmsnorm_bwd(dy, x, g, name, out_dtype, col0=0, res=None, split_prefix=False):
    L, width = dy.shape
    assert col0 % width == 0
    cb = col0 // width
    tm = PREFIX if split_prefix else _tile(L, (128, 64, 32, 16))

    def body(*refs):
        head_ref = None
        if split_prefix:
            refs, head_ref = refs[:-1], refs[-1]
        if res is None:
            dy_ref, x_ref, g_ref, dx_ref, dg_ref = refs
            r_ref = None
        else:
            dy_ref, x_ref, g_ref, r_ref, dx_ref, dg_ref = refs

        @pl.when(pl.program_id(0) == 0)
        def _():
            dg_ref[...] = jnp.zeros_like(dg_ref)

        xv = x_ref[...]
        dyv = dy_ref[...].astype(F32)
        r = lax.rsqrt(jnp.mean(xv * xv, axis=-1, keepdims=True) + NORM_EPS)
        z = dyv * g_ref[...]
        dx = r * z - xv * ((r * r * r) * jnp.mean(z * xv, axis=-1, keepdims=True))
        if r_ref is not None:
            dx = dx + r_ref[...]
        dg_ref[...] += jnp.sum(dyv * (xv * r), axis=0, keepdims=True)
        if head_ref is None:
            dx_ref[...] = dx.astype(out_dtype)
        else:
            @pl.when(pl.program_id(0) == 0)
            def _():
                head_ref[...] = dx.astype(out_dtype)

            @pl.when(pl.program_id(0) > 0)
            def _():
                dx_ref[...] = dx.astype(out_dtype)

    in_specs = [pl.BlockSpec((tm, width), lambda i: (i, 0)),
                pl.BlockSpec((tm, width), lambda i: (i, cb)),
                pl.BlockSpec((1, width), lambda i: (0, 0))]
    operands = [dy, x, g]
    if res is not None:
        in_specs.append(pl.BlockSpec((tm, width), lambda i: (i, 0)))
        operands.append(res)
    dg_shape, dg_spec = jax.ShapeDtypeStruct((1, width), F32), pl.BlockSpec((1, width), lambda i: (0, 0))
    if not split_prefix:
        return pl.pallas_call(
            body, name=name, out_shape=(jax.ShapeDtypeStruct((L, width), out_dtype), dg_shape), grid=(L // tm,),
            in_specs=in_specs, out_specs=(pl.BlockSpec((tm, width), lambda i: (i, 0)), dg_spec),
            compiler_params=_params("arbitrary"),
        )(*operands)
    dx_rest, dg, dx_head = pl.pallas_call(
        body, name=name,
        out_shape=(jax.ShapeDtypeStruct((L - PREFIX, width), out_dtype), dg_shape,
                   jax.ShapeDtypeStruct((PREFIX, width), out_dtype)),
        grid=(L // tm,), in_specs=in_specs,
        out_specs=(pl.BlockSpec((tm, width), lambda i: (jnp.maximum(i - 1, 0), 0)), dg_spec,
                   pl.BlockSpec((PREFIX, width), lambda i: (0, 0))),
        compiler_params=_params("arbitrary"),
    )(*operands)
    return (dx_rest, dx_head), dg


def _final_loss_bwd(h2, tgt, g):
    L, D = h2.shape
    tm = PREFIX
    inv_d = 1.0 / D

    def body(h_ref, t_ref, g_ref, dh_ref, loss_ref, dg_ref):
        i = pl.program_id(0)

        @pl.when(i == 0)
        def _():
            loss_ref[...] = jnp.zeros_like(loss_ref)
            dg_ref[...] = jnp.zeros_like(dg_ref)

        xv = h_ref[...]
        r = lax.rsqrt(jnp.mean(xv * xv, axis=-1, keepdims=True) + NORM_EPS)
        xn = xv * r
        y = xn * g_ref[...]
        real = (i >= PREFIX // tm).astype(F32)
        diff = (y - t_ref[...]) * real
        loss_ref[...] += 0.5 * inv_d * jnp.sum(diff * diff)
        dyv = diff * inv_d
        z = dyv * g_ref[...]
        dh_ref[...] = r * z - xv * ((r * r * r) * jnp.mean(z * xv, axis=-1, keepdims=True))
        dg_ref[...] += jnp.sum(dyv * xn, axis=0, keepdims=True)

    shift = PREFIX // tm
    return pl.pallas_call(
        body, name="final_loss_bwd",
        out_shape=(jax.ShapeDtypeStruct((L, D), F32), jax.ShapeDtypeStruct((1, 128), F32),
                   jax.ShapeDtypeStruct((1, D), F32)),
        grid=(L // tm,),
        in_specs=[pl.BlockSpec((tm, D), lambda i: (i, 0)),
                  pl.BlockSpec((tm, D), lambda i: (jnp.maximum(i - shift, 0), 0)),
                  pl.BlockSpec((1, D), lambda i: (0, 0))],
        out_specs=(pl.BlockSpec((tm, D), lambda i: (i, 0)), pl.BlockSpec((1, 128), lambda i: (0, 0)),
                   pl.BlockSpec((1, D), lambda i: (0, 0))),
        compiler_params=_params("arbitrary"),
    )(h2, tgt, g)


def _rot_half(x):
    lane = lax.broadcasted_iota(jnp.int32, x.shape, 1)
    return jnp.where(lane < ROPE // 2, -pltpu.roll(x, 128 - ROPE // 2, 1), pltpu.roll(x, ROPE // 2, 1))


def _rope_fwd(qp, kv, kr, cos, sin, heads):
    L = qp.shape[0]
    tm = _tile(L, (128,))

    def body(q_ref, kv_ref, kr_ref, c_ref, s_ref, qc_ref, kc_ref, v_ref):
        c, s = c_ref[...], s_ref[...]
        krv = kr_ref[...]
        kr_rot = (krv * c + _rot_half(krv) * s).astype(BF16)
        for h in range(heads):
            lo = h * QPAD
            qc_ref[:, lo:lo + HEAD] = (q_ref[:, lo:lo + HEAD].astype(F32) * SOFTMAX_SCALE).astype(BF16)
            qr = q_ref[:, lo + HEAD:lo + QPAD].astype(F32)
            qc_ref[:, lo + HEAD:lo + QPAD] = ((qr * c + _rot_half(qr) * s) * SOFTMAX_SCALE).astype(BF16)
            kc_ref[:, lo:lo + HEAD] = kv_ref[:, lo:lo + HEAD].astype(BF16)
            kc_ref[:, lo + HEAD:lo + QPAD] = kr_rot
            v_ref[:, h * HEAD:(h + 1) * HEAD] = kv_ref[:, lo + HEAD:lo + QPAD].astype(BF16)

    W = heads * QPAD
    row = lambda w: pl.BlockSpec((tm, w), lambda i: (i, 0))
    return pl.pallas_call(
        body, name="rope_fwd",
        out_shape=(jax.ShapeDtypeStruct((L, W), BF16), jax.ShapeDtypeStruct((L, W), BF16),
                   jax.ShapeDtypeStruct((L, heads * HEAD), BF16)),
        grid=(L // tm,),
        in_specs=[row(W), row(W), row(128), row(128), row(128)],
        out_specs=(row(W), row(W), row(heads * HEAD)),
        compiler_params=_params("parallel"),
    )(qp, kv, kr, cos, sin)


def _rope_bwd(dqc, dkc, dv, cos, sin, heads):
    L = dqc.shape[0]
    tm = _tile(L, (128,))

    def body(dq_ref, dk_ref, dv_ref, c_ref, s_ref, dqp_ref, dkv_ref, dkr_ref):
        c, s = c_ref[...], s_ref[...]
        acc = jnp.zeros((tm, 128), F32)
        for h in range(heads):
            lo = h * QPAD
            dqp_ref[:, lo:lo + HEAD] = (dq_ref[:, lo:lo + HEAD] * SOFTMAX_SCALE).astype(BF16)
            d = dq_ref[:, lo + HEAD:lo + QPAD]
            dqp_ref[:, lo + HEAD:lo + QPAD] = ((d * c - _rot_half(d) * s) * SOFTMAX_SCALE).astype(BF16)
            dkv_ref[:, lo:lo + HEAD] = dk_ref[:, lo:lo + HEAD].astype(BF16)
            dkv_ref[:, lo + HEAD:lo + QPAD] = dv_ref[:, h * HEAD:(h + 1) * HEAD].astype(BF16)
            acc = acc + dk_ref[:, lo + HEAD:lo + QPAD]
        dkr_ref[...] = (acc * c - _rot_half(acc) * s).astype(BF16)

    W = heads * QPAD
    row = lambda w: pl.BlockSpec((tm, w), lambda i: (i, 0))
    return pl.pallas_call(
        body, name="rope_bwd",
        out_shape=(jax.ShapeDtypeStruct((L, W), BF16), jax.ShapeDtypeStruct((L, W), BF16),
                   jax.ShapeDtypeStruct((L, 128), BF16)),
        grid=(L // tm,),
        in_specs=[row(W), row(W), row(heads * HEAD), row(128), row(128)],
        out_specs=(row(W), row(W), row(128)),
        compiler_params=_params("parallel"),
    )(dqc, dkc, dv, cos, sin)


def _attn_keep(qi, ki, ta):
    t = qi * ta + lax.broadcasted_iota(jnp.int32, (ta, ta), 0)
    s = ki * ta + lax.broadcasted_iota(jnp.int32, (ta, ta), 1)
    return (s <= t) & ((s >= PAD_LEN) | (s == t))


def _split_refs(refs, sizes):
    out, at = [], 0
    for n in sizes:
        out.append(refs[at:at + n])
        at += n
    return out


def _attn_fwd(qc, kc, v, heads, rider):
    L = qc.shape[0]
    ta = _tile(L, (640, 128))
    nb = L // ta
    pairs = [(i, j) for i in range(nb) for j in range(i + 1)]
    q_of = jnp.asarray([p[0] for p in pairs], jnp.int32)
    k_of = jnp.asarray([p[1] for p in pairs], jnp.int32)
    n_rin, n_rout = len(rider.operands), len(rider.out_shapes)
    per = _tile(heads, (ATTN_FWD_HEADS, 1))
    ng = heads // per

    def body(*refs):
        (q_of_ref, k_of_ref, q_ref, k_ref, v_ref), r_ins, (o_ref, lse_ref), r_outs, (m_sc, l_sc, acc_sc), r_sems = \
            _split_refs(refs, (5, n_rin, 2, n_rout, 3, len(rider.sem_shapes)))
        h, t = pl.program_id(0), pl.program_id(1)
        qi, ki = q_of_ref[t], k_of_ref[t]
        pl.when((h == 0) & (t == 0))(lambda: rider.start(r_ins, r_outs, r_sems))

        @pl.when(ki == 0)
        def _():
            m_sc[...] = jnp.full_like(m_sc, NEG)
            l_sc[...] = jnp.zeros_like(l_sc)
            acc_sc[...] = jnp.zeros_like(acc_sc)

        def step(masked):
            wide = lambda hh: slice(hh * QPAD, (hh + 1) * QPAD)
            lanes = lambda hh: slice(hh * HEAD, (hh + 1) * HEAD)
            ss = [_dot(q_ref[:, wide(hh)], k_ref[:, wide(hh)], NT) for hh in range(per)]
            if masked:
                keep = _attn_keep(qi, ki, ta)
                ss = [jnp.where(keep, s, NEG) for s in ss]
            for hh in range(per):
                m_old = m_sc[hh]
                m_new = jnp.maximum(m_old, jnp.max(ss[hh], axis=-1, keepdims=True))
                p = jnp.exp(ss[hh] - jnp.tile(m_new, (1, ta // HEAD)))
                alpha = jnp.exp(m_old - m_new)
                l_sc[hh] = alpha * l_sc[hh] + jnp.sum(p, axis=-1, keepdims=True)
                acc_sc[hh] = alpha * acc_sc[hh] + _dot(p, v_ref[:, lanes(hh)], NN)
                m_sc[hh] = m_new

        pl.when((ki == qi) | ((ki == 0) & (qi > 0)))(lambda: step(True))
        pl.when((ki > 0) & (ki < qi))(lambda: step(False))

        @pl.when(ki == qi)
        def _():
            for hh in range(per):
                l = l_sc[hh]
                o_ref[:, hh * HEAD:(hh + 1) * HEAD] = (acc_sc[hh] / l).astype(BF16)
                lse_ref[:, hh * HEAD:(hh + 1) * HEAD] = m_sc[hh] + jnp.log(l)

        pl.when((h == ng - 1) & (t == len(pairs) - 1))(lambda: rider.finish(r_ins, r_outs, r_sems))

    qrow = lambda w: pl.BlockSpec((ta, per * w), lambda h, t, q_of, k_of: (q_of[t], h))
    krow = lambda w: pl.BlockSpec((ta, per * w), lambda h, t, q_of, k_of: (k_of[t], h))
    stat = pltpu.VMEM((per, ta, HEAD), F32)
    res = pl.pallas_call(
        body, name="attn_fwd",
        out_shape=(jax.ShapeDtypeStruct((L, heads * HEAD), BF16), jax.ShapeDtypeStruct((L, heads * HEAD), F32),
                   *rider.out_shapes),
        grid_spec=pltpu.PrefetchScalarGridSpec(
            num_scalar_prefetch=2, grid=(ng, len(pairs)),
            in_specs=[qrow(QPAD), krow(QPAD), krow(HEAD)] + [ANY] * n_rin,
            out_specs=(qrow(HEAD), qrow(HEAD), *([ANY] * n_rout)),
            scratch_shapes=[stat, stat, stat] + rider.sem_shapes),
        compiler_params=pltpu.CompilerParams(dimension_semantics=("arbitrary", "arbitrary"),
                                             vmem_limit_bytes=VMEM_LIMIT_BYTES, has_side_effects=True),
    )(q_of, k_of, qc, kc, v, *rider.operands)
    return res[0], res[1], res[2:]


def _attn_delta(o, do, heads):
    L = o.shape[0]
    tm = _tile(L, (128,))

    def body(o_ref, do_ref, d_ref):
        for h in range(heads):
            cols = slice(h * HEAD, (h + 1) * HEAD)
            d = jnp.sum(do_ref[:, cols].astype(F32) * o_ref[:, cols].astype(F32), axis=-1, keepdims=True)
            d_ref[:, cols] = jnp.broadcast_to(d, (tm, HEAD))

    row = pl.BlockSpec((tm, heads * HEAD), lambda i: (i, 0))
    return pl.pallas_call(
        body, name="attn_delta", out_shape=jax.ShapeDtypeStruct((L, heads * HEAD), F32), grid=(L // tm,),
        in_specs=[row, row], out_specs=row, compiler_params=_params("parallel"),
    )(o, do)


def _attn_bwd(qc, kc, v, do, lse, delta, heads, rider):
    L = qc.shape[0]
    ta = _tile(L, (640, 128))
    nb = L // ta
    pairs = [(j, i) for j in range(nb) for i in range(j, nb)]
    k_of = jnp.asarray([p[0] for p in pairs], jnp.int32)
    q_of = jnp.asarray([p[1] for p in pairs], jnp.int32)
    n_rin, n_rout = len(rider.operands), len(rider.out_shapes)
    per = _tile(heads, (ATTN_BWD_HEADS, 1))
    ng = heads // per

    def body(*refs):
        ((k_of_ref, q_of_ref, q_ref, k_ref, v_ref, do_ref, lse_ref, delta_ref), r_ins, (dq_hbm, dk_ref, dv_ref),
         r_outs, (dq_sc, dq_sem), r_sems) = _split_refs(refs, (8, n_rin, 3, n_rout, 2, len(rider.sem_shapes)))
        h, t = pl.program_id(0), pl.program_id(1)
        kj, qi = k_of_ref[t], q_of_ref[t]
        pl.when((h == 0) & (t == 0))(lambda: rider.start(r_ins, r_outs, r_sems))

        @pl.when(t == 0)
        def _():
            dq_sc[...] = jnp.zeros_like(dq_sc)

        @pl.when(qi == kj)
        def _():
            dk_ref[...] = jnp.zeros_like(dk_ref)
            dv_ref[...] = jnp.zeros_like(dv_ref)

        def step(masked):
            wide = lambda hh: slice(hh * QPAD, (hh + 1) * QPAD)
            lanes = lambda hh: slice(hh * HEAD, (hh + 1) * HEAD)
            rep = (1, ta // HEAD)
            rows = pl.ds(pl.multiple_of(qi * ta, ta), ta)
            ss = [_dot(q_ref[:, wide(hh)], k_ref[:, wide(hh)], NT) for hh in range(per)]
            dps = [_dot(do_ref[:, lanes(hh)], v_ref[:, lanes(hh)], NT) for hh in range(per)]
            ps = [jnp.exp(ss[hh] - jnp.tile(lse_ref[:, lanes(hh)], rep)) for hh in range(per)]
            if masked:
                keep = _attn_keep(qi, kj, ta)
                ps = [jnp.where(keep, p, 0.0) for p in ps]
            dss = [ps[hh] * (dps[hh] - jnp.tile(delta_ref[:, lanes(hh)], rep)) for hh in range(per)]
            for hh in range(per):
                dv_ref[:, lanes(hh)] += _dot(ps[hh], do_ref[:, lanes(hh)], TN)
                dk_ref[:, wide(hh)] += _dot(dss[hh], q_ref[:, wide(hh)], TN)
                dq_sc[rows, wide(hh)] += _dot(dss[hh], k_ref[:, wide(hh)], NN)

        pl.when((qi == kj) | ((kj == 0) & (qi > 0)))(lambda: step(True))
        pl.when((kj > 0) & (qi > kj))(lambda: step(False))

        @pl.when(t == len(pairs) - 1)
        def _():
            cols = pl.ds(pl.multiple_of(h * (per * QPAD), per * QPAD), per * QPAD)
            out = pltpu.make_async_copy(dq_sc, dq_hbm.at[:, cols], dq_sem)
            out.start()
            out.wait()

        pl.when((h == ng - 1) & (t == len(pairs) - 1))(lambda: rider.finish(r_ins, r_outs, r_sems))

    qrow = lambda w: pl.BlockSpec((ta, per * w), lambda h, t, k_of, q_of: (q_of[t], h))
    krow = lambda w: pl.BlockSpec((ta, per * w), lambda h, t, k_of, q_of: (k_of[t], h))
    res = pl.pallas_call(
        body, name="attn_bwd",
        out_shape=(jax.ShapeDtypeStruct((L, heads * QPAD), F32), jax.ShapeDtypeStruct((L, heads * QPAD), F32),
                   jax.ShapeDtypeStruct((L, heads * HEAD), F32), *rider.out_shapes),
        grid_spec=pltpu.PrefetchScalarGridSpec(
            num_scalar_prefetch=2, grid=(ng, len(pairs)),
            in_specs=[qrow(QPAD), krow(QPAD), krow(HEAD), qrow(HEAD), qrow(HEAD), qrow(HEAD)] + [ANY] * n_rin,
            out_specs=(ANY, krow(QPAD), krow(HEAD), *([ANY] * n_rout)),
            scratch_shapes=[pltpu.VMEM((L, per * QPAD), F32), pltpu.SemaphoreType.DMA] + rider.sem_shapes),
        compiler_params=pltpu.CompilerParams(dimension_semantics=("arbitrary", "arbitrary"),
                                             vmem_limit_bytes=VMEM_LIMIT_BYTES, has_side_effects=True),
    )(k_of, q_of, qc, kc, v, do, lse, delta, *rider.operands)
    return res[0], res[1], res[2], res[3:]


def _hgrn_constants():
    t = jnp.arange(CHUNK)
    tril = (t[None, :] <= t[:, None]).astype(BF16)
    sel, blk = [], []
    for lv in range(LEVELS):
        hs = 1 << lv
        mid = (t // (2 * hs)) * (2 * hs) + hs - 1
        sel.append((t[None, :] == mid[:, None]).astype(BF16))
        blk.append(((t[:, None] // (2 * hs)) == (t[None, :] // (2 * hs))).astype(F32))
    return tril, jnp.concatenate(sel, axis=0), jnp.concatenate(blk, axis=0)


def _hgrn_decay_grad_constants():
    r = jnp.arange(CHUNK)[:, None]
    c = jnp.arange(CHUNK)[None, :]
    mats = []
    for lv in range(LEVELS):
        same = (r >> (lv + 1)) == (c >> (lv + 1))
        second = ((r >> lv) & 1) == 1
        mats.append(same & jnp.where(second, c >= r, c < r))
    mats += [c >= r, c < r]
    return jnp.concatenate(mats, axis=1).astype(BF16)


def _hgrn_gates(hq_ref, hf_ref, hi_ref, lb_ref, cols, row0):
    rows = row0 + lax.broadcasted_iota(jnp.int32, (CHUNK, 1), 0)
    valid = rows >= PAD_LEN
    lb = 1.0 / (1.0 + jnp.exp(lb_ref[1:2, cols] - lb_ref[0:1, cols]))
    hq = hq_ref[:, cols]
    sq = _sigmoid(hq)
    sg = _sigmoid(hf_ref[:, cols])
    f = lb + (1.0 - lb) * sg
    g = jnp.where(valid, jnp.log(f), 0.0)
    k = jnp.where(valid, 1.0 - f, 0.0)
    return dict(q=hq * sq, sq=sq, hq=hq, k=k, v=hi_ref[:, cols], g=g, f=f, sg=sg, lb=lb, valid=valid)


def _hgrn_prefix(tril_ref, sel_ref, gs):
    n = len(gs)
    b_all = _dot_exact_rhs(tril_ref[...], jnp.concatenate(gs, axis=1) if n > 1 else gs[0], NN)
    bm_all = _dot_exact_rhs(sel_ref[...], b_all, NN)
    cut = lambda a, i: a[:, i * HEAD:(i + 1) * HEAD]
    return [cut(b_all, i) for i in range(n)], [cut(bm_all, i) for i in range(n)]


def _hgrn_levels(q, k, b, bm_all, blk_ref):
    t = lax.broadcasted_iota(jnp.int32, (CHUNK, 1), 0)
    out = []
    for lv in range(LEVELS):
        bm = bm_all[lv * CHUNK:(lv + 1) * CHUNK, :]
        second = ((t >> lv) & 1) == 1
        eq = jnp.where(second, jnp.exp(jnp.minimum(b - bm, 0.0)), 0.0)
        ek = jnp.where(second, 0.0, jnp.exp(jnp.minimum(bm - b, 0.0)))
        same = blk_ref[lv * CHUNK:(lv + 1) * CHUNK, :]
        out.append((eq, ek, (q * eq).astype(BF16), (k * ek).astype(BF16), same))
    return out


def _hgrn_intra(qs, ks, levels):
    tt = lax.broadcasted_iota(jnp.int32, (CHUNK, CHUNK), 0)
    ss = lax.broadcasted_iota(jnp.int32, (CHUNK, CHUNK), 1)
    ps = [jnp.where(tt == ss, jnp.sum(q * k, axis=-1, keepdims=True), 0.0) for q, k in zip(qs, ks)]
    for lv in range(LEVELS):
        for i, lvl in enumerate(levels):
            _, _, ql, kl, same = lvl[lv]
            ps[i] = ps[i] + _dot(ql, kl, NT) * same
    return ps


def _hgrn_fwd(hp, lb_raw, g_norm, nh):
    L = hp.shape[0]
    D = nh * HEAD
    nc = L // CHUNK
    per = _tile(nh, (HGRN_FWD_HEADS, 2, 1))
    ng = nh // per
    tril, sel, blk = _hgrn_constants()

    def body(hq_ref, hf_ref, hi_ref, hg_ref, lb_ref, gn_ref, tril_ref, sel_ref, blk_ref,
             oh_ref, orec_ref, shist_ref, s_sc, b_sc):
        c = pl.program_id(1)

        @pl.when(c == 0)
        def _():
            s_sc[...] = jnp.zeros_like(s_sc)

        heads_here = range(per)
        lanes = [slice(hh * HEAD, (hh + 1) * HEAD) for hh in heads_here]
        ws = [_hgrn_gates(hq_ref, hf_ref, hi_ref, lb_ref, lanes[hh], c * CHUNK) for hh in heads_here]
        qs, ks, vs = [w["q"] for w in ws], [w["k"] for w in ws], [w["v"] for w in ws]
        bs, bms = _hgrn_prefix(tril_ref, sel_ref, [w["g"] for w in ws])
        for hh in heads_here:
            b_sc[hh] = bs[hh]
        b_lasts = [b_sc[hh, CHUNK - 1:CHUNK, :] for hh in heads_here]
        ps = _hgrn_intra(qs, ks, [_hgrn_levels(qs[hh], ks[hh], bs[hh], bms[hh], blk_ref) for hh in heads_here])
        s_ins = [s_sc[hh] for hh in heads_here]
        os_ = [_dot(ps[hh], vs[hh], NN) + _dot(qs[hh] * jnp.exp(bs[hh]), s_ins[hh], NT) for hh in heads_here]
        for hh in heads_here:
            shist_ref[0, hh] = s_ins[hh]
            s_sc[hh] = (jnp.exp(b_lasts[hh]) * s_ins[hh]
                        + _dot(vs[hh], ks[hh] * jnp.exp(b_lasts[hh] - bs[hh]), TN))
        for hh in heads_here:
            o = os_[hh]
            orec_ref[:, lanes[hh]] = o
            rn = lax.rsqrt(jnp.mean(o * o, axis=-1, keepdims=True) + NORM_EPS)
            hg = hg_ref[:, lanes[hh]]
            oh_ref[:, lanes[hh]] = (((o * rn) * gn_ref[...]) * (hg * _sigmoid(hg))).astype(BF16)

    col = lambda grp: pl.BlockSpec((CHUNK, per * HEAD), lambda h, c: (c, grp * ng + h))
    const = lambda shape: pl.BlockSpec(shape, lambda h, c: (0, 0))
    return pl.pallas_call(
        body, name="hgrn_fwd",
        out_shape=(jax.ShapeDtypeStruct((L, D), BF16), jax.ShapeDtypeStruct((L, D), F32),
                   jax.ShapeDtypeStruct((nc, nh, HEAD, HEAD), F32)),
        grid=(ng, nc),
        in_specs=[col(0), col(1), col(2), col(3),
                  pl.BlockSpec((2, per * HEAD), lambda h, c: (0, h)), const((1, HEAD)),
                  const((CHUNK, CHUNK)), const((LEVELS * CHUNK, CHUNK)), const((LEVELS * CHUNK, CHUNK))],
        out_specs=(pl.BlockSpec((CHUNK, per * HEAD), lambda h, c: (c, h)),
                   pl.BlockSpec((CHUNK, per * HEAD), lambda h, c: (c, h)),
                   pl.BlockSpec((1, per, HEAD, HEAD), lambda h, c: (c, h, 0, 0))),
        scratch_shapes=[pltpu.VMEM((per, HEAD, HEAD), F32), pltpu.VMEM((per, CHUNK, HEAD), F32)],
        compiler_params=_params("parallel", "arbitrary"),
    )(hp, hp, hp, hp, lb_raw, g_norm, tril, sel, blk)


def _hgrn_bwd(hp, lb_raw, g_norm, do_h, o_rec, s_hist, nh):
    L = hp.shape[0]
    D = nh * HEAD
    nc = L // CHUNK
    per = _tile(nh, (HGRN_BWD_HEADS, 1))
    ng = nh // per
    tril, sel, blk = _hgrn_constants()
    tdec = _hgrn_decay_grad_constants()

    def body(hq_ref, hf_ref, hi_ref, hg_ref, lb_ref, gn_ref, tril_ref, sel_ref, blk_ref, tdec_ref,
             do_ref, orec_ref, shist_ref, dhq_ref, dhf_ref, dhi_ref, dhg_ref, dgn_ref, dlb_ref, ds_sc, b_sc):
        ci = pl.program_id(1)
        c = nc - 1 - ci

        @pl.when(ci == 0)
        def _():
            ds_sc[...] = jnp.zeros_like(ds_sc)
            dgn_ref[...] = jnp.zeros_like(dgn_ref)
            dlb_ref[...] = jnp.zeros_like(dlb_ref)

        heads_here = range(per)
        lanes = [slice(hh * HEAD, (hh + 1) * HEAD) for hh in heads_here]
        ws = [_hgrn_gates(hq_ref, hf_ref, hi_ref, lb_ref, lanes[hh], c * CHUNK) for hh in heads_here]
        qs, ks, vs = [w["q"] for w in ws], [w["k"] for w in ws], [w["v"] for w in ws]
        bs, bms = _hgrn_prefix(tril_ref, sel_ref, [w["g"] for w in ws])
        for hh in heads_here:
            b_sc[hh] = bs[hh]
        b_lasts = [b_sc[hh, CHUNK - 1:CHUNK, :] for hh in heads_here]
        levels = [_hgrn_levels(qs[hh], ks[hh], bs[hh], bms[hh], blk_ref) for hh in heads_here]
        ps = _hgrn_intra(qs, ks, levels)
        s_ins = [shist_ref[0, hh] for hh in heads_here]
        ds_outs = [ds_sc[hh] for hh in heads_here]
        ebs = [jnp.exp(b) for b in bs]
        etails = [jnp.exp(b_lasts[hh] - bs[hh]) for hh in heads_here]
        decays = [jnp.exp(bl) for bl in b_lasts]

        dos = []
        for hh in heads_here:
            o = orec_ref[:, lanes[hh]]
            hg = hg_ref[:, lanes[hh]]
            sgg = _sigmoid(hg)
            rn = lax.rsqrt(jnp.mean(o * o, axis=-1, keepdims=True) + NORM_EPS)
            on = o * rn
            doh = do_ref[:, lanes[hh]]
            dy = doh * (hg * sgg)
            dhg_ref[:, lanes[hh]] = (doh * (on * gn_ref[...]) * (sgg * (1.0 + hg * (1.0 - sgg)))).astype(BF16)
            dgn_ref[hh] += jnp.broadcast_to(jnp.sum(dy * on, axis=0, keepdims=True), (8, HEAD))
            z = dy * gn_ref[...]
            dos.append(rn * z - o * ((rn * rn * rn) * jnp.mean(z * o, axis=-1, keepdims=True)))

        tt = lax.broadcasted_iota(jnp.int32, (CHUNK, CHUNK), 0)
        ss = lax.broadcasted_iota(jnp.int32, (CHUNK, CHUNK), 1)
        dps = [jnp.where(ss <= tt, _dot(dos[hh], vs[hh], NT), 0.0) for hh in heads_here]
        dvs = [_dot(ps[hh], dos[hh], TN) + _dot(ks[hh] * etails[hh], ds_outs[hh], NT) for hh in heads_here]
        dq_states = [ebs[hh] * _dot(dos[hh], s_ins[hh], NN) for hh in heads_here]
        dk_states = [etails[hh] * _dot(vs[hh], ds_outs[hh], NN) for hh in heads_here]
        dpds = [jnp.sum(jnp.where(tt == ss, dp, 0.0), axis=-1, keepdims=True) for dp in dps]
        dqs = [dpds[hh] * ks[hh] + dq_states[hh] for hh in heads_here]
        dks = [dpds[hh] * qs[hh] + dk_states[hh] for hh in heads_here]
        pair_terms = [[] for _ in heads_here]
        for lv in range(LEVELS):
            for hh in heads_here:
                eq, ek, ql, kl, same = levels[hh][lv]
                dpl = dps[hh] * same
                dq_l = eq * _dot(dpl, kl, NN)
                dk_l = ek * _dot(dpl, ql, TN)
                dqs[hh] = dqs[hh] + dq_l
                dks[hh] = dks[hh] + dk_l
                pair_terms[hh].append(qs[hh] * dq_l + ks[hh] * dk_l)
        for hh in heads_here:
            pair_terms[hh] += [qs[hh] * dq_states[hh], ks[hh] * dk_states[hh]]
            ds_sc[hh] = decays[hh] * ds_outs[hh] + _dot(dos[hh], qs[hh] * ebs[hh], TN)
        stacked = [jnp.concatenate(terms, axis=0) for terms in pair_terms]
        dg_all = _dot_exact_rhs(tdec_ref[...], jnp.concatenate(stacked, axis=1) if per > 1 else stacked[0], NN)

        for hh in heads_here:
            w = ws[hh]
            f, sg, lb, sq, hq = w["f"], w["sg"], w["lb"], w["sq"], w["hq"]
            through = jnp.sum((decays[hh] * s_ins[hh]) * ds_outs[hh], axis=0, keepdims=True)
            dg = dg_all[:, lanes[hh]] + through
            df = jnp.where(w["valid"], dg / f - dks[hh], 0.0)
            dhf_ref[:, lanes[hh]] = (df * (1.0 - lb) * sg * (1.0 - sg)).astype(BF16)
            dlb_ref[hh] += jnp.broadcast_to(jnp.sum(df * (1.0 - sg), axis=0, keepdims=True), (8, HEAD))
            dhq_ref[:, lanes[hh]] = (dqs[hh] * (sq * (1.0 + hq * (1.0 - sq)))).astype(BF16)
            dhi_ref[:, lanes[hh]] = dvs[hh].astype(BF16)

    col = lambda grp: pl.BlockSpec((CHUNK, per * HEAD), lambda h, c: (nc - 1 - c, grp * ng + h))
    const = lambda shape: pl.BlockSpec(shape, lambda h, c: (0, 0))
    tile = pl.BlockSpec((CHUNK, per * HEAD), lambda h, c: (nc - 1 - c, h))
    part = pl.BlockSpec((per, 8, HEAD), lambda h, c: (h, 0, 0))
    return pl.pallas_call(
        body, name="hgrn_bwd",
        out_shape=tuple([jax.ShapeDtypeStruct((L, D), BF16)] * 4 + [jax.ShapeDtypeStruct((nh, 8, HEAD), F32)] * 2),
        grid=(ng, nc),
        in_specs=[col(0), col(1), col(2), col(3),
                  pl.BlockSpec((2, per * HEAD), lambda h, c: (0, h)), const((1, HEAD)),
                  const((CHUNK, CHUNK)), const((LEVELS * CHUNK, CHUNK)), const((LEVELS * CHUNK, CHUNK)),
                  const((CHUNK, (LEVELS + 2) * CHUNK)),
                  tile, tile, pl.BlockSpec((1, per, HEAD, HEAD), lambda h, c: (nc - 1 - c, h, 0, 0))],
        out_specs=(tile, tile, tile, tile, part, part),
        scratch_shapes=[pltpu.VMEM((per, HEAD, HEAD), F32), pltpu.VMEM((per, CHUNK, HEAD), F32)],
        compiler_params=_params("parallel", "arbitrary"),
    )(hp, hp, hp, hp, lb_raw, g_norm, tril, sel, blk, tdec, do_h, o_rec, s_hist)


def _merge_fwd(a, bm, gates):
    L, D = a.shape
    tm = _tile(L, (128,))

    def body(a_ref, b_ref, g_ref, o_ref):
        o_ref[...] = (_sigmoid(g_ref[:, :D]) * a_ref[...] + _sigmoid(g_ref[:, D:]) * b_ref[...]).astype(BF16)

    row = lambda w: pl.BlockSpec((tm, w), lambda i: (i, 0))
    return pl.pallas_call(
        body, name="merge_fwd", out_shape=jax.ShapeDtypeStruct((L, D), BF16), grid=(L // tm,),
        in_specs=[row(D), row(D), row(2 * D)], out_specs=row(D), compiler_params=_params("parallel"),
    )(a, bm, gates)


def _merge_bwd(dm, a, bm, gates):
    L, D = a.shape
    tm = _tile(L, (128,))

    def body(dm_ref, a_ref, b_ref, g_ref, da_ref, db_ref, dg_ref):
        d = dm_ref[...]
        sa, sb = _sigmoid(g_ref[:, :D]), _sigmoid(g_ref[:, D:])
        da_ref[...] = (d * sa).astype(BF16)
        db_ref[...] = (d * sb).astype(BF16)
        dg_ref[:, :D] = (d * a_ref[...] * sa * (1.0 - sa)).astype(BF16)
        dg_ref[:, D:] = (d * b_ref[...] * sb * (1.0 - sb)).astype(BF16)

    row = lambda w: pl.BlockSpec((tm, w), lambda i: (i, 0))
    return pl.pallas_call(
        body, name="merge_bwd",
        out_shape=(jax.ShapeDtypeStruct((L, D), BF16), jax.ShapeDtypeStruct((L, D), BF16),
                   jax.ShapeDtypeStruct((L, 2 * D), BF16)),
        grid=(L // tm,),
        in_specs=[row(D), row(D), row(D), row(2 * D)], out_specs=(row(D), row(D), row(2 * D)),
        compiler_params=_params("parallel"),
    )(dm, a, bm, gates)


def _conv_taps(g_ref, halo_ref, i, tm):
    rows = i * tm + lax.broadcasted_iota(jnp.int32, (tm, 1), 0)
    g0 = jnp.where(rows >= PAD_LEN, g_ref[...].astype(F32), 0.0)
    sub = lax.broadcasted_iota(jnp.int32, (HALO, 1), 0)
    hrow = i * tm - HALO + sub
    halo = jnp.where(hrow >= PAD_LEN, halo_ref[...].astype(F32), 0.0)
    r = lax.broadcasted_iota(jnp.int32, (tm, 1), 0)
    h7 = jnp.sum(jnp.where(sub == HALO - 1, halo, 0.0), axis=0, keepdims=True)
    h6 = jnp.sum(jnp.where(sub == HALO - 2, halo, 0.0), axis=0, keepdims=True)
    g1 = jnp.where(r == 0, h7, pltpu.roll(g0, 1, 0))
    g2 = jnp.where(r == 0, h6, jnp.where(r == 1, h7, pltpu.roll(g0, 2, 0)))
    return g0, g1, g2


def _conv_fwd(gu, cw, cb):
    L, F2 = gu.shape
    F = F2 // 2
    tm = _tile(L, ROW_TILES)
    tn = _tile(F, (512, 256, 128))
    nj = F // tn

    def body(g_ref, halo_ref, u_ref, cw_ref, cb_ref, o_ref):
        g0, g1, g2 = _conv_taps(g_ref, halo_ref, pl.program_id(0), tm)
        conv = cw_ref[0:1, :] * g2 + cw_ref[1:2, :] * g1 + cw_ref[2:3, :] * g0 + cb_ref[...]
        o_ref[...] = (conv * _sigmoid(conv) * u_ref[...].astype(F32)).astype(BF16)

    return pl.pallas_call(
        body, name="conv_fwd", out_shape=jax.ShapeDtypeStruct((L, F), BF16), grid=(L // tm, nj),
        in_specs=[pl.BlockSpec((tm, tn), lambda i, j: (i, j)),
                  pl.BlockSpec((HALO, tn), lambda i, j: (jnp.maximum(i * (tm // HALO) - 1, 0), j)),
                  pl.BlockSpec((tm, tn), lambda i, j: (i, j + nj)),
                  pl.BlockSpec((3, tn), lambda i, j: (0, j)),
                  pl.BlockSpec((1, tn), lambda i, j: (0, j))],
        out_specs=pl.BlockSpec((tm, tn), lambda i, j: (i, j)),
        compiler_params=_params("parallel", "parallel"),
    )(gu, gu, gu, cw, cb)


def _conv_bwd_a(da, gu, cw, cb):
    L, F2 = gu.shape
    F = F2 // 2
    tm = _tile(L, ROW_TILES)
    tn = _tile(F, (512, 256, 128))
    nj = F // tn

    def body(da_ref, g_ref, halo_ref, u_ref, cw_ref, cb_ref, dc_ref, du_ref, dcb_ref, dcw_ref):
        i = pl.program_id(1)

        @pl.when(i == 0)
        def _():
            dcb_ref[...] = jnp.zeros_like(dcb_ref)
            dcw_ref[...] = jnp.zeros_like(dcw_ref)

        g0, g1, g2 = _conv_taps(g_ref, halo_ref, i, tm)
        conv = cw_ref[0:1, :] * g2 + cw_ref[1:2, :] * g1 + cw_ref[2:3, :] * g0 + cb_ref[...]
        sc = _sigmoid(conv)
        dav = da_ref[...]
        du_ref[...] = (dav * (conv * sc)).astype(BF16)
        dconv = dav * u_ref[...].astype(F32) * (sc * (1.0 + conv * (1.0 - sc)))
        dc_ref[...] = dconv
        dcb_ref[...] += jnp.sum(dconv, axis=0, keepdims=True)
        dcw_ref[0:1, :] += jnp.sum(dconv * g2, axis=0, keepdims=True)
        dcw_ref[1:2, :] += jnp.sum(dconv * g1, axis=0, keepdims=True)
        dcw_ref[2:3, :] += jnp.sum(dconv * g0, axis=0, keepdims=True)

    return pl.pallas_call(
        body, name="conv_bwd_a",
        out_shape=(jax.ShapeDtypeStruct((L, F), F32), jax.ShapeDtypeStruct((L, 2 * F), BF16),
                   jax.ShapeDtypeStruct((1, F), F32), jax.ShapeDtypeStruct((8, F), F32)),
        grid=(nj, L // tm),
        in_specs=[pl.BlockSpec((tm, tn), lambda j, i: (i, j)),
                  pl.BlockSpec((tm, tn), lambda j, i: (i, j)),
                  pl.BlockSpec((HALO, tn), lambda j, i: (jnp.maximum(i * (tm // HALO) - 1, 0), j)),
                  pl.BlockSpec((tm, tn), lambda j, i: (i, j + nj)),
                  pl.BlockSpec((3, tn), lambda j, i: (0, j)),
                  pl.BlockSpec((1, tn), lambda j, i: (0, j))],
        out_specs=(pl.BlockSpec((tm, tn), lambda j, i: (i, j)), pl.BlockSpec((tm, tn), lambda j, i: (i, j + nj)),
                   pl.BlockSpec((1, tn), lambda j, i: (0, j)), pl.BlockSpec((8, tn), lambda j, i: (0, j))),
        compiler_params=_params("parallel", "arbitrary"),
    )(da, gu, gu, gu, cw, cb)


def _conv_bwd_b(dconv, cw, dgu):
    L, F = dconv.shape
    tm = _tile(L, ROW_TILES)
    tn = _tile(F, (512, 256, 128))
    nblk8 = L // 8
    ni = L // tm

    def body(dc_ref, nxt_ref, cw_ref, dgu_ref, o_ref):
        i = pl.program_id(0)
        dc = dc_ref[...]
        nxt = jnp.where(i < ni - 1, nxt_ref[...], 0.0)
        sub = lax.broadcasted_iota(jnp.int32, (8, 1), 0)
        n0 = jnp.sum(jnp.where(sub == 0, nxt, 0.0), axis=0, keepdims=True)
        n1 = jnp.sum(jnp.where(sub == 1, nxt, 0.0), axis=0, keepdims=True)
        r = lax.broadcasted_iota(jnp.int32, (tm, 1), 0)
        d1 = jnp.where(r == tm - 1, n0, pltpu.roll(dc, tm - 1, 0))
        d2 = jnp.where(r == tm - 2, n0, jnp.where(r == tm - 1, n1, pltpu.roll(dc, tm - 2, 0)))
        dg = cw_ref[2:3, :] * dc + cw_ref[1:2, :] * d1 + cw_ref[0:1, :] * d2
        rows = i * tm + r
        o_ref[...] = jnp.where(rows >= PAD_LEN, dg, 0.0).astype(BF16)

    return pl.pallas_call(
        body, name="conv_bwd_b", out_shape=jax.ShapeDtypeStruct(dgu.shape, BF16), grid=(ni, F // tn),
        in_specs=[pl.BlockSpec((tm, tn), lambda i, j: (i, j)),
                  pl.BlockSpec((8, tn), lambda i, j: (jnp.minimum((i + 1) * (tm // 8), nblk8 - 1), j)),
                  pl.BlockSpec((3, tn), lambda i, j: (0, j)), ANY],
        out_specs=pl.BlockSpec((tm, tn), lambda i, j: (i, j)),
        input_output_aliases={3: 0},
        compiler_params=_params("parallel", "parallel"),
    )(dconv, dconv, cw, dgu)


ANY = pl.BlockSpec(memory_space=pl.ANY)


def _coords():
    return lax.axis_index("x"), lax.axis_index("y"), lax.axis_index("c")


def _flip(v, bit):
    return 1 - v if bit else v


CHIPS = [(1, 0), (0, 1), (1, 1)]
PEERS = [(dx, dy, dc) for dx in (0, 1) for dy in (0, 1) for dc in (0, 1)][1:]


class _Rider:
    def __init__(self, operands, out_shapes, sem_shapes, start, finish):
        self.operands, self.out_shapes, self.sem_shapes = list(operands), list(out_shapes), list(sem_shapes)
        self.start, self.finish = start, finish


def _run_rider(rider, name):
    n_in, n_out = len(rider.operands), len(rider.out_shapes)

    def body(*refs):
        ins, outs, sems = refs[:n_in], refs[n_in:n_in + n_out], refs[n_in + n_out:]
        rider.start(ins, outs, sems)
        rider.finish(ins, outs, sems)

    return pl.pallas_call(
        body, name=name, out_shape=tuple(rider.out_shapes),
        in_specs=[ANY] * n_in, out_specs=tuple([ANY] * n_out), scratch_shapes=rider.sem_shapes,
        compiler_params=pltpu.CompilerParams(has_side_effects=True),
    )(*rider.operands)


def _gather_rider(big, small):
    nbig, n = len(big), len(big) + len(small)
    arrays = list(big) + list(small)

    def plan(ins, outs, sems):
        ici_send, ici_recv, d2d_send, d2d_recv, local_sems = sems
        x, y, c = _coords()
        mine = 2 * x + y

        def half(w, h):
            r2 = arrays[w].shape[0] // 2
            return pl.ds(h * r2, r2)

        def ici(w, j, landing):
            px, py = _flip(x, CHIPS[j][0]), _flip(y, CHIPS[j][1])
            slot = 2 * px + py if landing else mine
            if w < nbig:
                src, dst = ins[w].at[half(w, c)], outs[w].at[slot, half(w, c)]
            else:
                src, dst = ins[w], outs[w].at[slot]
            return pltpu.make_async_remote_copy(
                src_ref=src, dst_ref=dst, send_sem=ici_send.at[w * 3 + j], recv_sem=ici_recv.at[w * 3 + j],
                device_id=(px, py, c), device_id_type=MESH)

        def d2d(w, j, landing):
            px, py = _flip(x, CHIPS[j][0]), _flip(y, CHIPS[j][1])
            mine_rows = outs[w].at[2 * px + py, half(w, c)]
            dst = outs[w].at[2 * px + py, half(w, 1 - c)] if landing else mine_rows
            return pltpu.make_async_remote_copy(
                src_ref=mine_rows, dst_ref=dst, send_sem=d2d_send.at[w * 3 + j], recv_sem=d2d_recv.at[w * 3 + j],
                device_id=(x, y, 1 - c), device_id_type=MESH)

        local = [pltpu.make_async_copy(ins[w], outs[w].at[mine], local_sems.at[w]) for w in range(n)]
        return ici, d2d, local

    def start(ins, outs, sems):
        ici, _, local = plan(ins, outs, sems)
        for cp in local:
            cp.start()
        for w in range(n):
            for j in range(3):
                ici(w, j, False).start()

    def finish(ins, outs, sems):
        ici, d2d, local = plan(ins, outs, sems)
        for w in range(n):
            for j in range(3):
                ici(w, j, True).wait_recv()
                if w < nbig:
                    d2d(w, j, False).start()
        for w in range(nbig):
            for j in range(3):
                d2d(w, j, True).wait_recv()
        for w in range(n):
            for j in range(3):
                ici(w, j, False).wait_send()
                if w < nbig:
                    d2d(w, j, False).wait_send()
        for cp in local:
            cp.wait()

    return _Rider(
        arrays, [jax.ShapeDtypeStruct((4,) + s.shape, s.dtype) for s in arrays],
        [pltpu.SemaphoreType.DMA((3 * n,)), pltpu.SemaphoreType.DMA((3 * n,)),
         pltpu.SemaphoreType.DMA((max(3 * nbig, 1),)), pltpu.SemaphoreType.DMA((max(3 * nbig, 1),)),
         pltpu.SemaphoreType.DMA((n,))],
        start, finish)


def _to_sibling(arrays, name):
    n = len(arrays)

    def body(*refs):
        ins, outs = refs[:n], refs[n:2 * n]
        send_sems, recv_sems = refs[2 * n:]
        x, y, c = _coords()

        def copy(w):
            return pltpu.make_async_remote_copy(
                src_ref=ins[w], dst_ref=outs[w], send_sem=send_sems.at[w], recv_sem=recv_sems.at[w],
                device_id=(x, y, 1 - c), device_id_type=MESH)

        for w in range(n):
            copy(w).start()
        for w in range(n):
            copy(w).wait_recv()
            copy(w).wait_send()

    return pl.pallas_call(
        body, name=name,
        out_shape=tuple(jax.ShapeDtypeStruct(a.shape, a.dtype) for a in arrays),
        in_specs=[ANY] * n, out_specs=tuple([ANY] * n),
        scratch_shapes=[pltpu.SemaphoreType.DMA((n,)), pltpu.SemaphoreType.DMA((n,))],
        compiler_params=pltpu.CompilerParams(has_side_effects=True),
    )(*arrays)


def _pair_sum(a, b, name):
    _, r, c = a.shape
    tr = _tile(r, (64, 32, 16))

    def body(a_ref, b_ref, o_ref):
        o_ref[...] = (a_ref[...].astype(F32) + b_ref[...].astype(F32)).astype(BF16)

    blk = pl.BlockSpec((4, tr, c), lambda i: (0, i, 0))
    return pl.pallas_call(
        body, name=name, out_shape=jax.ShapeDtypeStruct(a.shape, BF16), grid=(r // tr,),
        in_specs=[blk, blk], out_specs=blk, compiler_params=_params("parallel"),
    )(a, b)


def _scatter_rider(parts):
    n = len(parts)

    def plan(ins, outs, sems):
        send_sems, recv_sems, local_sems = sems
        x, y, c = _coords()
        mine = 2 * x + y

        def ici(w, j, landing):
            px, py = _flip(x, CHIPS[j][0]), _flip(y, CHIPS[j][1])
            return pltpu.make_async_remote_copy(
                src_ref=ins[w].at[2 * px + py], dst_ref=outs[w].at[2 * px + py if landing else mine],
                send_sem=send_sems.at[w * 3 + j], recv_sem=recv_sems.at[w * 3 + j],
                device_id=(px, py, c), device_id_type=MESH)

        local = [pltpu.make_async_copy(ins[w].at[mine], outs[w].at[mine], local_sems.at[w]) for w in range(n)]
        return ici, local

    def start(ins, outs, sems):
        ici, local = plan(ins, outs, sems)
        for cp in local:
            cp.start()
        for w in range(n):
            for j in range(3):
                ici(w, j, False).start()

    def finish(ins, outs, sems):
        ici, local = plan(ins, outs, sems)
        for w in range(n):
            for j in range(3):
                ici(w, j, True).wait_recv()
                ici(w, j, False).wait_send()
        for cp in local:
            cp.wait()

    return _Rider(
        parts, [jax.ShapeDtypeStruct(p.shape, p.dtype) for p in parts],
        [pltpu.SemaphoreType.DMA((3 * n,)), pltpu.SemaphoreType.DMA((3 * n,)), pltpu.SemaphoreType.DMA((n,))],
        start, finish)


def _sum4(recv, name):
    _, r, c = recv.shape
    tr = _tile(r, (64, 32, 16))

    def body(in_ref, o_ref):
        o_ref[...] = ((in_ref[0].astype(F32) + in_ref[1].astype(F32)) + in_ref[2].astype(F32)) + in_ref[3].astype(F32)

    return pl.pallas_call(
        body, name=name, out_shape=jax.ShapeDtypeStruct((r, c), F32), grid=(r // tr,),
        in_specs=[pl.BlockSpec((4, tr, c), lambda i: (0, i, 0))],
        out_specs=pl.BlockSpec((tr, c), lambda i: (i, 0)),
        compiler_params=_params("parallel"),
    )(recv)


def _allreduce_small(packed):
    R = packed.shape[0]

    def body(in_ref, o_ref, buf, send_sems, recv_sems):
        x, y, c = _coords()
        me = 4 * x + 2 * y + c
        buf[me] = in_ref[...]
        for j, (dx, dy, dc) in enumerate(PEERS):
            px, py, pc = _flip(x, dx), _flip(y, dy), _flip(c, dc)
            pltpu.make_async_remote_copy(
                src_ref=in_ref, dst_ref=buf.at[me], send_sem=send_sems.at[j], recv_sem=recv_sems.at[j],
                device_id=(px, py, pc), device_id_type=MESH).start()
        for j, (dx, dy, dc) in enumerate(PEERS):
            px, py, pc = _flip(x, dx), _flip(y, dy), _flip(c, dc)
            rc = pltpu.make_async_remote_copy(
                src_ref=in_ref, dst_ref=buf.at[4 * px + 2 * py + pc], send_sem=send_sems.at[j],
                recv_sem=recv_sems.at[j], device_id=(px, py, pc), device_id_type=MESH)
            rc.wait_recv()
            rc.wait_send()
        acc = buf[0]
        for s in range(1, 8):
            acc = acc + buf[s]
        o_ref[...] = acc

    return pl.pallas_call(
        body, name="allreduce_small", out_shape=jax.ShapeDtypeStruct((R, 128), F32),
        in_specs=[pl.BlockSpec(memory_space=pltpu.VMEM)], out_specs=pl.BlockSpec(memory_space=pltpu.VMEM),
        scratch_shapes=[pltpu.VMEM((8, R, 128), F32), pltpu.SemaphoreType.DMA((7,)), pltpu.SemaphoreType.DMA((7,))],
        compiler_params=pltpu.CompilerParams(has_side_effects=True, vmem_limit_bytes=VMEM_LIMIT_BYTES),
    )(packed)


def _adamw_math(w, g, m, v):
    m = ADAM_B1 * m + (1.0 - ADAM_B1) * g
    v = ADAM_B2 * v + (1.0 - ADAM_B2) * (g * g)
    m_hat = m / (1.0 - ADAM_B1 ** ADAM_STEP)
    v_hat = v / (1.0 - ADAM_B2 ** ADAM_STEP)
    delta = -ADAM_LR * (m_hat / (jnp.sqrt(v_hat) + ADAM_EPS) + ADAM_WD * w)
    return delta, m, v


def _adamw_big(mine, other, w, m, v, name):
    R, C = w.shape
    tr = _tile(R // 2, (128, 64, 32, 16, 8))
    nb = (R // 2) // tr

    def body(mine_ref, other_ref, w_ref, m_ref, v_ref, g_ref, d_ref, mo_ref, vo_ref):
        is_mine = (pl.program_id(0) // nb) == lax.axis_index("c")
        g = jnp.where(is_mine, mine_ref[...], other_ref[...])
        d, mn, vn = _adamw_math(w_ref[...], g, m_ref[...], v_ref[...])
        g_ref[...] = g
        d_ref[...] = d
        mo_ref[...] = mn
        vo_ref[...] = vn

    blk = pl.BlockSpec((tr, C), lambda i: (i, 0))
    half = pl.BlockSpec((tr, C), lambda i: (i % nb, 0))
    sds = jax.ShapeDtypeStruct((R, C), F32)
    return pl.pallas_call(
        body, name=name, out_shape=(sds, sds, sds, sds), grid=(2 * nb,),
        in_specs=[half, half, blk, blk, blk], out_specs=(blk, blk, blk, blk), compiler_params=_params("parallel"),
    )(mine, other, w, m, v)


def _adamw_small(items, lb_raw, dlb):
    n = len(items)
    lb_w, lb_m, lb_v = lb_raw

    def body(*refs):
        ins = refs[:4 * n]
        dlb_ref, lw_ref, lm_ref, lv_ref = refs[4 * n:4 * n + 4]
        outs = refs[4 * n + 4:]
        for t in range(n):
            g_ref, w_ref, m_ref, v_ref = ins[4 * t:4 * t + 4]
            d, mn, vn = _adamw_math(w_ref[...], g_ref[...], m_ref[...], v_ref[...])
            outs[3 * t][...] = d
            outs[3 * t + 1][...] = mn
            outs[3 * t + 2][...] = vn
        p0 = 1.0 / (1.0 + jnp.exp(lw_ref[1:2, :] - lw_ref[0:1, :]))
        g0 = dlb_ref[...] * p0 * (1.0 - p0)
        base = 3 * n
        outs[base][0:1, :] = g0
        outs[base][1:2, :] = -g0
        d, mn, vn = _adamw_math(lw_ref[...], outs[base][...], lm_ref[...], lv_ref[...])
        outs[base + 1][...] = d
        outs[base + 2][...] = mn
        outs[base + 3][...] = vn

    operands = [a for it in items for a in it] + [dlb, lb_w, lb_m, lb_v]
    out_shape = []
    for (g, w, m, v) in items:
        out_shape += [jax.ShapeDtypeStruct(w.shape, F32)] * 3
    out_shape += [jax.ShapeDtypeStruct(lb_w.shape, F32)] * 4
    vm = pl.BlockSpec(memory_space=pltpu.VMEM)
    res = pl.pallas_call(
        body, name="adamw_small", out_shape=tuple(out_shape),
        in_specs=[vm] * len(operands), out_specs=tuple([vm] * len(out_shape)),
        compiler_params=pltpu.CompilerParams(vmem_limit_bytes=VMEM_LIMIT_BYTES),
    )(*operands)
    deltas = [res[3 * t] for t in range(n)] + [res[3 * n + 1]]
    new_m = [res[3 * t + 1] for t in range(n)] + [res[3 * n + 2]]
    new_v = [res[3 * t + 2] for t in range(n)] + [res[3 * n + 3]]
    return res[3 * n], deltas, new_m, new_v


def _shard_row_half(g, by_cols, h):
    R, C = g.shape
    if by_cols:
        part = lax.dynamic_index_in_dim(g.reshape(2, R // 2, 4, C // 4), h, axis=0, keepdims=False)
        return part.transpose(1, 0, 2).astype(BF16)
    return lax.dynamic_index_in_dim(g.reshape(4, 2, R // 8, C), h, axis=1, keepdims=False).astype(BF16)


def kernel(x, positions, meta_tokens, w_in, w_q_up, w_kv_up, w_branch_mla, w_branch_hgrn, w_out, w_ffn_in, w_ffn_out, conv_w, conv_b, g_mix_norm, g_q_norm, g_kv_norm, g_hgrn_norm, g_ffn_norm, g_final_norm, lb_raw, loss_target, m_meta_tokens, m_w_in, m_w_q_up, m_w_kv_up, m_w_branch_mla, m_w_branch_hgrn, m_w_out, m_w_ffn_in, m_w_ffn_out, m_conv_w, m_conv_b, m_g_mix_norm, m_g_q_norm, m_g_kv_norm, m_g_hgrn_norm, m_g_ffn_norm, m_g_final_norm, m_lb_raw, v_meta_tokens, v_w_in, v_w_q_up, v_w_kv_up, v_w_branch_mla, v_w_branch_hgrn, v_w_out, v_w_ffn_in, v_w_ffn_out, v_conv_w, v_conv_b, v_g_mix_norm, v_g_q_norm, v_g_kv_norm, v_g_hgrn_norm, v_g_ffn_norm, v_g_final_norm, v_lb_raw):
    S, D = x.shape[1], x.shape[2]
    L = S + PREFIX
    QL, KVL = g_q_norm.shape[1], g_kv_norm.shape[1]
    F = conv_b.shape[1]
    heads = (4 * w_kv_up.shape[2]) // QPAD
    nh = D // HEAD
    assert lb_raw.shape[0] == 2 and g_hgrn_norm.shape[1] == HEAD and L % CHUNK == 0
    ix, iy, ic = _coords()
    chip = 2 * ix + iy

    big = [w_in, w_q_up, w_kv_up, w_branch_mla, w_branch_hgrn, w_out, w_ffn_in, w_ffn_out]
    col_sharded = [True, True, True, False, False, False, True, False]
    shards = [w[0].astype(BF16) for w in big]
    early = _run_rider(_gather_rider(shards[:1], [meta_tokens, conv_w[0]]), "gather_early")
    qkv_rider = _gather_rider(shards[1:3], [])
    late_rider = _gather_rider(shards[3:], [])

    def full(gw, by_cols):
        _, r, c = gw.shape
        return gw.transpose(1, 0, 2).reshape(r, 4 * c) if by_cols else gw.reshape(4 * r, c)

    def col_range(g4, lo, hi):
        c = g4.shape[2]
        pieces = [g4[k][:, max(lo - k * c, 0):min(hi - k * c, c)] for k in range(4) if lo < (k + 1) * c and hi > k * c]
        return pieces[0] if len(pieces) == 1 else jnp.concatenate(pieces, axis=1)

    meta_full = full(early[1], True)
    cw_full = full(early[2], True)
    c0 = QL + KVL
    W_lat = col_range(early[0], 0, c0)
    W_kr = jnp.pad(col_range(early[0], c0, c0 + ROPE), ((0, 0), (0, 128 - ROPE)))
    W_H = col_range(early[0], c0 + ROPE, c0 + ROPE + 4 * D)
    W_G = col_range(early[0], c0 + ROPE + 4 * D, 4 * early[0].shape[2])

    pos = jnp.concatenate([jnp.zeros((PAD_LEN,), jnp.int32), jnp.arange(N_META, dtype=jnp.int32),
                           positions[0].astype(jnp.int32) + N_META])
    inv = 1.0 / (ROPE_THETA ** (jnp.arange(0, ROPE, 2, dtype=F32) / ROPE))
    ang = pos.astype(F32)[:, None] * inv
    zero = jnp.zeros((L, 128 - ROPE), F32)
    cos = jnp.concatenate([jnp.cos(ang), jnp.cos(ang), zero], axis=1)
    sin = jnp.concatenate([jnp.sin(ang), jnp.sin(ang), zero], axis=1)

    h0, u1 = _embed_norm(x[0], meta_full, g_mix_norm)
    hp, qkv = _mm(u1, W_H, "nn", F32, "proj_hgrn", rider=qkv_rider)
    W_q, W_kv = [full(gw, True) for gw in qkv]
    W_qp = jnp.pad(W_q.reshape(QL, heads, QK_HEAD), ((0, 0), (0, 0), (0, QPAD - QK_HEAD))).reshape(QL, heads * QPAD)
    lat = _mm(u1, W_lat, "nn", F32, "proj_lat")
    gates = _mm(u1, W_G, "nn", F32, "proj_gates")
    kr = _mm(u1, W_kr, "nn", F32, "proj_krope")
    qn = _rmsnorm_fwd(lat, g_q_norm, "norm_q", col0=0, width=QL)
    kvn = _rmsnorm_fwd(lat, g_kv_norm, "norm_kv", col0=QL, width=KVL)
    qp = _mm(qn, W_qp, "nn", BF16, "q_up")
    kv = _mm(kvn, W_kv, "nn", BF16, "kv_up")
    qc, kc, vv = _rope_fwd(qp, kv, kr, cos, sin, heads)
    o_mla, lse, late = _attn_fwd(qc, kc, vv, heads, late_rider)
    W_a, W_b, W_o, W_fi, W_fo = [full(gw, bc) for gw, bc in zip(late, col_sharded[3:])]
    o_hgrn, o_rec, s_hist = _hgrn_fwd(hp, lb_raw, g_hgrn_norm, nh)
    br_a = _mm(o_mla, W_a, "nn", F32, "branch_mla")
    br_b = _mm(o_hgrn, W_b, "nn", F32, "branch_hgrn")
    merged = _merge_fwd(br_a, br_b, gates)
    h1 = _mm(merged, W_o, "nn", F32, "out_proj", res=h0)
    u2 = _rmsnorm_fwd(h1, g_ffn_norm, "norm_ffn")
    gu = _mm(u2, W_fi, "nn", BF16, "ffn_in")
    act = _conv_fwd(gu, cw_full, conv_b)
    h2 = _mm(act, W_fo, "nn", F32, "ffn_out", res=h1)
    dh2, loss_p, dg_final = _final_loss_bwd(h2, loss_target[0], g_final_norm.reshape(1, D))

    dact = _mm(dh2, W_fo, "nt", F32, "d_act")
    dW_fo = _mm(act, dh2, "tn", F32, "dw_ffn_out")
    dconv, dgu_right, dcb, dcw = _conv_bwd_a(dact, gu, cw_full, conv_b)
    dgu = _conv_bwd_b(dconv, cw_full, dgu_right)
    du2 = _mm(dgu, W_fi, "nt", F32, "d_u2")
    dW_fi = _mm(u2, dgu, "tn", F32, "dw_ffn_in")
    dh1, dg_ffn = _rmsnorm_bwd(du2, h1, g_ffn_norm, "norm_ffn_bwd", F32, res=dh2)
    dmerged = _mm(dh1, W_o, "nt", F32, "d_merged")
    dW_o = _mm(merged, dh1, "tn", F32, "dw_out")
    d_a, d_b, d_gates = _merge_bwd(dmerged, br_a, br_b, gates)
    do_mla = _mm(d_a, W_a, "nt", BF16, "d_o_mla")
    dW_a = _mm(o_mla, d_a, "tn", F32, "dw_branch_mla")
    do_hgrn = _mm(d_b, W_b, "nt", F32, "d_o_hgrn")
    dW_b = _mm(o_hgrn, d_b, "tn", F32, "dw_branch_hgrn")
    names = ["w_in", "w_q_up", "w_kv_up", "w_branch_mla", "w_branch_hgrn", "w_out", "w_ffn_in", "w_ffn_out"]

    def chip_partials(grads, by_cols, nms, tag):
        keep = [_shard_row_half(g, bc, ic) for g, bc in zip(grads, by_cols)]
        give = [_shard_row_half(g, bc, 1 - ic) for g, bc in zip(grads, by_cols)]
        taken = _to_sibling(give, "pair_exchange_" + tag)
        return [_pair_sum(a, b, "pair_sum_" + nm) for a, b, nm in zip(keep, taken, nms)]

    late_parts = chip_partials([dW_a, dW_b, dW_o, dW_fi, dW_fo], col_sharded[3:], names[3:], "late")
    dhq, dhf, dhi, dhg, dgn_p, dlb_p = _hgrn_bwd(hp, lb_raw, g_hgrn_norm, do_hgrn, o_rec, s_hist, nh)
    dqc, dkc, dvv, late_recv = _attn_bwd(qc, kc, vv, do_mla, lse, _attn_delta(o_mla, do_mla, heads), heads,
                                         _scatter_rider(late_parts))
    dqp, dkv, dkr = _rope_bwd(dqc, dkc, dvv, cos, sin, heads)
    dqn = _mm(dqp, W_qp, "nt", F32, "d_qn")
    dW_qp = _mm(qn, dqp, "tn", F32, "dw_q_up")
    dkvn = _mm(dkv, W_kv, "nt", F32, "d_kvn")
    dW_kv = _mm(kvn, dkv, "tn", F32, "dw_kv_up")
    dq_lat, dg_q = _rmsnorm_bwd(dqn, lat, g_q_norm, "norm_q_bwd", BF16, col0=0)
    dkv_lat, dg_kv = _rmsnorm_bwd(dkvn, lat, g_kv_norm, "norm_kv_bwd", BF16, col0=QL)
    dlat = jnp.concatenate([dq_lat, dkv_lat], axis=1)
    dhp = jnp.concatenate([dhq, dhf, dhi, dhg], axis=1)
    dW_lat = _mm(u1, dlat, "tn", F32, "dw_in_lat")
    dW_H = _mm(u1, dhp, "tn", F32, "dw_in_hgrn")
    dW_G = _mm(u1, d_gates, "tn", F32, "dw_in_gates")
    dW_kr = _mm(u1, dkr, "tn", F32, "dw_in_krope")
    dW_in = jnp.concatenate([dW_lat, dW_kr[:, :ROPE], dW_H, dW_G], axis=1)
    dW_q = dW_qp.reshape(QL, heads, QPAD)[:, :, :QK_HEAD].reshape(QL, heads * QK_HEAD)
    early_parts = chip_partials([dW_in, dW_q, dW_kv], col_sharded[:3], names[:3], "early")
    du1 = _mm(dlat, W_lat, "nt", F32, "d_u1_lat")
    du1, early_recv = _mm(dhp, W_H, "nt", F32, "d_u1_hgrn", res=du1, rider=_scatter_rider(early_parts))
    du1 = _mm(d_gates, W_G, "nt", F32, "d_u1_gates", res=du1)
    du1 = _mm(dkr, W_kr, "nt", F32, "d_u1_krope", res=du1)
    (dx_tokens, dh0_prefix), dg_mix = _rmsnorm_bwd(du1, h0, g_mix_norm, "norm_mix_bwd", F32, res=dh1,
                                                   split_prefix=True)
    grad_x = dx_tokens[None]

    received = list(early_recv) + list(late_recv)
    halves = [_sum4(r, "sum_" + nm) for r, nm in zip(received, names)]
    others = _to_sibling(halves, "swap_halves")
    big_m = [m_w_in, m_w_q_up, m_w_kv_up, m_w_branch_mla, m_w_branch_hgrn, m_w_out, m_w_ffn_in, m_w_ffn_out]
    big_v = [v_w_in, v_w_q_up, v_w_kv_up, v_w_branch_mla, v_w_branch_hgrn, v_w_out, v_w_ffn_in, v_w_ffn_out]
    big_out = {}
    for nm, mine, other, w, m, v in zip(names, halves, others, big, big_m, big_v):
        g, d, mn, vn = _adamw_big(mine, other, w[0], m[0], v[0], "adamw_" + nm)
        big_out[nm] = (g[None], d[None], mn[None], vn[None])

    pieces = [loss_p[:, :1], dg_mix, dg_q, dg_kv, jnp.sum(dgn_p[:, 0, :], axis=0, keepdims=True), dg_ffn, dg_final,
              dlb_p[:, 0, :].reshape(1, D), dcb, dcw[0:3].reshape(1, 3 * F), dh0_prefix[PAD_LEN:PREFIX].reshape(1, N_META * D)]
    sizes = [p.shape[1] for p in pieces]
    flat = jnp.concatenate(pieces, axis=1)[0]
    rows = -(-flat.shape[0] // 1024) * 8
    packed = jnp.pad(flat, (0, rows * 128 - flat.shape[0])).reshape(rows, 128)
    total = _allreduce_small(packed).reshape(-1)
    offs = [0]
    for s in sizes:
        offs.append(offs[-1] + s)
    loss, g_mix, g_q, g_kv, g_hg, g_ffn, g_fin, dlb, g_cb, g_cw, g_meta = [
        total[offs[t]:offs[t + 1]].reshape(1, sizes[t]) for t in range(len(sizes))]
    g_cw = lax.dynamic_slice_in_dim(g_cw.reshape(3, F), chip * (F // 4), F // 4, axis=1)
    g_meta = lax.dynamic_slice_in_dim(g_meta.reshape(N_META, D), chip * (D // 4), D // 4, axis=1)
    items = [(g_meta, meta_tokens, m_meta_tokens, v_meta_tokens),
             (g_cw, conv_w[0], m_conv_w[0], v_conv_w[0]),
             (g_cb, conv_b, m_conv_b, v_conv_b),
             (g_mix, g_mix_norm, m_g_mix_norm, v_g_mix_norm),
             (g_q, g_q_norm, m_g_q_norm, v_g_q_norm),
             (g_kv, g_kv_norm, m_g_kv_norm, v_g_kv_norm),
             (g_hg, g_hgrn_norm, m_g_hgrn_norm, v_g_hgrn_norm),
             (g_ffn, g_ffn_norm, m_g_ffn_norm, v_g_ffn_norm),
             (g_fin, g_final_norm.reshape(1, D), m_g_final_norm.reshape(1, D), v_g_final_norm.reshape(1, D))]
    g_lb, s_delta, s_m, s_v = _adamw_small(items, (lb_raw, m_lb_raw, v_lb_raw), dlb)
    s_grads = [it[0] for it in items] + [g_lb]

    def shape_small(vals):
        meta, cw, cb, mix, q, kvg, hg, ffn, fin, lb = vals
        return [meta, cw[None], cb, mix, q, kvg, hg, ffn, fin.reshape(D), lb]

    s_grads, s_delta, s_m, s_v = [shape_small(v) for v in (s_grads, s_delta, s_m, s_v)]

    def ordered(kind, small):
        bigs = [big_out[nm][kind] for nm in names]
        return [small[0]] + bigs + small[1:]

    return (loss.reshape(()), grad_x, *ordered(0, s_grads), *ordered(1, s_delta), *ordered(2, s_m), *ordered(3, s_v))
```

```python
import math

import jax
import jax.numpy as jnp
from jax import lax
from jax.experimental import pallas as pl
from jax.experimental.pallas import tpu as pltpu

F32 = jnp.float32
BF16 = jnp.bfloat16
MESH = pl.DeviceIdType.MESH

NORM_EPS = 1e-6
N_META = 16
PREFIX = 128
PAD_LEN = PREFIX - N_META
HEAD = 128
ROPE = 64
QK_HEAD = HEAD + ROPE
QPAD = 2 * HEAD
SOFTMAX_SCALE = QK_HEAD ** -0.5
ROPE_THETA = 10000.0
CHUNK = 128
HALO = 16
LEVELS = 7
HGRN_FWD_HEADS = 4
HGRN_BWD_HEADS = 4
ATTN_FWD_HEADS = 2
ATTN_BWD_HEADS = 2
NEG = -1e30

ADAM_LR = 0.001
ADAM_B1 = 0.9
ADAM_B2 = 0.999
ADAM_EPS = 1e-08
ADAM_WD = 0.01
ADAM_STEP = 10

VMEM_LIMIT_BYTES = 48 * 1024 * 1024


def _params(*sem):
    return pltpu.CompilerParams(dimension_semantics=sem, vmem_limit_bytes=VMEM_LIMIT_BYTES)


def _tile(n, prefs):
    for p in prefs:
        if n % p == 0:
            return p
    return n


ROW_TILES = (640, 512, 256, 128, 64, 32, 16, 8)
TN_ROW_TILES = (1024, 1408, 1536, 512, 256, 128)
TN_K_TILES = (1664, 640, 512, 256, 128)
TALL_ROW_TILE = 1664
SHARD_COL_TILES = (1408, 1024, 512, 256, 128)
COL_TILES = (1024, 512, 256, 128)
K_TILES = (2048, 1536, 1408, 1024, 512, 256, 128)


def _sigmoid(x):
    return 1.0 / (1.0 + jnp.exp(-x))


def _dot(a, b, dims):
    return lax.dot_general(a.astype(BF16), b.astype(BF16), (dims, ((), ())), preferred_element_type=F32)


NN = ((1,), (0,))
TN = ((0,), (0,))
NT = ((1,), (1,))


def _split3(x):
    hi = x.astype(BF16)
    r = x - hi.astype(F32)
    mid = r.astype(BF16)
    lo = (r - mid.astype(F32)).astype(BF16)
    return hi, mid, lo


def _dot_exact_rhs(sel, x, dims):
    hi, mid, lo = _split3(x)
    return _dot(sel, hi, dims) + _dot(sel, mid, dims) + _dot(sel, lo, dims)


def _mm(a, b, mode, out_dtype, name, res=None, rider=None, col_shards=1):
    if mode == "nn":
        (M, K), (K2, N) = a.shape, b.shape
    elif mode == "tn":
        (K, M), (K2, N) = a.shape, b.shape
    else:
        (M, K), (N, K2) = a.shape, b.shape
    assert K == K2, (name, a.shape, b.shape)
    tm = _tile(M, TN_ROW_TILES if mode == "tn" else ROW_TILES)
    tn = _tile(N // col_shards, COL_TILES if col_shards == 1 else SHARD_COL_TILES)
    tk = _tile(K, TN_K_TILES if mode == "tn" else K_TILES)
    nk = K // tk
    if mode != "tn" and nk > 1 and res is None and a.dtype == BF16 and M % TALL_ROW_TILE == 0:
        tm = TALL_ROW_TILE
    dims = {"nn": NN, "tn": TN, "nt": NT}[mode]

    n_rin = 0 if rider is None else len(rider.operands)
    n_rout = 0 if rider is None else len(rider.out_shapes)
    n_rsem = 0 if rider is None else len(rider.sem_shapes)
    grid = (M // tm, N // tn, nk)

    def body(*refs):
        main_in, r_ins, (o_ref,), r_outs, accs, r_sems = _split_refs(
            refs, (2 if res is None else 3, n_rin, 1, n_rout, 0 if nk == 1 else 1, n_rsem))
        a_ref, b_ref = main_in[:2]
        r_ref = None if res is None else main_in[2]
        i, j, k = pl.program_id(0), pl.program_id(1), pl.program_id(2)
        if rider is not None:
            pl.when((i == 0) & (j == 0) & (k == 0))(lambda: rider.start(r_ins, r_outs, r_sems))

        def finish(r):
            if r_ref is not None:
                r = r + r_ref[...].astype(F32)
            o_ref[...] = r.astype(out_dtype)

        if nk == 1:
            finish(_dot(a_ref[...], b_ref[...], dims))
        else:
            acc = accs[0]

            @pl.when(k == 0)
            def _():
                acc[...] = jnp.zeros_like(acc)

            acc[...] += _dot(a_ref[...], b_ref[...], dims)

            @pl.when(k == nk - 1)
            def _():
                finish(acc[...])

        if rider is not None:
            last = (i == grid[0] - 1) & (j == grid[1] - 1) & (k == nk - 1)
            pl.when(last)(lambda: rider.finish(r_ins, r_outs, r_sems))

    if mode == "nn":
        a_spec = pl.BlockSpec((tm, tk), lambda i, j, k: (i, k))
        b_spec = pl.BlockSpec((tk, tn), lambda i, j, k: (k, j))
    elif mode == "tn":
        a_spec = pl.BlockSpec((tk, tm), lambda i, j, k: (k, i))
        b_spec = pl.BlockSpec((tk, tn), lambda i, j, k: (k, j))
    else:
        a_spec = pl.BlockSpec((tm, tk), lambda i, j, k: (i, k))
        b_spec = pl.BlockSpec((tn, tk), lambda i, j, k: (j, k))
    in_specs = [a_spec, b_spec]
    operands = [a, b]
    if res is not None:
        in_specs.append(pl.BlockSpec((tm, tn), lambda i, j, k: (i, j)))
        operands.append(res)
    if col_shards == 1:
        out_shape = jax.ShapeDtypeStruct((M, N), out_dtype)
        out_spec = pl.BlockSpec((tm, tn), lambda i, j, k: (i, j))
    else:
        per_shard = (N // col_shards) // tn
        out_shape = jax.ShapeDtypeStruct((col_shards, M, N // col_shards), out_dtype)
        out_spec = pl.BlockSpec((None, tm, tn), lambda i, j, k: (j // per_shard, i, j % per_shard))
    scratch = [] if nk == 1 else [pltpu.VMEM((tm, tn), F32)]
    if rider is None:
        return pl.pallas_call(
            body, name=name, out_shape=out_shape, grid=grid, in_specs=in_specs, out_specs=out_spec,
            scratch_shapes=scratch, compiler_params=_params("parallel", "parallel", "arbitrary"),
        )(*operands)
    res_all = pl.pallas_call(
        body, name=name, out_shape=(out_shape, *rider.out_shapes), grid=grid,
        in_specs=in_specs + [ANY] * n_rin, out_specs=(out_spec, *([ANY] * n_rout)),
        scratch_shapes=scratch + rider.sem_shapes,
        compiler_params=pltpu.CompilerParams(dimension_semantics=("arbitrary", "arbitrary", "arbitrary"),
                                             vmem_limit_bytes=VMEM_LIMIT_BYTES, has_side_effects=True),
    )(*operands, *rider.operands)
    return res_all[0], res_all[1:]


def _rmsnorm_fwd(x, g, name, col0=0, width=None):
    L = x.shape[0]
    width = x.shape[1] if width is None else width
    assert col0 % width == 0
    cb = col0 // width
    tm = _tile(L, (128, 64, 32, 16))

    def body(x_ref, g_ref, o_ref):
        xv = x_ref[...]
        r = lax.rsqrt(jnp.mean(xv * xv, axis=-1, keepdims=True) + NORM_EPS)
        o_ref[...] = ((xv * r) * g_ref[...]).astype(BF16)

    return pl.pallas_call(
        body, name=name,
        out_shape=jax.ShapeDtypeStruct((L, width), BF16),
        grid=(L // tm,),
        in_specs=[pl.BlockSpec((tm, width), lambda i: (i, cb)), pl.BlockSpec((1, width), lambda i: (0, 0))],
        out_specs=pl.BlockSpec((tm, width), lambda i: (i, 0)),
        compiler_params=_params("parallel"),
    )(x, g)


def _embed_norm(x, meta, g):
    S, D = x.shape
    L = S + PREFIX
    tm = PREFIX

    def body(x_ref, meta_ref, g_ref, h_ref, u_ref):
        i = pl.program_id(0)

        @pl.when(i == 0)
        def _():
            h_ref[...] = jnp.zeros_like(h_ref)
            h_ref[PAD_LEN:PREFIX, :] = meta_ref[...]

        @pl.when(i > 0)
        def _():
            h_ref[...] = x_ref[...]

        xv = h_ref[...]
        r = lax.rsqrt(jnp.mean(xv * xv, axis=-1, keepdims=True) + NORM_EPS)
        u_ref[...] = ((xv * r) * g_ref[...]).astype(BF16)

    return pl.pallas_call(
        body, name="embed_norm_mix",
        out_shape=(jax.ShapeDtypeStruct((L, D), F32), jax.ShapeDtypeStruct((L, D), BF16)),
        grid=(L // tm,),
        in_specs=[pl.BlockSpec((tm, D), lambda i: (jnp.maximum(i - 1, 0), 0)),
                  pl.BlockSpec((N_META, D), lambda i: (0, 0)), pl.BlockSpec((1, D), lambda i: (0, 0))],
        out_specs=(pl.BlockSpec((tm, D), lambda i: (i, 0)), pl.BlockSpec((tm, D), lambda i: (i, 0))),
        compiler_params=_params("parallel"),
    )(x, meta, g)


def _rmsnorm_bwd(dy, x, g, name, out_dtype, col0=0, res=None, split_prefix=False):
    L, width = dy.shape
    assert col0 % width == 0
    cb = col0 // width
    tm = PREFIX if split_prefix else _tile(L, (128, 64, 32, 16))

    def body(*refs):
        head_ref = None
        if split_prefix:
            refs, head_ref = refs[:-1], refs[-1]
        if res is None:
            dy_ref, x_ref, g_ref, dx_ref, dg_ref = refs
            r_ref = None
        else:
            dy_ref, x_ref, g_ref, r_ref, dx_ref, dg_ref = refs

        @pl.when(pl.program_id(0) == 0)
        def _():
            dg_ref[...] = jnp.zeros_like(dg_ref)

        xv = x_ref[...]
        dyv = dy_ref[...].astype(F32)
        r = lax.rsqrt(jnp.mean(xv * xv, axis=-1, keepdims=True) + NORM_EPS)
        z = dyv * g_ref[...]
        dx = r * z - xv * ((r * r * r) * jnp.mean(z * xv, axis=-1, keepdims=True))
        if r_ref is not None:
            dx = dx + r_ref[...]
        dg_ref[...] += jnp.sum(dyv * (xv * r), axis=0, keepdims=True)
        if head_ref is None:
            dx_ref[...] = dx.astype(out_dtype)
        else:
            @pl.when(pl.program_id(0) == 0)
            def _():
                head_ref[...] = dx.astype(out_dtype)

            @pl.when(pl.program_id(0) > 0)
            def _():
                dx_ref[...] = dx.astype(out_dtype)

    in_specs = [pl.BlockSpec((tm, width), lambda i: (i, 0)),
                pl.BlockSpec((tm, width), lambda i: (i, cb)),
                pl.BlockSpec((1, width), lambda i: (0, 0))]
    operands = [dy, x, g]
    if res is not None:
        in_specs.append(pl.BlockSpec((tm, width), lambda i: (i, 0)))
        operands.append(res)
    dg_shape, dg_spec = jax.ShapeDtypeStruct((1, width), F32), pl.BlockSpec((1, width), lambda i: (0, 0))
    if not split_prefix:
        return pl.pallas_call(
            body, name=name, out_shape=(jax.ShapeDtypeStruct((L, width), out_dtype), dg_shape), grid=(L // tm,),
            in_specs=in_specs, out_specs=(pl.BlockSpec((tm, width), lambda i: (i, 0)), dg_spec),
            compiler_params=_params("arbitrary"),
        )(*operands)
    dx_rest, dg, dx_head = pl.pallas_call(
        body, name=name,
        out_shape=(jax.ShapeDtypeStruct((L - PREFIX, width), out_dtype), dg_shape,
                   jax.ShapeDtypeStruct((PREFIX, width), out_dtype)),
        grid=(L // tm,), in_specs=in_specs,
        out_specs=(pl.BlockSpec((tm, width), lambda i: (jnp.maximum(i - 1, 0), 0)), dg_spec,
                   pl.BlockSpec((PREFIX, width), lambda i: (0, 0))),
        compiler_params=_params("arbitrary"),
    )(*operands)
    return (dx_rest, dx_head), dg


def _final_loss_bwd(h2, tgt, g):
    L, D = h2.shape
    tm = PREFIX
    inv_d = 1.0 / D

    def body(h_ref, t_ref, g_ref, dh_ref, loss_ref, dg_ref):
        i = pl.program_id(0)

        @pl.when(i == 0)
        def _():
            loss_ref[...] = jnp.zeros_like(loss_ref)
            dg_ref[...] = jnp.zeros_like(dg_ref)

        xv = h_ref[...]
        r = lax.rsqrt(jnp.mean(xv * xv, axis=-1, keepdims=True) + NORM_EPS)
        xn = xv * r
        y = xn * g_ref[...]
        real = (i >= PREFIX // tm).astype(F32)
        diff = (y - t_ref[...]) * real
        loss_ref[...] += 0.5 * inv_d * jnp.sum(diff * diff)
        dyv = diff * inv_d
        z = dyv * g_ref[...]
        dh_ref[...] = r * z - xv * ((r * r * r) * jnp.mean(z * xv, axis=-1, keepdims=True))
        dg_ref[...] += jnp.sum(dyv * xn, axis=0, keepdims=True)

    shift = PREFIX // tm
    return pl.pallas_call(
        body, name="final_loss_bwd",
        out_shape=(jax.ShapeDtypeStruct((L, D), F32), jax.ShapeDtypeStruct((1, 128), F32),
                   jax.ShapeDtypeStruct((1, D), F32)),
        grid=(L // tm,),
        in_specs=[pl.BlockSpec((tm, D), lambda i: (i, 0)),
                  pl.BlockSpec((tm, D), lambda i: (jnp.maximum(i - shift, 0), 0)),
                  pl.BlockSpec((1, D), lambda i: (0, 0))],
        out_specs=(pl.BlockSpec((tm, D), lambda i: (i, 0)), pl.BlockSpec((1, 128), lambda i: (0, 0)),
                   pl.BlockSpec((1, D), lambda i: (0, 0))),
        compiler_params=_params("arbitrary"),
    )(h2, tgt, g)


def _rot_half(x):
    lane = lax.broadcasted_iota(jnp.int32, x.shape, 1)
    return jnp.where(lane < ROPE // 2, -pltpu.roll(x, 128 - ROPE // 2, 1), pltpu.roll(x, ROPE // 2, 1))


def _rope_fwd(qp, kv, kr, cos, sin, heads):
    L = qp.shape[0]
    tm = _tile(L, (128,))

    def body(q_ref, kv_ref, kr_ref, c_ref, s_ref, qc_ref, kc_ref, v_ref):
        c, s = c_ref[...], s_ref[...]
        krv = kr_ref[...]
        kr_rot = (krv * c + _rot_half(krv) * s).astype(BF16)
        for h in range(heads):
            lo = h * QPAD
            qc_ref[:, lo:lo + HEAD] = (q_ref[:, lo:lo + HEAD].astype(F32) * SOFTMAX_SCALE).astype(BF16)
            qr = q_ref[:, lo + HEAD:lo + QPAD].astype(F32)
            qc_ref[:, lo + HEAD:lo + QPAD] = ((qr * c + _rot_half(qr) * s) * SOFTMAX_SCALE).astype(BF16)
            kc_ref[:, lo:lo + HEAD] = kv_ref[:, lo:lo + HEAD].astype(BF16)
            kc_ref[:, lo + HEAD:lo + QPAD] = kr_rot
            v_ref[:, h * HEAD:(h + 1) * HEAD] = kv_ref[:, lo + HEAD:lo + QPAD].astype(BF16)

    W = heads * QPAD
    row = lambda w: pl.BlockSpec((tm, w), lambda i: (i, 0))
    return pl.pallas_call(
        body, name="rope_fwd",
        out_shape=(jax.ShapeDtypeStruct((L, W), BF16), jax.ShapeDtypeStruct((L, W), BF16),
                   jax.ShapeDtypeStruct((L, heads * HEAD), BF16)),
        grid=(L // tm,),
        in_specs=[row(W), row(W), row(128), row(128), row(128)],
        out_specs=(row(W), row(W), row(heads * HEAD)),
        compiler_params=_params("parallel"),
    )(qp, kv, kr, cos, sin)


def _rope_bwd(dqc, dkc, dv, cos, sin, heads):
    L = dqc.shape[0]
    tm = _tile(L, (128,))

    def body(dq_ref, dk_ref, dv_ref, c_ref, s_ref, dqp_ref, dkv_ref, dkr_ref):
        c, s = c_ref[...], s_ref[...]
        acc = jnp.zeros((tm, 128), F32)
        for h in range(heads):
            lo = h * QPAD
            dqp_ref[:, lo:lo + HEAD] = (dq_ref[:, lo:lo + HEAD] * SOFTMAX_SCALE).astype(BF16)
            d = dq_ref[:, lo + HEAD:lo + QPAD]
            dqp_ref[:, lo + HEAD:lo + QPAD] = ((d * c - _rot_half(d) * s) * SOFTMAX_SCALE).astype(BF16)
            dkv_ref[:, lo:lo + HEAD] = dk_ref[:, lo:lo + HEAD].astype(BF16)
            dkv_ref[:, lo + HEAD:lo + QPAD] = dv_ref[:, h * HEAD:(h + 1) * HEAD].astype(BF16)
            acc = acc + dk_ref[:, lo + HEAD:lo + QPAD]
        dkr_ref[...] = (acc * c - _rot_half(acc) * s).astype(BF16)

    W = heads * QPAD
    row = lambda w: pl.BlockSpec((tm, w), lambda i: (i, 0))
    return pl.pallas_call(
        body, name="rope_bwd",
        out_shape=(jax.ShapeDtypeStruct((L, W), BF16), jax.ShapeDtypeStruct((L, W), BF16),
                   jax.ShapeDtypeStruct((L, 128), BF16)),
        grid=(L // tm,),
        in_specs=[row(W), row(W), row(heads * HEAD), row(128), row(128)],
        out_specs=(row(W), row(W), row(128)),
        compiler_params=_params("parallel"),
    )(dqc, dkc, dv, cos, sin)


def _attn_keep(qi, ki, ta):
    t = qi * ta + lax.broadcasted_iota(jnp.int32, (ta, ta), 0)
    s = ki * ta + lax.broadcasted_iota(jnp.int32, (ta, ta), 1)
    return (s <= t) & ((s >= PAD_LEN) | (s == t))


def _split_refs(refs, sizes):
    out, at = [], 0
    for n in sizes:
        out.append(refs[at:at + n])
        at += n
    return out


def _attn_fwd(qc, kc, v, heads, rider):
    L = qc.shape[0]
    ta = _tile(L, (640, 128))
    nb = L // ta
    pairs = [(i, j) for i in range(nb) for j in range(i + 1)]
    q_of = jnp.asarray([p[0] for p in pairs], jnp.int32)
    k_of = jnp.asarray([p[1] for p in pairs], jnp.int32)
    n_rin, n_rout = len(rider.operands), len(rider.out_shapes)
    per = _tile(heads, (ATTN_FWD_HEADS, 1))
    ng = heads // per

    def body(*refs):
        (q_of_ref, k_of_ref, q_ref, k_ref, v_ref), r_ins, (o_ref, lse_ref), r_outs, (m_sc, l_sc, acc_sc), r_sems = \
            _split_refs(refs, (5, n_rin, 2, n_rout, 3, len(rider.sem_shapes)))
        h, t = pl.program_id(0), pl.program_id(1)
        qi, ki = q_of_ref[t], k_of_ref[t]
        pl.when((h == 0) & (t == 0))(lambda: rider.start(r_ins, r_outs, r_sems))

        @pl.when(ki == 0)
        def _():
            m_sc[...] = jnp.full_like(m_sc, NEG)
            l_sc[...] = jnp.zeros_like(l_sc)
            acc_sc[...] = jnp.zeros_like(acc_sc)

        def step(masked):
            wide = lambda hh: slice(hh * QPAD, (hh + 1) * QPAD)
            lanes = lambda hh: slice(hh * HEAD, (hh + 1) * HEAD)
            ss = [_dot(q_ref[:, wide(hh)], k_ref[:, wide(hh)], NT) for hh in range(per)]
            if masked:
                keep = _attn_keep(qi, ki, ta)
                ss = [jnp.where(keep, s, NEG) for s in ss]
            for hh in range(per):
                m_old = m_sc[hh]
                m_new = jnp.maximum(m_old, jnp.max(ss[hh], axis=-1, keepdims=True))
                p = jnp.exp(ss[hh] - jnp.tile(m_new, (1, ta // HEAD)))
                alpha = jnp.exp(m_old - m_new)
                l_sc[hh] = alpha * l_sc[hh] + jnp.sum(p, axis=-1, keepdims=True)
                acc_sc[hh] = alpha * acc_sc[hh] + _dot(p, v_ref[:, lanes(hh)], NN)
                m_sc[hh] = m_new

        pl.when((ki == qi) | ((ki == 0) & (qi > 0)))(lambda: step(True))
        pl.when((ki > 0) & (ki < qi))(lambda: step(False))

        @pl.when(ki == qi)
        def _():
            for hh in range(per):
                l = l_sc[hh]
                o_ref[:, hh * HEAD:(hh + 1) * HEAD] = (acc_sc[hh] / l).astype(BF16)
                lse_ref[:, hh * HEAD:(hh + 1) * HEAD] = m_sc[hh] + jnp.log(l)

        pl.when((h == ng - 1) & (t == len(pairs) - 1))(lambda: rider.finish(r_ins, r_outs, r_sems))

    qrow = lambda w: pl.BlockSpec((ta, per * w), lambda h, t, q_of, k_of: (q_of[t], h))
    krow = lambda w: pl.BlockSpec((ta, per * w), lambda h, t, q_of, k_of: (k_of[t], h))
    stat = pltpu.VMEM((per, ta, HEAD), F32)
    res = pl.pallas_call(
        body, name="attn_fwd",
        out_shape=(jax.ShapeDtypeStruct((L, heads * HEAD), BF16), jax.ShapeDtypeStruct((L, heads * HEAD), F32),
                   *rider.out_shapes),
        grid_spec=pltpu.PrefetchScalarGridSpec(
            num_scalar_prefetch=2, grid=(ng, len(pairs)),
            in_specs=[qrow(QPAD), krow(QPAD), krow(HEAD)] + [ANY] * n_rin,
            out_specs=(qrow(HEAD), qrow(HEAD), *([ANY] * n_rout)),
            scratch_shapes=[stat, stat, stat] + rider.sem_shapes),
        compiler_params=pltpu.CompilerParams(dimension_semantics=("arbitrary", "arbitrary"),
                                             vmem_limit_bytes=VMEM_LIMIT_BYTES, has_side_effects=True),
    )(q_of, k_of, qc, kc, v, *rider.operands)
    return res[0], res[1], res[2:]


def _attn_delta(o, do, heads):
    L = o.shape[0]
    tm = _tile(L, (128,))

    def body(o_ref, do_ref, d_ref):
        for h in range(heads):
            cols = slice(h * HEAD, (h + 1) * HEAD)
            d = jnp.sum(do_ref[:, cols].astype(F32) * o_ref[:, cols].astype(F32), axis=-1, keepdims=True)
            d_ref[:, cols] = jnp.broadcast_to(d, (tm, HEAD))

    row = pl.BlockSpec((tm, heads * HEAD), lambda i: (i, 0))
    return pl.pallas_call(
        body, name="attn_delta", out_shape=jax.ShapeDtypeStruct((L, heads * HEAD), F32), grid=(L // tm,),
        in_specs=[row, row], out_specs=row, compiler_params=_params("parallel"),
    )(o, do)


def _attn_bwd(qc, kc, v, do, lse, delta, heads, rider):
    L = qc.shape[0]
    ta = _tile(L, (640, 128))
    nb = L // ta
    pairs = [(j, i) for j in range(nb) for i in range(j, nb)]
    k_of = jnp.asarray([p[0] for p in pairs], jnp.int32)
    q_of = jnp.asarray([p[1] for p in pairs], jnp.int32)
    n_rin, n_rout = len(rider.operands), len(rider.out_shapes)
    per = _tile(heads, (ATTN_BWD_HEADS, 1))
    ng = heads // per

    def body(*refs):
        ((k_of_ref, q_of_ref, q_ref, k_ref, v_ref, do_ref, lse_ref, delta_ref), r_ins, (dq_hbm, dk_ref, dv_ref),
         r_outs, (dq_sc, dq_sem), r_sems) = _split_refs(refs, (8, n_rin, 3, n_rout, 2, len(rider.sem_shapes)))
        h, t = pl.program_id(0), pl.program_id(1)
        kj, qi = k_of_ref[t], q_of_ref[t]
        pl.when((h == 0) & (t == 0))(lambda: rider.start(r_ins, r_outs, r_sems))

        @pl.when(t == 0)
        def _():
            dq_sc[...] = jnp.zeros_like(dq_sc)

        @pl.when(qi == kj)
        def _():
            dk_ref[...] = jnp.zeros_like(dk_ref)
            dv_ref[...] = jnp.zeros_like(dv_ref)

        def step(masked):
            wide = lambda hh: slice(hh * QPAD, (hh + 1) * QPAD)
            lanes = lambda hh: slice(hh * HEAD, (hh + 1) * HEAD)
            rep = (1, ta // HEAD)
            rows = pl.ds(pl.multiple_of(qi * ta, ta), ta)
            ss = [_dot(q_ref[:, wide(hh)], k_ref[:, wide(hh)], NT) for hh in range(per)]
            dps = [_dot(do_ref[:, lanes(hh)], v_ref[:, lanes(hh)], NT) for hh in range(per)]
            ps = [jnp.exp(ss[hh] - jnp.tile(lse_ref[:, lanes(hh)], rep)) for hh in range(per)]
            if masked:
                keep = _attn_keep(qi, kj, ta)
                ps = [jnp.where(keep, p, 0.0) for p in ps]
            dss = [ps[hh] * (dps[hh] - jnp.tile(delta_ref[:, lanes(hh)], rep)) for hh in range(per)]
            for hh in range(per):
                dv_ref[:, lanes(hh)] += _dot(ps[hh], do_ref[:, lanes(hh)], TN)
                dk_ref[:, wide(hh)] += _dot(dss[hh], q_ref[:, wide(hh)], TN)
                dq_sc[rows, wide(hh)] += _dot(dss[hh], k_ref[:, wide(hh)], NN)

        pl.when((qi == kj) | ((kj == 0) & (qi > 0)))(lambda: step(True))
        pl.when((kj > 0) & (qi > kj))(lambda: step(False))

        @pl.when(t == len(pairs) - 1)
        def _():
            cols = pl.ds(pl.multiple_of(h * (per * QPAD), per * QPAD), per * QPAD)
            out = pltpu.make_async_copy(dq_sc, dq_hbm.at[:, cols], dq_sem)
            out.start()
            out.wait()

        pl.when((h == ng - 1) & (t == len(pairs) - 1))(lambda: rider.finish(r_ins, r_outs, r_sems))

    qrow = lambda w: pl.BlockSpec((ta, per * w), lambda h, t, k_of, q_of: (q_of[t], h))
    krow = lambda w: pl.BlockSpec((ta, per * w), lambda h, t, k_of, q_of: (k_of[t], h))
    res = pl.pallas_call(
        body, name="attn_bwd",
        out_shape=(jax.ShapeDtypeStruct((L, heads * QPAD), F32), jax.ShapeDtypeStruct((L, heads * QPAD), F32),
                   jax.ShapeDtypeStruct((L, heads * HEAD), F32), *rider.out_shapes),
        grid_spec=pltpu.PrefetchScalarGridSpec(
            num_scalar_prefetch=2, grid=(ng, len(pairs)),
            in_specs=[qrow(QPAD), krow(QPAD), krow(HEAD), qrow(HEAD), qrow(HEAD), qrow(HEAD)] + [ANY] * n_rin,
            out_specs=(ANY, krow(QPAD), krow(HEAD), *([ANY] * n_rout)),
            scratch_shapes=[pltpu.VMEM((L, per * QPAD), F32), pltpu.SemaphoreType.DMA] + rider.sem_shapes),
        compiler_params=pltpu.CompilerParams(dimension_semantics=("arbitrary", "arbitrary"),
                                             vmem_limit_bytes=VMEM_LIMIT_BYTES, has_side_effects=True),
    )(k_of, q_of, qc, kc, v, do, lse, delta, *rider.operands)
    return res[0], res[1], res[2], res[3:]


def _hgrn_constants():
    t = jnp.arange(CHUNK)
    tril = (t[None, :] <= t[:, None]).astype(BF16)
    sel, blk = [], []
    for lv in range(LEVELS):
        hs = 1 << lv
        mid = (t // (2 * hs)) * (2 * hs) + hs - 1
        sel.append((t[None, :] == mid[:, None]).astype(BF16))
        blk.append(((t[:, None] // (2 * hs)) == (t[None, :] // (2 * hs))).astype(F32))
    return tril, jnp.concatenate(sel, axis=0), jnp.concatenate(blk, axis=0)


def _hgrn_decay_grad_constants():
    r = jnp.arange(CHUNK)[:, None]
    c = jnp.arange(CHUNK)[None, :]
    mats = []
    for lv in range(LEVELS):
        same = (r >> (lv + 1)) == (c >> (lv + 1))
        second = ((r >> lv) & 1) == 1
        mats.append(same & jnp.where(second, c >= r, c < r))
    mats += [c >= r, c < r]
    return jnp.concatenate(mats, axis=1).astype(BF16)


def _hgrn_gates(hq_ref, hf_ref, hi_ref, lb_ref, cols, row0):
    rows = row0 + lax.broadcasted_iota(jnp.int32, (CHUNK, 1), 0)
    valid = rows >= PAD_LEN
    lb = 1.0 / (1.0 + jnp.exp(lb_ref[1:2, cols] - lb_ref[0:1, cols]))
    hq = hq_ref[:, cols]
    sq = _sigmoid(hq)
    sg = _sigmoid(hf_ref[:, cols])
    f = lb + (1.0 - lb) * sg
    g = jnp.where(valid, jnp.log(f), 0.0)
    k = jnp.where(valid, 1.0 - f, 0.0)
    return dict(q=hq * sq, sq=sq, hq=hq, k=k, v=hi_ref[:, cols], g=g, f=f, sg=sg, lb=lb, valid=valid)


def _hgrn_prefix(tril_ref, sel_ref, gs):
    n = len(gs)
    b_all = _dot_exact_rhs(tril_ref[...], jnp.concatenate(gs, axis=1) if n > 1 else gs[0], NN)
    bm_all = _dot_exact_rhs(sel_ref[...], b_all, NN)
    cut = lambda a, i: a[:, i * HEAD:(i + 1) * HEAD]
    return [cut(b_all, i) for i in range(n)], [cut(bm_all, i) for i in range(n)]


def _hgrn_levels(q, k, b, bm_all, blk_ref):
    t = lax.broadcasted_iota(jnp.int32, (CHUNK, 1), 0)
    out = []
    for lv in range(LEVELS):
        bm = bm_all[lv * CHUNK:(lv + 1) * CHUNK, :]
        second = ((t >> lv) & 1) == 1
        eq = jnp.where(second, jnp.exp(jnp.minimum(b - bm, 0.0)), 0.0)
        ek = jnp.where(second, 0.0, jnp.exp(jnp.minimum(bm - b, 0.0)))
        same = blk_ref[lv * CHUNK:(lv + 1) * CHUNK, :]
        out.append((eq, ek, (q * eq).astype(BF16), (k * ek).astype(BF16), same))
    return out


def _hgrn_intra(qs, ks, levels):
    tt = lax.broadcasted_iota(jnp.int32, (CHUNK, CHUNK), 0)
    ss = lax.broadcasted_iota(jnp.int32, (CHUNK, CHUNK), 1)
    ps = [jnp.where(tt == ss, jnp.sum(q * k, axis=-1, keepdims=True), 0.0) for q, k in zip(qs, ks)]
    for lv in range(LEVELS):
        for i, lvl in enumerate(levels):
            _, _, ql, kl, same = lvl[lv]
            ps[i] = ps[i] + _dot(ql, kl, NT) * same
    return ps


def _hgrn_fwd(hp, lb_raw, g_norm, nh):
    L = hp.shape[0]
    D = nh * HEAD
    nc = L // CHUNK
    per = _tile(nh, (HGRN_FWD_HEADS, 2, 1))
    ng = nh // per
    tril, sel, blk = _hgrn_constants()

    def body(hq_ref, hf_ref, hi_ref, hg_ref, lb_ref, gn_ref, tril_ref, sel_ref, blk_ref,
             oh_ref, orec_ref, shist_ref, s_sc, b_sc):
        c = pl.program_id(1)

        @pl.when(c == 0)
        def _():
            s_sc[...] = jnp.zeros_like(s_sc)

        heads_here = range(per)
        lanes = [slice(hh * HEAD, (hh + 1) * HEAD) for hh in heads_here]
        ws = [_hgrn_gates(hq_ref, hf_ref, hi_ref, lb_ref, lanes[hh], c * CHUNK) for hh in heads_here]
        qs, ks, vs = [w["q"] for w in ws], [w["k"] for w in ws], [w["v"] for w in ws]
        bs, bms = _hgrn_prefix(tril_ref, sel_ref, [w["g"] for w in ws])
        for hh in heads_here:
            b_sc[hh] = bs[hh]
        b_lasts = [b_sc[hh, CHUNK - 1:CHUNK, :] for hh in heads_here]
        ps = _hgrn_intra(qs, ks, [_hgrn_levels(qs[hh], ks[hh], bs[hh], bms[hh], blk_ref) for hh in heads_here])
        s_ins = [s_sc[hh] for hh in heads_here]
        os_ = [_dot(ps[hh], vs[hh], NN) + _dot(qs[hh] * jnp.exp(bs[hh]), s_ins[hh], NT) for hh in heads_here]
        for hh in heads_here:
            shist_ref[0, hh] = s_ins[hh]
            s_sc[hh] = (jnp.exp(b_lasts[hh]) * s_ins[hh]
                        + _dot(vs[hh], ks[hh] * jnp.exp(b_lasts[hh] - bs[hh]), TN))
        for hh in heads_here:
            o = os_[hh]
            orec_ref[:, lanes[hh]] = o
            rn = lax.rsqrt(jnp.mean(o * o, axis=-1, keepdims=True) + NORM_EPS)
            hg = hg_ref[:, lanes[hh]]
            oh_ref[:, lanes[hh]] = (((o * rn) * gn_ref[...]) * (hg * _sigmoid(hg))).astype(BF16)

    col = lambda grp: pl.BlockSpec((CHUNK, per * HEAD), lambda h, c: (c, grp * ng + h))
    const = lambda shape: pl.BlockSpec(shape, lambda h, c: (0, 0))
    return pl.pallas_call(
        body, name="hgrn_fwd",
        out_shape=(jax.ShapeDtypeStruct((L, D), BF16), jax.ShapeDtypeStruct((L, D), F32),
                   jax.ShapeDtypeStruct((nc, nh, HEAD, HEAD), F32)),
        grid=(ng, nc),
        in_specs=[col(0), col(1), col(2), col(3),
                  pl.BlockSpec((2, per * HEAD), lambda h, c: (0, h)), const((1, HEAD)),
                  const((CHUNK, CHUNK)), const((LEVELS * CHUNK, CHUNK)), const((LEVELS * CHUNK, CHUNK))],
        out_specs=(pl.BlockSpec((CHUNK, per * HEAD), lambda h, c: (c, h)),
                   pl.BlockSpec((CHUNK, per * HEAD), lambda h, c: (c, h)),
                   pl.BlockSpec((1, per, HEAD, HEAD), lambda h, c: (c, h, 0, 0))),
        scratch_shapes=[pltpu.VMEM((per, HEAD, HEAD), F32), pltpu.VMEM((per, CHUNK, HEAD), F32)],
        compiler_params=_params("parallel", "arbitrary"),
    )(hp, hp, hp, hp, lb_raw, g_norm, tril, sel, blk)


def _hgrn_bwd(hp, lb_raw, g_norm, do_h, o_rec, s_hist, nh):
    L = hp.shape[0]
    D = nh * HEAD
    nc = L // CHUNK
    per = _tile(nh, (HGRN_BWD_HEADS, 1))
    ng = nh // per
    tril, sel, blk = _hgrn_constants()
    tdec = _hgrn_decay_grad_constants()

    def body(hq_ref, hf_ref, hi_ref, hg_ref, lb_ref, gn_ref, tril_ref, sel_ref, blk_ref, tdec_ref,
             do_ref, orec_ref, shist_ref, dhq_ref, dhf_ref, dhi_ref, dhg_ref, dgn_ref, dlb_ref, ds_sc, b_sc):
        ci = pl.program_id(1)
        c = nc - 1 - ci

        @pl.when(ci == 0)
        def _():
            ds_sc[...] = jnp.zeros_like(ds_sc)
            dgn_ref[...] = jnp.zeros_like(dgn_ref)
            dlb_ref[...] = jnp.zeros_like(dlb_ref)

        heads_here = range(per)
        lanes = [slice(hh * HEAD, (hh + 1) * HEAD) for hh in heads_here]
        ws = [_hgrn_gates(hq_ref, hf_ref, hi_ref, lb_ref, lanes[hh], c * CHUNK) for hh in heads_here]
        qs, ks, vs = [w["q"] for w in ws], [w["k"] for w in ws], [w["v"] for w in ws]
        bs, bms = _hgrn_prefix(tril_ref, sel_ref, [w["g"] for w in ws])
        for hh in heads_here:
            b_sc[hh] = bs[hh]
        b_lasts = [b_sc[hh, CHUNK - 1:CHUNK, :] for hh in heads_here]
        levels = [_hgrn_levels(qs[hh], ks[hh], bs[hh], bms[hh], blk_ref) for hh in heads_here]
        ps = _hgrn_intra(qs, ks, levels)
        s_ins = [shist_ref[0, hh] for hh in heads_here]
        ds_outs = [ds_sc[hh] for hh in heads_here]
        ebs = [jnp.exp(b) for b in bs]
        etails = [jnp.exp(b_lasts[hh] - bs[hh]) for hh in heads_here]
        decays = [jnp.exp(bl) for bl in b_lasts]

        dos = []
        for hh in heads_here:
            o = orec_ref[:, lanes[hh]]
            hg = hg_ref[:, lanes[hh]]
            sgg = _sigmoid(hg)
            rn = lax.rsqrt(jnp.mean(o * o, axis=-1, keepdims=True) + NORM_EPS)
            on = o * rn
            doh = do_ref[:, lanes[hh]]
            dy = doh * (hg * sgg)
            dhg_ref[:, lanes[hh]] = (doh * (on * gn_ref[...]) * (sgg * (1.0 + hg * (1.0 - sgg)))).astype(BF16)
            dgn_ref[hh] += jnp.broadcast_to(jnp.sum(dy * on, axis=0, keepdims=True), (8, HEAD))
            z = dy * gn_ref[...]
            dos.append(rn * z - o * ((rn * rn * rn) * jnp.mean(z * o, axis=-1, keepdims=True)))

        tt = lax.broadcasted_iota(jnp.int32, (CHUNK, CHUNK), 0)
        ss = lax.broadcasted_iota(jnp.int32, (CHUNK, CHUNK), 1)
        dps = [jnp.where(ss <= tt, _dot(dos[hh], vs[hh], NT), 0.0) for hh in heads_here]
        dvs = [_dot(ps[hh], dos[hh], TN) + _dot(ks[hh] * etails[hh], ds_outs[hh], NT) for hh in heads_here]
        dq_states = [ebs[hh] * _dot(dos[hh], s_ins[hh], NN) for hh in heads_here]
        dk_states = [etails[hh] * _dot(vs[hh], ds_outs[hh], NN) for hh in heads_here]
        dpds = [jnp.sum(jnp.where(tt == ss, dp, 0.0), axis=-1, keepdims=True) for dp in dps]
        dqs = [dpds[hh] * ks[hh] + dq_states[hh] for hh in heads_here]
        dks = [dpds[hh] * qs[hh] + dk_states[hh] for hh in heads_here]
        pair_terms = [[] for _ in heads_here]
        for lv in range(LEVELS):
            for hh in heads_here:
                eq, ek, ql, kl, same = levels[hh][lv]
                dpl = dps[hh] * same
                dq_l = eq * _dot(dpl, kl, NN)
                dk_l = ek * _dot(dpl, ql, TN)
                dqs[hh] = dqs[hh] + dq_l
                dks[hh] = dks[hh] + dk_l
                pair_terms[hh].append(qs[hh] * dq_l + ks[hh] * dk_l)
        for hh in heads_here:
            pair_terms[hh] += [qs[hh] * dq_states[hh], ks[hh] * dk_states[hh]]
            ds_sc[hh] = decays[hh] * ds_outs[hh] + _dot(dos[hh], qs[hh] * ebs[hh], TN)
        stacked = [jnp.concatenate(terms, axis=0) for terms in pair_terms]
        dg_all = _dot_exact_rhs(tdec_ref[...], jnp.concatenate(stacked, axis=1) if per > 1 else stacked[0], NN)

        for hh in heads_here:
            w = ws[hh]
            f, sg, lb, sq, hq = w["f"], w["sg"], w["lb"], w["sq"], w["hq"]
            through = jnp.sum((decays[hh] * s_ins[hh]) * ds_outs[hh], axis=0, keepdims=True)
            dg = dg_all[:, lanes[hh]] + through
            df = jnp.where(w["valid"], dg / f - dks[hh], 0.0)
            dhf_ref[:, lanes[hh]] = (df * (1.0 - lb) * sg * (1.0 - sg)).astype(BF16)
            dlb_ref[hh] += jnp.broadcast_to(jnp.sum(df * (1.0 - sg), axis=0, keepdims=True), (8, HEAD))
            dhq_ref[:, lanes[hh]] = (dqs[hh] * (sq * (1.0 + hq * (1.0 - sq)))).astype(BF16)
            dhi_ref[:, lanes[hh]] = dvs[hh].astype(BF16)

    col = lambda grp: pl.BlockSpec((CHUNK, per * HEAD), lambda h, c: (nc - 1 - c, grp * ng + h))
    const = lambda shape: pl.BlockSpec(shape, lambda h, c: (0, 0))
    tile = pl.BlockSpec((CHUNK, per * HEAD), lambda h, c: (nc - 1 - c, h))
    part = pl.BlockSpec((per, 8, HEAD), lambda h, c: (h, 0, 0))
    return pl.pallas_call(
        body, name="hgrn_bwd",
        out_shape=tuple([jax.ShapeDtypeStruct((L, D), BF16)] * 4 + [jax.ShapeDtypeStruct((nh, 8, HEAD), F32)] * 2),
        grid=(ng, nc),
        in_specs=[col(0), col(1), col(2), col(3),
                  pl.BlockSpec((2, per * HEAD), lambda h, c: (0, h)), const((1, HEAD)),
                  const((CHUNK, CHUNK)), const((LEVELS * CHUNK, CHUNK)), const((LEVELS * CHUNK, CHUNK)),
                  const((CHUNK, (LEVELS + 2) * CHUNK)),
                  tile, tile, pl.BlockSpec((1, per, HEAD, HEAD), lambda h, c: (nc - 1 - c, h, 0, 0))],
        out_specs=(tile, tile, tile, tile, part, part),
        scratch_shapes=[pltpu.VMEM((per, HEAD, HEAD), F32), pltpu.VMEM((per, CHUNK, HEAD), F32)],
        compiler_params=_params("parallel", "arbitrary"),
    )(hp, hp, hp, hp, lb_raw, g_norm, tril, sel, blk, tdec, do_h, o_rec, s_hist)


def _merge_fwd(a, bm, gates):
    L, D = a.shape
    tm = _tile(L, (128,))

    def body(a_ref, b_ref, g_ref, o_ref):
        o_ref[...] = (_sigmoid(g_ref[:, :D]) * a_ref[...] + _sigmoid(g_ref[:, D:]) * b_ref[...]).astype(BF16)

    row = lambda w: pl.BlockSpec((tm, w), lambda i: (i, 0))
    return pl.pallas_call(
        body, name="merge_fwd", out_shape=jax.ShapeDtypeStruct((L, D), BF16), grid=(L // tm,),
        in_specs=[row(D), row(D), row(2 * D)], out_specs=row(D), compiler_params=_params("parallel"),
    )(a, bm, gates)


def _merge_bwd(dm, a, bm, gates):
    L, D = a.shape
    tm = _tile(L, (128,))

    def body(dm_ref, a_ref, b_ref, g_ref, da_ref, db_ref, dg_ref):
        d = dm_ref[...]
        sa, sb = _sigmoid(g_ref[:, :D]), _sigmoid(g_ref[:, D:])
        da_ref[...] = (d * sa).astype(BF16)
        db_ref[...] = (d * sb).astype(BF16)
        dg_ref[:, :D] = (d * a_ref[...] * sa * (1.0 - sa)).astype(BF16)
        dg_ref[:, D:] = (d * b_ref[...] * sb * (1.0 - sb)).astype(BF16)

    row = lambda w: pl.BlockSpec((tm, w), lambda i: (i, 0))
    return pl.pallas_call(
        body, name="merge_bwd",
        out_shape=(jax.ShapeDtypeStruct((L, D), BF16), jax.ShapeDtypeStruct((L, D), BF16),
                   jax.ShapeDtypeStruct((L, 2 * D), BF16)),
        grid=(L // tm,),
        in_specs=[row(D), row(D), row(D), row(2 * D)], out_specs=(row(D), row(D), row(2 * D)),
        compiler_params=_params("parallel"),
    )(dm, a, bm, gates)


def _conv_taps(g_ref, halo_ref, i, tm):
    rows = i * tm + lax.broadcasted_iota(jnp.int32, (tm, 1), 0)
    g0 = jnp.where(rows >= PAD_LEN, g_ref[...].astype(F32), 0.0)
    sub = lax.broadcasted_iota(jnp.int32, (HALO, 1), 0)
    hrow = i * tm - HALO + sub
    halo = jnp.where(hrow >= PAD_LEN, halo_ref[...].astype(F32), 0.0)
    r = lax.broadcasted_iota(jnp.int32, (tm, 1), 0)
    h7 = jnp.sum(jnp.where(sub == HALO - 1, halo, 0.0), axis=0, keepdims=True)
    h6 = jnp.sum(jnp.where(sub == HALO - 2, halo, 0.0), axis=0, keepdims=True)
    g1 = jnp.where(r == 0, h7, pltpu.roll(g0, 1, 0))
    g2 = jnp.where(r == 0, h6, jnp.where(r == 1, h7, pltpu.roll(g0, 2, 0)))
    return g0, g1, g2


def _conv_fwd(gu, cw, cb):
    L, F2 = gu.shape
    F = F2 // 2
    tm = _tile(L, ROW_TILES)
    tn = _tile(F, (512, 256, 128))
    nj = F // tn

    def body(g_ref, halo_ref, u_ref, cw_ref, cb_ref, o_ref):
        g0, g1, g2 = _conv_taps(g_ref, halo_ref, pl.program_id(0), tm)
        conv = cw_ref[0:1, :] * g2 + cw_ref[1:2, :] * g1 + cw_ref[2:3, :] * g0 + cb_ref[...]
        o_ref[...] = (conv * _sigmoid(conv) * u_ref[...].astype(F32)).astype(BF16)

    return pl.pallas_call(
        body, name="conv_fwd", out_shape=jax.ShapeDtypeStruct((L, F), BF16), grid=(L // tm, nj),
        in_specs=[pl.BlockSpec((tm, tn), lambda i, j: (i, j)),
                  pl.BlockSpec((HALO, tn), lambda i, j: (jnp.maximum(i * (tm // HALO) - 1, 0), j)),
                  pl.BlockSpec((tm, tn), lambda i, j: (i, j + nj)),
                  pl.BlockSpec((3, tn), lambda i, j: (0, j)),
                  pl.BlockSpec((1, tn), lambda i, j: (0, j))],
        out_specs=pl.BlockSpec((tm, tn), lambda i, j: (i, j)),
        compiler_params=_params("parallel", "parallel"),
    )(gu, gu, gu, cw, cb)


def _conv_bwd_a(da, gu, cw, cb):
    L, F2 = gu.shape
    F = F2 // 2
    tm = _tile(L, ROW_TILES)
    tn = _tile(F, (512, 256, 128))
    nj = F // tn

    def body(da_ref, g_ref, halo_ref, u_ref, cw_ref, cb_ref, dc_ref, du_ref, dcb_ref, dcw_ref):
        i = pl.program_id(1)

        @pl.when(i == 0)
        def _():
            dcb_ref[...] = jnp.zeros_like(dcb_ref)
            dcw_ref[...] = jnp.zeros_like(dcw_ref)

        g0, g1, g2 = _conv_taps(g_ref, halo_ref, i, tm)
        conv = cw_ref[0:1, :] * g2 + cw_ref[1:2, :] * g1 + cw_ref[2:3, :] * g0 + cb_ref[...]
        sc = _sigmoid(conv)
        dav = da_ref[...]
        du_ref[...] = (dav * (conv * sc)).astype(BF16)
        dconv = dav * u_ref[...].astype(F32) * (sc * (1.0 + conv * (1.0 - sc)))
        dc_ref[...] = dconv
        dcb_ref[...] += jnp.sum(dconv, axis=0, keepdims=True)
        dcw_ref[0:1, :] += jnp.sum(dconv * g2, axis=0, keepdims=True)
        dcw_ref[1:2, :] += jnp.sum(dconv * g1, axis=0, keepdims=True)
        dcw_ref[2:3, :] += jnp.sum(dconv * g0, axis=0, keepdims=True)

    return pl.pallas_call(
        body, name="conv_bwd_a",
        out_shape=(jax.ShapeDtypeStruct((L, F), F32), jax.ShapeDtypeStruct((L, 2 * F), BF16),
                   jax.ShapeDtypeStruct((1, F), F32), jax.ShapeDtypeStruct((8, F), F32)),
        grid=(nj, L // tm),
        in_specs=[pl.BlockSpec((tm, tn), lambda j, i: (i, j)),
                  pl.BlockSpec((tm, tn), lambda j, i: (i, j)),
                  pl.BlockSpec((HALO, tn), lambda j, i: (jnp.maximum(i * (tm // HALO) - 1, 0), j)),
                  pl.BlockSpec((tm, tn), lambda j, i: (i, j + nj)),
                  pl.BlockSpec((3, tn), lambda j, i: (0, j)),
                  pl.BlockSpec((1, tn), lambda j, i: (0, j))],
        out_specs=(pl.BlockSpec((tm, tn), lambda j, i: (i, j)), pl.BlockSpec((tm, tn), lambda j, i: (i, j + nj)),
                   pl.BlockSpec((1, tn), lambda j, i: (0, j)), pl.BlockSpec((8, tn), lambda j, i: (0, j))),
        compiler_params=_params("parallel", "arbitrary"),
    )(da, gu, gu, gu, cw, cb)


def _conv_bwd_b(dconv, cw, dgu):
    L, F = dconv.shape
    tm = _tile(L, ROW_TILES)
    tn = _tile(F, (512, 256, 128))
    nblk8 = L // 8
    ni = L // tm

    def body(dc_ref, nxt_ref, cw_ref, dgu_ref, o_ref):
        i = pl.program_id(0)
        dc = dc_ref[...]
        nxt = jnp.where(i < ni - 1, nxt_ref[...], 0.0)
        sub = lax.broadcasted_iota(jnp.int32, (8, 1), 0)
        n0 = jnp.sum(jnp.where(sub == 0, nxt, 0.0), axis=0, keepdims=True)
        n1 = jnp.sum(jnp.where(sub == 1, nxt, 0.0), axis=0, keepdims=True)
        r = lax.broadcasted_iota(jnp.int32, (tm, 1), 0)
        d1 = jnp.where(r == tm - 1, n0, pltpu.roll(dc, tm - 1, 0))
        d2 = jnp.where(r == tm - 2, n0, jnp.where(r == tm - 1, n1, pltpu.roll(dc, tm - 2, 0)))
        dg = cw_ref[2:3, :] * dc + cw_ref[1:2, :] * d1 + cw_ref[0:1, :] * d2
        rows = i * tm + r
        o_ref[...] = jnp.where(rows >= PAD_LEN, dg, 0.0).astype(BF16)

    return pl.pallas_call(
        body, name="conv_bwd_b", out_shape=jax.ShapeDtypeStruct(dgu.shape, BF16), grid=(ni, F // tn),
        in_specs=[pl.BlockSpec((tm, tn), lambda i, j: (i, j)),
                  pl.BlockSpec((8, tn), lambda i, j: (jnp.minimum((i + 1) * (tm // 8), nblk8 - 1), j)),
                  pl.BlockSpec((3, tn), lambda i, j: (0, j)), ANY],
        out_specs=pl.BlockSpec((tm, tn), lambda i, j: (i, j)),
        input_output_aliases={3: 0},
        compiler_params=_params("parallel", "parallel"),
    )(dconv, dconv, cw, dgu)


ANY = pl.BlockSpec(memory_space=pl.ANY)


def _coords():
    return lax.axis_index("x"), lax.axis_index("y"), lax.axis_index("c")


def _flip(v, bit):
    return 1 - v if bit else v


CHIPS = [(1, 0), (0, 1), (1, 1)]
PEERS = [(dx, dy, dc) for dx in (0, 1) for dy in (0, 1) for dc in (0, 1)][1:]


class _Rider:
    def __init__(self, operands, out_shapes, sem_shapes, start, finish):
        self.operands, self.out_shapes, self.sem_shapes = list(operands), list(out_shapes), list(sem_shapes)
        self.start, self.finish = start, finish


def _run_rider(rider, name):
    n_in, n_out = len(rider.operands), len(rider.out_shapes)

    def body(*refs):
        ins, outs, sems = refs[:n_in], refs[n_in:n_in + n_out], refs[n_in + n_out:]
        rider.start(ins, outs, sems)
        rider.finish(ins, outs, sems)

    return pl.pallas_call(
        body, name=name, out_shape=tuple(rider.out_shapes),
        in_specs=[ANY] * n_in, out_specs=tuple([ANY] * n_out), scratch_shapes=rider.sem_shapes,
        compiler_params=pltpu.CompilerParams(has_side_effects=True),
    )(*rider.operands)


def _gather_rider(big, small):
    nbig, n = len(big), len(big) + len(small)
    arrays = list(big) + list(small)

    def plan(ins, outs, sems):
        ici_send, ici_recv, d2d_send, d2d_recv, local_sems = sems
        x, y, c = _coords()
        mine = 2 * x + y

        def half(w, h):
            r2 = arrays[w].shape[0] // 2
            return pl.ds(h * r2, r2)

        def ici(w, j, landing):
            px, py = _flip(x, CHIPS[j][0]), _flip(y, CHIPS[j][1])
            slot = 2 * px + py if landing else mine
            if w < nbig:
                src, dst = ins[w].at[half(w, c)], outs[w].at[slot, half(w, c)]
            else:
                src, dst = ins[w], outs[w].at[slot]
            return pltpu.make_async_remote_copy(
                src_ref=src, dst_ref=dst, send_sem=ici_send.at[w * 3 + j], recv_sem=ici_recv.at[w * 3 + j],
                device_id=(px, py, c), device_id_type=MESH)

        def d2d(w, j, landing):
            px, py = _flip(x, CHIPS[j][0]), _flip(y, CHIPS[j][1])
            mine_rows = outs[w].at[2 * px + py, half(w, c)]
            dst = outs[w].at[2 * px + py, half(w, 1 - c)] if landing else mine_rows
            return pltpu.make_async_remote_copy(
                src_ref=mine_rows, dst_ref=dst, send_sem=d2d_send.at[w * 3 + j], recv_sem=d2d_recv.at[w * 3 + j],
                device_id=(x, y, 1 - c), device_id_type=MESH)

        local = [pltpu.make_async_copy(ins[w], outs[w].at[mine], local_sems.at[w]) for w in range(n)]
        return ici, d2d, local

    def start(ins, outs, sems):
        ici, _, local = plan(ins, outs, sems)
        for cp in local:
            cp.start()
        for w in range(n):
            for j in range(3):
                ici(w, j, False).start()

    def finish(ins, outs, sems):
        ici, d2d, local = plan(ins, outs, sems)
        for w in range(n):
            for j in range(3):
                ici(w, j, True).wait_recv()
                if w < nbig:
                    d2d(w, j, False).start()
        for w in range(nbig):
            for j in range(3):
                d2d(w, j, True).wait_recv()
        for w in range(n):
            for j in range(3):
                ici(w, j, False).wait_send()
                if w < nbig:
                    d2d(w, j, False).wait_send()
        for cp in local:
            cp.wait()

    return _Rider(
        arrays, [jax.ShapeDtypeStruct((4,) + s.shape, s.dtype) for s in arrays],
        [pltpu.SemaphoreType.DMA((3 * n,)), pltpu.SemaphoreType.DMA((3 * n,)),
         pltpu.SemaphoreType.DMA((max(3 * nbig, 1),)), pltpu.SemaphoreType.DMA((max(3 * nbig, 1),)),
         pltpu.SemaphoreType.DMA((n,))],
        start, finish)


def _to_sibling(arrays, name):
    n = len(arrays)

    def body(*refs):
        ins, outs = refs[:n], refs[n:2 * n]
        send_sems, recv_sems = refs[2 * n:]
        x, y, c = _coords()

        def copy(w):
            return pltpu.make_async_remote_copy(
                src_ref=ins[w], dst_ref=outs[w], send_sem=send_sems.at[w], recv_sem=recv_sems.at[w],
                device_id=(x, y, 1 - c), device_id_type=MESH)

        for w in range(n):
            copy(w).start()
        for w in range(n):
            copy(w).wait_recv()
            copy(w).wait_send()

    return pl.pallas_call(
        body, name=name,
        out_shape=tuple(jax.ShapeDtypeStruct(a.shape, a.dtype) for a in arrays),
        in_specs=[ANY] * n, out_specs=tuple([ANY] * n),
        scratch_shapes=[pltpu.SemaphoreType.DMA((n,)), pltpu.SemaphoreType.DMA((n,))],
        compiler_params=pltpu.CompilerParams(has_side_effects=True),
    )(*arrays)


def _pair_sum(a, b, name):
    _, r, c = a.shape
    tr = _tile(r, (64, 32, 16))

    def body(a_ref, b_ref, o_ref):
        o_ref[...] = (a_ref[...].astype(F32) + b_ref[...].astype(F32)).astype(BF16)

    blk = pl.BlockSpec((4, tr, c), lambda i: (0, i, 0))
    return pl.pallas_call(
        body, name=name, out_shape=jax.ShapeDtypeStruct(a.shape, BF16), grid=(r // tr,),
        in_specs=[blk, blk], out_specs=blk, compiler_params=_params("parallel"),
    )(a, b)


def _scatter_rider(parts):
    n = len(parts)

    def plan(ins, outs, sems):
        send_sems, recv_sems, local_sems = sems
        x, y, c = _coords()
        mine = 2 * x + y

        def ici(w, j, landing):
            px, py = _flip(x, CHIPS[j][0]), _flip(y, CHIPS[j][1])
            return pltpu.make_async_remote_copy(
                src_ref=ins[w].at[2 * px + py], dst_ref=outs[w].at[2 * px + py if landing else mine],
                send_sem=send_sems.at[w * 3 + j], recv_sem=recv_sems.at[w * 3 + j],
                device_id=(px, py, c), device_id_type=MESH)

        local = [pltpu.make_async_copy(ins[w].at[mine], outs[w].at[mine], local_sems.at[w]) for w in range(n)]
        return ici, local

    def start(ins, outs, sems):
        ici, local = plan(ins, outs, sems)
        for cp in local:
            cp.start()
        for w in range(n):
            for j in range(3):
                ici(w, j, False).start()

    def finish(ins, outs, sems):
        ici, local = plan(ins, outs, sems)
        for w in range(n):
            for j in range(3):
                ici(w, j, True).wait_recv()
                ici(w, j, False).wait_send()
        for cp in local:
            cp.wait()

    return _Rider(
        parts, [jax.ShapeDtypeStruct(p.shape, p.dtype) for p in parts],
        [pltpu.SemaphoreType.DMA((3 * n,)), pltpu.SemaphoreType.DMA((3 * n,)), pltpu.SemaphoreType.DMA((n,))],
        start, finish)


def _sum4(recv, name):
    _, r, c = recv.shape
    tr = _tile(r, (64, 32, 16))

    def body(in_ref, o_ref):
        o_ref[...] = ((in_ref[0].astype(F32) + in_ref[1].astype(F32)) + in_ref[2].astype(F32)) + in_ref[3].astype(F32)

    return pl.pallas_call(
        body, name=name, out_shape=jax.ShapeDtypeStruct((r, c), F32), grid=(r // tr,),
        in_specs=[pl.BlockSpec((4, tr, c), lambda i: (0, i, 0))],
        out_specs=pl.BlockSpec((tr, c), lambda i: (i, 0)),
        compiler_params=_params("parallel"),
    )(recv)


def _allreduce_small(packed):
    R = packed.shape[0]

    def body(in_ref, o_ref, buf, send_sems, recv_sems):
        x, y, c = _coords()
        me = 4 * x + 2 * y + c
        buf[me] = in_ref[...]
        for j, (dx, dy, dc) in enumerate(PEERS):
            px, py, pc = _flip(x, dx), _flip(y, dy), _flip(c, dc)
            pltpu.make_async_remote_copy(
                src_ref=in_ref, dst_ref=buf.at[me], send_sem=send_sems.at[j], recv_sem=recv_sems.at[j],
                device_id=(px, py, pc), device_id_type=MESH).start()
        for j, (dx, dy, dc) in enumerate(PEERS):
            px, py, pc = _flip(x, dx), _flip(y, dy), _flip(c, dc)
            rc = pltpu.make_async_remote_copy(
                src_ref=in_ref, dst_ref=buf.at[4 * px + 2 * py + pc], send_sem=send_sems.at[j],
                recv_sem=recv_sems.at[j], device_id=(px, py, pc), device_id_type=MESH)
            rc.wait_recv()
            rc.wait_send()
        acc = buf[0]
        for s in range(1, 8):
            acc = acc + buf[s]
        o_ref[...] = acc

    return pl.pallas_call(
        body, name="allreduce_small", out_shape=jax.ShapeDtypeStruct((R, 128), F32),
        in_specs=[pl.BlockSpec(memory_space=pltpu.VMEM)], out_specs=pl.BlockSpec(memory_space=pltpu.VMEM),
        scratch_shapes=[pltpu.VMEM((8, R, 128), F32), pltpu.SemaphoreType.DMA((7,)), pltpu.SemaphoreType.DMA((7,))],
        compiler_params=pltpu.CompilerParams(has_side_effects=True, vmem_limit_bytes=VMEM_LIMIT_BYTES),
    )(packed)


def _adamw_math(w, g, m, v):
    m = ADAM_B1 * m + (1.0 - ADAM_B1) * g
    v = ADAM_B2 * v + (1.0 - ADAM_B2) * (g * g)
    m_hat = m / (1.0 - ADAM_B1 ** ADAM_STEP)
    v_hat = v / (1.0 - ADAM_B2 ** ADAM_STEP)
    delta = -ADAM_LR * (m_hat / (jnp.sqrt(v_hat) + ADAM_EPS) + ADAM_WD * w)
    return delta, m, v


def _adamw_big(mine, other, w, m, v, name):
    R, C = w.shape
    tr = _tile(R // 2, (128, 64, 32, 16, 8))
    nb = (R // 2) // tr

    def body(mine_ref, other_ref, w_ref, m_ref, v_ref, g_ref, d_ref, mo_ref, vo_ref):
        is_mine = (pl.program_id(0) // nb) == lax.axis_index("c")
        g = jnp.where(is_mine, mine_ref[...], other_ref[...])
        d, mn, vn = _adamw_math(w_ref[...], g, m_ref[...], v_ref[...])
        g_ref[...] = g
        d_ref[...] = d
        mo_ref[...] = mn
        vo_ref[...] = vn

    blk = pl.BlockSpec((tr, C), lambda i: (i, 0))
    half = pl.BlockSpec((tr, C), lambda i: (i % nb, 0))
    sds = jax.ShapeDtypeStruct((R, C), F32)
    return pl.pallas_call(
        body, name=name, out_shape=(sds, sds, sds, sds), grid=(2 * nb,),
        in_specs=[half, half, blk, blk, blk], out_specs=(blk, blk, blk, blk), compiler_params=_params("parallel"),
    )(mine, other, w, m, v)


def _adamw_small(items, lb_raw, dlb):
    n = len(items)
    lb_w, lb_m, lb_v = lb_raw

    def body(*refs):
        ins = refs[:4 * n]
        dlb_ref, lw_ref, lm_ref, lv_ref = refs[4 * n:4 * n + 4]
        outs = refs[4 * n + 4:]
        for t in range(n):
            g_ref, w_ref, m_ref, v_ref = ins[4 * t:4 * t + 4]
            d, mn, vn = _adamw_math(w_ref[...], g_ref[...], m_ref[...], v_ref[...])
            outs[3 * t][...] = d
            outs[3 * t + 1][...] = mn
            outs[3 * t + 2][...] = vn
        p0 = 1.0 / (1.0 + jnp.exp(lw_ref[1:2, :] - lw_ref[0:1, :]))
        g0 = dlb_ref[...] * p0 * (1.0 - p0)
        base = 3 * n
        outs[base][0:1, :] = g0
        outs[base][1:2, :] = -g0
        d, mn, vn = _adamw_math(lw_ref[...], outs[base][...], lm_ref[...], lv_ref[...])
        outs[base + 1][...] = d
        outs[base + 2][...] = mn
        outs[base + 3][...] = vn

    operands = [a for it in items for a in it] + [dlb, lb_w, lb_m, lb_v]
    out_shape = []
    for (g, w, m, v) in items:
        out_shape += [jax.ShapeDtypeStruct(w.shape, F32)] * 3
    out_shape += [jax.ShapeDtypeStruct(lb_w.shape, F32)] * 4
    vm = pl.BlockSpec(memory_space=pltpu.VMEM)
    res = pl.pallas_call(
        body, name="adamw_small", out_shape=tuple(out_shape),
        in_specs=[vm] * len(operands), out_specs=tuple([vm] * len(out_shape)),
        compiler_params=pltpu.CompilerParams(vmem_limit_bytes=VMEM_LIMIT_BYTES),
    )(*operands)
    deltas = [res[3 * t] for t in range(n)] + [res[3 * n + 1]]
    new_m = [res[3 * t + 1] for t in range(n)] + [res[3 * n + 2]]
    new_v = [res[3 * t + 2] for t in range(n)] + [res[3 * n + 3]]
    return res[3 * n], deltas, new_m, new_v


def _shard_row_half(g, by_cols, h):
    if isinstance(g, (list, tuple)):
        R, widths = g[0].shape[0], [s.shape[1] for s in g]
        starts = [sum(widths[:n]) for n in range(len(g))]
        cs = sum(widths) // 4
        rows = [lax.dynamic_slice_in_dim(s, h * (R // 2), R // 2, axis=0) for s in g]
        shards = []
        for k in range(4):
            lo, hi = k * cs, (k + 1) * cs
            cut = [r[:, max(lo - o, 0):min(hi - o, w)] for r, o, w in zip(rows, starts, widths) if lo < o + w and hi > o]
            shards.append(jnp.concatenate(cut, axis=1).astype(BF16))
        return jnp.stack(shards)
    if g.ndim == 3:
        return lax.dynamic_slice_in_dim(g, h * (g.shape[1] // 2), g.shape[1] // 2, axis=1).astype(BF16)
    R, C = g.shape
    if by_cols:
        part = lax.dynamic_index_in_dim(g.reshape(2, R // 2, 4, C // 4), h, axis=0, keepdims=False)
        return part.transpose(1, 0, 2).astype(BF16)
    return lax.dynamic_index_in_dim(g.reshape(4, 2, R // 8, C), h, axis=1, keepdims=False).astype(BF16)


def kernel(x, positions, meta_tokens, w_in, w_q_up, w_kv_up, w_branch_mla, w_branch_hgrn, w_out, w_ffn_in, w_ffn_out, conv_w, conv_b, g_mix_norm, g_q_norm, g_kv_norm, g_hgrn_norm, g_ffn_norm, g_final_norm, lb_raw, loss_target, m_meta_tokens, m_w_in, m_w_q_up, m_w_kv_up, m_w_branch_mla, m_w_branch_hgrn, m_w_out, m_w_ffn_in, m_w_ffn_out, m_conv_w, m_conv_b, m_g_mix_norm, m_g_q_norm, m_g_kv_norm, m_g_hgrn_norm, m_g_ffn_norm, m_g_final_norm, m_lb_raw, v_meta_tokens, v_w_in, v_w_q_up, v_w_kv_up, v_w_branch_mla, v_w_branch_hgrn, v_w_out, v_w_ffn_in, v_w_ffn_out, v_conv_w, v_conv_b, v_g_mix_norm, v_g_q_norm, v_g_kv_norm, v_g_hgrn_norm, v_g_ffn_norm, v_g_final_norm, v_lb_raw):
    S, D = x.shape[1], x.shape[2]
    L = S + PREFIX
    QL, KVL = g_q_norm.shape[1], g_kv_norm.shape[1]
    F = conv_b.shape[1]
    heads = (4 * w_kv_up.shape[2]) // QPAD
    nh = D // HEAD
    assert lb_raw.shape[0] == 2 and g_hgrn_norm.shape[1] == HEAD and L % CHUNK == 0
    ix, iy, ic = _coords()
    chip = 2 * ix + iy

    big = [w_in, w_q_up, w_kv_up, w_branch_mla, w_branch_hgrn, w_out, w_ffn_in, w_ffn_out]
    col_sharded = [True, True, True, False, False, False, True, False]
    shards = [w[0].astype(BF16) for w in big]
    early = _run_rider(_gather_rider(shards[:1], [meta_tokens, conv_w[0]]), "gather_early")
    qkv_rider = _gather_rider(shards[1:3], [])
    late_rider = _gather_rider(shards[3:], [])

    def full(gw, by_cols):
        _, r, c = gw.shape
        return gw.transpose(1, 0, 2).reshape(r, 4 * c) if by_cols else gw.reshape(4 * r, c)

    def col_range(g4, lo, hi):
        c = g4.shape[2]
        pieces = [g4[k][:, max(lo - k * c, 0):min(hi - k * c, c)] for k in range(4) if lo < (k + 1) * c and hi > k * c]
        return pieces[0] if len(pieces) == 1 else jnp.concatenate(pieces, axis=1)

    meta_full = full(early[1], True)
    cw_full = full(early[2], True)
    c0 = QL + KVL
    W_lat = col_range(early[0], 0, c0)
    W_kr = jnp.pad(col_range(early[0], c0, c0 + ROPE), ((0, 0), (0, 128 - ROPE)))
    W_H = col_range(early[0], c0 + ROPE, c0 + ROPE + 4 * D)
    W_G = col_range(early[0], c0 + ROPE + 4 * D, 4 * early[0].shape[2])

    pos = jnp.concatenate([jnp.zeros((PAD_LEN,), jnp.int32), jnp.arange(N_META, dtype=jnp.int32),
                           positions[0].astype(jnp.int32) + N_META])
    inv = 1.0 / (ROPE_THETA ** (jnp.arange(0, ROPE, 2, dtype=F32) / ROPE))
    ang = pos.astype(F32)[:, None] * inv
    zero = jnp.zeros((L, 128 - ROPE), F32)
    cos = jnp.concatenate([jnp.cos(ang), jnp.cos(ang), zero], axis=1)
    sin = jnp.concatenate([jnp.sin(ang), jnp.sin(ang), zero], axis=1)

    h0, u1 = _embed_norm(x[0], meta_full, g_mix_norm)
    hp, qkv = _mm(u1, W_H, "nn", F32, "proj_hgrn", rider=qkv_rider)
    W_q, W_kv = [full(gw, True) for gw in qkv]
    W_qp = jnp.pad(W_q.reshape(QL, heads, QK_HEAD), ((0, 0), (0, 0), (0, QPAD - QK_HEAD))).reshape(QL, heads * QPAD)
    lat = _mm(u1, W_lat, "nn", F32, "proj_lat")
    gates = _mm(u1, W_G, "nn", F32, "proj_gates")
    kr = _mm(u1, W_kr, "nn", F32, "proj_krope")
    qn = _rmsnorm_fwd(lat, g_q_norm, "norm_q", col0=0, width=QL)
    kvn = _rmsnorm_fwd(lat, g_kv_norm, "norm_kv", col0=QL, width=KVL)
    qp = _mm(qn, W_qp, "nn", BF16, "q_up")
    kv = _mm(kvn, W_kv, "nn", BF16, "kv_up")
    qc, kc, vv = _rope_fwd(qp, kv, kr, cos, sin, heads)
    o_mla, lse, late = _attn_fwd(qc, kc, vv, heads, late_rider)
    W_a, W_b, W_o, W_fi, W_fo = [full(gw, bc) for gw, bc in zip(late, col_sharded[3:])]
    o_hgrn, o_rec, s_hist = _hgrn_fwd(hp, lb_raw, g_hgrn_norm, nh)
    br_a = _mm(o_mla, W_a, "nn", F32, "branch_mla")
    br_b = _mm(o_hgrn, W_b, "nn", F32, "branch_hgrn")
    merged = _merge_fwd(br_a, br_b, gates)
    h1 = _mm(merged, W_o, "nn", F32, "out_proj", res=h0)
    u2 = _rmsnorm_fwd(h1, g_ffn_norm, "norm_ffn")
    gu = _mm(u2, W_fi, "nn", BF16, "ffn_in")
    act = _conv_fwd(gu, cw_full, conv_b)
    h2 = _mm(act, W_fo, "nn", F32, "ffn_out", res=h1)
    dh2, loss_p, dg_final = _final_loss_bwd(h2, loss_target[0], g_final_norm.reshape(1, D))

    dact = _mm(dh2, W_fo, "nt", F32, "d_act")
    dW_fo = _mm(act, dh2, "tn", F32, "dw_ffn_out")
    dconv, dgu_right, dcb, dcw = _conv_bwd_a(dact, gu, cw_full, conv_b)
    dgu = _conv_bwd_b(dconv, cw_full, dgu_right)
    du2 = _mm(dgu, W_fi, "nt", F32, "d_u2")
    dW_fi = _mm(u2, dgu, "tn", F32, "dw_ffn_in", col_shards=4)
    dh1, dg_ffn = _rmsnorm_bwd(du2, h1, g_ffn_norm, "norm_ffn_bwd", F32, res=dh2)
    dmerged = _mm(dh1, W_o, "nt", F32, "d_merged")
    dW_o = _mm(merged, dh1, "tn", F32, "dw_out")
    d_a, d_b, d_gates = _merge_bwd(dmerged, br_a, br_b, gates)
    do_mla = _mm(d_a, W_a, "nt", BF16, "d_o_mla")
    dW_a = _mm(o_mla, d_a, "tn", F32, "dw_branch_mla")
    do_hgrn = _mm(d_b, W_b, "nt", F32, "d_o_hgrn")
    dW_b = _mm(o_hgrn, d_b, "tn", F32, "dw_branch_hgrn")
    names = ["w_in", "w_q_up", "w_kv_up", "w_branch_mla", "w_branch_hgrn", "w_out", "w_ffn_in", "w_ffn_out"]

    def chip_partials(grads, by_cols, nms, tag):
        keep = [_shard_row_half(g, bc, ic) for g, bc in zip(grads, by_cols)]
        give = [_shard_row_half(g, bc, 1 - ic) for g, bc in zip(grads, by_cols)]
        taken = _to_sibling(give, "pair_exchange_" + tag)
        return [_pair_sum(a, b, "pair_sum_" + nm) for a, b, nm in zip(keep, taken, nms)]

    late_parts = chip_partials([dW_a, dW_b, dW_o, dW_fi, dW_fo], col_sharded[3:], names[3:], "late")
    dhq, dhf, dhi, dhg, dgn_p, dlb_p = _hgrn_bwd(hp, lb_raw, g_hgrn_norm, do_hgrn, o_rec, s_hist, nh)
    dqc, dkc, dvv, late_recv = _attn_bwd(qc, kc, vv, do_mla, lse, _attn_delta(o_mla, do_mla, heads), heads,
                                         _scatter_rider(late_parts))
    dqp, dkv, dkr = _rope_bwd(dqc, dkc, dvv, cos, sin, heads)
    dqn = _mm(dqp, W_qp, "nt", F32, "d_qn")
    dW_qp = _mm(qn, dqp, "tn", F32, "dw_q_up")
    dkvn = _mm(dkv, W_kv, "nt", F32, "d_kvn")
    dW_kv = _mm(kvn, dkv, "tn", F32, "dw_kv_up")
    dq_lat, dg_q = _rmsnorm_bwd(dqn, lat, g_q_norm, "norm_q_bwd", BF16, col0=0)
    dkv_lat, dg_kv = _rmsnorm_bwd(dkvn, lat, g_kv_norm, "norm_kv_bwd", BF16, col0=QL)
    dlat = jnp.concatenate([dq_lat, dkv_lat], axis=1)
    dhp = jnp.concatenate([dhq, dhf, dhi, dhg], axis=1)
    dW_lat = _mm(u1, dlat, "tn", F32, "dw_in_lat")
    dW_H = _mm(u1, dhp, "tn", F32, "dw_in_hgrn")
    dW_G = _mm(u1, d_gates, "tn", F32, "dw_in_gates")
    dW_kr = _mm(u1, dkr, "tn", F32, "dw_in_krope")
    dW_in = [dW_lat, dW_kr[:, :ROPE], dW_H, dW_G]
    dW_q = dW_qp.reshape(QL, heads, QPAD)[:, :, :QK_HEAD].reshape(QL, heads * QK_HEAD)
    early_parts = chip_partials([dW_in, dW_q, dW_kv], col_sharded[:3], names[:3], "early")
    du1 = _mm(dlat, W_lat, "nt", F32, "d_u1_lat")
    du1, early_recv = _mm(dhp, W_H, "nt", F32, "d_u1_hgrn", res=du1, rider=_scatter_rider(early_parts))
    du1 = _mm(d_gates, W_G, "nt", F32, "d_u1_gates", res=du1)
    du1 = _mm(dkr, W_kr, "nt", F32, "d_u1_krope", res=du1)
    (dx_tokens, dh0_prefix), dg_mix = _rmsnorm_bwd(du1, h0, g_mix_norm, "norm_mix_bwd", F32, res=dh1,
                                                   split_prefix=True)
    grad_x = dx_tokens[None]

    received = list(early_recv) + list(late_recv)
    halves = [_sum4(r, "sum_" + nm) for r, nm in zip(received, names)]
    others = _to_sibling(halves, "swap_halves")
    big_m = [m_w_in, m_w_q_up, m_w_kv_up, m_w_branch_mla, m_w_branch_hgrn, m_w_out, m_w_ffn_in, m_w_ffn_out]
    big_v = [v_w_in, v_w_q_up, v_w_kv_up, v_w_branch_mla, v_w_branch_hgrn, v_w_out, v_w_ffn_in, v_w_ffn_out]
    big_out = {}
    for nm, mine, other, w, m, v in zip(names, halves, others, big, big_m, big_v):
        g, d, mn, vn = _adamw_big(mine, other, w[0], m[0], v[0], "adamw_" + nm)
        big_out[nm] = (g[None], d[None], mn[None], vn[None])

    pieces = [loss_p[:, :1], dg_mix, dg_q, dg_kv, jnp.sum(dgn_p[:, 0, :], axis=0, keepdims=True), dg_ffn, dg_final,
              dlb_p[:, 0, :].reshape(1, D), dcb, dcw[0:3].reshape(1, 3 * F), dh0_prefix[PAD_LEN:PREFIX].reshape(1, N_META * D)]
    sizes = [p.shape[1] for p in pieces]
    flat = jnp.concatenate(pieces, axis=1)[0]
    rows = -(-flat.shape[0] // 1024) * 8
    packed = jnp.pad(flat, (0, rows * 128 - flat.shape[0])).reshape(rows, 128)
    total = _allreduce_small(packed).reshape(-1)
    offs = [0]
    for s in sizes:
        offs.append(offs[-1] + s)
    loss, g_mix, g_q, g_kv, g_hg, g_ffn, g_fin, dlb, g_cb, g_cw, g_meta = [
        total[offs[t]:offs[t + 1]].reshape(1, sizes[t]) for t in range(len(sizes))]
    g_cw = lax.dynamic_slice_in_dim(g_cw.reshape(3, F), chip * (F // 4), F // 4, axis=1)
    g_meta = lax.dynamic_slice_in_dim(g_meta.reshape(N_META, D), chip * (D // 4), D // 4, axis=1)
    items = [(g_meta, meta_tokens, m_meta_tokens, v_meta_tokens),
             (g_cw, conv_w[0], m_conv_w[0], v_conv_w[0]),
             (g_cb, conv_b, m_conv_b, v_conv_b),
             (g_mix, g_mix_norm, m_g_mix_norm, v_g_mix_norm),
             (g_q, g_q_norm, m_g_q_norm, v_g_q_norm),
             (g_kv, g_kv_norm, m_g_kv_norm, v_g_kv_norm),
             (g_hg, g_hgrn_norm, m_g_hgrn_norm, v_g_hgrn_norm),
             (g_ffn, g_ffn_norm, m_g_ffn_norm, v_g_ffn_norm),
             (g_fin, g_final_norm.reshape(1, D), m_g_final_norm.reshape(1, D), v_g_final_norm.reshape(1, D))]
    g_lb, s_delta, s_m, s_v = _adamw_small(items, (lb_raw, m_lb_raw, v_lb_raw), dlb)
    s_grads = [it[0] for it in items] + [g_lb]

    def shape_small(vals):
        meta, cw, cb, mix, q, kvg, hg, ffn, fin, lb = vals
        return [meta, cw[None], cb, mix, q, kvg, hg, ffn, fin.reshape(D), lb]

    s_grads, s_delta, s_m, s_v = [shape_small(v) for v in (s_grads, s_delta, s_m, s_v)]

    def ordered(kind, small):
        bigs = [big_out[nm][kind] for nm in names]
        return [small[0]] + bigs + small[1:]

    return (loss.reshape(()), grad_x, *ordered(0, s_grads), *ordered(1, s_delta), *ordered(2, s_m), *ordered(3, s_v))
```

```python
import math

import jax
import jax.numpy as jnp
from jax import lax
from jax.experimental import pallas as pl
from jax.experimental.pallas import tpu as pltpu

F32 = jnp.float32
BF16 = jnp.bfloat16
MESH = pl.DeviceIdType.MESH

NORM_EPS = 1e-6
N_META = 16
PREFIX = 128
PAD_LEN = PREFIX - N_META
HEAD = 128
ROPE = 64
QK_HEAD = HEAD + ROPE
QPAD = 2 * HEAD
SOFTMAX_SCALE = QK_HEAD ** -0.5
ROPE_THETA = 10000.0
CHUNK = 128
HALO = 16
LEVELS = 7
HGRN_FWD_HEADS = 4
HGRN_BWD_HEADS = 4
ATTN_FWD_HEADS = 2
ATTN_BWD_HEADS = 2
NEG = -1e30

ADAM_LR = 0.001
ADAM_B1 = 0.9
ADAM_B2 = 0.999
ADAM_EPS = 1e-08
ADAM_WD = 0.01
ADAM_STEP = 10

VMEM_LIMIT_BYTES = 48 * 1024 * 1024


def _params(*sem):
    return pltpu.CompilerParams(dimension_semantics=sem, vmem_limit_bytes=VMEM_LIMIT_BYTES)


def _tile(n, prefs):
    for p in prefs:
        if n % p == 0:
            return p
    return n


ROW_TILES = (640, 512, 256, 128, 64, 32, 16, 8)
TN_ROW_TILES = (1024, 1408, 1536, 512, 256, 128)
TN_K_TILES = (1664, 640, 512, 256, 128)
TALL_ROW_TILE = 1664
SHARD_COL_TILES = (1408, 1024, 512, 256, 128)
COL_TILES = (1024, 512, 256, 128)
K_TILES = (2048, 1536, 1408, 1024, 512, 256, 128)


def _sigmoid(x):
    return 1.0 / (1.0 + jnp.exp(-x))


def _dot(a, b, dims):
    return lax.dot_general(a.astype(BF16), b.astype(BF16), (dims, ((), ())), preferred_element_type=F32)


NN = ((1,), (0,))
TN = ((0,), (0,))
NT = ((1,), (1,))


def _split3(x):
    hi = x.astype(BF16)
    r = x - hi.astype(F32)
    mid = r.astype(BF16)
    lo = (r - mid.astype(F32)).astype(BF16)
    return hi, mid, lo


def _dot_exact_rhs(sel, x, dims):
    hi, mid, lo = _split3(x)
    return _dot(sel, hi, dims) + _dot(sel, mid, dims) + _dot(sel, lo, dims)


def _mm(a, b, mode, out_dtype, name, res=None, rider=None, col_shards=1):
    if mode == "nn":
        (M, K), (K2, N) = a.shape, b.shape
    elif mode == "tn":
        (K, M), (K2, N) = a.shape, b.shape
    else:
        (M, K), (N, K2) = a.shape, b.shape
    assert K == K2, (name, a.shape, b.shape)
    tm = _tile(M, TN_ROW_TILES if mode == "tn" else ROW_TILES)
    tn = _tile(N // col_shards, COL_TILES if col_shards == 1 else SHARD_COL_TILES)
    tk = _tile(K, TN_K_TILES if mode == "tn" else K_TILES)
    nk = K // tk
    if mode != "tn" and nk > 1 and res is None and a.dtype == BF16 and M % TALL_ROW_TILE == 0:
        tm = TALL_ROW_TILE
    dims = {"nn": NN, "tn": TN, "nt": NT}[mode]

    n_rin = 0 if rider is None else len(rider.operands)
    n_rout = 0 if rider is None else len(rider.out_shapes)
    n_rsem = 0 if rider is None else len(rider.sem_shapes)
    grid = (M // tm, N // tn, nk)

    def body(*refs):
        main_in, r_ins, (o_ref,), r_outs, accs, r_sems = _split_refs(
            refs, (2 if res is None else 3, n_rin, 1, n_rout, 0 if nk == 1 else 1, n_rsem))
        a_ref, b_ref = main_in[:2]
        r_ref = None if res is None else main_in[2]
        i, j, k = pl.program_id(0), pl.program_id(1), pl.program_id(2)
        if rider is not None:
            pl.when((i == 0) & (j == 0) & (k == 0))(lambda: rider.start(r_ins, r_outs, r_sems))

        def finish(r):
            if r_ref is not None:
                r = r + r_ref[...].astype(F32)
            o_ref[...] = r.astype(out_dtype)

        if nk == 1:
            finish(_dot(a_ref[...], b_ref[...], dims))
        else:
            acc = accs[0]

            @pl.when(k == 0)
            def _():
                acc[...] = jnp.zeros_like(acc)

            acc[...] += _dot(a_ref[...], b_ref[...], dims)

            @pl.when(k == nk - 1)
            def _():
                finish(acc[...])

        if rider is not None:
            last = (i == grid[0] - 1) & (j == grid[1] - 1) & (k == nk - 1)
            pl.when(last)(lambda: rider.finish(r_ins, r_outs, r_sems))

    if mode == "nn":
        a_spec = pl.BlockSpec((tm, tk), lambda i, j, k: (i, k))
        b_spec = pl.BlockSpec((tk, tn), lambda i, j, k: (k, j))
    elif mode == "tn":
        a_spec = pl.BlockSpec((tk, tm), lambda i, j, k: (k, i))
        b_spec = pl.BlockSpec((tk, tn), lambda i, j, k: (k, j))
    else:
        a_spec = pl.BlockSpec((tm, tk), lambda i, j, k: (i, k))
        b_spec = pl.BlockSpec((tn, tk), lambda i, j, k: (j, k))
    in_specs = [a_spec, b_spec]
    operands = [a, b]
    if res is not None:
        in_specs.append(pl.BlockSpec((tm, tn), lambda i, j, k: (i, j)))
        operands.append(res)
    if col_shards == 1:
        out_shape = jax.ShapeDtypeStruct((M, N), out_dtype)
        out_spec = pl.BlockSpec((tm, tn), lambda i, j, k: (i, j))
    else:
        per_shard = (N // col_shards) // tn
        out_shape = jax.ShapeDtypeStruct((col_shards, M, N // col_shards), out_dtype)
        out_spec = pl.BlockSpec((None, tm, tn), lambda i, j, k: (j // per_shard, i, j % per_shard))
    scratch = [] if nk == 1 else [pltpu.VMEM((tm, tn), F32)]
    if rider is None:
        return pl.pallas_call(
            body, name=name, out_shape=out_shape, grid=grid, in_specs=in_specs, out_specs=out_spec,
            scratch_shapes=scratch, compiler_params=_params("parallel", "parallel", "arbitrary"),
        )(*operands)
    res_all = pl.pallas_call(
        body, name=name, out_shape=(out_shape, *rider.out_shapes), grid=grid,
        in_specs=in_specs + [ANY] * n_rin, out_specs=(out_spec, *([ANY] * n_rout)),
        scratch_shapes=scratch + rider.sem_shapes,
        compiler_params=pltpu.CompilerParams(dimension_semantics=("arbitrary", "arbitrary", "arbitrary"),
                                             vmem_limit_bytes=VMEM_LIMIT_BYTES, has_side_effects=True),
    )(*operands, *rider.operands)
    return res_all[0], res_all[1:]


def _rmsnorm_fwd(x, g, name, col0=0, width=None):
    L = x.shape[0]
    width = x.shape[1] if width is None else width
    assert col0 % width == 0
    cb = col0 // width
    tm = _tile(L, (128, 64, 32, 16))

    def body(x_ref, g_ref, o_ref):
        xv = x_ref[...]
        r = lax.rsqrt(jnp.mean(xv * xv, axis=-1, keepdims=True) + NORM_EPS)
        o_ref[...] = ((xv * r) * g_ref[...]).astype(BF16)

    return pl.pallas_call(
        body, name=name,
        out_shape=jax.ShapeDtypeStruct((L, width), BF16),
        grid=(L // tm,),
        in_specs=[pl.BlockSpec((tm, width), lambda i: (i, cb)), pl.BlockSpec((1, width), lambda i: (0, 0))],
        out_specs=pl.BlockSpec((tm, width), lambda i: (i, 0)),
        compiler_params=_params("parallel"),
    )(x, g)


def _embed_norm(x, meta, g):
    S, D = x.shape
    L = S + PREFIX
    tm = PREFIX

    def body(x_ref, meta_ref, g_ref, h_ref, u_ref):
        i = pl.program_id(0)

        @pl.when(i == 0)
        def _():
            h_ref[...] = jnp.zeros_like(h_ref)
            h_ref[PAD_LEN:PREFIX, :] = meta_ref[...]

        @pl.when(i > 0)
        def _():
            h_ref[...] = x_ref[...]

        xv = h_ref[...]
        r = lax.rsqrt(jnp.mean(xv * xv, axis=-1, keepdims=True) + NORM_EPS)
        u_ref[...] = ((xv * r) * g_ref[...]).astype(BF16)

    return pl.pallas_call(
        body, name="embed_norm_mix",
        out_shape=(jax.ShapeDtypeStruct((L, D), F32), jax.ShapeDtypeStruct((L, D), BF16)),
        grid=(L // tm,),
        in_specs=[pl.BlockSpec((tm, D), lambda i: (jnp.maximum(i - 1, 0), 0)),
                  pl.BlockSpec((N_META, D), lambda i: (0, 0)), pl.BlockSpec((1, D), lambda i: (0, 0))],
        out_specs=(pl.BlockSpec((tm, D), lambda i: (i, 0)), pl.BlockSpec((tm, D), lambda i: (i, 0))),
        compiler_params=_params("parallel"),
    )(x, meta, g)


def _rmsnorm_bwd(dy, x, g, name, out_dtype, col0=0, res=None, split_prefix=False):
    L, width = dy.shape
    assert col0 % width == 0
    cb = col0 // width
    tm = PREFIX if split_prefix else _tile(L, (128, 64, 32, 16))

    def body(*refs):
        head_ref = None
        if split_prefix:
            refs, head_ref = refs[:-1], refs[-1]
        if res is None:
            dy_ref, x_ref, g_ref, dx_ref, dg_ref = refs
            r_ref = None
        else:
            dy_ref, x_ref, g_ref, r_ref, dx_ref, dg_ref = refs

        @pl.when(pl.program_id(0) == 0)
        def _():
            dg_ref[...] = jnp.zeros_like(dg_ref)

        xv = x_ref[...]
        dyv = dy_ref[...].astype(F32)
        r = lax.rsqrt(jnp.mean(xv * xv, axis=-1, keepdims=True) + NORM_EPS)
        z = dyv * g_ref[...]
        dx = r * z - xv * ((r * r * r) * jnp.mean(z * xv, axis=-1, keepdims=True))
        if r_ref is not None:
            dx = dx + r_ref[...]
        dg_ref[...] += jnp.sum(dyv * (xv * r), axis=0, keepdims=True)
        if head_ref is None:
            dx_ref[...] = dx.astype(out_dtype)
        else:
            @pl.when(pl.program_id(0) == 0)
            def _():
                head_ref[...] = dx.astype(out_dtype)

            @pl.when(pl.program_id(0) > 0)
            def _():
                dx_ref[...] = dx.astype(out_dtype)

    in_specs = [pl.BlockSpec((tm, width), lambda i: (i, 0)),
                pl.BlockSpec((tm, width), lambda i: (i, cb)),
                pl.BlockSpec((1, width), lambda i: (0, 0))]
    operands = [dy, x, g]
    if res is not None:
        in_specs.append(pl.BlockSpec((tm, width), lambda i: (i, 0)))
        operands.append(res)
    dg_shape, dg_spec = jax.ShapeDtypeStruct((1, width), F32), pl.BlockSpec((1, width), lambda i: (0, 0))
    if not split_prefix:
        return pl.pallas_call(
            body, name=name, out_shape=(jax.ShapeDtypeStruct((L, width), out_dtype), dg_shape), grid=(L // tm,),
            in_specs=in_specs, out_specs=(pl.BlockSpec((tm, width), lambda i: (i, 0)), dg_spec),
            compiler_params=_params("arbitrary"),
        )(*operands)
    dx_rest, dg, dx_head = pl.pallas_call(
        body, name=name,
        out_shape=(jax.ShapeDtypeStruct((L - PREFIX, width), out_dtype), dg_shape,
                   jax.ShapeDtypeStruct((PREFIX, width), out_dtype)),
        grid=(L // tm,), in_specs=in_specs,
        out_specs=(pl.BlockSpec((tm, width), lambda i: (jnp.maximum(i - 1, 0), 0)), dg_spec,
                   pl.BlockSpec((PREFIX, width), lambda i: (0, 0))),
        compiler_params=_params("arbitrary"),
    )(*operands)
    return (dx_rest, dx_head), dg


def _final_loss_bwd(h2, tgt, g):
    L, D = h2.shape
    tm = PREFIX
    inv_d = 1.0 / D

    def body(h_ref, t_ref, g_ref, dh_ref, loss_ref, dg_ref):
        i = pl.program_id(0)

        @pl.when(i == 0)
        def _():
            loss_ref[...] = jnp.zeros_like(loss_ref)
            dg_ref[...] = jnp.zeros_like(dg_ref)

        xv = h_ref[...]
        r = lax.rsqrt(jnp.mean(xv * xv, axis=-1, keepdims=True) + NORM_EPS)
        xn = xv * r
        y = xn * g_ref[...]
        real = (i >= PREFIX // tm).astype(F32)
        diff = (y - t_ref[...]) * real
        loss_ref[...] += 0.5 * inv_d * jnp.sum(diff * diff)
        dyv = diff * inv_d
        z = dyv * g_ref[...]
        dh_ref[...] = r * z - xv * ((r * r * r) * jnp.mean(z * xv, axis=-1, keepdims=True))
        dg_ref[...] += jnp.sum(dyv * xn, axis=0, keepdims=True)

    shift = PREFIX // tm
    return pl.pallas_call(
        body, name="final_loss_bwd",
        out_shape=(jax.ShapeDtypeStruct((L, D), F32), jax.ShapeDtypeStruct((1, 128), F32),
                   jax.ShapeDtypeStruct((1, D), F32)),
        grid=(L // tm,),
        in_specs=[pl.BlockSpec((tm, D), lambda i: (i, 0)),
                  pl.BlockSpec((tm, D), lambda i: (jnp.maximum(i - shift, 0), 0)),
                  pl.BlockSpec((1, D), lambda i: (0, 0))],
        out_specs=(pl.BlockSpec((tm, D), lambda i: (i, 0)), pl.BlockSpec((1, 128), lambda i: (0, 0)),
                   pl.BlockSpec((1, D), lambda i: (0, 0))),
        compiler_params=_params("arbitrary"),
    )(h2, tgt, g)


def _rot_half(x):
    lane = lax.broadcasted_iota(jnp.int32, x.shape, 1)
    return jnp.where(lane < ROPE // 2, -pltpu.roll(x, 128 - ROPE // 2, 1), pltpu.roll(x, ROPE // 2, 1))


def _rope_fwd(qp, kv, kr, cos, sin, heads):
    L = qp.shape[0]
    tm = _tile(L, (128,))

    def body(q_ref, kv_ref, kr_ref, c_ref, s_ref, qc_ref, kc_ref, v_ref):
        c, s = c_ref[...], s_ref[...]
        krv = kr_ref[...]
        kr_rot = (krv * c + _rot_half(krv) * s).astype(BF16)
        for h in range(heads):
            lo = h * QPAD
            qc_ref[:, lo:lo + HEAD] = (q_ref[:, lo:lo + HEAD].astype(F32) * SOFTMAX_SCALE).astype(BF16)
            qr = q_ref[:, lo + HEAD:lo + QPAD].astype(F32)
            qc_ref[:, lo + HEAD:lo + QPAD] = ((qr * c + _rot_half(qr) * s) * SOFTMAX_SCALE).astype(BF16)
            kc_ref[:, lo:lo + HEAD] = kv_ref[:, lo:lo + HEAD].astype(BF16)
            kc_ref[:, lo + HEAD:lo + QPAD] = kr_rot
            v_ref[:, h * HEAD:(h + 1) * HEAD] = kv_ref[:, lo + HEAD:lo + QPAD].astype(BF16)

    W = heads * QPAD
    row = lambda w: pl.BlockSpec((tm, w), lambda i: (i, 0))
    return pl.pallas_call(
        body, name="rope_fwd",
        out_shape=(jax.ShapeDtypeStruct((L, W), BF16), jax.ShapeDtypeStruct((L, W), BF16),
                   jax.ShapeDtypeStruct((L, heads * HEAD), BF16)),
        grid=(L // tm,),
        in_specs=[row(W), row(W), row(128), row(128), row(128)],
        out_specs=(row(W), row(W), row(heads * HEAD)),
        compiler_params=_params("parallel"),
    )(qp, kv, kr, cos, sin)


def _rope_bwd(dqc, dkc, dv, cos, sin, heads):
    L = dqc.shape[0]
    tm = _tile(L, (128,))

    def body(dq_ref, dk_ref, dv_ref, c_ref, s_ref, dqp_ref, dkv_ref, dkr_ref):
        c, s = c_ref[...], s_ref[...]
        acc = jnp.zeros((tm, 128), F32)
        for h in range(heads):
            lo = h * QPAD
            dqp_ref[:, lo:lo + HEAD] = (dq_ref[:, lo:lo + HEAD] * SOFTMAX_SCALE).astype(BF16)
            d = dq_ref[:, lo + HEAD:lo + QPAD]
            dqp_ref[:, lo + HEAD:lo + QPAD] = ((d * c - _rot_half(d) * s) * SOFTMAX_SCALE).astype(BF16)
            dkv_ref[:, lo:lo + HEAD] = dk_ref[:, lo:lo + HEAD].astype(BF16)
            dkv_ref[:, lo + HEAD:lo + QPAD] = dv_ref[:, h * HEAD:(h + 1) * HEAD].astype(BF16)
            acc = acc + dk_ref[:, lo + HEAD:lo + QPAD]
        dkr_ref[...] = (acc * c - _rot_half(acc) * s).astype(BF16)

    W = heads * QPAD
    row = lambda w: pl.BlockSpec((tm, w), lambda i: (i, 0))
    return pl.pallas_call(
        body, name="rope_bwd",
        out_shape=(jax.ShapeDtypeStruct((L, W), BF16), jax.ShapeDtypeStruct((L, W), BF16),
                   jax.ShapeDtypeStruct((L, 128), BF16)),
        grid=(L // tm,),
        in_specs=[row(W), row(W), row(heads * HEAD), row(128), row(128)],
        out_specs=(row(W), row(W), row(128)),
        compiler_params=_params("parallel"),
    )(dqc, dkc, dv, cos, sin)


def _attn_keep(qi, ki, ta):
    t = qi * ta + lax.broadcasted_iota(jnp.int32, (ta, ta), 0)
    s = ki * ta + lax.broadcasted_iota(jnp.int32, (ta, ta), 1)
    return (s <= t) & ((s >= PAD_LEN) | (s == t))


def _split_refs(refs, sizes):
    out, at = [], 0
    for n in sizes:
        out.append(refs[at:at + n])
        at += n
    return out


def _attn_fwd(qc, kc, v, heads, rider):
    L = qc.shape[0]
    ta = _tile(L, (640, 128))
    nb = L // ta
    pairs = [(i, j) for i in range(nb) for j in range(i + 1)]
    q_of = jnp.asarray([p[0] for p in pairs], jnp.int32)
    k_of = jnp.asarray([p[1] for p in pairs], jnp.int32)
    n_rin, n_rout = len(rider.operands), len(rider.out_shapes)
    per = _tile(heads, (ATTN_FWD_HEADS, 1))
    ng = heads // per

    def body(*refs):
        (q_of_ref, k_of_ref, q_ref, k_ref, v_ref), r_ins, (o_ref, lse_ref), r_outs, (m_sc, l_sc, acc_sc), r_sems = \
            _split_refs(refs, (5, n_rin, 2, n_rout, 3, len(rider.sem_shapes)))
        h, t = pl.program_id(0), pl.program_id(1)
        qi, ki = q_of_ref[t], k_of_ref[t]
        pl.when((h == 0) & (t == 0))(lambda: rider.start(r_ins, r_outs, r_sems))

        @pl.when(ki == 0)
        def _():
            m_sc[...] = jnp.full_like(m_sc, NEG)
            l_sc[...] = jnp.zeros_like(l_sc)
            acc_sc[...] = jnp.zeros_like(acc_sc)

        def step(masked):
            wide = lambda hh: slice(hh * QPAD, (hh + 1) * QPAD)
            lanes = lambda hh: slice(hh * HEAD, (hh + 1) * HEAD)
            ss = [_dot(q_ref[:, wide(hh)], k_ref[:, wide(hh)], NT) for hh in range(per)]
            if masked:
                keep = _attn_keep(qi, ki, ta)
                ss = [jnp.where(keep, s, NEG) for s in ss]
            for hh in range(per):
                m_old = m_sc[hh]
                m_new = jnp.maximum(m_old, jnp.max(ss[hh], axis=-1, keepdims=True))
                p = jnp.exp(ss[hh] - jnp.tile(m_new, (1, ta // HEAD)))
                alpha = jnp.exp(m_old - m_new)
                l_sc[hh] = alpha * l_sc[hh] + jnp.sum(p, axis=-1, keepdims=True)
                acc_sc[hh] = alpha * acc_sc[hh] + _dot(p, v_ref[:, lanes(hh)], NN)
                m_sc[hh] = m_new

        pl.when((ki == qi) | ((ki == 0) & (qi > 0)))(lambda: step(True))
        pl.when((ki > 0) & (ki < qi))(lambda: step(False))

        @pl.when(ki == qi)
        def _():
            for hh in range(per):
                l = l_sc[hh]
                o_ref[:, hh * HEAD:(hh + 1) * HEAD] = (acc_sc[hh] / l).astype(BF16)
                lse_ref[:, hh * HEAD:(hh + 1) * HEAD] = m_sc[hh] + jnp.log(l)

        pl.when((h == ng - 1) & (t == len(pairs) - 1))(lambda: rider.finish(r_ins, r_outs, r_sems))

    qrow = lambda w: pl.BlockSpec((ta, per * w), lambda h, t, q_of, k_of: (q_of[t], h))
    krow = lambda w: pl.BlockSpec((ta, per * w), lambda h, t, q_of, k_of: (k_of[t], h))
    stat = pltpu.VMEM((per, ta, HEAD), F32)
    res = pl.pallas_call(
        body, name="attn_fwd",
        out_shape=(jax.ShapeDtypeStruct((L, heads * HEAD), BF16), jax.ShapeDtypeStruct((L, heads * HEAD), F32),
                   *rider.out_shapes),
        grid_spec=pltpu.PrefetchScalarGridSpec(
            num_scalar_prefetch=2, grid=(ng, len(pairs)),
            in_specs=[qrow(QPAD), krow(QPAD), krow(HEAD)] + [ANY] * n_rin,
            out_specs=(qrow(HEAD), qrow(HEAD), *([ANY] * n_rout)),
            scratch_shapes=[stat, stat, stat] + rider.sem_shapes),
        compiler_params=pltpu.CompilerParams(dimension_semantics=("arbitrary", "arbitrary"),
                                             vmem_limit_bytes=VMEM_LIMIT_BYTES, has_side_effects=True),
    )(q_of, k_of, qc, kc, v, *rider.operands)
    return res[0], res[1], res[2:]


def _attn_delta(o, do, heads):
    L = o.shape[0]
    tm = _tile(L, (128,))

    def body(o_ref, do_ref, d_ref):
        for h in range(heads):
            cols = slice(h * HEAD, (h + 1) * HEAD)
            d = jnp.sum(do_ref[:, cols].astype(F32) * o_ref[:, cols].astype(F32), axis=-1, keepdims=True)
            d_ref[:, cols] = jnp.broadcast_to(d, (tm, HEAD))

    row = pl.BlockSpec((tm, heads * HEAD), lambda i: (i, 0))
    return pl.pallas_call(
        body, name="attn_delta", out_shape=jax.ShapeDtypeStruct((L, heads * HEAD), F32), grid=(L // tm,),
        in_specs=[row, row], out_specs=row, compiler_params=_params("parallel"),
    )(o, do)


def _attn_bwd(qc, kc, v, do, lse, delta, heads, rider):
    L = qc.shape[0]
    ta = _tile(L, (640, 128))
    nb = L // ta
    pairs = [(j, i) for j in range(nb) for i in range(j, nb)]
    k_of = jnp.asarray([p[0] for p in pairs], jnp.int32)
    q_of = jnp.asarray([p[1] for p in pairs], jnp.int32)
    n_rin, n_rout = len(rider.operands), len(rider.out_shapes)
    per = _tile(heads, (ATTN_BWD_HEADS, 1))
    ng = heads // per

    def body(*refs):
        ((k_of_ref, q_of_ref, q_ref, k_ref, v_ref, do_ref, lse_ref, delta_ref), r_ins, (dq_hbm, dk_ref, dv_ref),
         r_outs, (dq_sc, dq_sem), r_sems) = _split_refs(refs, (8, n_rin, 3, n_rout, 2, len(rider.sem_shapes)))
        h, t = pl.program_id(0), pl.program_id(1)
        kj, qi = k_of_ref[t], q_of_ref[t]
        pl.when((h == 0) & (t == 0))(lambda: rider.start(r_ins, r_outs, r_sems))

        @pl.when(t == 0)
        def _():
            dq_sc[...] = jnp.zeros_like(dq_sc)

        @pl.when(qi == kj)
        def _():
            dk_ref[...] = jnp.zeros_like(dk_ref)
            dv_ref[...] = jnp.zeros_like(dv_ref)

        def step(masked):
            wide = lambda hh: slice(hh * QPAD, (hh + 1) * QPAD)
            lanes = lambda hh: slice(hh * HEAD, (hh + 1) * HEAD)
            rep = (1, ta // HEAD)
            rows = pl.ds(pl.multiple_of(qi * ta, ta), ta)
            ss = [_dot(q_ref[:, wide(hh)], k_ref[:, wide(hh)], NT) for hh in range(per)]
            dps = [_dot(do_ref[:, lanes(hh)], v_ref[:, lanes(hh)], NT) for hh in range(per)]
            ps = [jnp.exp(ss[hh] - jnp.tile(lse_ref[:, lanes(hh)], rep)) for hh in range(per)]
            if masked:
                keep = _attn_keep(qi, kj, ta)
                ps = [jnp.where(keep, p, 0.0) for p in ps]
            dss = [ps[hh] * (dps[hh] - jnp.tile(delta_ref[:, lanes(hh)], rep)) for hh in range(per)]
            for hh in range(per):
                dv_ref[:, lanes(hh)] += _dot(ps[hh], do_ref[:, lanes(hh)], TN)
                dk_ref[:, wide(hh)] += _dot(dss[hh], q_ref[:, wide(hh)], TN)
                dq_sc[rows, wide(hh)] += _dot(dss[hh], k_ref[:, wide(hh)], NN)

        pl.when((qi == kj) | ((kj == 0) & (qi > 0)))(lambda: step(True))
        pl.when((kj > 0) & (qi > kj))(lambda: step(False))

        @pl.when(t == len(pairs) - 1)
        def _():
            cols = pl.ds(pl.multiple_of(h * (per * QPAD), per * QPAD), per * QPAD)
            out = pltpu.make_async_copy(dq_sc, dq_hbm.at[:, cols], dq_sem)
            out.start()
            out.wait()

        pl.when((h == ng - 1) & (t == len(pairs) - 1))(lambda: rider.finish(r_ins, r_outs, r_sems))

    qrow = lambda w: pl.BlockSpec((ta, per * w), lambda h, t, k_of, q_of: (q_of[t], h))
    krow = lambda w: pl.BlockSpec((ta, per * w), lambda h, t, k_of, q_of: (k_of[t], h))
    res = pl.pallas_call(
        body, name="attn_bwd",
        out_shape=(jax.ShapeDtypeStruct((L, heads * QPAD), F32), jax.ShapeDtypeStruct((L, heads * QPAD), F32),
                   jax.ShapeDtypeStruct((L, heads * HEAD), F32), *rider.out_shapes),
        grid_spec=pltpu.PrefetchScalarGridSpec(
            num_scalar_prefetch=2, grid=(ng, len(pairs)),
            in_specs=[qrow(QPAD), krow(QPAD), krow(HEAD), qrow(HEAD), qrow(HEAD), qrow(HEAD)] + [ANY] * n_rin,
            out_specs=(ANY, krow(QPAD), krow(HEAD), *([ANY] * n_rout)),
            scratch_shapes=[pltpu.VMEM((L, per * QPAD), F32), pltpu.SemaphoreType.DMA] + rider.sem_shapes),
        compiler_params=pltpu.CompilerParams(dimension_semantics=("arbitrary", "arbitrary"),
                                             vmem_limit_bytes=VMEM_LIMIT_BYTES, has_side_effects=True),
    )(k_of, q_of, qc, kc, v, do, lse, delta, *rider.operands)
    return res[0], res[1], res[2], res[3:]


def _hgrn_constants():
    t = jnp.arange(CHUNK)
    tril = (t[None, :] <= t[:, None]).astype(BF16)
    sel, blk = [], []
    for lv in range(LEVELS):
        hs = 1 << lv
        mid = (t // (2 * hs)) * (2 * hs) + hs - 1
        sel.append((t[None, :] == mid[:, None]).astype(BF16))
        blk.append(((t[:, None] // (2 * hs)) == (t[None, :] // (2 * hs))).astype(F32))
    return tril, jnp.concatenate(sel, axis=0), jnp.concatenate(blk, axis=0)


def _hgrn_decay_grad_constants():
    r = jnp.arange(CHUNK)[:, None]
    c = jnp.arange(CHUNK)[None, :]
    mats = []
    for lv in range(LEVELS):
        same = (r >> (lv + 1)) == (c >> (lv + 1))
        second = ((r >> lv) & 1) == 1
        mats.append(same & jnp.where(second, c >= r, c < r))
    mats += [c >= r, c < r]
    return jnp.concatenate(mats, axis=1).astype(BF16)


def _hgrn_gates(hq_ref, hf_ref, hi_ref, lb_ref, cols, row0):
    rows = row0 + lax.broadcasted_iota(jnp.int32, (CHUNK, 1), 0)
    valid = rows >= PAD_LEN
    lb = 1.0 / (1.0 + jnp.exp(lb_ref[1:2, cols] - lb_ref[0:1, cols]))
    hq = hq_ref[:, cols].astype(F32)
    sq = _sigmoid(hq)
    sg = _sigmoid(hf_ref[:, cols].astype(F32))
    f = lb + (1.0 - lb) * sg
    g = jnp.where(valid, jnp.log(f), 0.0)
    k = jnp.where(valid, 1.0 - f, 0.0)
    return dict(q=hq * sq, sq=sq, hq=hq, k=k, v=hi_ref[:, cols].astype(F32), g=g, f=f, sg=sg, lb=lb, valid=valid)


def _hgrn_prefix(tril_ref, sel_ref, gs):
    n = len(gs)
    b_all = _dot_exact_rhs(tril_ref[...], jnp.concatenate(gs, axis=1) if n > 1 else gs[0], NN)
    bm_all = _dot_exact_rhs(sel_ref[...], b_all, NN)
    cut = lambda a, i: a[:, i * HEAD:(i + 1) * HEAD]
    return [cut(b_all, i) for i in range(n)], [cut(bm_all, i) for i in range(n)]


def _hgrn_levels(q, k, b, bm_all, blk_ref):
    t = lax.broadcasted_iota(jnp.int32, (CHUNK, 1), 0)
    out = []
    for lv in range(LEVELS):
        bm = bm_all[lv * CHUNK:(lv + 1) * CHUNK, :]
        second = ((t >> lv) & 1) == 1
        eq = jnp.where(second, jnp.exp(jnp.minimum(b - bm, 0.0)), 0.0)
        ek = jnp.where(second, 0.0, jnp.exp(jnp.minimum(bm - b, 0.0)))
        same = blk_ref[lv * CHUNK:(lv + 1) * CHUNK, :]
        out.append((eq, ek, (q * eq).astype(BF16), (k * ek).astype(BF16), same))
    return out


def _hgrn_intra(qs, ks, levels):
    tt = lax.broadcasted_iota(jnp.int32, (CHUNK, CHUNK), 0)
    ss = lax.broadcasted_iota(jnp.int32, (CHUNK, CHUNK), 1)
    ps = [jnp.where(tt == ss, jnp.sum(q * k, axis=-1, keepdims=True), 0.0) for q, k in zip(qs, ks)]
    for lv in range(LEVELS):
        for i, lvl in enumerate(levels):
            _, _, ql, kl, same = lvl[lv]
            ps[i] = ps[i] + _dot(ql, kl, NT) * same
    return ps


def _hgrn_fwd(hp, lb_raw, g_norm, nh):
    L = hp.shape[0]
    D = nh * HEAD
    nc = L // CHUNK
    per = _tile(nh, (HGRN_FWD_HEADS, 2, 1))
    ng = nh // per
    tril, sel, blk = _hgrn_constants()

    def body(hq_ref, hf_ref, hi_ref, hg_ref, lb_ref, gn_ref, tril_ref, sel_ref, blk_ref,
             oh_ref, orec_ref, shist_ref, s_sc, b_sc):
        c = pl.program_id(1)

        @pl.when(c == 0)
        def _():
            s_sc[...] = jnp.zeros_like(s_sc)

        heads_here = range(per)
        lanes = [slice(hh * HEAD, (hh + 1) * HEAD) for hh in heads_here]
        ws = [_hgrn_gates(hq_ref, hf_ref, hi_ref, lb_ref, lanes[hh], c * CHUNK) for hh in heads_here]
        qs, ks, vs = [w["q"] for w in ws], [w["k"] for w in ws], [w["v"] for w in ws]
        bs, bms = _hgrn_prefix(tril_ref, sel_ref, [w["g"] for w in ws])
        for hh in heads_here:
            b_sc[hh] = bs[hh]
        b_lasts = [b_sc[hh, CHUNK - 1:CHUNK, :] for hh in heads_here]
        ps = _hgrn_intra(qs, ks, [_hgrn_levels(qs[hh], ks[hh], bs[hh], bms[hh], blk_ref) for hh in heads_here])
        s_ins = [s_sc[hh] for hh in heads_here]
        os_ = [_dot(ps[hh], vs[hh], NN) + _dot(qs[hh] * jnp.exp(bs[hh]), s_ins[hh], NT) for hh in heads_here]
        for hh in heads_here:
            shist_ref[0, hh] = s_ins[hh]
            s_sc[hh] = (jnp.exp(b_lasts[hh]) * s_ins[hh]
                        + _dot(vs[hh], ks[hh] * jnp.exp(b_lasts[hh] - bs[hh]), TN))
        for hh in heads_here:
            o = os_[hh]
            orec_ref[:, lanes[hh]] = o
            rn = lax.rsqrt(jnp.mean(o * o, axis=-1, keepdims=True) + NORM_EPS)
            hg = hg_ref[:, lanes[hh]].astype(F32)
            oh_ref[:, lanes[hh]] = (((o * rn) * gn_ref[...]) * (hg * _sigmoid(hg))).astype(BF16)

    col = lambda grp: pl.BlockSpec((CHUNK, per * HEAD), lambda h, c: (c, grp * ng + h))
    const = lambda shape: pl.BlockSpec(shape, lambda h, c: (0, 0))
    return pl.pallas_call(
        body, name="hgrn_fwd",
        out_shape=(jax.ShapeDtypeStruct((L, D), BF16), jax.ShapeDtypeStruct((L, D), F32),
                   jax.ShapeDtypeStruct((nc, nh, HEAD, HEAD), F32)),
        grid=(ng, nc),
        in_specs=[col(0), col(1), col(2), col(3),
                  pl.BlockSpec((2, per * HEAD), lambda h, c: (0, h)), const((1, HEAD)),
                  const((CHUNK, CHUNK)), const((LEVELS * CHUNK, CHUNK)), const((LEVELS * CHUNK, CHUNK))],
        out_specs=(pl.BlockSpec((CHUNK, per * HEAD), lambda h, c: (c, h)),
                   pl.BlockSpec((CHUNK, per * HEAD), lambda h, c: (c, h)),
                   pl.BlockSpec((1, per, HEAD, HEAD), lambda h, c: (c, h, 0, 0))),
        scratch_shapes=[pltpu.VMEM((per, HEAD, HEAD), F32), pltpu.VMEM((per, CHUNK, HEAD), F32)],
        compiler_params=_params("parallel", "arbitrary"),
    )(hp, hp, hp, hp, lb_raw, g_norm, tril, sel, blk)


def _hgrn_bwd(hp, lb_raw, g_norm, do_h, o_rec, s_hist, nh):
    L = hp.shape[0]
    D = nh * HEAD
    nc = L // CHUNK
    per = _tile(nh, (HGRN_BWD_HEADS, 1))
    ng = nh // per
    tril, sel, blk = _hgrn_constants()
    tdec = _hgrn_decay_grad_constants()

    def body(hq_ref, hf_ref, hi_ref, hg_ref, lb_ref, gn_ref, tril_ref, sel_ref, blk_ref, tdec_ref,
             do_ref, orec_ref, shist_ref, dhq_ref, dhf_ref, dhi_ref, dhg_ref, dgn_ref, dlb_ref, ds_sc, b_sc):
        ci = pl.program_id(1)
        c = nc - 1 - ci

        @pl.when(ci == 0)
        def _():
            ds_sc[...] = jnp.zeros_like(ds_sc)
            dgn_ref[...] = jnp.zeros_like(dgn_ref)
            dlb_ref[...] = jnp.zeros_like(dlb_ref)

        heads_here = range(per)
        lanes = [slice(hh * HEAD, (hh + 1) * HEAD) for hh in heads_here]
        ws = [_hgrn_gates(hq_ref, hf_ref, hi_ref, lb_ref, lanes[hh], c * CHUNK) for hh in heads_here]
        qs, ks, vs = [w["q"] for w in ws], [w["k"] for w in ws], [w["v"] for w in ws]
        bs, bms = _hgrn_prefix(tril_ref, sel_ref, [w["g"] for w in ws])
        for hh in heads_here:
            b_sc[hh] = bs[hh]
        b_lasts = [b_sc[hh, CHUNK - 1:CHUNK, :] for hh in heads_here]
        levels = [_hgrn_levels(qs[hh], ks[hh], bs[hh], bms[hh], blk_ref) for hh in heads_here]
        ps = _hgrn_intra(qs, ks, levels)
        s_ins = [shist_ref[0, hh] for hh in heads_here]
        ds_outs = [ds_sc[hh] for hh in heads_here]
        ebs = [jnp.exp(b) for b in bs]
        etails = [jnp.exp(b_lasts[hh] - bs[hh]) for hh in heads_here]
        decays = [jnp.exp(bl) for bl in b_lasts]

        dos = []
        for hh in heads_here:
            o = orec_ref[:, lanes[hh]]
            hg = hg_ref[:, lanes[hh]].astype(F32)
            sgg = _sigmoid(hg)
            rn = lax.rsqrt(jnp.mean(o * o, axis=-1, keepdims=True) + NORM_EPS)
            on = o * rn
            doh = do_ref[:, lanes[hh]]
            dy = doh * (hg * sgg)
            dhg_ref[:, lanes[hh]] = (doh * (on * gn_ref[...]) * (sgg * (1.0 + hg * (1.0 - sgg)))).astype(BF16)
            dgn_ref[hh] += jnp.broadcast_to(jnp.sum(dy * on, axis=0, keepdims=True), (8, HEAD))
            z = dy * gn_ref[...]
            dos.append(rn * z - o * ((rn * rn * rn) * jnp.mean(z * o, axis=-1, keepdims=True)))

        tt = lax.broadcasted_iota(jnp.int32, (CHUNK, CHUNK), 0)
        ss = lax.broadcasted_iota(jnp.int32, (CHUNK, CHUNK), 1)
        dps = [jnp.where(ss <= tt, _dot(dos[hh], vs[hh], NT), 0.0) for hh in heads_here]
        dvs = [_dot(ps[hh], dos[hh], TN) + _dot(ks[hh] * etails[hh], ds_outs[hh], NT) for hh in heads_here]
        dq_states = [ebs[hh] * _dot(dos[hh], s_ins[hh], NN) for hh in heads_here]
        dk_states = [etails[hh] * _dot(vs[hh], ds_outs[hh], NN) for hh in heads_here]
        dpds = [jnp.sum(jnp.where(tt == ss, dp, 0.0), axis=-1, keepdims=True) for dp in dps]
        dqs = [dpds[hh] * ks[hh] + dq_states[hh] for hh in heads_here]
        dks = [dpds[hh] * qs[hh] + dk_states[hh] for hh in heads_here]
        pair_terms = [[] for _ in heads_here]
        for lv in range(LEVELS):
            for hh in heads_here:
                eq, ek, ql, kl, same = levels[hh][lv]
                dpl = dps[hh] * same
                dq_l = eq * _dot(dpl, kl, NN)
                dk_l = ek * _dot(dpl, ql, TN)
                dqs[hh] = dqs[hh] + dq_l
                dks[hh] = dks[hh] + dk_l
                pair_terms[hh].append(qs[hh] * dq_l + ks[hh] * dk_l)
        for hh in heads_here:
            pair_terms[hh] += [qs[hh] * dq_states[hh], ks[hh] * dk_states[hh]]
            ds_sc[hh] = decays[hh] * ds_outs[hh] + _dot(dos[hh], qs[hh] * ebs[hh], TN)
        stacked = [jnp.concatenate(terms, axis=0) for terms in pair_terms]
        dg_all = _dot_exact_rhs(tdec_ref[...], jnp.concatenate(stacked, axis=1) if per > 1 else stacked[0], NN)

        for hh in heads_here:
            w = ws[hh]
            f, sg, lb, sq, hq = w["f"], w["sg"], w["lb"], w["sq"], w["hq"]
            through = jnp.sum((decays[hh] * s_ins[hh]) * ds_outs[hh], axis=0, keepdims=True)
            dg = dg_all[:, lanes[hh]] + through
            df = jnp.where(w["valid"], dg / f - dks[hh], 0.0)
            dhf_ref[:, lanes[hh]] = (df * (1.0 - lb) * sg * (1.0 - sg)).astype(BF16)
            dlb_ref[hh] += jnp.broadcast_to(jnp.sum(df * (1.0 - sg), axis=0, keepdims=True), (8, HEAD))
            dhq_ref[:, lanes[hh]] = (dqs[hh] * (sq * (1.0 + hq * (1.0 - sq)))).astype(BF16)
            dhi_ref[:, lanes[hh]] = dvs[hh].astype(BF16)

    col = lambda grp: pl.BlockSpec((CHUNK, per * HEAD), lambda h, c: (nc - 1 - c, grp * ng + h))
    const = lambda shape: pl.BlockSpec(shape, lambda h, c: (0, 0))
    tile = pl.BlockSpec((CHUNK, per * HEAD), lambda h, c: (nc - 1 - c, h))
    part = pl.BlockSpec((per, 8, HEAD), lambda h, c: (h, 0, 0))
    return pl.pallas_call(
        body, name="hgrn_bwd",
        out_shape=tuple([jax.ShapeDtypeStruct((L, D), BF16)] * 4 + [jax.ShapeDtypeStruct((nh, 8, HEAD), F32)] * 2),
        grid=(ng, nc),
        in_specs=[col(0), col(1), col(2), col(3),
                  pl.BlockSpec((2, per * HEAD), lambda h, c: (0, h)), const((1, HEAD)),
                  const((CHUNK, CHUNK)), const((LEVELS * CHUNK, CHUNK)), const((LEVELS * CHUNK, CHUNK)),
                  const((CHUNK, (LEVELS + 2) * CHUNK)),
                  tile, tile, pl.BlockSpec((1, per, HEAD, HEAD), lambda h, c: (nc - 1 - c, h, 0, 0))],
        out_specs=(tile, tile, tile, tile, part, part),
        scratch_shapes=[pltpu.VMEM((per, HEAD, HEAD), F32), pltpu.VMEM((per, CHUNK, HEAD), F32)],
        compiler_params=_params("parallel", "arbitrary"),
    )(hp, hp, hp, hp, lb_raw, g_norm, tril, sel, blk, tdec, do_h, o_rec, s_hist)


def _merge_fwd(a, bm, gates):
    L, D = a.shape
    tm = _tile(L, (128,))

    def body(a_ref, b_ref, g_ref, o_ref):
        sa, sb = _sigmoid(g_ref[:, :D].astype(F32)), _sigmoid(g_ref[:, D:].astype(F32))
        o_ref[...] = (sa * a_ref[...] + sb * b_ref[...]).astype(BF16)

    row = lambda w: pl.BlockSpec((tm, w), lambda i: (i, 0))
    return pl.pallas_call(
        body, name="merge_fwd", out_shape=jax.ShapeDtypeStruct((L, D), BF16), grid=(L // tm,),
        in_specs=[row(D), row(D), row(2 * D)], out_specs=row(D), compiler_params=_params("parallel"),
    )(a, bm, gates)


def _merge_bwd(dm, a, bm, gates):
    L, D = a.shape
    tm = _tile(L, (128,))

    def body(dm_ref, a_ref, b_ref, g_ref, da_ref, db_ref, dg_ref):
        d = dm_ref[...]
        sa, sb = _sigmoid(g_ref[:, :D].astype(F32)), _sigmoid(g_ref[:, D:].astype(F32))
        da_ref[...] = (d * sa).astype(BF16)
        db_ref[...] = (d * sb).astype(BF16)
        dg_ref[:, :D] = (d * a_ref[...] * sa * (1.0 - sa)).astype(BF16)
        dg_ref[:, D:] = (d * b_ref[...] * sb * (1.0 - sb)).astype(BF16)

    row = lambda w: pl.BlockSpec((tm, w), lambda i: (i, 0))
    return pl.pallas_call(
        body, name="merge_bwd",
        out_shape=(jax.ShapeDtypeStruct((L, D), BF16), jax.ShapeDtypeStruct((L, D), BF16),
                   jax.ShapeDtypeStruct((L, 2 * D), BF16)),
        grid=(L // tm,),
        in_specs=[row(D), row(D), row(D), row(2 * D)], out_specs=(row(D), row(D), row(2 * D)),
        compiler_params=_params("parallel"),
    )(dm, a, bm, gates)


def _conv_taps(g_ref, halo_ref, i, tm):
    rows = i * tm + lax.broadcasted_iota(jnp.int32, (tm, 1), 0)
    g0 = jnp.where(rows >= PAD_LEN, g_ref[...].astype(F32), 0.0)
    sub = lax.broadcasted_iota(jnp.int32, (HALO, 1), 0)
    hrow = i * tm - HALO + sub
    halo = jnp.where(hrow >= PAD_LEN, halo_ref[...].astype(F32), 0.0)
    r = lax.broadcasted_iota(jnp.int32, (tm, 1), 0)
    h7 = jnp.sum(jnp.where(sub == HALO - 1, halo, 0.0), axis=0, keepdims=True)
    h6 = jnp.sum(jnp.where(sub == HALO - 2, halo, 0.0), axis=0, keepdims=True)
    g1 = jnp.where(r == 0, h7, pltpu.roll(g0, 1, 0))
    g2 = jnp.where(r == 0, h6, jnp.where(r == 1, h7, pltpu.roll(g0, 2, 0)))
    return g0, g1, g2


def _conv_fwd(gu, cw, cb):
    L, F2 = gu.shape
    F = F2 // 2
    tm = _tile(L, ROW_TILES)
    tn = _tile(F, (512, 256, 128))
    nj = F // tn

    def body(g_ref, halo_ref, u_ref, cw_ref, cb_ref, o_ref):
        g0, g1, g2 = _conv_taps(g_ref, halo_ref, pl.program_id(0), tm)
        conv = cw_ref[0:1, :] * g2 + cw_ref[1:2, :] * g1 + cw_ref[2:3, :] * g0 + cb_ref[...]
        o_ref[...] = (conv * _sigmoid(conv) * u_ref[...].astype(F32)).astype(BF16)

    return pl.pallas_call(
        body, name="conv_fwd", out_shape=jax.ShapeDtypeStruct((L, F), BF16), grid=(L // tm, nj),
        in_specs=[pl.BlockSpec((tm, tn), lambda i, j: (i, j)),
                  pl.BlockSpec((HALO, tn), lambda i, j: (jnp.maximum(i * (tm // HALO) - 1, 0), j)),
                  pl.BlockSpec((tm, tn), lambda i, j: (i, j + nj)),
                  pl.BlockSpec((3, tn), lambda i, j: (0, j)),
                  pl.BlockSpec((1, tn), lambda i, j: (0, j))],
        out_specs=pl.BlockSpec((tm, tn), lambda i, j: (i, j)),
        compiler_params=_params("parallel", "parallel"),
    )(gu, gu, gu, cw, cb)


def _conv_bwd_a(da, gu, cw, cb):
    L, F2 = gu.shape
    F = F2 // 2
    tm = _tile(L, ROW_TILES)
    tn = _tile(F, (512, 256, 128))
    nj = F // tn

    def body(da_ref, g_ref, halo_ref, u_ref, cw_ref, cb_ref, dc_ref, du_ref, dcb_ref, dcw_ref):
        i = pl.program_id(1)

        @pl.when(i == 0)
        def _():
            dcb_ref[...] = jnp.zeros_like(dcb_ref)
            dcw_ref[...] = jnp.zeros_like(dcw_ref)

        g0, g1, g2 = _conv_taps(g_ref, halo_ref, i, tm)
        conv = cw_ref[0:1, :] * g2 + cw_ref[1:2, :] * g1 + cw_ref[2:3, :] * g0 + cb_ref[...]
        sc = _sigmoid(conv)
        dav = da_ref[...]
        du_ref[...] = (dav * (conv * sc)).astype(BF16)
        dconv = dav * u_ref[...].astype(F32) * (sc * (1.0 + conv * (1.0 - sc)))
        dc_ref[...] = dconv
        dcb_ref[...] += jnp.sum(dconv, axis=0, keepdims=True)
        dcw_ref[0:1, :] += jnp.sum(dconv * g2, axis=0, keepdims=True)
        dcw_ref[1:2, :] += jnp.sum(dconv * g1, axis=0, keepdims=True)
        dcw_ref[2:3, :] += jnp.sum(dconv * g0, axis=0, keepdims=True)

    return pl.pallas_call(
        body, name="conv_bwd_a",
        out_shape=(jax.ShapeDtypeStruct((L, F), F32), jax.ShapeDtypeStruct((L, 2 * F), BF16),
                   jax.ShapeDtypeStruct((1, F), F32), jax.ShapeDtypeStruct((8, F), F32)),
        grid=(nj, L // tm),
        in_specs=[pl.BlockSpec((tm, tn), lambda j, i: (i, j)),
                  pl.BlockSpec((tm, tn), lambda j, i: (i, j)),
                  pl.BlockSpec((HALO, tn), lambda j, i: (jnp.maximum(i * (tm // HALO) - 1, 0), j)),
                  pl.BlockSpec((tm, tn), lambda j, i: (i, j + nj)),
                  pl.BlockSpec((3, tn), lambda j, i: (0, j)),
                  pl.BlockSpec((1, tn), lambda j, i: (0, j))],
        out_specs=(pl.BlockSpec((tm, tn), lambda j, i: (i, j)), pl.BlockSpec((tm, tn), lambda j, i: (i, j + nj)),
                   pl.BlockSpec((1, tn), lambda j, i: (0, j)), pl.BlockSpec((8, tn), lambda j, i: (0, j))),
        compiler_params=_params("parallel", "arbitrary"),
    )(da, gu, gu, gu, cw, cb)


def _conv_bwd_b(dconv, cw, dgu):
    L, F = dconv.shape
    tm = _tile(L, ROW_TILES)
    tn = _tile(F, (512, 256, 128))
    nblk8 = L // 8
    ni = L // tm

    def body(dc_ref, nxt_ref, cw_ref, dgu_ref, o_ref):
        i = pl.program_id(0)
        dc = dc_ref[...]
        nxt = jnp.where(i < ni - 1, nxt_ref[...], 0.0)
        sub = lax.broadcasted_iota(jnp.int32, (8, 1), 0)
        n0 = jnp.sum(jnp.where(sub == 0, nxt, 0.0), axis=0, keepdims=True)
        n1 = jnp.sum(jnp.where(sub == 1, nxt, 0.0), axis=0, keepdims=True)
        r = lax.broadcasted_iota(jnp.int32, (tm, 1), 0)
        d1 = jnp.where(r == tm - 1, n0, pltpu.roll(dc, tm - 1, 0))
        d2 = jnp.where(r == tm - 2, n0, jnp.where(r == tm - 1, n1, pltpu.roll(dc, tm - 2, 0)))
        dg = cw_ref[2:3, :] * dc + cw_ref[1:2, :] * d1 + cw_ref[0:1, :] * d2
        rows = i * tm + r
        o_ref[...] = jnp.where(rows >= PAD_LEN, dg, 0.0).astype(BF16)

    return pl.pallas_call(
        body, name="conv_bwd_b", out_shape=jax.ShapeDtypeStruct(dgu.shape, BF16), grid=(ni, F // tn),
        in_specs=[pl.BlockSpec((tm, tn), lambda i, j: (i, j)),
                  pl.BlockSpec((8, tn), lambda i, j: (jnp.minimum((i + 1) * (tm // 8), nblk8 - 1), j)),
                  pl.BlockSpec((3, tn), lambda i, j: (0, j)), ANY],
        out_specs=pl.BlockSpec((tm, tn), lambda i, j: (i, j)),
        input_output_aliases={3: 0},
        compiler_params=_params("parallel", "parallel"),
    )(dconv, dconv, cw, dgu)


ANY = pl.BlockSpec(memory_space=pl.ANY)


def _coords():
    return lax.axis_index("x"), lax.axis_index("y"), lax.axis_index("c")


def _flip(v, bit):
    return 1 - v if bit else v


CHIPS = [(1, 0), (0, 1), (1, 1)]
PEERS = [(dx, dy, dc) for dx in (0, 1) for dy in (0, 1) for dc in (0, 1)][1:]


class _Rider:
    def __init__(self, operands, out_shapes, sem_shapes, start, finish):
        self.operands, self.out_shapes, self.sem_shapes = list(operands), list(out_shapes), list(sem_shapes)
        self.start, self.finish = start, finish


def _run_rider(rider, name):
    n_in, n_out = len(rider.operands), len(rider.out_shapes)

    def body(*refs):
        ins, outs, sems = refs[:n_in], refs[n_in:n_in + n_out], refs[n_in + n_out:]
        rider.start(ins, outs, sems)
        rider.finish(ins, outs, sems)

    return pl.pallas_call(
        body, name=name, out_shape=tuple(rider.out_shapes),
        in_specs=[ANY] * n_in, out_specs=tuple([ANY] * n_out), scratch_shapes=rider.sem_shapes,
        compiler_params=pltpu.CompilerParams(has_side_effects=True),
    )(*rider.operands)


def _gather_rider(big, small):
    nbig, n = len(big), len(big) + len(small)
    arrays = list(big) + list(small)

    def plan(ins, outs, sems):
        ici_send, ici_recv, d2d_send, d2d_recv, local_sems = sems
        x, y, c = _coords()
        mine = 2 * x + y

        def half(w, h):
            r2 = arrays[w].shape[0] // 2
            return pl.ds(h * r2, r2)

        def ici(w, j, landing):
            px, py = _flip(x, CHIPS[j][0]), _flip(y, CHIPS[j][1])
            slot = 2 * px + py if landing else mine
            if w < nbig:
                src, dst = ins[w].at[half(w, c)], outs[w].at[slot, half(w, c)]
            else:
                src, dst = ins[w], outs[w].at[slot]
            return pltpu.make_async_remote_copy(
                src_ref=src, dst_ref=dst, send_sem=ici_send.at[w * 3 + j], recv_sem=ici_recv.at[w * 3 + j],
                device_id=(px, py, c), device_id_type=MESH)

        def d2d(w, j, landing):
            px, py = _flip(x, CHIPS[j][0]), _flip(y, CHIPS[j][1])
            mine_rows = outs[w].at[2 * px + py, half(w, c)]
            dst = outs[w].at[2 * px + py, half(w, 1 - c)] if landing else mine_rows
            return pltpu.make_async_remote_copy(
                src_ref=mine_rows, dst_ref=dst, send_sem=d2d_send.at[w * 3 + j], recv_sem=d2d_recv.at[w * 3 + j],
                device_id=(x, y, 1 - c), device_id_type=MESH)

        local = [pltpu.make_async_copy(ins[w], outs[w].at[mine], local_sems.at[w]) for w in range(n)]
        return ici, d2d, local

    def start(ins, outs, sems):
        ici, _, local = plan(ins, outs, sems)
        for cp in local:
            cp.start()
        for w in range(n):
            for j in range(3):
                ici(w, j, False).start()

    def finish(ins, outs, sems):
        ici, d2d, local = plan(ins, outs, sems)
        for w in range(n):
            for j in range(3):
                ici(w, j, True).wait_recv()
                if w < nbig:
                    d2d(w, j, False).start()
        for w in range(nbig):
            for j in range(3):
                d2d(w, j, True).wait_recv()
        for w in range(n):
            for j in range(3):
                ici(w, j, False).wait_send()
                if w < nbig:
                    d2d(w, j, False).wait_send()
        for cp in local:
            cp.wait()

    return _Rider(
        arrays, [jax.ShapeDtypeStruct((4,) + s.shape, s.dtype) for s in arrays],
        [pltpu.SemaphoreType.DMA((3 * n,)), pltpu.SemaphoreType.DMA((3 * n,)),
         pltpu.SemaphoreType.DMA((max(3 * nbig, 1),)), pltpu.SemaphoreType.DMA((max(3 * nbig, 1),)),
         pltpu.SemaphoreType.DMA((n,))],
        start, finish)


def _to_sibling(arrays, name):
    n = len(arrays)

    def body(*refs):
        ins, outs = refs[:n], refs[n:2 * n]
        send_sems, recv_sems = refs[2 * n:]
        x, y, c = _coords()

        def copy(w):
            return pltpu.make_async_remote_copy(
                src_ref=ins[w], dst_ref=outs[w], send_sem=send_sems.at[w], recv_sem=recv_sems.at[w],
                device_id=(x, y, 1 - c), device_id_type=MESH)

        for w in range(n):
            copy(w).start()
        for w in range(n):
            copy(w).wait_recv()
            copy(w).wait_send()

    return pl.pallas_call(
        body, name=name,
        out_shape=tuple(jax.ShapeDtypeStruct(a.shape, a.dtype) for a in arrays),
        in_specs=[ANY] * n, out_specs=tuple([ANY] * n),
        scratch_shapes=[pltpu.SemaphoreType.DMA((n,)), pltpu.SemaphoreType.DMA((n,))],
        compiler_params=pltpu.CompilerParams(has_side_effects=True),
    )(*arrays)


def _pair_sum(a, b, name):
    _, r, c = a.shape
    tr = _tile(r, (64, 32, 16))

    def body(a_ref, b_ref, o_ref):
        o_ref[...] = (a_ref[...].astype(F32) + b_ref[...].astype(F32)).astype(BF16)

    blk = pl.BlockSpec((4, tr, c), lambda i: (0, i, 0))
    return pl.pallas_call(
        body, name=name, out_shape=jax.ShapeDtypeStruct(a.shape, BF16), grid=(r // tr,),
        in_specs=[blk, blk], out_specs=blk, compiler_params=_params("parallel"),
    )(a, b)


def _scatter_rider(parts):
    n = len(parts)

    def plan(ins, outs, sems):
        send_sems, recv_sems, local_sems = sems
        x, y, c = _coords()
        mine = 2 * x + y

        def ici(w, j, landing):
            px, py = _flip(x, CHIPS[j][0]), _flip(y, CHIPS[j][1])
            return pltpu.make_async_remote_copy(
                src_ref=ins[w].at[2 * px + py], dst_ref=outs[w].at[2 * px + py if landing else mine],
                send_sem=send_sems.at[w * 3 + j], recv_sem=recv_sems.at[w * 3 + j],
                device_id=(px, py, c), device_id_type=MESH)

        local = [pltpu.make_async_copy(ins[w].at[mine], outs[w].at[mine], local_sems.at[w]) for w in range(n)]
        return ici, local

    def start(ins, outs, sems):
        ici, local = plan(ins, outs, sems)
        for cp in local:
            cp.start()
        for w in range(n):
            for j in range(3):
                ici(w, j, False).start()

    def finish(ins, outs, sems):
        ici, local = plan(ins, outs, sems)
        for w in range(n):
            for j in range(3):
                ici(w, j, True).wait_recv()
                ici(w, j, False).wait_send()
        for cp in local:
            cp.wait()

    return _Rider(
        parts, [jax.ShapeDtypeStruct(p.shape, p.dtype) for p in parts],
        [pltpu.SemaphoreType.DMA((3 * n,)), pltpu.SemaphoreType.DMA((3 * n,)), pltpu.SemaphoreType.DMA((n,))],
        start, finish)


def _sum4(recv, name):
    _, r, c = recv.shape
    tr = _tile(r, (64, 32, 16))

    def body(in_ref, o_ref):
        o_ref[...] = ((in_ref[0].astype(F32) + in_ref[1].astype(F32)) + in_ref[2].astype(F32)) + in_ref[3].astype(F32)

    return pl.pallas_call(
        body, name=name, out_shape=jax.ShapeDtypeStruct((r, c), F32), grid=(r // tr,),
        in_specs=[pl.BlockSpec((4, tr, c), lambda i: (0, i, 0))],
        out_specs=pl.BlockSpec((tr, c), lambda i: (i, 0)),
        compiler_params=_params("parallel"),
    )(recv)


def _allreduce_small(packed):
    R = packed.shape[0]

    def body(in_ref, o_ref, buf, send_sems, recv_sems):
        x, y, c = _coords()
        me = 4 * x + 2 * y + c
        buf[me] = in_ref[...]
        for j, (dx, dy, dc) in enumerate(PEERS):
            px, py, pc = _flip(x, dx), _flip(y, dy), _flip(c, dc)
            pltpu.make_async_remote_copy(
                src_ref=in_ref, dst_ref=buf.at[me], send_sem=send_sems.at[j], recv_sem=recv_sems.at[j],
                device_id=(px, py, pc), device_id_type=MESH).start()
        for j, (dx, dy, dc) in enumerate(PEERS):
            px, py, pc = _flip(x, dx), _flip(y, dy), _flip(c, dc)
            rc = pltpu.make_async_remote_copy(
                src_ref=in_ref, dst_ref=buf.at[4 * px + 2 * py + pc], send_sem=send_sems.at[j],
                recv_sem=recv_sems.at[j], device_id=(px, py, pc), device_id_type=MESH)
            rc.wait_recv()
            rc.wait_send()
        acc = buf[0]
        for s in range(1, 8):
            acc = acc + buf[s]
        o_ref[...] = acc

    return pl.pallas_call(
        body, name="allreduce_small", out_shape=jax.ShapeDtypeStruct((R, 128), F32),
        in_specs=[pl.BlockSpec(memory_space=pltpu.VMEM)], out_specs=pl.BlockSpec(memory_space=pltpu.VMEM),
        scratch_shapes=[pltpu.VMEM((8, R, 128), F32), pltpu.SemaphoreType.DMA((7,)), pltpu.SemaphoreType.DMA((7,))],
        compiler_params=pltpu.CompilerParams(has_side_effects=True, vmem_limit_bytes=VMEM_LIMIT_BYTES),
    )(packed)


def _adamw_math(w, g, m, v):
    m = ADAM_B1 * m + (1.0 - ADAM_B1) * g
    v = ADAM_B2 * v + (1.0 - ADAM_B2) * (g * g)
    m_hat = m / (1.0 - ADAM_B1 ** ADAM_STEP)
    v_hat = v / (1.0 - ADAM_B2 ** ADAM_STEP)
    delta = -ADAM_LR * (m_hat / (jnp.sqrt(v_hat) + ADAM_EPS) + ADAM_WD * w)
    return delta, m, v


def _adamw_big(mine, other, w, m, v, name):
    R, C = w.shape
    tr = _tile(R // 2, (128, 64, 32, 16, 8))
    nb = (R // 2) // tr

    def body(mine_ref, other_ref, w_ref, m_ref, v_ref, g_ref, d_ref, mo_ref, vo_ref):
        is_mine = (pl.program_id(0) // nb) == lax.axis_index("c")
        g = jnp.where(is_mine, mine_ref[...], other_ref[...])
        d, mn, vn = _adamw_math(w_ref[...], g, m_ref[...], v_ref[...])
        g_ref[...] = g
        d_ref[...] = d
        mo_ref[...] = mn
        vo_ref[...] = vn

    blk = pl.BlockSpec((tr, C), lambda i: (i, 0))
    half = pl.BlockSpec((tr, C), lambda i: (i % nb, 0))
    sds = jax.ShapeDtypeStruct((R, C), F32)
    return pl.pallas_call(
        body, name=name, out_shape=(sds, sds, sds, sds), grid=(2 * nb,),
        in_specs=[half, half, blk, blk, blk], out_specs=(blk, blk, blk, blk), compiler_params=_params("parallel"),
    )(mine, other, w, m, v)


def _adamw_small(items, lb_raw, dlb):
    n = len(items)
    lb_w, lb_m, lb_v = lb_raw

    def body(*refs):
        ins = refs[:4 * n]
        dlb_ref, lw_ref, lm_ref, lv_ref = refs[4 * n:4 * n + 4]
        outs = refs[4 * n + 4:]
        for t in range(n):
            g_ref, w_ref, m_ref, v_ref = ins[4 * t:4 * t + 4]
            d, mn, vn = _adamw_math(w_ref[...], g_ref[...], m_ref[...], v_ref[...])
            outs[3 * t][...] = d
            outs[3 * t + 1][...] = mn
            outs[3 * t + 2][...] = vn
        p0 = 1.0 / (1.0 + jnp.exp(lw_ref[1:2, :] - lw_ref[0:1, :]))
        g0 = dlb_ref[...] * p0 * (1.0 - p0)
        base = 3 * n
        outs[base][0:1, :] = g0
        outs[base][1:2, :] = -g0
        d, mn, vn = _adamw_math(lw_ref[...], outs[base][...], lm_ref[...], lv_ref[...])
        outs[base + 1][...] = d
        outs[base + 2][...] = mn
        outs[base + 3][...] = vn

    operands = [a for it in items for a in it] + [dlb, lb_w, lb_m, lb_v]
    out_shape = []
    for (g, w, m, v) in items:
        out_shape += [jax.ShapeDtypeStruct(w.shape, F32)] * 3
    out_shape += [jax.ShapeDtypeStruct(lb_w.shape, F32)] * 4
    vm = pl.BlockSpec(memory_space=pltpu.VMEM)
    res = pl.pallas_call(
        body, name="adamw_small", out_shape=tuple(out_shape),
        in_specs=[vm] * len(operands), out_specs=tuple([vm] * len(out_shape)),
        compiler_params=pltpu.CompilerParams(vmem_limit_bytes=VMEM_LIMIT_BYTES),
    )(*operands)
    deltas = [res[3 * t] for t in range(n)] + [res[3 * n + 1]]
    new_m = [res[3 * t + 1] for t in range(n)] + [res[3 * n + 2]]
    new_v = [res[3 * t + 2] for t in range(n)] + [res[3 * n + 3]]
    return res[3 * n], deltas, new_m, new_v


def _shard_row_half(g, by_cols, h):
    if isinstance(g, (list, tuple)):
        R, widths = g[0].shape[0], [s.shape[1] for s in g]
        starts = [sum(widths[:n]) for n in range(len(g))]
        cs = sum(widths) // 4
        rows = [lax.dynamic_slice_in_dim(s, h * (R // 2), R // 2, axis=0) for s in g]
        shards = []
        for k in range(4):
            lo, hi = k * cs, (k + 1) * cs
            cut = [r[:, max(lo - o, 0):min(hi - o, w)] for r, o, w in zip(rows, starts, widths) if lo < o + w and hi > o]
            shards.append(jnp.concatenate(cut, axis=1).astype(BF16))
        return jnp.stack(shards)
    if g.ndim == 3:
        return lax.dynamic_slice_in_dim(g, h * (g.shape[1] // 2), g.shape[1] // 2, axis=1).astype(BF16)
    R, C = g.shape
    if by_cols:
        part = lax.dynamic_index_in_dim(g.reshape(2, R // 2, 4, C // 4), h, axis=0, keepdims=False)
        return part.transpose(1, 0, 2).astype(BF16)
    return lax.dynamic_index_in_dim(g.reshape(4, 2, R // 8, C), h, axis=1, keepdims=False).astype(BF16)


def kernel(x, positions, meta_tokens, w_in, w_q_up, w_kv_up, w_branch_mla, w_branch_hgrn, w_out, w_ffn_in, w_ffn_out, conv_w, conv_b, g_mix_norm, g_q_norm, g_kv_norm, g_hgrn_norm, g_ffn_norm, g_final_norm, lb_raw, loss_target, m_meta_tokens, m_w_in, m_w_q_up, m_w_kv_up, m_w_branch_mla, m_w_branch_hgrn, m_w_out, m_w_ffn_in, m_w_ffn_out, m_conv_w, m_conv_b, m_g_mix_norm, m_g_q_norm, m_g_kv_norm, m_g_hgrn_norm, m_g_ffn_norm, m_g_final_norm, m_lb_raw, v_meta_tokens, v_w_in, v_w_q_up, v_w_kv_up, v_w_branch_mla, v_w_branch_hgrn, v_w_out, v_w_ffn_in, v_w_ffn_out, v_conv_w, v_conv_b, v_g_mix_norm, v_g_q_norm, v_g_kv_norm, v_g_hgrn_norm, v_g_ffn_norm, v_g_final_norm, v_lb_raw):
    S, D = x.shape[1], x.shape[2]
    L = S + PREFIX
    QL, KVL = g_q_norm.shape[1], g_kv_norm.shape[1]
    F = conv_b.shape[1]
    heads = (4 * w_kv_up.shape[2]) // QPAD
    nh = D // HEAD
    assert lb_raw.shape[0] == 2 and g_hgrn_norm.shape[1] == HEAD and L % CHUNK == 0
    ix, iy, ic = _coords()
    chip = 2 * ix + iy

    big = [w_in, w_q_up, w_kv_up, w_branch_mla, w_branch_hgrn, w_out, w_ffn_in, w_ffn_out]
    col_sharded = [True, True, True, False, False, False, True, False]
    shards = [w[0].astype(BF16) for w in big]
    early = _run_rider(_gather_rider(shards[:1], [meta_tokens, conv_w[0]]), "gather_early")
    qkv_rider = _gather_rider(shards[1:3], [])
    late_rider = _gather_rider(shards[3:], [])

    def full(gw, by_cols):
        _, r, c = gw.shape
        return gw.transpose(1, 0, 2).reshape(r, 4 * c) if by_cols else gw.reshape(4 * r, c)

    def col_range(g4, lo, hi):
        c = g4.shape[2]
        pieces = [g4[k][:, max(lo - k * c, 0):min(hi - k * c, c)] for k in range(4) if lo < (k + 1) * c and hi > k * c]
        return pieces[0] if len(pieces) == 1 else jnp.concatenate(pieces, axis=1)

    meta_full = full(early[1], True)
    cw_full = full(early[2], True)
    c0 = QL + KVL
    W_lat = col_range(early[0], 0, c0)
    W_kr = jnp.pad(col_range(early[0], c0, c0 + ROPE), ((0, 0), (0, 128 - ROPE)))
    W_H = col_range(early[0], c0 + ROPE, c0 + ROPE + 4 * D)
    W_G = col_range(early[0], c0 + ROPE + 4 * D, 4 * early[0].shape[2])

    pos = jnp.concatenate([jnp.zeros((PAD_LEN,), jnp.int32), jnp.arange(N_META, dtype=jnp.int32),
                           positions[0].astype(jnp.int32) + N_META])
    inv = 1.0 / (ROPE_THETA ** (jnp.arange(0, ROPE, 2, dtype=F32) / ROPE))
    ang = pos.astype(F32)[:, None] * inv
    zero = jnp.zeros((L, 128 - ROPE), F32)
    cos = jnp.concatenate([jnp.cos(ang), jnp.cos(ang), zero], axis=1)
    sin = jnp.concatenate([jnp.sin(ang), jnp.sin(ang), zero], axis=1)

    h0, u1 = _embed_norm(x[0], meta_full, g_mix_norm)
    hp, qkv = _mm(u1, W_H, "nn", BF16, "proj_hgrn", rider=qkv_rider)
    W_q, W_kv = [full(gw, True) for gw in qkv]
    W_qp = jnp.pad(W_q.reshape(QL, heads, QK_HEAD), ((0, 0), (0, 0), (0, QPAD - QK_HEAD))).reshape(QL, heads * QPAD)
    lat = _mm(u1, W_lat, "nn", F32, "proj_lat")
    gates = _mm(u1, W_G, "nn", BF16, "proj_gates")
    kr = _mm(u1, W_kr, "nn", F32, "proj_krope")
    qn = _rmsnorm_fwd(lat, g_q_norm, "norm_q", col0=0, width=QL)
    kvn = _rmsnorm_fwd(lat, g_kv_norm, "norm_kv", col0=QL, width=KVL)
    qp = _mm(qn, W_qp, "nn", BF16, "q_up")
    kv = _mm(kvn, W_kv, "nn", BF16, "kv_up")
    qc, kc, vv = _rope_fwd(qp, kv, kr, cos, sin, heads)
    o_mla, lse, late = _attn_fwd(qc, kc, vv, heads, late_rider)
    W_a, W_b, W_o, W_fi, W_fo = [full(gw, bc) for gw, bc in zip(late, col_sharded[3:])]
    o_hgrn, o_rec, s_hist = _hgrn_fwd(hp, lb_raw, g_hgrn_norm, nh)
    br_a = _mm(o_mla, W_a, "nn", F32, "branch_mla")
    br_b = _mm(o_hgrn, W_b, "nn", F32, "branch_hgrn")
    merged = _merge_fwd(br_a, br_b, gates)
    h1 = _mm(merged, W_o, "nn", F32, "out_proj", res=h0)
    u2 = _rmsnorm_fwd(h1, g_ffn_norm, "norm_ffn")
    gu = _mm(u2, W_fi, "nn", BF16, "ffn_in")
    act = _conv_fwd(gu, cw_full, conv_b)
    h2 = _mm(act, W_fo, "nn", F32, "ffn_out", res=h1)
    dh2, loss_p, dg_final = _final_loss_bwd(h2, loss_target[0], g_final_norm.reshape(1, D))

    dact = _mm(dh2, W_fo, "nt", F32, "d_act")
    dW_fo = _mm(act, dh2, "tn", F32, "dw_ffn_out")
    dconv, dgu_right, dcb, dcw = _conv_bwd_a(dact, gu, cw_full, conv_b)
    dgu = _conv_bwd_b(dconv, cw_full, dgu_right)
    du2 = _mm(dgu, W_fi, "nt", F32, "d_u2")
    dW_fi = _mm(u2, dgu, "tn", F32, "dw_ffn_in", col_shards=4)
    dh1, dg_ffn = _rmsnorm_bwd(du2, h1, g_ffn_norm, "norm_ffn_bwd", F32, res=dh2)
    dmerged = _mm(dh1, W_o, "nt", F32, "d_merged")
    dW_o = _mm(merged, dh1, "tn", F32, "dw_out")
    d_a, d_b, d_gates = _merge_bwd(dmerged, br_a, br_b, gates)
    do_mla = _mm(d_a, W_a, "nt", BF16, "d_o_mla")
    dW_a = _mm(o_mla, d_a, "tn", F32, "dw_branch_mla")
    do_hgrn = _mm(d_b, W_b, "nt", F32, "d_o_hgrn")
    dW_b = _mm(o_hgrn, d_b, "tn", F32, "dw_branch_hgrn")
    names = ["w_in", "w_q_up", "w_kv_up", "w_branch_mla", "w_branch_hgrn", "w_out", "w_ffn_in", "w_ffn_out"]

    def chip_partials(grads, by_cols, nms, tag):
        keep = [_shard_row_half(g, bc, ic) for g, bc in zip(grads, by_cols)]
        give = [_shard_row_half(g, bc, 1 - ic) for g, bc in zip(grads, by_cols)]
        taken = _to_sibling(give, "pair_exchange_" + tag)
        return [_pair_sum(a, b, "pair_sum_" + nm) for a, b, nm in zip(keep, taken, nms)]

    late_parts = chip_partials([dW_a, dW_b, dW_o, dW_fi, dW_fo], col_sharded[3:], names[3:], "late")
    dhq, dhf, dhi, dhg, dgn_p, dlb_p = _hgrn_bwd(hp, lb_raw, g_hgrn_norm, do_hgrn, o_rec, s_hist, nh)
    dqc, dkc, dvv, late_recv = _attn_bwd(qc, kc, vv, do_mla, lse, _attn_delta(o_mla, do_mla, heads), heads,
                                         _scatter_rider(late_parts))
    dqp, dkv, dkr = _rope_bwd(dqc, dkc, dvv, cos, sin, heads)
    dqn = _mm(dqp, W_qp, "nt", F32, "d_qn")
    dW_qp = _mm(qn, dqp, "tn", F32, "dw_q_up")
    dkvn = _mm(dkv, W_kv, "nt", F32, "d_kvn")
    dW_kv = _mm(kvn, dkv, "tn", F32, "dw_kv_up")
    dq_lat, dg_q = _rmsnorm_bwd(dqn, lat, g_q_norm, "norm_q_bwd", BF16, col0=0)
    dkv_lat, dg_kv = _rmsnorm_bwd(dkvn, lat, g_kv_norm, "norm_kv_bwd", BF16, col0=QL)
    dlat = jnp.concatenate([dq_lat, dkv_lat], axis=1)
    dhp = jnp.concatenate([dhq, dhf, dhi, dhg], axis=1)
    dW_lat = _mm(u1, dlat, "tn", F32, "dw_in_lat")
    dW_H = _mm(u1, dhp, "tn", F32, "dw_in_hgrn")
    dW_G = _mm(u1, d_gates, "tn", F32, "dw_in_gates")
    dW_kr = _mm(u1, dkr, "tn", F32, "dw_in_krope")
    dW_in = [dW_lat, dW_kr[:, :ROPE], dW_H, dW_G]
    dW_q = dW_qp.reshape(QL, heads, QPAD)[:, :, :QK_HEAD].reshape(QL, heads * QK_HEAD)
    early_parts = chip_partials([dW_in, dW_q, dW_kv], col_sharded[:3], names[:3], "early")
    du1 = _mm(dlat, W_lat, "nt", F32, "d_u1_lat")
    du1, early_recv = _mm(dhp, W_H, "nt", F32, "d_u1_hgrn", res=du1, rider=_scatter_rider(early_parts))
    du1 = _mm(d_gates, W_G, "nt", F32, "d_u1_gates", res=du1)
    du1 = _mm(dkr, W_kr, "nt", F32, "d_u1_krope", res=du1)
    (dx_tokens, dh0_prefix), dg_mix = _rmsnorm_bwd(du1, h0, g_mix_norm, "norm_mix_bwd", F32, res=dh1,
                                                   split_prefix=True)
    grad_x = dx_tokens[None]

    received = list(early_recv) + list(late_recv)
    halves = [_sum4(r, "sum_" + nm) for r, nm in zip(received, names)]
    others = _to_sibling(halves, "swap_halves")
    big_m = [m_w_in, m_w_q_up, m_w_kv_up, m_w_branch_mla, m_w_branch_hgrn, m_w_out, m_w_ffn_in, m_w_ffn_out]
    big_v = [v_w_in, v_w_q_up, v_w_kv_up, v_w_branch_mla, v_w_branch_hgrn, v_w_out, v_w_ffn_in, v_w_ffn_out]
    big_out = {}
    for nm, mine, other, w, m, v in zip(names, halves, others, big, big_m, big_v):
        g, d, mn, vn = _adamw_big(mine, other, w[0], m[0], v[0], "adamw_" + nm)
        big_out[nm] = (g[None], d[None], mn[None], vn[None])

    pieces = [loss_p[:, :1], dg_mix, dg_q, dg_kv, jnp.sum(dgn_p[:, 0, :], axis=0, keepdims=True), dg_ffn, dg_final,
              dlb_p[:, 0, :].reshape(1, D), dcb, dcw[0:3].reshape(1, 3 * F), dh0_prefix[PAD_LEN:PREFIX].reshape(1, N_META * D)]
    sizes = [p.shape[1] for p in pieces]
    flat = jnp.concatenate(pieces, axis=1)[0]
    rows = -(-flat.shape[0] // 1024) * 8
    packed = jnp.pad(flat, (0, rows * 128 - flat.shape[0])).reshape(rows, 128)
    total = _allreduce_small(packed).reshape(-1)
    offs = [0]
    for s in sizes:
        offs.append(offs[-1] + s)
    loss, g_mix, g_q, g_kv, g_hg, g_ffn, g_fin, dlb, g_cb, g_cw, g_meta = [
        total[offs[t]:offs[t + 1]].reshape(1, sizes[t]) for t in range(len(sizes))]
    g_cw = lax.dynamic_slice_in_dim(g_cw.reshape(3, F), chip * (F // 4), F // 4, axis=1)
    g_meta = lax.dynamic_slice_in_dim(g_meta.reshape(N_META, D), chip * (D // 4), D // 4, axis=1)
    items = [(g_meta, meta_tokens, m_meta_tokens, v_meta_tokens),
             (g_cw, conv_w[0], m_conv_w[0], v_conv_w[0]),
             (g_cb, conv_b, m_conv_b, v_conv_b),
             (g_mix, g_mix_norm, m_g_mix_norm, v_g_mix_norm),
             (g_q, g_q_norm, m_g_q_norm, v_g_q_norm),
             (g_kv, g_kv_norm, m_g_kv_norm, v_g_kv_norm),
             (g_hg, g_hgrn_norm, m_g_hgrn_norm, v_g_hgrn_norm),
             (g_ffn, g_ffn_norm, m_g_ffn_norm, v_g_ffn_norm),
             (g_fin, g_final_norm.reshape(1, D), m_g_final_norm.reshape(1, D), v_g_final_norm.reshape(1, D))]
    g_lb, s_delta, s_m, s_v = _adamw_small(items, (lb_raw, m_lb_raw, v_lb_raw), dlb)
    s_grads = [it[0] for it in items] + [g_lb]

    def shape_small(vals):
        meta, cw, cb, mix, q, kvg, hg, ffn, fin, lb = vals
        return [meta, cw[None], cb, mix, q, kvg, hg, ffn, fin.reshape(D), lb]

    s_grads, s_delta, s_m, s_v = [shape_small(v) for v in (s_grads, s_delta, s_m, s_v)]

    def ordered(kind, small):
        bigs = [big_out[nm][kind] for nm in names]
        return [small[0]] + bigs + small[1:]

    return (loss.reshape(()), grad_x, *ordered(0, s_grads), *ordered(1, s_delta), *ordered(2, s_m), *ordered(3, s_v))
```

```python
import math

import jax
import jax.numpy as jnp
from jax import lax
from jax.experimental import pallas as pl
from jax.experimental.pallas import tpu as pltpu

F32 = jnp.float32
BF16 = jnp.bfloat16
MESH = pl.DeviceIdType.MESH

NORM_EPS = 1e-6
N_META = 16
PREFIX = 128
PAD_LEN = PREFIX - N_META
HEAD = 128
ROPE = 64
QK_HEAD = HEAD + ROPE
QPAD = 2 * HEAD
SOFTMAX_SCALE = QK_HEAD ** -0.5
ROPE_THETA = 10000.0
CHUNK = 128
HALO = 16
LEVELS = 7
HGRN_FWD_HEADS = 4
HGRN_BWD_HEADS = 4
ATTN_FWD_HEADS = 2
ATTN_BWD_HEADS = 2
NEG = -1e30

ADAM_LR = 0.001
ADAM_B1 = 0.9
ADAM_B2 = 0.999
ADAM_EPS = 1e-08
ADAM_WD = 0.01
ADAM_STEP = 10

VMEM_LIMIT_BYTES = 48 * 1024 * 1024


def _params(*sem):
    return pltpu.CompilerParams(dimension_semantics=sem, vmem_limit_bytes=VMEM_LIMIT_BYTES)


def _tile(n, prefs):
    for p in prefs:
        if n % p == 0:
            return p
    return n


ROW_TILES = (640, 512, 256, 128, 64, 32, 16, 8)
TN_ROW_TILES = (1024, 1408, 1536, 512, 256, 128)
TN_K_TILES = (1664, 640, 512, 256, 128)
TALL_ROW_TILE = 1664
SHARD_COL_TILES = (1408, 1024, 512, 256, 128)
COL_TILES = (1024, 512, 256, 128)
K_TILES = (2048, 1536, 1408, 1024, 512, 256, 128)


def _sigmoid(x):
    return 1.0 / (1.0 + jnp.exp(-x))


def _dot(a, b, dims):
    return lax.dot_general(a.astype(BF16), b.astype(BF16), (dims, ((), ())), preferred_element_type=F32)


NN = ((1,), (0,))
TN = ((0,), (0,))
NT = ((1,), (1,))


def _split3(x):
    hi = x.astype(BF16)
    r = x - hi.astype(F32)
    mid = r.astype(BF16)
    lo = (r - mid.astype(F32)).astype(BF16)
    return hi, mid, lo


def _dot_exact_rhs(sel, x, dims):
    hi, mid, lo = _split3(x)
    return _dot(sel, hi, dims) + _dot(sel, mid, dims) + _dot(sel, lo, dims)


def _mm(a, b, mode, out_dtype, name, res=None, rider=None, col_shards=1):
    if mode == "nn":
        (M, K), (K2, N) = a.shape, b.shape
    elif mode == "tn":
        (K, M), (K2, N) = a.shape, b.shape
    else:
        (M, K), (N, K2) = a.shape, b.shape
    assert K == K2, (name, a.shape, b.shape)
    tm = _tile(M, TN_ROW_TILES if mode == "tn" else ROW_TILES)
    tn = _tile(N // col_shards, COL_TILES if col_shards == 1 else SHARD_COL_TILES)
    tk = _tile(K, TN_K_TILES if mode == "tn" else K_TILES)
    nk = K // tk
    if mode != "tn" and nk > 1 and res is None and a.dtype == BF16 and M % TALL_ROW_TILE == 0:
        tm = TALL_ROW_TILE
    dims = {"nn": NN, "tn": TN, "nt": NT}[mode]

    n_rin = 0 if rider is None else len(rider.operands)
    n_rout = 0 if rider is None else len(rider.out_shapes)
    n_rsem = 0 if rider is None else len(rider.sem_shapes)
    grid = (M // tm, N // tn, nk)

    def body(*refs):
        main_in, r_ins, (o_ref,), r_outs, accs, r_sems = _split_refs(
            refs, (2 if res is None else 3, n_rin, 1, n_rout, 0 if nk == 1 else 1, n_rsem))
        a_ref, b_ref = main_in[:2]
        r_ref = None if res is None else main_in[2]
        i, j, k = pl.program_id(0), pl.program_id(1), pl.program_id(2)
        if rider is not None:
            pl.when((i == 0) & (j == 0) & (k == 0))(lambda: rider.start(r_ins, r_outs, r_sems))

        def finish(r):
            if r_ref is not None:
                r = r + r_ref[...].astype(F32)
            o_ref[...] = r.astype(out_dtype)

        if nk == 1:
            finish(_dot(a_ref[...], b_ref[...], dims))
        else:
            acc = accs[0]

            @pl.when(k == 0)
            def _():
                acc[...] = jnp.zeros_like(acc)

            acc[...] += _dot(a_ref[...], b_ref[...], dims)

            @pl.when(k == nk - 1)
            def _():
                finish(acc[...])

        if rider is not None:
            last = (i == grid[0] - 1) & (j == grid[1] - 1) & (k == nk - 1)
            pl.when(last)(lambda: rider.finish(r_ins, r_outs, r_sems))

    if mode == "nn":
        a_spec = pl.BlockSpec((tm, tk), lambda i, j, k: (i, k))
        b_spec = pl.BlockSpec((tk, tn), lambda i, j, k: (k, j))
    elif mode == "tn":
        a_spec = pl.BlockSpec((tk, tm), lambda i, j, k: (k, i))
        b_spec = pl.BlockSpec((tk, tn), lambda i, j, k: (k, j))
    else:
        a_spec = pl.BlockSpec((tm, tk), lambda i, j, k: (i, k))
        b_spec = pl.BlockSpec((tn, tk), lambda i, j, k: (j, k))
    in_specs = [a_spec, b_spec]
    operands = [a, b]
    if res is not None:
        in_specs.append(pl.BlockSpec((tm, tn), lambda i, j, k: (i, j)))
        operands.append(res)
    if col_shards == 1:
        out_shape = jax.ShapeDtypeStruct((M, N), out_dtype)
        out_spec = pl.BlockSpec((tm, tn), lambda i, j, k: (i, j))
    else:
        per_shard = (N // col_shards) // tn
        out_shape = jax.ShapeDtypeStruct((col_shards, M, N // col_shards), out_dtype)
        out_spec = pl.BlockSpec((None, tm, tn), lambda i, j, k: (j // per_shard, i, j % per_shard))
    scratch = [] if nk == 1 else [pltpu.VMEM((tm, tn), F32)]
    if rider is None:
        return pl.pallas_call(
            body, name=name, out_shape=out_shape, grid=grid, in_specs=in_specs, out_specs=out_spec,
            scratch_shapes=scratch, compiler_params=_params("parallel", "parallel", "arbitrary"),
        )(*operands)
    res_all = pl.pallas_call(
        body, name=name, out_shape=(out_shape, *rider.out_shapes), grid=grid,
        in_specs=in_specs + [ANY] * n_rin, out_specs=(out_spec, *([ANY] * n_rout)),
        scratch_shapes=scratch + rider.sem_shapes,
        compiler_params=pltpu.CompilerParams(dimension_semantics=("arbitrary", "arbitrary", "arbitrary"),
                                             vmem_limit_bytes=VMEM_LIMIT_BYTES, has_side_effects=True),
    )(*operands, *rider.operands)
    return res_all[0], res_all[1:]


def _rmsnorm_fwd(x, g, name, col0=0, width=None):
    L = x.shape[0]
    width = x.shape[1] if width is None else width
    assert col0 % width == 0
    cb = col0 // width
    tm = _tile(L, (128, 64, 32, 16))

    def body(x_ref, g_ref, o_ref):
        xv = x_ref[...].astype(F32)
        r = lax.rsqrt(jnp.mean(xv * xv, axis=-1, keepdims=True) + NORM_EPS)
        o_ref[...] = ((xv * r) * g_ref[...]).astype(BF16)

    return pl.pallas_call(
        body, name=name,
        out_shape=jax.ShapeDtypeStruct((L, width), BF16),
        grid=(L // tm,),
        in_specs=[pl.BlockSpec((tm, width), lambda i: (i, cb)), pl.BlockSpec((1, width), lambda i: (0, 0))],
        out_specs=pl.BlockSpec((tm, width), lambda i: (i, 0)),
        compiler_params=_params("parallel"),
    )(x, g)


def _embed_norm(x, meta, g):
    S, D = x.shape
    L = S + PREFIX
    tm = PREFIX

    def body(x_ref, meta_ref, g_ref, h_ref, u_ref):
        i = pl.program_id(0)

        @pl.when(i == 0)
        def _():
            h_ref[...] = jnp.zeros_like(h_ref)
            h_ref[PAD_LEN:PREFIX, :] = meta_ref[...]

        @pl.when(i > 0)
        def _():
            h_ref[...] = x_ref[...]

        xv = h_ref[...]
        r = lax.rsqrt(jnp.mean(xv * xv, axis=-1, keepdims=True) + NORM_EPS)
        u_ref[...] = ((xv * r) * g_ref[...]).astype(BF16)

    return pl.pallas_call(
        body, name="embed_norm_mix",
        out_shape=(jax.ShapeDtypeStruct((L, D), F32), jax.ShapeDtypeStruct((L, D), BF16)),
        grid=(L // tm,),
        in_specs=[pl.BlockSpec((tm, D), lambda i: (jnp.maximum(i - 1, 0), 0)),
                  pl.BlockSpec((N_META, D), lambda i: (0, 0)), pl.BlockSpec((1, D), lambda i: (0, 0))],
        out_specs=(pl.BlockSpec((tm, D), lambda i: (i, 0)), pl.BlockSpec((tm, D), lambda i: (i, 0))),
        compiler_params=_params("parallel"),
    )(x, meta, g)


def _rmsnorm_bwd(dy, x, g, name, out_dtype, col0=0, res=None, split_prefix=False):
    L, width = dy.shape
    assert col0 % width == 0
    cb = col0 // width
    tm = PREFIX if split_prefix else _tile(L, (128, 64, 32, 16))

    def body(*refs):
        head_ref = None
        if split_prefix:
            refs, head_ref = refs[:-1], refs[-1]
        if res is None:
            dy_ref, x_ref, g_ref, dx_ref, dg_ref = refs
            r_ref = None
        else:
            dy_ref, x_ref, g_ref, r_ref, dx_ref, dg_ref = refs

        @pl.when(pl.program_id(0) == 0)
        def _():
            dg_ref[...] = jnp.zeros_like(dg_ref)

        xv = x_ref[...].astype(F32)
        dyv = dy_ref[...].astype(F32)
        r = lax.rsqrt(jnp.mean(xv * xv, axis=-1, keepdims=True) + NORM_EPS)
        z = dyv * g_ref[...]
        dx = r * z - xv * ((r * r * r) * jnp.mean(z * xv, axis=-1, keepdims=True))
        if r_ref is not None:
            dx = dx + r_ref[...]
        dg_ref[...] += jnp.sum(dyv * (xv * r), axis=0, keepdims=True)
        if head_ref is None:
            dx_ref[...] = dx.astype(out_dtype)
        else:
            @pl.when(pl.program_id(0) == 0)
            def _():
                head_ref[...] = dx.astype(out_dtype)

            @pl.when(pl.program_id(0) > 0)
            def _():
                dx_ref[...] = dx.astype(out_dtype)

    in_specs = [pl.BlockSpec((tm, width), lambda i: (i, 0)),
                pl.BlockSpec((tm, width), lambda i: (i, cb)),
                pl.BlockSpec((1, width), lambda i: (0, 0))]
    operands = [dy, x, g]
    if res is not None:
        in_specs.append(pl.BlockSpec((tm, width), lambda i: (i, 0)))
        operands.append(res)
    dg_shape, dg_spec = jax.ShapeDtypeStruct((1, width), F32), pl.BlockSpec((1, width), lambda i: (0, 0))
    if not split_prefix:
        return pl.pallas_call(
            body, name=name, out_shape=(jax.ShapeDtypeStruct((L, width), out_dtype), dg_shape), grid=(L // tm,),
            in_specs=in_specs, out_specs=(pl.BlockSpec((tm, width), lambda i: (i, 0)), dg_spec),
            compiler_params=_params("arbitrary"),
        )(*operands)
    dx_rest, dg, dx_head = pl.pallas_call(
        body, name=name,
        out_shape=(jax.ShapeDtypeStruct((L - PREFIX, width), out_dtype), dg_shape,
                   jax.ShapeDtypeStruct((PREFIX, width), out_dtype)),
        grid=(L // tm,), in_specs=in_specs,
        out_specs=(pl.BlockSpec((tm, width), lambda i: (jnp.maximum(i - 1, 0), 0)), dg_spec,
                   pl.BlockSpec((PREFIX, width), lambda i: (0, 0))),
        compiler_params=_params("arbitrary"),
    )(*operands)
    return (dx_rest, dx_head), dg


def _final_loss_bwd(h2, tgt, g):
    L, D = h2.shape
    tm = PREFIX
    inv_d = 1.0 / D

    def body(h_ref, t_ref, g_ref, dh_ref, loss_ref, dg_ref):
        i = pl.program_id(0)

        @pl.when(i == 0)
        def _():
            loss_ref[...] = jnp.zeros_like(loss_ref)
            dg_ref[...] = jnp.zeros_like(dg_ref)

        xv = h_ref[...]
        r = lax.rsqrt(jnp.mean(xv * xv, axis=-1, keepdims=True) + NORM_EPS)
        xn = xv * r
        y = xn * g_ref[...]
        real = (i >= PREFIX // tm).astype(F32)
        diff = (y - t_ref[...]) * real
        loss_ref[...] += 0.5 * inv_d * jnp.sum(diff * diff)
        dyv = diff * inv_d
        z = dyv * g_ref[...]
        dh_ref[...] = r * z - xv * ((r * r * r) * jnp.mean(z * xv, axis=-1, keepdims=True))
        dg_ref[...] += jnp.sum(dyv * xn, axis=0, keepdims=True)

    shift = PREFIX // tm
    return pl.pallas_call(
        body, name="final_loss_bwd",
        out_shape=(jax.ShapeDtypeStruct((L, D), F32), jax.ShapeDtypeStruct((1, 128), F32),
                   jax.ShapeDtypeStruct((1, D), F32)),
        grid=(L // tm,),
        in_specs=[pl.BlockSpec((tm, D), lambda i: (i, 0)),
                  pl.BlockSpec((tm, D), lambda i: (jnp.maximum(i - shift, 0), 0)),
                  pl.BlockSpec((1, D), lambda i: (0, 0))],
        out_specs=(pl.BlockSpec((tm, D), lambda i: (i, 0)), pl.BlockSpec((1, 128), lambda i: (0, 0)),
                   pl.BlockSpec((1, D), lambda i: (0, 0))),
        compiler_params=_params("arbitrary"),
    )(h2, tgt, g)


def _rot_half(x):
    lane = lax.broadcasted_iota(jnp.int32, x.shape, 1)
    return jnp.where(lane < ROPE // 2, -pltpu.roll(x, 128 - ROPE // 2, 1), pltpu.roll(x, ROPE // 2, 1))


def _rope_fwd(qp, kv, kr, cos, sin, heads):
    L = qp.shape[0]
    tm = _tile(L, (128,))

    def body(q_ref, kv_ref, kr_ref, c_ref, s_ref, qc_ref, kc_ref, v_ref):
        c, s = c_ref[...], s_ref[...]
        krv = kr_ref[...]
        kr_rot = (krv * c + _rot_half(krv) * s).astype(BF16)
        for h in range(heads):
            lo = h * QPAD
            qc_ref[:, lo:lo + HEAD] = (q_ref[:, lo:lo + HEAD].astype(F32) * SOFTMAX_SCALE).astype(BF16)
            qr = q_ref[:, lo + HEAD:lo + QPAD].astype(F32)
            qc_ref[:, lo + HEAD:lo + QPAD] = ((qr * c + _rot_half(qr) * s) * SOFTMAX_SCALE).astype(BF16)
            kc_ref[:, lo:lo + HEAD] = kv_ref[:, lo:lo + HEAD].astype(BF16)
            kc_ref[:, lo + HEAD:lo + QPAD] = kr_rot
            v_ref[:, h * HEAD:(h + 1) * HEAD] = kv_ref[:, lo + HEAD:lo + QPAD].astype(BF16)

    W = heads * QPAD
    row = lambda w: pl.BlockSpec((tm, w), lambda i: (i, 0))
    return pl.pallas_call(
        body, name="rope_fwd",
        out_shape=(jax.ShapeDtypeStruct((L, W), BF16), jax.ShapeDtypeStruct((L, W), BF16),
                   jax.ShapeDtypeStruct((L, heads * HEAD), BF16)),
        grid=(L // tm,),
        in_specs=[row(W), row(W), row(128), row(128), row(128)],
        out_specs=(row(W), row(W), row(heads * HEAD)),
        compiler_params=_params("parallel"),
    )(qp, kv, kr, cos, sin)


def _rope_bwd(dqc, dkc, dv, cos, sin, heads):
    L = dqc.shape[0]
    tm = _tile(L, (128,))

    def body(dq_ref, dk_ref, dv_ref, c_ref, s_ref, dqp_ref, dkv_ref, dkr_ref):
        c, s = c_ref[...], s_ref[...]
        acc = jnp.zeros((tm, 128), F32)
        for h in range(heads):
            lo = h * QPAD
            dqp_ref[:, lo:lo + HEAD] = (dq_ref[:, lo:lo + HEAD] * SOFTMAX_SCALE).astype(BF16)
            d = dq_ref[:, lo + HEAD:lo + QPAD]
            dqp_ref[:, lo + HEAD:lo + QPAD] = ((d * c - _rot_half(d) * s) * SOFTMAX_SCALE).astype(BF16)
            dkv_ref[:, lo:lo + HEAD] = dk_ref[:, lo:lo + HEAD].astype(BF16)
            dkv_ref[:, lo + HEAD:lo + QPAD] = dv_ref[:, h * HEAD:(h + 1) * HEAD].astype(BF16)
            acc = acc + dk_ref[:, lo + HEAD:lo + QPAD]
        dkr_ref[...] = (acc * c - _rot_half(acc) * s).astype(BF16)

    W = heads * QPAD
    row = lambda w: pl.BlockSpec((tm, w), lambda i: (i, 0))
    return pl.pallas_call(
        body, name="rope_bwd",
        out_shape=(jax.ShapeDtypeStruct((L, W), BF16), jax.ShapeDtypeStruct((L, W), BF16),
                   jax.ShapeDtypeStruct((L, 128), BF16)),
        grid=(L // tm,),
        in_specs=[row(W), row(W), row(heads * HEAD), row(128), row(128)],
        out_specs=(row(W), row(W), row(128)),
        compiler_params=_params("parallel"),
    )(dqc, dkc, dv, cos, sin)


def _attn_keep(qi, ki, ta):
    t = qi * ta + lax.broadcasted_iota(jnp.int32, (ta, ta), 0)
    s = ki * ta + lax.broadcasted_iota(jnp.int32, (ta, ta), 1)
    return (s <= t) & ((s >= PAD_LEN) | (s == t))


def _split_refs(refs, sizes):
    out, at = [], 0
    for n in sizes:
        out.append(refs[at:at + n])
        at += n
    return out


def _attn_fwd(qc, kc, v, heads, rider):
    L = qc.shape[0]
    ta = _tile(L, (640, 128))
    nb = L // ta
    pairs = [(i, j) for i in range(nb) for j in range(i + 1)]
    q_of = jnp.asarray([p[0] for p in pairs], jnp.int32)
    k_of = jnp.asarray([p[1] for p in pairs], jnp.int32)
    n_rin, n_rout = len(rider.operands), len(rider.out_shapes)
    per = _tile(heads, (ATTN_FWD_HEADS, 1))
    ng = heads // per

    def body(*refs):
        (q_of_ref, k_of_ref, q_ref, k_ref, v_ref), r_ins, (o_ref, lse_ref), r_outs, (m_sc, l_sc, acc_sc), r_sems = \
            _split_refs(refs, (5, n_rin, 2, n_rout, 3, len(rider.sem_shapes)))
        h, t = pl.program_id(0), pl.program_id(1)
        qi, ki = q_of_ref[t], k_of_ref[t]
        pl.when((h == 0) & (t == 0))(lambda: rider.start(r_ins, r_outs, r_sems))

        @pl.when(ki == 0)
        def _():
            m_sc[...] = jnp.full_like(m_sc, NEG)
            l_sc[...] = jnp.zeros_like(l_sc)
            acc_sc[...] = jnp.zeros_like(acc_sc)

        def step(masked):
            wide = lambda hh: slice(hh * QPAD, (hh + 1) * QPAD)
            lanes = lambda hh: slice(hh * HEAD, (hh + 1) * HEAD)
            ss = [_dot(q_ref[:, wide(hh)], k_ref[:, wide(hh)], NT) for hh in range(per)]
            if masked:
                keep = _attn_keep(qi, ki, ta)
                ss = [jnp.where(keep, s, NEG) for s in ss]
            for hh in range(per):
                m_old = m_sc[hh]
                m_new = jnp.maximum(m_old, jnp.max(ss[hh], axis=-1, keepdims=True))
                p = jnp.exp(ss[hh] - jnp.tile(m_new, (1, ta // HEAD)))
                alpha = jnp.exp(m_old - m_new)
                l_sc[hh] = alpha * l_sc[hh] + jnp.sum(p, axis=-1, keepdims=True)
                acc_sc[hh] = alpha * acc_sc[hh] + _dot(p, v_ref[:, lanes(hh)], NN)
                m_sc[hh] = m_new

        pl.when((ki == qi) | ((ki == 0) & (qi > 0)))(lambda: step(True))
        pl.when((ki > 0) & (ki < qi))(lambda: step(False))

        @pl.when(ki == qi)
        def _():
            for hh in range(per):
                l = l_sc[hh]
                o_ref[:, hh * HEAD:(hh + 1) * HEAD] = (acc_sc[hh] / l).astype(BF16)
                lse_ref[:, hh * HEAD:(hh + 1) * HEAD] = m_sc[hh] + jnp.log(l)

        pl.when((h == ng - 1) & (t == len(pairs) - 1))(lambda: rider.finish(r_ins, r_outs, r_sems))

    qrow = lambda w: pl.BlockSpec((ta, per * w), lambda h, t, q_of, k_of: (q_of[t], h))
    krow = lambda w: pl.BlockSpec((ta, per * w), lambda h, t, q_of, k_of: (k_of[t], h))
    stat = pltpu.VMEM((per, ta, HEAD), F32)
    res = pl.pallas_call(
        body, name="attn_fwd",
        out_shape=(jax.ShapeDtypeStruct((L, heads * HEAD), BF16), jax.ShapeDtypeStruct((L, heads * HEAD), F32),
                   *rider.out_shapes),
        grid_spec=pltpu.PrefetchScalarGridSpec(
            num_scalar_prefetch=2, grid=(ng, len(pairs)),
            in_specs=[qrow(QPAD), krow(QPAD), krow(HEAD)] + [ANY] * n_rin,
            out_specs=(qrow(HEAD), qrow(HEAD), *([ANY] * n_rout)),
            scratch_shapes=[stat, stat, stat] + rider.sem_shapes),
        compiler_params=pltpu.CompilerParams(dimension_semantics=("arbitrary", "arbitrary"),
                                             vmem_limit_bytes=VMEM_LIMIT_BYTES, has_side_effects=True),
    )(q_of, k_of, qc, kc, v, *rider.operands)
    return res[0], res[1], res[2:]


def _attn_delta(o, do, heads):
    L = o.shape[0]
    tm = _tile(L, (128,))

    def body(o_ref, do_ref, d_ref):
        for h in range(heads):
            cols = slice(h * HEAD, (h + 1) * HEAD)
            d = jnp.sum(do_ref[:, cols].astype(F32) * o_ref[:, cols].astype(F32), axis=-1, keepdims=True)
            d_ref[:, cols] = jnp.broadcast_to(d, (tm, HEAD))

    row = pl.BlockSpec((tm, heads * HEAD), lambda i: (i, 0))
    return pl.pallas_call(
        body, name="attn_delta", out_shape=jax.ShapeDtypeStruct((L, heads * HEAD), F32), grid=(L // tm,),
        in_specs=[row, row], out_specs=row, compiler_params=_params("parallel"),
    )(o, do)


def _attn_bwd(qc, kc, v, do, lse, delta, heads, rider):
    L = qc.shape[0]
    ta = _tile(L, (640, 128))
    nb = L // ta
    pairs = [(j, i) for j in range(nb) for i in range(j, nb)]
    k_of = jnp.asarray([p[0] for p in pairs], jnp.int32)
    q_of = jnp.asarray([p[1] for p in pairs], jnp.int32)
    n_rin, n_rout = len(rider.operands), len(rider.out_shapes)
    per = _tile(heads, (ATTN_BWD_HEADS, 1))
    ng = heads // per

    def body(*refs):
        ((k_of_ref, q_of_ref, q_ref, k_ref, v_ref, do_ref, lse_ref, delta_ref), r_ins, (dq_hbm, dk_ref, dv_ref),
         r_outs, (dq_sc, dq_sem), r_sems) = _split_refs(refs, (8, n_rin, 3, n_rout, 2, len(rider.sem_shapes)))
        h, t = pl.program_id(0), pl.program_id(1)
        kj, qi = k_of_ref[t], q_of_ref[t]
        pl.when((h == 0) & (t == 0))(lambda: rider.start(r_ins, r_outs, r_sems))

        @pl.when(t == 0)
        def _():
            dq_sc[...] = jnp.zeros_like(dq_sc)

        @pl.when(qi == kj)
        def _():
            dk_ref[...] = jnp.zeros_like(dk_ref)
            dv_ref[...] = jnp.zeros_like(dv_ref)

        def step(masked):
            wide = lambda hh: slice(hh * QPAD, (hh + 1) * QPAD)
            lanes = lambda hh: slice(hh * HEAD, (hh + 1) * HEAD)
            rep = (1, ta // HEAD)
            rows = pl.ds(pl.multiple_of(qi * ta, ta), ta)
            ss = [_dot(q_ref[:, wide(hh)], k_ref[:, wide(hh)], NT) for hh in range(per)]
            dps = [_dot(do_ref[:, lanes(hh)], v_ref[:, lanes(hh)], NT) for hh in range(per)]
            ps = [jnp.exp(ss[hh] - jnp.tile(lse_ref[:, lanes(hh)], rep)) for hh in range(per)]
            if masked:
                keep = _attn_keep(qi, kj, ta)
                ps = [jnp.where(keep, p, 0.0) for p in ps]
            dss = [ps[hh] * (dps[hh] - jnp.tile(delta_ref[:, lanes(hh)], rep)) for hh in range(per)]
            for hh in range(per):
                dv_ref[:, lanes(hh)] += _dot(ps[hh], do_ref[:, lanes(hh)], TN)
                dk_ref[:, wide(hh)] += _dot(dss[hh], q_ref[:, wide(hh)], TN)
                dq_sc[rows, wide(hh)] += _dot(dss[hh], k_ref[:, wide(hh)], NN)

        pl.when((qi == kj) | ((kj == 0) & (qi > 0)))(lambda: step(True))
        pl.when((kj > 0) & (qi > kj))(lambda: step(False))

        @pl.when(t == len(pairs) - 1)
        def _():
            cols = pl.ds(pl.multiple_of(h * (per * QPAD), per * QPAD), per * QPAD)
            out = pltpu.make_async_copy(dq_sc, dq_hbm.at[:, cols], dq_sem)
            out.start()
            out.wait()

        pl.when((h == ng - 1) & (t == len(pairs) - 1))(lambda: rider.finish(r_ins, r_outs, r_sems))

    qrow = lambda w: pl.BlockSpec((ta, per * w), lambda h, t, k_of, q_of: (q_of[t], h))
    krow = lambda w: pl.BlockSpec((ta, per * w), lambda h, t, k_of, q_of: (k_of[t], h))
    res = pl.pallas_call(
        body, name="attn_bwd",
        out_shape=(jax.ShapeDtypeStruct((L, heads * QPAD), F32), jax.ShapeDtypeStruct((L, heads * QPAD), F32),
                   jax.ShapeDtypeStruct((L, heads * HEAD), F32), *rider.out_shapes),
        grid_spec=pltpu.PrefetchScalarGridSpec(
            num_scalar_prefetch=2, grid=(ng, len(pairs)),
            in_specs=[qrow(QPAD), krow(QPAD), krow(HEAD), qrow(HEAD), qrow(HEAD), qrow(HEAD)] + [ANY] * n_rin,
            out_specs=(ANY, krow(QPAD), krow(HEAD), *([ANY] * n_rout)),
            scratch_shapes=[pltpu.VMEM((L, per * QPAD), F32), pltpu.SemaphoreType.DMA] + rider.sem_shapes),
        compiler_params=pltpu.CompilerParams(dimension_semantics=("arbitrary", "arbitrary"),
                                             vmem_limit_bytes=VMEM_LIMIT_BYTES, has_side_effects=True),
    )(k_of, q_of, qc, kc, v, do, lse, delta, *rider.operands)
    return res[0], res[1], res[2], res[3:]


def _hgrn_constants():
    t = jnp.arange(CHUNK)
    tril = (t[None, :] <= t[:, None]).astype(BF16)
    sel, blk = [], []
    for lv in range(LEVELS):
        hs = 1 << lv
        mid = (t // (2 * hs)) * (2 * hs) + hs - 1
        sel.append((t[None, :] == mid[:, None]).astype(BF16))
        blk.append(((t[:, None] // (2 * hs)) == (t[None, :] // (2 * hs))).astype(F32))
    return tril, jnp.concatenate(sel, axis=0), jnp.concatenate(blk, axis=0)


def _hgrn_decay_grad_constants():
    r = jnp.arange(CHUNK)[:, None]
    c = jnp.arange(CHUNK)[None, :]
    mats = []
    for lv in range(LEVELS):
        same = (r >> (lv + 1)) == (c >> (lv + 1))
        second = ((r >> lv) & 1) == 1
        mats.append(same & jnp.where(second, c >= r, c < r))
    mats += [c >= r, c < r]
    return jnp.concatenate(mats, axis=1).astype(BF16)


def _hgrn_gates(hq_ref, hf_ref, hi_ref, lb_ref, cols, row0):
    rows = row0 + lax.broadcasted_iota(jnp.int32, (CHUNK, 1), 0)
    valid = rows >= PAD_LEN
    lb = 1.0 / (1.0 + jnp.exp(lb_ref[1:2, cols] - lb_ref[0:1, cols]))
    hq = hq_ref[:, cols].astype(F32)
    sq = _sigmoid(hq)
    sg = _sigmoid(hf_ref[:, cols].astype(F32))
    f = lb + (1.0 - lb) * sg
    g = jnp.where(valid, jnp.log(f), 0.0)
    k = jnp.where(valid, 1.0 - f, 0.0)
    return dict(q=hq * sq, sq=sq, hq=hq, k=k, v=hi_ref[:, cols].astype(F32), g=g, f=f, sg=sg, lb=lb, valid=valid)


def _hgrn_prefix(tril_ref, sel_ref, gs):
    n = len(gs)
    b_all = _dot_exact_rhs(tril_ref[...], jnp.concatenate(gs, axis=1) if n > 1 else gs[0], NN)
    bm_all = _dot_exact_rhs(sel_ref[...], b_all, NN)
    cut = lambda a, i: a[:, i * HEAD:(i + 1) * HEAD]
    return [cut(b_all, i) for i in range(n)], [cut(bm_all, i) for i in range(n)]


def _hgrn_levels(q, k, b, bm_all, blk_ref):
    t = lax.broadcasted_iota(jnp.int32, (CHUNK, 1), 0)
    out = []
    for lv in range(LEVELS):
        bm = bm_all[lv * CHUNK:(lv + 1) * CHUNK, :]
        second = ((t >> lv) & 1) == 1
        eq = jnp.where(second, jnp.exp(jnp.minimum(b - bm, 0.0)), 0.0)
        ek = jnp.where(second, 0.0, jnp.exp(jnp.minimum(bm - b, 0.0)))
        same = blk_ref[lv * CHUNK:(lv + 1) * CHUNK, :]
        out.append((eq, ek, (q * eq).astype(BF16), (k * ek).astype(BF16), same))
    return out


def _hgrn_intra(qs, ks, levels):
    tt = lax.broadcasted_iota(jnp.int32, (CHUNK, CHUNK), 0)
    ss = lax.broadcasted_iota(jnp.int32, (CHUNK, CHUNK), 1)
    ps = [jnp.where(tt == ss, jnp.sum(q * k, axis=-1, keepdims=True), 0.0) for q, k in zip(qs, ks)]
    for lv in range(LEVELS):
        for i, lvl in enumerate(levels):
            _, _, ql, kl, same = lvl[lv]
            ps[i] = ps[i] + _dot(ql, kl, NT) * same
    return ps


def _hgrn_fwd(hp, lb_raw, g_norm, nh):
    L = hp.shape[0]
    D = nh * HEAD
    nc = L // CHUNK
    per = _tile(nh, (HGRN_FWD_HEADS, 2, 1))
    ng = nh // per
    tril, sel, blk = _hgrn_constants()

    def body(hq_ref, hf_ref, hi_ref, hg_ref, lb_ref, gn_ref, tril_ref, sel_ref, blk_ref,
             oh_ref, orec_ref, shist_ref, s_sc, b_sc):
        c = pl.program_id(1)

        @pl.when(c == 0)
        def _():
            s_sc[...] = jnp.zeros_like(s_sc)

        heads_here = range(per)
        lanes = [slice(hh * HEAD, (hh + 1) * HEAD) for hh in heads_here]
        ws = [_hgrn_gates(hq_ref, hf_ref, hi_ref, lb_ref, lanes[hh], c * CHUNK) for hh in heads_here]
        qs, ks, vs = [w["q"] for w in ws], [w["k"] for w in ws], [w["v"] for w in ws]
        bs, bms = _hgrn_prefix(tril_ref, sel_ref, [w["g"] for w in ws])
        for hh in heads_here:
            b_sc[hh] = bs[hh]
        b_lasts = [b_sc[hh, CHUNK - 1:CHUNK, :] for hh in heads_here]
        ps = _hgrn_intra(qs, ks, [_hgrn_levels(qs[hh], ks[hh], bs[hh], bms[hh], blk_ref) for hh in heads_here])
        s_ins = [s_sc[hh] for hh in heads_here]
        os_ = [_dot(ps[hh], vs[hh], NN) + _dot(qs[hh] * jnp.exp(bs[hh]), s_ins[hh], NT) for hh in heads_here]
        for hh in heads_here:
            shist_ref[0, hh] = s_ins[hh]
            s_sc[hh] = (jnp.exp(b_lasts[hh]) * s_ins[hh]
                        + _dot(vs[hh], ks[hh] * jnp.exp(b_lasts[hh] - bs[hh]), TN))
        for hh in heads_here:
            o = os_[hh]
            orec_ref[:, lanes[hh]] = o
            rn = lax.rsqrt(jnp.mean(o * o, axis=-1, keepdims=True) + NORM_EPS)
            hg = hg_ref[:, lanes[hh]].astype(F32)
            oh_ref[:, lanes[hh]] = (((o * rn) * gn_ref[...]) * (hg * _sigmoid(hg))).astype(BF16)

    col = lambda grp: pl.BlockSpec((CHUNK, per * HEAD), lambda h, c: (c, grp * ng + h))
    const = lambda shape: pl.BlockSpec(shape, lambda h, c: (0, 0))
    return pl.pallas_call(
        body, name="hgrn_fwd",
        out_shape=(jax.ShapeDtypeStruct((L, D), BF16), jax.ShapeDtypeStruct((L, D), F32),
                   jax.ShapeDtypeStruct((nc, nh, HEAD, HEAD), F32)),
        grid=(ng, nc),
        in_specs=[col(0), col(1), col(2), col(3),
                  pl.BlockSpec((2, per * HEAD), lambda h, c: (0, h)), const((1, HEAD)),
                  const((CHUNK, CHUNK)), const((LEVELS * CHUNK, CHUNK)), const((LEVELS * CHUNK, CHUNK))],
        out_specs=(pl.BlockSpec((CHUNK, per * HEAD), lambda h, c: (c, h)),
                   pl.BlockSpec((CHUNK, per * HEAD), lambda h, c: (c, h)),
                   pl.BlockSpec((1, per, HEAD, HEAD), lambda h, c: (c, h, 0, 0))),
        scratch_shapes=[pltpu.VMEM((per, HEAD, HEAD), F32), pltpu.VMEM((per, CHUNK, HEAD), F32)],
        compiler_params=_params("parallel", "arbitrary"),
    )(hp, hp, hp, hp, lb_raw, g_norm, tril, sel, blk)


def _hgrn_bwd(hp, lb_raw, g_norm, do_h, o_rec, s_hist, nh):
    L = hp.shape[0]
    D = nh * HEAD
    nc = L // CHUNK
    per = _tile(nh, (HGRN_BWD_HEADS, 1))
    ng = nh // per
    tril, sel, blk = _hgrn_constants()
    tdec = _hgrn_decay_grad_constants()

    def body(hq_ref, hf_ref, hi_ref, hg_ref, lb_ref, gn_ref, tril_ref, sel_ref, blk_ref, tdec_ref,
             do_ref, orec_ref, shist_ref, dhq_ref, dhf_ref, dhi_ref, dhg_ref, dgn_ref, dlb_ref, ds_sc, b_sc):
        ci = pl.program_id(1)
        c = nc - 1 - ci

        @pl.when(ci == 0)
        def _():
            ds_sc[...] = jnp.zeros_like(ds_sc)
            dgn_ref[...] = jnp.zeros_like(dgn_ref)
            dlb_ref[...] = jnp.zeros_like(dlb_ref)

        heads_here = range(per)
        lanes = [slice(hh * HEAD, (hh + 1) * HEAD) for hh in heads_here]
        ws = [_hgrn_gates(hq_ref, hf_ref, hi_ref, lb_ref, lanes[hh], c * CHUNK) for hh in heads_here]
        qs, ks, vs = [w["q"] for w in ws], [w["k"] for w in ws], [w["v"] for w in ws]
        bs, bms = _hgrn_prefix(tril_ref, sel_ref, [w["g"] for w in ws])
        for hh in heads_here:
            b_sc[hh] = bs[hh]
        b_lasts = [b_sc[hh, CHUNK - 1:CHUNK, :] for hh in heads_here]
        levels = [_hgrn_levels(qs[hh], ks[hh], bs[hh], bms[hh], blk_ref) for hh in heads_here]
        ps = _hgrn_intra(qs, ks, levels)
        s_ins = [shist_ref[0, hh] for hh in heads_here]
        ds_outs = [ds_sc[hh] for hh in heads_here]
        ebs = [jnp.exp(b) for b in bs]
        etails = [jnp.exp(b_lasts[hh] - bs[hh]) for hh in heads_here]
        decays = [jnp.exp(bl) for bl in b_lasts]

        dos = []
        for hh in heads_here:
            o = orec_ref[:, lanes[hh]]
            hg = hg_ref[:, lanes[hh]].astype(F32)
            sgg = _sigmoid(hg)
            rn = lax.rsqrt(jnp.mean(o * o, axis=-1, keepdims=True) + NORM_EPS)
            on = o * rn
            doh = do_ref[:, lanes[hh]]
            dy = doh * (hg * sgg)
            dhg_ref[:, lanes[hh]] = (doh * (on * gn_ref[...]) * (sgg * (1.0 + hg * (1.0 - sgg)))).astype(BF16)
            dgn_ref[hh] += jnp.broadcast_to(jnp.sum(dy * on, axis=0, keepdims=True), (8, HEAD))
            z = dy * gn_ref[...]
            dos.append(rn * z - o * ((rn * rn * rn) * jnp.mean(z * o, axis=-1, keepdims=True)))

        tt = lax.broadcasted_iota(jnp.int32, (CHUNK, CHUNK), 0)
        ss = lax.broadcasted_iota(jnp.int32, (CHUNK, CHUNK), 1)
        dps = [jnp.where(ss <= tt, _dot(dos[hh], vs[hh], NT), 0.0) for hh in heads_here]
        dvs = [_dot(ps[hh], dos[hh], TN) + _dot(ks[hh] * etails[hh], ds_outs[hh], NT) for hh in heads_here]
        dq_states = [ebs[hh] * _dot(dos[hh], s_ins[hh], NN) for hh in heads_here]
        dk_states = [etails[hh] * _dot(vs[hh], ds_outs[hh], NN) for hh in heads_here]
        dpds = [jnp.sum(jnp.where(tt == ss, dp, 0.0), axis=-1, keepdims=True) for dp in dps]
        dqs = [dpds[hh] * ks[hh] + dq_states[hh] for hh in heads_here]
        dks = [dpds[hh] * qs[hh] + dk_states[hh] for hh in heads_here]
        pair_terms = [[] for _ in heads_here]
        for lv in range(LEVELS):
            for hh in heads_here:
                eq, ek, ql, kl, same = levels[hh][lv]
                dpl = dps[hh] * same
                dq_l = eq * _dot(dpl, kl, NN)
                dk_l = ek * _dot(dpl, ql, TN)
                dqs[hh] = dqs[hh] + dq_l
                dks[hh] = dks[hh] + dk_l
                pair_terms[hh].append(qs[hh] * dq_l + ks[hh] * dk_l)
        for hh in heads_here:
            pair_terms[hh] += [qs[hh] * dq_states[hh], ks[hh] * dk_states[hh]]
            ds_sc[hh] = decays[hh] * ds_outs[hh] + _dot(dos[hh], qs[hh] * ebs[hh], TN)
        stacked = [jnp.concatenate(terms, axis=0) for terms in pair_terms]
        dg_all = _dot_exact_rhs(tdec_ref[...], jnp.concatenate(stacked, axis=1) if per > 1 else stacked[0], NN)

        for hh in heads_here:
            w = ws[hh]
            f, sg, lb, sq, hq = w["f"], w["sg"], w["lb"], w["sq"], w["hq"]
            through = jnp.sum((decays[hh] * s_ins[hh]) * ds_outs[hh], axis=0, keepdims=True)
            dg = dg_all[:, lanes[hh]] + through
            df = jnp.where(w["valid"], dg / f - dks[hh], 0.0)
            dhf_ref[:, lanes[hh]] = (df * (1.0 - lb) * sg * (1.0 - sg)).astype(BF16)
            dlb_ref[hh] += jnp.broadcast_to(jnp.sum(df * (1.0 - sg), axis=0, keepdims=True), (8, HEAD))
            dhq_ref[:, lanes[hh]] = (dqs[hh] * (sq * (1.0 + hq * (1.0 - sq)))).astype(BF16)
            dhi_ref[:, lanes[hh]] = dvs[hh].astype(BF16)

    col = lambda grp: pl.BlockSpec((CHUNK, per * HEAD), lambda h, c: (nc - 1 - c, grp * ng + h))
    const = lambda shape: pl.BlockSpec(shape, lambda h, c: (0, 0))
    tile = pl.BlockSpec((CHUNK, per * HEAD), lambda h, c: (nc - 1 - c, h))
    part = pl.BlockSpec((per, 8, HEAD), lambda h, c: (h, 0, 0))
    return pl.pallas_call(
        body, name="hgrn_bwd",
        out_shape=tuple([jax.ShapeDtypeStruct((L, D), BF16)] * 4 + [jax.ShapeDtypeStruct((nh, 8, HEAD), F32)] * 2),
        grid=(ng, nc),
        in_specs=[col(0), col(1), col(2), col(3),
                  pl.BlockSpec((2, per * HEAD), lambda h, c: (0, h)), const((1, HEAD)),
                  const((CHUNK, CHUNK)), const((LEVELS * CHUNK, CHUNK)), const((LEVELS * CHUNK, CHUNK)),
                  const((CHUNK, (LEVELS + 2) * CHUNK)),
                  tile, tile, pl.BlockSpec((1, per, HEAD, HEAD), lambda h, c: (nc - 1 - c, h, 0, 0))],
        out_specs=(tile, tile, tile, tile, part, part),
        scratch_shapes=[pltpu.VMEM((per, HEAD, HEAD), F32), pltpu.VMEM((per, CHUNK, HEAD), F32)],
        compiler_params=_params("parallel", "arbitrary"),
    )(hp, hp, hp, hp, lb_raw, g_norm, tril, sel, blk, tdec, do_h, o_rec, s_hist)


def _merge_fwd(a, bm, gates):
    L, D = a.shape
    tm = _tile(L, (128,))

    def body(a_ref, b_ref, g_ref, o_ref):
        sa, sb = _sigmoid(g_ref[:, :D].astype(F32)), _sigmoid(g_ref[:, D:].astype(F32))
        o_ref[...] = (sa * a_ref[...] + sb * b_ref[...]).astype(BF16)

    row = lambda w: pl.BlockSpec((tm, w), lambda i: (i, 0))
    return pl.pallas_call(
        body, name="merge_fwd", out_shape=jax.ShapeDtypeStruct((L, D), BF16), grid=(L // tm,),
        in_specs=[row(D), row(D), row(2 * D)], out_specs=row(D), compiler_params=_params("parallel"),
    )(a, bm, gates)


def _merge_bwd(dm, a, bm, gates):
    L, D = a.shape
    tm = _tile(L, (128,))

    def body(dm_ref, a_ref, b_ref, g_ref, da_ref, db_ref, dg_ref):
        d = dm_ref[...]
        sa, sb = _sigmoid(g_ref[:, :D].astype(F32)), _sigmoid(g_ref[:, D:].astype(F32))
        da_ref[...] = (d * sa).astype(BF16)
        db_ref[...] = (d * sb).astype(BF16)
        dg_ref[:, :D] = (d * a_ref[...] * sa * (1.0 - sa)).astype(BF16)
        dg_ref[:, D:] = (d * b_ref[...] * sb * (1.0 - sb)).astype(BF16)

    row = lambda w: pl.BlockSpec((tm, w), lambda i: (i, 0))
    return pl.pallas_call(
        body, name="merge_bwd",
        out_shape=(jax.ShapeDtypeStruct((L, D), BF16), jax.ShapeDtypeStruct((L, D), BF16),
                   jax.ShapeDtypeStruct((L, 2 * D), BF16)),
        grid=(L // tm,),
        in_specs=[row(D), row(D), row(D), row(2 * D)], out_specs=(row(D), row(D), row(2 * D)),
        compiler_params=_params("parallel"),
    )(dm, a, bm, gates)


def _conv_taps(g_ref, halo_ref, i, tm):
    rows = i * tm + lax.broadcasted_iota(jnp.int32, (tm, 1), 0)
    g0 = jnp.where(rows >= PAD_LEN, g_ref[...].astype(F32), 0.0)
    sub = lax.broadcasted_iota(jnp.int32, (HALO, 1), 0)
    hrow = i * tm - HALO + sub
    halo = jnp.where(hrow >= PAD_LEN, halo_ref[...].astype(F32), 0.0)
    r = lax.broadcasted_iota(jnp.int32, (tm, 1), 0)
    h7 = jnp.sum(jnp.where(sub == HALO - 1, halo, 0.0), axis=0, keepdims=True)
    h6 = jnp.sum(jnp.where(sub == HALO - 2, halo, 0.0), axis=0, keepdims=True)
    g1 = jnp.where(r == 0, h7, pltpu.roll(g0, 1, 0))
    g2 = jnp.where(r == 0, h6, jnp.where(r == 1, h7, pltpu.roll(g0, 2, 0)))
    return g0, g1, g2


def _conv_fwd(gu, cw, cb):
    L, F2 = gu.shape
    F = F2 // 2
    tm = _tile(L, ROW_TILES)
    tn = _tile(F, (512, 256, 128))
    nj = F // tn

    def body(g_ref, halo_ref, u_ref, cw_ref, cb_ref, o_ref):
        g0, g1, g2 = _conv_taps(g_ref, halo_ref, pl.program_id(0), tm)
        conv = cw_ref[0:1, :] * g2 + cw_ref[1:2, :] * g1 + cw_ref[2:3, :] * g0 + cb_ref[...]
        o_ref[...] = (conv * _sigmoid(conv) * u_ref[...].astype(F32)).astype(BF16)

    return pl.pallas_call(
        body, name="conv_fwd", out_shape=jax.ShapeDtypeStruct((L, F), BF16), grid=(L // tm, nj),
        in_specs=[pl.BlockSpec((tm, tn), lambda i, j: (i, j)),
                  pl.BlockSpec((HALO, tn), lambda i, j: (jnp.maximum(i * (tm // HALO) - 1, 0), j)),
                  pl.BlockSpec((tm, tn), lambda i, j: (i, j + nj)),
                  pl.BlockSpec((3, tn), lambda i, j: (0, j)),
                  pl.BlockSpec((1, tn), lambda i, j: (0, j))],
        out_specs=pl.BlockSpec((tm, tn), lambda i, j: (i, j)),
        compiler_params=_params("parallel", "parallel"),
    )(gu, gu, gu, cw, cb)


def _conv_bwd_a(da, gu, cw, cb):
    L, F2 = gu.shape
    F = F2 // 2
    tm = _tile(L, ROW_TILES)
    tn = _tile(F, (512, 256, 128))
    nj = F // tn

    def body(da_ref, g_ref, halo_ref, u_ref, cw_ref, cb_ref, dc_ref, du_ref, dcb_ref, dcw_ref):
        i = pl.program_id(1)

        @pl.when(i == 0)
        def _():
            dcb_ref[...] = jnp.zeros_like(dcb_ref)
            dcw_ref[...] = jnp.zeros_like(dcw_ref)

        g0, g1, g2 = _conv_taps(g_ref, halo_ref, i, tm)
        conv = cw_ref[0:1, :] * g2 + cw_ref[1:2, :] * g1 + cw_ref[2:3, :] * g0 + cb_ref[...]
        sc = _sigmoid(conv)
        dav = da_ref[...]
        du_ref[...] = (dav * (conv * sc)).astype(BF16)
        dconv = dav * u_ref[...].astype(F32) * (sc * (1.0 + conv * (1.0 - sc)))
        dc_ref[...] = dconv
        dcb_ref[...] += jnp.sum(dconv, axis=0, keepdims=True)
        dcw_ref[0:1, :] += jnp.sum(dconv * g2, axis=0, keepdims=True)
        dcw_ref[1:2, :] += jnp.sum(dconv * g1, axis=0, keepdims=True)
        dcw_ref[2:3, :] += jnp.sum(dconv * g0, axis=0, keepdims=True)

    return pl.pallas_call(
        body, name="conv_bwd_a",
        out_shape=(jax.ShapeDtypeStruct((L, F), F32), jax.ShapeDtypeStruct((L, 2 * F), BF16),
                   jax.ShapeDtypeStruct((1, F), F32), jax.ShapeDtypeStruct((8, F), F32)),
        grid=(nj, L // tm),
        in_specs=[pl.BlockSpec((tm, tn), lambda j, i: (i, j)),
                  pl.BlockSpec((tm, tn), lambda j, i: (i, j)),
                  pl.BlockSpec((HALO, tn), lambda j, i: (jnp.maximum(i * (tm // HALO) - 1, 0), j)),
                  pl.BlockSpec((tm, tn), lambda j, i: (i, j + nj)),
                  pl.BlockSpec((3, tn), lambda j, i: (0, j)),
                  pl.BlockSpec((1, tn), lambda j, i: (0, j))],
        out_specs=(pl.BlockSpec((tm, tn), lambda j, i: (i, j)), pl.BlockSpec((tm, tn), lambda j, i: (i, j + nj)),
                   pl.BlockSpec((1, tn), lambda j, i: (0, j)), pl.BlockSpec((8, tn), lambda j, i: (0, j))),
        compiler_params=_params("parallel", "arbitrary"),
    )(da, gu, gu, gu, cw, cb)


def _conv_bwd_b(dconv, cw, dgu):
    L, F = dconv.shape
    tm = _tile(L, ROW_TILES)
    tn = _tile(F, (512, 256, 128))
    nblk8 = L // 8
    ni = L // tm

    def body(dc_ref, nxt_ref, cw_ref, dgu_ref, o_ref):
        i = pl.program_id(0)
        dc = dc_ref[...]
        nxt = jnp.where(i < ni - 1, nxt_ref[...], 0.0)
        sub = lax.broadcasted_iota(jnp.int32, (8, 1), 0)
        n0 = jnp.sum(jnp.where(sub == 0, nxt, 0.0), axis=0, keepdims=True)
        n1 = jnp.sum(jnp.where(sub == 1, nxt, 0.0), axis=0, keepdims=True)
        r = lax.broadcasted_iota(jnp.int32, (tm, 1), 0)
        d1 = jnp.where(r == tm - 1, n0, pltpu.roll(dc, tm - 1, 0))
        d2 = jnp.where(r == tm - 2, n0, jnp.where(r == tm - 1, n1, pltpu.roll(dc, tm - 2, 0)))
        dg = cw_ref[2:3, :] * dc + cw_ref[1:2, :] * d1 + cw_ref[0:1, :] * d2
        rows = i * tm + r
        o_ref[...] = jnp.where(rows >= PAD_LEN, dg, 0.0).astype(BF16)

    return pl.pallas_call(
        body, name="conv_bwd_b", out_shape=jax.ShapeDtypeStruct(dgu.shape, BF16), grid=(ni, F // tn),
        in_specs=[pl.BlockSpec((tm, tn), lambda i, j: (i, j)),
                  pl.BlockSpec((8, tn), lambda i, j: (jnp.minimum((i + 1) * (tm // 8), nblk8 - 1), j)),
                  pl.BlockSpec((3, tn), lambda i, j: (0, j)), ANY],
        out_specs=pl.BlockSpec((tm, tn), lambda i, j: (i, j)),
        input_output_aliases={3: 0},
        compiler_params=_params("parallel", "parallel"),
    )(dconv, dconv, cw, dgu)


ANY = pl.BlockSpec(memory_space=pl.ANY)


def _coords():
    return lax.axis_index("x"), lax.axis_index("y"), lax.axis_index("c")


def _flip(v, bit):
    return 1 - v if bit else v


CHIPS = [(1, 0), (0, 1), (1, 1)]
PEERS = [(dx, dy, dc) for dx in (0, 1) for dy in (0, 1) for dc in (0, 1)][1:]


class _Rider:
    def __init__(self, operands, out_shapes, sem_shapes, start, finish):
        self.operands, self.out_shapes, self.sem_shapes = list(operands), list(out_shapes), list(sem_shapes)
        self.start, self.finish = start, finish


def _run_rider(rider, name):
    n_in, n_out = len(rider.operands), len(rider.out_shapes)

    def body(*refs):
        ins, outs, sems = refs[:n_in], refs[n_in:n_in + n_out], refs[n_in + n_out:]
        rider.start(ins, outs, sems)
        rider.finish(ins, outs, sems)

    return pl.pallas_call(
        body, name=name, out_shape=tuple(rider.out_shapes),
        in_specs=[ANY] * n_in, out_specs=tuple([ANY] * n_out), scratch_shapes=rider.sem_shapes,
        compiler_params=pltpu.CompilerParams(has_side_effects=True),
    )(*rider.operands)


def _gather_rider(big, small):
    nbig, n = len(big), len(big) + len(small)
    arrays = list(big) + list(small)

    def plan(ins, outs, sems):
        ici_send, ici_recv, d2d_send, d2d_recv, local_sems = sems
        x, y, c = _coords()
        mine = 2 * x + y

        def half(w, h):
            r2 = arrays[w].shape[0] // 2
            return pl.ds(h * r2, r2)

        def ici(w, j, landing):
            px, py = _flip(x, CHIPS[j][0]), _flip(y, CHIPS[j][1])
            slot = 2 * px + py if landing else mine
            if w < nbig:
                src, dst = ins[w].at[half(w, c)], outs[w].at[slot, half(w, c)]
            else:
                src, dst = ins[w], outs[w].at[slot]
            return pltpu.make_async_remote_copy(
                src_ref=src, dst_ref=dst, send_sem=ici_send.at[w * 3 + j], recv_sem=ici_recv.at[w * 3 + j],
                device_id=(px, py, c), device_id_type=MESH)

        def d2d(w, j, landing):
            px, py = _flip(x, CHIPS[j][0]), _flip(y, CHIPS[j][1])
            mine_rows = outs[w].at[2 * px + py, half(w, c)]
            dst = outs[w].at[2 * px + py, half(w, 1 - c)] if landing else mine_rows
            return pltpu.make_async_remote_copy(
                src_ref=mine_rows, dst_ref=dst, send_sem=d2d_send.at[w * 3 + j], recv_sem=d2d_recv.at[w * 3 + j],
                device_id=(x, y, 1 - c), device_id_type=MESH)

        local = [pltpu.make_async_copy(ins[w], outs[w].at[mine], local_sems.at[w]) for w in range(n)]
        return ici, d2d, local

    def start(ins, outs, sems):
        ici, _, local = plan(ins, outs, sems)
        for cp in local:
            cp.start()
        for w in range(n):
            for j in range(3):
                ici(w, j, False).start()

    def finish(ins, outs, sems):
        ici, d2d, local = plan(ins, outs, sems)
        for w in range(n):
            for j in range(3):
                ici(w, j, True).wait_recv()
                if w < nbig:
                    d2d(w, j, False).start()
        for w in range(nbig):
            for j in range(3):
                d2d(w, j, True).wait_recv()
        for w in range(n):
            for j in range(3):
                ici(w, j, False).wait_send()
                if w < nbig:
                    d2d(w, j, False).wait_send()
        for cp in local:
            cp.wait()

    return _Rider(
        arrays, [jax.ShapeDtypeStruct((4,) + s.shape, s.dtype) for s in arrays],
        [pltpu.SemaphoreType.DMA((3 * n,)), pltpu.SemaphoreType.DMA((3 * n,)),
         pltpu.SemaphoreType.DMA((max(3 * nbig, 1),)), pltpu.SemaphoreType.DMA((max(3 * nbig, 1),)),
         pltpu.SemaphoreType.DMA((n,))],
        start, finish)


def _to_sibling(arrays, name):
    n = len(arrays)

    def body(*refs):
        ins, outs = refs[:n], refs[n:2 * n]
        send_sems, recv_sems = refs[2 * n:]
        x, y, c = _coords()

        def copy(w):
            return pltpu.make_async_remote_copy(
                src_ref=ins[w], dst_ref=outs[w], send_sem=send_sems.at[w], recv_sem=recv_sems.at[w],
                device_id=(x, y, 1 - c), device_id_type=MESH)

        for w in range(n):
            copy(w).start()
        for w in range(n):
            copy(w).wait_recv()
            copy(w).wait_send()

    return pl.pallas_call(
        body, name=name,
        out_shape=tuple(jax.ShapeDtypeStruct(a.shape, a.dtype) for a in arrays),
        in_specs=[ANY] * n, out_specs=tuple([ANY] * n),
        scratch_shapes=[pltpu.SemaphoreType.DMA((n,)), pltpu.SemaphoreType.DMA((n,))],
        compiler_params=pltpu.CompilerParams(has_side_effects=True),
    )(*arrays)


def _pair_sum(a, b, name):
    _, r, c = a.shape
    tr = _tile(r, (64, 32, 16))

    def body(a_ref, b_ref, o_ref):
        o_ref[...] = (a_ref[...].astype(F32) + b_ref[...].astype(F32)).astype(BF16)

    blk = pl.BlockSpec((4, tr, c), lambda i: (0, i, 0))
    return pl.pallas_call(
        body, name=name, out_shape=jax.ShapeDtypeStruct(a.shape, BF16), grid=(r // tr,),
        in_specs=[blk, blk], out_specs=blk, compiler_params=_params("parallel"),
    )(a, b)


def _scatter_rider(parts):
    n = len(parts)

    def plan(ins, outs, sems):
        send_sems, recv_sems, local_sems = sems
        x, y, c = _coords()
        mine = 2 * x + y

        def ici(w, j, landing):
            px, py = _flip(x, CHIPS[j][0]), _flip(y, CHIPS[j][1])
            return pltpu.make_async_remote_copy(
                src_ref=ins[w].at[2 * px + py], dst_ref=outs[w].at[2 * px + py if landing else mine],
                send_sem=send_sems.at[w * 3 + j], recv_sem=recv_sems.at[w * 3 + j],
                device_id=(px, py, c), device_id_type=MESH)

        local = [pltpu.make_async_copy(ins[w].at[mine], outs[w].at[mine], local_sems.at[w]) for w in range(n)]
        return ici, local

    def start(ins, outs, sems):
        ici, local = plan(ins, outs, sems)
        for cp in local:
            cp.start()
        for w in range(n):
            for j in range(3):
                ici(w, j, False).start()

    def finish(ins, outs, sems):
        ici, local = plan(ins, outs, sems)
        for w in range(n):
            for j in range(3):
                ici(w, j, True).wait_recv()
                ici(w, j, False).wait_send()
        for cp in local:
            cp.wait()

    return _Rider(
        parts, [jax.ShapeDtypeStruct(p.shape, p.dtype) for p in parts],
        [pltpu.SemaphoreType.DMA((3 * n,)), pltpu.SemaphoreType.DMA((3 * n,)), pltpu.SemaphoreType.DMA((n,))],
        start, finish)


def _sum4(recv, name):
    _, r, c = recv.shape
    tr = _tile(r, (64, 32, 16))

    def body(in_ref, o_ref):
        o_ref[...] = ((in_ref[0].astype(F32) + in_ref[1].astype(F32)) + in_ref[2].astype(F32)) + in_ref[3].astype(F32)

    return pl.pallas_call(
        body, name=name, out_shape=jax.ShapeDtypeStruct((r, c), F32), grid=(r // tr,),
        in_specs=[pl.BlockSpec((4, tr, c), lambda i: (0, i, 0))],
        out_specs=pl.BlockSpec((tr, c), lambda i: (i, 0)),
        compiler_params=_params("parallel"),
    )(recv)


def _allreduce_small(packed):
    R = packed.shape[0]

    def body(in_ref, o_ref, buf, send_sems, recv_sems):
        x, y, c = _coords()
        me = 4 * x + 2 * y + c
        buf[me] = in_ref[...]
        for j, (dx, dy, dc) in enumerate(PEERS):
            px, py, pc = _flip(x, dx), _flip(y, dy), _flip(c, dc)
            pltpu.make_async_remote_copy(
                src_ref=in_ref, dst_ref=buf.at[me], send_sem=send_sems.at[j], recv_sem=recv_sems.at[j],
                device_id=(px, py, pc), device_id_type=MESH).start()
        for j, (dx, dy, dc) in enumerate(PEERS):
            px, py, pc = _flip(x, dx), _flip(y, dy), _flip(c, dc)
            rc = pltpu.make_async_remote_copy(
                src_ref=in_ref, dst_ref=buf.at[4 * px + 2 * py + pc], send_sem=send_sems.at[j],
                recv_sem=recv_sems.at[j], device_id=(px, py, pc), device_id_type=MESH)
            rc.wait_recv()
            rc.wait_send()
        acc = buf[0]
        for s in range(1, 8):
            acc = acc + buf[s]
        o_ref[...] = acc

    return pl.pallas_call(
        body, name="allreduce_small", out_shape=jax.ShapeDtypeStruct((R, 128), F32),
        in_specs=[pl.BlockSpec(memory_space=pltpu.VMEM)], out_specs=pl.BlockSpec(memory_space=pltpu.VMEM),
        scratch_shapes=[pltpu.VMEM((8, R, 128), F32), pltpu.SemaphoreType.DMA((7,)), pltpu.SemaphoreType.DMA((7,))],
        compiler_params=pltpu.CompilerParams(has_side_effects=True, vmem_limit_bytes=VMEM_LIMIT_BYTES),
    )(packed)


def _adamw_math(w, g, m, v):
    m = ADAM_B1 * m + (1.0 - ADAM_B1) * g
    v = ADAM_B2 * v + (1.0 - ADAM_B2) * (g * g)
    m_hat = m / (1.0 - ADAM_B1 ** ADAM_STEP)
    v_hat = v / (1.0 - ADAM_B2 ** ADAM_STEP)
    delta = -ADAM_LR * (m_hat / (jnp.sqrt(v_hat) + ADAM_EPS) + ADAM_WD * w)
    return delta, m, v


def _adamw_big(mine, other, w, m, v, name):
    R, C = w.shape
    tr = _tile(R // 2, (128, 64, 32, 16, 8))
    nb = (R // 2) // tr

    def body(mine_ref, other_ref, w_ref, m_ref, v_ref, g_ref, d_ref, mo_ref, vo_ref):
        is_mine = (pl.program_id(0) // nb) == lax.axis_index("c")
        g = jnp.where(is_mine, mine_ref[...], other_ref[...])
        d, mn, vn = _adamw_math(w_ref[...], g, m_ref[...], v_ref[...])
        g_ref[...] = g
        d_ref[...] = d
        mo_ref[...] = mn
        vo_ref[...] = vn

    blk = pl.BlockSpec((tr, C), lambda i: (i, 0))
    half = pl.BlockSpec((tr, C), lambda i: (i % nb, 0))
    sds = jax.ShapeDtypeStruct((R, C), F32)
    return pl.pallas_call(
        body, name=name, out_shape=(sds, sds, sds, sds), grid=(2 * nb,),
        in_specs=[half, half, blk, blk, blk], out_specs=(blk, blk, blk, blk), compiler_params=_params("parallel"),
    )(mine, other, w, m, v)


def _adamw_small(items, lb_raw, dlb):
    n = len(items)
    lb_w, lb_m, lb_v = lb_raw

    def body(*refs):
        ins = refs[:4 * n]
        dlb_ref, lw_ref, lm_ref, lv_ref = refs[4 * n:4 * n + 4]
        outs = refs[4 * n + 4:]
        for t in range(n):
            g_ref, w_ref, m_ref, v_ref = ins[4 * t:4 * t + 4]
            d, mn, vn = _adamw_math(w_ref[...], g_ref[...], m_ref[...], v_ref[...])
            outs[3 * t][...] = d
            outs[3 * t + 1][...] = mn
            outs[3 * t + 2][...] = vn
        p0 = 1.0 / (1.0 + jnp.exp(lw_ref[1:2, :] - lw_ref[0:1, :]))
        g0 = dlb_ref[...] * p0 * (1.0 - p0)
        base = 3 * n
        outs[base][0:1, :] = g0
        outs[base][1:2, :] = -g0
        d, mn, vn = _adamw_math(lw_ref[...], outs[base][...], lm_ref[...], lv_ref[...])
        outs[base + 1][...] = d
        outs[base + 2][...] = mn
        outs[base + 3][...] = vn

    operands = [a for it in items for a in it] + [dlb, lb_w, lb_m, lb_v]
    out_shape = []
    for (g, w, m, v) in items:
        out_shape += [jax.ShapeDtypeStruct(w.shape, F32)] * 3
    out_shape += [jax.ShapeDtypeStruct(lb_w.shape, F32)] * 4
    vm = pl.BlockSpec(memory_space=pltpu.VMEM)
    res = pl.pallas_call(
        body, name="adamw_small", out_shape=tuple(out_shape),
        in_specs=[vm] * len(operands), out_specs=tuple([vm] * len(out_shape)),
        compiler_params=pltpu.CompilerParams(vmem_limit_bytes=VMEM_LIMIT_BYTES),
    )(*operands)
    deltas = [res[3 * t] for t in range(n)] + [res[3 * n + 1]]
    new_m = [res[3 * t + 1] for t in range(n)] + [res[3 * n + 2]]
    new_v = [res[3 * t + 2] for t in range(n)] + [res[3 * n + 3]]
    return res[3 * n], deltas, new_m, new_v


def _shard_row_half(g, by_cols, h):
    if isinstance(g, (list, tuple)):
        R, widths = g[0].shape[0], [s.shape[1] for s in g]
        starts = [sum(widths[:n]) for n in range(len(g))]
        cs = sum(widths) // 4
        rows = [lax.dynamic_slice_in_dim(s, h * (R // 2), R // 2, axis=0) for s in g]
        shards = []
        for k in range(4):
            lo, hi = k * cs, (k + 1) * cs
            cut = [r[:, max(lo - o, 0):min(hi - o, w)] for r, o, w in zip(rows, starts, widths) if lo < o + w and hi > o]
            shards.append(jnp.concatenate(cut, axis=1).astype(BF16))
        return jnp.stack(shards)
    if g.ndim == 3:
        return lax.dynamic_slice_in_dim(g, h * (g.shape[1] // 2), g.shape[1] // 2, axis=1).astype(BF16)
    R, C = g.shape
    if by_cols:
        part = lax.dynamic_index_in_dim(g.reshape(2, R // 2, 4, C // 4), h, axis=0, keepdims=False)
        return part.transpose(1, 0, 2).astype(BF16)
    return lax.dynamic_index_in_dim(g.reshape(4, 2, R // 8, C), h, axis=1, keepdims=False).astype(BF16)


def kernel(x, positions, meta_tokens, w_in, w_q_up, w_kv_up, w_branch_mla, w_branch_hgrn, w_out, w_ffn_in, w_ffn_out, conv_w, conv_b, g_mix_norm, g_q_norm, g_kv_norm, g_hgrn_norm, g_ffn_norm, g_final_norm, lb_raw, loss_target, m_meta_tokens, m_w_in, m_w_q_up, m_w_kv_up, m_w_branch_mla, m_w_branch_hgrn, m_w_out, m_w_ffn_in, m_w_ffn_out, m_conv_w, m_conv_b, m_g_mix_norm, m_g_q_norm, m_g_kv_norm, m_g_hgrn_norm, m_g_ffn_norm, m_g_final_norm, m_lb_raw, v_meta_tokens, v_w_in, v_w_q_up, v_w_kv_up, v_w_branch_mla, v_w_branch_hgrn, v_w_out, v_w_ffn_in, v_w_ffn_out, v_conv_w, v_conv_b, v_g_mix_norm, v_g_q_norm, v_g_kv_norm, v_g_hgrn_norm, v_g_ffn_norm, v_g_final_norm, v_lb_raw):
    S, D = x.shape[1], x.shape[2]
    L = S + PREFIX
    QL, KVL = g_q_norm.shape[1], g_kv_norm.shape[1]
    F = conv_b.shape[1]
    heads = (4 * w_kv_up.shape[2]) // QPAD
    nh = D // HEAD
    assert lb_raw.shape[0] == 2 and g_hgrn_norm.shape[1] == HEAD and L % CHUNK == 0
    ix, iy, ic = _coords()
    chip = 2 * ix + iy

    big = [w_in, w_q_up, w_kv_up, w_branch_mla, w_branch_hgrn, w_out, w_ffn_in, w_ffn_out]
    col_sharded = [True, True, True, False, False, False, True, False]
    shards = [w[0].astype(BF16) for w in big]
    early = _run_rider(_gather_rider(shards[:1], [meta_tokens, conv_w[0]]), "gather_early")
    qkv_rider = _gather_rider(shards[1:3], [])
    late_rider = _gather_rider(shards[3:], [])

    def full(gw, by_cols):
        _, r, c = gw.shape
        return gw.transpose(1, 0, 2).reshape(r, 4 * c) if by_cols else gw.reshape(4 * r, c)

    def col_range(g4, lo, hi):
        c = g4.shape[2]
        pieces = [g4[k][:, max(lo - k * c, 0):min(hi - k * c, c)] for k in range(4) if lo < (k + 1) * c and hi > k * c]
        return pieces[0] if len(pieces) == 1 else jnp.concatenate(pieces, axis=1)

    meta_full = full(early[1], True)
    cw_full = full(early[2], True)
    c0 = QL + KVL
    W_lat = col_range(early[0], 0, c0)
    W_kr = jnp.pad(col_range(early[0], c0, c0 + ROPE), ((0, 0), (0, 128 - ROPE)))
    W_H = col_range(early[0], c0 + ROPE, c0 + ROPE + 4 * D)
    W_G = col_range(early[0], c0 + ROPE + 4 * D, 4 * early[0].shape[2])

    pos = jnp.concatenate([jnp.zeros((PAD_LEN,), jnp.int32), jnp.arange(N_META, dtype=jnp.int32),
                           positions[0].astype(jnp.int32) + N_META])
    inv = 1.0 / (ROPE_THETA ** (jnp.arange(0, ROPE, 2, dtype=F32) / ROPE))
    ang = pos.astype(F32)[:, None] * inv
    zero = jnp.zeros((L, 128 - ROPE), F32)
    cos = jnp.concatenate([jnp.cos(ang), jnp.cos(ang), zero], axis=1)
    sin = jnp.concatenate([jnp.sin(ang), jnp.sin(ang), zero], axis=1)

    h0, u1 = _embed_norm(x[0], meta_full, g_mix_norm)
    hp, qkv = _mm(u1, W_H, "nn", BF16, "proj_hgrn", rider=qkv_rider)
    W_q, W_kv = [full(gw, True) for gw in qkv]
    W_qp = jnp.pad(W_q.reshape(QL, heads, QK_HEAD), ((0, 0), (0, 0), (0, QPAD - QK_HEAD))).reshape(QL, heads * QPAD)
    lat = _mm(u1, W_lat, "nn", BF16, "proj_lat")
    gates = _mm(u1, W_G, "nn", BF16, "proj_gates")
    kr = _mm(u1, W_kr, "nn", F32, "proj_krope")
    qn = _rmsnorm_fwd(lat, g_q_norm, "norm_q", col0=0, width=QL)
    kvn = _rmsnorm_fwd(lat, g_kv_norm, "norm_kv", col0=QL, width=KVL)
    qp = _mm(qn, W_qp, "nn", BF16, "q_up")
    kv = _mm(kvn, W_kv, "nn", BF16, "kv_up")
    qc, kc, vv = _rope_fwd(qp, kv, kr, cos, sin, heads)
    o_mla, lse, late = _attn_fwd(qc, kc, vv, heads, late_rider)
    W_a, W_b, W_o, W_fi, W_fo = [full(gw, bc) for gw, bc in zip(late, col_sharded[3:])]
    o_hgrn, o_rec, s_hist = _hgrn_fwd(hp, lb_raw, g_hgrn_norm, nh)
    br_a = _mm(o_mla, W_a, "nn", BF16, "branch_mla")
    br_b = _mm(o_hgrn, W_b, "nn", BF16, "branch_hgrn")
    merged = _merge_fwd(br_a, br_b, gates)
    h1 = _mm(merged, W_o, "nn", F32, "out_proj", res=h0)
    u2 = _rmsnorm_fwd(h1, g_ffn_norm, "norm_ffn")
    gu = _mm(u2, W_fi, "nn", BF16, "ffn_in")
    act = _conv_fwd(gu, cw_full, conv_b)
    h2 = _mm(act, W_fo, "nn", F32, "ffn_out", res=h1)
    dh2, loss_p, dg_final = _final_loss_bwd(h2, loss_target[0], g_final_norm.reshape(1, D))

    dact = _mm(dh2, W_fo, "nt", F32, "d_act")
    dW_fo = _mm(act, dh2, "tn", F32, "dw_ffn_out")
    dconv, dgu_right, dcb, dcw = _conv_bwd_a(dact, gu, cw_full, conv_b)
    dgu = _conv_bwd_b(dconv, cw_full, dgu_right)
    du2 = _mm(dgu, W_fi, "nt", F32, "d_u2")
    dW_fi = _mm(u2, dgu, "tn", F32, "dw_ffn_in", col_shards=4)
    dh1, dg_ffn = _rmsnorm_bwd(du2, h1, g_ffn_norm, "norm_ffn_bwd", F32, res=dh2)
    dmerged = _mm(dh1, W_o, "nt", F32, "d_merged")
    dW_o = _mm(merged, dh1, "tn", F32, "dw_out")
    d_a, d_b, d_gates = _merge_bwd(dmerged, br_a, br_b, gates)
    do_mla = _mm(d_a, W_a, "nt", BF16, "d_o_mla")
    dW_a = _mm(o_mla, d_a, "tn", F32, "dw_branch_mla")
    do_hgrn = _mm(d_b, W_b, "nt", F32, "d_o_hgrn")
    dW_b = _mm(o_hgrn, d_b, "tn", F32, "dw_branch_hgrn")
    names = ["w_in", "w_q_up", "w_kv_up", "w_branch_mla", "w_branch_hgrn", "w_out", "w_ffn_in", "w_ffn_out"]

    def chip_partials(grads, by_cols, nms, tag):
        keep = [_shard_row_half(g, bc, ic) for g, bc in zip(grads, by_cols)]
        give = [_shard_row_half(g, bc, 1 - ic) for g, bc in zip(grads, by_cols)]
        taken = _to_sibling(give, "pair_exchange_" + tag)
        return [_pair_sum(a, b, "pair_sum_" + nm) for a, b, nm in zip(keep, taken, nms)]

    late_parts = chip_partials([dW_a, dW_b, dW_o, dW_fi, dW_fo], col_sharded[3:], names[3:], "late")
    dhq, dhf, dhi, dhg, dgn_p, dlb_p = _hgrn_bwd(hp, lb_raw, g_hgrn_norm, do_hgrn, o_rec, s_hist, nh)
    dqc, dkc, dvv, late_recv = _attn_bwd(qc, kc, vv, do_mla, lse, _attn_delta(o_mla, do_mla, heads), heads,
                                         _scatter_rider(late_parts))
    dqp, dkv, dkr = _rope_bwd(dqc, dkc, dvv, cos, sin, heads)
    dqn = _mm(dqp, W_qp, "nt", F32, "d_qn")
    dW_qp = _mm(qn, dqp, "tn", F32, "dw_q_up")
    dkvn = _mm(dkv, W_kv, "nt", F32, "d_kvn")
    dW_kv = _mm(kvn, dkv, "tn", F32, "dw_kv_up")
    dq_lat, dg_q = _rmsnorm_bwd(dqn, lat, g_q_norm, "norm_q_bwd", BF16, col0=0)
    dkv_lat, dg_kv = _rmsnorm_bwd(dkvn, lat, g_kv_norm, "norm_kv_bwd", BF16, col0=QL)
    dlat = jnp.concatenate([dq_lat, dkv_lat], axis=1)
    dhp = jnp.concatenate([dhq, dhf, dhi, dhg], axis=1)
    dW_lat = _mm(u1, dlat, "tn", F32, "dw_in_lat")
    dW_H = _mm(u1, dhp, "tn", F32, "dw_in_hgrn")
    dW_G = _mm(u1, d_gates, "tn", F32, "dw_in_gates")
    dW_kr = _mm(u1, dkr, "tn", F32, "dw_in_krope")
    dW_in = [dW_lat, dW_kr[:, :ROPE], dW_H, dW_G]
    dW_q = dW_qp.reshape(QL, heads, QPAD)[:, :, :QK_HEAD].reshape(QL, heads * QK_HEAD)
    early_parts = chip_partials([dW_in, dW_q, dW_kv], col_sharded[:3], names[:3], "early")
    du1 = _mm(dlat, W_lat, "nt", F32, "d_u1_lat")
    du1, early_recv = _mm(dhp, W_H, "nt", F32, "d_u1_hgrn", res=du1, rider=_scatter_rider(early_parts))
    du1 = _mm(d_gates, W_G, "nt", F32, "d_u1_gates", res=du1)
    du1 = _mm(dkr, W_kr, "nt", F32, "d_u1_krope", res=du1)
    (dx_tokens, dh0_prefix), dg_mix = _rmsnorm_bwd(du1, h0, g_mix_norm, "norm_mix_bwd", F32, res=dh1,
                                                   split_prefix=True)
    grad_x = dx_tokens[None]

    received = list(early_recv) + list(late_recv)
    halves = [_sum4(r, "sum_" + nm) for r, nm in zip(received, names)]
    others = _to_sibling(halves, "swap_halves")
    big_m = [m_w_in, m_w_q_up, m_w_kv_up, m_w_branch_mla, m_w_branch_hgrn, m_w_out, m_w_ffn_in, m_w_ffn_out]
    big_v = [v_w_in, v_w_q_up, v_w_kv_up, v_w_branch_mla, v_w_branch_hgrn, v_w_out, v_w_ffn_in, v_w_ffn_out]
    big_out = {}
    for nm, mine, other, w, m, v in zip(names, halves, others, big, big_m, big_v):
        g, d, mn, vn = _adamw_big(mine, other, w[0], m[0], v[0], "adamw_" + nm)
        big_out[nm] = (g[None], d[None], mn[None], vn[None])

    pieces = [loss_p[:, :1], dg_mix, dg_q, dg_kv, jnp.sum(dgn_p[:, 0, :], axis=0, keepdims=True), dg_ffn, dg_final,
              dlb_p[:, 0, :].reshape(1, D), dcb, dcw[0:3].reshape(1, 3 * F), dh0_prefix[PAD_LEN:PREFIX].reshape(1, N_META * D)]
    sizes = [p.shape[1] for p in pieces]
    flat = jnp.concatenate(pieces, axis=1)[0]
    rows = -(-flat.shape[0] // 1024) * 8
    packed = jnp.pad(flat, (0, rows * 128 - flat.shape[0])).reshape(rows, 128)
    total = _allreduce_small(packed).reshape(-1)
    offs = [0]
    for s in sizes:
        offs.append(offs[-1] + s)
    loss, g_mix, g_q, g_kv, g_hg, g_ffn, g_fin, dlb, g_cb, g_cw, g_meta = [
        total[offs[t]:offs[t + 1]].reshape(1, sizes[t]) for t in range(len(sizes))]
    g_cw = lax.dynamic_slice_in_dim(g_cw.reshape(3, F), chip * (F // 4), F // 4, axis=1)
    g_meta = lax.dynamic_slice_in_dim(g_meta.reshape(N_META, D), chip * (D // 4), D // 4, axis=1)
    items = [(g_meta, meta_tokens, m_meta_tokens, v_meta_tokens),
             (g_cw, conv_w[0], m_conv_w[0], v_conv_w[0]),
             (g_cb, conv_b, m_conv_b, v_conv_b),
             (g_mix, g_mix_norm, m_g_mix_norm, v_g_mix_norm),
             (g_q, g_q_norm, m_g_q_norm, v_g_q_norm),
             (g_kv, g_kv_norm, m_g_kv_norm, v_g_kv_norm),
             (g_hg, g_hgrn_norm, m_g_hgrn_norm, v_g_hgrn_norm),
             (g_ffn, g_ffn_norm, m_g_ffn_norm, v_g_ffn_norm),
             (g_fin, g_final_norm.reshape(1, D), m_g_final_norm.reshape(1, D), v_g_final_norm.reshape(1, D))]
    g_lb, s_delta, s_m, s_v = _adamw_small(items, (lb_raw, m_lb_raw, v_lb_raw), dlb)
    s_grads = [it[0] for it in items] + [g_lb]

    def shape_small(vals):
        meta, cw, cb, mix, q, kvg, hg, ffn, fin, lb = vals
        return [meta, cw[None], cb, mix, q, kvg, hg, ffn, fin.reshape(D), lb]

    s_grads, s_delta, s_m, s_v = [shape_small(v) for v in (s_grads, s_delta, s_m, s_v)]

    def ordered(kind, small):
        bigs = [big_out[nm][kind] for nm in names]
        return [small[0]] + bigs + small[1:]

    return (loss.reshape(()), grad_x, *ordered(0, s_grads), *ordered(1, s_delta), *ordered(2, s_m), *ordered(3, s_v))
```

```python
import math

import jax
import jax.numpy as jnp
from jax import lax
from jax.experimental import pallas as pl
from jax.experimental.pallas import tpu as pltpu

F32 = jnp.float32
BF16 = jnp.bfloat16
MESH = pl.DeviceIdType.MESH

NORM_EPS = 1e-6
N_META = 16
PREFIX = 128
PAD_LEN = PREFIX - N_META
HEAD = 128
ROPE = 64
QK_HEAD = HEAD + ROPE
QPAD = 2 * HEAD
SOFTMAX_SCALE = QK_HEAD ** -0.5
ROPE_THETA = 10000.0
CHUNK = 128
HALO = 16
LEVELS = 7
HGRN_FWD_HEADS = 8
HGRN_BWD_HEADS = 4
ATTN_FWD_HEADS = 4
ATTN_BWD_HEADS = 2
NEG = -1e30

ADAM_LR = 0.001
ADAM_B1 = 0.9
ADAM_B2 = 0.999
ADAM_EPS = 1e-08
ADAM_WD = 0.01
ADAM_STEP = 10

VMEM_LIMIT_BYTES = 48 * 1024 * 1024


def _params(*sem):
    return pltpu.CompilerParams(dimension_semantics=sem, vmem_limit_bytes=VMEM_LIMIT_BYTES)


def _tile(n, prefs):
    for p in prefs:
        if n % p == 0:
            return p
    return n


ROW_TILES = (640, 512, 256, 128, 64, 32, 16, 8)
TN_ROW_TILES = (1024, 1408, 1536, 512, 256, 128)
TN_K_TILES = (1664, 640, 512, 256, 128)
TALL_ROW_TILE = 1664
SHARD_COL_TILES = (1408, 1024, 512, 256, 128)
COL_TILES = (1024, 512, 256, 128)
K_TILES = (2048, 1536, 1408, 1024, 512, 256, 128)


def _sigmoid(x):
    return 1.0 / (1.0 + jnp.exp(-x))


def _dot(a, b, dims):
    return lax.dot_general(a.astype(BF16), b.astype(BF16), (dims, ((), ())), preferred_element_type=F32)


NN = ((1,), (0,))
TN = ((0,), (0,))
NT = ((1,), (1,))


def _split3(x):
    hi = x.astype(BF16)
    r = x - hi.astype(F32)
    mid = r.astype(BF16)
    lo = (r - mid.astype(F32)).astype(BF16)
    return hi, mid, lo


def _dot_exact_rhs(sel, x, dims):
    hi, mid, lo = _split3(x)
    return _dot(sel, hi, dims) + _dot(sel, mid, dims) + _dot(sel, lo, dims)


def _mm(a, b, mode, out_dtype, name, res=None, rider=None, col_shards=1):
    if mode == "nn":
        (M, K), (K2, N) = a.shape, b.shape
    elif mode == "tn":
        (K, M), (K2, N) = a.shape, b.shape
    else:
        (M, K), (N, K2) = a.shape, b.shape
    assert K == K2, (name, a.shape, b.shape)
    tm = _tile(M, TN_ROW_TILES if mode == "tn" else ROW_TILES)
    tn = _tile(N // col_shards, COL_TILES if col_shards == 1 else SHARD_COL_TILES)
    tk = _tile(K, TN_K_TILES if mode == "tn" else K_TILES)
    nk = K // tk
    if mode != "tn" and nk > 1 and res is None and a.dtype == BF16 and M % TALL_ROW_TILE == 0:
        tm = TALL_ROW_TILE
    dims = {"nn": NN, "tn": TN, "nt": NT}[mode]

    n_rin = 0 if rider is None else len(rider.operands)
    n_rout = 0 if rider is None else len(rider.out_shapes)
    n_rsem = 0 if rider is None else len(rider.sem_shapes)
    grid = (M // tm, N // tn, nk)

    def body(*refs):
        main_in, r_ins, (o_ref,), r_outs, accs, r_sems = _split_refs(
            refs, (2 if res is None else 3, n_rin, 1, n_rout, 0 if nk == 1 else 1, n_rsem))
        a_ref, b_ref = main_in[:2]
        r_ref = None if res is None else main_in[2]
        i, j, k = pl.program_id(0), pl.program_id(1), pl.program_id(2)
        if rider is not None:
            pl.when((i == 0) & (j == 0) & (k == 0))(lambda: rider.start(r_ins, r_outs, r_sems))

        def finish(r):
            if r_ref is not None:
                r = r + r_ref[...].astype(F32)
            o_ref[...] = r.astype(out_dtype)

        if nk == 1:
            finish(_dot(a_ref[...], b_ref[...], dims))
        else:
            acc = accs[0]

            @pl.when(k == 0)
            def _():
                acc[...] = jnp.zeros_like(acc)

            acc[...] += _dot(a_ref[...], b_ref[...], dims)

            @pl.when(k == nk - 1)
            def _():
                finish(acc[...])

        if rider is not None:
            last = (i == grid[0] - 1) & (j == grid[1] - 1) & (k == nk - 1)
            pl.when(last)(lambda: rider.finish(r_ins, r_outs, r_sems))

    if mode == "nn":
        a_spec = pl.BlockSpec((tm, tk), lambda i, j, k: (i, k))
        b_spec = pl.BlockSpec((tk, tn), lambda i, j, k: (k, j))
    elif mode == "tn":
        a_spec = pl.BlockSpec((tk, tm), lambda i, j, k: (k, i))
        b_spec = pl.BlockSpec((tk, tn), lambda i, j, k: (k, j))
    else:
        a_spec = pl.BlockSpec((tm, tk), lambda i, j, k: (i, k))
        b_spec = pl.BlockSpec((tn, tk), lambda i, j, k: (j, k))
    in_specs = [a_spec, b_spec]
    operands = [a, b]
    if res is not None:
        in_specs.append(pl.BlockSpec((tm, tn), lambda i, j, k: (i, j)))
        operands.append(res)
    if col_shards == 1:
        out_shape = jax.ShapeDtypeStruct((M, N), out_dtype)
        out_spec = pl.BlockSpec((tm, tn), lambda i, j, k: (i, j))
    else:
        per_shard = (N // col_shards) // tn
        out_shape = jax.ShapeDtypeStruct((col_shards, M, N // col_shards), out_dtype)
        out_spec = pl.BlockSpec((None, tm, tn), lambda i, j, k: (j // per_shard, i, j % per_shard))
    scratch = [] if nk == 1 else [pltpu.VMEM((tm, tn), F32)]
    if rider is None:
        return pl.pallas_call(
            body, name=name, out_shape=out_shape, grid=grid, in_specs=in_specs, out_specs=out_spec,
            scratch_shapes=scratch, compiler_params=_params("parallel", "parallel", "arbitrary"),
        )(*operands)
    res_all = pl.pallas_call(
        body, name=name, out_shape=(out_shape, *rider.out_shapes), grid=grid,
        in_specs=in_specs + [ANY] * n_rin, out_specs=(out_spec, *([ANY] * n_rout)),
        scratch_shapes=scratch + rider.sem_shapes,
        compiler_params=pltpu.CompilerParams(dimension_semantics=("arbitrary", "arbitrary", "arbitrary"),
                                             vmem_limit_bytes=VMEM_LIMIT_BYTES, has_side_effects=True),
    )(*operands, *rider.operands)
    return res_all[0], res_all[1:]


def _rmsnorm_fwd(x, g, name, col0=0, width=None):
    L = x.shape[0]
    width = x.shape[1] if width is None else width
    assert col0 % width == 0
    cb = col0 // width
    tm = _tile(L, (128, 64, 32, 16))

    def body(x_ref, g_ref, o_ref):
        xv = x_ref[...].astype(F32)
        r = lax.rsqrt(jnp.mean(xv * xv, axis=-1, keepdims=True) + NORM_EPS)
        o_ref[...] = ((xv * r) * g_ref[...]).astype(BF16)

    return pl.pallas_call(
        body, name=name,
        out_shape=jax.ShapeDtypeStruct((L, width), BF16),
        grid=(L // tm,),
        in_specs=[pl.BlockSpec((tm, width), lambda i: (i, cb)), pl.BlockSpec((1, width), lambda i: (0, 0))],
        out_specs=pl.BlockSpec((tm, width), lambda i: (i, 0)),
        compiler_params=_params("parallel"),
    )(x, g)


def _embed_norm(x, meta, g):
    S, D = x.shape
    L = S + PREFIX
    tm = PREFIX

    def body(x_ref, meta_ref, g_ref, h_ref, u_ref):
        i = pl.program_id(0)

        @pl.when(i == 0)
        def _():
            h_ref[...] = jnp.zeros_like(h_ref)
            h_ref[PAD_LEN:PREFIX, :] = meta_ref[...]

        @pl.when(i > 0)
        def _():
            h_ref[...] = x_ref[...]

        xv = h_ref[...]
        r = lax.rsqrt(jnp.mean(xv * xv, axis=-1, keepdims=True) + NORM_EPS)
        u_ref[...] = ((xv * r) * g_ref[...]).astype(BF16)

    return pl.pallas_call(
        body, name="embed_norm_mix",
        out_shape=(jax.ShapeDtypeStruct((L, D), F32), jax.ShapeDtypeStruct((L, D), BF16)),
        grid=(L // tm,),
        in_specs=[pl.BlockSpec((tm, D), lambda i: (jnp.maximum(i - 1, 0), 0)),
                  pl.BlockSpec((N_META, D), lambda i: (0, 0)), pl.BlockSpec((1, D), lambda i: (0, 0))],
        out_specs=(pl.BlockSpec((tm, D), lambda i: (i, 0)), pl.BlockSpec((tm, D), lambda i: (i, 0))),
        compiler_params=_params("parallel"),
    )(x, meta, g)


def _rmsnorm_bwd(dy, x, g, name, out_dtype, col0=0, res=None, split_prefix=False):
    L, width = dy.shape
    assert col0 % width == 0
    cb = col0 // width
    tm = PREFIX if split_prefix else _tile(L, (128, 64, 32, 16))

    def body(*refs):
        head_ref = None
        if split_prefix:
            refs, head_ref = refs[:-1], refs[-1]
        if res is None:
            dy_ref, x_ref, g_ref, dx_ref, dg_ref = refs
            r_ref = None
        else:
            dy_ref, x_ref, g_ref, r_ref, dx_ref, dg_ref = refs

        @pl.when(pl.program_id(0) == 0)
        def _():
            dg_ref[...] = jnp.zeros_like(dg_ref)

        xv = x_ref[...].astype(F32)
        dyv = dy_ref[...].astype(F32)
        r = lax.rsqrt(jnp.mean(xv * xv, axis=-1, keepdims=True) + NORM_EPS)
        z = dyv * g_ref[...]
        dx = r * z - xv * ((r * r * r) * jnp.mean(z * xv, axis=-1, keepdims=True))
        if r_ref is not None:
            dx = dx + r_ref[...]
        dg_ref[...] += jnp.sum(dyv * (xv * r), axis=0, keepdims=True)
        if head_ref is None:
            dx_ref[...] = dx.astype(out_dtype)
        else:
            @pl.when(pl.program_id(0) == 0)
            def _():
                head_ref[...] = dx.astype(out_dtype)

            @pl.when(pl.program_id(0) > 0)
            def _():
                dx_ref[...] = dx.astype(out_dtype)

    in_specs = [pl.BlockSpec((tm, width), lambda i: (i, 0)),
                pl.BlockSpec((tm, width), lambda i: (i, cb)),
                pl.BlockSpec((1, width), lambda i: (0, 0))]
    operands = [dy, x, g]
    if res is not None:
        in_specs.append(pl.BlockSpec((tm, width), lambda i: (i, 0)))
        operands.append(res)
    dg_shape, dg_spec = jax.ShapeDtypeStruct((1, width), F32), pl.BlockSpec((1, width), lambda i: (0, 0))
    if not split_prefix:
        return pl.pallas_call(
            body, name=name, out_shape=(jax.ShapeDtypeStruct((L, width), out_dtype), dg_shape), grid=(L // tm,),
            in_specs=in_specs, out_specs=(pl.BlockSpec((tm, width), lambda i: (i, 0)), dg_spec),
            compiler_params=_params("arbitrary"),
        )(*operands)
    dx_rest, dg, dx_head = pl.pallas_call(
        body, name=name,
        out_shape=(jax.ShapeDtypeStruct((L - PREFIX, width), out_dtype), dg_shape,
                   jax.ShapeDtypeStruct((PREFIX, width), out_dtype)),
        grid=(L // tm,), in_specs=in_specs,
        out_specs=(pl.BlockSpec((tm, width), lambda i: (jnp.maximum(i - 1, 0), 0)), dg_spec,
                   pl.BlockSpec((PREFIX, width), lambda i: (0, 0))),
        compiler_params=_params("arbitrary"),
    )(*operands)
    return (dx_rest, dx_head), dg


def _final_loss_bwd(h2, tgt, g):
    L, D = h2.shape
    tm = PREFIX
    inv_d = 1.0 / D

    def body(h_ref, t_ref, g_ref, dh_ref, loss_ref, dg_ref):
        i = pl.program_id(0)

        @pl.when(i == 0)
        def _():
            loss_ref[...] = jnp.zeros_like(loss_ref)
            dg_ref[...] = jnp.zeros_like(dg_ref)

        xv = h_ref[...]
        r = lax.rsqrt(jnp.mean(xv * xv, axis=-1, keepdims=True) + NORM_EPS)
        xn = xv * r
        y = xn * g_ref[...]
        real = (i >= PREFIX // tm).astype(F32)
        diff = (y - t_ref[...]) * real
        loss_ref[...] += 0.5 * inv_d * jnp.sum(diff * diff)
        dyv = diff * inv_d
        z = dyv * g_ref[...]
        dh_ref[...] = r * z - xv * ((r * r * r) * jnp.mean(z * xv, axis=-1, keepdims=True))
        dg_ref[...] += jnp.sum(dyv * xn, axis=0, keepdims=True)

    shift = PREFIX // tm
    return pl.pallas_call(
        body, name="final_loss_bwd",
        out_shape=(jax.ShapeDtypeStruct((L, D), F32), jax.ShapeDtypeStruct((1, 128), F32),
                   jax.ShapeDtypeStruct((1, D), F32)),
        grid=(L // tm,),
        in_specs=[pl.BlockSpec((tm, D), lambda i: (i, 0)),
                  pl.BlockSpec((tm, D), lambda i: (jnp.maximum(i - shift, 0), 0)),
                  pl.BlockSpec((1, D), lambda i: (0, 0))],
        out_specs=(pl.BlockSpec((tm, D), lambda i: (i, 0)), pl.BlockSpec((1, 128), lambda i: (0, 0)),
                   pl.BlockSpec((1, D), lambda i: (0, 0))),
        compiler_params=_params("arbitrary"),
    )(h2, tgt, g)


def _rot_half(x):
    lane = lax.broadcasted_iota(jnp.int32, x.shape, 1)
    return jnp.where(lane < ROPE // 2, -pltpu.roll(x, 128 - ROPE // 2, 1), pltpu.roll(x, ROPE // 2, 1))


def _rope_fwd(qp, kv, kr, cos, sin, heads):
    L = qp.shape[0]
    tm = _tile(L, (128,))

    def body(q_ref, kv_ref, kr_ref, c_ref, s_ref, qc_ref, kc_ref, v_ref):
        c, s = c_ref[...], s_ref[...]
        krv = kr_ref[...]
        kr_rot = (krv * c + _rot_half(krv) * s).astype(BF16)
        for h in range(heads):
            lo = h * QPAD
            qc_ref[:, lo:lo + HEAD] = (q_ref[:, lo:lo + HEAD].astype(F32) * SOFTMAX_SCALE).astype(BF16)
            qr = q_ref[:, lo + HEAD:lo + QPAD].astype(F32)
            qc_ref[:, lo + HEAD:lo + QPAD] = ((qr * c + _rot_half(qr) * s) * SOFTMAX_SCALE).astype(BF16)
            kc_ref[:, lo:lo + HEAD] = kv_ref[:, lo:lo + HEAD].astype(BF16)
            kc_ref[:, lo + HEAD:lo + QPAD] = kr_rot
            v_ref[:, h * HEAD:(h + 1) * HEAD] = kv_ref[:, lo + HEAD:lo + QPAD].astype(BF16)

    W = heads * QPAD
    row = lambda w: pl.BlockSpec((tm, w), lambda i: (i, 0))
    return pl.pallas_call(
        body, name="rope_fwd",
        out_shape=(jax.ShapeDtypeStruct((L, W), BF16), jax.ShapeDtypeStruct((L, W), BF16),
                   jax.ShapeDtypeStruct((L, heads * HEAD), BF16)),
        grid=(L // tm,),
        in_specs=[row(W), row(W), row(128), row(128), row(128)],
        out_specs=(row(W), row(W), row(heads * HEAD)),
        compiler_params=_params("parallel"),
    )(qp, kv, kr, cos, sin)


def _rope_bwd(dqc, dkc, dv, cos, sin, heads):
    L = dqc.shape[0]
    tm = _tile(L, (128,))

    def body(dq_ref, dk_ref, dv_ref, c_ref, s_ref, dqp_ref, dkv_ref, dkr_ref):
        c, s = c_ref[...], s_ref[...]
        acc = jnp.zeros((tm, 128), F32)
        for h in range(heads):
            lo = h * QPAD
            dqp_ref[:, lo:lo + HEAD] = (dq_ref[:, lo:lo + HEAD] * SOFTMAX_SCALE).astype(BF16)
            d = dq_ref[:, lo + HEAD:lo + QPAD]
            dqp_ref[:, lo + HEAD:lo + QPAD] = ((d * c - _rot_half(d) * s) * SOFTMAX_SCALE).astype(BF16)
            dkv_ref[:, lo:lo + HEAD] = dk_ref[:, lo:lo + HEAD].astype(BF16)
            dkv_ref[:, lo + HEAD:lo + QPAD] = dv_ref[:, h * HEAD:(h + 1) * HEAD].astype(BF16)
            acc = acc + dk_ref[:, lo + HEAD:lo + QPAD]
        dkr_ref[...] = (acc * c - _rot_half(acc) * s).astype(BF16)

    W = heads * QPAD
    row = lambda w: pl.BlockSpec((tm, w), lambda i: (i, 0))
    return pl.pallas_call(
        body, name="rope_bwd",
        out_shape=(jax.ShapeDtypeStruct((L, W), BF16), jax.ShapeDtypeStruct((L, W), BF16),
                   jax.ShapeDtypeStruct((L, 128), BF16)),
        grid=(L // tm,),
        in_specs=[row(W), row(W), row(heads * HEAD), row(128), row(128)],
        out_specs=(row(W), row(W), row(128)),
        compiler_params=_params("parallel"),
    )(dqc, dkc, dv, cos, sin)


def _attn_keep(qi, ki, ta):
    t = qi * ta + lax.broadcasted_iota(jnp.int32, (ta, ta), 0)
    s = ki * ta + lax.broadcasted_iota(jnp.int32, (ta, ta), 1)
    return (s <= t) & ((s >= PAD_LEN) | (s == t))


def _split_refs(refs, sizes):
    out, at = [], 0
    for n in sizes:
        out.append(refs[at:at + n])
        at += n
    return out


def _attn_fwd(qc, kc, v, heads, rider):
    L = qc.shape[0]
    ta = _tile(L, (640, 128))
    nb = L // ta
    pairs = [(i, j) for i in range(nb) for j in range(i + 1)]
    q_of = jnp.asarray([p[0] for p in pairs], jnp.int32)
    k_of = jnp.asarray([p[1] for p in pairs], jnp.int32)
    n_rin, n_rout = len(rider.operands), len(rider.out_shapes)
    per = _tile(heads, (ATTN_FWD_HEADS, 1))
    ng = heads // per

    def body(*refs):
        (q_of_ref, k_of_ref, q_ref, k_ref, v_ref), r_ins, (o_ref, lse_ref), r_outs, (m_sc, l_sc, acc_sc), r_sems = \
            _split_refs(refs, (5, n_rin, 2, n_rout, 3, len(rider.sem_shapes)))
        h, t = pl.program_id(0), pl.program_id(1)
        qi, ki = q_of_ref[t], k_of_ref[t]
        pl.when((h == 0) & (t == 0))(lambda: rider.start(r_ins, r_outs, r_sems))

        @pl.when(ki == 0)
        def _():
            m_sc[...] = jnp.full_like(m_sc, NEG)
            l_sc[...] = jnp.zeros_like(l_sc)
            acc_sc[...] = jnp.zeros_like(acc_sc)

        def step(masked):
            wide = lambda hh: slice(hh * QPAD, (hh + 1) * QPAD)
            lanes = lambda hh: slice(hh * HEAD, (hh + 1) * HEAD)
            ss = [_dot(q_ref[:, wide(hh)], k_ref[:, wide(hh)], NT) for hh in range(per)]
            if masked:
                keep = _attn_keep(qi, ki, ta)
                ss = [jnp.where(keep, s, NEG) for s in ss]
            for hh in range(per):
                m_old = m_sc[hh]
                m_new = jnp.maximum(m_old, jnp.max(ss[hh], axis=-1, keepdims=True))
                p = jnp.exp(ss[hh] - jnp.tile(m_new, (1, ta // HEAD)))
                alpha = jnp.exp(m_old - m_new)
                l_sc[hh] = alpha * l_sc[hh] + jnp.sum(p, axis=-1, keepdims=True)
                acc_sc[hh] = alpha * acc_sc[hh] + _dot(p, v_ref[:, lanes(hh)], NN)
                m_sc[hh] = m_new

        pl.when((ki == qi) | ((ki == 0) & (qi > 0)))(lambda: step(True))
        pl.when((ki > 0) & (ki < qi))(lambda: step(False))

        @pl.when(ki == qi)
        def _():
            for hh in range(per):
                l = l_sc[hh]
                o_ref[:, hh * HEAD:(hh + 1) * HEAD] = (acc_sc[hh] / l).astype(BF16)
                lse_ref[:, hh * HEAD:(hh + 1) * HEAD] = m_sc[hh] + jnp.log(l)

        pl.when((h == ng - 1) & (t == len(pairs) - 1))(lambda: rider.finish(r_ins, r_outs, r_sems))

    qrow = lambda w: pl.BlockSpec((ta, per * w), lambda h, t, q_of, k_of: (q_of[t], h))
    krow = lambda w: pl.BlockSpec((ta, per * w), lambda h, t, q_of, k_of: (k_of[t], h))
    stat = pltpu.VMEM((per, ta, HEAD), F32)
    res = pl.pallas_call(
        body, name="attn_fwd",
        out_shape=(jax.ShapeDtypeStruct((L, heads * HEAD), BF16), jax.ShapeDtypeStruct((L, heads * HEAD), F32),
                   *rider.out_shapes),
        grid_spec=pltpu.PrefetchScalarGridSpec(
            num_scalar_prefetch=2, grid=(ng, len(pairs)),
            in_specs=[qrow(QPAD), krow(QPAD), krow(HEAD)] + [ANY] * n_rin,
            out_specs=(qrow(HEAD), qrow(HEAD), *([ANY] * n_rout)),
            scratch_shapes=[stat, stat, stat] + rider.sem_shapes),
        compiler_params=pltpu.CompilerParams(dimension_semantics=("arbitrary", "arbitrary"),
                                             vmem_limit_bytes=VMEM_LIMIT_BYTES, has_side_effects=True),
    )(q_of, k_of, qc, kc, v, *rider.operands)
    return res[0], res[1], res[2:]


def _attn_delta(o, do, heads):
    L = o.shape[0]
    tm = _tile(L, (128,))

    def body(o_ref, do_ref, d_ref):
        for h in range(heads):
            cols = slice(h * HEAD, (h + 1) * HEAD)
            d = jnp.sum(do_ref[:, cols].astype(F32) * o_ref[:, cols].astype(F32), axis=-1, keepdims=True)
            d_ref[:, cols] = jnp.broadcast_to(d, (tm, HEAD))

    row = pl.BlockSpec((tm, heads * HEAD), lambda i: (i, 0))
    return pl.pallas_call(
        body, name="attn_delta", out_shape=jax.ShapeDtypeStruct((L, heads * HEAD), F32), grid=(L // tm,),
        in_specs=[row, row], out_specs=row, compiler_params=_params("parallel"),
    )(o, do)


def _attn_bwd(qc, kc, v, do, lse, delta, heads, rider):
    L = qc.shape[0]
    ta = _tile(L, (640, 128))
    nb = L // ta
    pairs = [(j, i) for j in range(nb) for i in range(j, nb)]
    k_of = jnp.asarray([p[0] for p in pairs], jnp.int32)
    q_of = jnp.asarray([p[1] for p in pairs], jnp.int32)
    n_rin, n_rout = len(rider.operands), len(rider.out_shapes)
    per = _tile(heads, (ATTN_BWD_HEADS, 1))
    ng = heads // per

    def body(*refs):
        ((k_of_ref, q_of_ref, q_ref, k_ref, v_ref, do_ref, lse_ref, delta_ref), r_ins, (dq_hbm, dk_ref, dv_ref),
         r_outs, (dq_sc, dq_sem), r_sems) = _split_refs(refs, (8, n_rin, 3, n_rout, 2, len(rider.sem_shapes)))
        h, t = pl.program_id(0), pl.program_id(1)
        kj, qi = k_of_ref[t], q_of_ref[t]
        pl.when((h == 0) & (t == 0))(lambda: rider.start(r_ins, r_outs, r_sems))

        @pl.when(t == 0)
        def _():
            dq_sc[...] = jnp.zeros_like(dq_sc)

        @pl.when(qi == kj)
        def _():
            dk_ref[...] = jnp.zeros_like(dk_ref)
            dv_ref[...] = jnp.zeros_like(dv_ref)

        def step(masked):
            wide = lambda hh: slice(hh * QPAD, (hh + 1) * QPAD)
            lanes = lambda hh: slice(hh * HEAD, (hh + 1) * HEAD)
            rep = (1, ta // HEAD)
            rows = pl.ds(pl.multiple_of(qi * ta, ta), ta)
            ss = [_dot(q_ref[:, wide(hh)], k_ref[:, wide(hh)], NT) for hh in range(per)]
            dps = [_dot(do_ref[:, lanes(hh)], v_ref[:, lanes(hh)], NT) for hh in range(per)]
            ps = [jnp.exp(ss[hh] - jnp.tile(lse_ref[:, lanes(hh)], rep)) for hh in range(per)]
            if masked:
                keep = _attn_keep(qi, kj, ta)
                ps = [jnp.where(keep, p, 0.0) for p in ps]
            dss = [ps[hh] * (dps[hh] - jnp.tile(delta_ref[:, lanes(hh)], rep)) for hh in range(per)]
            for hh in range(per):
                dv_ref[:, lanes(hh)] += _dot(ps[hh], do_ref[:, lanes(hh)], TN)
                dk_ref[:, wide(hh)] += _dot(dss[hh], q_ref[:, wide(hh)], TN)
                dq_sc[rows, wide(hh)] += _dot(dss[hh], k_ref[:, wide(hh)], NN)

        pl.when((qi == kj) | ((kj == 0) & (qi > 0)))(lambda: step(True))
        pl.when((kj > 0) & (qi > kj))(lambda: step(False))

        @pl.when(t == len(pairs) - 1)
        def _():
            cols = pl.ds(pl.multiple_of(h * (per * QPAD), per * QPAD), per * QPAD)
            out = pltpu.make_async_copy(dq_sc, dq_hbm.at[:, cols], dq_sem)
            out.start()
            out.wait()

        pl.when((h == ng - 1) & (t == len(pairs) - 1))(lambda: rider.finish(r_ins, r_outs, r_sems))

    qrow = lambda w: pl.BlockSpec((ta, per * w), lambda h, t, k_of, q_of: (q_of[t], h))
    krow = lambda w: pl.BlockSpec((ta, per * w), lambda h, t, k_of, q_of: (k_of[t], h))
    res = pl.pallas_call(
        body, name="attn_bwd",
        out_shape=(jax.ShapeDtypeStruct((L, heads * QPAD), F32), jax.ShapeDtypeStruct((L, heads * QPAD), F32),
                   jax.ShapeDtypeStruct((L, heads * HEAD), F32), *rider.out_shapes),
        grid_spec=pltpu.PrefetchScalarGridSpec(
            num_scalar_prefetch=2, grid=(ng, len(pairs)),
            in_specs=[qrow(QPAD), krow(QPAD), krow(HEAD), qrow(HEAD), qrow(HEAD), qrow(HEAD)] + [ANY] * n_rin,
            out_specs=(ANY, krow(QPAD), krow(HEAD), *([ANY] * n_rout)),
            scratch_shapes=[pltpu.VMEM((L, per * QPAD), F32), pltpu.SemaphoreType.DMA] + rider.sem_shapes),
        compiler_params=pltpu.CompilerParams(dimension_semantics=("arbitrary", "arbitrary"),
                                             vmem_limit_bytes=VMEM_LIMIT_BYTES, has_side_effects=True),
    )(k_of, q_of, qc, kc, v, do, lse, delta, *rider.operands)
    return res[0], res[1], res[2], res[3:]


def _hgrn_constants():
    t = jnp.arange(CHUNK)
    tril = (t[None, :] <= t[:, None]).astype(BF16)
    sel, blk = [], []
    for lv in range(LEVELS):
        hs = 1 << lv
        mid = (t // (2 * hs)) * (2 * hs) + hs - 1
        sel.append((t[None, :] == mid[:, None]).astype(BF16))
        blk.append(((t[:, None] // (2 * hs)) == (t[None, :] // (2 * hs))).astype(F32))
    return tril, jnp.concatenate(sel, axis=0), jnp.concatenate(blk, axis=0)


def _hgrn_decay_grad_constants():
    r = jnp.arange(CHUNK)[:, None]
    c = jnp.arange(CHUNK)[None, :]
    mats = []
    for lv in range(LEVELS):
        same = (r >> (lv + 1)) == (c >> (lv + 1))
        second = ((r >> lv) & 1) == 1
        mats.append(same & jnp.where(second, c >= r, c < r))
    mats += [c >= r, c < r]
    return jnp.concatenate(mats, axis=1).astype(BF16)


def _hgrn_gates(hq_ref, hf_ref, hi_ref, lb_ref, cols, row0):
    rows = row0 + lax.broadcasted_iota(jnp.int32, (CHUNK, 1), 0)
    valid = rows >= PAD_LEN
    lb = 1.0 / (1.0 + jnp.exp(lb_ref[1:2, cols] - lb_ref[0:1, cols]))
    hq = hq_ref[:, cols].astype(F32)
    sq = _sigmoid(hq)
    sg = _sigmoid(hf_ref[:, cols].astype(F32))
    f = lb + (1.0 - lb) * sg
    g = jnp.where(valid, jnp.log(f), 0.0)
    k = jnp.where(valid, 1.0 - f, 0.0)
    return dict(q=hq * sq, sq=sq, hq=hq, k=k, v=hi_ref[:, cols].astype(F32), g=g, f=f, sg=sg, lb=lb, valid=valid)


def _hgrn_prefix(tril_ref, sel_ref, gs):
    n = len(gs)
    b_all = _dot_exact_rhs(tril_ref[...], jnp.concatenate(gs, axis=1) if n > 1 else gs[0], NN)
    bm_all = _dot_exact_rhs(sel_ref[...], b_all, NN)
    cut = lambda a, i: a[:, i * HEAD:(i + 1) * HEAD]
    return [cut(b_all, i) for i in range(n)], [cut(bm_all, i) for i in range(n)]


def _hgrn_levels(q, k, b, bm_all, blk_ref):
    t = lax.broadcasted_iota(jnp.int32, (CHUNK, 1), 0)
    out = []
    for lv in range(LEVELS):
        bm = bm_all[lv * CHUNK:(lv + 1) * CHUNK, :]
        second = ((t >> lv) & 1) == 1
        eq = jnp.where(second, jnp.exp(jnp.minimum(b - bm, 0.0)), 0.0)
        ek = jnp.where(second, 0.0, jnp.exp(jnp.minimum(bm - b, 0.0)))
        same = blk_ref[lv * CHUNK:(lv + 1) * CHUNK, :]
        out.append((eq, ek, (q * eq).astype(BF16), (k * ek).astype(BF16), same))
    return out


def _hgrn_intra(qs, ks, levels):
    tt = lax.broadcasted_iota(jnp.int32, (CHUNK, CHUNK), 0)
    ss = lax.broadcasted_iota(jnp.int32, (CHUNK, CHUNK), 1)
    ps = [jnp.where(tt == ss, jnp.sum(q * k, axis=-1, keepdims=True), 0.0) for q, k in zip(qs, ks)]
    for lv in range(LEVELS):
        for i, lvl in enumerate(levels):
            _, _, ql, kl, same = lvl[lv]
            ps[i] = ps[i] + _dot(ql, kl, NT) * same
    return ps


def _hgrn_fwd(hp, lb_raw, g_norm, nh):
    L = hp.shape[0]
    D = nh * HEAD
    nc = L // CHUNK
    per = _tile(nh, (HGRN_FWD_HEADS, 2, 1))
    ng = nh // per
    tril, sel, blk = _hgrn_constants()

    def body(hq_ref, hf_ref, hi_ref, hg_ref, lb_ref, gn_ref, tril_ref, sel_ref, blk_ref,
             oh_ref, orec_ref, shist_ref, s_sc, b_sc):
        c = pl.program_id(1)

        @pl.when(c == 0)
        def _():
            s_sc[...] = jnp.zeros_like(s_sc)

        heads_here = range(per)
        lanes = [slice(hh * HEAD, (hh + 1) * HEAD) for hh in heads_here]
        ws = [_hgrn_gates(hq_ref, hf_ref, hi_ref, lb_ref, lanes[hh], c * CHUNK) for hh in heads_here]
        qs, ks, vs = [w["q"] for w in ws], [w["k"] for w in ws], [w["v"] for w in ws]
        bs, bms = _hgrn_prefix(tril_ref, sel_ref, [w["g"] for w in ws])
        for hh in heads_here:
            b_sc[hh] = bs[hh]
        b_lasts = [b_sc[hh, CHUNK - 1:CHUNK, :] for hh in heads_here]
        ps = _hgrn_intra(qs, ks, [_hgrn_levels(qs[hh], ks[hh], bs[hh], bms[hh], blk_ref) for hh in heads_here])
        s_ins = [s_sc[hh] for hh in heads_here]
        os_ = [_dot(ps[hh], vs[hh], NN) + _dot(qs[hh] * jnp.exp(bs[hh]), s_ins[hh], NT) for hh in heads_here]
        for hh in heads_here:
            shist_ref[0, hh] = s_ins[hh]
            s_sc[hh] = (jnp.exp(b_lasts[hh]) * s_ins[hh]
                        + _dot(vs[hh], ks[hh] * jnp.exp(b_lasts[hh] - bs[hh]), TN))
        for hh in heads_here:
            o = os_[hh]
            orec_ref[:, lanes[hh]] = o
            rn = lax.rsqrt(jnp.mean(o * o, axis=-1, keepdims=True) + NORM_EPS)
            hg = hg_ref[:, lanes[hh]].astype(F32)
            oh_ref[:, lanes[hh]] = (((o * rn) * gn_ref[...]) * (hg * _sigmoid(hg))).astype(BF16)

    col = lambda grp: pl.BlockSpec((CHUNK, per * HEAD), lambda h, c: (c, grp * ng + h))
    const = lambda shape: pl.BlockSpec(shape, lambda h, c: (0, 0))
    return pl.pallas_call(
        body, name="hgrn_fwd",
        out_shape=(jax.ShapeDtypeStruct((L, D), BF16), jax.ShapeDtypeStruct((L, D), F32),
                   jax.ShapeDtypeStruct((nc, nh, HEAD, HEAD), F32)),
        grid=(ng, nc),
        in_specs=[col(0), col(1), col(2), col(3),
                  pl.BlockSpec((2, per * HEAD), lambda h, c: (0, h)), const((1, HEAD)),
                  const((CHUNK, CHUNK)), const((LEVELS * CHUNK, CHUNK)), const((LEVELS * CHUNK, CHUNK))],
        out_specs=(pl.BlockSpec((CHUNK, per * HEAD), lambda h, c: (c, h)),
                   pl.BlockSpec((CHUNK, per * HEAD), lambda h, c: (c, h)),
                   pl.BlockSpec((1, per, HEAD, HEAD), lambda h, c: (c, h, 0, 0))),
        scratch_shapes=[pltpu.VMEM((per, HEAD, HEAD), F32), pltpu.VMEM((per, CHUNK, HEAD), F32)],
        compiler_params=_params("parallel", "arbitrary"),
    )(hp, hp, hp, hp, lb_raw, g_norm, tril, sel, blk)


def _hgrn_bwd(hp, lb_raw, g_norm, do_h, o_rec, s_hist, nh):
    L = hp.shape[0]
    D = nh * HEAD
    nc = L // CHUNK
    per = _tile(nh, (HGRN_BWD_HEADS, 1))
    ng = nh // per
    tril, sel, blk = _hgrn_constants()
    tdec = _hgrn_decay_grad_constants()

    def body(hq_ref, hf_ref, hi_ref, hg_ref, lb_ref, gn_ref, tril_ref, sel_ref, blk_ref, tdec_ref,
             do_ref, orec_ref, shist_ref, dhq_ref, dhf_ref, dhi_ref, dhg_ref, dgn_ref, dlb_ref, ds_sc, b_sc):
        ci = pl.program_id(1)
        c = nc - 1 - ci

        @pl.when(ci == 0)
        def _():
            ds_sc[...] = jnp.zeros_like(ds_sc)
            dgn_ref[...] = jnp.zeros_like(dgn_ref)
            dlb_ref[...] = jnp.zeros_like(dlb_ref)

        heads_here = range(per)
        lanes = [slice(hh * HEAD, (hh + 1) * HEAD) for hh in heads_here]
        ws = [_hgrn_gates(hq_ref, hf_ref, hi_ref, lb_ref, lanes[hh], c * CHUNK) for hh in heads_here]
        qs, ks, vs = [w["q"] for w in ws], [w["k"] for w in ws], [w["v"] for w in ws]
        bs, bms = _hgrn_prefix(tril_ref, sel_ref, [w["g"] for w in ws])
        for hh in heads_here:
            b_sc[hh] = bs[hh]
        b_lasts = [b_sc[hh, CHUNK - 1:CHUNK, :] for hh in heads_here]
        levels = [_hgrn_levels(qs[hh], ks[hh], bs[hh], bms[hh], blk_ref) for hh in heads_here]
        ps = _hgrn_intra(qs, ks, levels)
        s_ins = [shist_ref[0, hh] for hh in heads_here]
        ds_outs = [ds_sc[hh] for hh in heads_here]
        ebs = [jnp.exp(b) for b in bs]
        etails = [jnp.exp(b_lasts[hh] - bs[hh]) for hh in heads_here]
        decays = [jnp.exp(bl) for bl in b_lasts]

        dos = []
        for hh in heads_here:
            o = orec_ref[:, lanes[hh]]
            hg = hg_ref[:, lanes[hh]].astype(F32)
            sgg = _sigmoid(hg)
            rn = lax.rsqrt(jnp.mean(o * o, axis=-1, keepdims=True) + NORM_EPS)
            on = o * rn
            doh = do_ref[:, lanes[hh]]
            dy = doh * (hg * sgg)
            dhg_ref[:, lanes[hh]] = (doh * (on * gn_ref[...]) * (sgg * (1.0 + hg * (1.0 - sgg)))).astype(BF16)
            dgn_ref[hh] += jnp.broadcast_to(jnp.sum(dy * on, axis=0, keepdims=True), (8, HEAD))
            z = dy * gn_ref[...]
            dos.append(rn * z - o * ((rn * rn * rn) * jnp.mean(z * o, axis=-1, keepdims=True)))

        tt = lax.broadcasted_iota(jnp.int32, (CHUNK, CHUNK), 0)
        ss = lax.broadcasted_iota(jnp.int32, (CHUNK, CHUNK), 1)
        dps = [jnp.where(ss <= tt, _dot(dos[hh], vs[hh], NT), 0.0) for hh in heads_here]
        dvs = [_dot(ps[hh], dos[hh], TN) + _dot(ks[hh] * etails[hh], ds_outs[hh], NT) for hh in heads_here]
        dq_states = [ebs[hh] * _dot(dos[hh], s_ins[hh], NN) for hh in heads_here]
        dk_states = [etails[hh] * _dot(vs[hh], ds_outs[hh], NN) for hh in heads_here]
        dpds = [jnp.sum(jnp.where(tt == ss, dp, 0.0), axis=-1, keepdims=True) for dp in dps]
        dqs = [dpds[hh] * ks[hh] + dq_states[hh] for hh in heads_here]
        dks = [dpds[hh] * qs[hh] + dk_states[hh] for hh in heads_here]
        pair_terms = [[] for _ in heads_here]
        for lv in range(LEVELS):
            for hh in heads_here:
                eq, ek, ql, kl, same = levels[hh][lv]
                dpl = dps[hh] * same
                dq_l = eq * _dot(dpl, kl, NN)
                dk_l = ek * _dot(dpl, ql, TN)
                dqs[hh] = dqs[hh] + dq_l
                dks[hh] = dks[hh] + dk_l
                pair_terms[hh].append(qs[hh] * dq_l + ks[hh] * dk_l)
        for hh in heads_here:
            pair_terms[hh] += [qs[hh] * dq_states[hh], ks[hh] * dk_states[hh]]
            ds_sc[hh] = decays[hh] * ds_outs[hh] + _dot(dos[hh], qs[hh] * ebs[hh], TN)
        stacked = [jnp.concatenate(terms, axis=0) for terms in pair_terms]
        dg_all = _dot_exact_rhs(tdec_ref[...], jnp.concatenate(stacked, axis=1) if per > 1 else stacked[0], NN)

        for hh in heads_here:
            w = ws[hh]
            f, sg, lb, sq, hq = w["f"], w["sg"], w["lb"], w["sq"], w["hq"]
            through = jnp.sum((decays[hh] * s_ins[hh]) * ds_outs[hh], axis=0, keepdims=True)
            dg = dg_all[:, lanes[hh]] + through
            df = jnp.where(w["valid"], dg / f - dks[hh], 0.0)
            dhf_ref[:, lanes[hh]] = (df * (1.0 - lb) * sg * (1.0 - sg)).astype(BF16)
            dlb_ref[hh] += jnp.broadcast_to(jnp.sum(df * (1.0 - sg), axis=0, keepdims=True), (8, HEAD))
            dhq_ref[:, lanes[hh]] = (dqs[hh] * (sq * (1.0 + hq * (1.0 - sq)))).astype(BF16)
            dhi_ref[:, lanes[hh]] = dvs[hh].astype(BF16)

    col = lambda grp: pl.BlockSpec((CHUNK, per * HEAD), lambda h, c: (nc - 1 - c, grp * ng + h))
    const = lambda shape: pl.BlockSpec(shape, lambda h, c: (0, 0))
    tile = pl.BlockSpec((CHUNK, per * HEAD), lambda h, c: (nc - 1 - c, h))
    part = pl.BlockSpec((per, 8, HEAD), lambda h, c: (h, 0, 0))
    return pl.pallas_call(
        body, name="hgrn_bwd",
        out_shape=tuple([jax.ShapeDtypeStruct((L, D), BF16)] * 4 + [jax.ShapeDtypeStruct((nh, 8, HEAD), F32)] * 2),
        grid=(ng, nc),
        in_specs=[col(0), col(1), col(2), col(3),
                  pl.BlockSpec((2, per * HEAD), lambda h, c: (0, h)), const((1, HEAD)),
                  const((CHUNK, CHUNK)), const((LEVELS * CHUNK, CHUNK)), const((LEVELS * CHUNK, CHUNK)),
                  const((CHUNK, (LEVELS + 2) * CHUNK)),
                  tile, tile, pl.BlockSpec((1, per, HEAD, HEAD), lambda h, c: (nc - 1 - c, h, 0, 0))],
        out_specs=(tile, tile, tile, tile, part, part),
        scratch_shapes=[pltpu.VMEM((per, HEAD, HEAD), F32), pltpu.VMEM((per, CHUNK, HEAD), F32)],
        compiler_params=_params("parallel", "arbitrary"),
    )(hp, hp, hp, hp, lb_raw, g_norm, tril, sel, blk, tdec, do_h, o_rec, s_hist)


def _merge_fwd(a, bm, gates):
    L, D = a.shape
    tm = _tile(L, (128,))

    def body(a_ref, b_ref, g_ref, o_ref):
        sa, sb = _sigmoid(g_ref[:, :D].astype(F32)), _sigmoid(g_ref[:, D:].astype(F32))
        o_ref[...] = (sa * a_ref[...] + sb * b_ref[...]).astype(BF16)

    row = lambda w: pl.BlockSpec((tm, w), lambda i: (i, 0))
    return pl.pallas_call(
        body, name="merge_fwd", out_shape=jax.ShapeDtypeStruct((L, D), BF16), grid=(L // tm,),
        in_specs=[row(D), row(D), row(2 * D)], out_specs=row(D), compiler_params=_params("parallel"),
    )(a, bm, gates)


def _merge_bwd(dm, a, bm, gates):
    L, D = a.shape
    tm = _tile(L, (128,))

    def body(dm_ref, a_ref, b_ref, g_ref, da_ref, db_ref, dg_ref):
        d = dm_ref[...]
        sa, sb = _sigmoid(g_ref[:, :D].astype(F32)), _sigmoid(g_ref[:, D:].astype(F32))
        da_ref[...] = (d * sa).astype(BF16)
        db_ref[...] = (d * sb).astype(BF16)
        dg_ref[:, :D] = (d * a_ref[...] * sa * (1.0 - sa)).astype(BF16)
        dg_ref[:, D:] = (d * b_ref[...] * sb * (1.0 - sb)).astype(BF16)

    row = lambda w: pl.BlockSpec((tm, w), lambda i: (i, 0))
    return pl.pallas_call(
        body, name="merge_bwd",
        out_shape=(jax.ShapeDtypeStruct((L, D), BF16), jax.ShapeDtypeStruct((L, D), BF16),
                   jax.ShapeDtypeStruct((L, 2 * D), BF16)),
        grid=(L // tm,),
        in_specs=[row(D), row(D), row(D), row(2 * D)], out_specs=(row(D), row(D), row(2 * D)),
        compiler_params=_params("parallel"),
    )(dm, a, bm, gates)


def _conv_taps(g_ref, halo_ref, i, tm):
    rows = i * tm + lax.broadcasted_iota(jnp.int32, (tm, 1), 0)
    g0 = jnp.where(rows >= PAD_LEN, g_ref[...].astype(F32), 0.0)
    sub = lax.broadcasted_iota(jnp.int32, (HALO, 1), 0)
    hrow = i * tm - HALO + sub
    halo = jnp.where(hrow >= PAD_LEN, halo_ref[...].astype(F32), 0.0)
    r = lax.broadcasted_iota(jnp.int32, (tm, 1), 0)
    h7 = jnp.sum(jnp.where(sub == HALO - 1, halo, 0.0), axis=0, keepdims=True)
    h6 = jnp.sum(jnp.where(sub == HALO - 2, halo, 0.0), axis=0, keepdims=True)
    g1 = jnp.where(r == 0, h7, pltpu.roll(g0, 1, 0))
    g2 = jnp.where(r == 0, h6, jnp.where(r == 1, h7, pltpu.roll(g0, 2, 0)))
    return g0, g1, g2


def _conv_fwd(gu, cw, cb):
    L, F2 = gu.shape
    F = F2 // 2
    tm = _tile(L, ROW_TILES)
    tn = _tile(F, (512, 256, 128))
    nj = F // tn

    def body(g_ref, halo_ref, u_ref, cw_ref, cb_ref, o_ref):
        g0, g1, g2 = _conv_taps(g_ref, halo_ref, pl.program_id(0), tm)
        conv = cw_ref[0:1, :] * g2 + cw_ref[1:2, :] * g1 + cw_ref[2:3, :] * g0 + cb_ref[...]
        o_ref[...] = (conv * _sigmoid(conv) * u_ref[...].astype(F32)).astype(BF16)

    return pl.pallas_call(
        body, name="conv_fwd", out_shape=jax.ShapeDtypeStruct((L, F), BF16), grid=(L // tm, nj),
        in_specs=[pl.BlockSpec((tm, tn), lambda i, j: (i, j)),
                  pl.BlockSpec((HALO, tn), lambda i, j: (jnp.maximum(i * (tm // HALO) - 1, 0), j)),
                  pl.BlockSpec((tm, tn), lambda i, j: (i, j + nj)),
                  pl.BlockSpec((3, tn), lambda i, j: (0, j)),
                  pl.BlockSpec((1, tn), lambda i, j: (0, j))],
        out_specs=pl.BlockSpec((tm, tn), lambda i, j: (i, j)),
        compiler_params=_params("parallel", "parallel"),
    )(gu, gu, gu, cw, cb)


def _conv_bwd_a(da, gu, cw, cb):
    L, F2 = gu.shape
    F = F2 // 2
    tm = _tile(L, ROW_TILES)
    tn = _tile(F, (512, 256, 128))
    nj = F // tn

    def body(da_ref, g_ref, halo_ref, u_ref, cw_ref, cb_ref, dc_ref, du_ref, dcb_ref, dcw_ref):
        i = pl.program_id(1)

        @pl.when(i == 0)
        def _():
            dcb_ref[...] = jnp.zeros_like(dcb_ref)
            dcw_ref[...] = jnp.zeros_like(dcw_ref)

        g0, g1, g2 = _conv_taps(g_ref, halo_ref, i, tm)
        conv = cw_ref[0:1, :] * g2 + cw_ref[1:2, :] * g1 + cw_ref[2:3, :] * g0 + cb_ref[...]
        sc = _sigmoid(conv)
        dav = da_ref[...]
        du_ref[...] = (dav * (conv * sc)).astype(BF16)
        dconv = dav * u_ref[...].astype(F32) * (sc * (1.0 + conv * (1.0 - sc)))
        dc_ref[...] = dconv
        dcb_ref[...] += jnp.sum(dconv, axis=0, keepdims=True)
        dcw_ref[0:1, :] += jnp.sum(dconv * g2, axis=0, keepdims=True)
        dcw_ref[1:2, :] += jnp.sum(dconv * g1, axis=0, keepdims=True)
        dcw_ref[2:3, :] += jnp.sum(dconv * g0, axis=0, keepdims=True)

    return pl.pallas_call(
        body, name="conv_bwd_a",
        out_shape=(jax.ShapeDtypeStruct((L, F), F32), jax.ShapeDtypeStruct((L, 2 * F), BF16),
                   jax.ShapeDtypeStruct((1, F), F32), jax.ShapeDtypeStruct((8, F), F32)),
        grid=(nj, L // tm),
        in_specs=[pl.BlockSpec((tm, tn), lambda j, i: (i, j)),
                  pl.BlockSpec((tm, tn), lambda j, i: (i, j)),
                  pl.BlockSpec((HALO, tn), lambda j, i: (jnp.maximum(i * (tm // HALO) - 1, 0), j)),
                  pl.BlockSpec((tm, tn), lambda j, i: (i, j + nj)),
                  pl.BlockSpec((3, tn), lambda j, i: (0, j)),
                  pl.BlockSpec((1, tn), lambda j, i: (0, j))],
        out_specs=(pl.BlockSpec((tm, tn), lambda j, i: (i, j)), pl.BlockSpec((tm, tn), lambda j, i: (i, j + nj)),
                   pl.BlockSpec((1, tn), lambda j, i: (0, j)), pl.BlockSpec((8, tn), lambda j, i: (0, j))),
        compiler_params=_params("parallel", "arbitrary"),
    )(da, gu, gu, gu, cw, cb)


def _conv_bwd_b(dconv, cw, dgu):
    L, F = dconv.shape
    tm = _tile(L, ROW_TILES)
    tn = _tile(F, (512, 256, 128))
    nblk8 = L // 8
    ni = L // tm

    def body(dc_ref, nxt_ref, cw_ref, dgu_ref, o_ref):
        i = pl.program_id(0)
        dc = dc_ref[...]
        nxt = jnp.where(i < ni - 1, nxt_ref[...], 0.0)
        sub = lax.broadcasted_iota(jnp.int32, (8, 1), 0)
        n0 = jnp.sum(jnp.where(sub == 0, nxt, 0.0), axis=0, keepdims=True)
        n1 = jnp.sum(jnp.where(sub == 1, nxt, 0.0), axis=0, keepdims=True)
        r = lax.broadcasted_iota(jnp.int32, (tm, 1), 0)
        d1 = jnp.where(r == tm - 1, n0, pltpu.roll(dc, tm - 1, 0))
        d2 = jnp.where(r == tm - 2, n0, jnp.where(r == tm - 1, n1, pltpu.roll(dc, tm - 2, 0)))
        dg = cw_ref[2:3, :] * dc + cw_ref[1:2, :] * d1 + cw_ref[0:1, :] * d2
        rows = i * tm + r
        o_ref[...] = jnp.where(rows >= PAD_LEN, dg, 0.0).astype(BF16)

    return pl.pallas_call(
        body, name="conv_bwd_b", out_shape=jax.ShapeDtypeStruct(dgu.shape, BF16), grid=(ni, F // tn),
        in_specs=[pl.BlockSpec((tm, tn), lambda i, j: (i, j)),
                  pl.BlockSpec((8, tn), lambda i, j: (jnp.minimum((i + 1) * (tm // 8), nblk8 - 1), j)),
                  pl.BlockSpec((3, tn), lambda i, j: (0, j)), ANY],
        out_specs=pl.BlockSpec((tm, tn), lambda i, j: (i, j)),
        input_output_aliases={3: 0},
        compiler_params=_params("parallel", "parallel"),
    )(dconv, dconv, cw, dgu)


ANY = pl.BlockSpec(memory_space=pl.ANY)


def _coords():
    return lax.axis_index("x"), lax.axis_index("y"), lax.axis_index("c")


def _flip(v, bit):
    return 1 - v if bit else v


CHIPS = [(1, 0), (0, 1), (1, 1)]
PEERS = [(dx, dy, dc) for dx in (0, 1) for dy in (0, 1) for dc in (0, 1)][1:]


class _Rider:
    def __init__(self, operands, out_shapes, sem_shapes, start, finish):
        self.operands, self.out_shapes, self.sem_shapes = list(operands), list(out_shapes), list(sem_shapes)
        self.start, self.finish = start, finish


def _run_rider(rider, name):
    n_in, n_out = len(rider.operands), len(rider.out_shapes)

    def body(*refs):
        ins, outs, sems = refs[:n_in], refs[n_in:n_in + n_out], refs[n_in + n_out:]
        rider.start(ins, outs, sems)
        rider.finish(ins, outs, sems)

    return pl.pallas_call(
        body, name=name, out_shape=tuple(rider.out_shapes),
        in_specs=[ANY] * n_in, out_specs=tuple([ANY] * n_out), scratch_shapes=rider.sem_shapes,
        compiler_params=pltpu.CompilerParams(has_side_effects=True),
    )(*rider.operands)


def _gather_rider(big, small):
    nbig, n = len(big), len(big) + len(small)
    arrays = list(big) + list(small)

    def plan(ins, outs, sems):
        ici_send, ici_recv, d2d_send, d2d_recv, local_sems = sems
        x, y, c = _coords()
        mine = 2 * x + y

        def half(w, h):
            r2 = arrays[w].shape[0] // 2
            return pl.ds(h * r2, r2)

        def ici(w, j, landing):
            px, py = _flip(x, CHIPS[j][0]), _flip(y, CHIPS[j][1])
            slot = 2 * px + py if landing else mine
            if w < nbig:
                src, dst = ins[w].at[half(w, c)], outs[w].at[slot, half(w, c)]
            else:
                src, dst = ins[w], outs[w].at[slot]
            return pltpu.make_async_remote_copy(
                src_ref=src, dst_ref=dst, send_sem=ici_send.at[w * 3 + j], recv_sem=ici_recv.at[w * 3 + j],
                device_id=(px, py, c), device_id_type=MESH)

        def d2d(w, j, landing):
            px, py = _flip(x, CHIPS[j][0]), _flip(y, CHIPS[j][1])
            mine_rows = outs[w].at[2 * px + py, half(w, c)]
            dst = outs[w].at[2 * px + py, half(w, 1 - c)] if landing else mine_rows
            return pltpu.make_async_remote_copy(
                src_ref=mine_rows, dst_ref=dst, send_sem=d2d_send.at[w * 3 + j], recv_sem=d2d_recv.at[w * 3 + j],
                device_id=(x, y, 1 - c), device_id_type=MESH)

        local = [pltpu.make_async_copy(ins[w], outs[w].at[mine], local_sems.at[w]) for w in range(n)]
        return ici, d2d, local

    def start(ins, outs, sems):
        ici, _, local = plan(ins, outs, sems)
        for cp in local:
            cp.start()
        for w in range(n):
            for j in range(3):
                ici(w, j, False).start()

    def finish(ins, outs, sems):
        ici, d2d, local = plan(ins, outs, sems)
        for w in range(n):
            for j in range(3):
                ici(w, j, True).wait_recv()
                if w < nbig:
                    d2d(w, j, False).start()
        for w in range(nbig):
            for j in range(3):
                d2d(w, j, True).wait_recv()
        for w in range(n):
            for j in range(3):
                ici(w, j, False).wait_send()
                if w < nbig:
                    d2d(w, j, False).wait_send()
        for cp in local:
            cp.wait()

    return _Rider(
        arrays, [jax.ShapeDtypeStruct((4,) + s.shape, s.dtype) for s in arrays],
        [pltpu.SemaphoreType.DMA((3 * n,)), pltpu.SemaphoreType.DMA((3 * n,)),
         pltpu.SemaphoreType.DMA((max(3 * nbig, 1),)), pltpu.SemaphoreType.DMA((max(3 * nbig, 1),)),
         pltpu.SemaphoreType.DMA((n,))],
        start, finish)


def _to_sibling(arrays, name):
    n = len(arrays)

    def body(*refs):
        ins, outs = refs[:n], refs[n:2 * n]
        send_sems, recv_sems = refs[2 * n:]
        x, y, c = _coords()

        def copy(w):
            return pltpu.make_async_remote_copy(
                src_ref=ins[w], dst_ref=outs[w], send_sem=send_sems.at[w], recv_sem=recv_sems.at[w],
                device_id=(x, y, 1 - c), device_id_type=MESH)

        for w in range(n):
            copy(w).start()
        for w in range(n):
            copy(w).wait_recv()
            copy(w).wait_send()

    return pl.pallas_call(
        body, name=name,
        out_shape=tuple(jax.ShapeDtypeStruct(a.shape, a.dtype) for a in arrays),
        in_specs=[ANY] * n, out_specs=tuple([ANY] * n),
        scratch_shapes=[pltpu.SemaphoreType.DMA((n,)), pltpu.SemaphoreType.DMA((n,))],
        compiler_params=pltpu.CompilerParams(has_side_effects=True),
    )(*arrays)


def _pair_sum(a, b, name):
    _, r, c = a.shape
    tr = _tile(r, (64, 32, 16))

    def body(a_ref, b_ref, o_ref):
        o_ref[...] = (a_ref[...].astype(F32) + b_ref[...].astype(F32)).astype(BF16)

    blk = pl.BlockSpec((4, tr, c), lambda i: (0, i, 0))
    return pl.pallas_call(
        body, name=name, out_shape=jax.ShapeDtypeStruct(a.shape, BF16), grid=(r // tr,),
        in_specs=[blk, blk], out_specs=blk, compiler_params=_params("parallel"),
    )(a, b)


def _scatter_rider(parts):
    n = len(parts)

    def plan(ins, outs, sems):
        send_sems, recv_sems, local_sems = sems
        x, y, c = _coords()
        mine = 2 * x + y

        def ici(w, j, landing):
            px, py = _flip(x, CHIPS[j][0]), _flip(y, CHIPS[j][1])
            return pltpu.make_async_remote_copy(
                src_ref=ins[w].at[2 * px + py], dst_ref=outs[w].at[2 * px + py if landing else mine],
                send_sem=send_sems.at[w * 3 + j], recv_sem=recv_sems.at[w * 3 + j],
                device_id=(px, py, c), device_id_type=MESH)

        local = [pltpu.make_async_copy(ins[w].at[mine], outs[w].at[mine], local_sems.at[w]) for w in range(n)]
        return ici, local

    def start(ins, outs, sems):
        ici, local = plan(ins, outs, sems)
        for cp in local:
            cp.start()
        for w in range(n):
            for j in range(3):
                ici(w, j, False).start()

    def finish(ins, outs, sems):
        ici, local = plan(ins, outs, sems)
        for w in range(n):
            for j in range(3):
                ici(w, j, True).wait_recv()
                ici(w, j, False).wait_send()
        for cp in local:
            cp.wait()

    return _Rider(
        parts, [jax.ShapeDtypeStruct(p.shape, p.dtype) for p in parts],
        [pltpu.SemaphoreType.DMA((3 * n,)), pltpu.SemaphoreType.DMA((3 * n,)), pltpu.SemaphoreType.DMA((n,))],
        start, finish)


def _sum4(recv, name):
    _, r, c = recv.shape
    tr = _tile(r, (64, 32, 16))

    def body(in_ref, o_ref):
        o_ref[...] = ((in_ref[0].astype(F32) + in_ref[1].astype(F32)) + in_ref[2].astype(F32)) + in_ref[3].astype(F32)

    return pl.pallas_call(
        body, name=name, out_shape=jax.ShapeDtypeStruct((r, c), F32), grid=(r // tr,),
        in_specs=[pl.BlockSpec((4, tr, c), lambda i: (0, i, 0))],
        out_specs=pl.BlockSpec((tr, c), lambda i: (i, 0)),
        compiler_params=_params("parallel"),
    )(recv)


def _allreduce_small(packed):
    R = packed.shape[0]

    def body(in_ref, o_ref, buf, send_sems, recv_sems):
        x, y, c = _coords()
        me = 4 * x + 2 * y + c
        buf[me] = in_ref[...]
        for j, (dx, dy, dc) in enumerate(PEERS):
            px, py, pc = _flip(x, dx), _flip(y, dy), _flip(c, dc)
            pltpu.make_async_remote_copy(
                src_ref=in_ref, dst_ref=buf.at[me], send_sem=send_sems.at[j], recv_sem=recv_sems.at[j],
                device_id=(px, py, pc), device_id_type=MESH).start()
        for j, (dx, dy, dc) in enumerate(PEERS):
            px, py, pc = _flip(x, dx), _flip(y, dy), _flip(c, dc)
            rc = pltpu.make_async_remote_copy(
                src_ref=in_ref, dst_ref=buf.at[4 * px + 2 * py + pc], send_sem=send_sems.at[j],
                recv_sem=recv_sems.at[j], device_id=(px, py, pc), device_id_type=MESH)
            rc.wait_recv()
            rc.wait_send()
        acc = buf[0]
        for s in range(1, 8):
            acc = acc + buf[s]
        o_ref[...] = acc

    return pl.pallas_call(
        body, name="allreduce_small", out_shape=jax.ShapeDtypeStruct((R, 128), F32),
        in_specs=[pl.BlockSpec(memory_space=pltpu.VMEM)], out_specs=pl.BlockSpec(memory_space=pltpu.VMEM),
        scratch_shapes=[pltpu.VMEM((8, R, 128), F32), pltpu.SemaphoreType.DMA((7,)), pltpu.SemaphoreType.DMA((7,))],
        compiler_params=pltpu.CompilerParams(has_side_effects=True, vmem_limit_bytes=VMEM_LIMIT_BYTES),
    )(packed)


def _adamw_math(w, g, m, v):
    m = ADAM_B1 * m + (1.0 - ADAM_B1) * g
    v = ADAM_B2 * v + (1.0 - ADAM_B2) * (g * g)
    m_hat = m / (1.0 - ADAM_B1 ** ADAM_STEP)
    v_hat = v / (1.0 - ADAM_B2 ** ADAM_STEP)
    delta = -ADAM_LR * (m_hat / (jnp.sqrt(v_hat) + ADAM_EPS) + ADAM_WD * w)
    return delta, m, v


def _adamw_big(mine, other, w, m, v, name):
    R, C = w.shape
    tr = _tile(R // 2, (128, 64, 32, 16, 8))
    nb = (R // 2) // tr

    def body(mine_ref, other_ref, w_ref, m_ref, v_ref, g_ref, d_ref, mo_ref, vo_ref):
        is_mine = (pl.program_id(0) // nb) == lax.axis_index("c")
        g = jnp.where(is_mine, mine_ref[...], other_ref[...])
        d, mn, vn = _adamw_math(w_ref[...], g, m_ref[...], v_ref[...])
        g_ref[...] = g
        d_ref[...] = d
        mo_ref[...] = mn
        vo_ref[...] = vn

    blk = pl.BlockSpec((tr, C), lambda i: (i, 0))
    half = pl.BlockSpec((tr, C), lambda i: (i % nb, 0))
    sds = jax.ShapeDtypeStruct((R, C), F32)
    return pl.pallas_call(
        body, name=name, out_shape=(sds, sds, sds, sds), grid=(2 * nb,),
        in_specs=[half, half, blk, blk, blk], out_specs=(blk, blk, blk, blk), compiler_params=_params("parallel"),
    )(mine, other, w, m, v)


def _adamw_small(items, lb_raw, dlb):
    n = len(items)
    lb_w, lb_m, lb_v = lb_raw

    def body(*refs):
        ins = refs[:4 * n]
        dlb_ref, lw_ref, lm_ref, lv_ref = refs[4 * n:4 * n + 4]
        outs = refs[4 * n + 4:]
        for t in range(n):
            g_ref, w_ref, m_ref, v_ref = ins[4 * t:4 * t + 4]
            d, mn, vn = _adamw_math(w_ref[...], g_ref[...], m_ref[...], v_ref[...])
            outs[3 * t][...] = d
            outs[3 * t + 1][...] = mn
            outs[3 * t + 2][...] = vn
        p0 = 1.0 / (1.0 + jnp.exp(lw_ref[1:2, :] - lw_ref[0:1, :]))
        g0 = dlb_ref[...] * p0 * (1.0 - p0)
        base = 3 * n
        outs[base][0:1, :] = g0
        outs[base][1:2, :] = -g0
        d, mn, vn = _adamw_math(lw_ref[...], outs[base][...], lm_ref[...], lv_ref[...])
        outs[base + 1][...] = d
        outs[base + 2][...] = mn
        outs[base + 3][...] = vn

    operands = [a for it in items for a in it] + [dlb, lb_w, lb_m, lb_v]
    out_shape = []
    for (g, w, m, v) in items:
        out_shape += [jax.ShapeDtypeStruct(w.shape, F32)] * 3
    out_shape += [jax.ShapeDtypeStruct(lb_w.shape, F32)] * 4
    vm = pl.BlockSpec(memory_space=pltpu.VMEM)
    res = pl.pallas_call(
        body, name="adamw_small", out_shape=tuple(out_shape),
        in_specs=[vm] * len(operands), out_specs=tuple([vm] * len(out_shape)),
        compiler_params=pltpu.CompilerParams(vmem_limit_bytes=VMEM_LIMIT_BYTES),
    )(*operands)
    deltas = [res[3 * t] for t in range(n)] + [res[3 * n + 1]]
    new_m = [res[3 * t + 1] for t in range(n)] + [res[3 * n + 2]]
    new_v = [res[3 * t + 2] for t in range(n)] + [res[3 * n + 3]]
    return res[3 * n], deltas, new_m, new_v


def _shard_row_half(g, by_cols, h):
    if isinstance(g, (list, tuple)):
        R, widths = g[0].shape[0], [s.shape[1] for s in g]
        starts = [sum(widths[:n]) for n in range(len(g))]
        cs = sum(widths) // 4
        rows = [lax.dynamic_slice_in_dim(s, h * (R // 2), R // 2, axis=0) for s in g]
        shards = []
        for k in range(4):
            lo, hi = k * cs, (k + 1) * cs
            cut = [r[:, max(lo - o, 0):min(hi - o, w)] for r, o, w in zip(rows, starts, widths) if lo < o + w and hi > o]
            shards.append(jnp.concatenate(cut, axis=1).astype(BF16))
        return jnp.stack(shards)
    if g.ndim == 3:
        return lax.dynamic_slice_in_dim(g, h * (g.shape[1] // 2), g.shape[1] // 2, axis=1).astype(BF16)
    R, C = g.shape
    if by_cols:
        part = lax.dynamic_index_in_dim(g.reshape(2, R // 2, 4, C // 4), h, axis=0, keepdims=False)
        return part.transpose(1, 0, 2).astype(BF16)
    return lax.dynamic_index_in_dim(g.reshape(4, 2, R // 8, C), h, axis=1, keepdims=False).astype(BF16)


def kernel(x, positions, meta_tokens, w_in, w_q_up, w_kv_up, w_branch_mla, w_branch_hgrn, w_out, w_ffn_in, w_ffn_out, conv_w, conv_b, g_mix_norm, g_q_norm, g_kv_norm, g_hgrn_norm, g_ffn_norm, g_final_norm, lb_raw, loss_target, m_meta_tokens, m_w_in, m_w_q_up, m_w_kv_up, m_w_branch_mla, m_w_branch_hgrn, m_w_out, m_w_ffn_in, m_w_ffn_out, m_conv_w, m_conv_b, m_g_mix_norm, m_g_q_norm, m_g_kv_norm, m_g_hgrn_norm, m_g_ffn_norm, m_g_final_norm, m_lb_raw, v_meta_tokens, v_w_in, v_w_q_up, v_w_kv_up, v_w_branch_mla, v_w_branch_hgrn, v_w_out, v_w_ffn_in, v_w_ffn_out, v_conv_w, v_conv_b, v_g_mix_norm, v_g_q_norm, v_g_kv_norm, v_g_hgrn_norm, v_g_ffn_norm, v_g_final_norm, v_lb_raw):
    S, D = x.shape[1], x.shape[2]
    L = S + PREFIX
    QL, KVL = g_q_norm.shape[1], g_kv_norm.shape[1]
    F = conv_b.shape[1]
    heads = (4 * w_kv_up.shape[2]) // QPAD
    nh = D // HEAD
    assert lb_raw.shape[0] == 2 and g_hgrn_norm.shape[1] == HEAD and L % CHUNK == 0
    ix, iy, ic = _coords()
    chip = 2 * ix + iy

    big = [w_in, w_q_up, w_kv_up, w_branch_mla, w_branch_hgrn, w_out, w_ffn_in, w_ffn_out]
    col_sharded = [True, True, True, False, False, False, True, False]
    shards = [w[0].astype(BF16) for w in big]
    early = _run_rider(_gather_rider(shards[:1], [meta_tokens, conv_w[0]]), "gather_early")
    qkv_rider = _gather_rider(shards[1:3], [])
    late_rider = _gather_rider(shards[3:], [])

    def full(gw, by_cols):
        _, r, c = gw.shape
        return gw.transpose(1, 0, 2).reshape(r, 4 * c) if by_cols else gw.reshape(4 * r, c)

    def col_range(g4, lo, hi):
        c = g4.shape[2]
        pieces = [g4[k][:, max(lo - k * c, 0):min(hi - k * c, c)] for k in range(4) if lo < (k + 1) * c and hi > k * c]
        return pieces[0] if len(pieces) == 1 else jnp.concatenate(pieces, axis=1)

    meta_full = full(early[1], True)
    cw_full = full(early[2], True)
    c0 = QL + KVL
    W_lat = col_range(early[0], 0, c0)
    W_kr = jnp.pad(col_range(early[0], c0, c0 + ROPE), ((0, 0), (0, 128 - ROPE)))
    W_H = col_range(early[0], c0 + ROPE, c0 + ROPE + 4 * D)
    W_G = col_range(early[0], c0 + ROPE + 4 * D, 4 * early[0].shape[2])

    pos = jnp.concatenate([jnp.zeros((PAD_LEN,), jnp.int32), jnp.arange(N_META, dtype=jnp.int32),
                           positions[0].astype(jnp.int32) + N_META])
    inv = 1.0 / (ROPE_THETA ** (jnp.arange(0, ROPE, 2, dtype=F32) / ROPE))
    ang = pos.astype(F32)[:, None] * inv
    zero = jnp.zeros((L, 128 - ROPE), F32)
    cos = jnp.concatenate([jnp.cos(ang), jnp.cos(ang), zero], axis=1)
    sin = jnp.concatenate([jnp.sin(ang), jnp.sin(ang), zero], axis=1)

    h0, u1 = _embed_norm(x[0], meta_full, g_mix_norm)
    hp, qkv = _mm(u1, W_H, "nn", BF16, "proj_hgrn", rider=qkv_rider)
    W_q, W_kv = [full(gw, True) for gw in qkv]
    W_qp = jnp.pad(W_q.reshape(QL, heads, QK_HEAD), ((0, 0), (0, 0), (0, QPAD - QK_HEAD))).reshape(QL, heads * QPAD)
    lat = _mm(u1, W_lat, "nn", BF16, "proj_lat")
    gates = _mm(u1, W_G, "nn", BF16, "proj_gates")
    kr = _mm(u1, W_kr, "nn", F32, "proj_krope")
    qn = _rmsnorm_fwd(lat, g_q_norm, "norm_q", col0=0, width=QL)
    kvn = _rmsnorm_fwd(lat, g_kv_norm, "norm_kv", col0=QL, width=KVL)
    qp = _mm(qn, W_qp, "nn", BF16, "q_up")
    kv = _mm(kvn, W_kv, "nn", BF16, "kv_up")
    qc, kc, vv = _rope_fwd(qp, kv, kr, cos, sin, heads)
    o_mla, lse, late = _attn_fwd(qc, kc, vv, heads, late_rider)
    W_a, W_b, W_o, W_fi, W_fo = [full(gw, bc) for gw, bc in zip(late, col_sharded[3:])]
    o_hgrn, o_rec, s_hist = _hgrn_fwd(hp, lb_raw, g_hgrn_norm, nh)
    br_a = _mm(o_mla, W_a, "nn", BF16, "branch_mla")
    br_b = _mm(o_hgrn, W_b, "nn", BF16, "branch_hgrn")
    merged = _merge_fwd(br_a, br_b, gates)
    h1 = _mm(merged, W_o, "nn", F32, "out_proj", res=h0)
    u2 = _rmsnorm_fwd(h1, g_ffn_norm, "norm_ffn")
    gu = _mm(u2, W_fi, "nn", BF16, "ffn_in")
    act = _conv_fwd(gu, cw_full, conv_b)
    h2 = _mm(act, W_fo, "nn", F32, "ffn_out", res=h1)
    dh2, loss_p, dg_final = _final_loss_bwd(h2, loss_target[0], g_final_norm.reshape(1, D))

    dact = _mm(dh2, W_fo, "nt", F32, "d_act")
    dW_fo = _mm(act, dh2, "tn", F32, "dw_ffn_out")
    dconv, dgu_right, dcb, dcw = _conv_bwd_a(dact, gu, cw_full, conv_b)
    dgu = _conv_bwd_b(dconv, cw_full, dgu_right)
    du2 = _mm(dgu, W_fi, "nt", F32, "d_u2")
    dW_fi = _mm(u2, dgu, "tn", F32, "dw_ffn_in", col_shards=4)
    dh1, dg_ffn = _rmsnorm_bwd(du2, h1, g_ffn_norm, "norm_ffn_bwd", F32, res=dh2)
    dmerged = _mm(dh1, W_o, "nt", F32, "d_merged")
    dW_o = _mm(merged, dh1, "tn", F32, "dw_out")
    d_a, d_b, d_gates = _merge_bwd(dmerged, br_a, br_b, gates)
    do_mla = _mm(d_a, W_a, "nt", BF16, "d_o_mla")
    dW_a = _mm(o_mla, d_a, "tn", F32, "dw_branch_mla")
    do_hgrn = _mm(d_b, W_b, "nt", F32, "d_o_hgrn")
    dW_b = _mm(o_hgrn, d_b, "tn", F32, "dw_branch_hgrn")
    names = ["w_in", "w_q_up", "w_kv_up", "w_branch_mla", "w_branch_hgrn", "w_out", "w_ffn_in", "w_ffn_out"]

    def chip_partials(grads, by_cols, nms, tag):
        keep = [_shard_row_half(g, bc, ic) for g, bc in zip(grads, by_cols)]
        give = [_shard_row_half(g, bc, 1 - ic) for g, bc in zip(grads, by_cols)]
        taken = _to_sibling(give, "pair_exchange_" + tag)
        return [_pair_sum(a, b, "pair_sum_" + nm) for a, b, nm in zip(keep, taken, nms)]

    late_parts = chip_partials([dW_a, dW_b, dW_o, dW_fi, dW_fo], col_sharded[3:], names[3:], "late")
    dhq, dhf, dhi, dhg, dgn_p, dlb_p = _hgrn_bwd(hp, lb_raw, g_hgrn_norm, do_hgrn, o_rec, s_hist, nh)
    dqc, dkc, dvv, late_recv = _attn_bwd(qc, kc, vv, do_mla, lse, _attn_delta(o_mla, do_mla, heads), heads,
                                         _scatter_rider(late_parts))
    dqp, dkv, dkr = _rope_bwd(dqc, dkc, dvv, cos, sin, heads)
    dqn = _mm(dqp, W_qp, "nt", F32, "d_qn")
    dW_qp = _mm(qn, dqp, "tn", F32, "dw_q_up")
    dkvn = _mm(dkv, W_kv, "nt", F32, "d_kvn")
    dW_kv = _mm(kvn, dkv, "tn", F32, "dw_kv_up")
    dq_lat, dg_q = _rmsnorm_bwd(dqn, lat, g_q_norm, "norm_q_bwd", BF16, col0=0)
    dkv_lat, dg_kv = _rmsnorm_bwd(dkvn, lat, g_kv_norm, "norm_kv_bwd", BF16, col0=QL)
    dlat = jnp.concatenate([dq_lat, dkv_lat], axis=1)
    dhp = jnp.concatenate([dhq, dhf, dhi, dhg], axis=1)
    dW_lat = _mm(u1, dlat, "tn", F32, "dw_in_lat")
    dW_H = _mm(u1, dhp, "tn", F32, "dw_in_hgrn")
    dW_G = _mm(u1, d_gates, "tn", F32, "dw_in_gates")
    dW_kr = _mm(u1, dkr, "tn", F32, "dw_in_krope")
    dW_in = [dW_lat, dW_kr[:, :ROPE], dW_H, dW_G]
    dW_q = dW_qp.reshape(QL, heads, QPAD)[:, :, :QK_HEAD].reshape(QL, heads * QK_HEAD)
    early_parts = chip_partials([dW_in, dW_q, dW_kv], col_sharded[:3], names[:3], "early")
    du1 = _mm(dlat, W_lat, "nt", F32, "d_u1_lat")
    du1, early_recv = _mm(dhp, W_H, "nt", F32, "d_u1_hgrn", res=du1, rider=_scatter_rider(early_parts))
    du1 = _mm(d_gates, W_G, "nt", F32, "d_u1_gates", res=du1)
    du1 = _mm(dkr, W_kr, "nt", F32, "d_u1_krope", res=du1)
    (dx_tokens, dh0_prefix), dg_mix = _rmsnorm_bwd(du1, h0, g_mix_norm, "norm_mix_bwd", F32, res=dh1,
                                                   split_prefix=True)
    grad_x = dx_tokens[None]

    received = list(early_recv) + list(late_recv)
    halves = [_sum4(r, "sum_" + nm) for r, nm in zip(received, names)]
    others = _to_sibling(halves, "swap_halves")
    big_m = [m_w_in, m_w_q_up, m_w_kv_up, m_w_branch_mla, m_w_branch_hgrn, m_w_out, m_w_ffn_in, m_w_ffn_out]
    big_v = [v_w_in, v_w_q_up, v_w_kv_up, v_w_branch_mla, v_w_branch_hgrn, v_w_out, v_w_ffn_in, v_w_ffn_out]
    big_out = {}
    for nm, mine, other, w, m, v in zip(names, halves, others, big, big_m, big_v):
        g, d, mn, vn = _adamw_big(mine, other, w[0], m[0], v[0], "adamw_" + nm)
        big_out[nm] = (g[None], d[None], mn[None], vn[None])

    pieces = [loss_p[:, :1], dg_mix, dg_q, dg_kv, jnp.sum(dgn_p[:, 0, :], axis=0, keepdims=True), dg_ffn, dg_final,
              dlb_p[:, 0, :].reshape(1, D), dcb, dcw[0:3].reshape(1, 3 * F), dh0_prefix[PAD_LEN:PREFIX].reshape(1, N_META * D)]
    sizes = [p.shape[1] for p in pieces]
    flat = jnp.concatenate(pieces, axis=1)[0]
    rows = -(-flat.shape[0] // 1024) * 8
    packed = jnp.pad(flat, (0, rows * 128 - flat.shape[0])).reshape(rows, 128)
    total = _allreduce_small(packed).reshape(-1)
    offs = [0]
    for s in sizes:
        offs.append(offs[-1] + s)
    loss, g_mix, g_q, g_kv, g_hg, g_ffn, g_fin, dlb, g_cb, g_cw, g_meta = [
        total[offs[t]:offs[t + 1]].reshape(1, sizes[t]) for t in range(len(sizes))]
    g_cw = lax.dynamic_slice_in_dim(g_cw.reshape(3, F), chip * (F // 4), F // 4, axis=1)
    g_meta = lax.dynamic_slice_in_dim(g_meta.reshape(N_META, D), chip * (D // 4), D // 4, axis=1)
    items = [(g_meta, meta_tokens, m_meta_tokens, v_meta_tokens),
             (g_cw, conv_w[0], m_conv_w[0], v_conv_w[0]),
             (g_cb, conv_b, m_conv_b, v_conv_b),
             (g_mix, g_mix_norm, m_g_mix_norm, v_g_mix_norm),
             (g_q, g_q_norm, m_g_q_norm, v_g_q_norm),
             (g_kv, g_kv_norm, m_g_kv_norm, v_g_kv_norm),
             (g_hg, g_hgrn_norm, m_g_hgrn_norm, v_g_hgrn_norm),
             (g_ffn, g_ffn_norm, m_g_ffn_norm, v_g_ffn_norm),
             (g_fin, g_final_norm.reshape(1, D), m_g_final_norm.reshape(1, D), v_g_final_norm.reshape(1, D))]
    g_lb, s_delta, s_m, s_v = _adamw_small(items, (lb_raw, m_lb_raw, v_lb_raw), dlb)
    s_grads = [it[0] for it in items] + [g_lb]

    def shape_small(vals):
        meta, cw, cb, mix, q, kvg, hg, ffn, fin, lb = vals
        return [meta, cw[None], cb, mix, q, kvg, hg, ffn, fin.reshape(D), lb]

    s_grads, s_delta, s_m, s_v = [shape_small(v) for v in (s_grads, s_delta, s_m, s_v)]

    def ordered(kind, small):
        bigs = [big_out[nm][kind] for nm in names]
        return [small[0]] + bigs + small[1:]

    return (loss.reshape(()), grad_x, *ordered(0, s_grads), *ordered(1, s_delta), *ordered(2, s_m), *ordered(3, s_v))
```

```python
import math

import jax
import jax.numpy as jnp
from jax import lax
from jax.experimental import pallas as pl
from jax.experimental.pallas import tpu as pltpu

F32 = jnp.float32
BF16 = jnp.bfloat16
MESH = pl.DeviceIdType.MESH

NORM_EPS = 1e-6
N_META = 16
PREFIX = 128
PAD_LEN = PREFIX - N_META
HEAD = 128
ROPE = 64
QK_HEAD = HEAD + ROPE
QPAD = 2 * HEAD
SOFTMAX_SCALE = QK_HEAD ** -0.5
ROPE_THETA = 10000.0
CHUNK = 128
HALO = 16
LEVELS = 7
HGRN_FWD_HEADS = 8
HGRN_BWD_HEADS = 4
ATTN_FWD_HEADS = 4
ATTN_BWD_HEADS = 2
NEG = -1e30

ADAM_LR = 0.001
ADAM_B1 = 0.9
ADAM_B2 = 0.999
ADAM_EPS = 1e-08
ADAM_WD = 0.01
ADAM_STEP = 10

VMEM_LIMIT_BYTES = 48 * 1024 * 1024


def _params(*sem):
    return pltpu.CompilerParams(dimension_semantics=sem, vmem_limit_bytes=VMEM_LIMIT_BYTES)


def _tile(n, prefs):
    for p in prefs:
        if n % p == 0:
            return p
    return n


ROW_TILES = (640, 512, 256, 128, 64, 32, 16, 8)
TN_ROW_TILES = (1024, 1408, 1536, 512, 256, 128)
TN_K_TILES = (1664, 640, 512, 256, 128)
TALL_ROW_TILE = 1664
SHARD_COL_TILES = (1408, 1024, 512, 256, 128)
COL_TILES = (1024, 512, 256, 128)
K_TILES = (2048, 1536, 1408, 1024, 512, 256, 128)


def _sigmoid(x):
    return 1.0 / (1.0 + jnp.exp(-x))


def _dot(a, b, dims):
    return lax.dot_general(a.astype(BF16), b.astype(BF16), (dims, ((), ())), preferred_element_type=F32)


NN = ((1,), (0,))
TN = ((0,), (0,))
NT = ((1,), (1,))


def _split3(x):
    hi = x.astype(BF16)
    r = x - hi.astype(F32)
    mid = r.astype(BF16)
    lo = (r - mid.astype(F32)).astype(BF16)
    return hi, mid, lo


def _dot_exact_rhs(sel, x, dims):
    hi, mid, lo = _split3(x)
    return _dot(sel, hi, dims) + _dot(sel, mid, dims) + _dot(sel, lo, dims)


def _mm(a, b, mode, out_dtype, name, res=None, rider=None, col_shards=1):
    if mode == "nn":
        (M, K), (K2, N) = a.shape, b.shape
    elif mode == "tn":
        (K, M), (K2, N) = a.shape, b.shape
    else:
        (M, K), (N, K2) = a.shape, b.shape
    assert K == K2, (name, a.shape, b.shape)
    tm = _tile(M, TN_ROW_TILES if mode == "tn" else ROW_TILES)
    tn = _tile(N // col_shards, COL_TILES if col_shards == 1 else SHARD_COL_TILES)
    tk = _tile(K, TN_K_TILES if mode == "tn" else K_TILES)
    nk = K // tk
    if mode != "tn" and nk > 1 and res is None and a.dtype == BF16 and M % TALL_ROW_TILE == 0:
        tm = TALL_ROW_TILE
    dims = {"nn": NN, "tn": TN, "nt": NT}[mode]

    n_rin = 0 if rider is None else len(rider.operands)
    n_rout = 0 if rider is None else len(rider.out_shapes)
    n_rsem = 0 if rider is None else len(rider.sem_shapes)
    grid = (M // tm, N // tn, nk)

    def body(*refs):
        main_in, r_ins, (o_ref,), r_outs, accs, r_sems = _split_refs(
            refs, (2 if res is None else 3, n_rin, 1, n_rout, 0 if nk == 1 else 1, n_rsem))
        a_ref, b_ref = main_in[:2]
        r_ref = None if res is None else main_in[2]
        i, j, k = pl.program_id(0), pl.program_id(1), pl.program_id(2)
        if rider is not None:
            pl.when((i == 0) & (j == 0) & (k == 0))(lambda: rider.start(r_ins, r_outs, r_sems))

        def finish(r):
            if r_ref is not None:
                r = r + r_ref[...].astype(F32)
            o_ref[...] = r.astype(out_dtype)

        if nk == 1:
            finish(_dot(a_ref[...], b_ref[...], dims))
        else:
            acc = accs[0]

            @pl.when(k == 0)
            def _():
                acc[...] = jnp.zeros_like(acc)

            acc[...] += _dot(a_ref[...], b_ref[...], dims)

            @pl.when(k == nk - 1)
            def _():
                finish(acc[...])

        if rider is not None:
            last = (i == grid[0] - 1) & (j == grid[1] - 1) & (k == nk - 1)
            pl.when(last)(lambda: rider.finish(r_ins, r_outs, r_sems))

    if mode == "nn":
        a_spec = pl.BlockSpec((tm, tk), lambda i, j, k: (i, k))
        b_spec = pl.BlockSpec((tk, tn), lambda i, j, k: (k, j))
    elif mode == "tn":
        a_spec = pl.BlockSpec((tk, tm), lambda i, j, k: (k, i))
        b_spec = pl.BlockSpec((tk, tn), lambda i, j, k: (k, j))
    else:
        a_spec = pl.BlockSpec((tm, tk), lambda i, j, k: (i, k))
        b_spec = pl.BlockSpec((tn, tk), lambda i, j, k: (j, k))
    in_specs = [a_spec, b_spec]
    operands = [a, b]
    if res is not None:
        in_specs.append(pl.BlockSpec((tm, tn), lambda i, j, k: (i, j)))
        operands.append(res)
    if col_shards == 1:
        out_shape = jax.ShapeDtypeStruct((M, N), out_dtype)
        out_spec = pl.BlockSpec((tm, tn), lambda i, j, k: (i, j))
    else:
        per_shard = (N // col_shards) // tn
        out_shape = jax.ShapeDtypeStruct((col_shards, M, N // col_shards), out_dtype)
        out_spec = pl.BlockSpec((None, tm, tn), lambda i, j, k: (j // per_shard, i, j % per_shard))
    scratch = [] if nk == 1 else [pltpu.VMEM((tm, tn), F32)]
    if rider is None:
        return pl.pallas_call(
            body, name=name, out_shape=out_shape, grid=grid, in_specs=in_specs, out_specs=out_spec,
            scratch_shapes=scratch, compiler_params=_params("parallel", "parallel", "arbitrary"),
        )(*operands)
    res_all = pl.pallas_call(
        body, name=name, out_shape=(out_shape, *rider.out_shapes), grid=grid,
        in_specs=in_specs + [ANY] * n_rin, out_specs=(out_spec, *([ANY] * n_rout)),
        scratch_shapes=scratch + rider.sem_shapes,
        compiler_params=pltpu.CompilerParams(dimension_semantics=("arbitrary", "arbitrary", "arbitrary"),
                                             vmem_limit_bytes=VMEM_LIMIT_BYTES, has_side_effects=True),
    )(*operands, *rider.operands)
    return res_all[0], res_all[1:]


def _rmsnorm_fwd(x, g, name, col0=0, width=None):
    L = x.shape[0]
    width = x.shape[1] if width is None else width
    assert col0 % width == 0
    cb = col0 // width
    tm = _tile(L, (128, 64, 32, 16))

    def body(x_ref, g_ref, o_ref):
        xv = x_ref[...].astype(F32)
        r = lax.rsqrt(jnp.mean(xv * xv, axis=-1, keepdims=True) + NORM_EPS)
        o_ref[...] = ((xv * r) * g_ref[...]).astype(BF16)

    return pl.pallas_call(
        body, name=name,
        out_shape=jax.ShapeDtypeStruct((L, width), BF16),
        grid=(L // tm,),
        in_specs=[pl.BlockSpec((tm, width), lambda i: (i, cb)), pl.BlockSpec((1, width), lambda i: (0, 0))],
        out_specs=pl.BlockSpec((tm, width), lambda i: (i, 0)),
        compiler_params=_params("parallel"),
    )(x, g)


def _embed_norm(x, meta, g):
    S, D = x.shape
    L = S + PREFIX
    tm = PREFIX

    def body(x_ref, meta_ref, g_ref, h_ref, u_ref):
        i = pl.program_id(0)

        @pl.when(i == 0)
        def _():
            h_ref[...] = jnp.zeros_like(h_ref)
            h_ref[PAD_LEN:PREFIX, :] = meta_ref[...]

        @pl.when(i > 0)
        def _():
            h_ref[...] = x_ref[...]

        xv = h_ref[...]
        r = lax.rsqrt(jnp.mean(xv * xv, axis=-1, keepdims=True) + NORM_EPS)
        u_ref[...] = ((xv * r) * g_ref[...]).astype(BF16)

    return pl.pallas_call(
        body, name="embed_norm_mix",
        out_shape=(jax.ShapeDtypeStruct((L, D), F32), jax.ShapeDtypeStruct((L, D), BF16)),
        grid=(L // tm,),
        in_specs=[pl.BlockSpec((tm, D), lambda i: (jnp.maximum(i - 1, 0), 0)),
                  pl.BlockSpec((N_META, D), lambda i: (0, 0)), pl.BlockSpec((1, D), lambda i: (0, 0))],
        out_specs=(pl.BlockSpec((tm, D), lambda i: (i, 0)), pl.BlockSpec((tm, D), lambda i: (i, 0))),
        compiler_params=_params("parallel"),
    )(x, meta, g)


def _rmsnorm_bwd(dy, x, g, name, out_dtype, col0=0, res=None, split_prefix=False):
    L, width = dy.shape
    assert col0 % width == 0
    cb = col0 // width
    tm = PREFIX if split_prefix else _tile(L, (128, 64, 32, 16))

    def body(*refs):
        head_ref = None
        if split_prefix:
            refs, head_ref = refs[:-1], refs[-1]
        if res is None:
            dy_ref, x_ref, g_ref, dx_ref, dg_ref = refs
            r_ref = None
        else:
            dy_ref, x_ref, g_ref, r_ref, dx_ref, dg_ref = refs

        @pl.when(pl.program_id(0) == 0)
        def _():
            dg_ref[...] = jnp.zeros_like(dg_ref)

        xv = x_ref[...].astype(F32)
        dyv = dy_ref[...].astype(F32)
        r = lax.rsqrt(jnp.mean(xv * xv, axis=-1, keepdims=True) + NORM_EPS)
        z = dyv * g_ref[...]
        dx = r * z - xv * ((r * r * r) * jnp.mean(z * xv, axis=-1, keepdims=True))
        if r_ref is not None:
            dx = dx + r_ref[...]
        dg_ref[...] += jnp.sum(dyv * (xv * r), axis=0, keepdims=True)
        if head_ref is None:
            dx_ref[...] = dx.astype(out_dtype)
        else:
            @pl.when(pl.program_id(0) == 0)
            def _():
                head_ref[...] = dx.astype(out_dtype)

            @pl.when(pl.program_id(0) > 0)
            def _():
                dx_ref[...] = dx.astype(out_dtype)

    in_specs = [pl.BlockSpec((tm, width), lambda i: (i, 0)),
                pl.BlockSpec((tm, width), lambda i: (i, cb)),
                pl.BlockSpec((1, width), lambda i: (0, 0))]
    operands = [dy, x, g]
    if res is not None:
        in_specs.append(pl.BlockSpec((tm, width), lambda i: (i, 0)))
        operands.append(res)
    dg_shape, dg_spec = jax.ShapeDtypeStruct((1, width), F32), pl.BlockSpec((1, width), lambda i: (0, 0))
    if not split_prefix:
        return pl.pallas_call(
            body, name=name, out_shape=(jax.ShapeDtypeStruct((L, width), out_dtype), dg_shape), grid=(L // tm,),
            in_specs=in_specs, out_specs=(pl.BlockSpec((tm, width), lambda i: (i, 0)), dg_spec),
            compiler_params=_params("arbitrary"),
        )(*operands)
    dx_rest, dg, dx_head = pl.pallas_call(
        body, name=name,
        out_shape=(jax.ShapeDtypeStruct((L - PREFIX, width), out_dtype), dg_shape,
                   jax.ShapeDtypeStruct((PREFIX, width), out_dtype)),
        grid=(L // tm,), in_specs=in_specs,
        out_specs=(pl.BlockSpec((tm, width), lambda i: (jnp.maximum(i - 1, 0), 0)), dg_spec,
                   pl.BlockSpec((PREFIX, width), lambda i: (0, 0))),
        compiler_params=_params("arbitrary"),
    )(*operands)
    return (dx_rest, dx_head), dg


def _final_loss_bwd(h2, tgt, g):
    L, D = h2.shape
    tm = PREFIX
    inv_d = 1.0 / D

    def body(h_ref, t_ref, g_ref, dh_ref, loss_ref, dg_ref, dhb_ref):
        i = pl.program_id(0)

        @pl.when(i == 0)
        def _():
            loss_ref[...] = jnp.zeros_like(loss_ref)
            dg_ref[...] = jnp.zeros_like(dg_ref)

        xv = h_ref[...]
        r = lax.rsqrt(jnp.mean(xv * xv, axis=-1, keepdims=True) + NORM_EPS)
        xn = xv * r
        y = xn * g_ref[...]
        real = (i >= PREFIX // tm).astype(F32)
        diff = (y - t_ref[...]) * real
        loss_ref[...] += 0.5 * inv_d * jnp.sum(diff * diff)
        dyv = diff * inv_d
        z = dyv * g_ref[...]
        dh = r * z - xv * ((r * r * r) * jnp.mean(z * xv, axis=-1, keepdims=True))
        dh_ref[...] = dh
        dhb_ref[...] = dh.astype(BF16)
        dg_ref[...] += jnp.sum(dyv * xn, axis=0, keepdims=True)

    shift = PREFIX // tm
    return pl.pallas_call(
        body, name="final_loss_bwd",
        out_shape=(jax.ShapeDtypeStruct((L, D), F32), jax.ShapeDtypeStruct((1, 128), F32),
                   jax.ShapeDtypeStruct((1, D), F32), jax.ShapeDtypeStruct((L, D), BF16)),
        grid=(L // tm,),
        in_specs=[pl.BlockSpec((tm, D), lambda i: (i, 0)),
                  pl.BlockSpec((tm, D), lambda i: (jnp.maximum(i - shift, 0), 0)),
                  pl.BlockSpec((1, D), lambda i: (0, 0))],
        out_specs=(pl.BlockSpec((tm, D), lambda i: (i, 0)), pl.BlockSpec((1, 128), lambda i: (0, 0)),
                   pl.BlockSpec((1, D), lambda i: (0, 0)), pl.BlockSpec((tm, D), lambda i: (i, 0))),
        compiler_params=_params("arbitrary"),
    )(h2, tgt, g)


def _rot_half(x):
    lane = lax.broadcasted_iota(jnp.int32, x.shape, 1)
    return jnp.where(lane < ROPE // 2, -pltpu.roll(x, 128 - ROPE // 2, 1), pltpu.roll(x, ROPE // 2, 1))


def _rope_fwd(qp, kv, kr, cos, sin, heads):
    L = qp.shape[0]
    tm = _tile(L, (128,))

    def body(q_ref, kv_ref, kr_ref, c_ref, s_ref, qc_ref, kc_ref, v_ref):
        c, s = c_ref[...], s_ref[...]
        krv = kr_ref[...]
        kr_rot = (krv * c + _rot_half(krv) * s).astype(BF16)
        for h in range(heads):
            lo = h * QPAD
            qc_ref[:, lo:lo + HEAD] = (q_ref[:, lo:lo + HEAD].astype(F32) * SOFTMAX_SCALE).astype(BF16)
            qr = q_ref[:, lo + HEAD:lo + QPAD].astype(F32)
            qc_ref[:, lo + HEAD:lo + QPAD] = ((qr * c + _rot_half(qr) * s) * SOFTMAX_SCALE).astype(BF16)
            kc_ref[:, lo:lo + HEAD] = kv_ref[:, lo:lo + HEAD].astype(BF16)
            kc_ref[:, lo + HEAD:lo + QPAD] = kr_rot
            v_ref[:, h * HEAD:(h + 1) * HEAD] = kv_ref[:, lo + HEAD:lo + QPAD].astype(BF16)

    W = heads * QPAD
    row = lambda w: pl.BlockSpec((tm, w), lambda i: (i, 0))
    return pl.pallas_call(
        body, name="rope_fwd",
        out_shape=(jax.ShapeDtypeStruct((L, W), BF16), jax.ShapeDtypeStruct((L, W), BF16),
                   jax.ShapeDtypeStruct((L, heads * HEAD), BF16)),
        grid=(L // tm,),
        in_specs=[row(W), row(W), row(128), row(128), row(128)],
        out_specs=(row(W), row(W), row(heads * HEAD)),
        compiler_params=_params("parallel"),
    )(qp, kv, kr, cos, sin)


def _rope_bwd(dqc, dkc, dv, cos, sin, heads):
    L = dqc.shape[0]
    tm = _tile(L, (128,))

    def body(dq_ref, dk_ref, dv_ref, c_ref, s_ref, dqp_ref, dkv_ref, dkr_ref):
        c, s = c_ref[...], s_ref[...]
        acc = jnp.zeros((tm, 128), F32)
        for h in range(heads):
            lo = h * QPAD
            dqp_ref[:, lo:lo + HEAD] = (dq_ref[:, lo:lo + HEAD] * SOFTMAX_SCALE).astype(BF16)
            d = dq_ref[:, lo + HEAD:lo + QPAD]
            dqp_ref[:, lo + HEAD:lo + QPAD] = ((d * c - _rot_half(d) * s) * SOFTMAX_SCALE).astype(BF16)
            dkv_ref[:, lo:lo + HEAD] = dk_ref[:, lo:lo + HEAD].astype(BF16)
            dkv_ref[:, lo + HEAD:lo + QPAD] = dv_ref[:, h * HEAD:(h + 1) * HEAD].astype(BF16)
            acc = acc + dk_ref[:, lo + HEAD:lo + QPAD]
        dkr_ref[...] = (acc * c - _rot_half(acc) * s).astype(BF16)

    W = heads * QPAD
    row = lambda w: pl.BlockSpec((tm, w), lambda i: (i, 0))
    return pl.pallas_call(
        body, name="rope_bwd",
        out_shape=(jax.ShapeDtypeStruct((L, W), BF16), jax.ShapeDtypeStruct((L, W), BF16),
                   jax.ShapeDtypeStruct((L, 128), BF16)),
        grid=(L // tm,),
        in_specs=[row(W), row(W), row(heads * HEAD), row(128), row(128)],
        out_specs=(row(W), row(W), row(128)),
        compiler_params=_params("parallel"),
    )(dqc, dkc, dv, cos, sin)


def _attn_keep(qi, ki, ta):
    t = qi * ta + lax.broadcasted_iota(jnp.int32, (ta, ta), 0)
    s = ki * ta + lax.broadcasted_iota(jnp.int32, (ta, ta), 1)
    return (s <= t) & ((s >= PAD_LEN) | (s == t))


def _split_refs(refs, sizes):
    out, at = [], 0
    for n in sizes:
        out.append(refs[at:at + n])
        at += n
    return out


def _attn_fwd(qc, kc, v, heads, rider):
    L = qc.shape[0]
    ta = _tile(L, (640, 128))
    nb = L // ta
    pairs = [(i, j) for i in range(nb) for j in range(i + 1)]
    q_of = jnp.asarray([p[0] for p in pairs], jnp.int32)
    k_of = jnp.asarray([p[1] for p in pairs], jnp.int32)
    n_rin, n_rout = len(rider.operands), len(rider.out_shapes)
    per = _tile(heads, (ATTN_FWD_HEADS, 1))
    ng = heads // per

    def body(*refs):
        (q_of_ref, k_of_ref, q_ref, k_ref, v_ref), r_ins, (o_ref, lse_ref), r_outs, (m_sc, l_sc, acc_sc), r_sems = \
            _split_refs(refs, (5, n_rin, 2, n_rout, 3, len(rider.sem_shapes)))
        h, t = pl.program_id(0), pl.program_id(1)
        qi, ki = q_of_ref[t], k_of_ref[t]
        pl.when((h == 0) & (t == 0))(lambda: rider.start(r_ins, r_outs, r_sems))

        @pl.when(ki == 0)
        def _():
            m_sc[...] = jnp.full_like(m_sc, NEG)
            l_sc[...] = jnp.zeros_like(l_sc)
            acc_sc[...] = jnp.zeros_like(acc_sc)

        def step(masked):
            wide = lambda hh: slice(hh * QPAD, (hh + 1) * QPAD)
            lanes = lambda hh: slice(hh * HEAD, (hh + 1) * HEAD)
            ss = [_dot(q_ref[:, wide(hh)], k_ref[:, wide(hh)], NT) for hh in range(per)]
            if masked:
                keep = _attn_keep(qi, ki, ta)
                ss = [jnp.where(keep, s, NEG) for s in ss]
            for hh in range(per):
                m_old = m_sc[hh]
                m_new = jnp.maximum(m_old, jnp.max(ss[hh], axis=-1, keepdims=True))
                p = jnp.exp(ss[hh] - jnp.tile(m_new, (1, ta // HEAD)))
                alpha = jnp.exp(m_old - m_new)
                l_sc[hh] = alpha * l_sc[hh] + jnp.sum(p, axis=-1, keepdims=True)
                acc_sc[hh] = alpha * acc_sc[hh] + _dot(p, v_ref[:, lanes(hh)], NN)
                m_sc[hh] = m_new

        pl.when((ki == qi) | ((ki == 0) & (qi > 0)))(lambda: step(True))
        pl.when((ki > 0) & (ki < qi))(lambda: step(False))

        @pl.when(ki == qi)
        def _():
            for hh in range(per):
                l = l_sc[hh]
                o_ref[:, hh * HEAD:(hh + 1) * HEAD] = (acc_sc[hh] / l).astype(BF16)
                lse_ref[:, hh * HEAD:(hh + 1) * HEAD] = m_sc[hh] + jnp.log(l)

        pl.when((h == ng - 1) & (t == len(pairs) - 1))(lambda: rider.finish(r_ins, r_outs, r_sems))

    qrow = lambda w: pl.BlockSpec((ta, per * w), lambda h, t, q_of, k_of: (q_of[t], h))
    krow = lambda w: pl.BlockSpec((ta, per * w), lambda h, t, q_of, k_of: (k_of[t], h))
    stat = pltpu.VMEM((per, ta, HEAD), F32)
    res = pl.pallas_call(
        body, name="attn_fwd",
        out_shape=(jax.ShapeDtypeStruct((L, heads * HEAD), BF16), jax.ShapeDtypeStruct((L, heads * HEAD), F32),
                   *rider.out_shapes),
        grid_spec=pltpu.PrefetchScalarGridSpec(
            num_scalar_prefetch=2, grid=(ng, len(pairs)),
            in_specs=[qrow(QPAD), krow(QPAD), krow(HEAD)] + [ANY] * n_rin,
            out_specs=(qrow(HEAD), qrow(HEAD), *([ANY] * n_rout)),
            scratch_shapes=[stat, stat, stat] + rider.sem_shapes),
        compiler_params=pltpu.CompilerParams(dimension_semantics=("arbitrary", "arbitrary"),
                                             vmem_limit_bytes=VMEM_LIMIT_BYTES, has_side_effects=True),
    )(q_of, k_of, qc, kc, v, *rider.operands)
    return res[0], res[1], res[2:]


def _attn_delta(o, do, heads):
    L = o.shape[0]
    tm = _tile(L, (128,))

    def body(o_ref, do_ref, d_ref):
        for h in range(heads):
            cols = slice(h * HEAD, (h + 1) * HEAD)
            d = jnp.sum(do_ref[:, cols].astype(F32) * o_ref[:, cols].astype(F32), axis=-1, keepdims=True)
            d_ref[:, cols] = jnp.broadcast_to(d, (tm, HEAD))

    row = pl.BlockSpec((tm, heads * HEAD), lambda i: (i, 0))
    return pl.pallas_call(
        body, name="attn_delta", out_shape=jax.ShapeDtypeStruct((L, heads * HEAD), F32), grid=(L // tm,),
        in_specs=[row, row], out_specs=row, compiler_params=_params("parallel"),
    )(o, do)


def _attn_bwd(qc, kc, v, do, lse, delta, heads, rider):
    L = qc.shape[0]
    ta = _tile(L, (640, 128))
    nb = L // ta
    pairs = [(j, i) for j in range(nb) for i in range(j, nb)]
    k_of = jnp.asarray([p[0] for p in pairs], jnp.int32)
    q_of = jnp.asarray([p[1] for p in pairs], jnp.int32)
    n_rin, n_rout = len(rider.operands), len(rider.out_shapes)
    per = _tile(heads, (ATTN_BWD_HEADS, 1))
    ng = heads // per

    def body(*refs):
        ((k_of_ref, q_of_ref, q_ref, k_ref, v_ref, do_ref, lse_ref, delta_ref), r_ins, (dq_hbm, dk_ref, dv_ref),
         r_outs, (dq_sc, dq_sem), r_sems) = _split_refs(refs, (8, n_rin, 3, n_rout, 2, len(rider.sem_shapes)))
        h, t = pl.program_id(0), pl.program_id(1)
        kj, qi = k_of_ref[t], q_of_ref[t]
        pl.when((h == 0) & (t == 0))(lambda: rider.start(r_ins, r_outs, r_sems))

        @pl.when(t == 0)
        def _():
            dq_sc[...] = jnp.zeros_like(dq_sc)

        @pl.when(qi == kj)
        def _():
            dk_ref[...] = jnp.zeros_like(dk_ref)
            dv_ref[...] = jnp.zeros_like(dv_ref)

        def step(masked):
            wide = lambda hh: slice(hh * QPAD, (hh + 1) * QPAD)
            lanes = lambda hh: slice(hh * HEAD, (hh + 1) * HEAD)
            rep = (1, ta // HEAD)
            rows = pl.ds(pl.multiple_of(qi * ta, ta), ta)
            ss = [_dot(q_ref[:, wide(hh)], k_ref[:, wide(hh)], NT) for hh in range(per)]
            dps = [_dot(do_ref[:, lanes(hh)], v_ref[:, lanes(hh)], NT) for hh in range(per)]
            ps = [jnp.exp(ss[hh] - jnp.tile(lse_ref[:, lanes(hh)], rep)) for hh in range(per)]
            if masked:
                keep = _attn_keep(qi, kj, ta)
                ps = [jnp.where(keep, p, 0.0) for p in ps]
            dss = [ps[hh] * (dps[hh] - jnp.tile(delta_ref[:, lanes(hh)], rep)) for hh in range(per)]
            for hh in range(per):
                dv_ref[:, lanes(hh)] += _dot(ps[hh], do_ref[:, lanes(hh)], TN)
                dk_ref[:, wide(hh)] += _dot(dss[hh], q_ref[:, wide(hh)], TN)
                dq_sc[rows, wide(hh)] += _dot(dss[hh], k_ref[:, wide(hh)], NN)

        pl.when((qi == kj) | ((kj == 0) & (qi > 0)))(lambda: step(True))
        pl.when((kj > 0) & (qi > kj))(lambda: step(False))

        @pl.when(t == len(pairs) - 1)
        def _():
            cols = pl.ds(pl.multiple_of(h * (per * QPAD), per * QPAD), per * QPAD)
            out = pltpu.make_async_copy(dq_sc, dq_hbm.at[:, cols], dq_sem)
            out.start()
            out.wait()

        pl.when((h == ng - 1) & (t == len(pairs) - 1))(lambda: rider.finish(r_ins, r_outs, r_sems))

    qrow = lambda w: pl.BlockSpec((ta, per * w), lambda h, t, k_of, q_of: (q_of[t], h))
    krow = lambda w: pl.BlockSpec((ta, per * w), lambda h, t, k_of, q_of: (k_of[t], h))
    res = pl.pallas_call(
        body, name="attn_bwd",
        out_shape=(jax.ShapeDtypeStruct((L, heads * QPAD), F32), jax.ShapeDtypeStruct((L, heads * QPAD), F32),
                   jax.ShapeDtypeStruct((L, heads * HEAD), F32), *rider.out_shapes),
        grid_spec=pltpu.PrefetchScalarGridSpec(
            num_scalar_prefetch=2, grid=(ng, len(pairs)),
            in_specs=[qrow(QPAD), krow(QPAD), krow(HEAD), qrow(HEAD), qrow(HEAD), qrow(HEAD)] + [ANY] * n_rin,
            out_specs=(ANY, krow(QPAD), krow(HEAD), *([ANY] * n_rout)),
            scratch_shapes=[pltpu.VMEM((L, per * QPAD), F32), pltpu.SemaphoreType.DMA] + rider.sem_shapes),
        compiler_params=pltpu.CompilerParams(dimension_semantics=("arbitrary", "arbitrary"),
                                             vmem_limit_bytes=VMEM_LIMIT_BYTES, has_side_effects=True),
    )(k_of, q_of, qc, kc, v, do, lse, delta, *rider.operands)
    return res[0], res[1], res[2], res[3:]


def _hgrn_constants():
    t = jnp.arange(CHUNK)
    tril = (t[None, :] <= t[:, None]).astype(BF16)
    sel, blk = [], []
    for lv in range(LEVELS):
        hs = 1 << lv
        mid = (t // (2 * hs)) * (2 * hs) + hs - 1
        sel.append((t[None, :] == mid[:, None]).astype(BF16))
        blk.append(((t[:, None] // (2 * hs)) == (t[None, :] // (2 * hs))).astype(F32))
    return tril, jnp.concatenate(sel, axis=0), jnp.concatenate(blk, axis=0)


def _hgrn_decay_grad_constants():
    r = jnp.arange(CHUNK)[:, None]
    c = jnp.arange(CHUNK)[None, :]
    mats = []
    for lv in range(LEVELS):
        same = (r >> (lv + 1)) == (c >> (lv + 1))
        second = ((r >> lv) & 1) == 1
        mats.append(same & jnp.where(second, c >= r, c < r))
    mats += [c >= r, c < r]
    return jnp.concatenate(mats, axis=1).astype(BF16)


def _hgrn_gates(hq_ref, hf_ref, hi_ref, lb_ref, cols, row0):
    rows = row0 + lax.broadcasted_iota(jnp.int32, (CHUNK, 1), 0)
    valid = rows >= PAD_LEN
    lb = 1.0 / (1.0 + jnp.exp(lb_ref[1:2, cols] - lb_ref[0:1, cols]))
    hq = hq_ref[:, cols].astype(F32)
    sq = _sigmoid(hq)
    sg = _sigmoid(hf_ref[:, cols].astype(F32))
    f = lb + (1.0 - lb) * sg
    g = jnp.where(valid, jnp.log(f), 0.0)
    k = jnp.where(valid, 1.0 - f, 0.0)
    return dict(q=hq * sq, sq=sq, hq=hq, k=k, v=hi_ref[:, cols].astype(F32), g=g, f=f, sg=sg, lb=lb, valid=valid)


def _hgrn_prefix(tril_ref, sel_ref, gs):
    n = len(gs)
    b_all = _dot_exact_rhs(tril_ref[...], jnp.concatenate(gs, axis=1) if n > 1 else gs[0], NN)
    bm_all = _dot_exact_rhs(sel_ref[...], b_all, NN)
    cut = lambda a, i: a[:, i * HEAD:(i + 1) * HEAD]
    return [cut(b_all, i) for i in range(n)], [cut(bm_all, i) for i in range(n)]


def _hgrn_levels(q, k, b, bm_all, blk_ref):
    t = lax.broadcasted_iota(jnp.int32, (CHUNK, 1), 0)
    out = []
    for lv in range(LEVELS):
        bm = bm_all[lv * CHUNK:(lv + 1) * CHUNK, :]
        second = ((t >> lv) & 1) == 1
        eq = jnp.where(second, jnp.exp(jnp.minimum(b - bm, 0.0)), 0.0)
        ek = jnp.where(second, 0.0, jnp.exp(jnp.minimum(bm - b, 0.0)))
        same = blk_ref[lv * CHUNK:(lv + 1) * CHUNK, :]
        out.append((eq, ek, (q * eq).astype(BF16), (k * ek).astype(BF16), same))
    return out


def _hgrn_intra(qs, ks, levels):
    tt = lax.broadcasted_iota(jnp.int32, (CHUNK, CHUNK), 0)
    ss = lax.broadcasted_iota(jnp.int32, (CHUNK, CHUNK), 1)
    ps = [jnp.where(tt == ss, jnp.sum(q * k, axis=-1, keepdims=True), 0.0) for q, k in zip(qs, ks)]
    for lv in range(LEVELS):
        for i, lvl in enumerate(levels):
            _, _, ql, kl, same = lvl[lv]
            ps[i] = ps[i] + _dot(ql, kl, NT) * same
    return ps


def _hgrn_fwd(hp, lb_raw, g_norm, nh):
    L = hp.shape[0]
    D = nh * HEAD
    nc = L // CHUNK
    per = _tile(nh, (HGRN_FWD_HEADS, 2, 1))
    ng = nh // per
    tril, sel, blk = _hgrn_constants()

    def body(hq_ref, hf_ref, hi_ref, hg_ref, lb_ref, gn_ref, tril_ref, sel_ref, blk_ref,
             oh_ref, orec_ref, shist_ref, s_sc, b_sc):
        c = pl.program_id(1)

        @pl.when(c == 0)
        def _():
            s_sc[...] = jnp.zeros_like(s_sc)

        heads_here = range(per)
        lanes = [slice(hh * HEAD, (hh + 1) * HEAD) for hh in heads_here]
        ws = [_hgrn_gates(hq_ref, hf_ref, hi_ref, lb_ref, lanes[hh], c * CHUNK) for hh in heads_here]
        qs, ks, vs = [w["q"] for w in ws], [w["k"] for w in ws], [w["v"] for w in ws]
        bs, bms = _hgrn_prefix(tril_ref, sel_ref, [w["g"] for w in ws])
        for hh in heads_here:
            b_sc[hh] = bs[hh]
        b_lasts = [b_sc[hh, CHUNK - 1:CHUNK, :] for hh in heads_here]
        ps = _hgrn_intra(qs, ks, [_hgrn_levels(qs[hh], ks[hh], bs[hh], bms[hh], blk_ref) for hh in heads_here])
        s_ins = [s_sc[hh] for hh in heads_here]
        os_ = [_dot(ps[hh], vs[hh], NN) + _dot(qs[hh] * jnp.exp(bs[hh]), s_ins[hh], NT) for hh in heads_here]
        for hh in heads_here:
            shist_ref[0, hh] = s_ins[hh]
            s_sc[hh] = (jnp.exp(b_lasts[hh]) * s_ins[hh]
                        + _dot(vs[hh], ks[hh] * jnp.exp(b_lasts[hh] - bs[hh]), TN))
        for hh in heads_here:
            o = os_[hh]
            orec_ref[:, lanes[hh]] = o
            rn = lax.rsqrt(jnp.mean(o * o, axis=-1, keepdims=True) + NORM_EPS)
            hg = hg_ref[:, lanes[hh]].astype(F32)
            oh_ref[:, lanes[hh]] = (((o * rn) * gn_ref[...]) * (hg * _sigmoid(hg))).astype(BF16)

    col = lambda grp: pl.BlockSpec((CHUNK, per * HEAD), lambda h, c: (c, grp * ng + h))
    const = lambda shape: pl.BlockSpec(shape, lambda h, c: (0, 0))
    return pl.pallas_call(
        body, name="hgrn_fwd",
        out_shape=(jax.ShapeDtypeStruct((L, D), BF16), jax.ShapeDtypeStruct((L, D), F32),
                   jax.ShapeDtypeStruct((nc, nh, HEAD, HEAD), F32)),
        grid=(ng, nc),
        in_specs=[col(0), col(1), col(2), col(3),
                  pl.BlockSpec((2, per * HEAD), lambda h, c: (0, h)), const((1, HEAD)),
                  const((CHUNK, CHUNK)), const((LEVELS * CHUNK, CHUNK)), const((LEVELS * CHUNK, CHUNK))],
        out_specs=(pl.BlockSpec((CHUNK, per * HEAD), lambda h, c: (c, h)),
                   pl.BlockSpec((CHUNK, per * HEAD), lambda h, c: (c, h)),
                   pl.BlockSpec((1, per, HEAD, HEAD), lambda h, c: (c, h, 0, 0))),
        scratch_shapes=[pltpu.VMEM((per, HEAD, HEAD), F32), pltpu.VMEM((per, CHUNK, HEAD), F32)],
        compiler_params=_params("parallel", "arbitrary"),
    )(hp, hp, hp, hp, lb_raw, g_norm, tril, sel, blk)


def _hgrn_bwd(hp, lb_raw, g_norm, do_h, o_rec, s_hist, nh):
    L = hp.shape[0]
    D = nh * HEAD
    nc = L // CHUNK
    per = _tile(nh, (HGRN_BWD_HEADS, 1))
    ng = nh // per
    tril, sel, blk = _hgrn_constants()
    tdec = _hgrn_decay_grad_constants()

    def body(hq_ref, hf_ref, hi_ref, hg_ref, lb_ref, gn_ref, tril_ref, sel_ref, blk_ref, tdec_ref,
             do_ref, orec_ref, shist_ref, dhq_ref, dhf_ref, dhi_ref, dhg_ref, dgn_ref, dlb_ref, ds_sc, b_sc):
        ci = pl.program_id(1)
        c = nc - 1 - ci

        @pl.when(ci == 0)
        def _():
            ds_sc[...] = jnp.zeros_like(ds_sc)
            dgn_ref[...] = jnp.zeros_like(dgn_ref)
            dlb_ref[...] = jnp.zeros_like(dlb_ref)

        heads_here = range(per)
        lanes = [slice(hh * HEAD, (hh + 1) * HEAD) for hh in heads_here]
        ws = [_hgrn_gates(hq_ref, hf_ref, hi_ref, lb_ref, lanes[hh], c * CHUNK) for hh in heads_here]
        qs, ks, vs = [w["q"] for w in ws], [w["k"] for w in ws], [w["v"] for w in ws]
        bs, bms = _hgrn_prefix(tril_ref, sel_ref, [w["g"] for w in ws])
        for hh in heads_here:
            b_sc[hh] = bs[hh]
        b_lasts = [b_sc[hh, CHUNK - 1:CHUNK, :] for hh in heads_here]
        levels = [_hgrn_levels(qs[hh], ks[hh], bs[hh], bms[hh], blk_ref) for hh in heads_here]
        ps = _hgrn_intra(qs, ks, levels)
        s_ins = [shist_ref[0, hh] for hh in heads_here]
        ds_outs = [ds_sc[hh] for hh in heads_here]
        ebs = [jnp.exp(b) for b in bs]
        etails = [jnp.exp(b_lasts[hh] - bs[hh]) for hh in heads_here]
        decays = [jnp.exp(bl) for bl in b_lasts]

        dos = []
        for hh in heads_here:
            o = orec_ref[:, lanes[hh]]
            hg = hg_ref[:, lanes[hh]].astype(F32)
            sgg = _sigmoid(hg)
            rn = lax.rsqrt(jnp.mean(o * o, axis=-1, keepdims=True) + NORM_EPS)
            on = o * rn
            doh = do_ref[:, lanes[hh]]
            dy = doh * (hg * sgg)
            dhg_ref[:, lanes[hh]] = (doh * (on * gn_ref[...]) * (sgg * (1.0 + hg * (1.0 - sgg)))).astype(BF16)
            dgn_ref[hh] += jnp.broadcast_to(jnp.sum(dy * on, axis=0, keepdims=True), (8, HEAD))
            z = dy * gn_ref[...]
            dos.append(rn * z - o * ((rn * rn * rn) * jnp.mean(z * o, axis=-1, keepdims=True)))

        tt = lax.broadcasted_iota(jnp.int32, (CHUNK, CHUNK), 0)
        ss = lax.broadcasted_iota(jnp.int32, (CHUNK, CHUNK), 1)
        dps = [jnp.where(ss <= tt, _dot(dos[hh], vs[hh], NT), 0.0) for hh in heads_here]
        dvs = [_dot(ps[hh], dos[hh], TN) + _dot(ks[hh] * etails[hh], ds_outs[hh], NT) for hh in heads_here]
        dq_states = [ebs[hh] * _dot(dos[hh], s_ins[hh], NN) for hh in heads_here]
        dk_states = [etails[hh] * _dot(vs[hh], ds_outs[hh], NN) for hh in heads_here]
        dpds = [jnp.sum(jnp.where(tt == ss, dp, 0.0), axis=-1, keepdims=True) for dp in dps]
        dqs = [dpds[hh] * ks[hh] + dq_states[hh] for hh in heads_here]
        dks = [dpds[hh] * qs[hh] + dk_states[hh] for hh in heads_here]
        pair_terms = [[] for _ in heads_here]
        for lv in range(LEVELS):
            for hh in heads_here:
                eq, ek, ql, kl, same = levels[hh][lv]
                dpl = dps[hh] * same
                dq_l = eq * _dot(dpl, kl, NN)
                dk_l = ek * _dot(dpl, ql, TN)
                dqs[hh] = dqs[hh] + dq_l
                dks[hh] = dks[hh] + dk_l
                pair_terms[hh].append(qs[hh] * dq_l + ks[hh] * dk_l)
        for hh in heads_here:
            pair_terms[hh] += [qs[hh] * dq_states[hh], ks[hh] * dk_states[hh]]
            ds_sc[hh] = decays[hh] * ds_outs[hh] + _dot(dos[hh], qs[hh] * ebs[hh], TN)
        stacked = [jnp.concatenate(terms, axis=0) for terms in pair_terms]
        dg_all = _dot_exact_rhs(tdec_ref[...], jnp.concatenate(stacked, axis=1) if per > 1 else stacked[0], NN)

        for hh in heads_here:
            w = ws[hh]
            f, sg, lb, sq, hq = w["f"], w["sg"], w["lb"], w["sq"], w["hq"]
            through = jnp.sum((decays[hh] * s_ins[hh]) * ds_outs[hh], axis=0, keepdims=True)
            dg = dg_all[:, lanes[hh]] + through
            df = jnp.where(w["valid"], dg / f - dks[hh], 0.0)
            dhf_ref[:, lanes[hh]] = (df * (1.0 - lb) * sg * (1.0 - sg)).astype(BF16)
            dlb_ref[hh] += jnp.broadcast_to(jnp.sum(df * (1.0 - sg), axis=0, keepdims=True), (8, HEAD))
            dhq_ref[:, lanes[hh]] = (dqs[hh] * (sq * (1.0 + hq * (1.0 - sq)))).astype(BF16)
            dhi_ref[:, lanes[hh]] = dvs[hh].astype(BF16)

    col = lambda grp: pl.BlockSpec((CHUNK, per * HEAD), lambda h, c: (nc - 1 - c, grp * ng + h))
    const = lambda shape: pl.BlockSpec(shape, lambda h, c: (0, 0))
    tile = pl.BlockSpec((CHUNK, per * HEAD), lambda h, c: (nc - 1 - c, h))
    part = pl.BlockSpec((per, 8, HEAD), lambda h, c: (h, 0, 0))
    return pl.pallas_call(
        body, name="hgrn_bwd",
        out_shape=tuple([jax.ShapeDtypeStruct((L, D), BF16)] * 4 + [jax.ShapeDtypeStruct((nh, 8, HEAD), F32)] * 2),
        grid=(ng, nc),
        in_specs=[col(0), col(1), col(2), col(3),
                  pl.BlockSpec((2, per * HEAD), lambda h, c: (0, h)), const((1, HEAD)),
                  const((CHUNK, CHUNK)), const((LEVELS * CHUNK, CHUNK)), const((LEVELS * CHUNK, CHUNK)),
                  const((CHUNK, (LEVELS + 2) * CHUNK)),
                  tile, tile, pl.BlockSpec((1, per, HEAD, HEAD), lambda h, c: (nc - 1 - c, h, 0, 0))],
        out_specs=(tile, tile, tile, tile, part, part),
        scratch_shapes=[pltpu.VMEM((per, HEAD, HEAD), F32), pltpu.VMEM((per, CHUNK, HEAD), F32)],
        compiler_params=_params("parallel", "arbitrary"),
    )(hp, hp, hp, hp, lb_raw, g_norm, tril, sel, blk, tdec, do_h, o_rec, s_hist)


def _merge_fwd(a, bm, gates):
    L, D = a.shape
    tm = _tile(L, (128,))

    def body(a_ref, b_ref, g_ref, o_ref):
        sa, sb = _sigmoid(g_ref[:, :D].astype(F32)), _sigmoid(g_ref[:, D:].astype(F32))
        o_ref[...] = (sa * a_ref[...] + sb * b_ref[...]).astype(BF16)

    row = lambda w: pl.BlockSpec((tm, w), lambda i: (i, 0))
    return pl.pallas_call(
        body, name="merge_fwd", out_shape=jax.ShapeDtypeStruct((L, D), BF16), grid=(L // tm,),
        in_specs=[row(D), row(D), row(2 * D)], out_specs=row(D), compiler_params=_params("parallel"),
    )(a, bm, gates)


def _merge_bwd(dm, a, bm, gates):
    L, D = a.shape
    tm = _tile(L, (128,))

    def body(dm_ref, a_ref, b_ref, g_ref, da_ref, db_ref, dg_ref):
        d = dm_ref[...]
        sa, sb = _sigmoid(g_ref[:, :D].astype(F32)), _sigmoid(g_ref[:, D:].astype(F32))
        da_ref[...] = (d * sa).astype(BF16)
        db_ref[...] = (d * sb).astype(BF16)
        dg_ref[:, :D] = (d * a_ref[...] * sa * (1.0 - sa)).astype(BF16)
        dg_ref[:, D:] = (d * b_ref[...] * sb * (1.0 - sb)).astype(BF16)

    row = lambda w: pl.BlockSpec((tm, w), lambda i: (i, 0))
    return pl.pallas_call(
        body, name="merge_bwd",
        out_shape=(jax.ShapeDtypeStruct((L, D), BF16), jax.ShapeDtypeStruct((L, D), BF16),
                   jax.ShapeDtypeStruct((L, 2 * D), BF16)),
        grid=(L // tm,),
        in_specs=[row(D), row(D), row(D), row(2 * D)], out_specs=(row(D), row(D), row(2 * D)),
        compiler_params=_params("parallel"),
    )(dm, a, bm, gates)


def _conv_taps(g_ref, halo_ref, i, tm):
    rows = i * tm + lax.broadcasted_iota(jnp.int32, (tm, 1), 0)
    g0 = jnp.where(rows >= PAD_LEN, g_ref[...].astype(F32), 0.0)
    sub = lax.broadcasted_iota(jnp.int32, (HALO, 1), 0)
    hrow = i * tm - HALO + sub
    halo = jnp.where(hrow >= PAD_LEN, halo_ref[...].astype(F32), 0.0)
    r = lax.broadcasted_iota(jnp.int32, (tm, 1), 0)
    h7 = jnp.sum(jnp.where(sub == HALO - 1, halo, 0.0), axis=0, keepdims=True)
    h6 = jnp.sum(jnp.where(sub == HALO - 2, halo, 0.0), axis=0, keepdims=True)
    g1 = jnp.where(r == 0, h7, pltpu.roll(g0, 1, 0))
    g2 = jnp.where(r == 0, h6, jnp.where(r == 1, h7, pltpu.roll(g0, 2, 0)))
    return g0, g1, g2


def _conv_fwd(gu, cw, cb):
    L, F2 = gu.shape
    F = F2 // 2
    tm = _tile(L, ROW_TILES)
    tn = _tile(F, (512, 256, 128))
    nj = F // tn

    def body(g_ref, halo_ref, u_ref, cw_ref, cb_ref, o_ref):
        g0, g1, g2 = _conv_taps(g_ref, halo_ref, pl.program_id(0), tm)
        conv = cw_ref[0:1, :] * g2 + cw_ref[1:2, :] * g1 + cw_ref[2:3, :] * g0 + cb_ref[...]
        o_ref[...] = (conv * _sigmoid(conv) * u_ref[...].astype(F32)).astype(BF16)

    return pl.pallas_call(
        body, name="conv_fwd", out_shape=jax.ShapeDtypeStruct((L, F), BF16), grid=(L // tm, nj),
        in_specs=[pl.BlockSpec((tm, tn), lambda i, j: (i, j)),
                  pl.BlockSpec((HALO, tn), lambda i, j: (jnp.maximum(i * (tm // HALO) - 1, 0), j)),
                  pl.BlockSpec((tm, tn), lambda i, j: (i, j + nj)),
                  pl.BlockSpec((3, tn), lambda i, j: (0, j)),
                  pl.BlockSpec((1, tn), lambda i, j: (0, j))],
        out_specs=pl.BlockSpec((tm, tn), lambda i, j: (i, j)),
        compiler_params=_params("parallel", "parallel"),
    )(gu, gu, gu, cw, cb)


def _conv_bwd_a(da, gu, cw, cb):
    L, F2 = gu.shape
    F = F2 // 2
    tm = _tile(L, ROW_TILES)
    tn = _tile(F, (512, 256, 128))
    nj = F // tn

    def body(da_ref, g_ref, halo_ref, u_ref, cw_ref, cb_ref, dc_ref, du_ref, dcb_ref, dcw_ref):
        i = pl.program_id(1)

        @pl.when(i == 0)
        def _():
            dcb_ref[...] = jnp.zeros_like(dcb_ref)
            dcw_ref[...] = jnp.zeros_like(dcw_ref)

        g0, g1, g2 = _conv_taps(g_ref, halo_ref, i, tm)
        conv = cw_ref[0:1, :] * g2 + cw_ref[1:2, :] * g1 + cw_ref[2:3, :] * g0 + cb_ref[...]
        sc = _sigmoid(conv)
        dav = da_ref[...]
        du_ref[...] = (dav * (conv * sc)).astype(BF16)
        dconv = dav * u_ref[...].astype(F32) * (sc * (1.0 + conv * (1.0 - sc)))
        dc_ref[...] = dconv
        dcb_ref[...] += jnp.sum(dconv, axis=0, keepdims=True)
        dcw_ref[0:1, :] += jnp.sum(dconv * g2, axis=0, keepdims=True)
        dcw_ref[1:2, :] += jnp.sum(dconv * g1, axis=0, keepdims=True)
        dcw_ref[2:3, :] += jnp.sum(dconv * g0, axis=0, keepdims=True)

    return pl.pallas_call(
        body, name="conv_bwd_a",
        out_shape=(jax.ShapeDtypeStruct((L, F), F32), jax.ShapeDtypeStruct((L, 2 * F), BF16),
                   jax.ShapeDtypeStruct((1, F), F32), jax.ShapeDtypeStruct((8, F), F32)),
        grid=(nj, L // tm),
        in_specs=[pl.BlockSpec((tm, tn), lambda j, i: (i, j)),
                  pl.BlockSpec((tm, tn), lambda j, i: (i, j)),
                  pl.BlockSpec((HALO, tn), lambda j, i: (jnp.maximum(i * (tm // HALO) - 1, 0), j)),
                  pl.BlockSpec((tm, tn), lambda j, i: (i, j + nj)),
                  pl.BlockSpec((3, tn), lambda j, i: (0, j)),
                  pl.BlockSpec((1, tn), lambda j, i: (0, j))],
        out_specs=(pl.BlockSpec((tm, tn), lambda j, i: (i, j)), pl.BlockSpec((tm, tn), lambda j, i: (i, j + nj)),
                   pl.BlockSpec((1, tn), lambda j, i: (0, j)), pl.BlockSpec((8, tn), lambda j, i: (0, j))),
        compiler_params=_params("parallel", "arbitrary"),
    )(da, gu, gu, gu, cw, cb)


def _conv_bwd_b(dconv, cw, dgu):
    L, F = dconv.shape
    tm = _tile(L, ROW_TILES)
    tn = _tile(F, (512, 256, 128))
    nblk8 = L // 8
    ni = L // tm

    def body(dc_ref, nxt_ref, cw_ref, dgu_ref, o_ref):
        i = pl.program_id(0)
        dc = dc_ref[...]
        nxt = jnp.where(i < ni - 1, nxt_ref[...], 0.0)
        sub = lax.broadcasted_iota(jnp.int32, (8, 1), 0)
        n0 = jnp.sum(jnp.where(sub == 0, nxt, 0.0), axis=0, keepdims=True)
        n1 = jnp.sum(jnp.where(sub == 1, nxt, 0.0), axis=0, keepdims=True)
        r = lax.broadcasted_iota(jnp.int32, (tm, 1), 0)
        d1 = jnp.where(r == tm - 1, n0, pltpu.roll(dc, tm - 1, 0))
        d2 = jnp.where(r == tm - 2, n0, jnp.where(r == tm - 1, n1, pltpu.roll(dc, tm - 2, 0)))
        dg = cw_ref[2:3, :] * dc + cw_ref[1:2, :] * d1 + cw_ref[0:1, :] * d2
        rows = i * tm + r
        o_ref[...] = jnp.where(rows >= PAD_LEN, dg, 0.0).astype(BF16)

    return pl.pallas_call(
        body, name="conv_bwd_b", out_shape=jax.ShapeDtypeStruct(dgu.shape, BF16), grid=(ni, F // tn),
        in_specs=[pl.BlockSpec((tm, tn), lambda i, j: (i, j)),
                  pl.BlockSpec((8, tn), lambda i, j: (jnp.minimum((i + 1) * (tm // 8), nblk8 - 1), j)),
                  pl.BlockSpec((3, tn), lambda i, j: (0, j)), ANY],
        out_specs=pl.BlockSpec((tm, tn), lambda i, j: (i, j)),
        input_output_aliases={3: 0},
        compiler_params=_params("parallel", "parallel"),
    )(dconv, dconv, cw, dgu)


ANY = pl.BlockSpec(memory_space=pl.ANY)


def _coords():
    return lax.axis_index("x"), lax.axis_index("y"), lax.axis_index("c")


def _flip(v, bit):
    return 1 - v if bit else v


CHIPS = [(1, 0), (0, 1), (1, 1)]
PEERS = [(dx, dy, dc) for dx in (0, 1) for dy in (0, 1) for dc in (0, 1)][1:]


class _Rider:
    def __init__(self, operands, out_shapes, sem_shapes, start, finish):
        self.operands, self.out_shapes, self.sem_shapes = list(operands), list(out_shapes), list(sem_shapes)
        self.start, self.finish = start, finish


def _run_rider(rider, name):
    n_in, n_out = len(rider.operands), len(rider.out_shapes)

    def body(*refs):
        ins, outs, sems = refs[:n_in], refs[n_in:n_in + n_out], refs[n_in + n_out:]
        rider.start(ins, outs, sems)
        rider.finish(ins, outs, sems)

    return pl.pallas_call(
        body, name=name, out_shape=tuple(rider.out_shapes),
        in_specs=[ANY] * n_in, out_specs=tuple([ANY] * n_out), scratch_shapes=rider.sem_shapes,
        compiler_params=pltpu.CompilerParams(has_side_effects=True),
    )(*rider.operands)


def _gather_rider(big, small):
    nbig, n = len(big), len(big) + len(small)
    arrays = list(big) + list(small)

    def plan(ins, outs, sems):
        ici_send, ici_recv, d2d_send, d2d_recv, local_sems = sems
        x, y, c = _coords()
        mine = 2 * x + y

        def half(w, h):
            r2 = arrays[w].shape[0] // 2
            return pl.ds(h * r2, r2)

        def ici(w, j, landing):
            px, py = _flip(x, CHIPS[j][0]), _flip(y, CHIPS[j][1])
            slot = 2 * px + py if landing else mine
            if w < nbig:
                src, dst = ins[w].at[half(w, c)], outs[w].at[slot, half(w, c)]
            else:
                src, dst = ins[w], outs[w].at[slot]
            return pltpu.make_async_remote_copy(
                src_ref=src, dst_ref=dst, send_sem=ici_send.at[w * 3 + j], recv_sem=ici_recv.at[w * 3 + j],
                device_id=(px, py, c), device_id_type=MESH)

        def d2d(w, j, landing):
            px, py = _flip(x, CHIPS[j][0]), _flip(y, CHIPS[j][1])
            mine_rows = outs[w].at[2 * px + py, half(w, c)]
            dst = outs[w].at[2 * px + py, half(w, 1 - c)] if landing else mine_rows
            return pltpu.make_async_remote_copy(
                src_ref=mine_rows, dst_ref=dst, send_sem=d2d_send.at[w * 3 + j], recv_sem=d2d_recv.at[w * 3 + j],
                device_id=(x, y, 1 - c), device_id_type=MESH)

        local = [pltpu.make_async_copy(ins[w], outs[w].at[mine], local_sems.at[w]) for w in range(n)]
        return ici, d2d, local

    def start(ins, outs, sems):
        ici, _, local = plan(ins, outs, sems)
        for cp in local:
            cp.start()
        for w in range(n):
            for j in range(3):
                ici(w, j, False).start()

    def finish(ins, outs, sems):
        ici, d2d, local = plan(ins, outs, sems)
        for w in range(n):
            for j in range(3):
                ici(w, j, True).wait_recv()
                if w < nbig:
                    d2d(w, j, False).start()
        for w in range(nbig):
            for j in range(3):
                d2d(w, j, True).wait_recv()
        for w in range(n):
            for j in range(3):
                ici(w, j, False).wait_send()
                if w < nbig:
                    d2d(w, j, False).wait_send()
        for cp in local:
            cp.wait()

    return _Rider(
        arrays, [jax.ShapeDtypeStruct((4,) + s.shape, s.dtype) for s in arrays],
        [pltpu.SemaphoreType.DMA((3 * n,)), pltpu.SemaphoreType.DMA((3 * n,)),
         pltpu.SemaphoreType.DMA((max(3 * nbig, 1),)), pltpu.SemaphoreType.DMA((max(3 * nbig, 1),)),
         pltpu.SemaphoreType.DMA((n,))],
        start, finish)


def _to_sibling(arrays, name):
    n = len(arrays)

    def body(*refs):
        ins, outs = refs[:n], refs[n:2 * n]
        send_sems, recv_sems = refs[2 * n:]
        x, y, c = _coords()

        def copy(w):
            return pltpu.make_async_remote_copy(
                src_ref=ins[w], dst_ref=outs[w], send_sem=send_sems.at[w], recv_sem=recv_sems.at[w],
                device_id=(x, y, 1 - c), device_id_type=MESH)

        for w in range(n):
            copy(w).start()
        for w in range(n):
            copy(w).wait_recv()
            copy(w).wait_send()

    return pl.pallas_call(
        body, name=name,
        out_shape=tuple(jax.ShapeDtypeStruct(a.shape, a.dtype) for a in arrays),
        in_specs=[ANY] * n, out_specs=tuple([ANY] * n),
        scratch_shapes=[pltpu.SemaphoreType.DMA((n,)), pltpu.SemaphoreType.DMA((n,))],
        compiler_params=pltpu.CompilerParams(has_side_effects=True),
    )(*arrays)


def _pair_sum(a, b, name):
    _, r, c = a.shape
    tr = _tile(r, (64, 32, 16))

    def body(a_ref, b_ref, o_ref):
        o_ref[...] = (a_ref[...].astype(F32) + b_ref[...].astype(F32)).astype(BF16)

    blk = pl.BlockSpec((4, tr, c), lambda i: (0, i, 0))
    return pl.pallas_call(
        body, name=name, out_shape=jax.ShapeDtypeStruct(a.shape, BF16), grid=(r // tr,),
        in_specs=[blk, blk], out_specs=blk, compiler_params=_params("parallel"),
    )(a, b)


def _scatter_rider(parts):
    n = len(parts)

    def plan(ins, outs, sems):
        send_sems, recv_sems, local_sems = sems
        x, y, c = _coords()
        mine = 2 * x + y

        def ici(w, j, landing):
            px, py = _flip(x, CHIPS[j][0]), _flip(y, CHIPS[j][1])
            return pltpu.make_async_remote_copy(
                src_ref=ins[w].at[2 * px + py], dst_ref=outs[w].at[2 * px + py if landing else mine],
                send_sem=send_sems.at[w * 3 + j], recv_sem=recv_sems.at[w * 3 + j],
                device_id=(px, py, c), device_id_type=MESH)

        local = [pltpu.make_async_copy(ins[w].at[mine], outs[w].at[mine], local_sems.at[w]) for w in range(n)]
        return ici, local

    def start(ins, outs, sems):
        ici, local = plan(ins, outs, sems)
        for cp in local:
            cp.start()
        for w in range(n):
            for j in range(3):
                ici(w, j, False).start()

    def finish(ins, outs, sems):
        ici, local = plan(ins, outs, sems)
        for w in range(n):
            for j in range(3):
                ici(w, j, True).wait_recv()
                ici(w, j, False).wait_send()
        for cp in local:
            cp.wait()

    return _Rider(
        parts, [jax.ShapeDtypeStruct(p.shape, p.dtype) for p in parts],
        [pltpu.SemaphoreType.DMA((3 * n,)), pltpu.SemaphoreType.DMA((3 * n,)), pltpu.SemaphoreType.DMA((n,))],
        start, finish)


def _sum4(recv, name):
    _, r, c = recv.shape
    tr = _tile(r, (64, 32, 16))

    def body(in_ref, o_ref):
        o_ref[...] = ((in_ref[0].astype(F32) + in_ref[1].astype(F32)) + in_ref[2].astype(F32)) + in_ref[3].astype(F32)

    return pl.pallas_call(
        body, name=name, out_shape=jax.ShapeDtypeStruct((r, c), F32), grid=(r // tr,),
        in_specs=[pl.BlockSpec((4, tr, c), lambda i: (0, i, 0))],
        out_specs=pl.BlockSpec((tr, c), lambda i: (i, 0)),
        compiler_params=_params("parallel"),
    )(recv)


def _allreduce_small(packed):
    R = packed.shape[0]

    def body(in_ref, o_ref, buf, send_sems, recv_sems):
        x, y, c = _coords()
        me = 4 * x + 2 * y + c
        buf[me] = in_ref[...]
        for j, (dx, dy, dc) in enumerate(PEERS):
            px, py, pc = _flip(x, dx), _flip(y, dy), _flip(c, dc)
            pltpu.make_async_remote_copy(
                src_ref=in_ref, dst_ref=buf.at[me], send_sem=send_sems.at[j], recv_sem=recv_sems.at[j],
                device_id=(px, py, pc), device_id_type=MESH).start()
        for j, (dx, dy, dc) in enumerate(PEERS):
            px, py, pc = _flip(x, dx), _flip(y, dy), _flip(c, dc)
            rc = pltpu.make_async_remote_copy(
                src_ref=in_ref, dst_ref=buf.at[4 * px + 2 * py + pc], send_sem=send_sems.at[j],
                recv_sem=recv_sems.at[j], device_id=(px, py, pc), device_id_type=MESH)
            rc.wait_recv()
            rc.wait_send()
        acc = buf[0]
        for s in range(1, 8):
            acc = acc + buf[s]
        o_ref[...] = acc

    return pl.pallas_call(
        body, name="allreduce_small", out_shape=jax.ShapeDtypeStruct((R, 128), F32),
        in_specs=[pl.BlockSpec(memory_space=pltpu.VMEM)], out_specs=pl.BlockSpec(memory_space=pltpu.VMEM),
        scratch_shapes=[pltpu.VMEM((8, R, 128), F32), pltpu.SemaphoreType.DMA((7,)), pltpu.SemaphoreType.DMA((7,))],
        compiler_params=pltpu.CompilerParams(has_side_effects=True, vmem_limit_bytes=VMEM_LIMIT_BYTES),
    )(packed)


def _adamw_math(w, g, m, v):
    m = ADAM_B1 * m + (1.0 - ADAM_B1) * g
    v = ADAM_B2 * v + (1.0 - ADAM_B2) * (g * g)
    m_hat = m / (1.0 - ADAM_B1 ** ADAM_STEP)
    v_hat = v / (1.0 - ADAM_B2 ** ADAM_STEP)
    delta = -ADAM_LR * (m_hat / (jnp.sqrt(v_hat) + ADAM_EPS) + ADAM_WD * w)
    return delta, m, v


def _adamw_big(mine, other, w, m, v, name):
    R, C = w.shape
    tr = _tile(R // 2, (128, 64, 32, 16, 8))
    nb = (R // 2) // tr

    def body(mine_ref, other_ref, w_ref, m_ref, v_ref, g_ref, d_ref, mo_ref, vo_ref):
        is_mine = (pl.program_id(0) // nb) == lax.axis_index("c")
        g = jnp.where(is_mine, mine_ref[...], other_ref[...])
        d, mn, vn = _adamw_math(w_ref[...], g, m_ref[...], v_ref[...])
        g_ref[...] = g
        d_ref[...] = d
        mo_ref[...] = mn
        vo_ref[...] = vn

    blk = pl.BlockSpec((tr, C), lambda i: (i, 0))
    half = pl.BlockSpec((tr, C), lambda i: (i % nb, 0))
    sds = jax.ShapeDtypeStruct((R, C), F32)
    return pl.pallas_call(
        body, name=name, out_shape=(sds, sds, sds, sds), grid=(2 * nb,),
        in_specs=[half, half, blk, blk, blk], out_specs=(blk, blk, blk, blk), compiler_params=_params("parallel"),
    )(mine, other, w, m, v)


def _adamw_small(items, lb_raw, dlb):
    n = len(items)
    lb_w, lb_m, lb_v = lb_raw

    def body(*refs):
        ins = refs[:4 * n]
        dlb_ref, lw_ref, lm_ref, lv_ref = refs[4 * n:4 * n + 4]
        outs = refs[4 * n + 4:]
        for t in range(n):
            g_ref, w_ref, m_ref, v_ref = ins[4 * t:4 * t + 4]
            d, mn, vn = _adamw_math(w_ref[...], g_ref[...], m_ref[...], v_ref[...])
            outs[3 * t][...] = d
            outs[3 * t + 1][...] = mn
            outs[3 * t + 2][...] = vn
        p0 = 1.0 / (1.0 + jnp.exp(lw_ref[1:2, :] - lw_ref[0:1, :]))
        g0 = dlb_ref[...] * p0 * (1.0 - p0)
        base = 3 * n
        outs[base][0:1, :] = g0
        outs[base][1:2, :] = -g0
        d, mn, vn = _adamw_math(lw_ref[...], outs[base][...], lm_ref[...], lv_ref[...])
        outs[base + 1][...] = d
        outs[base + 2][...] = mn
        outs[base + 3][...] = vn

    operands = [a for it in items for a in it] + [dlb, lb_w, lb_m, lb_v]
    out_shape = []
    for (g, w, m, v) in items:
        out_shape += [jax.ShapeDtypeStruct(w.shape, F32)] * 3
    out_shape += [jax.ShapeDtypeStruct(lb_w.shape, F32)] * 4
    vm = pl.BlockSpec(memory_space=pltpu.VMEM)
    res = pl.pallas_call(
        body, name="adamw_small", out_shape=tuple(out_shape),
        in_specs=[vm] * len(operands), out_specs=tuple([vm] * len(out_shape)),
        compiler_params=pltpu.CompilerParams(vmem_limit_bytes=VMEM_LIMIT_BYTES),
    )(*operands)
    deltas = [res[3 * t] for t in range(n)] + [res[3 * n + 1]]
    new_m = [res[3 * t + 1] for t in range(n)] + [res[3 * n + 2]]
    new_v = [res[3 * t + 2] for t in range(n)] + [res[3 * n + 3]]
    return res[3 * n], deltas, new_m, new_v


def _shard_row_half(g, by_cols, h):
    if isinstance(g, (list, tuple)):
        R, widths = g[0].shape[0], [s.shape[1] for s in g]
        starts = [sum(widths[:n]) for n in range(len(g))]
        cs = sum(widths) // 4
        rows = [lax.dynamic_slice_in_dim(s, h * (R // 2), R // 2, axis=0) for s in g]
        shards = []
        for k in range(4):
            lo, hi = k * cs, (k + 1) * cs
            cut = [r[:, max(lo - o, 0):min(hi - o, w)] for r, o, w in zip(rows, starts, widths) if lo < o + w and hi > o]
            shards.append(jnp.concatenate(cut, axis=1).astype(BF16))
        return jnp.stack(shards)
    if g.ndim == 3:
        return lax.dynamic_slice_in_dim(g, h * (g.shape[1] // 2), g.shape[1] // 2, axis=1).astype(BF16)
    R, C = g.shape
    if by_cols:
        part = lax.dynamic_index_in_dim(g.reshape(2, R // 2, 4, C // 4), h, axis=0, keepdims=False)
        return part.transpose(1, 0, 2).astype(BF16)
    return lax.dynamic_index_in_dim(g.reshape(4, 2, R // 8, C), h, axis=1, keepdims=False).astype(BF16)


def kernel(x, positions, meta_tokens, w_in, w_q_up, w_kv_up, w_branch_mla, w_branch_hgrn, w_out, w_ffn_in, w_ffn_out, conv_w, conv_b, g_mix_norm, g_q_norm, g_kv_norm, g_hgrn_norm, g_ffn_norm, g_final_norm, lb_raw, loss_target, m_meta_tokens, m_w_in, m_w_q_up, m_w_kv_up, m_w_branch_mla, m_w_branch_hgrn, m_w_out, m_w_ffn_in, m_w_ffn_out, m_conv_w, m_conv_b, m_g_mix_norm, m_g_q_norm, m_g_kv_norm, m_g_hgrn_norm, m_g_ffn_norm, m_g_final_norm, m_lb_raw, v_meta_tokens, v_w_in, v_w_q_up, v_w_kv_up, v_w_branch_mla, v_w_branch_hgrn, v_w_out, v_w_ffn_in, v_w_ffn_out, v_conv_w, v_conv_b, v_g_mix_norm, v_g_q_norm, v_g_kv_norm, v_g_hgrn_norm, v_g_ffn_norm, v_g_final_norm, v_lb_raw):
    S, D = x.shape[1], x.shape[2]
    L = S + PREFIX
    QL, KVL = g_q_norm.shape[1], g_kv_norm.shape[1]
    F = conv_b.shape[1]
    heads = (4 * w_kv_up.shape[2]) // QPAD
    nh = D // HEAD
    assert lb_raw.shape[0] == 2 and g_hgrn_norm.shape[1] == HEAD and L % CHUNK == 0
    ix, iy, ic = _coords()
    chip = 2 * ix + iy

    big = [w_in, w_q_up, w_kv_up, w_branch_mla, w_branch_hgrn, w_out, w_ffn_in, w_ffn_out]
    col_sharded = [True, True, True, False, False, False, True, False]
    shards = [w[0].astype(BF16) for w in big]
    early = _run_rider(_gather_rider(shards[:1], [meta_tokens, conv_w[0]]), "gather_early")
    qkv_rider = _gather_rider(shards[1:3], [])
    late_rider = _gather_rider(shards[3:], [])

    def full(gw, by_cols):
        _, r, c = gw.shape
        return gw.transpose(1, 0, 2).reshape(r, 4 * c) if by_cols else gw.reshape(4 * r, c)

    def col_range(g4, lo, hi):
        c = g4.shape[2]
        pieces = [g4[k][:, max(lo - k * c, 0):min(hi - k * c, c)] for k in range(4) if lo < (k + 1) * c and hi > k * c]
        return pieces[0] if len(pieces) == 1 else jnp.concatenate(pieces, axis=1)

    meta_full = full(early[1], True)
    cw_full = full(early[2], True)
    c0 = QL + KVL
    W_lat = col_range(early[0], 0, c0)
    W_kr = jnp.pad(col_range(early[0], c0, c0 + ROPE), ((0, 0), (0, 128 - ROPE)))
    W_H = col_range(early[0], c0 + ROPE, c0 + ROPE + 4 * D)
    W_G = col_range(early[0], c0 + ROPE + 4 * D, 4 * early[0].shape[2])

    pos = jnp.concatenate([jnp.zeros((PAD_LEN,), jnp.int32), jnp.arange(N_META, dtype=jnp.int32),
                           positions[0].astype(jnp.int32) + N_META])
    inv = 1.0 / (ROPE_THETA ** (jnp.arange(0, ROPE, 2, dtype=F32) / ROPE))
    ang = pos.astype(F32)[:, None] * inv
    zero = jnp.zeros((L, 128 - ROPE), F32)
    cos = jnp.concatenate([jnp.cos(ang), jnp.cos(ang), zero], axis=1)
    sin = jnp.concatenate([jnp.sin(ang), jnp.sin(ang), zero], axis=1)

    h0, u1 = _embed_norm(x[0], meta_full, g_mix_norm)
    hp, qkv = _mm(u1, W_H, "nn", BF16, "proj_hgrn", rider=qkv_rider)
    W_q, W_kv = [full(gw, True) for gw in qkv]
    W_qp = jnp.pad(W_q.reshape(QL, heads, QK_HEAD), ((0, 0), (0, 0), (0, QPAD - QK_HEAD))).reshape(QL, heads * QPAD)
    lat = _mm(u1, W_lat, "nn", BF16, "proj_lat")
    gates = _mm(u1, W_G, "nn", BF16, "proj_gates")
    kr = _mm(u1, W_kr, "nn", F32, "proj_krope")
    qn = _rmsnorm_fwd(lat, g_q_norm, "norm_q", col0=0, width=QL)
    kvn = _rmsnorm_fwd(lat, g_kv_norm, "norm_kv", col0=QL, width=KVL)
    qp = _mm(qn, W_qp, "nn", BF16, "q_up")
    kv = _mm(kvn, W_kv, "nn", BF16, "kv_up")
    qc, kc, vv = _rope_fwd(qp, kv, kr, cos, sin, heads)
    o_mla, lse, late = _attn_fwd(qc, kc, vv, heads, late_rider)
    W_a, W_b, W_o, W_fi, W_fo = [full(gw, bc) for gw, bc in zip(late, col_sharded[3:])]
    o_hgrn, o_rec, s_hist = _hgrn_fwd(hp, lb_raw, g_hgrn_norm, nh)
    br_a = _mm(o_mla, W_a, "nn", BF16, "branch_mla")
    br_b = _mm(o_hgrn, W_b, "nn", BF16, "branch_hgrn")
    merged = _merge_fwd(br_a, br_b, gates)
    h1 = _mm(merged, W_o, "nn", F32, "out_proj", res=h0)
    u2 = _rmsnorm_fwd(h1, g_ffn_norm, "norm_ffn")
    gu = _mm(u2, W_fi, "nn", BF16, "ffn_in")
    act = _conv_fwd(gu, cw_full, conv_b)
    h2 = _mm(act, W_fo, "nn", F32, "ffn_out", res=h1)
    dh2, loss_p, dg_final, dh2_b = _final_loss_bwd(h2, loss_target[0], g_final_norm.reshape(1, D))

    dact = _mm(dh2_b, W_fo, "nt", F32, "d_act")
    dW_fo = _mm(act, dh2_b, "tn", F32, "dw_ffn_out")
    dconv, dgu_right, dcb, dcw = _conv_bwd_a(dact, gu, cw_full, conv_b)
    dgu = _conv_bwd_b(dconv, cw_full, dgu_right)
    du2 = _mm(dgu, W_fi, "nt", F32, "d_u2")
    dW_fi = _mm(u2, dgu, "tn", F32, "dw_ffn_in", col_shards=4)
    dh1, dg_ffn = _rmsnorm_bwd(du2, h1, g_ffn_norm, "norm_ffn_bwd", F32, res=dh2)
    dmerged = _mm(dh1, W_o, "nt", F32, "d_merged")
    dW_o = _mm(merged, dh1, "tn", F32, "dw_out")
    d_a, d_b, d_gates = _merge_bwd(dmerged, br_a, br_b, gates)
    do_mla = _mm(d_a, W_a, "nt", BF16, "d_o_mla")
    dW_a = _mm(o_mla, d_a, "tn", F32, "dw_branch_mla")
    do_hgrn = _mm(d_b, W_b, "nt", F32, "d_o_hgrn")
    dW_b = _mm(o_hgrn, d_b, "tn", F32, "dw_branch_hgrn")
    names = ["w_in", "w_q_up", "w_kv_up", "w_branch_mla", "w_branch_hgrn", "w_out", "w_ffn_in", "w_ffn_out"]

    def chip_partials(grads, by_cols, nms, tag):
        keep = [_shard_row_half(g, bc, ic) for g, bc in zip(grads, by_cols)]
        give = [_shard_row_half(g, bc, 1 - ic) for g, bc in zip(grads, by_cols)]
        taken = _to_sibling(give, "pair_exchange_" + tag)
        return [_pair_sum(a, b, "pair_sum_" + nm) for a, b, nm in zip(keep, taken, nms)]

    late_parts = chip_partials([dW_a, dW_b, dW_o, dW_fi, dW_fo], col_sharded[3:], names[3:], "late")
    dhq, dhf, dhi, dhg, dgn_p, dlb_p = _hgrn_bwd(hp, lb_raw, g_hgrn_norm, do_hgrn, o_rec, s_hist, nh)
    dqc, dkc, dvv, late_recv = _attn_bwd(qc, kc, vv, do_mla, lse, _attn_delta(o_mla, do_mla, heads), heads,
                                         _scatter_rider(late_parts))
    dqp, dkv, dkr = _rope_bwd(dqc, dkc, dvv, cos, sin, heads)
    dqn = _mm(dqp, W_qp, "nt", F32, "d_qn")
    dW_qp = _mm(qn, dqp, "tn", F32, "dw_q_up")
    dkvn = _mm(dkv, W_kv, "nt", F32, "d_kvn")
    dW_kv = _mm(kvn, dkv, "tn", F32, "dw_kv_up")
    dq_lat, dg_q = _rmsnorm_bwd(dqn, lat, g_q_norm, "norm_q_bwd", BF16, col0=0)
    dkv_lat, dg_kv = _rmsnorm_bwd(dkvn, lat, g_kv_norm, "norm_kv_bwd", BF16, col0=QL)
    dlat = jnp.concatenate([dq_lat, dkv_lat], axis=1)
    dhp = jnp.concatenate([dhq, dhf, dhi, dhg], axis=1)
    dW_lat = _mm(u1, dlat, "tn", F32, "dw_in_lat")
    dW_H = _mm(u1, dhp, "tn", F32, "dw_in_hgrn")
    dW_G = _mm(u1, d_gates, "tn", F32, "dw_in_gates")
    dW_kr = _mm(u1, dkr, "tn", F32, "dw_in_krope")
    dW_in = [dW_lat, dW_kr[:, :ROPE], dW_H, dW_G]
    dW_q = dW_qp.reshape(QL, heads, QPAD)[:, :, :QK_HEAD].reshape(QL, heads * QK_HEAD)
    early_parts = chip_partials([dW_in, dW_q, dW_kv], col_sharded[:3], names[:3], "early")
    du1 = _mm(dlat, W_lat, "nt", F32, "d_u1_lat")
    du1, early_recv = _mm(dhp, W_H, "nt", F32, "d_u1_hgrn", res=du1, rider=_scatter_rider(early_parts))
    du1 = _mm(d_gates, W_G, "nt", F32, "d_u1_gates", res=du1)
    du1 = _mm(dkr, W_kr, "nt", F32, "d_u1_krope", res=du1)
    (dx_tokens, dh0_prefix), dg_mix = _rmsnorm_bwd(du1, h0, g_mix_norm, "norm_mix_bwd", F32, res=dh1,
                                                   split_prefix=True)
    grad_x = dx_tokens[None]

    received = list(early_recv) + list(late_recv)
    halves = [_sum4(r, "sum_" + nm) for r, nm in zip(received, names)]
    others = _to_sibling(halves, "swap_halves")
    big_m = [m_w_in, m_w_q_up, m_w_kv_up, m_w_branch_mla, m_w_branch_hgrn, m_w_out, m_w_ffn_in, m_w_ffn_out]
    big_v = [v_w_in, v_w_q_up, v_w_kv_up, v_w_branch_mla, v_w_branch_hgrn, v_w_out, v_w_ffn_in, v_w_ffn_out]
    big_out = {}
    for nm, mine, other, w, m, v in zip(names, halves, others, big, big_m, big_v):
        g, d, mn, vn = _adamw_big(mine, other, w[0], m[0], v[0], "adamw_" + nm)
        big_out[nm] = (g[None], d[None], mn[None], vn[None])

    pieces = [loss_p[:, :1], dg_mix, dg_q, dg_kv, jnp.sum(dgn_p[:, 0, :], axis=0, keepdims=True), dg_ffn, dg_final,
              dlb_p[:, 0, :].reshape(1, D), dcb, dcw[0:3].reshape(1, 3 * F), dh0_prefix[PAD_LEN:PREFIX].reshape(1, N_META * D)]
    sizes = [p.shape[1] for p in pieces]
    flat = jnp.concatenate(pieces, axis=1)[0]
    rows = -(-flat.shape[0] // 1024) * 8
    packed = jnp.pad(flat, (0, rows * 128 - flat.shape[0])).reshape(rows, 128)
    total = _allreduce_small(packed).reshape(-1)
    offs = [0]
    for s in sizes:
        offs.append(offs[-1] + s)
    loss, g_mix, g_q, g_kv, g_hg, g_ffn, g_fin, dlb, g_cb, g_cw, g_meta = [
        total[offs[t]:offs[t + 1]].reshape(1, sizes[t]) for t in range(len(sizes))]
    g_cw = lax.dynamic_slice_in_dim(g_cw.reshape(3, F), chip * (F // 4), F // 4, axis=1)
    g_meta = lax.dynamic_slice_in_dim(g_meta.reshape(N_META, D), chip * (D // 4), D // 4, axis=1)
    items = [(g_meta, meta_tokens, m_meta_tokens, v_meta_tokens),
             (g_cw, conv_w[0], m_conv_w[0], v_conv_w[0]),
             (g_cb, conv_b, m_conv_b, v_conv_b),
             (g_mix, g_mix_norm, m_g_mix_norm, v_g_mix_norm),
             (g_q, g_q_norm, m_g_q_norm, v_g_q_norm),
             (g_kv, g_kv_norm, m_g_kv_norm, v_g_kv_norm),
             (g_hg, g_hgrn_norm, m_g_hgrn_norm, v_g_hgrn_norm),
             (g_ffn, g_ffn_norm, m_g_ffn_norm, v_g_ffn_norm),
             (g_fin, g_final_norm.reshape(1, D), m_g_final_norm.reshape(1, D), v_g_final_norm.reshape(1, D))]
    g_lb, s_delta, s_m, s_v = _adamw_small(items, (lb_raw, m_lb_raw, v_lb_raw), dlb)
    s_grads = [it[0] for it in items] + [g_lb]

    def shape_small(vals):
        meta, cw, cb, mix, q, kvg, hg, ffn, fin, lb = vals
        return [meta, cw[None], cb, mix, q, kvg, hg, ffn, fin.reshape(D), lb]

    s_grads, s_delta, s_m, s_v = [shape_small(v) for v in (s_grads, s_delta, s_m, s_v)]

    def ordered(kind, small):
        bigs = [big_out[nm][kind] for nm in names]
        return [small[0]] + bigs + small[1:]

    return (loss.reshape(()), grad_x, *ordered(0, s_grads), *ordered(1, s_delta), *ordered(2, s_m), *ordered(3, s_v))
```
